```python
import math
import jax
import jax.numpy as jnp
from jax import lax
import numpy as np

D_MODEL = 1024
BATCH = 8
SEQ = 2048
DEPTH = 1

N_META = 16
CHUNK = 64
META_PAD = (-N_META) % CHUNK
N_FGROUPS = 4
FGROUP_DIM = 128
F_WIDTH = N_FGROUPS * FGROUP_DIM
N_HEADS = 8
HEAD_DK = 128
HEAD_DV = 128
QK_WIDTH = N_HEADS * HEAD_DK
V_WIDTH = N_HEADS * HEAD_DV
QKV_WIDTH = 2 * QK_WIDTH + V_WIDTH
CONV_K = 5
SPLIT_SIZES = (F_WIDTH, QKV_WIDTH, V_WIDTH, 4 * N_HEADS, D_MODEL, D_MODEL)
IN_WIDTH = sum(SPLIT_SIZES)
N_GROUPS = 8
EXPERTS_PER_GROUP = 8
N_EXPERTS = N_GROUPS * EXPERTS_PER_GROUP
TOP_K = 2
D_EXPERT = 256
EXPERT_BLOCK = 128
NORM_EPS = 1e-6

kernel_name = "fnet_bigdn_hier_moe_hybrid"


def rms_norm(x, gain):
    xf = x.astype(jnp.float32)
    y = xf * lax.rsqrt(jnp.mean(xf * xf, axis=-1, keepdims=True) + NORM_EPS)
    return (y * gain.astype(jnp.float32)).astype(x.dtype)


def l2_normalize(x):
    return x * lax.rsqrt(jnp.sum(x * x, axis=-1, keepdims=True) + NORM_EPS)


def depthwise_conv_centred(x, w):
    k = w.shape[0]
    return lax.conv_general_dilated(
        x, w[:, None, :].astype(x.dtype), window_strides=(1,),
        padding=[((k - 1) // 2, k // 2)],
        dimension_numbers=("NWC", "WIO", "NWC"),
        feature_group_count=x.shape[-1])


def gated_delta_chunked(q, k, v, g, beta):
    b, h, lp, dk = k.shape
    dv = v.shape[-1]
    nc = lp // CHUNK
    q = q.reshape(b, h, nc, CHUNK, dk)
    k = k.reshape(b, h, nc, CHUNK, dk)
    v = v.reshape(b, h, nc, CHUNK, dv)
    beta = beta.reshape(b, h, nc, CHUNK)
    g = jnp.cumsum(g.reshape(b, h, nc, CHUNK), axis=-1)
    lower = jnp.tril(jnp.ones((CHUNK, CHUNK), dtype=bool))
    strict = jnp.tril(jnp.ones((CHUNK, CHUNK), dtype=bool), -1)
    diff = g[..., :, None] - g[..., None, :]
    decay = jnp.where(lower, jnp.exp(jnp.where(lower, diff, 0.0)), 0.0)
    k_beta = k * beta[..., None]
    a_mat = jnp.where(strict, jnp.einsum("bhncd,bhnsd->bhncs", k_beta, k) * decay, 0.0)
    eye = jnp.eye(CHUNK, dtype=a_mat.dtype)
    t_mat = lax.linalg.triangular_solve(
        eye + a_mat, jnp.broadcast_to(eye, a_mat.shape),
        left_side=True, lower=True, unit_diagonal=True)
    u = jnp.einsum("bhncs,bhnsv->bhncv", t_mat, v * beta[..., None])
    w = jnp.einsum("bhncs,bhnsd->bhncd", t_mat, k_beta * jnp.exp(g)[..., None])
    qk = jnp.where(lower, jnp.einsum("bhncd,bhnsd->bhncs", q, k) * decay, 0.0)
    q_dec = q * jnp.exp(g)[..., None]
    k_dec = k * jnp.exp(g[..., -1:] - g)[..., None]
    chunk_decay = jnp.exp(g[..., -1])

    def step(state, inp):
        q_c, k_c, u_c, w_c, qk_c, dec_c = inp
        v_new = u_c - jnp.einsum("bhcd,bhdv->bhcv", w_c, state)
        o_c = (jnp.einsum("bhcd,bhdv->bhcv", q_c, state)
               + jnp.einsum("bhcs,bhsv->bhcv", qk_c, v_new))
        state = state * dec_c[..., None, None] + jnp.einsum("bhcd,bhcv->bhdv", k_c, v_new)
        return state, o_c

    xs = tuple(jnp.moveaxis(t, 2, 0) for t in (q_dec, k_dec, u, w, qk, chunk_decay))
    state0 = jnp.zeros((b, h, dk, dv), jnp.float32)
    _, o = lax.scan(step, state0, xs)
    return jnp.moveaxis(o, 0, 2).reshape(b, h, lp, dv)


def pad_to_chunks(t):
    t = jnp.pad(t, [(0, 0), (META_PAD, 0)] + [(0, 0)] * (t.ndim - 2))
    return jnp.moveaxis(t, 1, 2)


def hybrid_mixer(h, norm_g, w_in, conv_w, a_log_f, dt_bias_f, a_log_b, dt_bias_b,
                 out_norm_g, w_fourier, w_delta, w_out):
    bsz, seq_len, _ = h.shape
    hn = rms_norm(h, norm_g)
    proj = hn @ w_in
    points = np.cumsum(SPLIT_SIZES)[:-1].tolist()
    f_in, qkv, z, ab, gate_f, gate_d = jnp.split(proj, points, axis=-1)

    fg = f_in.reshape(bsz, seq_len, N_FGROUPS, FGROUP_DIM).astype(jnp.float32)
    f_mix = jnp.real(jnp.fft.fft2(fg, axes=(1, 3), norm="ortho")).astype(h.dtype)
    y_f = f_mix.reshape(bsz, seq_len, F_WIDTH) @ w_fourier

    qkv = jax.nn.silu(depthwise_conv_centred(qkv, conv_w))
    q, k, v = jnp.split(qkv, [QK_WIDTH, 2 * QK_WIDTH], axis=-1)
    q = l2_normalize(q.reshape(bsz, seq_len, N_HEADS, HEAD_DK).astype(jnp.float32)) * (HEAD_DK ** -0.5)
    k = l2_normalize(k.reshape(bsz, seq_len, N_HEADS, HEAD_DK).astype(jnp.float32))
    v = v.reshape(bsz, seq_len, N_HEADS, HEAD_DV).astype(jnp.float32)
    a_f, b_f, a_b, b_b = jnp.split(ab.astype(jnp.float32), 4, axis=-1)
    g_f = -jnp.exp(a_log_f.astype(jnp.float32)) * jax.nn.softplus(a_f + dt_bias_f.astype(jnp.float32))
    g_b = -jnp.exp(a_log_b.astype(jnp.float32)) * jax.nn.softplus(a_b + dt_bias_b.astype(jnp.float32))
    beta_f = jax.nn.sigmoid(b_f)
    beta_b = jax.nn.sigmoid(b_b)

    qp, kp, vp = pad_to_chunks(q), pad_to_chunks(k), pad_to_chunks(v)
    o_fwd = gated_delta_chunked(qp, kp, vp, pad_to_chunks(g_f), pad_to_chunks(beta_f))
    flip = lambda t: jnp.flip(t, axis=2)
    o_bwd = flip(gated_delta_chunked(flip(qp), flip(kp), flip(vp),
                                     flip(pad_to_chunks(g_b)), flip(pad_to_chunks(beta_b))))
    o = jnp.moveaxis(o_fwd + o_bwd, 1, 2)[:, META_PAD:]
    zf = z.reshape(bsz, seq_len, N_HEADS, HEAD_DV).astype(jnp.float32)
    o = rms_norm(o, out_norm_g) * jax.nn.silu(zf)
    y_d = o.reshape(bsz, seq_len, V_WIDTH).astype(h.dtype) @ w_delta

    merged = jax.nn.sigmoid(gate_f) * y_f + jax.nn.sigmoid(gate_d) * y_d
    return merged @ w_out


def hierarchical_moe(h, norm_g, w_rg, b_rg, w_re, b_re, w_gate, w_up, w_down):
    bsz, seq_len, d = h.shape
    hn = rms_norm(h, norm_g).reshape(-1, d)
    n_tok = hn.shape[0]
    gl = (hn @ w_rg).astype(jnp.float32) + b_rg.astype(jnp.float32)
    gp = jax.nn.softmax(gl, axis=-1)
    p_group, g_idx = lax.top_k(gp, 1)
    el = (hn @ w_re).astype(jnp.float32) + b_re.astype(jnp.float32)
    el = el.reshape(n_tok, N_GROUPS, EXPERTS_PER_GROUP)
    el_g = jnp.take_along_axis(el, g_idx[:, :, None], axis=1)[:, 0]
    ep = jax.nn.softmax(el_g, axis=-1)
    top_w, top_i = lax.top_k(ep, TOP_K)
    top_w = top_w / jnp.sum(top_w, axis=-1, keepdims=True)
    weights = p_group * top_w
    expert_idx = g_idx * EXPERTS_PER_GROUP + top_i

    n_assign = n_tok * TOP_K
    n_blocks = -(-n_assign // EXPERT_BLOCK) + N_EXPERTS
    n_slots = n_blocks * EXPERT_BLOCK
    flat_e = expert_idx.reshape(-1).astype(jnp.int32)
    flat_tok = jnp.repeat(jnp.arange(n_tok, dtype=jnp.int32), TOP_K)
    flat_w = weights.reshape(-1)
    order = jnp.argsort(flat_e, stable=True)
    e_sorted = flat_e[order]
    counts = jnp.bincount(flat_e, length=N_EXPERTS)
    padded_counts = (counts + EXPERT_BLOCK - 1) // EXPERT_BLOCK * EXPERT_BLOCK
    padded_ends = jnp.cumsum(padded_counts)
    padded_start = padded_ends - padded_counts
    start = jnp.cumsum(counts) - counts
    rank = jnp.arange(n_assign, dtype=jnp.int32) - start[e_sorted]
    dest = padded_start[e_sorted] + rank
    slot_tok = jnp.zeros((n_slots,), jnp.int32).at[dest].set(flat_tok[order])
    slot_w = jnp.zeros((n_slots,), hn.dtype).at[dest].set(flat_w[order].astype(hn.dtype))
    block_start = jnp.arange(n_blocks, dtype=jnp.int32) * EXPERT_BLOCK
    block_expert = jnp.minimum(jnp.searchsorted(padded_ends, block_start, side="right"),
                               N_EXPERTS - 1).astype(jnp.int32)
    xb = hn[slot_tok].reshape(n_blocks, EXPERT_BLOCK, d)

    def expert_block(args):
        x_blk, e = args
        return (jax.nn.silu(x_blk @ w_gate[e]) * (x_blk @ w_up[e])) @ w_down[e]

    yb = lax.map(expert_block, (xb, block_expert)).reshape(n_slots, d)
    out = jnp.zeros((n_tok, d), hn.dtype).at[slot_tok].add(yb * slot_w[:, None])
    return out.reshape(bsz, seq_len, d)


def setup_inputs(seed: int = 0) -> dict:
    key = jax.random.key(seed)
    ks = jax.random.split(key, 24)
    nrm = lambda k, shape, scale: jax.random.normal(k, shape, jnp.float32) * scale
    gain = lambda k, shape: 1.0 + 0.05 * jax.random.normal(k, shape, jnp.float32)

    def a_log(k):
        return jnp.log(jax.random.uniform(k, (DEPTH, N_HEADS), jnp.float32, minval=1.0, maxval=16.0))

    def dt_bias(k):
        dt = jnp.exp(jax.random.uniform(k, (DEPTH, N_HEADS), jnp.float32,
                                        minval=math.log(1e-3), maxval=math.log(1e-1)))
        return dt + jnp.log(-jnp.expm1(-dt))

    return {
        "x": nrm(ks[0], (BATCH, SEQ, D_MODEL), 1.0),
        "meta_tokens": nrm(ks[1], (N_META, D_MODEL), 1.0),
        "norm1_g": gain(ks[2], (DEPTH, D_MODEL)),
        "w_in": nrm(ks[3], (DEPTH, D_MODEL, IN_WIDTH), D_MODEL ** -0.5),
        "conv_w": nrm(ks[4], (DEPTH, CONV_K, QKV_WIDTH), CONV_K ** -0.5),
        "a_log_fwd": a_log(ks[5]),
        "dt_bias_fwd": dt_bias(ks[6]),
        "a_log_bwd": a_log(ks[7]),
        "dt_bias_bwd": dt_bias(ks[8]),
        "out_norm_g": gain(ks[9], (DEPTH, HEAD_DV)),
        "w_fourier": nrm(ks[10], (DEPTH, F_WIDTH, D_MODEL), F_WIDTH ** -0.5),
        "w_delta": nrm(ks[11], (DEPTH, V_WIDTH, D_MODEL), V_WIDTH ** -0.5),
        "w_out": nrm(ks[12], (DEPTH, D_MODEL, D_MODEL), D_MODEL ** -0.5),
        "norm2_g": gain(ks[13], (DEPTH, D_MODEL)),
        "w_router_group": nrm(ks[14], (DEPTH, D_MODEL, N_GROUPS), D_MODEL ** -0.5),
        "b_router_group": nrm(ks[15], (DEPTH, N_GROUPS), 0.01),
        "w_router_expert": nrm(ks[16], (DEPTH, D_MODEL, N_EXPERTS), D_MODEL ** -0.5),
        "b_router_expert": nrm(ks[17], (DEPTH, N_EXPERTS), 0.01),
        "w_gate_e": nrm(ks[18], (DEPTH, N_EXPERTS, D_MODEL, D_EXPERT), D_MODEL ** -0.5),
        "w_up_e": nrm(ks[19], (DEPTH, N_EXPERTS, D_MODEL, D_EXPERT), D_MODEL ** -0.5),
        "w_down_e": nrm(ks[20], (DEPTH, N_EXPERTS, D_EXPERT, D_MODEL), D_EXPERT ** -0.5),
        "final_norm_g": gain(ks[21], (D_MODEL,)),
    }


def reference(x, meta_tokens, norm1_g, w_in, conv_w, a_log_fwd, dt_bias_fwd, a_log_bwd,
              dt_bias_bwd, out_norm_g, w_fourier, w_delta, w_out, norm2_g, w_router_group,
              b_router_group, w_router_expert, b_router_expert, w_gate_e, w_up_e, w_down_e,
              final_norm_g):
    bsz = x.shape[0]
    meta = jnp.broadcast_to(meta_tokens[None].astype(x.dtype), (bsz, N_META, D_MODEL))
    h = jnp.concatenate([meta, x], axis=1)
    for layer in range(DEPTH):
        h = h + hybrid_mixer(h, norm1_g[layer], w_in[layer], conv_w[layer],
                             a_log_fwd[layer], dt_bias_fwd[layer], a_log_bwd[layer],
                             dt_bias_bwd[layer], out_norm_g[layer], w_fourier[layer],
                             w_delta[layer], w_out[layer])
        h = h + hierarchical_moe(h, norm2_g[layer], w_router_group[layer], b_router_group[layer],
                                 w_router_expert[layer], b_router_expert[layer],
                                 w_gate_e[layer], w_up_e[layer], w_down_e[layer])
    return rms_norm(h, final_norm_g)[:, N_META:]
```

```python
import functools
import math

import jax
import jax.numpy as jnp
from jax import lax
from jax.experimental import pallas as pl
from jax.experimental.pallas import tpu as pltpu

F32 = jnp.float32
MXU_DTYPE = jnp.bfloat16

N_META = 16
CHUNK = 64
N_FGROUPS = 4
FGROUP_DIM = 128
F_WIDTH = N_FGROUPS * FGROUP_DIM
N_HEADS = 8
HEAD_DIM = 128
QK_WIDTH = N_HEADS * HEAD_DIM
CONV_K = 5
N_GROUPS = 8
EXPERTS_PER_GROUP = 8
N_EXPERTS = N_GROUPS * EXPERTS_PER_GROUP
TOP_K = 2
EXPERT_BLOCK = 128
NORM_EPS = 1e-6
LANES = 128
SUBLANES = 8
VMEM_LIMIT = 52 * 1024 * 1024

COL_QKV = 0
COL_Z = 3 * QK_WIDTH
COL_GATE_F = COL_Z + QK_WIDTH
COL_GATE_D = COL_GATE_F + 1024
COL_FIN = COL_GATE_D + 1024
COL_AB = COL_FIN + F_WIDTH
PROJ_TN = 768
PROJ_WIDTH = -(-(COL_AB + LANES) // PROJ_TN) * PROJ_TN


def _dot(a, b):
    return jnp.dot(a, b, preferred_element_type=F32)


def _dot_nt(a, b):
    return lax.dot_general(a, b, (((1,), (1,)), ((), ())), preferred_element_type=F32)


def _dot_tn(a, b):
    return lax.dot_general(a, b, (((0,), (0,)), ((), ())), preferred_element_type=F32)


def _split3(x):
    hi = x.astype(MXU_DTYPE)
    r1 = x - hi.astype(F32)
    mid = r1.astype(MXU_DTYPE)
    lo = (r1 - mid.astype(F32)).astype(MXU_DTYPE)
    return hi, mid, lo


def _exact_left(sel, x):
    hi, mid, lo = _split3(x)
    return _dot(sel, hi) + _dot(sel, mid) + _dot(sel, lo)


def _exact_right(x, sel):
    hi, mid, lo = _split3(x)
    return _dot(hi, sel) + _dot(mid, sel) + _dot(lo, sel)


def _silu(x):
    return x * jax.nn.sigmoid(x)


def _params(sem):
    return pltpu.CompilerParams(dimension_semantics=sem, vmem_limit_bytes=VMEM_LIMIT)


def _inproj_kernel(h_ref, g_ref, w_ref, o_ref, hn_ref):
    @pl.when(pl.program_id(1) == 0)
    def _():
        x = h_ref[...]
        ms = jnp.mean(x * x, axis=-1, keepdims=True)
        hn_ref[...] = (x * lax.rsqrt(ms + NORM_EPS) * g_ref[...]).astype(MXU_DTYPE)

    o_ref[...] = _dot(hn_ref[...], w_ref[...])


def _in_proj(h2d, gain, w_cat, tm):
    t, d = h2d.shape
    n = w_cat.shape[1]
    return pl.pallas_call(
        _inproj_kernel,
        grid=(t // tm, n // PROJ_TN),
        in_specs=[
            pl.BlockSpec((tm, d), lambda i, j: (i, 0)),
            pl.BlockSpec((1, d), lambda i, j: (0, 0)),
            pl.BlockSpec((d, PROJ_TN), lambda i, j: (0, j)),
        ],
        out_specs=pl.BlockSpec((tm, PROJ_TN), lambda i, j: (i, j)),
        out_shape=jax.ShapeDtypeStruct((t, n), F32),
        scratch_shapes=[pltpu.VMEM((tm, d), MXU_DTYPE)],
        compiler_params=_params(("arbitrary", "arbitrary")),
        name="in_proj",
    )(h2d, gain, w_cat)


def _fnet_kernel(x_ref, cs_ref, tw_ref, o_ref, r_ref, *, seq, kp, scale):
    @pl.when(pl.program_id(1) == 0)
    def _():
        tail = kp - seq
        if tail:
            r_ref[seq:kp, :] = jnp.zeros((tail, F_WIDTH), MXU_DTYPE)
            r_ref[kp + seq:, :] = jnp.zeros((tail, F_WIDTH), MXU_DTYPE)
        cs = cs_ref[...]
        for g in range(N_FGROUPS):
            cols = slice(g * FGROUP_DIM, (g + 1) * FGROUP_DIM)
            p = _dot(x_ref[:, cols].astype(MXU_DTYPE), cs)
            r_ref[0:seq, cols] = p[:, :FGROUP_DIM].astype(MXU_DTYPE)
            r_ref[kp:kp + seq, cols] = p[:, FGROUP_DIM:].astype(MXU_DTYPE)

    o_ref[...] = (_dot(tw_ref[...], r_ref[...]) * scale).astype(o_ref.dtype)


def _fnet(proj3, cs128, twid, tmf):
    b, seq, _ = proj3.shape
    kp = twid.shape[1] // 2
    scale = 1.0 / math.sqrt(seq * FGROUP_DIM)
    return pl.pallas_call(
        functools.partial(_fnet_kernel, seq=seq, kp=kp, scale=scale),
        grid=(b, seq // tmf),
        in_specs=[
            pl.BlockSpec((None, seq, F_WIDTH), lambda bi, i: (bi, 0, COL_FIN // F_WIDTH)),
            pl.BlockSpec((FGROUP_DIM, 2 * FGROUP_DIM), lambda bi, i: (0, 0)),
            pl.BlockSpec((tmf, 2 * kp), lambda bi, i: (i, 0)),
        ],
        out_specs=pl.BlockSpec((None, tmf, F_WIDTH), lambda bi, i: (bi, i, 0)),
        out_shape=jax.ShapeDtypeStruct((b, seq, F_WIDTH), MXU_DTYPE),
        scratch_shapes=[pltpu.VMEM((2 * kp, F_WIDTH), MXU_DTYPE)],
        compiler_params=_params(("arbitrary", "arbitrary")),
        name="fnet",
    )(proj3, cs128, twid)


def _neumann_inverse(a):
    n = a.shape[0]
    ri = lax.broadcasted_iota(jnp.int32, (n, n), 0)
    ci = lax.broadcasted_iota(jnp.int32, (n, n), 1)
    t = jnp.where(ri == ci, 1.0, 0.0).astype(F32) - a
    a16 = a.astype(MXU_DTYPE)
    p = _dot(a16, a16)
    steps = int(math.log2(n)) - 1
    for s in range(steps):
        p16 = p.astype(MXU_DTYPE)
        if s < steps - 1:
            tp = _dot(jnp.concatenate([t, p], axis=0).astype(MXU_DTYPE), p16)
            t = t + tp[:n]
            p = tp[n:]
        else:
            t = t + _dot(t.astype(MXU_DTYPE), p16)
    return t


def _gdn_kernel(q_ref, k_ref, v_ref, z_ref, ab_ref, cwq_ref, cwk_ref, cwv_ref, prm_ref, ong_ref,
                o_ref,
                raw_ref, qs_ref, ks_ref, vs_ref, gsel_ref, qe_ref, mp_ref, nn_ref, dd_ref, osum_ref,
                *, seq, lp):
    head = pl.program_id(1)
    nc = lp // CHUNK
    padr = lp - seq
    halo = SUBLANES
    row64 = lax.broadcasted_iota(jnp.int32, (CHUNK, LANES), 0)
    lane64 = lax.broadcasted_iota(jnp.int32, (CHUNK, LANES), 1)

    def conv_into(src_ref, cw_ref, dst_ref, l2_scale):
        raw_ref[0:padr + halo, :] = jnp.zeros((padr + halo, LANES), F32)
        raw_ref[padr + halo:padr + halo + seq, :] = src_ref[...]
        raw_ref[lp + halo:lp + 2 * halo, :] = jnp.zeros((halo, LANES), F32)
        cw = cw_ref[...]

        def body(c, carry):
            base = pl.multiple_of(c * CHUNK, CHUNK)
            x = raw_ref[pl.ds(base, CHUNK + 2 * halo), :]
            acc = jnp.zeros((CHUNK, LANES), F32)
            for j in range(CONV_K):
                off = halo - (CONV_K - 1) // 2 + j
                acc = acc + cw[j:j + 1, :] * x[off:off + CHUNK, :]
            y = _silu(acc)
            y = jnp.where(row64 + base >= padr, y, 0.0)
            if l2_scale is not None:
                y = y * lax.rsqrt(jnp.sum(y * y, axis=-1, keepdims=True) + NORM_EPS) * l2_scale
            dst_ref[pl.ds(base, CHUNK), :] = y
            return carry

        lax.fori_loop(0, nc, body, 0)

    conv_into(q_ref, cwq_ref, qs_ref, HEAD_DIM ** -0.5)
    conv_into(k_ref, cwk_ref, ks_ref, 1.0)
    conv_into(v_ref, cwv_ref, vs_ref, None)

    raw_ref[0:padr, :] = jnp.zeros((padr, LANES), F32)
    raw_ref[padr:padr + seq, :] = ab_ref[...]
    neg_a = -jnp.exp(prm_ref[0:1, :])
    dt_bias = prm_ref[1:2, :]
    is_g = (lane64 < 8) | ((lane64 >= 16) & (lane64 < 24))
    is_beta = ((lane64 >= 8) & (lane64 < 16)) | ((lane64 >= 24) & (lane64 < 32))
    ri = lax.broadcasted_iota(jnp.int32, (CHUNK, CHUNK), 0)
    ci = lax.broadcasted_iota(jnp.int32, (CHUNK, CHUNK), 1)
    lower_incl = ri >= ci
    upper_incl = ri <= ci
    tri_lo = jnp.where(lower_incl, 1.0, 0.0).astype(MXU_DTYPE)
    tri_up = jnp.where(upper_incl, 1.0, 0.0).astype(MXU_DTYPE)
    sel_r = lax.broadcasted_iota(jnp.int32, (LANES, 4 * LANES), 0)
    sel_c = lax.broadcasted_iota(jnp.int32, (LANES, 4 * LANES), 1)
    sel = jnp.where(sel_r == head + 8 * (sel_c // LANES), 1.0, 0.0).astype(MXU_DTYPE)

    def gate_body(c, carry):
        base = pl.multiple_of(c * CHUNK, CHUNK)
        ab = raw_ref[pl.ds(base, CHUNK), :]
        g = neg_a * jax.nn.softplus(ab + dt_bias)
        beta = jax.nn.sigmoid(ab)
        gates = jnp.where(is_g, g, jnp.where(is_beta, beta, 0.0))
        gates = jnp.where(row64 + base >= padr, gates, 0.0)
        h3 = _split3(gates)
        cum_f = _dot(tri_lo, h3[0]) + _dot(tri_lo, h3[1]) + _dot(tri_lo, h3[2])
        cum_b = _dot(tri_up, h3[0]) + _dot(tri_up, h3[1]) + _dot(tri_up, h3[2])
        mixed = jnp.where(lane64 < 8, cum_f, jnp.where((lane64 >= 16) & (lane64 < 24), cum_b, gates))
        gsel_ref[pl.ds(base, CHUNK), :] = _exact_right(mixed, sel)
        return carry

    lax.fori_loop(0, nc, gate_body, 0)

    def chunk_body(c, carry):
        base = pl.multiple_of(c * CHUNK, CHUNK)
        q = qs_ref[pl.ds(base, CHUNK), :]
        k = ks_ref[pl.ds(base, CHUNK), :]
        v = vs_ref[pl.ds(base, CHUNK), :]
        gs = gsel_ref[pl.ds(base, CHUNK), :]
        k16 = k.astype(MXU_DTYPE)
        kk = _dot_nt(k16, k16)
        qk = _dot_nt(q.astype(MXU_DTYPE), k16)
        o_intra = jnp.zeros((CHUNK, LANES), F32)
        for d in range(2):
            gc = gs[:, (2 * d) * LANES:(2 * d + 1) * LANES]
            bt = gs[:, (2 * d + 1) * LANES:(2 * d + 2) * LANES]
            incl = lower_incl if d == 0 else upper_incl
            strict = (ri > ci) if d == 0 else (ri < ci)
            g_col = gc[:, :CHUNK]
            g_row = gc.T[:CHUNK, :]
            decay = jnp.where(incl, jnp.exp(jnp.where(incl, g_col - g_row, 0.0)), 0.0)
            a_mat = jnp.where(strict, kk * bt[:, :CHUNK] * decay, 0.0)
            t_mat = _neumann_inverse(a_mat)
            eg = jnp.exp(gc)
            kb = k * bt
            uw = _dot(t_mat.astype(MXU_DTYPE),
                      jnp.concatenate([v * bt, kb * eg], axis=1).astype(MXU_DTYPE))
            uw16 = uw.astype(MXU_DTYPE)
            qkd = jnp.where(incl, qk * decay, 0.0)
            x = _dot(qkd.astype(MXU_DTYPE), uw16)
            o_intra = o_intra + x[:, :LANES]
            q_eff = q * eg - x[:, LANES:]
            g_last = gc[CHUNK - 1:CHUNK, :] if d == 0 else gc[0:1, :]
            k_dec = k * jnp.exp(g_last - gc)
            mn = _dot_tn(k_dec.astype(MXU_DTYPE), uw16)
            qe_ref[d, pl.ds(base, CHUNK), :] = q_eff.astype(MXU_DTYPE)
            cbase = pl.multiple_of(c * HEAD_DIM, HEAD_DIM)
            nn_ref[d, pl.ds(cbase, HEAD_DIM), :] = mn[:, :LANES]
            mp_ref[d, pl.ds(cbase, HEAD_DIM), :] = mn[:, LANES:].astype(MXU_DTYPE)
            dbase = pl.multiple_of(c * SUBLANES, SUBLANES)
            dd_ref[d, pl.ds(dbase, SUBLANES), :] = jnp.broadcast_to(jnp.exp(g_last), (SUBLANES, LANES))
        osum_ref[pl.ds(base, CHUNK), :] = o_intra
        return carry

    lax.fori_loop(0, nc, chunk_body, 0)

    def scan_body(i, carry):
        new = []
        for d in range(2):
            s = carry[d]
            c = i if d == 0 else nc - 1 - i
            base = pl.multiple_of(c * CHUNK, CHUNK)
            cbase = pl.multiple_of(c * HEAD_DIM, HEAD_DIM)
            dbase = pl.multiple_of(c * SUBLANES, SUBLANES)
            s16 = s.astype(MXU_DTYPE)
            osum_ref[pl.ds(base, CHUNK), :] += _dot(qe_ref[d, pl.ds(base, CHUNK), :], s16)
            dec = dd_ref[d, pl.ds(dbase, 1), :]
            new.append(dec * s - _dot(mp_ref[d, pl.ds(cbase, HEAD_DIM), :], s16)
                       + nn_ref[d, pl.ds(cbase, HEAD_DIM), :])
        return tuple(new)

    zero_state = jnp.zeros((HEAD_DIM, HEAD_DIM), F32)
    lax.fori_loop(0, nc, scan_body, (zero_state, zero_state))

    gain = ong_ref[...]

    def finish(o, z):
        ms = jnp.mean(o * o, axis=-1, keepdims=True)
        return (o * lax.rsqrt(ms + NORM_EPS) * gain * _silu(z)).astype(o_ref.dtype)

    first = CHUNK - padr if padr else 0
    if first:
        o_ref[0:first, :] = finish(osum_ref[padr:CHUNK, :], z_ref[0:first, :])

    def out_body(c, carry):
        base = pl.multiple_of(c * CHUNK, CHUNK)
        dst = pl.multiple_of(c * CHUNK - padr, SUBLANES)
        o_ref[pl.ds(dst, CHUNK), :] = finish(osum_ref[pl.ds(base, CHUNK), :], z_ref[pl.ds(dst, CHUNK), :])
        return carry

    lax.fori_loop(1 if first else 0, nc, out_body, 0)


def _gdn(proj3, conv_w, gate_prm, out_norm_g):
    b, seq, _ = proj3.shape
    lp = -(-seq // CHUNK) * CHUNK
    nc = lp // CHUNK
    col = lambda off: pl.BlockSpec((None, seq, LANES), lambda bi, hi: (bi, 0, off + hi))
    cw = lambda off: pl.BlockSpec((CONV_K, LANES), lambda bi, hi: (0, off + hi))
    return pl.pallas_call(
        functools.partial(_gdn_kernel, seq=seq, lp=lp),
        grid=(b, N_HEADS),
        in_specs=[
            col(COL_QKV // LANES), col(COL_QKV // LANES + N_HEADS), col(COL_QKV // LANES + 2 * N_HEADS),
            col(COL_Z // LANES),
            pl.BlockSpec((None, seq, LANES), lambda bi, hi: (bi, 0, COL_AB // LANES)),
            cw(0), cw(N_HEADS), cw(2 * N_HEADS),
            pl.BlockSpec((SUBLANES, LANES), lambda bi, hi: (0, 0)),
            pl.BlockSpec((1, LANES), lambda bi, hi: (0, 0)),
        ],
        out_specs=pl.BlockSpec((None, seq, LANES), lambda bi, hi: (bi, 0, hi)),
        out_shape=jax.ShapeDtypeStruct((b, seq, N_HEADS * HEAD_DIM), MXU_DTYPE),
        scratch_shapes=[
            pltpu.VMEM((lp + 2 * SUBLANES, LANES), F32),
            pltpu.VMEM((lp, LANES), F32),
            pltpu.VMEM((lp, LANES), F32),
            pltpu.VMEM((lp, LANES), F32),
            pltpu.VMEM((lp, 4 * LANES), F32),
            pltpu.VMEM((2, lp, LANES), MXU_DTYPE),
            pltpu.VMEM((2, nc * HEAD_DIM, LANES), MXU_DTYPE),
            pltpu.VMEM((2, nc * HEAD_DIM, LANES), F32),
            pltpu.VMEM((2, nc * SUBLANES, LANES), F32),
            pltpu.VMEM((lp, LANES), F32),
        ],
        compiler_params=_params(("arbitrary", "arbitrary")),
        name="gdn",
    )(proj3, proj3, proj3, proj3, proj3, conv_w, conv_w, conv_w, gate_prm, out_norm_g)


def _merge_kernel(h_ref, og_ref, ym_ref, gf_ref, gd_ref, wf_ref, wd_ref, wo_ref, n2_ref, wr_ref, br_ref,
                  h1_ref, hn_ref, ridx_ref, rw_ref, cnt_ref):
    @pl.when(pl.program_id(0) == 0)
    def _():
        cnt_ref[...] = jnp.zeros_like(cnt_ref)

    y_f = _dot(ym_ref[...], wf_ref[...])
    y_d = _dot(og_ref[...], wd_ref[...])
    merged = jax.nn.sigmoid(gf_ref[...]) * y_f + jax.nn.sigmoid(gd_ref[...]) * y_d
    h1 = h_ref[...] + _dot(merged.astype(MXU_DTYPE), wo_ref[...])
    h1_ref[...] = h1
    ms = jnp.mean(h1 * h1, axis=-1, keepdims=True)
    hn = h1 * lax.rsqrt(ms + NORM_EPS) * n2_ref[...]
    hn_ref[...] = hn

    x_hi = hn.astype(MXU_DTYPE)
    x_lo = (hn - x_hi.astype(F32)).astype(MXU_DTYPE)
    w = wr_ref[...]
    w_hi = w.astype(MXU_DTYPE)
    w_lo = (w - w_hi.astype(F32)).astype(MXU_DTYPE)
    logits = _dot(x_hi, w_hi) + _dot(x_hi, w_lo) + _dot(x_lo, w_hi) + br_ref[...]

    tm = logits.shape[0]
    lane = lax.broadcasted_iota(jnp.int32, (tm, LANES), 1)
    neg_inf = -jnp.inf
    is_group = lane < N_GROUPS
    gl = jnp.where(is_group, logits, neg_inf)
    ge = jnp.exp(gl - jnp.max(gl, axis=-1, keepdims=True))
    gp = ge / jnp.sum(ge, axis=-1, keepdims=True)
    p_group = jnp.max(gp, axis=-1, keepdims=True)
    g_idx = jnp.min(jnp.where(is_group & (gp == p_group), lane, LANES), axis=-1, keepdims=True)

    in_group = (lane >= N_GROUPS) & (lane < N_GROUPS + N_EXPERTS) & (((lane - N_GROUPS) >> 3) == g_idx)
    el = jnp.where(in_group, logits, neg_inf)
    ee = jnp.exp(el - jnp.max(el, axis=-1, keepdims=True))
    ep = ee / jnp.sum(ee, axis=-1, keepdims=True)
    v1 = jnp.max(jnp.where(in_group, ep, -1.0), axis=-1, keepdims=True)
    i1 = jnp.min(jnp.where(in_group & (ep == v1), lane, LANES), axis=-1, keepdims=True)
    rest = in_group & (lane != i1)
    v2 = jnp.max(jnp.where(rest, ep, -1.0), axis=-1, keepdims=True)
    i2 = jnp.min(jnp.where(rest & (ep == v2), lane, LANES), axis=-1, keepdims=True)
    denom = v1 + v2
    w1 = p_group * (v1 / denom)
    w2 = p_group * (v2 / denom)
    e1 = i1 - N_GROUPS
    e2 = i2 - N_GROUPS
    ridx_ref[...] = jnp.where(lane == 0, e1, jnp.where(lane == 1, e2, 0))
    rw_ref[...] = jnp.where(lane == 0, w1, jnp.where(lane == 1, w2, 0.0))
    onehots = jnp.where((lane == e1) | (lane == e2), 1.0, 0.0)
    cnt_ref[...] += jnp.broadcast_to(jnp.sum(onehots, axis=0, keepdims=True), cnt_ref.shape)


def _merge(h2d, og2d, ymix2d, proj2d, w_fourier, w_delta, w_out, norm2_g, w_router, b_router, tm):
    t, d = h2d.shape
    row = lambda w: pl.BlockSpec((tm, w), lambda i: (i, 0))
    const = lambda shape: pl.BlockSpec(shape, lambda i: (0, 0))
    return pl.pallas_call(
        _merge_kernel,
        grid=(t // tm,),
        in_specs=[
            row(d), row(d), row(F_WIDTH),
            pl.BlockSpec((tm, d), lambda i: (i, COL_GATE_F // d)),
            pl.BlockSpec((tm, d), lambda i: (i, COL_GATE_D // d)),
            const((F_WIDTH, d)), const((d, d)), const((d, d)), const((1, d)),
            const((d, LANES)), const((1, LANES)),
        ],
        out_specs=[row(d), row(d), row(LANES), row(LANES), const((SUBLANES, LANES))],
        out_shape=[
            jax.ShapeDtypeStruct((t, d), F32),
            jax.ShapeDtypeStruct((t, d), F32),
            jax.ShapeDtypeStruct((t, LANES), jnp.int32),
            jax.ShapeDtypeStruct((t, LANES), F32),
            jax.ShapeDtypeStruct((SUBLANES, LANES), F32),
        ],
        compiler_params=_params(("arbitrary",)),
        name="merge",
    )(h2d, og2d, ymix2d, proj2d, proj2d, w_fourier, w_delta, w_out, norm2_g, w_router, b_router)


def _plan_kernel(ridx_ref, cnt_ref, dest_ref, blk_ref, base_ref, run_ref, *, n_blk_rows):
    tt = ridx_ref.shape[0]
    lane8 = lax.broadcasted_iota(jnp.int32, (SUBLANES, LANES), 1)

    @pl.when(pl.program_id(0) == 0)
    def _():
        cnt = cnt_ref[...].astype(jnp.int32)
        padded = (((cnt + (EXPERT_BLOCK - 1)) // EXPERT_BLOCK) * EXPERT_BLOCK).astype(F32)
        r = lax.broadcasted_iota(jnp.int32, (LANES, LANES), 0)
        c = lax.broadcasted_iota(jnp.int32, (LANES, LANES), 1)
        before = jnp.where(r < c, 1.0, 0.0).astype(MXU_DTYPE)
        start = _exact_right(padded, before)
        base_ref[...] = start
        run_ref[...] = jnp.zeros_like(run_ref)
        ends = (start + padded)[0:1, :]
        blk_start = (lax.broadcasted_iota(jnp.int32, (n_blk_rows, LANES), 0) * EXPERT_BLOCK).astype(F32)
        lane = lax.broadcasted_iota(jnp.int32, (n_blk_rows, LANES), 1)
        hit = jnp.where((lane < N_EXPERTS) & (ends <= blk_start), 1.0, 0.0)
        owner = jnp.minimum(jnp.sum(hit, axis=-1, keepdims=True), float(N_EXPERTS - 1))
        n_used = jnp.max(jnp.where(lane8[0:1, :] < N_EXPERTS, ends, 0.0), axis=-1, keepdims=True) / EXPERT_BLOCK
        blk_ref[0:n_blk_rows, :] = jnp.broadcast_to(owner, (n_blk_rows, LANES)).astype(jnp.int32)
        blk_ref[n_blk_rows:, :] = jnp.broadcast_to(n_used, (SUBLANES, LANES)).astype(jnp.int32)

    lane = lax.broadcasted_iota(jnp.int32, (tt, LANES), 1)
    ridx = ridx_ref[...]
    oh0 = jnp.where(lane == ridx[:, 0:1], 1.0, 0.0)
    oh1 = jnp.where(lane == ridx[:, 1:2], 1.0, 0.0)
    r = lax.broadcasted_iota(jnp.int32, (tt, tt), 0)
    c = lax.broadcasted_iota(jnp.int32, (tt, tt), 1)
    earlier = jnp.where(c < r, 1.0, 0.0).astype(MXU_DTYPE)
    prefix = _dot(earlier, jnp.concatenate([oh0, oh1], axis=1).astype(MXU_DTYPE))
    c0 = jnp.sum(oh0, axis=0, keepdims=True)
    c1 = jnp.sum(oh1, axis=0, keepdims=True)
    first = base_ref[0:1, :] + run_ref[0:1, :]
    d0 = jnp.sum(oh0 * (prefix[:, :LANES] + first), axis=-1, keepdims=True)
    d1 = jnp.sum(oh1 * (prefix[:, LANES:] + first + c0), axis=-1, keepdims=True)
    run_ref[...] += jnp.broadcast_to(c0 + c1, run_ref.shape)
    dest_ref[...] = jnp.where(lane == 0, d0, jnp.where(lane == 1, d1, 0.0)).astype(jnp.int32)


def _plan(ridx, cnt, tt, n_blocks):
    t = ridx.shape[0]
    n_blk_rows = -(-n_blocks // SUBLANES) * SUBLANES
    return pl.pallas_call(
        functools.partial(_plan_kernel, n_blk_rows=n_blk_rows),
        grid=(t // tt,),
        in_specs=[pl.BlockSpec((tt, LANES), lambda i: (i, 0)),
                  pl.BlockSpec((SUBLANES, LANES), lambda i: (0, 0))],
        out_specs=[pl.BlockSpec((tt, LANES), lambda i: (i, 0)),
                   pl.BlockSpec((n_blk_rows + SUBLANES, LANES), lambda i: (0, 0))],
        out_shape=[jax.ShapeDtypeStruct((t, LANES), jnp.int32),
                   jax.ShapeDtypeStruct((n_blk_rows + SUBLANES, LANES), jnp.int32)],
        scratch_shapes=[pltpu.VMEM((SUBLANES, LANES), F32), pltpu.VMEM((SUBLANES, LANES), F32)],
        compiler_params=_params(("arbitrary",)),
        name="plan",
    )(ridx, cnt)


def _scatter_kernel(dest_ref, hn_ref, xb_in_ref, xb_ref, sem, *, ts):
    del xb_in_ref
    tile = pl.program_id(0)

    def issue(t, carry):
        src = hn_ref.at[pl.ds(tile * ts + t, 1)]
        for k in range(TOP_K):
            pltpu.make_async_copy(src, xb_ref.at[pl.ds(dest_ref[0, TOP_K * t + k], 1)], sem).start()
        return carry

    lax.fori_loop(0, ts, issue, 0)

    def drain(t, carry):
        src = hn_ref.at[pl.ds(tile * ts + t, 1)]
        for k in range(TOP_K):
            pltpu.make_async_copy(src, xb_ref.at[pl.ds(dest_ref[0, TOP_K * t + k], 1)], sem).wait()
        return carry

    lax.fori_loop(0, ts, drain, 0)


def _scatter(dest_flat, hn2d, xb_zero, ts):
    t, d = hn2d.shape
    return pl.pallas_call(
        functools.partial(_scatter_kernel, ts=ts),
        grid=(t // ts,),
        in_specs=[pl.BlockSpec((None, 1, TOP_K * ts), lambda i: (i, 0, 0), memory_space=pltpu.SMEM),
                  pl.BlockSpec(memory_space=pl.ANY),
                  pl.BlockSpec(memory_space=pl.ANY)],
        out_specs=pl.BlockSpec(memory_space=pl.ANY),
        out_shape=jax.ShapeDtypeStruct(xb_zero.shape, F32),
        scratch_shapes=[pltpu.SemaphoreType.DMA(())],
        input_output_aliases={2: 0},
        compiler_params=_params(("arbitrary",)),
        name="scatter",
    )(dest_flat, hn2d, xb_zero)


def _expert_kernel(owner_ref, nused_ref, x_ref, wg_ref, wu_ref, wd_ref, y_ref):
    i = pl.program_id(0)

    @pl.when(i < nused_ref[0])
    def _():
        x = x_ref[...].astype(MXU_DTYPE)
        gate = _dot(x, wg_ref[...].astype(MXU_DTYPE))
        up = _dot(x, wu_ref[...].astype(MXU_DTYPE))
        y_ref[...] = _dot((_silu(gate) * up).astype(MXU_DTYPE), wd_ref[...].astype(MXU_DTYPE))

    @pl.when(i >= nused_ref[0])
    def _():
        y_ref[...] = jnp.zeros_like(y_ref)


def _experts(owner, n_used, xb, w_gate, w_up, w_down):
    n_slots, d = xb.shape
    de = w_gate.shape[-1]
    n_blocks = n_slots // EXPERT_BLOCK
    grid_spec = pltpu.PrefetchScalarGridSpec(
        num_scalar_prefetch=2,
        grid=(n_blocks,),
        in_specs=[
            pl.BlockSpec((EXPERT_BLOCK, d), lambda i, own, nu: (i, 0)),
            pl.BlockSpec((None, d, de), lambda i, own, nu: (own[i], 0, 0)),
            pl.BlockSpec((None, d, de), lambda i, own, nu: (own[i], 0, 0)),
            pl.BlockSpec((None, de, d), lambda i, own, nu: (own[i], 0, 0)),
        ],
        out_specs=pl.BlockSpec((EXPERT_BLOCK, d), lambda i, own, nu: (i, 0)),
    )
    return pl.pallas_call(
        _expert_kernel,
        grid_spec=grid_spec,
        out_shape=jax.ShapeDtypeStruct((n_slots, d), F32),
        compiler_params=_params(("arbitrary",)),
        name="experts",
    )(owner, n_used, xb, w_gate, w_up, w_down)


def _combine_kernel(dest_ref, yb_ref, h1_ref, rw_ref, fg_ref, o_ref, buf_ref, sem, *, tc):
    def issue(t, carry):
        for k in range(TOP_K):
            pltpu.make_async_copy(yb_ref.at[pl.ds(dest_ref[0, TOP_K * t + k], 1)],
                                  buf_ref.at[k, pl.ds(t, 1)], sem).start()
        return carry

    lax.fori_loop(0, tc, issue, 0)

    def drain(t, carry):
        for k in range(TOP_K):
            pltpu.make_async_copy(yb_ref.at[pl.ds(dest_ref[0, TOP_K * t + k], 1)],
                                  buf_ref.at[k, pl.ds(t, 1)], sem).wait()
        return carry

    lax.fori_loop(0, tc, drain, 0)

    rw = rw_ref[...]
    h2 = h1_ref[...] + (buf_ref[0] * rw[:, 0:1] + buf_ref[1] * rw[:, 1:2])
    ms = jnp.mean(h2 * h2, axis=-1, keepdims=True)
    o_ref[...] = h2 * lax.rsqrt(ms + NORM_EPS) * fg_ref[...]


def _combine(dest_flat, yb, h1, rw, final_g, tc):
    t, d = h1.shape
    return pl.pallas_call(
        functools.partial(_combine_kernel, tc=tc),
        grid=(t // tc,),
        in_specs=[pl.BlockSpec((None, 1, TOP_K * tc), lambda i: (i, 0, 0), memory_space=pltpu.SMEM),
                  pl.BlockSpec(memory_space=pl.ANY),
                  pl.BlockSpec((tc, d), lambda i: (i, 0)),
                  pl.BlockSpec((tc, LANES), lambda i: (i, 0)),
                  pl.BlockSpec((1, d), lambda i: (0, 0))],
        out_specs=pl.BlockSpec((tc, d), lambda i: (i, 0)),
        out_shape=jax.ShapeDtypeStruct((t, d), F32),
        scratch_shapes=[pltpu.VMEM((TOP_K, tc, d), F32), pltpu.SemaphoreType.DMA(())],
        compiler_params=_params(("arbitrary",)),
        name="combine",
    )(dest_flat, yb, h1, rw, final_g)


def _tile(total, candidates):
    for c in candidates:
        if total % c == 0:
            return c
    raise ValueError(f"no tile for {total}")


def _dft_tables(seq):
    kp = -(-seq // LANES) * LANES
    c = jnp.arange(FGROUP_DIM, dtype=jnp.int32)
    ang_c = (2.0 * math.pi / FGROUP_DIM) * ((c[:, None] * c[None, :]) % FGROUP_DIM).astype(F32)
    cs128 = jnp.concatenate([jnp.cos(ang_c), jnp.sin(ang_c)], axis=1).astype(MXU_DTYPE)
    n = jnp.arange(seq, dtype=jnp.int32)
    ang_l = (2.0 * math.pi / seq) * ((n[:, None] * n[None, :]) % seq).astype(F32)
    pad = ((0, 0), (0, kp - seq))
    twid = jnp.concatenate([jnp.pad(jnp.cos(ang_l), pad), jnp.pad(-jnp.sin(ang_l), pad)], axis=1)
    return cs128, twid.astype(MXU_DTYPE)


def kernel(x, meta_tokens, norm1_g, w_in, conv_w, a_log_fwd, dt_bias_fwd, a_log_bwd, dt_bias_bwd, out_norm_g, w_fourier, w_delta, w_out, norm2_g, w_router_group, b_router_group, w_router_expert, b_router_expert, w_gate_e, w_up_e, w_down_e, final_norm_g):
    bsz, n_real, d = x.shape
    seq = N_META + n_real
    n_tok = bsz * seq
    assert (CHUNK - N_META) % CHUNK + seq == -(-seq // CHUNK) * CHUNK, "meta tokens must fill the tail of chunk 0"
    assert w_in.shape[0] == 1, "single trunk layer"
    layer = 0

    meta = jnp.broadcast_to(meta_tokens[None].astype(x.dtype), (bsz, N_META, d))
    h = jnp.concatenate([meta, x], axis=1).reshape(n_tok, d)

    w = w_in[layer]
    o_f, o_qkv, o_z, o_ab, o_gf, o_gd = 0, F_WIDTH, F_WIDTH + 3 * QK_WIDTH, F_WIDTH + 4 * QK_WIDTH, \
        F_WIDTH + 4 * QK_WIDTH + 4 * N_HEADS, F_WIDTH + 4 * QK_WIDTH + 4 * N_HEADS + d
    w_cat = jnp.concatenate(
        [w[:, o_qkv:o_z], w[:, o_z:o_ab], w[:, o_gf:o_gd], w[:, o_gd:o_gd + d], w[:, o_f:o_qkv], w[:, o_ab:o_gf],
         jnp.zeros((d, PROJ_WIDTH - w.shape[1]), w.dtype)], axis=1).astype(MXU_DTYPE)

    proj = _in_proj(h, norm1_g[layer][None, :], w_cat, _tile(n_tok, (1032, 104, 8)))
    proj3 = proj.reshape(bsz, seq, PROJ_WIDTH)

    cs128, twid = _dft_tables(seq)
    ymix = _fnet(proj3, cs128, twid, _tile(seq, (344, 104, 8)))

    gate_prm = jnp.zeros((SUBLANES, LANES), F32)
    gate_prm = gate_prm.at[0, 0:N_HEADS].set(a_log_fwd[layer]).at[0, 2 * N_HEADS:3 * N_HEADS].set(a_log_bwd[layer])
    gate_prm = gate_prm.at[1, 0:N_HEADS].set(dt_bias_fwd[layer]).at[1, 2 * N_HEADS:3 * N_HEADS].set(dt_bias_bwd[layer])
    og = _gdn(proj3, conv_w[layer], gate_prm, out_norm_g[layer][None, :])

    w_router = jnp.zeros((d, LANES), F32)
    w_router = w_router.at[:, 0:N_GROUPS].set(w_router_group[layer])
    w_router = w_router.at[:, N_GROUPS:N_GROUPS + N_EXPERTS].set(w_router_expert[layer])
    b_router = jnp.zeros((1, LANES), F32)
    b_router = b_router.at[0, 0:N_GROUPS].set(b_router_group[layer])
    b_router = b_router.at[0, N_GROUPS:N_GROUPS + N_EXPERTS].set(b_router_expert[layer])
    tt = _tile(n_tok, (384, 128))
    h1, hn, ridx, rw, cnt = _merge(
        h, og.reshape(n_tok, d), ymix.reshape(n_tok, F_WIDTH), proj,
        w_fourier[layer].astype(MXU_DTYPE), w_delta[layer].astype(MXU_DTYPE), w_out[layer].astype(MXU_DTYPE),
        norm2_g[layer][None, :], w_router, b_router, tt)

    n_assign = n_tok * TOP_K
    n_blocks = -(-n_assign // EXPERT_BLOCK) + N_EXPERTS
    dest, blk = _plan(ridx, cnt, tt, n_blocks)
    dest_flat = dest[:, :TOP_K].reshape(n_tok // tt, 1, TOP_K * tt)
    owner = blk[:n_blocks, 0]
    n_used = blk[-1:, 0]

    xb = _scatter(dest_flat, hn, jnp.zeros((n_blocks * EXPERT_BLOCK, d), F32), tt)
    yb = _experts(owner, n_used, xb, w_gate_e[layer], w_up_e[layer], w_down_e[layer])
    out = _combine(dest_flat, yb, h1, rw, final_norm_g[None, :], tt)
    return out.reshape(bsz, seq, d)[:, N_META:]
```

```python
import functools
import math

import jax
import jax.numpy as jnp
from jax import lax
from jax.experimental import pallas as pl
from jax.experimental.pallas import tpu as pltpu

F32 = jnp.float32
MXU_DTYPE = jnp.bfloat16

N_META = 16
CHUNK = 64
N_FGROUPS = 4
FGROUP_DIM = 128
F_WIDTH = N_FGROUPS * FGROUP_DIM
N_HEADS = 8
HEAD_DIM = 128
QK_WIDTH = N_HEADS * HEAD_DIM
CONV_K = 5
N_GROUPS = 8
EXPERTS_PER_GROUP = 8
N_EXPERTS = N_GROUPS * EXPERTS_PER_GROUP
TOP_K = 2
EXPERT_BLOCK = 128
NORM_EPS = 1e-6
LANES = 128
SUBLANES = 8
VMEM_LIMIT = 52 * 1024 * 1024

COL_QKV = 0
COL_Z = 3 * QK_WIDTH
COL_GATE_F = COL_Z + QK_WIDTH
COL_GATE_D = COL_GATE_F + 1024
COL_FIN = COL_GATE_D + 1024
COL_AB = COL_FIN + F_WIDTH
PROJ_TN = 768
PROJ_WIDTH = -(-(COL_AB + LANES) // PROJ_TN) * PROJ_TN


def _dot(a, b):
    return jnp.dot(a, b, preferred_element_type=F32)


def _dot_nt(a, b):
    return lax.dot_general(a, b, (((1,), (1,)), ((), ())), preferred_element_type=F32)


def _dot_tn(a, b):
    return lax.dot_general(a, b, (((0,), (0,)), ((), ())), preferred_element_type=F32)


def _split3(x):
    hi = x.astype(MXU_DTYPE)
    r1 = x - hi.astype(F32)
    mid = r1.astype(MXU_DTYPE)
    lo = (r1 - mid.astype(F32)).astype(MXU_DTYPE)
    return hi, mid, lo


def _exact_left(sel, x):
    hi, mid, lo = _split3(x)
    return _dot(sel, hi) + _dot(sel, mid) + _dot(sel, lo)


def _exact_right(x, sel):
    hi, mid, lo = _split3(x)
    return _dot(hi, sel) + _dot(mid, sel) + _dot(lo, sel)


def _silu(x):
    return x * jax.nn.sigmoid(x)


def _params(sem):
    return pltpu.CompilerParams(dimension_semantics=sem, vmem_limit_bytes=VMEM_LIMIT)


def _inproj_kernel(h_ref, g_ref, w_ref, o_ref, hn_ref):
    @pl.when(pl.program_id(1) == 0)
    def _():
        x = h_ref[...]
        ms = jnp.mean(x * x, axis=-1, keepdims=True)
        hn_ref[...] = (x * lax.rsqrt(ms + NORM_EPS) * g_ref[...]).astype(MXU_DTYPE)

    o_ref[...] = _dot(hn_ref[...], w_ref[...])


def _in_proj(h2d, gain, w_cat, tm):
    t, d = h2d.shape
    n = w_cat.shape[1]
    return pl.pallas_call(
        _inproj_kernel,
        grid=(t // tm, n // PROJ_TN),
        in_specs=[
            pl.BlockSpec((tm, d), lambda i, j: (i, 0)),
            pl.BlockSpec((1, d), lambda i, j: (0, 0)),
            pl.BlockSpec((d, PROJ_TN), lambda i, j: (0, j)),
        ],
        out_specs=pl.BlockSpec((tm, PROJ_TN), lambda i, j: (i, j)),
        out_shape=jax.ShapeDtypeStruct((t, n), F32),
        scratch_shapes=[pltpu.VMEM((tm, d), MXU_DTYPE)],
        compiler_params=_params(("arbitrary", "arbitrary")),
        name="in_proj",
    )(h2d, gain, w_cat)


def _fnet_kernel(x_ref, cs_ref, tw_ref, o_ref, r_ref, *, seq, kp, scale):
    @pl.when(pl.program_id(1) == 0)
    def _():
        tail = kp - seq
        if tail:
            r_ref[seq:kp, :] = jnp.zeros((tail, F_WIDTH), MXU_DTYPE)
            r_ref[kp + seq:, :] = jnp.zeros((tail, F_WIDTH), MXU_DTYPE)
        cs = cs_ref[...]
        for g in range(N_FGROUPS):
            cols = slice(g * FGROUP_DIM, (g + 1) * FGROUP_DIM)
            p = _dot(x_ref[:, cols].astype(MXU_DTYPE), cs)
            r_ref[0:seq, cols] = p[:, :FGROUP_DIM].astype(MXU_DTYPE)
            r_ref[kp:kp + seq, cols] = p[:, FGROUP_DIM:].astype(MXU_DTYPE)

    o_ref[...] = (_dot(tw_ref[...], r_ref[...]) * scale).astype(o_ref.dtype)


def _fnet(proj3, cs128, twid, tmf):
    b, seq, _ = proj3.shape
    kp = twid.shape[1] // 2
    scale = 1.0 / math.sqrt(seq * FGROUP_DIM)
    return pl.pallas_call(
        functools.partial(_fnet_kernel, seq=seq, kp=kp, scale=scale),
        grid=(b, seq // tmf),
        in_specs=[
            pl.BlockSpec((None, seq, F_WIDTH), lambda bi, i: (bi, 0, COL_FIN // F_WIDTH)),
            pl.BlockSpec((FGROUP_DIM, 2 * FGROUP_DIM), lambda bi, i: (0, 0)),
            pl.BlockSpec((tmf, 2 * kp), lambda bi, i: (i, 0)),
        ],
        out_specs=pl.BlockSpec((None, tmf, F_WIDTH), lambda bi, i: (bi, i, 0)),
        out_shape=jax.ShapeDtypeStruct((b, seq, F_WIDTH), MXU_DTYPE),
        scratch_shapes=[pltpu.VMEM((2 * kp, F_WIDTH), MXU_DTYPE)],
        compiler_params=_params(("arbitrary", "arbitrary")),
        name="fnet",
    )(proj3, cs128, twid)


def _neumann_inverse(a):
    n = a.shape[0]
    ri = lax.broadcasted_iota(jnp.int32, (n, n), 0)
    ci = lax.broadcasted_iota(jnp.int32, (n, n), 1)
    t = jnp.where(ri == ci, 1.0, 0.0).astype(F32) - a
    a16 = a.astype(MXU_DTYPE)
    p = _dot(a16, a16)
    steps = int(math.log2(n)) - 1
    for s in range(steps):
        p16 = p.astype(MXU_DTYPE)
        if s < steps - 1:
            tp = _dot(jnp.concatenate([t, p], axis=0).astype(MXU_DTYPE), p16)
            t = t + tp[:n]
            p = tp[n:]
        else:
            t = t + _dot(t.astype(MXU_DTYPE), p16)
    return t


def _gdn_kernel(q_ref, k_ref, v_ref, z_ref, ab_ref, cwq_ref, cwk_ref, cwv_ref, prm_ref, ong_ref,
                o_ref,
                raw_ref, qs_ref, ks_ref, vs_ref, gsel_ref, qe_ref, mp_ref, nn_ref, dd_ref, osum_ref,
                *, seq, lp):
    head = pl.program_id(1)
    nc = lp // CHUNK
    padr = lp - seq
    halo = SUBLANES
    row64 = lax.broadcasted_iota(jnp.int32, (CHUNK, LANES), 0)
    lane64 = lax.broadcasted_iota(jnp.int32, (CHUNK, LANES), 1)

    def conv_into(src_ref, cw_ref, dst_ref, l2_scale):
        raw_ref[0:padr + halo, :] = jnp.zeros((padr + halo, LANES), F32)
        raw_ref[padr + halo:padr + halo + seq, :] = src_ref[...]
        raw_ref[lp + halo:lp + 2 * halo, :] = jnp.zeros((halo, LANES), F32)
        cw = cw_ref[...]

        def body(c, carry):
            base = pl.multiple_of(c * CHUNK, CHUNK)
            x = raw_ref[pl.ds(base, CHUNK + 2 * halo), :]
            acc = jnp.zeros((CHUNK, LANES), F32)
            for j in range(CONV_K):
                off = halo - (CONV_K - 1) // 2 + j
                acc = acc + cw[j:j + 1, :] * x[off:off + CHUNK, :]
            y = _silu(acc)
            y = jnp.where(row64 + base >= padr, y, 0.0)
            if l2_scale is not None:
                y = y * lax.rsqrt(jnp.sum(y * y, axis=-1, keepdims=True) + NORM_EPS) * l2_scale
            dst_ref[pl.ds(base, CHUNK), :] = y
            return carry

        lax.fori_loop(0, nc, body, 0)

    conv_into(q_ref, cwq_ref, qs_ref, HEAD_DIM ** -0.5)
    conv_into(k_ref, cwk_ref, ks_ref, 1.0)
    conv_into(v_ref, cwv_ref, vs_ref, None)

    raw_ref[0:padr, :] = jnp.zeros((padr, LANES), F32)
    raw_ref[padr:padr + seq, :] = ab_ref[...]
    neg_a = -jnp.exp(prm_ref[0:1, :])
    dt_bias = prm_ref[1:2, :]
    is_g = (lane64 < 8) | ((lane64 >= 16) & (lane64 < 24))
    is_beta = ((lane64 >= 8) & (lane64 < 16)) | ((lane64 >= 24) & (lane64 < 32))
    ri = lax.broadcasted_iota(jnp.int32, (CHUNK, CHUNK), 0)
    ci = lax.broadcasted_iota(jnp.int32, (CHUNK, CHUNK), 1)
    lower_incl = ri >= ci
    upper_incl = ri <= ci
    tri_lo = jnp.where(lower_incl, 1.0, 0.0).astype(MXU_DTYPE)
    tri_up = jnp.where(upper_incl, 1.0, 0.0).astype(MXU_DTYPE)
    sel_r = lax.broadcasted_iota(jnp.int32, (LANES, 4 * LANES), 0)
    sel_c = lax.broadcasted_iota(jnp.int32, (LANES, 4 * LANES), 1)
    sel = jnp.where(sel_r == head + 8 * (sel_c // LANES), 1.0, 0.0).astype(MXU_DTYPE)

    def gate_body(c, carry):
        base = pl.multiple_of(c * CHUNK, CHUNK)
        ab = raw_ref[pl.ds(base, CHUNK), :]
        g = neg_a * jax.nn.softplus(ab + dt_bias)
        beta = jax.nn.sigmoid(ab)
        gates = jnp.where(is_g, g, jnp.where(is_beta, beta, 0.0))
        gates = jnp.where(row64 + base >= padr, gates, 0.0)
        h3 = _split3(gates)
        cum_f = _dot(tri_lo, h3[0]) + _dot(tri_lo, h3[1]) + _dot(tri_lo, h3[2])
        cum_b = _dot(tri_up, h3[0]) + _dot(tri_up, h3[1]) + _dot(tri_up, h3[2])
        mixed = jnp.where(lane64 < 8, cum_f, jnp.where((lane64 >= 16) & (lane64 < 24), cum_b, gates))
        gsel_ref[pl.ds(base, CHUNK), :] = _exact_right(mixed, sel)
        return carry

    lax.fori_loop(0, nc, gate_body, 0)

    def chunk_body(c, carry):
        base = pl.multiple_of(c * CHUNK, CHUNK)
        q = qs_ref[pl.ds(base, CHUNK), :]
        k = ks_ref[pl.ds(base, CHUNK), :]
        v = vs_ref[pl.ds(base, CHUNK), :]
        gs = gsel_ref[pl.ds(base, CHUNK), :]
        k16 = k.astype(MXU_DTYPE)
        kk = _dot_nt(k16, k16)
        qk = _dot_nt(q.astype(MXU_DTYPE), k16)
        o_intra = jnp.zeros((CHUNK, LANES), F32)
        for d in range(2):
            gc = gs[:, (2 * d) * LANES:(2 * d + 1) * LANES]
            bt = gs[:, (2 * d + 1) * LANES:(2 * d + 2) * LANES]
            incl = lower_incl if d == 0 else upper_incl
            strict = (ri > ci) if d == 0 else (ri < ci)
            g_col = gc[:, :CHUNK]
            g_row = gc.T[:CHUNK, :]
            decay = jnp.where(incl, jnp.exp(jnp.where(incl, g_col - g_row, 0.0)), 0.0)
            a_mat = jnp.where(strict, kk * bt[:, :CHUNK] * decay, 0.0)
            t_mat = _neumann_inverse(a_mat)
            eg = jnp.exp(gc)
            kb = k * bt
            uw = _dot(t_mat.astype(MXU_DTYPE),
                      jnp.concatenate([v * bt, kb * eg], axis=1).astype(MXU_DTYPE))
            uw16 = uw.astype(MXU_DTYPE)
            qkd = jnp.where(incl, qk * decay, 0.0)
            x = _dot(qkd.astype(MXU_DTYPE), uw16)
            o_intra = o_intra + x[:, :LANES]
            q_eff = q * eg - x[:, LANES:]
            g_last = gc[CHUNK - 1:CHUNK, :] if d == 0 else gc[0:1, :]
            k_dec = k * jnp.exp(g_last - gc)
            mn = _dot_tn(k_dec.astype(MXU_DTYPE), uw16)
            qe_ref[d, pl.ds(base, CHUNK), :] = q_eff.astype(MXU_DTYPE)
            cbase = pl.multiple_of(c * HEAD_DIM, HEAD_DIM)
            nn_ref[d, pl.ds(cbase, HEAD_DIM), :] = mn[:, :LANES]
            mp_ref[d, pl.ds(cbase, HEAD_DIM), :] = mn[:, LANES:].astype(MXU_DTYPE)
            dbase = pl.multiple_of(c * SUBLANES, SUBLANES)
            dd_ref[d, pl.ds(dbase, SUBLANES), :] = jnp.broadcast_to(jnp.exp(g_last), (SUBLANES, LANES))
        osum_ref[pl.ds(base, CHUNK), :] = o_intra
        return carry

    lax.fori_loop(0, nc, chunk_body, 0)

    def scan_body(i, carry):
        new = []
        for d in range(2):
            s = carry[d]
            c = i if d == 0 else nc - 1 - i
            base = pl.multiple_of(c * CHUNK, CHUNK)
            cbase = pl.multiple_of(c * HEAD_DIM, HEAD_DIM)
            dbase = pl.multiple_of(c * SUBLANES, SUBLANES)
            s16 = s.astype(MXU_DTYPE)
            osum_ref[pl.ds(base, CHUNK), :] += _dot(qe_ref[d, pl.ds(base, CHUNK), :], s16)
            dec = dd_ref[d, pl.ds(dbase, 1), :]
            new.append(dec * s - _dot(mp_ref[d, pl.ds(cbase, HEAD_DIM), :], s16)
                       + nn_ref[d, pl.ds(cbase, HEAD_DIM), :])
        return tuple(new)

    zero_state = jnp.zeros((HEAD_DIM, HEAD_DIM), F32)
    lax.fori_loop(0, nc, scan_body, (zero_state, zero_state))

    gain = ong_ref[...]

    def finish(o, z):
        ms = jnp.mean(o * o, axis=-1, keepdims=True)
        return (o * lax.rsqrt(ms + NORM_EPS) * gain * _silu(z)).astype(o_ref.dtype)

    first = CHUNK - padr if padr else 0
    if first:
        o_ref[0:first, :] = finish(osum_ref[padr:CHUNK, :], z_ref[0:first, :])

    def out_body(c, carry):
        base = pl.multiple_of(c * CHUNK, CHUNK)
        dst = pl.multiple_of(c * CHUNK - padr, SUBLANES)
        o_ref[pl.ds(dst, CHUNK), :] = finish(osum_ref[pl.ds(base, CHUNK), :], z_ref[pl.ds(dst, CHUNK), :])
        return carry

    lax.fori_loop(1 if first else 0, nc, out_body, 0)


def _gdn(proj3, conv_w, gate_prm, out_norm_g):
    b, seq, _ = proj3.shape
    lp = -(-seq // CHUNK) * CHUNK
    nc = lp // CHUNK
    col = lambda off: pl.BlockSpec((None, seq, LANES), lambda bi, hi: (bi, 0, off + hi))
    cw = lambda off: pl.BlockSpec((CONV_K, LANES), lambda bi, hi: (0, off + hi))
    return pl.pallas_call(
        functools.partial(_gdn_kernel, seq=seq, lp=lp),
        grid=(b, N_HEADS),
        in_specs=[
            col(COL_QKV // LANES), col(COL_QKV // LANES + N_HEADS), col(COL_QKV // LANES + 2 * N_HEADS),
            col(COL_Z // LANES),
            pl.BlockSpec((None, seq, LANES), lambda bi, hi: (bi, 0, COL_AB // LANES)),
            cw(0), cw(N_HEADS), cw(2 * N_HEADS),
            pl.BlockSpec((SUBLANES, LANES), lambda bi, hi: (0, 0)),
            pl.BlockSpec((1, LANES), lambda bi, hi: (0, 0)),
        ],
        out_specs=pl.BlockSpec((None, seq, LANES), lambda bi, hi: (bi, 0, hi)),
        out_shape=jax.ShapeDtypeStruct((b, seq, N_HEADS * HEAD_DIM), MXU_DTYPE),
        scratch_shapes=[
            pltpu.VMEM((lp + 2 * SUBLANES, LANES), F32),
            pltpu.VMEM((lp, LANES), F32),
            pltpu.VMEM((lp, LANES), F32),
            pltpu.VMEM((lp, LANES), F32),
            pltpu.VMEM((lp, 4 * LANES), F32),
            pltpu.VMEM((2, lp, LANES), MXU_DTYPE),
            pltpu.VMEM((2, nc * HEAD_DIM, LANES), MXU_DTYPE),
            pltpu.VMEM((2, nc * HEAD_DIM, LANES), F32),
            pltpu.VMEM((2, nc * SUBLANES, LANES), F32),
            pltpu.VMEM((lp, LANES), F32),
        ],
        compiler_params=_params(("arbitrary", "arbitrary")),
        name="gdn",
    )(proj3, proj3, proj3, proj3, proj3, conv_w, conv_w, conv_w, gate_prm, out_norm_g)


def _merge_kernel(h_ref, og_ref, ym_ref, gf_ref, gd_ref, wf_ref, wd_ref, wo_ref, n2_ref, wr_ref, br_ref,
                  h1_ref, hn_ref, ridx_ref, rw_ref, cnt_ref):
    @pl.when(pl.program_id(0) == 0)
    def _():
        cnt_ref[...] = jnp.zeros_like(cnt_ref)

    y_f = _dot(ym_ref[...], wf_ref[...])
    y_d = _dot(og_ref[...], wd_ref[...])
    merged = jax.nn.sigmoid(gf_ref[...]) * y_f + jax.nn.sigmoid(gd_ref[...]) * y_d
    h1 = h_ref[...] + _dot(merged.astype(MXU_DTYPE), wo_ref[...])
    h1_ref[...] = h1
    ms = jnp.mean(h1 * h1, axis=-1, keepdims=True)
    hn = h1 * lax.rsqrt(ms + NORM_EPS) * n2_ref[...]
    hn_ref[...] = hn

    x_hi = hn.astype(MXU_DTYPE)
    x_lo = (hn - x_hi.astype(F32)).astype(MXU_DTYPE)
    w = wr_ref[...]
    w_hi = w.astype(MXU_DTYPE)
    w_lo = (w - w_hi.astype(F32)).astype(MXU_DTYPE)
    logits = _dot(x_hi, w_hi) + _dot(x_hi, w_lo) + _dot(x_lo, w_hi) + br_ref[...]

    tm = logits.shape[0]
    lane = lax.broadcasted_iota(jnp.int32, (tm, LANES), 1)
    neg_inf = -jnp.inf
    is_group = lane < N_GROUPS
    gl = jnp.where(is_group, logits, neg_inf)
    ge = jnp.exp(gl - jnp.max(gl, axis=-1, keepdims=True))
    gp = ge / jnp.sum(ge, axis=-1, keepdims=True)
    p_group = jnp.max(gp, axis=-1, keepdims=True)
    g_idx = jnp.min(jnp.where(is_group & (gp == p_group), lane, LANES), axis=-1, keepdims=True)

    in_group = (lane >= N_GROUPS) & (lane < N_GROUPS + N_EXPERTS) & (((lane - N_GROUPS) >> 3) == g_idx)
    el = jnp.where(in_group, logits, neg_inf)
    ee = jnp.exp(el - jnp.max(el, axis=-1, keepdims=True))
    ep = ee / jnp.sum(ee, axis=-1, keepdims=True)
    v1 = jnp.max(jnp.where(in_group, ep, -1.0), axis=-1, keepdims=True)
    i1 = jnp.min(jnp.where(in_group & (ep == v1), lane, LANES), axis=-1, keepdims=True)
    rest = in_group & (lane != i1)
    v2 = jnp.max(jnp.where(rest, ep, -1.0), axis=-1, keepdims=True)
    i2 = jnp.min(jnp.where(rest & (ep == v2), lane, LANES), axis=-1, keepdims=True)
    denom = v1 + v2
    w1 = p_group * (v1 / denom)
    w2 = p_group * (v2 / denom)
    e1 = i1 - N_GROUPS
    e2 = i2 - N_GROUPS
    ridx_ref[...] = jnp.where(lane == 0, e1, jnp.where(lane == 1, e2, 0))
    rw_ref[...] = jnp.where(lane == 0, w1, jnp.where(lane == 1, w2, 0.0))
    onehots = jnp.where((lane == e1) | (lane == e2), 1.0, 0.0)
    cnt_ref[...] += jnp.broadcast_to(jnp.sum(onehots, axis=0, keepdims=True), cnt_ref.shape)


def _merge(h2d, og2d, ymix2d, proj2d, w_fourier, w_delta, w_out, norm2_g, w_router, b_router, tm):
    t, d = h2d.shape
    row = lambda w: pl.BlockSpec((tm, w), lambda i: (i, 0))
    const = lambda shape: pl.BlockSpec(shape, lambda i: (0, 0))
    return pl.pallas_call(
        _merge_kernel,
        grid=(t // tm,),
        in_specs=[
            row(d), row(d), row(F_WIDTH),
            pl.BlockSpec((tm, d), lambda i: (i, COL_GATE_F // d)),
            pl.BlockSpec((tm, d), lambda i: (i, COL_GATE_D // d)),
            const((F_WIDTH, d)), const((d, d)), const((d, d)), const((1, d)),
            const((d, LANES)), const((1, LANES)),
        ],
        out_specs=[row(d), row(d), row(LANES), row(LANES), const((SUBLANES, LANES))],
        out_shape=[
            jax.ShapeDtypeStruct((t, d), F32),
            jax.ShapeDtypeStruct((t, d), F32),
            jax.ShapeDtypeStruct((t, LANES), jnp.int32),
            jax.ShapeDtypeStruct((t, LANES), F32),
            jax.ShapeDtypeStruct((SUBLANES, LANES), F32),
        ],
        compiler_params=_params(("arbitrary",)),
        name="merge",
    )(h2d, og2d, ymix2d, proj2d, proj2d, w_fourier, w_delta, w_out, norm2_g, w_router, b_router)


def _plan_kernel(ridx_ref, cnt_ref, dest_ref, blk_ref, base_ref, run_ref, *, n_blk_rows):
    tt = ridx_ref.shape[0]
    lane8 = lax.broadcasted_iota(jnp.int32, (SUBLANES, LANES), 1)

    @pl.when(pl.program_id(0) == 0)
    def _():
        cnt = cnt_ref[...].astype(jnp.int32)
        padded = (((cnt + (EXPERT_BLOCK - 1)) // EXPERT_BLOCK) * EXPERT_BLOCK).astype(F32)
        r = lax.broadcasted_iota(jnp.int32, (LANES, LANES), 0)
        c = lax.broadcasted_iota(jnp.int32, (LANES, LANES), 1)
        before = jnp.where(r < c, 1.0, 0.0).astype(MXU_DTYPE)
        start = _exact_right(padded, before)
        base_ref[...] = start
        run_ref[...] = jnp.zeros_like(run_ref)
        ends = (start + padded)[0:1, :]
        blk_start = (lax.broadcasted_iota(jnp.int32, (n_blk_rows, LANES), 0) * EXPERT_BLOCK).astype(F32)
        lane = lax.broadcasted_iota(jnp.int32, (n_blk_rows, LANES), 1)
        hit = jnp.where((lane < N_EXPERTS) & (ends <= blk_start), 1.0, 0.0)
        owner = jnp.minimum(jnp.sum(hit, axis=-1, keepdims=True), float(N_EXPERTS - 1))
        n_used = jnp.max(jnp.where(lane8[0:1, :] < N_EXPERTS, ends, 0.0), axis=-1, keepdims=True) / EXPERT_BLOCK
        blk_ref[0:n_blk_rows, :] = jnp.broadcast_to(owner, (n_blk_rows, LANES)).astype(jnp.int32)
        blk_ref[n_blk_rows:, :] = jnp.broadcast_to(n_used, (SUBLANES, LANES)).astype(jnp.int32)

    lane = lax.broadcasted_iota(jnp.int32, (tt, LANES), 1)
    ridx = ridx_ref[...]
    oh0 = jnp.where(lane == ridx[:, 0:1], 1.0, 0.0)
    oh1 = jnp.where(lane == ridx[:, 1:2], 1.0, 0.0)
    r = lax.broadcasted_iota(jnp.int32, (tt, tt), 0)
    c = lax.broadcasted_iota(jnp.int32, (tt, tt), 1)
    earlier = jnp.where(c < r, 1.0, 0.0).astype(MXU_DTYPE)
    prefix = _dot(earlier, jnp.concatenate([oh0, oh1], axis=1).astype(MXU_DTYPE))
    c0 = jnp.sum(oh0, axis=0, keepdims=True)
    c1 = jnp.sum(oh1, axis=0, keepdims=True)
    first = base_ref[0:1, :] + run_ref[0:1, :]
    d0 = jnp.sum(oh0 * (prefix[:, :LANES] + first), axis=-1, keepdims=True)
    d1 = jnp.sum(oh1 * (prefix[:, LANES:] + first + c0), axis=-1, keepdims=True)
    run_ref[...] += jnp.broadcast_to(c0 + c1, run_ref.shape)
    dest_ref[...] = jnp.where(lane == 0, d0, jnp.where(lane == 1, d1, 0.0)).astype(jnp.int32)


def _plan(ridx, cnt, tt, n_blocks):
    t = ridx.shape[0]
    n_blk_rows = -(-n_blocks // SUBLANES) * SUBLANES
    return pl.pallas_call(
        functools.partial(_plan_kernel, n_blk_rows=n_blk_rows),
        grid=(t // tt,),
        in_specs=[pl.BlockSpec((tt, LANES), lambda i: (i, 0)),
                  pl.BlockSpec((SUBLANES, LANES), lambda i: (0, 0))],
        out_specs=[pl.BlockSpec((tt, LANES), lambda i: (i, 0)),
                   pl.BlockSpec((n_blk_rows + SUBLANES, LANES), lambda i: (0, 0))],
        out_shape=[jax.ShapeDtypeStruct((t, LANES), jnp.int32),
                   jax.ShapeDtypeStruct((n_blk_rows + SUBLANES, LANES), jnp.int32)],
        scratch_shapes=[pltpu.VMEM((SUBLANES, LANES), F32), pltpu.VMEM((SUBLANES, LANES), F32)],
        compiler_params=_params(("arbitrary",)),
        name="plan",
    )(ridx, cnt)


def _scatter_kernel(dest_ref, hn_ref, xb_in_ref, xb_ref, sem, *, ts):
    del xb_in_ref

    def issue(t, carry):
        src = hn_ref.at[pl.ds(t, 1)]
        for k in range(TOP_K):
            pltpu.make_async_copy(src, xb_ref.at[pl.ds(dest_ref[0, TOP_K * t + k], 1)], sem).start()
        return carry

    lax.fori_loop(0, ts, issue, 0)

    def drain(t, carry):
        src = hn_ref.at[pl.ds(t, 1)]
        for k in range(TOP_K):
            pltpu.make_async_copy(src, xb_ref.at[pl.ds(dest_ref[0, TOP_K * t + k], 1)], sem).wait()
        return carry

    lax.fori_loop(0, ts, drain, 0)


def _scatter(dest_flat, hn2d, xb_zero, ts):
    t, d = hn2d.shape
    return pl.pallas_call(
        functools.partial(_scatter_kernel, ts=ts),
        grid=(t // ts,),
        in_specs=[pl.BlockSpec((None, 1, TOP_K * ts), lambda i: (i, 0, 0), memory_space=pltpu.SMEM),
                  pl.BlockSpec((ts, d), lambda i: (i, 0)),
                  pl.BlockSpec(memory_space=pl.ANY)],
        out_specs=pl.BlockSpec(memory_space=pl.ANY),
        out_shape=jax.ShapeDtypeStruct(xb_zero.shape, F32),
        scratch_shapes=[pltpu.SemaphoreType.DMA(())],
        input_output_aliases={2: 0},
        compiler_params=_params(("arbitrary",)),
        name="scatter",
    )(dest_flat, hn2d, xb_zero)


def _expert_kernel(owner_ref, nused_ref, x_ref, wg_ref, wu_ref, wd_ref, y_ref):
    i = pl.program_id(0)

    @pl.when(i < nused_ref[0])
    def _():
        x = x_ref[...].astype(MXU_DTYPE)
        gate = _dot(x, wg_ref[...].astype(MXU_DTYPE))
        up = _dot(x, wu_ref[...].astype(MXU_DTYPE))
        y_ref[...] = _dot((_silu(gate) * up).astype(MXU_DTYPE), wd_ref[...].astype(MXU_DTYPE))

    @pl.when(i >= nused_ref[0])
    def _():
        y_ref[...] = jnp.zeros_like(y_ref)


def _experts(owner, n_used, xb, w_gate, w_up, w_down):
    n_slots, d = xb.shape
    de = w_gate.shape[-1]
    n_blocks = n_slots // EXPERT_BLOCK
    grid_spec = pltpu.PrefetchScalarGridSpec(
        num_scalar_prefetch=2,
        grid=(n_blocks,),
        in_specs=[
            pl.BlockSpec((EXPERT_BLOCK, d), lambda i, own, nu: (i, 0)),
            pl.BlockSpec((None, d, de), lambda i, own, nu: (own[i], 0, 0)),
            pl.BlockSpec((None, d, de), lambda i, own, nu: (own[i], 0, 0)),
            pl.BlockSpec((None, de, d), lambda i, own, nu: (own[i], 0, 0)),
        ],
        out_specs=pl.BlockSpec((EXPERT_BLOCK, d), lambda i, own, nu: (i, 0)),
    )
    return pl.pallas_call(
        _expert_kernel,
        grid_spec=grid_spec,
        out_shape=jax.ShapeDtypeStruct((n_slots, d), F32),
        compiler_params=_params(("arbitrary",)),
        name="experts",
    )(owner, n_used, xb, w_gate, w_up, w_down)


def _combine_kernel(dest_ref, yb_ref, h1_ref, rw_ref, fg_ref, o_ref, buf_ref, sem, *, tc):
    def issue(t, carry):
        for k in range(TOP_K):
            pltpu.make_async_copy(yb_ref.at[pl.ds(dest_ref[0, TOP_K * t + k], 1)],
                                  buf_ref.at[k, pl.ds(t, 1)], sem).start()
        return carry

    lax.fori_loop(0, tc, issue, 0)

    def drain(t, carry):
        for k in range(TOP_K):
            pltpu.make_async_copy(yb_ref.at[pl.ds(dest_ref[0, TOP_K * t + k], 1)],
                                  buf_ref.at[k, pl.ds(t, 1)], sem).wait()
        return carry

    lax.fori_loop(0, tc, drain, 0)

    rw = rw_ref[...]
    h2 = h1_ref[...] + (buf_ref[0] * rw[:, 0:1] + buf_ref[1] * rw[:, 1:2])
    ms = jnp.mean(h2 * h2, axis=-1, keepdims=True)
    o_ref[...] = h2 * lax.rsqrt(ms + NORM_EPS) * fg_ref[...]


def _combine(dest_flat, yb, h1, rw, final_g, tc):
    t, d = h1.shape
    return pl.pallas_call(
        functools.partial(_combine_kernel, tc=tc),
        grid=(t // tc,),
        in_specs=[pl.BlockSpec((None, 1, TOP_K * tc), lambda i: (i, 0, 0), memory_space=pltpu.SMEM),
                  pl.BlockSpec(memory_space=pl.ANY),
                  pl.BlockSpec((tc, d), lambda i: (i, 0)),
                  pl.BlockSpec((tc, LANES), lambda i: (i, 0)),
                  pl.BlockSpec((1, d), lambda i: (0, 0))],
        out_specs=pl.BlockSpec((tc, d), lambda i: (i, 0)),
        out_shape=jax.ShapeDtypeStruct((t, d), F32),
        scratch_shapes=[pltpu.VMEM((TOP_K, tc, d), F32), pltpu.SemaphoreType.DMA(())],
        compiler_params=_params(("arbitrary",)),
        name="combine",
    )(dest_flat, yb, h1, rw, final_g)


def _tile(total, candidates):
    for c in candidates:
        if total % c == 0:
            return c
    raise ValueError(f"no tile for {total}")


def _dft_tables(seq):
    kp = -(-seq // LANES) * LANES
    c = jnp.arange(FGROUP_DIM, dtype=jnp.int32)
    ang_c = (2.0 * math.pi / FGROUP_DIM) * ((c[:, None] * c[None, :]) % FGROUP_DIM).astype(F32)
    cs128 = jnp.concatenate([jnp.cos(ang_c), jnp.sin(ang_c)], axis=1).astype(MXU_DTYPE)
    n = jnp.arange(seq, dtype=jnp.int32)
    ang_l = (2.0 * math.pi / seq) * ((n[:, None] * n[None, :]) % seq).astype(F32)
    pad = ((0, 0), (0, kp - seq))
    twid = jnp.concatenate([jnp.pad(jnp.cos(ang_l), pad), jnp.pad(-jnp.sin(ang_l), pad)], axis=1)
    return cs128, twid.astype(MXU_DTYPE)


def kernel(x, meta_tokens, norm1_g, w_in, conv_w, a_log_fwd, dt_bias_fwd, a_log_bwd, dt_bias_bwd, out_norm_g, w_fourier, w_delta, w_out, norm2_g, w_router_group, b_router_group, w_router_expert, b_router_expert, w_gate_e, w_up_e, w_down_e, final_norm_g):
    bsz, n_real, d = x.shape
    seq = N_META + n_real
    n_tok = bsz * seq
    assert (CHUNK - N_META) % CHUNK + seq == -(-seq // CHUNK) * CHUNK, "meta tokens must fill the tail of chunk 0"
    assert w_in.shape[0] == 1, "single trunk layer"
    layer = 0

    meta = jnp.broadcast_to(meta_tokens[None].astype(x.dtype), (bsz, N_META, d))
    h = jnp.concatenate([meta, x], axis=1).reshape(n_tok, d)

    w = w_in[layer]
    o_f, o_qkv, o_z, o_ab, o_gf, o_gd = 0, F_WIDTH, F_WIDTH + 3 * QK_WIDTH, F_WIDTH + 4 * QK_WIDTH, \
        F_WIDTH + 4 * QK_WIDTH + 4 * N_HEADS, F_WIDTH + 4 * QK_WIDTH + 4 * N_HEADS + d
    w_cat = jnp.concatenate(
        [w[:, o_qkv:o_z], w[:, o_z:o_ab], w[:, o_gf:o_gd], w[:, o_gd:o_gd + d], w[:, o_f:o_qkv], w[:, o_ab:o_gf],
         jnp.zeros((d, PROJ_WIDTH - w.shape[1]), w.dtype)], axis=1).astype(MXU_DTYPE)

    proj = _in_proj(h, norm1_g[layer][None, :], w_cat, _tile(n_tok, (1032, 104, 8)))
    proj3 = proj.reshape(bsz, seq, PROJ_WIDTH)

    cs128, twid = _dft_tables(seq)
    ymix = _fnet(proj3, cs128, twid, _tile(seq, (344, 104, 8)))

    gate_prm = jnp.zeros((SUBLANES, LANES), F32)
    gate_prm = gate_prm.at[0, 0:N_HEADS].set(a_log_fwd[layer]).at[0, 2 * N_HEADS:3 * N_HEADS].set(a_log_bwd[layer])
    gate_prm = gate_prm.at[1, 0:N_HEADS].set(dt_bias_fwd[layer]).at[1, 2 * N_HEADS:3 * N_HEADS].set(dt_bias_bwd[layer])
    og = _gdn(proj3, conv_w[layer], gate_prm, out_norm_g[layer][None, :])

    w_router = jnp.zeros((d, LANES), F32)
    w_router = w_router.at[:, 0:N_GROUPS].set(w_router_group[layer])
    w_router = w_router.at[:, N_GROUPS:N_GROUPS + N_EXPERTS].set(w_router_expert[layer])
    b_router = jnp.zeros((1, LANES), F32)
    b_router = b_router.at[0, 0:N_GROUPS].set(b_router_group[layer])
    b_router = b_router.at[0, N_GROUPS:N_GROUPS + N_EXPERTS].set(b_router_expert[layer])
    tt = _tile(n_tok, (384, 128))
    h1, hn, ridx, rw, cnt = _merge(
        h, og.reshape(n_tok, d), ymix.reshape(n_tok, F_WIDTH), proj,
        w_fourier[layer].astype(MXU_DTYPE), w_delta[layer].astype(MXU_DTYPE), w_out[layer].astype(MXU_DTYPE),
        norm2_g[layer][None, :], w_router, b_router, tt)

    n_assign = n_tok * TOP_K
    n_blocks = -(-n_assign // EXPERT_BLOCK) + N_EXPERTS
    dest, blk = _plan(ridx, cnt, tt, n_blocks)
    dest_flat = dest[:, :TOP_K].reshape(n_tok // tt, 1, TOP_K * tt)
    owner = blk[:n_blocks, 0]
    n_used = blk[-1:, 0]

    xb = _scatter(dest_flat, hn, jnp.zeros((n_blocks * EXPERT_BLOCK, d), F32), tt)
    yb = _experts(owner, n_used, xb, w_gate_e[layer], w_up_e[layer], w_down_e[layer])
    out = _combine(dest_flat, yb, h1, rw, final_norm_g[None, :], tt)
    return out.reshape(bsz, seq, d)[:, N_META:]
```

```python
import functools
import math

import jax
import jax.numpy as jnp
from jax import lax
from jax.experimental import pallas as pl
from jax.experimental.pallas import tpu as pltpu

F32 = jnp.float32
MXU_DTYPE = jnp.bfloat16

N_META = 16
CHUNK = 64
N_FGROUPS = 4
FGROUP_DIM = 128
F_WIDTH = N_FGROUPS * FGROUP_DIM
N_HEADS = 8
HEAD_DIM = 128
QK_WIDTH = N_HEADS * HEAD_DIM
CONV_K = 5
N_GROUPS = 8
EXPERTS_PER_GROUP = 8
N_EXPERTS = N_GROUPS * EXPERTS_PER_GROUP
TOP_K = 2
EXPERT_BLOCK = 128
NORM_EPS = 1e-6
LANES = 128
SUBLANES = 8
VMEM_LIMIT = 52 * 1024 * 1024
GDN_GROUP = 11

COL_QKV = 0
COL_Z = 3 * QK_WIDTH
COL_GATE_F = COL_Z + QK_WIDTH
COL_GATE_D = COL_GATE_F + 1024
COL_FIN = COL_GATE_D + 1024
COL_AB = COL_FIN + F_WIDTH
PROJ_TN = 768
PROJ_WIDTH = -(-(COL_AB + LANES) // PROJ_TN) * PROJ_TN


def _dot(a, b):
    return jnp.dot(a, b, preferred_element_type=F32)


def _dot_nt(a, b):
    return lax.dot_general(a, b, (((1,), (1,)), ((), ())), preferred_element_type=F32)


def _dot_tn(a, b):
    return lax.dot_general(a, b, (((0,), (0,)), ((), ())), preferred_element_type=F32)


def _split3(x):
    hi = x.astype(MXU_DTYPE)
    r1 = x - hi.astype(F32)
    mid = r1.astype(MXU_DTYPE)
    lo = (r1 - mid.astype(F32)).astype(MXU_DTYPE)
    return hi, mid, lo


def _exact_left(sel, x):
    hi, mid, lo = _split3(x)
    return _dot(sel, hi) + _dot(sel, mid) + _dot(sel, lo)


def _exact_right(x, sel):
    hi, mid, lo = _split3(x)
    return _dot(hi, sel) + _dot(mid, sel) + _dot(lo, sel)


def _silu(x):
    return x * jax.nn.sigmoid(x)


def _params(sem):
    return pltpu.CompilerParams(dimension_semantics=sem, vmem_limit_bytes=VMEM_LIMIT)


def _inproj_kernel(h_ref, g_ref, w_ref, o_ref, hn_ref):
    @pl.when(pl.program_id(1) == 0)
    def _():
        x = h_ref[...]
        ms = jnp.mean(x * x, axis=-1, keepdims=True)
        hn_ref[...] = (x * lax.rsqrt(ms + NORM_EPS) * g_ref[...]).astype(MXU_DTYPE)

    o_ref[...] = _dot(hn_ref[...], w_ref[...])


def _in_proj(h2d, gain, w_cat, tm):
    t, d = h2d.shape
    n = w_cat.shape[1]
    return pl.pallas_call(
        _inproj_kernel,
        grid=(t // tm, n // PROJ_TN),
        in_specs=[
            pl.BlockSpec((tm, d), lambda i, j: (i, 0)),
            pl.BlockSpec((1, d), lambda i, j: (0, 0)),
            pl.BlockSpec((d, PROJ_TN), lambda i, j: (0, j)),
        ],
        out_specs=pl.BlockSpec((tm, PROJ_TN), lambda i, j: (i, j)),
        out_shape=jax.ShapeDtypeStruct((t, n), F32),
        scratch_shapes=[pltpu.VMEM((tm, d), MXU_DTYPE)],
        compiler_params=_params(("arbitrary", "arbitrary")),
        name="in_proj",
    )(h2d, gain, w_cat)


def _fnet_kernel(x_ref, cs_ref, tw_ref, o_ref, r_ref, *, seq, kp, scale):
    @pl.when(pl.program_id(1) == 0)
    def _():
        tail = kp - seq
        if tail:
            r_ref[seq:kp, :] = jnp.zeros((tail, F_WIDTH), MXU_DTYPE)
            r_ref[kp + seq:, :] = jnp.zeros((tail, F_WIDTH), MXU_DTYPE)
        cs = cs_ref[...]
        for g in range(N_FGROUPS):
            cols = slice(g * FGROUP_DIM, (g + 1) * FGROUP_DIM)
            p = _dot(x_ref[:, cols].astype(MXU_DTYPE), cs)
            r_ref[0:seq, cols] = p[:, :FGROUP_DIM].astype(MXU_DTYPE)
            r_ref[kp:kp + seq, cols] = p[:, FGROUP_DIM:].astype(MXU_DTYPE)

    o_ref[...] = (_dot(tw_ref[...], r_ref[...]) * scale).astype(o_ref.dtype)


def _fnet(proj3, cs128, twid, tmf):
    b, seq, _ = proj3.shape
    kp = twid.shape[1] // 2
    scale = 1.0 / math.sqrt(seq * FGROUP_DIM)
    return pl.pallas_call(
        functools.partial(_fnet_kernel, seq=seq, kp=kp, scale=scale),
        grid=(b, seq // tmf),
        in_specs=[
            pl.BlockSpec((None, seq, F_WIDTH), lambda bi, i: (bi, 0, COL_FIN // F_WIDTH)),
            pl.BlockSpec((FGROUP_DIM, 2 * FGROUP_DIM), lambda bi, i: (0, 0)),
            pl.BlockSpec((tmf, 2 * kp), lambda bi, i: (i, 0)),
        ],
        out_specs=pl.BlockSpec((None, tmf, F_WIDTH), lambda bi, i: (bi, i, 0)),
        out_shape=jax.ShapeDtypeStruct((b, seq, F_WIDTH), MXU_DTYPE),
        scratch_shapes=[pltpu.VMEM((2 * kp, F_WIDTH), MXU_DTYPE)],
        compiler_params=_params(("arbitrary", "arbitrary")),
        name="fnet",
    )(proj3, cs128, twid)


def _aligned(x, m):
    return x if isinstance(x, int) else pl.multiple_of(x, m)


def _gdn_kernel(q_ref, k_ref, v_ref, z_ref, ab_ref, cwq_ref, cwk_ref, cwv_ref, prm_ref, ong_ref,
                o_ref,
                rawq_ref, rawk_ref, rawv_ref, qs_ref, ks_ref, vs_ref, gsel_ref, grow_ref,
                qe_ref, mp_ref, nn_ref, dd_ref, osum_ref,
                *, seq, lp, group):
    head = pl.program_id(1)
    nc = lp // CHUNK
    padr = lp - seq
    halo = SUBLANES
    row64 = lax.broadcasted_iota(jnp.int32, (CHUNK, LANES), 0)
    lane64 = lax.broadcasted_iota(jnp.int32, (CHUNK, LANES), 1)

    def for_groups(fn):
        n_full, rem = divmod(nc, group)
        if n_full:
            def body(i, carry):
                fn([i * group + j for j in range(group)])
                return carry
            lax.fori_loop(0, n_full, body, 0)
        if rem:
            fn([n_full * group + j for j in range(rem)])

    streams = ((rawq_ref, q_ref, cwq_ref, qs_ref, HEAD_DIM ** -0.5),
               (rawk_ref, k_ref, cwk_ref, ks_ref, 1.0),
               (rawv_ref, v_ref, cwv_ref, vs_ref, None))
    for raw_ref, src_ref, _, _, _ in streams:
        raw_ref[0:padr + halo, :] = jnp.zeros((padr + halo, LANES), F32)
        raw_ref[padr + halo:padr + halo + seq, :] = src_ref[...]
        raw_ref[lp + halo:lp + 2 * halo, :] = jnp.zeros((halo, LANES), F32)

    def conv_body(c, carry):
        base = pl.multiple_of(c * CHUNK, CHUNK)
        for raw_ref, _, cw_ref, dst_ref, l2_scale in streams:
            x = raw_ref[pl.ds(base, CHUNK + 2 * halo), :]
            cw = cw_ref[...]
            acc = jnp.zeros((CHUNK, LANES), F32)
            for j in range(CONV_K):
                off = halo - (CONV_K - 1) // 2 + j
                acc = acc + cw[j:j + 1, :] * x[off:off + CHUNK, :]
            y = _silu(acc)
            y = jnp.where(row64 + base >= padr, y, 0.0)
            if l2_scale is not None:
                y = y * lax.rsqrt(jnp.sum(y * y, axis=-1, keepdims=True) + NORM_EPS) * l2_scale
            dst_ref[pl.ds(base, CHUNK), :] = y
        return carry

    lax.fori_loop(0, nc, conv_body, 0, unroll=3)

    raw_ref = rawq_ref
    raw_ref[0:padr, :] = jnp.zeros((padr, LANES), F32)
    raw_ref[padr:padr + seq, :] = ab_ref[...]
    neg_a = -jnp.exp(prm_ref[0:1, :])
    dt_bias = prm_ref[1:2, :]
    is_g = (lane64 < 8) | ((lane64 >= 16) & (lane64 < 24))
    is_beta = ((lane64 >= 8) & (lane64 < 16)) | ((lane64 >= 24) & (lane64 < 32))
    r128 = lax.broadcasted_iota(jnp.int32, (2 * CHUNK, CHUNK), 0)
    c128 = lax.broadcasted_iota(jnp.int32, (2 * CHUNK, CHUNK), 1)
    tri2 = jnp.where(((r128 < CHUNK) & (r128 >= c128)) | ((r128 >= CHUNK) & (r128 - CHUNK <= c128)),
                     1.0, 0.0).astype(MXU_DTYPE)
    sel_r = lax.broadcasted_iota(jnp.int32, (LANES, 4 * LANES), 0)
    sel_c = lax.broadcasted_iota(jnp.int32, (LANES, 4 * LANES), 1)
    sel = jnp.where(sel_r == head + 8 * (sel_c // LANES), 1.0, 0.0).astype(MXU_DTYPE)

    def gate_group(cs):
        bases = [_aligned(c * CHUNK, CHUNK) for c in cs]
        gates, parts = [], []
        for base in bases:
            ab = raw_ref[pl.ds(base, CHUNK), :]
            g = neg_a * jax.nn.softplus(ab + dt_bias)
            beta = jax.nn.sigmoid(ab)
            gt = jnp.where(is_g, g, jnp.where(is_beta, beta, 0.0))
            gt = jnp.where(row64 + base >= padr, gt, 0.0)
            gates.append(gt)
            parts.append(_split3(gt))
        cums = [_dot(tri2, p[0]) + _dot(tri2, p[1]) + _dot(tri2, p[2]) for p in parts]
        mixed = [jnp.where(lane64 < 8, cm[:CHUNK],
                           jnp.where((lane64 >= 16) & (lane64 < 24), cm[CHUNK:], gt))
                 for cm, gt in zip(cums, gates)]
        parts = [_split3(m) for m in mixed]
        for base, p in zip(bases, parts):
            gsel_ref[pl.ds(base, CHUNK), :] = _dot(p[0], sel) + _dot(p[1], sel) + _dot(p[2], sel)
        for c, m in zip(cs, mixed):
            cbase = _aligned(c * LANES, LANES)
            grow_ref[pl.ds(cbase, LANES), :] = jnp.concatenate([m, m], axis=0).T
        return None

    for_groups(gate_group)

    fwd_half = lane64 < CHUNK
    col64 = lane64 & (CHUNK - 1)
    incl2 = (fwd_half & (row64 >= col64)) | (~fwd_half & (row64 <= col64))
    strict2 = (fwd_half & (row64 > col64)) | (~fwd_half & (row64 < col64))
    eye2 = jnp.where(row64 == col64, 1.0, 0.0).astype(F32)

    def blockdiag(p):
        return jnp.concatenate([jnp.where(fwd_half, p, 0.0), jnp.where(fwd_half, 0.0, p)],
                               axis=0).astype(MXU_DTYPE)

    def chunk_group(cs):
        n = len(cs)
        bases = [_aligned(c * CHUNK, CHUNK) for c in cs]
        cb128 = [_aligned(c * LANES, LANES) for c in cs]
        cb8 = [_aligned(c * SUBLANES, SUBLANES) for c in cs]

        def gate_cols(j):
            gs = gsel_ref[pl.ds(bases[j], CHUNK), :]
            return gs[:, 0:LANES], gs[:, LANES:2 * LANES], gs[:, 2 * LANES:3 * LANES], gs[:, 3 * LANES:]

        kq = []
        for j in range(n):
            k16 = ks_ref[pl.ds(bases[j], CHUNK), :].astype(MXU_DTYPE)
            q16 = qs_ref[pl.ds(bases[j], CHUNK), :].astype(MXU_DTYPE)
            kq.append(_dot_nt(jnp.concatenate([k16, q16], axis=0), jnp.concatenate([k16, k16], axis=0)))

        ts, ps, qkds = [], [], []
        for j in range(n):
            gf, bf, gb, bb = gate_cols(j)
            row_f = grow_ref[pl.ds(cb128[j] + head, 1), :]
            row_b = grow_ref[pl.ds(cb128[j] + 2 * N_HEADS + head, 1), :]
            g_col = jnp.where(fwd_half, gf, gb)
            g_row = jnp.where(fwd_half, row_f, row_b)
            decay = jnp.where(incl2, jnp.exp(jnp.where(incl2, g_col - g_row, 0.0)), 0.0)
            a2 = jnp.where(strict2, kq[j][:CHUNK] * jnp.where(fwd_half, bf, bb) * decay, 0.0)
            qkds.append(jnp.where(incl2, kq[j][CHUNK:] * decay, 0.0))
            ts.append(eye2 - a2)
            ps.append(a2)

        ps = [_dot(p.astype(MXU_DTYPE), blockdiag(p)) for p in ps]
        for _ in range(int(math.log2(CHUNK)) - 2):
            tps = [_dot(jnp.concatenate([t, p], axis=0).astype(MXU_DTYPE), blockdiag(p)) for t, p in zip(ts, ps)]
            ts = [t + tp[:CHUNK] for t, tp in zip(ts, tps)]
            ps = [tp[CHUNK:] for tp in tps]
        ts = [t + _dot(t.astype(MXU_DTYPE), blockdiag(p)) for t, p in zip(ts, ps)]

        uws = []
        for j in range(n):
            gf, bf, gb, bb = gate_cols(j)
            k = ks_ref[pl.ds(bases[j], CHUNK), :]
            v = vs_ref[pl.ds(bases[j], CHUNK), :]
            x_f = jnp.concatenate([v * bf, k * bf * jnp.exp(gf)], axis=1)
            x_b = jnp.concatenate([v * bb, k * bb * jnp.exp(gb)], axis=1)
            uw = _dot(blockdiag(ts[j]), jnp.concatenate([x_f, x_b], axis=0).astype(MXU_DTYPE))
            uws.append(uw.astype(MXU_DTYPE))

        xxs = [_dot(blockdiag(qkds[j]), uws[j]) for j in range(n)]
        for j in range(n):
            gf, _, gb, _ = gate_cols(j)
            k = ks_ref[pl.ds(bases[j], CHUNK), :]
            q = qs_ref[pl.ds(bases[j], CHUNK), :]
            xx = xxs[j]
            osum_ref[pl.ds(bases[j], CHUNK), :] = xx[:CHUNK, :LANES] + xx[CHUNK:, :LANES]
            for d, gc in enumerate((gf, gb)):
                rows = slice(d * CHUNK, (d + 1) * CHUNK)
                g_last = gc[CHUNK - 1:CHUNK, :] if d == 0 else gc[0:1, :]
                k_dec = k * jnp.exp(g_last - gc)
                mn = _dot_tn(k_dec.astype(MXU_DTYPE), uws[j][rows])
                qe_ref[d, pl.ds(bases[j], CHUNK), :] = (q * jnp.exp(gc) - xx[rows, LANES:]).astype(MXU_DTYPE)
                nn_ref[d, pl.ds(cb128[j], HEAD_DIM), :] = mn[:, :LANES]
                mp_ref[d, pl.ds(cb128[j], HEAD_DIM), :] = mn[:, LANES:].astype(MXU_DTYPE)
                dd_ref[d, pl.ds(cb8[j], SUBLANES), :] = jnp.broadcast_to(jnp.exp(g_last), (SUBLANES, LANES))
        return None

    for_groups(chunk_group)

    def scan_body(i, carry):
        new = []
        for d in range(2):
            s = carry[d]
            c = i if d == 0 else nc - 1 - i
            base = pl.multiple_of(c * CHUNK, CHUNK)
            cbase = pl.multiple_of(c * HEAD_DIM, HEAD_DIM)
            dbase = pl.multiple_of(c * SUBLANES, SUBLANES)
            s16 = s.astype(MXU_DTYPE)
            osum_ref[pl.ds(base, CHUNK), :] += _dot(qe_ref[d, pl.ds(base, CHUNK), :], s16)
            dec = dd_ref[d, pl.ds(dbase, 1), :]
            new.append(dec * s - _dot(mp_ref[d, pl.ds(cbase, HEAD_DIM), :], s16)
                       + nn_ref[d, pl.ds(cbase, HEAD_DIM), :])
        return tuple(new)

    zero_state = jnp.zeros((HEAD_DIM, HEAD_DIM), F32)
    lax.fori_loop(0, nc, scan_body, (zero_state, zero_state))

    gain = ong_ref[...]

    def finish(o, z):
        ms = jnp.mean(o * o, axis=-1, keepdims=True)
        return (o * lax.rsqrt(ms + NORM_EPS) * gain * _silu(z)).astype(o_ref.dtype)

    first = CHUNK - padr if padr else 0
    if first:
        o_ref[0:first, :] = finish(osum_ref[padr:CHUNK, :], z_ref[0:first, :])

    def out_body(c, carry):
        base = pl.multiple_of(c * CHUNK, CHUNK)
        dst = pl.multiple_of(c * CHUNK - padr, SUBLANES)
        o_ref[pl.ds(dst, CHUNK), :] = finish(osum_ref[pl.ds(base, CHUNK), :], z_ref[pl.ds(dst, CHUNK), :])
        return carry

    lax.fori_loop(1 if first else 0, nc, out_body, 0)


def _gdn(proj3, conv_w, gate_prm, out_norm_g):
    b, seq, _ = proj3.shape
    lp = -(-seq // CHUNK) * CHUNK
    nc = lp // CHUNK
    col = lambda off: pl.BlockSpec((None, seq, LANES), lambda bi, hi: (bi, 0, off + hi))
    cw = lambda off: pl.BlockSpec((CONV_K, LANES), lambda bi, hi: (0, off + hi))
    return pl.pallas_call(
        functools.partial(_gdn_kernel, seq=seq, lp=lp, group=GDN_GROUP),
        grid=(b, N_HEADS),
        in_specs=[
            col(COL_QKV // LANES), col(COL_QKV // LANES + N_HEADS), col(COL_QKV // LANES + 2 * N_HEADS),
            col(COL_Z // LANES),
            pl.BlockSpec((None, seq, LANES), lambda bi, hi: (bi, 0, COL_AB // LANES)),
            cw(0), cw(N_HEADS), cw(2 * N_HEADS),
            pl.BlockSpec((SUBLANES, LANES), lambda bi, hi: (0, 0)),
            pl.BlockSpec((1, LANES), lambda bi, hi: (0, 0)),
        ],
        out_specs=pl.BlockSpec((None, seq, LANES), lambda bi, hi: (bi, 0, hi)),
        out_shape=jax.ShapeDtypeStruct((b, seq, N_HEADS * HEAD_DIM), MXU_DTYPE),
        scratch_shapes=[
            pltpu.VMEM((lp + 2 * SUBLANES, LANES), F32),
            pltpu.VMEM((lp + 2 * SUBLANES, LANES), F32),
            pltpu.VMEM((lp + 2 * SUBLANES, LANES), F32),
            pltpu.VMEM((lp, LANES), F32),
            pltpu.VMEM((lp, LANES), F32),
            pltpu.VMEM((lp, LANES), F32),
            pltpu.VMEM((lp, 4 * LANES), F32),
            pltpu.VMEM((nc * LANES, LANES), F32),
            pltpu.VMEM((2, lp, LANES), MXU_DTYPE),
            pltpu.VMEM((2, nc * HEAD_DIM, LANES), MXU_DTYPE),
            pltpu.VMEM((2, nc * HEAD_DIM, LANES), F32),
            pltpu.VMEM((2, nc * SUBLANES, LANES), F32),
            pltpu.VMEM((lp, LANES), F32),
        ],
        compiler_params=_params(("arbitrary", "arbitrary")),
        name="gdn",
    )(proj3, proj3, proj3, proj3, proj3, conv_w, conv_w, conv_w, gate_prm, out_norm_g)


def _merge_kernel(h_ref, og_ref, ym_ref, gf_ref, gd_ref, wf_ref, wd_ref, wo_ref, n2_ref, wr_ref, br_ref,
                  h1_ref, hn_ref, ridx_ref, rw_ref, cnt_ref):
    @pl.when(pl.program_id(0) == 0)
    def _():
        cnt_ref[...] = jnp.zeros_like(cnt_ref)

    y_f = _dot(ym_ref[...], wf_ref[...])
    y_d = _dot(og_ref[...], wd_ref[...])
    merged = jax.nn.sigmoid(gf_ref[...]) * y_f + jax.nn.sigmoid(gd_ref[...]) * y_d
    h1 = h_ref[...] + _dot(merged.astype(MXU_DTYPE), wo_ref[...])
    h1_ref[...] = h1
    ms = jnp.mean(h1 * h1, axis=-1, keepdims=True)
    hn = h1 * lax.rsqrt(ms + NORM_EPS) * n2_ref[...]
    hn_ref[...] = hn

    x_hi = hn.astype(MXU_DTYPE)
    x_lo = (hn - x_hi.astype(F32)).astype(MXU_DTYPE)
    w = wr_ref[...]
    w_hi = w.astype(MXU_DTYPE)
    w_lo = (w - w_hi.astype(F32)).astype(MXU_DTYPE)
    logits = _dot(x_hi, w_hi) + _dot(x_hi, w_lo) + _dot(x_lo, w_hi) + br_ref[...]

    tm = logits.shape[0]
    lane = lax.broadcasted_iota(jnp.int32, (tm, LANES), 1)
    neg_inf = -jnp.inf
    is_group = lane < N_GROUPS
    gl = jnp.where(is_group, logits, neg_inf)
    ge = jnp.exp(gl - jnp.max(gl, axis=-1, keepdims=True))
    gp = ge / jnp.sum(ge, axis=-1, keepdims=True)
    p_group = jnp.max(gp, axis=-1, keepdims=True)
    g_idx = jnp.min(jnp.where(is_group & (gp == p_group), lane, LANES), axis=-1, keepdims=True)

    in_group = (lane >= N_GROUPS) & (lane < N_GROUPS + N_EXPERTS) & (((lane - N_GROUPS) >> 3) == g_idx)
    el = jnp.where(in_group, logits, neg_inf)
    ee = jnp.exp(el - jnp.max(el, axis=-1, keepdims=True))
    ep = ee / jnp.sum(ee, axis=-1, keepdims=True)
    v1 = jnp.max(jnp.where(in_group, ep, -1.0), axis=-1, keepdims=True)
    i1 = jnp.min(jnp.where(in_group & (ep == v1), lane, LANES), axis=-1, keepdims=True)
    rest = in_group & (lane != i1)
    v2 = jnp.max(jnp.where(rest, ep, -1.0), axis=-1, keepdims=True)
    i2 = jnp.min(jnp.where(rest & (ep == v2), lane, LANES), axis=-1, keepdims=True)
    denom = v1 + v2
    w1 = p_group * (v1 / denom)
    w2 = p_group * (v2 / denom)
    e1 = i1 - N_GROUPS
    e2 = i2 - N_GROUPS
    ridx_ref[...] = jnp.where(lane == 0, e1, jnp.where(lane == 1, e2, 0))
    rw_ref[...] = jnp.where(lane == 0, w1, jnp.where(lane == 1, w2, 0.0))
    onehots = jnp.where((lane == e1) | (lane == e2), 1.0, 0.0)
    cnt_ref[...] += jnp.broadcast_to(jnp.sum(onehots, axis=0, keepdims=True), cnt_ref.shape)


def _merge(h2d, og2d, ymix2d, proj2d, w_fourier, w_delta, w_out, norm2_g, w_router, b_router, tm):
    t, d = h2d.shape
    row = lambda w: pl.BlockSpec((tm, w), lambda i: (i, 0))
    const = lambda shape: pl.BlockSpec(shape, lambda i: (0, 0))
    return pl.pallas_call(
        _merge_kernel,
        grid=(t // tm,),
        in_specs=[
            row(d), row(d), row(F_WIDTH),
            pl.BlockSpec((tm, d), lambda i: (i, COL_GATE_F // d)),
            pl.BlockSpec((tm, d), lambda i: (i, COL_GATE_D // d)),
            const((F_WIDTH, d)), const((d, d)), const((d, d)), const((1, d)),
            const((d, LANES)), const((1, LANES)),
        ],
        out_specs=[row(d), row(d), row(LANES), row(LANES), const((SUBLANES, LANES))],
        out_shape=[
            jax.ShapeDtypeStruct((t, d), F32),
            jax.ShapeDtypeStruct((t, d), F32),
            jax.ShapeDtypeStruct((t, LANES), jnp.int32),
            jax.ShapeDtypeStruct((t, LANES), F32),
            jax.ShapeDtypeStruct((SUBLANES, LANES), F32),
        ],
        compiler_params=_params(("arbitrary",)),
        name="merge",
    )(h2d, og2d, ymix2d, proj2d, proj2d, w_fourier, w_delta, w_out, norm2_g, w_router, b_router)


def _plan_kernel(ridx_ref, cnt_ref, dest_ref, blk_ref, base_ref, run_ref, *, n_blk_rows):
    tt = ridx_ref.shape[0]
    lane8 = lax.broadcasted_iota(jnp.int32, (SUBLANES, LANES), 1)

    @pl.when(pl.program_id(0) == 0)
    def _():
        cnt = cnt_ref[...].astype(jnp.int32)
        padded = (((cnt + (EXPERT_BLOCK - 1)) // EXPERT_BLOCK) * EXPERT_BLOCK).astype(F32)
        r = lax.broadcasted_iota(jnp.int32, (LANES, LANES), 0)
        c = lax.broadcasted_iota(jnp.int32, (LANES, LANES), 1)
        before = jnp.where(r < c, 1.0, 0.0).astype(MXU_DTYPE)
        start = _exact_right(padded, before)
        base_ref[...] = start
        run_ref[...] = jnp.zeros_like(run_ref)
        ends = (start + padded)[0:1, :]
        blk_start = (lax.broadcasted_iota(jnp.int32, (n_blk_rows, LANES), 0) * EXPERT_BLOCK).astype(F32)
        lane = lax.broadcasted_iota(jnp.int32, (n_blk_rows, LANES), 1)
        hit = jnp.where((lane < N_EXPERTS) & (ends <= blk_start), 1.0, 0.0)
        owner = jnp.minimum(jnp.sum(hit, axis=-1, keepdims=True), float(N_EXPERTS - 1))
        n_used = jnp.max(jnp.where(lane8[0:1, :] < N_EXPERTS, ends, 0.0), axis=-1, keepdims=True) / EXPERT_BLOCK
        blk_ref[0:n_blk_rows, :] = jnp.broadcast_to(owner, (n_blk_rows, LANES)).astype(jnp.int32)
        blk_ref[n_blk_rows:, :] = jnp.broadcast_to(n_used, (SUBLANES, LANES)).astype(jnp.int32)

    lane = lax.broadcasted_iota(jnp.int32, (tt, LANES), 1)
    ridx = ridx_ref[...]
    oh0 = jnp.where(lane == ridx[:, 0:1], 1.0, 0.0)
    oh1 = jnp.where(lane == ridx[:, 1:2], 1.0, 0.0)
    r = lax.broadcasted_iota(jnp.int32, (tt, tt), 0)
    c = lax.broadcasted_iota(jnp.int32, (tt, tt), 1)
    earlier = jnp.where(c < r, 1.0, 0.0).astype(MXU_DTYPE)
    prefix = _dot(earlier, jnp.concatenate([oh0, oh1], axis=1).astype(MXU_DTYPE))
    c0 = jnp.sum(oh0, axis=0, keepdims=True)
    c1 = jnp.sum(oh1, axis=0, keepdims=True)
    first = base_ref[0:1, :] + run_ref[0:1, :]
    d0 = jnp.sum(oh0 * (prefix[:, :LANES] + first), axis=-1, keepdims=True)
    d1 = jnp.sum(oh1 * (prefix[:, LANES:] + first + c0), axis=-1, keepdims=True)
    run_ref[...] += jnp.broadcast_to(c0 + c1, run_ref.shape)
    dest_ref[...] = jnp.where(lane == 0, d0, jnp.where(lane == 1, d1, 0.0)).astype(jnp.int32)


def _plan(ridx, cnt, tt, n_blocks):
    t = ridx.shape[0]
    n_blk_rows = -(-n_blocks // SUBLANES) * SUBLANES
    return pl.pallas_call(
        functools.partial(_plan_kernel, n_blk_rows=n_blk_rows),
        grid=(t // tt,),
        in_specs=[pl.BlockSpec((tt, LANES), lambda i: (i, 0)),
                  pl.BlockSpec((SUBLANES, LANES), lambda i: (0, 0))],
        out_specs=[pl.BlockSpec((tt, LANES), lambda i: (i, 0)),
                   pl.BlockSpec((n_blk_rows + SUBLANES, LANES), lambda i: (0, 0))],
        out_shape=[jax.ShapeDtypeStruct((t, LANES), jnp.int32),
                   jax.ShapeDtypeStruct((n_blk_rows + SUBLANES, LANES), jnp.int32)],
        scratch_shapes=[pltpu.VMEM((SUBLANES, LANES), F32), pltpu.VMEM((SUBLANES, LANES), F32)],
        compiler_params=_params(("arbitrary",)),
        name="plan",
    )(ridx, cnt)


def _scatter_kernel(dest_ref, hn_ref, xb_in_ref, xb_ref, sem, *, ts):
    del xb_in_ref

    def issue(t, carry):
        src = hn_ref.at[pl.ds(t, 1)]
        for k in range(TOP_K):
            pltpu.make_async_copy(src, xb_ref.at[pl.ds(dest_ref[0, TOP_K * t + k], 1)], sem).start()
        return carry

    lax.fori_loop(0, ts, issue, 0)

    def drain(t, carry):
        src = hn_ref.at[pl.ds(t, 1)]
        for k in range(TOP_K):
            pltpu.make_async_copy(src, xb_ref.at[pl.ds(dest_ref[0, TOP_K * t + k], 1)], sem).wait()
        return carry

    lax.fori_loop(0, ts, drain, 0)


def _scatter(dest_flat, hn2d, xb_zero, ts):
    t, d = hn2d.shape
    return pl.pallas_call(
        functools.partial(_scatter_kernel, ts=ts),
        grid=(t // ts,),
        in_specs=[pl.BlockSpec((None, 1, TOP_K * ts), lambda i: (i, 0, 0), memory_space=pltpu.SMEM),
                  pl.BlockSpec((ts, d), lambda i: (i, 0)),
                  pl.BlockSpec(memory_space=pl.ANY)],
        out_specs=pl.BlockSpec(memory_space=pl.ANY),
        out_shape=jax.ShapeDtypeStruct(xb_zero.shape, F32),
        scratch_shapes=[pltpu.SemaphoreType.DMA(())],
        input_output_aliases={2: 0},
        compiler_params=_params(("arbitrary",)),
        name="scatter",
    )(dest_flat, hn2d, xb_zero)


def _expert_kernel(owner_ref, nused_ref, x_ref, wg_ref, wu_ref, wd_ref, y_ref):
    i = pl.program_id(0)

    @pl.when(i < nused_ref[0])
    def _():
        x = x_ref[...].astype(MXU_DTYPE)
        gate = _dot(x, wg_ref[...].astype(MXU_DTYPE))
        up = _dot(x, wu_ref[...].astype(MXU_DTYPE))
        y_ref[...] = _dot((_silu(gate) * up).astype(MXU_DTYPE), wd_ref[...].astype(MXU_DTYPE))

    @pl.when(i >= nused_ref[0])
    def _():
        y_ref[...] = jnp.zeros_like(y_ref)


def _experts(owner, n_used, xb, w_gate, w_up, w_down):
    n_slots, d = xb.shape
    de = w_gate.shape[-1]
    n_blocks = n_slots // EXPERT_BLOCK
    grid_spec = pltpu.PrefetchScalarGridSpec(
        num_scalar_prefetch=2,
        grid=(n_blocks,),
        in_specs=[
            pl.BlockSpec((EXPERT_BLOCK, d), lambda i, own, nu: (i, 0)),
            pl.BlockSpec((None, d, de), lambda i, own, nu: (own[i], 0, 0)),
            pl.BlockSpec((None, d, de), lambda i, own, nu: (own[i], 0, 0)),
            pl.BlockSpec((None, de, d), lambda i, own, nu: (own[i], 0, 0)),
        ],
        out_specs=pl.BlockSpec((EXPERT_BLOCK, d), lambda i, own, nu: (i, 0)),
    )
    return pl.pallas_call(
        _expert_kernel,
        grid_spec=grid_spec,
        out_shape=jax.ShapeDtypeStruct((n_slots, d), F32),
        compiler_params=_params(("arbitrary",)),
        name="experts",
    )(owner, n_used, xb, w_gate, w_up, w_down)


def _combine_kernel(dest_ref, yb_ref, h1_ref, rw_ref, fg_ref, o_ref, buf_ref, sem, *, tc):
    def issue(t, carry):
        for k in range(TOP_K):
            pltpu.make_async_copy(yb_ref.at[pl.ds(dest_ref[0, TOP_K * t + k], 1)],
                                  buf_ref.at[k, pl.ds(t, 1)], sem).start()
        return carry

    lax.fori_loop(0, tc, issue, 0)

    def drain(t, carry):
        for k in range(TOP_K):
            pltpu.make_async_copy(yb_ref.at[pl.ds(dest_ref[0, TOP_K * t + k], 1)],
                                  buf_ref.at[k, pl.ds(t, 1)], sem).wait()
        return carry

    lax.fori_loop(0, tc, drain, 0)

    rw = rw_ref[...]
    h2 = h1_ref[...] + (buf_ref[0] * rw[:, 0:1] + buf_ref[1] * rw[:, 1:2])
    ms = jnp.mean(h2 * h2, axis=-1, keepdims=True)
    o_ref[...] = h2 * lax.rsqrt(ms + NORM_EPS) * fg_ref[...]


def _combine(dest_flat, yb, h1, rw, final_g, tc):
    t, d = h1.shape
    return pl.pallas_call(
        functools.partial(_combine_kernel, tc=tc),
        grid=(t // tc,),
        in_specs=[pl.BlockSpec((None, 1, TOP_K * tc), lambda i: (i, 0, 0), memory_space=pltpu.SMEM),
                  pl.BlockSpec(memory_space=pl.ANY),
                  pl.BlockSpec((tc, d), lambda i: (i, 0)),
                  pl.BlockSpec((tc, LANES), lambda i: (i, 0)),
                  pl.BlockSpec((1, d), lambda i: (0, 0))],
        out_specs=pl.BlockSpec((tc, d), lambda i: (i, 0)),
        out_shape=jax.ShapeDtypeStruct((t, d), F32),
        scratch_shapes=[pltpu.VMEM((TOP_K, tc, d), F32), pltpu.SemaphoreType.DMA(())],
        compiler_params=_params(("arbitrary",)),
        name="combine",
    )(dest_flat, yb, h1, rw, final_g)


def _tile(total, candidates):
    for c in candidates:
        if total % c == 0:
            return c
    raise ValueError(f"no tile for {total}")


def _dft_tables(seq):
    kp = -(-seq // LANES) * LANES
    c = jnp.arange(FGROUP_DIM, dtype=jnp.int32)
    ang_c = (2.0 * math.pi / FGROUP_DIM) * ((c[:, None] * c[None, :]) % FGROUP_DIM).astype(F32)
    cs128 = jnp.concatenate([jnp.cos(ang_c), jnp.sin(ang_c)], axis=1).astype(MXU_DTYPE)
    n = jnp.arange(seq, dtype=jnp.int32)
    ang_l = (2.0 * math.pi / seq) * ((n[:, None] * n[None, :]) % seq).astype(F32)
    pad = ((0, 0), (0, kp - seq))
    twid = jnp.concatenate([jnp.pad(jnp.cos(ang_l), pad), jnp.pad(-jnp.sin(ang_l), pad)], axis=1)
    return cs128, twid.astype(MXU_DTYPE)


def kernel(x, meta_tokens, norm1_g, w_in, conv_w, a_log_fwd, dt_bias_fwd, a_log_bwd, dt_bias_bwd, out_norm_g, w_fourier, w_delta, w_out, norm2_g, w_router_group, b_router_group, w_router_expert, b_router_expert, w_gate_e, w_up_e, w_down_e, final_norm_g):
    bsz, n_real, d = x.shape
    seq = N_META + n_real
    n_tok = bsz * seq
    assert (CHUNK - N_META) % CHUNK + seq == -(-seq // CHUNK) * CHUNK, "meta tokens must fill the tail of chunk 0"
    assert w_in.shape[0] == 1, "single trunk layer"
    layer = 0

    meta = jnp.broadcast_to(meta_tokens[None].astype(x.dtype), (bsz, N_META, d))
    h = jnp.concatenate([meta, x], axis=1).reshape(n_tok, d)

    w = w_in[layer]
    o_f, o_qkv, o_z, o_ab, o_gf, o_gd = 0, F_WIDTH, F_WIDTH + 3 * QK_WIDTH, F_WIDTH + 4 * QK_WIDTH, \
        F_WIDTH + 4 * QK_WIDTH + 4 * N_HEADS, F_WIDTH + 4 * QK_WIDTH + 4 * N_HEADS + d
    w_cat = jnp.concatenate(
        [w[:, o_qkv:o_z], w[:, o_z:o_ab], w[:, o_gf:o_gd], w[:, o_gd:o_gd + d], w[:, o_f:o_qkv], w[:, o_ab:o_gf],
         jnp.zeros((d, PROJ_WIDTH - w.shape[1]), w.dtype)], axis=1).astype(MXU_DTYPE)

    proj = _in_proj(h, norm1_g[layer][None, :], w_cat, _tile(n_tok, (1032, 104, 8)))
    proj3 = proj.reshape(bsz, seq, PROJ_WIDTH)

    cs128, twid = _dft_tables(seq)
    ymix = _fnet(proj3, cs128, twid, _tile(seq, (344, 104, 8)))

    gate_prm = jnp.zeros((SUBLANES, LANES), F32)
    gate_prm = gate_prm.at[0, 0:N_HEADS].set(a_log_fwd[layer]).at[0, 2 * N_HEADS:3 * N_HEADS].set(a_log_bwd[layer])
    gate_prm = gate_prm.at[1, 0:N_HEADS].set(dt_bias_fwd[layer]).at[1, 2 * N_HEADS:3 * N_HEADS].set(dt_bias_bwd[layer])
    og = _gdn(proj3, conv_w[layer], gate_prm, out_norm_g[layer][None, :])

    w_router = jnp.zeros((d, LANES), F32)
    w_router = w_router.at[:, 0:N_GROUPS].set(w_router_group[layer])
    w_router = w_router.at[:, N_GROUPS:N_GROUPS + N_EXPERTS].set(w_router_expert[layer])
    b_router = jnp.zeros((1, LANES), F32)
    b_router = b_router.at[0, 0:N_GROUPS].set(b_router_group[layer])
    b_router = b_router.at[0, N_GROUPS:N_GROUPS + N_EXPERTS].set(b_router_expert[layer])
    tt = _tile(n_tok, (384, 128))
    h1, hn, ridx, rw, cnt = _merge(
        h, og.reshape(n_tok, d), ymix.reshape(n_tok, F_WIDTH), proj,
        w_fourier[layer].astype(MXU_DTYPE), w_delta[layer].astype(MXU_DTYPE), w_out[layer].astype(MXU_DTYPE),
        norm2_g[layer][None, :], w_router, b_router, tt)

    n_assign = n_tok * TOP_K
    n_blocks = -(-n_assign // EXPERT_BLOCK) + N_EXPERTS
    dest, blk = _plan(ridx, cnt, tt, n_blocks)
    dest_flat = dest[:, :TOP_K].reshape(n_tok // tt, 1, TOP_K * tt)
    owner = blk[:n_blocks, 0]
    n_used = blk[-1:, 0]

    xb = _scatter(dest_flat, hn, jnp.zeros((n_blocks * EXPERT_BLOCK, d), F32), tt)
    yb = _experts(owner, n_used, xb, w_gate_e[layer], w_up_e[layer], w_down_e[layer])
    out = _combine(dest_flat, yb, h1, rw, final_norm_g[None, :], tt)
    return out.reshape(bsz, seq, d)[:, N_META:]
```

```python
import functools
import math

import jax
import jax.numpy as jnp
from jax import lax
from jax.experimental import pallas as pl
from jax.experimental.pallas import tpu as pltpu

F32 = jnp.float32
MXU_DTYPE = jnp.bfloat16

N_META = 16
CHUNK = 64
N_FGROUPS = 4
FGROUP_DIM = 128
F_WIDTH = N_FGROUPS * FGROUP_DIM
N_HEADS = 8
HEAD_DIM = 128
QK_WIDTH = N_HEADS * HEAD_DIM
CONV_K = 5
N_GROUPS = 8
EXPERTS_PER_GROUP = 8
N_EXPERTS = N_GROUPS * EXPERTS_PER_GROUP
TOP_K = 2
EXPERT_BLOCK = 256
NORM_EPS = 1e-6
LANES = 128
SUBLANES = 8
VMEM_LIMIT = 52 * 1024 * 1024
GDN_GROUP = 11
DMA_UNROLL = 8

COL_QKV = 0
COL_Z = 3 * QK_WIDTH
COL_GATE_F = COL_Z + QK_WIDTH
COL_GATE_D = COL_GATE_F + 1024
COL_FIN = COL_GATE_D + 1024
COL_AB = COL_FIN + F_WIDTH
PROJ_TN = 768
PROJ_WIDTH = -(-(COL_AB + LANES) // PROJ_TN) * PROJ_TN


def _dot(a, b):
    return jnp.dot(a, b, preferred_element_type=F32)


def _dot_nt(a, b):
    return lax.dot_general(a, b, (((1,), (1,)), ((), ())), preferred_element_type=F32)


def _dot_tn(a, b):
    return lax.dot_general(a, b, (((0,), (0,)), ((), ())), preferred_element_type=F32)


def _split3(x):
    hi = x.astype(MXU_DTYPE)
    r1 = x - hi.astype(F32)
    mid = r1.astype(MXU_DTYPE)
    lo = (r1 - mid.astype(F32)).astype(MXU_DTYPE)
    return hi, mid, lo


def _exact_left(sel, x):
    hi, mid, lo = _split3(x)
    return _dot(sel, hi) + _dot(sel, mid) + _dot(sel, lo)


def _exact_right(x, sel):
    hi, mid, lo = _split3(x)
    return _dot(hi, sel) + _dot(mid, sel) + _dot(lo, sel)


def _silu(x):
    return x * jax.nn.sigmoid(x)


def _params(sem):
    return pltpu.CompilerParams(dimension_semantics=sem, vmem_limit_bytes=VMEM_LIMIT)


def _inproj_kernel(h_ref, g_ref, w_ref, o_ref, hn_ref):
    @pl.when(pl.program_id(1) == 0)
    def _():
        x = h_ref[...]
        ms = jnp.mean(x * x, axis=-1, keepdims=True)
        hn_ref[...] = (x * lax.rsqrt(ms + NORM_EPS) * g_ref[...]).astype(MXU_DTYPE)

    o_ref[...] = _dot(hn_ref[...], w_ref[...])


def _in_proj(h2d, gain, w_cat, tm):
    t, d = h2d.shape
    n = w_cat.shape[1]
    return pl.pallas_call(
        _inproj_kernel,
        grid=(t // tm, n // PROJ_TN),
        in_specs=[
            pl.BlockSpec((tm, d), lambda i, j: (i, 0)),
            pl.BlockSpec((1, d), lambda i, j: (0, 0)),
            pl.BlockSpec((d, PROJ_TN), lambda i, j: (0, j)),
        ],
        out_specs=pl.BlockSpec((tm, PROJ_TN), lambda i, j: (i, j)),
        out_shape=jax.ShapeDtypeStruct((t, n), F32),
        scratch_shapes=[pltpu.VMEM((tm, d), MXU_DTYPE)],
        compiler_params=_params(("arbitrary", "arbitrary")),
        name="in_proj",
    )(h2d, gain, w_cat)


def _fnet_kernel(x_ref, cs_ref, tw_ref, o_ref, r_ref, *, seq, kp, scale):
    @pl.when(pl.program_id(1) == 0)
    def _():
        tail = kp - seq
        if tail:
            r_ref[seq:kp, :] = jnp.zeros((tail, F_WIDTH), MXU_DTYPE)
            r_ref[kp + seq:, :] = jnp.zeros((tail, F_WIDTH), MXU_DTYPE)
        cs = cs_ref[...]
        for g in range(N_FGROUPS):
            cols = slice(g * FGROUP_DIM, (g + 1) * FGROUP_DIM)
            p = _dot(x_ref[:, cols].astype(MXU_DTYPE), cs)
            r_ref[0:seq, cols] = p[:, :FGROUP_DIM].astype(MXU_DTYPE)
            r_ref[kp:kp + seq, cols] = p[:, FGROUP_DIM:].astype(MXU_DTYPE)

    o_ref[...] = (_dot(tw_ref[...], r_ref[...]) * scale).astype(o_ref.dtype)


def _fnet(proj3, cs128, twid, tmf):
    b, seq, _ = proj3.shape
    kp = twid.shape[1] // 2
    scale = 1.0 / math.sqrt(seq * FGROUP_DIM)
    return pl.pallas_call(
        functools.partial(_fnet_kernel, seq=seq, kp=kp, scale=scale),
        grid=(b, seq // tmf),
        in_specs=[
            pl.BlockSpec((None, seq, F_WIDTH), lambda bi, i: (bi, 0, COL_FIN // F_WIDTH)),
            pl.BlockSpec((FGROUP_DIM, 2 * FGROUP_DIM), lambda bi, i: (0, 0)),
            pl.BlockSpec((tmf, 2 * kp), lambda bi, i: (i, 0)),
        ],
        out_specs=pl.BlockSpec((None, tmf, F_WIDTH), lambda bi, i: (bi, i, 0)),
        out_shape=jax.ShapeDtypeStruct((b, seq, F_WIDTH), MXU_DTYPE),
        scratch_shapes=[pltpu.VMEM((2 * kp, F_WIDTH), MXU_DTYPE)],
        compiler_params=_params(("arbitrary", "arbitrary")),
        name="fnet",
    )(proj3, cs128, twid)


def _aligned(x, m):
    return x if isinstance(x, int) else pl.multiple_of(x, m)


def _gdn_kernel(q_ref, k_ref, v_ref, z_ref, ab_ref, cwq_ref, cwk_ref, cwv_ref, prm_ref, ong_ref,
                o_ref,
                rawq_ref, rawk_ref, rawv_ref, qs_ref, ks_ref, vs_ref, gsel_ref, grow_ref,
                qe_ref, mp_ref, nn_ref, dd_ref, osum_ref,
                *, seq, lp, group, n_heads_total):
    step = pl.program_id(0)
    head = jnp.minimum(step, n_heads_total - 1) % N_HEADS
    nc = lp // CHUNK
    padr = lp - seq
    halo = SUBLANES
    row64 = lax.broadcasted_iota(jnp.int32, (CHUNK, LANES), 0)
    lane64 = lax.broadcasted_iota(jnp.int32, (CHUNK, LANES), 1)

    def for_groups(fn):
        n_full, rem = divmod(nc, group)
        if n_full:
            def body(i, carry):
                fn([i * group + j for j in range(group)])
                return carry
            lax.fori_loop(0, n_full, body, 0)
        if rem:
            fn([n_full * group + j for j in range(rem)])

    streams = ((rawq_ref, q_ref, cwq_ref, qs_ref, HEAD_DIM ** -0.5),
               (rawk_ref, k_ref, cwk_ref, ks_ref, 1.0),
               (rawv_ref, v_ref, cwv_ref, vs_ref, None))
    def stage_conv_inputs():
        for raw_ref, src_ref, _, _, _ in streams:
            raw_ref[0:padr + halo, :] = jnp.zeros((padr + halo, LANES), F32)
            raw_ref[padr + halo:padr + halo + seq, :] = src_ref[...]
            raw_ref[lp + halo:lp + 2 * halo, :] = jnp.zeros((halo, LANES), F32)

    def conv_chunk(c):
        base = pl.multiple_of(c * CHUNK, CHUNK)
        for raw_ref, _, cw_ref, dst_ref, l2_scale in streams:
            x = raw_ref[pl.ds(base, CHUNK + 2 * halo), :]
            cw = cw_ref[...]
            acc = jnp.zeros((CHUNK, LANES), F32)
            for j in range(CONV_K):
                off = halo - (CONV_K - 1) // 2 + j
                acc = acc + cw[j:j + 1, :] * x[off:off + CHUNK, :]
            y = _silu(acc)
            y = jnp.where(row64 + base >= padr, y, 0.0)
            if l2_scale is not None:
                y = y * lax.rsqrt(jnp.sum(y * y, axis=-1, keepdims=True) + NORM_EPS) * l2_scale
            dst_ref[pl.ds(base, CHUNK), :] = y

    raw_ref = rawq_ref

    def stage_gate_inputs():
        raw_ref[0:padr, :] = jnp.zeros((padr, LANES), F32)
        raw_ref[padr:padr + seq, :] = ab_ref[...]

    neg_a = -jnp.exp(prm_ref[0:1, :])
    dt_bias = prm_ref[1:2, :]
    is_g = (lane64 < 8) | ((lane64 >= 16) & (lane64 < 24))
    is_beta = ((lane64 >= 8) & (lane64 < 16)) | ((lane64 >= 24) & (lane64 < 32))
    r128 = lax.broadcasted_iota(jnp.int32, (2 * CHUNK, CHUNK), 0)
    c128 = lax.broadcasted_iota(jnp.int32, (2 * CHUNK, CHUNK), 1)
    tri2 = jnp.where(((r128 < CHUNK) & (r128 >= c128)) | ((r128 >= CHUNK) & (r128 - CHUNK <= c128)),
                     1.0, 0.0).astype(MXU_DTYPE)
    sel_r = lax.broadcasted_iota(jnp.int32, (LANES, 4 * LANES), 0)
    sel_c = lax.broadcasted_iota(jnp.int32, (LANES, 4 * LANES), 1)
    sel = jnp.where(sel_r == head + 8 * (sel_c // LANES), 1.0, 0.0).astype(MXU_DTYPE)

    def gate_group(cs):
        bases = [_aligned(c * CHUNK, CHUNK) for c in cs]
        gates, parts = [], []
        for base in bases:
            ab = raw_ref[pl.ds(base, CHUNK), :]
            g = neg_a * jax.nn.softplus(ab + dt_bias)
            beta = jax.nn.sigmoid(ab)
            gt = jnp.where(is_g, g, jnp.where(is_beta, beta, 0.0))
            gt = jnp.where(row64 + base >= padr, gt, 0.0)
            gates.append(gt)
            parts.append(_split3(gt))
        cums = [_dot(tri2, p[0]) + _dot(tri2, p[1]) + _dot(tri2, p[2]) for p in parts]
        mixed = [jnp.where(lane64 < 8, cm[:CHUNK],
                           jnp.where((lane64 >= 16) & (lane64 < 24), cm[CHUNK:], gt))
                 for cm, gt in zip(cums, gates)]
        parts = [_split3(m) for m in mixed]
        for base, p in zip(bases, parts):
            gsel_ref[pl.ds(base, CHUNK), :] = _dot(p[0], sel) + _dot(p[1], sel) + _dot(p[2], sel)
        for c, m in zip(cs, mixed):
            cbase = _aligned(c * LANES, LANES)
            grow_ref[pl.ds(cbase, LANES), :] = jnp.concatenate([m, m], axis=0).T
        return None

    fwd_half = lane64 < CHUNK
    col64 = lane64 & (CHUNK - 1)
    incl2 = (fwd_half & (row64 >= col64)) | (~fwd_half & (row64 <= col64))
    strict2 = (fwd_half & (row64 > col64)) | (~fwd_half & (row64 < col64))
    eye2 = jnp.where(row64 == col64, 1.0, 0.0).astype(F32)

    def blockdiag(p):
        return jnp.concatenate([jnp.where(fwd_half, p, 0.0), jnp.where(fwd_half, 0.0, p)],
                               axis=0).astype(MXU_DTYPE)

    def chunk_group(cs):
        n = len(cs)
        bases = [_aligned(c * CHUNK, CHUNK) for c in cs]
        cb128 = [_aligned(c * LANES, LANES) for c in cs]
        cb8 = [_aligned(c * SUBLANES, SUBLANES) for c in cs]

        def gate_cols(j):
            gs = gsel_ref[pl.ds(bases[j], CHUNK), :]
            return gs[:, 0:LANES], gs[:, LANES:2 * LANES], gs[:, 2 * LANES:3 * LANES], gs[:, 3 * LANES:]

        kq = []
        for j in range(n):
            k16 = ks_ref[pl.ds(bases[j], CHUNK), :].astype(MXU_DTYPE)
            q16 = qs_ref[pl.ds(bases[j], CHUNK), :].astype(MXU_DTYPE)
            kq.append(_dot_nt(jnp.concatenate([k16, q16], axis=0), jnp.concatenate([k16, k16], axis=0)))

        ts, ps, qkds = [], [], []
        for j in range(n):
            gf, bf, gb, bb = gate_cols(j)
            row_f = grow_ref[pl.ds(cb128[j] + head, 1), :]
            row_b = grow_ref[pl.ds(cb128[j] + 2 * N_HEADS + head, 1), :]
            g_col = jnp.where(fwd_half, gf, gb)
            g_row = jnp.where(fwd_half, row_f, row_b)
            decay = jnp.where(incl2, jnp.exp(jnp.where(incl2, g_col - g_row, 0.0)), 0.0)
            a2 = jnp.where(strict2, kq[j][:CHUNK] * jnp.where(fwd_half, bf, bb) * decay, 0.0)
            qkds.append(jnp.where(incl2, kq[j][CHUNK:] * decay, 0.0))
            ts.append(eye2 - a2)
            ps.append(a2)

        ps = [_dot(p.astype(MXU_DTYPE), blockdiag(p)) for p in ps]
        for _ in range(int(math.log2(CHUNK)) - 2):
            tps = [_dot(jnp.concatenate([t, p], axis=0).astype(MXU_DTYPE), blockdiag(p)) for t, p in zip(ts, ps)]
            ts = [t + tp[:CHUNK] for t, tp in zip(ts, tps)]
            ps = [tp[CHUNK:] for tp in tps]
        ts = [t + _dot(t.astype(MXU_DTYPE), blockdiag(p)) for t, p in zip(ts, ps)]

        uws = []
        for j in range(n):
            gf, bf, gb, bb = gate_cols(j)
            k = ks_ref[pl.ds(bases[j], CHUNK), :]
            v = vs_ref[pl.ds(bases[j], CHUNK), :]
            x_f = jnp.concatenate([v * bf, k * bf * jnp.exp(gf)], axis=1)
            x_b = jnp.concatenate([v * bb, k * bb * jnp.exp(gb)], axis=1)
            uw = _dot(blockdiag(ts[j]), jnp.concatenate([x_f, x_b], axis=0).astype(MXU_DTYPE))
            uws.append(uw.astype(MXU_DTYPE))

        xxs = [_dot(blockdiag(qkds[j]), uws[j]) for j in range(n)]
        for j in range(n):
            gf, _, gb, _ = gate_cols(j)
            k = ks_ref[pl.ds(bases[j], CHUNK), :]
            q = qs_ref[pl.ds(bases[j], CHUNK), :]
            xx = xxs[j]
            osum_ref[pl.ds(bases[j], CHUNK), :] = xx[:CHUNK, :LANES] + xx[CHUNK:, :LANES]
            for d, gc in enumerate((gf, gb)):
                rows = slice(d * CHUNK, (d + 1) * CHUNK)
                g_last = gc[CHUNK - 1:CHUNK, :] if d == 0 else gc[0:1, :]
                k_dec = k * jnp.exp(g_last - gc)
                mn = _dot_tn(k_dec.astype(MXU_DTYPE), uws[j][rows])
                qe_ref[d, pl.ds(bases[j], CHUNK), :] = (q * jnp.exp(gc) - xx[rows, LANES:]).astype(MXU_DTYPE)
                nn_ref[d, pl.ds(cb128[j], HEAD_DIM), :] = mn[:, :LANES]
                mp_ref[d, pl.ds(cb128[j], HEAD_DIM), :] = mn[:, LANES:].astype(MXU_DTYPE)
                dd_ref[d, pl.ds(cb8[j], SUBLANES), :] = jnp.broadcast_to(jnp.exp(g_last), (SUBLANES, LANES))
        return None

    def scan_step(i, carry):
        new = []
        for d in range(2):
            s = carry[d]
            c = i if d == 0 else nc - 1 - i
            base = pl.multiple_of(c * CHUNK, CHUNK)
            cbase = pl.multiple_of(c * HEAD_DIM, HEAD_DIM)
            dbase = pl.multiple_of(c * SUBLANES, SUBLANES)
            s16 = s.astype(MXU_DTYPE)
            osum_ref[pl.ds(base, CHUNK), :] += _dot(qe_ref[d, pl.ds(base, CHUNK), :], s16)
            dec = dd_ref[d, pl.ds(dbase, 1), :]
            new.append(dec * s - _dot(mp_ref[d, pl.ds(cbase, HEAD_DIM), :], s16)
                       + nn_ref[d, pl.ds(cbase, HEAD_DIM), :])
        return tuple(new)

    zero_state = jnp.zeros((HEAD_DIM, HEAD_DIM), F32)

    def write_output():
        gain = ong_ref[...]

        def finish(o, z):
            ms = jnp.mean(o * o, axis=-1, keepdims=True)
            return (o * lax.rsqrt(ms + NORM_EPS) * gain * _silu(z)).astype(o_ref.dtype)

        first = CHUNK - padr if padr else 0
        if first:
            o_ref[0:first, :] = finish(osum_ref[padr:CHUNK, :], z_ref[0:first, :])

        def out_body(c, carry):
            base = pl.multiple_of(c * CHUNK, CHUNK)
            dst = pl.multiple_of(c * CHUNK - padr, SUBLANES)
            o_ref[pl.ds(dst, CHUNK), :] = finish(osum_ref[pl.ds(base, CHUNK), :], z_ref[pl.ds(dst, CHUNK), :])
            return carry

        lax.fori_loop(1 if first else 0, nc, out_body, 0, unroll=4)

    def conv_only(c, carry):
        conv_chunk(c)
        return carry

    def conv_and_scan(c, carry):
        conv_chunk(c)
        return scan_step(c, carry)

    @pl.when(step == 0)
    def _():
        stage_conv_inputs()
        lax.fori_loop(0, nc, conv_only, 0, unroll=3)

    @pl.when((step > 0) & (step < n_heads_total))
    def _():
        stage_conv_inputs()
        lax.fori_loop(0, nc, conv_and_scan, (zero_state, zero_state), unroll=3)
        write_output()

    @pl.when(step == n_heads_total)
    def _():
        lax.fori_loop(0, nc, scan_step, (zero_state, zero_state))
        write_output()

    @pl.when(step < n_heads_total)
    def _():
        stage_gate_inputs()
        for_groups(gate_group)
        for_groups(chunk_group)


def _gdn(proj3, conv_w, gate_prm, out_norm_g):
    b, seq, _ = proj3.shape
    lp = -(-seq // CHUNK) * CHUNK
    nc = lp // CHUNK
    n_total = b * N_HEADS
    cur = lambda i: jnp.minimum(i, n_total - 1)
    prev = lambda i: jnp.maximum(i - 1, 0)
    col = lambda off: pl.BlockSpec((None, seq, LANES), lambda i: (cur(i) // N_HEADS, 0, off + cur(i) % N_HEADS))
    cw = lambda off: pl.BlockSpec((CONV_K, LANES), lambda i: (0, off + cur(i) % N_HEADS))
    return pl.pallas_call(
        functools.partial(_gdn_kernel, seq=seq, lp=lp, group=GDN_GROUP, n_heads_total=n_total),
        grid=(n_total + 1,),
        in_specs=[
            col(COL_QKV // LANES), col(COL_QKV // LANES + N_HEADS), col(COL_QKV // LANES + 2 * N_HEADS),
            pl.BlockSpec((None, seq, LANES),
                         lambda i: (prev(i) // N_HEADS, 0, COL_Z // LANES + prev(i) % N_HEADS)),
            pl.BlockSpec((None, seq, LANES), lambda i: (cur(i) // N_HEADS, 0, COL_AB // LANES)),
            cw(0), cw(N_HEADS), cw(2 * N_HEADS),
            pl.BlockSpec((SUBLANES, LANES), lambda i: (0, 0)),
            pl.BlockSpec((1, LANES), lambda i: (0, 0)),
        ],
        out_specs=pl.BlockSpec((None, seq, LANES), lambda i: (prev(i) // N_HEADS, 0, prev(i) % N_HEADS)),
        out_shape=jax.ShapeDtypeStruct((b, seq, N_HEADS * HEAD_DIM), MXU_DTYPE),
        scratch_shapes=[
            pltpu.VMEM((lp + 2 * SUBLANES, LANES), F32),
            pltpu.VMEM((lp + 2 * SUBLANES, LANES), F32),
            pltpu.VMEM((lp + 2 * SUBLANES, LANES), F32),
            pltpu.VMEM((lp, LANES), F32),
            pltpu.VMEM((lp, LANES), F32),
            pltpu.VMEM((lp, LANES), F32),
            pltpu.VMEM((lp, 4 * LANES), F32),
            pltpu.VMEM((nc * LANES, LANES), F32),
            pltpu.VMEM((2, lp, LANES), MXU_DTYPE),
            pltpu.VMEM((2, nc * HEAD_DIM, LANES), MXU_DTYPE),
            pltpu.VMEM((2, nc * HEAD_DIM, LANES), F32),
            pltpu.VMEM((2, nc * SUBLANES, LANES), F32),
            pltpu.VMEM((lp, LANES), F32),
        ],
        compiler_params=_params(("arbitrary",)),
        name="gdn",
    )(proj3, proj3, proj3, proj3, proj3, conv_w, conv_w, conv_w, gate_prm, out_norm_g)


def _merge_kernel(h_ref, og_ref, ym_ref, gf_ref, gd_ref, wf_ref, wd_ref, wo_ref, n2_ref, wr_ref, br_ref,
                  h1_ref, hn_ref, ridx_ref, rw_ref, cnt_ref):
    @pl.when(pl.program_id(0) == 0)
    def _():
        cnt_ref[...] = jnp.zeros_like(cnt_ref)

    y_f = _dot(ym_ref[...], wf_ref[...])
    y_d = _dot(og_ref[...], wd_ref[...])
    merged = jax.nn.sigmoid(gf_ref[...]) * y_f + jax.nn.sigmoid(gd_ref[...]) * y_d
    h1 = h_ref[...] + _dot(merged.astype(MXU_DTYPE), wo_ref[...])
    h1_ref[...] = h1
    ms = jnp.mean(h1 * h1, axis=-1, keepdims=True)
    hn = h1 * lax.rsqrt(ms + NORM_EPS) * n2_ref[...]
    hn_ref[...] = hn

    x_hi = hn.astype(MXU_DTYPE)
    x_lo = (hn - x_hi.astype(F32)).astype(MXU_DTYPE)
    w = wr_ref[...]
    w_hi = w.astype(MXU_DTYPE)
    w_lo = (w - w_hi.astype(F32)).astype(MXU_DTYPE)
    logits = _dot(x_hi, w_hi) + _dot(x_hi, w_lo) + _dot(x_lo, w_hi) + br_ref[...]

    tm = logits.shape[0]
    lane = lax.broadcasted_iota(jnp.int32, (tm, LANES), 1)
    neg_inf = -jnp.inf
    is_group = lane < N_GROUPS
    gl = jnp.where(is_group, logits, neg_inf)
    ge = jnp.exp(gl - jnp.max(gl, axis=-1, keepdims=True))
    gp = ge / jnp.sum(ge, axis=-1, keepdims=True)
    p_group = jnp.max(gp, axis=-1, keepdims=True)
    g_idx = jnp.min(jnp.where(is_group & (gp == p_group), lane, LANES), axis=-1, keepdims=True)

    in_group = (lane >= N_GROUPS) & (lane < N_GROUPS + N_EXPERTS) & (((lane - N_GROUPS) >> 3) == g_idx)
    el = jnp.where(in_group, logits, neg_inf)
    ee = jnp.exp(el - jnp.max(el, axis=-1, keepdims=True))
    ep = ee / jnp.sum(ee, axis=-1, keepdims=True)
    v1 = jnp.max(jnp.where(in_group, ep, -1.0), axis=-1, keepdims=True)
    i1 = jnp.min(jnp.where(in_group & (ep == v1), lane, LANES), axis=-1, keepdims=True)
    rest = in_group & (lane != i1)
    v2 = jnp.max(jnp.where(rest, ep, -1.0), axis=-1, keepdims=True)
    i2 = jnp.min(jnp.where(rest & (ep == v2), lane, LANES), axis=-1, keepdims=True)
    denom = v1 + v2
    w1 = p_group * (v1 / denom)
    w2 = p_group * (v2 / denom)
    e1 = i1 - N_GROUPS
    e2 = i2 - N_GROUPS
    ridx_ref[...] = jnp.where(lane == 0, e1, jnp.where(lane == 1, e2, 0))
    rw_ref[...] = jnp.where(lane == 0, w1, jnp.where(lane == 1, w2, 0.0))
    onehots = jnp.where((lane == e1) | (lane == e2), 1.0, 0.0)
    cnt_ref[...] += jnp.broadcast_to(jnp.sum(onehots, axis=0, keepdims=True), cnt_ref.shape)


def _merge(h2d, og2d, ymix2d, proj2d, w_fourier, w_delta, w_out, norm2_g, w_router, b_router, tm):
    t, d = h2d.shape
    row = lambda w: pl.BlockSpec((tm, w), lambda i: (i, 0))
    const = lambda shape: pl.BlockSpec(shape, lambda i: (0, 0))
    return pl.pallas_call(
        _merge_kernel,
        grid=(t // tm,),
        in_specs=[
            row(d), row(d), row(F_WIDTH),
            pl.BlockSpec((tm, d), lambda i: (i, COL_GATE_F // d)),
            pl.BlockSpec((tm, d), lambda i: (i, COL_GATE_D // d)),
            const((F_WIDTH, d)), const((d, d)), const((d, d)), const((1, d)),
            const((d, LANES)), const((1, LANES)),
        ],
        out_specs=[row(d), row(d), row(LANES), row(LANES), const((SUBLANES, LANES))],
        out_shape=[
            jax.ShapeDtypeStruct((t, d), F32),
            jax.ShapeDtypeStruct((t, d), F32),
            jax.ShapeDtypeStruct((t, LANES), jnp.int32),
            jax.ShapeDtypeStruct((t, LANES), F32),
            jax.ShapeDtypeStruct((SUBLANES, LANES), F32),
        ],
        compiler_params=_params(("arbitrary",)),
        name="merge",
    )(h2d, og2d, ymix2d, proj2d, proj2d, w_fourier, w_delta, w_out, norm2_g, w_router, b_router)


def _plan_kernel(ridx_ref, cnt_ref, dest_ref, blk_ref, base_ref, run_ref, *, n_blk_rows):
    tt = ridx_ref.shape[0]
    lane8 = lax.broadcasted_iota(jnp.int32, (SUBLANES, LANES), 1)

    @pl.when(pl.program_id(0) == 0)
    def _():
        cnt = cnt_ref[...].astype(jnp.int32)
        padded = (((cnt + (EXPERT_BLOCK - 1)) // EXPERT_BLOCK) * EXPERT_BLOCK).astype(F32)
        r = lax.broadcasted_iota(jnp.int32, (LANES, LANES), 0)
        c = lax.broadcasted_iota(jnp.int32, (LANES, LANES), 1)
        before = jnp.where(r < c, 1.0, 0.0).astype(MXU_DTYPE)
        start = _exact_right(padded, before)
        base_ref[...] = start
        run_ref[...] = jnp.zeros_like(run_ref)
        ends = (start + padded)[0:1, :]
        blk_start = (lax.broadcasted_iota(jnp.int32, (n_blk_rows, LANES), 0) * EXPERT_BLOCK).astype(F32)
        lane = lax.broadcasted_iota(jnp.int32, (n_blk_rows, LANES), 1)
        hit = jnp.where((lane < N_EXPERTS) & (ends <= blk_start), 1.0, 0.0)
        owner = jnp.minimum(jnp.sum(hit, axis=-1, keepdims=True), float(N_EXPERTS - 1))
        n_used = jnp.max(jnp.where(lane8[0:1, :] < N_EXPERTS, ends, 0.0), axis=-1, keepdims=True) / EXPERT_BLOCK
        blk_ref[0:n_blk_rows, :] = jnp.broadcast_to(owner, (n_blk_rows, LANES)).astype(jnp.int32)
        blk_ref[n_blk_rows:, :] = jnp.broadcast_to(n_used, (SUBLANES, LANES)).astype(jnp.int32)

    lane = lax.broadcasted_iota(jnp.int32, (tt, LANES), 1)
    ridx = ridx_ref[...]
    oh0 = jnp.where(lane == ridx[:, 0:1], 1.0, 0.0)
    oh1 = jnp.where(lane == ridx[:, 1:2], 1.0, 0.0)
    r = lax.broadcasted_iota(jnp.int32, (tt, tt), 0)
    c = lax.broadcasted_iota(jnp.int32, (tt, tt), 1)
    earlier = jnp.where(c < r, 1.0, 0.0).astype(MXU_DTYPE)
    prefix = _dot(earlier, jnp.concatenate([oh0, oh1], axis=1).astype(MXU_DTYPE))
    c0 = jnp.sum(oh0, axis=0, keepdims=True)
    c1 = jnp.sum(oh1, axis=0, keepdims=True)
    first = base_ref[0:1, :] + run_ref[0:1, :]
    d0 = jnp.sum(oh0 * (prefix[:, :LANES] + first), axis=-1, keepdims=True)
    d1 = jnp.sum(oh1 * (prefix[:, LANES:] + first + c0), axis=-1, keepdims=True)
    run_ref[...] += jnp.broadcast_to(c0 + c1, run_ref.shape)
    dest_ref[...] = jnp.where(lane == 0, d0, jnp.where(lane == 1, d1, 0.0)).astype(jnp.int32)


def _plan(ridx, cnt, tt, n_blocks):
    t = ridx.shape[0]
    n_blk_rows = -(-n_blocks // SUBLANES) * SUBLANES
    return pl.pallas_call(
        functools.partial(_plan_kernel, n_blk_rows=n_blk_rows),
        grid=(t // tt,),
        in_specs=[pl.BlockSpec((tt, LANES), lambda i: (i, 0)),
                  pl.BlockSpec((SUBLANES, LANES), lambda i: (0, 0))],
        out_specs=[pl.BlockSpec((tt, LANES), lambda i: (i, 0)),
                   pl.BlockSpec((n_blk_rows + SUBLANES, LANES), lambda i: (0, 0))],
        out_shape=[jax.ShapeDtypeStruct((t, LANES), jnp.int32),
                   jax.ShapeDtypeStruct((n_blk_rows + SUBLANES, LANES), jnp.int32)],
        scratch_shapes=[pltpu.VMEM((SUBLANES, LANES), F32), pltpu.VMEM((SUBLANES, LANES), F32)],
        compiler_params=_params(("arbitrary",)),
        name="plan",
    )(ridx, cnt)


def _scatter_kernel(dest_ref, hn_ref, xb_in_ref, xb_ref, sem, *, ts):
    del xb_in_ref

    def issue(t, carry):
        src = hn_ref.at[pl.ds(t, 1)]
        for k in range(TOP_K):
            pltpu.make_async_copy(src, xb_ref.at[pl.ds(dest_ref[0, TOP_K * t + k], 1)], sem).start()
        return carry

    lax.fori_loop(0, ts, issue, 0, unroll=DMA_UNROLL)
    for _ in range(TOP_K):
        pltpu.make_async_copy(hn_ref, xb_ref.at[pl.ds(0, ts)], sem).wait()


def _scatter(dest_flat, hn2d, xb_zero, ts):
    t, d = hn2d.shape
    return pl.pallas_call(
        functools.partial(_scatter_kernel, ts=ts),
        grid=(t // ts,),
        in_specs=[pl.BlockSpec((None, 1, TOP_K * ts), lambda i: (i, 0, 0), memory_space=pltpu.SMEM),
                  pl.BlockSpec((ts, d), lambda i: (i, 0)),
                  pl.BlockSpec(memory_space=pl.ANY)],
        out_specs=pl.BlockSpec(memory_space=pl.ANY),
        out_shape=jax.ShapeDtypeStruct(xb_zero.shape, F32),
        scratch_shapes=[pltpu.SemaphoreType.DMA(())],
        input_output_aliases={2: 0},
        compiler_params=_params(("arbitrary",)),
        name="scatter",
    )(dest_flat, hn2d, xb_zero)


def _expert_kernel(owner_ref, nused_ref, x_ref, wg_ref, wu_ref, wd_ref, y_ref):
    i = pl.program_id(0)

    @pl.when(i < nused_ref[0])
    def _():
        x = x_ref[...].astype(MXU_DTYPE)
        gate = _dot(x, wg_ref[...].astype(MXU_DTYPE))
        up = _dot(x, wu_ref[...].astype(MXU_DTYPE))
        y_ref[...] = _dot((_silu(gate) * up).astype(MXU_DTYPE), wd_ref[...].astype(MXU_DTYPE))

    @pl.when(i >= nused_ref[0])
    def _():
        y_ref[...] = jnp.zeros_like(y_ref)


def _experts(owner, n_used, xb, w_gate, w_up, w_down):
    n_slots, d = xb.shape
    de = w_gate.shape[-1]
    n_blocks = n_slots // EXPERT_BLOCK
    grid_spec = pltpu.PrefetchScalarGridSpec(
        num_scalar_prefetch=2,
        grid=(n_blocks,),
        in_specs=[
            pl.BlockSpec((EXPERT_BLOCK, d), lambda i, own, nu: (i, 0)),
            pl.BlockSpec((None, d, de), lambda i, own, nu: (own[i], 0, 0)),
            pl.BlockSpec((None, d, de), lambda i, own, nu: (own[i], 0, 0)),
            pl.BlockSpec((None, de, d), lambda i, own, nu: (own[i], 0, 0)),
        ],
        out_specs=pl.BlockSpec((EXPERT_BLOCK, d), lambda i, own, nu: (i, 0)),
    )
    return pl.pallas_call(
        _expert_kernel,
        grid_spec=grid_spec,
        out_shape=jax.ShapeDtypeStruct((n_slots, d), F32),
        compiler_params=_params(("arbitrary",)),
        name="experts",
    )(owner, n_used, xb, w_gate, w_up, w_down)


def _combine_kernel(dest_ref, yb_ref, h1_ref, rw_ref, fg_ref, o_ref, buf_ref, sem, *, tc):
    def issue(t, carry):
        for k in range(TOP_K):
            pltpu.make_async_copy(yb_ref.at[pl.ds(dest_ref[0, TOP_K * t + k], 1)],
                                  buf_ref.at[k, pl.ds(t, 1)], sem).start()
        return carry

    lax.fori_loop(0, tc, issue, 0, unroll=DMA_UNROLL)
    for k in range(TOP_K):
        pltpu.make_async_copy(yb_ref.at[pl.ds(0, tc)], buf_ref.at[k], sem).wait()

    rw = rw_ref[...]
    h2 = h1_ref[...] + (buf_ref[0] * rw[:, 0:1] + buf_ref[1] * rw[:, 1:2])
    ms = jnp.mean(h2 * h2, axis=-1, keepdims=True)
    o_ref[...] = h2 * lax.rsqrt(ms + NORM_EPS) * fg_ref[...]


def _combine(dest_flat, yb, h1, rw, final_g, tc):
    t, d = h1.shape
    return pl.pallas_call(
        functools.partial(_combine_kernel, tc=tc),
        grid=(t // tc,),
        in_specs=[pl.BlockSpec((None, 1, TOP_K * tc), lambda i: (i, 0, 0), memory_space=pltpu.SMEM),
                  pl.BlockSpec(memory_space=pl.ANY),
                  pl.BlockSpec((tc, d), lambda i: (i, 0)),
                  pl.BlockSpec((tc, LANES), lambda i: (i, 0)),
                  pl.BlockSpec((1, d), lambda i: (0, 0))],
        out_specs=pl.BlockSpec((tc, d), lambda i: (i, 0)),
        out_shape=jax.ShapeDtypeStruct((t, d), F32),
        scratch_shapes=[pltpu.VMEM((TOP_K, tc, d), F32), pltpu.SemaphoreType.DMA(())],
        compiler_params=_params(("arbitrary",)),
        name="combine",
    )(dest_flat, yb, h1, rw, final_g)


def _tile(total, candidates):
    for c in candidates:
        if total % c == 0:
            return c
    raise ValueError(f"no tile for {total}")


def _dft_tables(seq):
    kp = -(-seq // LANES) * LANES
    c = jnp.arange(FGROUP_DIM, dtype=jnp.int32)
    ang_c = (2.0 * math.pi / FGROUP_DIM) * ((c[:, None] * c[None, :]) % FGROUP_DIM).astype(F32)
    cs128 = jnp.concatenate([jnp.cos(ang_c), jnp.sin(ang_c)], axis=1).astype(MXU_DTYPE)
    n = jnp.arange(seq, dtype=jnp.int32)
    ang_l = (2.0 * math.pi / seq) * ((n[:, None] * n[None, :]) % seq).astype(F32)
    pad = ((0, 0), (0, kp - seq))
    twid = jnp.concatenate([jnp.pad(jnp.cos(ang_l), pad), jnp.pad(-jnp.sin(ang_l), pad)], axis=1)
    return cs128, twid.astype(MXU_DTYPE)


def kernel(x, meta_tokens, norm1_g, w_in, conv_w, a_log_fwd, dt_bias_fwd, a_log_bwd, dt_bias_bwd, out_norm_g, w_fourier, w_delta, w_out, norm2_g, w_router_group, b_router_group, w_router_expert, b_router_expert, w_gate_e, w_up_e, w_down_e, final_norm_g):
    bsz, n_real, d = x.shape
    seq = N_META + n_real
    n_tok = bsz * seq
    assert (CHUNK - N_META) % CHUNK + seq == -(-seq // CHUNK) * CHUNK, "meta tokens must fill the tail of chunk 0"
    assert w_in.shape[0] == 1, "single trunk layer"
    layer = 0

    meta = jnp.broadcast_to(meta_tokens[None].astype(x.dtype), (bsz, N_META, d))
    h = jnp.concatenate([meta, x], axis=1).reshape(n_tok, d)

    w = w_in[layer]
    o_f, o_qkv, o_z, o_ab, o_gf, o_gd = 0, F_WIDTH, F_WIDTH + 3 * QK_WIDTH, F_WIDTH + 4 * QK_WIDTH, \
        F_WIDTH + 4 * QK_WIDTH + 4 * N_HEADS, F_WIDTH + 4 * QK_WIDTH + 4 * N_HEADS + d
    w_cat = jnp.concatenate(
        [w[:, o_qkv:o_z], w[:, o_z:o_ab], w[:, o_gf:o_gd], w[:, o_gd:o_gd + d], w[:, o_f:o_qkv], w[:, o_ab:o_gf],
         jnp.zeros((d, PROJ_WIDTH - w.shape[1]), w.dtype)], axis=1).astype(MXU_DTYPE)

    proj = _in_proj(h, norm1_g[layer][None, :], w_cat, _tile(n_tok, (2064, 104, 8)))
    proj3 = proj.reshape(bsz, seq, PROJ_WIDTH)

    cs128, twid = _dft_tables(seq)
    ymix = _fnet(proj3, cs128, twid, _tile(seq, (344, 104, 8)))

    gate_prm = jnp.zeros((SUBLANES, LANES), F32)
    gate_prm = gate_prm.at[0, 0:N_HEADS].set(a_log_fwd[layer]).at[0, 2 * N_HEADS:3 * N_HEADS].set(a_log_bwd[layer])
    gate_prm = gate_prm.at[1, 0:N_HEADS].set(dt_bias_fwd[layer]).at[1, 2 * N_HEADS:3 * N_HEADS].set(dt_bias_bwd[layer])
    og = _gdn(proj3, conv_w[layer], gate_prm, out_norm_g[layer][None, :])

    w_router = jnp.zeros((d, LANES), F32)
    w_router = w_router.at[:, 0:N_GROUPS].set(w_router_group[layer])
    w_router = w_router.at[:, N_GROUPS:N_GROUPS + N_EXPERTS].set(w_router_expert[layer])
    b_router = jnp.zeros((1, LANES), F32)
    b_router = b_router.at[0, 0:N_GROUPS].set(b_router_group[layer])
    b_router = b_router.at[0, N_GROUPS:N_GROUPS + N_EXPERTS].set(b_router_expert[layer])
    tt = _tile(n_tok, (384, 128))
    h1, hn, ridx, rw, cnt = _merge(
        h, og.reshape(n_tok, d), ymix.reshape(n_tok, F_WIDTH), proj,
        w_fourier[layer].astype(MXU_DTYPE), w_delta[layer].astype(MXU_DTYPE), w_out[layer].astype(MXU_DTYPE),
        norm2_g[layer][None, :], w_router, b_router, tt)

    n_assign = n_tok * TOP_K
    n_blocks = -(-n_assign // EXPERT_BLOCK) + N_EXPERTS
    dest, blk = _plan(ridx, cnt, tt, n_blocks)
    dest_flat = dest[:, :TOP_K].reshape(n_tok // tt, 1, TOP_K * tt)
    owner = blk[:n_blocks, 0]
    n_used = blk[-1:, 0]

    xb = _scatter(dest_flat, hn, jnp.zeros((n_blocks * EXPERT_BLOCK, d), F32), tt)
    yb = _experts(owner, n_used, xb, w_gate_e[layer], w_up_e[layer], w_down_e[layer])
    out = _combine(dest_flat, yb, h1, rw, final_norm_g[None, :], tt)
    return out.reshape(bsz, seq, d)[:, N_META:]
```

```python
import functools
import math

import jax
import jax.numpy as jnp
import numpy as np
from jax import lax
from jax.experimental import pallas as pl
from jax.experimental.pallas import tpu as pltpu

F32 = jnp.float32
MXU_DTYPE = jnp.bfloat16

N_META = 16
CHUNK = 64
N_FGROUPS = 4
FGROUP_DIM = 128
F_WIDTH = N_FGROUPS * FGROUP_DIM
N_HEADS = 8
HEAD_DIM = 128
QK_WIDTH = N_HEADS * HEAD_DIM
CONV_K = 5
N_GROUPS = 8
EXPERTS_PER_GROUP = 8
N_EXPERTS = N_GROUPS * EXPERTS_PER_GROUP
TOP_K = 2
EXPERT_BLOCK = 256
NORM_EPS = 1e-6
LANES = 128
SUBLANES = 8
VMEM_LIMIT = 52 * 1024 * 1024
GDN_GROUP = 11
DMA_UNROLL = 8

COL_QKV = 0
COL_Z = 3 * QK_WIDTH
COL_GATE_F = COL_Z + QK_WIDTH
COL_GATE_D = COL_GATE_F + 1024
COL_FIN = COL_GATE_D + 1024
COL_AB = COL_FIN + F_WIDTH
PROJ_TN = 768
PROJ_WIDTH = -(-(COL_AB + LANES) // PROJ_TN) * PROJ_TN


def _dot(a, b):
    return jnp.dot(a, b, preferred_element_type=F32)


def _dot_nt(a, b):
    return lax.dot_general(a, b, (((1,), (1,)), ((), ())), preferred_element_type=F32)


def _dot_tn(a, b):
    return lax.dot_general(a, b, (((0,), (0,)), ((), ())), preferred_element_type=F32)


def _split3(x):
    hi = x.astype(MXU_DTYPE)
    r1 = x - hi.astype(F32)
    mid = r1.astype(MXU_DTYPE)
    lo = (r1 - mid.astype(F32)).astype(MXU_DTYPE)
    return hi, mid, lo


def _exact_left(sel, x):
    hi, mid, lo = _split3(x)
    return _dot(sel, hi) + _dot(sel, mid) + _dot(sel, lo)


def _exact_right(x, sel):
    hi, mid, lo = _split3(x)
    return _dot(hi, sel) + _dot(mid, sel) + _dot(lo, sel)


def _silu(x):
    return x * jax.nn.sigmoid(x)


def _params(sem):
    return pltpu.CompilerParams(dimension_semantics=sem, vmem_limit_bytes=VMEM_LIMIT)


def _inproj_kernel(h_ref, g_ref, w_ref, o_ref, hn_ref):
    @pl.when(pl.program_id(1) == 0)
    def _():
        x = h_ref[...]
        ms = jnp.mean(x * x, axis=-1, keepdims=True)
        hn_ref[...] = (x * lax.rsqrt(ms + NORM_EPS) * g_ref[...]).astype(MXU_DTYPE)

    o_ref[...] = _dot(hn_ref[...], w_ref[...])


def _in_proj(h2d, gain, w_cat, tm):
    t, d = h2d.shape
    n = w_cat.shape[1]
    return pl.pallas_call(
        _inproj_kernel,
        grid=(t // tm, n // PROJ_TN),
        in_specs=[
            pl.BlockSpec((tm, d), lambda i, j: (i, 0)),
            pl.BlockSpec((1, d), lambda i, j: (0, 0)),
            pl.BlockSpec((d, PROJ_TN), lambda i, j: (0, j)),
        ],
        out_specs=pl.BlockSpec((tm, PROJ_TN), lambda i, j: (i, j)),
        out_shape=jax.ShapeDtypeStruct((t, n), F32),
        scratch_shapes=[pltpu.VMEM((tm, d), MXU_DTYPE)],
        compiler_params=_params(("arbitrary", "arbitrary")),
        name="in_proj",
    )(h2d, gain, w_cat)


def _fnet_kernel(x_ref, cs_ref, tw_ref, o_ref, r_ref, *, seq, kp, scale):
    @pl.when(pl.program_id(1) == 0)
    def _():
        tail = kp - seq
        if tail:
            r_ref[seq:kp, :] = jnp.zeros((tail, F_WIDTH), MXU_DTYPE)
            r_ref[kp + seq:, :] = jnp.zeros((tail, F_WIDTH), MXU_DTYPE)
        cs = cs_ref[...]
        for g in range(N_FGROUPS):
            cols = slice(g * FGROUP_DIM, (g + 1) * FGROUP_DIM)
            p = _dot(x_ref[:, cols].astype(MXU_DTYPE), cs)
            r_ref[0:seq, cols] = p[:, :FGROUP_DIM].astype(MXU_DTYPE)
            r_ref[kp:kp + seq, cols] = p[:, FGROUP_DIM:].astype(MXU_DTYPE)

    o_ref[...] = (_dot(tw_ref[...], r_ref[...]) * scale).astype(o_ref.dtype)


def _fnet(proj3, cs128, twid, tmf):
    b, seq, _ = proj3.shape
    kp = twid.shape[1] // 2
    scale = 1.0 / math.sqrt(seq * FGROUP_DIM)
    return pl.pallas_call(
        functools.partial(_fnet_kernel, seq=seq, kp=kp, scale=scale),
        grid=(b, seq // tmf),
        in_specs=[
            pl.BlockSpec((None, seq, F_WIDTH), lambda bi, i: (bi, 0, COL_FIN // F_WIDTH)),
            pl.BlockSpec((FGROUP_DIM, 2 * FGROUP_DIM), lambda bi, i: (0, 0)),
            pl.BlockSpec((tmf, 2 * kp), lambda bi, i: (i, 0)),
        ],
        out_specs=pl.BlockSpec((None, tmf, F_WIDTH), lambda bi, i: (bi, i, 0)),
        out_shape=jax.ShapeDtypeStruct((b, seq, F_WIDTH), MXU_DTYPE),
        scratch_shapes=[pltpu.VMEM((2 * kp, F_WIDTH), MXU_DTYPE)],
        compiler_params=_params(("arbitrary", "arbitrary")),
        name="fnet",
    )(proj3, cs128, twid)


def _aligned(x, m):
    return x if isinstance(x, int) else pl.multiple_of(x, m)


def _gdn_kernel(q_ref, k_ref, v_ref, z_ref, ab_ref, cwq_ref, cwk_ref, cwv_ref, prm_ref, ong_ref,
                o_ref,
                rawq_ref, rawk_ref, rawv_ref, qs_ref, ks_ref, vs_ref, gsel_ref, grow_ref, gmix_ref,
                qe_ref, mp_ref, nn_ref, dd_ref, osum_ref,
                *, seq, lp, group, n_heads_total):
    step = pl.program_id(0)
    head = jnp.minimum(step, n_heads_total - 1) % N_HEADS
    nc = lp // CHUNK
    padr = lp - seq
    halo = SUBLANES
    row64 = lax.broadcasted_iota(jnp.int32, (CHUNK, LANES), 0)
    lane64 = lax.broadcasted_iota(jnp.int32, (CHUNK, LANES), 1)

    def for_groups(fn):
        n_full, rem = divmod(nc, group)
        if n_full:
            def body(i, carry):
                fn([i * group + j for j in range(group)])
                return carry
            lax.fori_loop(0, n_full, body, 0)
        if rem:
            fn([n_full * group + j for j in range(rem)])

    streams = ((rawq_ref, q_ref, cwq_ref, qs_ref, HEAD_DIM ** -0.5),
               (rawk_ref, k_ref, cwk_ref, ks_ref, 1.0),
               (rawv_ref, v_ref, cwv_ref, vs_ref, None))
    def stage_conv_inputs():
        for raw_ref, src_ref, _, _, _ in streams:
            raw_ref[0:padr + halo, :] = jnp.zeros((padr + halo, LANES), F32)
            raw_ref[padr + halo:padr + halo + seq, :] = src_ref[...]
            raw_ref[lp + halo:lp + 2 * halo, :] = jnp.zeros((halo, LANES), F32)

    def conv_chunk(c):
        base = pl.multiple_of(c * CHUNK, CHUNK)
        for raw_ref, _, cw_ref, dst_ref, l2_scale in streams:
            x = raw_ref[pl.ds(base, CHUNK + 2 * halo), :]
            cw = cw_ref[...]
            acc = jnp.zeros((CHUNK, LANES), F32)
            for j in range(CONV_K):
                off = halo - (CONV_K - 1) // 2 + j
                acc = acc + cw[j:j + 1, :] * x[off:off + CHUNK, :]
            y = _silu(acc)
            y = jnp.where(row64 + base >= padr, y, 0.0)
            if l2_scale is not None:
                y = y * lax.rsqrt(jnp.sum(y * y, axis=-1, keepdims=True) + NORM_EPS) * l2_scale
            dst_ref[pl.ds(base, CHUNK), :] = y

    raw_ref = rawq_ref

    def stage_gate_inputs():
        raw_ref[0:padr, :] = jnp.zeros((padr, LANES), F32)
        raw_ref[padr:padr + seq, :] = ab_ref[...]

    neg_a = -jnp.exp(prm_ref[0:1, :])
    dt_bias = prm_ref[1:2, :]
    is_g = (lane64 < 8) | ((lane64 >= 16) & (lane64 < 24))
    is_beta = ((lane64 >= 8) & (lane64 < 16)) | ((lane64 >= 24) & (lane64 < 32))
    r128 = lax.broadcasted_iota(jnp.int32, (2 * CHUNK, CHUNK), 0)
    c128 = lax.broadcasted_iota(jnp.int32, (2 * CHUNK, CHUNK), 1)
    tri2 = jnp.where(((r128 < CHUNK) & (r128 >= c128)) | ((r128 >= CHUNK) & (r128 - CHUNK <= c128)),
                     1.0, 0.0).astype(MXU_DTYPE)
    sel_r = lax.broadcasted_iota(jnp.int32, (LANES, 4 * LANES), 0)
    sel_c = lax.broadcasted_iota(jnp.int32, (LANES, 4 * LANES), 1)
    n_gate_cols = 4 * N_HEADS
    sel_packed = jnp.where((sel_r < 3 * n_gate_cols)
                           & (sel_r % n_gate_cols == head + N_HEADS * (sel_c // LANES)), 1.0, 0.0).astype(MXU_DTYPE)
    tri2x3 = jnp.concatenate([tri2, tri2, tri2], axis=1)

    def gate_batch_group(cs):
        bases = [_aligned(c * CHUNK, CHUNK) for c in cs]
        gates, parts = [], []
        for base in bases:
            ab = raw_ref[pl.ds(base, CHUNK), :]
            g = neg_a * jax.nn.softplus(ab + dt_bias)
            beta = jax.nn.sigmoid(ab)
            gt = jnp.where(is_g, g, jnp.where(is_beta, beta, 0.0))
            gt = jnp.where(row64 + base >= padr, gt, 0.0)
            gates.append(gt)
            parts.append(jnp.concatenate(_split3(gt), axis=0))
        cums = [_dot(tri2x3, p) for p in parts]
        for c, base, cm, gt in zip(cs, bases, cums, gates):
            m = jnp.where(lane64 < 8, cm[:CHUNK], jnp.where((lane64 >= 16) & (lane64 < 24), cm[CHUNK:], gt))
            gmix_ref[pl.ds(base, CHUNK), :] = m
            grow_ref[pl.ds(_aligned(c * LANES, LANES), LANES), :] = jnp.concatenate([m, m], axis=0).T
        return None

    def gate_head_group(cs):
        rows = len(cs) * CHUNK
        base = _aligned(cs[0] * CHUNK, CHUNK)
        hi, mid, lo = _split3(gmix_ref[pl.ds(base, rows), :])
        packed = (hi.astype(F32) + pltpu.roll(mid.astype(F32), n_gate_cols, 1)
                  + pltpu.roll(lo.astype(F32), 2 * n_gate_cols, 1)).astype(MXU_DTYPE)
        gsel_ref[pl.ds(base, rows), :] = _dot(packed, sel_packed)
        return None

    fwd_half = lane64 < CHUNK
    col64 = lane64 & (CHUNK - 1)
    incl2 = (fwd_half & (row64 >= col64)) | (~fwd_half & (row64 <= col64))
    strict2 = (fwd_half & (row64 > col64)) | (~fwd_half & (row64 < col64))
    eye2 = jnp.where(row64 == col64, 1.0, 0.0).astype(F32)

    def blockdiag(p):
        return jnp.concatenate([jnp.where(fwd_half, p, 0.0), jnp.where(fwd_half, 0.0, p)],
                               axis=0).astype(MXU_DTYPE)

    def chunk_group(cs):
        n = len(cs)
        bases = [_aligned(c * CHUNK, CHUNK) for c in cs]
        cb128 = [_aligned(c * LANES, LANES) for c in cs]
        cb8 = [_aligned(c * SUBLANES, SUBLANES) for c in cs]

        def gate_cols(j):
            gs = gsel_ref[pl.ds(bases[j], CHUNK), :]
            return gs[:, 0:LANES], gs[:, LANES:2 * LANES], gs[:, 2 * LANES:3 * LANES], gs[:, 3 * LANES:]

        kq = []
        for j in range(n):
            k16 = ks_ref[pl.ds(bases[j], CHUNK), :].astype(MXU_DTYPE)
            q16 = qs_ref[pl.ds(bases[j], CHUNK), :].astype(MXU_DTYPE)
            kq.append(_dot_nt(jnp.concatenate([k16, q16], axis=0), jnp.concatenate([k16, k16], axis=0)))

        ts, ps, qkds = [], [], []
        for j in range(n):
            gf, bf, gb, bb = gate_cols(j)
            row_f = grow_ref[pl.ds(cb128[j] + head, 1), :]
            row_b = grow_ref[pl.ds(cb128[j] + 2 * N_HEADS + head, 1), :]
            g_col = jnp.where(fwd_half, gf, gb)
            g_row = jnp.where(fwd_half, row_f, row_b)
            decay = jnp.where(incl2, jnp.exp(jnp.where(incl2, g_col - g_row, 0.0)), 0.0)
            a2 = jnp.where(strict2, kq[j][:CHUNK] * jnp.where(fwd_half, bf, bb) * decay, 0.0)
            qkds.append(jnp.where(incl2, kq[j][CHUNK:] * decay, 0.0))
            ts.append(eye2 - a2)
            ps.append(a2)

        ps = [_dot(p.astype(MXU_DTYPE), blockdiag(p)) for p in ps]
        for _ in range(int(math.log2(CHUNK)) - 2):
            tps = [_dot(jnp.concatenate([t, p], axis=0).astype(MXU_DTYPE), blockdiag(p)) for t, p in zip(ts, ps)]
            ts = [t + tp[:CHUNK] for t, tp in zip(ts, tps)]
            ps = [tp[CHUNK:] for tp in tps]
        ts = [t + _dot(t.astype(MXU_DTYPE), blockdiag(p)) for t, p in zip(ts, ps)]

        uws = []
        for j in range(n):
            gf, bf, gb, bb = gate_cols(j)
            k = ks_ref[pl.ds(bases[j], CHUNK), :]
            v = vs_ref[pl.ds(bases[j], CHUNK), :]
            x_f = jnp.concatenate([v * bf, k * bf * jnp.exp(gf)], axis=1)
            x_b = jnp.concatenate([v * bb, k * bb * jnp.exp(gb)], axis=1)
            uw = _dot(blockdiag(ts[j]), jnp.concatenate([x_f, x_b], axis=0).astype(MXU_DTYPE))
            uws.append(uw.astype(MXU_DTYPE))

        xxs = [_dot(blockdiag(qkds[j]), uws[j]) for j in range(n)]
        for j in range(n):
            gf, _, gb, _ = gate_cols(j)
            k = ks_ref[pl.ds(bases[j], CHUNK), :]
            q = qs_ref[pl.ds(bases[j], CHUNK), :]
            xx = xxs[j]
            osum_ref[pl.ds(bases[j], CHUNK), :] = xx[:CHUNK, :LANES] + xx[CHUNK:, :LANES]
            for d, gc in enumerate((gf, gb)):
                rows = slice(d * CHUNK, (d + 1) * CHUNK)
                g_last = gc[CHUNK - 1:CHUNK, :] if d == 0 else gc[0:1, :]
                k_dec = k * jnp.exp(g_last - gc)
                mn = _dot_tn(k_dec.astype(MXU_DTYPE), uws[j][rows])
                qe_ref[d, pl.ds(bases[j], CHUNK), :] = (q * jnp.exp(gc) - xx[rows, LANES:]).astype(MXU_DTYPE)
                nn_ref[d, pl.ds(cb128[j], HEAD_DIM), :] = mn[:, :LANES]
                mp_ref[d, pl.ds(cb128[j], HEAD_DIM), :] = mn[:, LANES:].astype(MXU_DTYPE)
                dd_ref[d, pl.ds(cb8[j], SUBLANES), :] = jnp.broadcast_to(jnp.exp(g_last), (SUBLANES, LANES))
        return None

    def scan_step(i, carry):
        new = []
        for d in range(2):
            s = carry[d]
            c = i if d == 0 else nc - 1 - i
            base = pl.multiple_of(c * CHUNK, CHUNK)
            cbase = pl.multiple_of(c * HEAD_DIM, HEAD_DIM)
            dbase = pl.multiple_of(c * SUBLANES, SUBLANES)
            s16 = s.astype(MXU_DTYPE)
            osum_ref[pl.ds(base, CHUNK), :] += _dot(qe_ref[d, pl.ds(base, CHUNK), :], s16)
            dec = dd_ref[d, pl.ds(dbase, 1), :]
            new.append(dec * s - _dot(mp_ref[d, pl.ds(cbase, HEAD_DIM), :], s16)
                       + nn_ref[d, pl.ds(cbase, HEAD_DIM), :])
        return tuple(new)

    zero_state = jnp.zeros((HEAD_DIM, HEAD_DIM), F32)

    def write_output():
        gain = ong_ref[...]

        def finish(o, z):
            ms = jnp.mean(o * o, axis=-1, keepdims=True)
            return (o * lax.rsqrt(ms + NORM_EPS) * gain * _silu(z)).astype(o_ref.dtype)

        first = CHUNK - padr if padr else 0
        if first:
            o_ref[0:first, :] = finish(osum_ref[padr:CHUNK, :], z_ref[0:first, :])

        def out_body(c, carry):
            base = pl.multiple_of(c * CHUNK, CHUNK)
            dst = pl.multiple_of(c * CHUNK - padr, SUBLANES)
            o_ref[pl.ds(dst, CHUNK), :] = finish(osum_ref[pl.ds(base, CHUNK), :], z_ref[pl.ds(dst, CHUNK), :])
            return carry

        lax.fori_loop(1 if first else 0, nc, out_body, 0, unroll=4)

    def conv_only(c, carry):
        conv_chunk(c)
        return carry

    def conv_and_scan(c, carry):
        conv_chunk(c)
        return scan_step(c, carry)

    @pl.when(step == 0)
    def _():
        stage_conv_inputs()
        lax.fori_loop(0, nc, conv_only, 0, unroll=3)

    @pl.when((step > 0) & (step < n_heads_total))
    def _():
        stage_conv_inputs()
        lax.fori_loop(0, nc, conv_and_scan, (zero_state, zero_state), unroll=3)
        write_output()

    @pl.when(step == n_heads_total)
    def _():
        lax.fori_loop(0, nc, scan_step, (zero_state, zero_state))
        write_output()

    @pl.when((step < n_heads_total) & (head == 0))
    def _():
        stage_gate_inputs()
        for_groups(gate_batch_group)

    @pl.when(step < n_heads_total)
    def _():
        for_groups(gate_head_group)
        for_groups(chunk_group)


def _gdn(proj3, conv_w, gate_prm, out_norm_g):
    b, seq, _ = proj3.shape
    lp = -(-seq // CHUNK) * CHUNK
    nc = lp // CHUNK
    n_total = b * N_HEADS
    cur = lambda i: jnp.minimum(i, n_total - 1)
    prev = lambda i: jnp.maximum(i - 1, 0)
    col = lambda off: pl.BlockSpec((None, seq, LANES), lambda i: (cur(i) // N_HEADS, 0, off + cur(i) % N_HEADS))
    cw = lambda off: pl.BlockSpec((CONV_K, LANES), lambda i: (0, off + cur(i) % N_HEADS))
    return pl.pallas_call(
        functools.partial(_gdn_kernel, seq=seq, lp=lp, group=GDN_GROUP, n_heads_total=n_total),
        grid=(n_total + 1,),
        in_specs=[
            col(COL_QKV // LANES), col(COL_QKV // LANES + N_HEADS), col(COL_QKV // LANES + 2 * N_HEADS),
            pl.BlockSpec((None, seq, LANES),
                         lambda i: (prev(i) // N_HEADS, 0, COL_Z // LANES + prev(i) % N_HEADS)),
            pl.BlockSpec((None, seq, LANES), lambda i: (cur(i) // N_HEADS, 0, COL_AB // LANES)),
            cw(0), cw(N_HEADS), cw(2 * N_HEADS),
            pl.BlockSpec((SUBLANES, LANES), lambda i: (0, 0)),
            pl.BlockSpec((1, LANES), lambda i: (0, 0)),
        ],
        out_specs=pl.BlockSpec((None, seq, LANES), lambda i: (prev(i) // N_HEADS, 0, prev(i) % N_HEADS)),
        out_shape=jax.ShapeDtypeStruct((b, seq, N_HEADS * HEAD_DIM), MXU_DTYPE),
        scratch_shapes=[
            pltpu.VMEM((lp + 2 * SUBLANES, LANES), F32),
            pltpu.VMEM((lp + 2 * SUBLANES, LANES), F32),
            pltpu.VMEM((lp + 2 * SUBLANES, LANES), F32),
            pltpu.VMEM((lp, LANES), F32),
            pltpu.VMEM((lp, LANES), F32),
            pltpu.VMEM((lp, LANES), F32),
            pltpu.VMEM((lp, 4 * LANES), F32),
            pltpu.VMEM((nc * LANES, LANES), F32),
            pltpu.VMEM((lp, LANES), F32),
            pltpu.VMEM((2, lp, LANES), MXU_DTYPE),
            pltpu.VMEM((2, nc * HEAD_DIM, LANES), MXU_DTYPE),
            pltpu.VMEM((2, nc * HEAD_DIM, LANES), F32),
            pltpu.VMEM((2, nc * SUBLANES, LANES), F32),
            pltpu.VMEM((lp, LANES), F32),
        ],
        compiler_params=_params(("arbitrary",)),
        name="gdn",
    )(proj3, proj3, proj3, proj3, proj3, conv_w, conv_w, conv_w, gate_prm, out_norm_g)


def _merge_kernel(h_ref, og_ref, ym_ref, gf_ref, gd_ref, wf_ref, wd_ref, wo_ref, n2_ref, wr_ref, br_ref,
                  h1_ref, hn_ref, ridx_ref, rw_ref, cnt_ref):
    @pl.when(pl.program_id(0) == 0)
    def _():
        cnt_ref[...] = jnp.zeros_like(cnt_ref)

    y_f = _dot(ym_ref[...], wf_ref[...])
    y_d = _dot(og_ref[...], wd_ref[...])
    merged = jax.nn.sigmoid(gf_ref[...]) * y_f + jax.nn.sigmoid(gd_ref[...]) * y_d
    h1 = h_ref[...] + _dot(merged.astype(MXU_DTYPE), wo_ref[...])
    h1_ref[...] = h1
    ms = jnp.mean(h1 * h1, axis=-1, keepdims=True)
    hn = h1 * lax.rsqrt(ms + NORM_EPS) * n2_ref[...]
    hn_ref[...] = hn

    x_hi = hn.astype(MXU_DTYPE)
    x_lo = (hn - x_hi.astype(F32)).astype(MXU_DTYPE)
    w = wr_ref[...]
    w_hi = w.astype(MXU_DTYPE)
    w_lo = (w - w_hi.astype(F32)).astype(MXU_DTYPE)
    logits = _dot(x_hi, w_hi) + _dot(x_hi, w_lo) + _dot(x_lo, w_hi) + br_ref[...]

    tm = logits.shape[0]
    lane = lax.broadcasted_iota(jnp.int32, (tm, LANES), 1)
    neg_inf = -jnp.inf
    is_group = lane < N_GROUPS
    gl = jnp.where(is_group, logits, neg_inf)
    ge = jnp.exp(gl - jnp.max(gl, axis=-1, keepdims=True))
    gp = ge / jnp.sum(ge, axis=-1, keepdims=True)
    p_group = jnp.max(gp, axis=-1, keepdims=True)
    g_idx = jnp.min(jnp.where(is_group & (gp == p_group), lane, LANES), axis=-1, keepdims=True)

    in_group = (lane >= N_GROUPS) & (lane < N_GROUPS + N_EXPERTS) & (((lane - N_GROUPS) >> 3) == g_idx)
    el = jnp.where(in_group, logits, neg_inf)
    ee = jnp.exp(el - jnp.max(el, axis=-1, keepdims=True))
    ep = ee / jnp.sum(ee, axis=-1, keepdims=True)
    v1 = jnp.max(jnp.where(in_group, ep, -1.0), axis=-1, keepdims=True)
    i1 = jnp.min(jnp.where(in_group & (ep == v1), lane, LANES), axis=-1, keepdims=True)
    rest = in_group & (lane != i1)
    v2 = jnp.max(jnp.where(rest, ep, -1.0), axis=-1, keepdims=True)
    i2 = jnp.min(jnp.where(rest & (ep == v2), lane, LANES), axis=-1, keepdims=True)
    denom = v1 + v2
    w1 = p_group * (v1 / denom)
    w2 = p_group * (v2 / denom)
    e1 = i1 - N_GROUPS
    e2 = i2 - N_GROUPS
    ridx_ref[...] = jnp.where(lane == 0, e1, jnp.where(lane == 1, e2, 0))
    rw_ref[...] = jnp.where(lane == 0, w1, jnp.where(lane == 1, w2, 0.0))
    onehots = jnp.where((lane == e1) | (lane == e2), 1.0, 0.0)
    cnt_ref[...] += jnp.broadcast_to(jnp.sum(onehots, axis=0, keepdims=True), cnt_ref.shape)


def _merge(h2d, og2d, ymix2d, proj2d, w_fourier, w_delta, w_out, norm2_g, w_router, b_router, tm):
    t, d = h2d.shape
    row = lambda w: pl.BlockSpec((tm, w), lambda i: (i, 0))
    const = lambda shape: pl.BlockSpec(shape, lambda i: (0, 0))
    return pl.pallas_call(
        _merge_kernel,
        grid=(t // tm,),
        in_specs=[
            row(d), row(d), row(F_WIDTH),
            pl.BlockSpec((tm, d), lambda i: (i, COL_GATE_F // d)),
            pl.BlockSpec((tm, d), lambda i: (i, COL_GATE_D // d)),
            const((F_WIDTH, d)), const((d, d)), const((d, d)), const((1, d)),
            const((d, LANES)), const((1, LANES)),
        ],
        out_specs=[row(d), row(d), row(LANES), row(LANES), const((SUBLANES, LANES))],
        out_shape=[
            jax.ShapeDtypeStruct((t, d), F32),
            jax.ShapeDtypeStruct((t, d), F32),
            jax.ShapeDtypeStruct((t, LANES), jnp.int32),
            jax.ShapeDtypeStruct((t, LANES), F32),
            jax.ShapeDtypeStruct((SUBLANES, LANES), F32),
        ],
        compiler_params=_params(("arbitrary",)),
        name="merge",
    )(h2d, og2d, ymix2d, proj2d, proj2d, w_fourier, w_delta, w_out, norm2_g, w_router, b_router)


def _plan_kernel(ridx_ref, cnt_ref, dest_ref, seg_ref, base_ref, run_ref):
    tt = ridx_ref.shape[0]

    @pl.when(pl.program_id(0) == 0)
    def _():
        cnt = cnt_ref[...].astype(jnp.int32)
        n_blk = (cnt + (EXPERT_BLOCK - 1)) // EXPERT_BLOCK
        padded = (n_blk * EXPERT_BLOCK).astype(F32)
        r = lax.broadcasted_iota(jnp.int32, (LANES, LANES), 0)
        c = lax.broadcasted_iota(jnp.int32, (LANES, LANES), 1)
        before = jnp.where(r < c, 1.0, 0.0).astype(MXU_DTYPE)
        start = _exact_right(padded, before)
        base_ref[...] = start
        run_ref[...] = jnp.zeros_like(run_ref)
        seg_ref[0:SUBLANES, :] = start.astype(jnp.int32)
        seg_ref[SUBLANES:2 * SUBLANES, :] = n_blk
        seg_ref[2 * SUBLANES:, :] = cnt

    lane = lax.broadcasted_iota(jnp.int32, (tt, LANES), 1)
    ridx = ridx_ref[...]
    oh0 = jnp.where(lane == ridx[:, 0:1], 1.0, 0.0)
    oh1 = jnp.where(lane == ridx[:, 1:2], 1.0, 0.0)
    r = lax.broadcasted_iota(jnp.int32, (tt, tt), 0)
    c = lax.broadcasted_iota(jnp.int32, (tt, tt), 1)
    earlier = jnp.where(c < r, 1.0, 0.0).astype(MXU_DTYPE)
    prefix = _dot(earlier, jnp.concatenate([oh0, oh1], axis=1).astype(MXU_DTYPE))
    c0 = jnp.sum(oh0, axis=0, keepdims=True)
    c1 = jnp.sum(oh1, axis=0, keepdims=True)
    first = base_ref[0:1, :] + run_ref[0:1, :]
    d0 = jnp.sum(oh0 * (prefix[:, :LANES] + first), axis=-1, keepdims=True)
    d1 = jnp.sum(oh1 * (prefix[:, LANES:] + first + c0), axis=-1, keepdims=True)
    run_ref[...] += jnp.broadcast_to(c0 + c1, run_ref.shape)
    dest_ref[...] = jnp.where(lane == 0, d0, jnp.where(lane == 1, d1, 0.0)).astype(jnp.int32)


def _plan(ridx, cnt, tt):
    t = ridx.shape[0]
    return pl.pallas_call(
        _plan_kernel,
        grid=(t // tt,),
        in_specs=[pl.BlockSpec((tt, LANES), lambda i: (i, 0)),
                  pl.BlockSpec((SUBLANES, LANES), lambda i: (0, 0))],
        out_specs=[pl.BlockSpec((tt, LANES), lambda i: (i, 0)),
                   pl.BlockSpec((3 * SUBLANES, LANES), lambda i: (0, 0))],
        out_shape=[jax.ShapeDtypeStruct((t, LANES), jnp.int32),
                   jax.ShapeDtypeStruct((3 * SUBLANES, LANES), jnp.int32)],
        scratch_shapes=[pltpu.VMEM((SUBLANES, LANES), F32), pltpu.VMEM((SUBLANES, LANES), F32)],
        compiler_params=_params(("arbitrary",)),
        name="plan",
    )(ridx, cnt)


def _scatter_kernel(dest_ref, hn_ref, xb_in_ref, xb_ref, sem, *, ts):
    del xb_in_ref

    def issue(t, carry):
        src = hn_ref.at[pl.ds(t, 1)]
        for k in range(TOP_K):
            pltpu.make_async_copy(src, xb_ref.at[pl.ds(dest_ref[0, TOP_K * t + k], 1)], sem).start()
        return carry

    lax.fori_loop(0, ts, issue, 0, unroll=DMA_UNROLL)
    for _ in range(TOP_K):
        pltpu.make_async_copy(hn_ref, xb_ref.at[pl.ds(0, ts)], sem).wait()


def _scatter(dest_flat, hn2d, xb_zero, ts):
    t, d = hn2d.shape
    return pl.pallas_call(
        functools.partial(_scatter_kernel, ts=ts),
        grid=(t // ts,),
        in_specs=[pl.BlockSpec((None, 1, TOP_K * ts), lambda i: (i, 0, 0), memory_space=pltpu.SMEM),
                  pl.BlockSpec((ts, d), lambda i: (i, 0)),
                  pl.BlockSpec(memory_space=pl.ANY)],
        out_specs=pl.BlockSpec(memory_space=pl.ANY),
        out_shape=jax.ShapeDtypeStruct(xb_zero.shape, F32),
        scratch_shapes=[pltpu.SemaphoreType.DMA(())],
        input_output_aliases={2: 0},
        compiler_params=_params(("arbitrary",)),
        name="scatter",
    )(dest_flat, hn2d, xb_zero)


def _expert_kernel(start_ref, nblk_ref, xb_ref, wg_ref, wu_ref, wd_ref, yb_ref,
                   wg16_ref, wu16_ref, wd16_ref, xbuf_ref, ybuf_ref, sem_in, sem_out):
    expert = pl.program_id(0)
    n = nblk_ref[expert]
    row0 = start_ref[expert]

    def rows(i):
        return pl.ds(pl.multiple_of(row0 + i * EXPERT_BLOCK, EXPERT_BLOCK), EXPERT_BLOCK)

    def x_copy(i, slot):
        return pltpu.make_async_copy(xb_ref.at[rows(i)], xbuf_ref.at[slot], sem_in.at[slot])

    def y_copy(i, slot):
        return pltpu.make_async_copy(ybuf_ref.at[slot], yb_ref.at[rows(i)], sem_out.at[slot])

    @pl.when(n > 0)
    def _():
        x_copy(0, 0).start()
        wg16_ref[...] = wg_ref[...].astype(MXU_DTYPE)
        wu16_ref[...] = wu_ref[...].astype(MXU_DTYPE)
        wd16_ref[...] = wd_ref[...].astype(MXU_DTYPE)

        def body(i, carry):
            slot = i % 2
            x_copy(i, slot).wait()

            @pl.when(i + 1 < n)
            def _():
                x_copy(i + 1, 1 - slot).start()

            @pl.when(i >= 2)
            def _():
                y_copy(i - 2, slot).wait()

            x = xbuf_ref[slot].astype(MXU_DTYPE)
            gate = _dot(x, wg16_ref[...])
            up = _dot(x, wu16_ref[...])
            ybuf_ref[slot] = _dot((_silu(gate) * up).astype(MXU_DTYPE), wd16_ref[...])
            y_copy(i, slot).start()
            return carry

        lax.fori_loop(0, n, body, 0)

        @pl.when(n >= 2)
        def _():
            y_copy(n - 2, n % 2).wait()

        y_copy(n - 1, (n - 1) % 2).wait()


def _experts(seg_start, seg_blocks, xb, w_gate, w_up, w_down):
    n_slots, d = xb.shape
    n_exp, _, de = w_gate.shape
    grid_spec = pltpu.PrefetchScalarGridSpec(
        num_scalar_prefetch=2,
        grid=(n_exp,),
        in_specs=[
            pl.BlockSpec(memory_space=pl.ANY),
            pl.BlockSpec((None, d, de), lambda e, st, nb: (e, 0, 0)),
            pl.BlockSpec((None, d, de), lambda e, st, nb: (e, 0, 0)),
            pl.BlockSpec((None, de, d), lambda e, st, nb: (e, 0, 0)),
        ],
        out_specs=pl.BlockSpec(memory_space=pl.ANY),
        scratch_shapes=[
            pltpu.VMEM((d, de), MXU_DTYPE), pltpu.VMEM((d, de), MXU_DTYPE), pltpu.VMEM((de, d), MXU_DTYPE),
            pltpu.VMEM((2, EXPERT_BLOCK, d), F32), pltpu.VMEM((2, EXPERT_BLOCK, d), F32),
            pltpu.SemaphoreType.DMA((2,)), pltpu.SemaphoreType.DMA((2,)),
        ],
    )
    return pl.pallas_call(
        _expert_kernel,
        grid_spec=grid_spec,
        out_shape=jax.ShapeDtypeStruct((n_slots, d), F32),
        input_output_aliases={2: 0},
        compiler_params=_params(("arbitrary",)),
        name="experts",
    )(seg_start, seg_blocks, xb, w_gate, w_up, w_down)


def _combine_kernel(dest_ref, yb_ref, h1_ref, rw_ref, fg_ref, o_ref, buf_ref, sem, *, tc):
    def issue(t, carry):
        for k in range(TOP_K):
            pltpu.make_async_copy(yb_ref.at[pl.ds(dest_ref[0, TOP_K * t + k], 1)],
                                  buf_ref.at[k, pl.ds(t, 1)], sem).start()
        return carry

    lax.fori_loop(0, tc, issue, 0, unroll=DMA_UNROLL)
    for k in range(TOP_K):
        pltpu.make_async_copy(yb_ref.at[pl.ds(0, tc)], buf_ref.at[k], sem).wait()

    rw = rw_ref[...]
    h2 = h1_ref[...] + (buf_ref[0] * rw[:, 0:1] + buf_ref[1] * rw[:, 1:2])
    ms = jnp.mean(h2 * h2, axis=-1, keepdims=True)
    o_ref[...] = h2 * lax.rsqrt(ms + NORM_EPS) * fg_ref[...]


def _combine(dest_flat, yb, h1, rw, final_g, tc):
    t, d = h1.shape
    return pl.pallas_call(
        functools.partial(_combine_kernel, tc=tc),
        grid=(t // tc,),
        in_specs=[pl.BlockSpec((None, 1, TOP_K * tc), lambda i: (i, 0, 0), memory_space=pltpu.SMEM),
                  pl.BlockSpec(memory_space=pl.ANY),
                  pl.BlockSpec((tc, d), lambda i: (i, 0)),
                  pl.BlockSpec((tc, LANES), lambda i: (i, 0)),
                  pl.BlockSpec((1, d), lambda i: (0, 0))],
        out_specs=pl.BlockSpec((tc, d), lambda i: (i, 0)),
        out_shape=jax.ShapeDtypeStruct((t, d), F32),
        scratch_shapes=[pltpu.VMEM((TOP_K, tc, d), F32), pltpu.SemaphoreType.DMA(())],
        compiler_params=_params(("arbitrary",)),
        name="combine",
    )(dest_flat, yb, h1, rw, final_g)


def _tile(total, candidates):
    for c in candidates:
        if total % c == 0:
            return c
    raise ValueError(f"no tile for {total}")


def _dft_tables(seq):
    kp = -(-seq // LANES) * LANES
    c = np.arange(FGROUP_DIM, dtype=np.int64)
    ang_c = (2.0 * np.pi / FGROUP_DIM) * ((c[:, None] * c[None, :]) % FGROUP_DIM)
    cs128 = np.concatenate([np.cos(ang_c), np.sin(ang_c)], axis=1)
    n = np.arange(seq, dtype=np.int64)
    ang_l = (2.0 * np.pi / seq) * ((n[:, None] * n[None, :]) % seq)
    twid = np.zeros((seq, 2 * kp), np.float32)
    twid[:, :seq] = np.cos(ang_l)
    twid[:, kp:kp + seq] = -np.sin(ang_l)
    return jnp.asarray(cs128.astype(MXU_DTYPE)), jnp.asarray(twid.astype(MXU_DTYPE))


def kernel(x, meta_tokens, norm1_g, w_in, conv_w, a_log_fwd, dt_bias_fwd, a_log_bwd, dt_bias_bwd, out_norm_g, w_fourier, w_delta, w_out, norm2_g, w_router_group, b_router_group, w_router_expert, b_router_expert, w_gate_e, w_up_e, w_down_e, final_norm_g):
    bsz, n_real, d = x.shape
    seq = N_META + n_real
    n_tok = bsz * seq
    assert (CHUNK - N_META) % CHUNK + seq == -(-seq // CHUNK) * CHUNK, "meta tokens must fill the tail of chunk 0"
    assert w_in.shape[0] == 1, "single trunk layer"
    layer = 0

    meta = jnp.broadcast_to(meta_tokens[None].astype(x.dtype), (bsz, N_META, d))
    h = jnp.concatenate([meta, x], axis=1).reshape(n_tok, d)

    w = w_in[layer]
    o_f, o_qkv, o_z, o_ab, o_gf, o_gd = 0, F_WIDTH, F_WIDTH + 3 * QK_WIDTH, F_WIDTH + 4 * QK_WIDTH, \
        F_WIDTH + 4 * QK_WIDTH + 4 * N_HEADS, F_WIDTH + 4 * QK_WIDTH + 4 * N_HEADS + d
    w_cat = jnp.concatenate(
        [w[:, o_qkv:o_z], w[:, o_z:o_ab], w[:, o_gf:o_gd], w[:, o_gd:o_gd + d], w[:, o_f:o_qkv], w[:, o_ab:o_gf],
         jnp.zeros((d, PROJ_WIDTH - w.shape[1]), w.dtype)], axis=1).astype(MXU_DTYPE)

    proj = _in_proj(h, norm1_g[layer][None, :], w_cat, _tile(n_tok, (2064, 104, 8)))
    proj3 = proj.reshape(bsz, seq, PROJ_WIDTH)

    cs128, twid = _dft_tables(seq)
    ymix = _fnet(proj3, cs128, twid, _tile(seq, (344, 104, 8)))

    gate_prm = jnp.zeros((SUBLANES, LANES), F32)
    gate_prm = gate_prm.at[0, 0:N_HEADS].set(a_log_fwd[layer]).at[0, 2 * N_HEADS:3 * N_HEADS].set(a_log_bwd[layer])
    gate_prm = gate_prm.at[1, 0:N_HEADS].set(dt_bias_fwd[layer]).at[1, 2 * N_HEADS:3 * N_HEADS].set(dt_bias_bwd[layer])
    og = _gdn(proj3, conv_w[layer], gate_prm, out_norm_g[layer][None, :])

    w_router = jnp.zeros((d, LANES), F32)
    w_router = w_router.at[:, 0:N_GROUPS].set(w_router_group[layer])
    w_router = w_router.at[:, N_GROUPS:N_GROUPS + N_EXPERTS].set(w_router_expert[layer])
    b_router = jnp.zeros((1, LANES), F32)
    b_router = b_router.at[0, 0:N_GROUPS].set(b_router_group[layer])
    b_router = b_router.at[0, N_GROUPS:N_GROUPS + N_EXPERTS].set(b_router_expert[layer])
    tt = _tile(n_tok, (384, 128))
    h1, hn, ridx, rw, cnt = _merge(
        h, og.reshape(n_tok, d), ymix.reshape(n_tok, F_WIDTH), proj,
        w_fourier[layer].astype(MXU_DTYPE), w_delta[layer].astype(MXU_DTYPE), w_out[layer].astype(MXU_DTYPE),
        norm2_g[layer][None, :], w_router, b_router, tt)

    n_assign = n_tok * TOP_K
    n_blocks = -(-n_assign // EXPERT_BLOCK) + N_EXPERTS
    dest, seg = _plan(ridx, cnt, tt)
    dest_flat = dest[:, :TOP_K].reshape(n_tok // tt, 1, TOP_K * tt)
    seg_start = seg[0, :N_EXPERTS]
    seg_blocks = seg[SUBLANES, :N_EXPERTS]

    xb = _scatter(dest_flat, hn, jnp.zeros((n_blocks * EXPERT_BLOCK, d), F32), tt)
    yb = _experts(seg_start, seg_blocks, xb, w_gate_e[layer], w_up_e[layer], w_down_e[layer])
    out = _combine(dest_flat, yb, h1, rw, final_norm_g[None, :], tt)
    return out.reshape(bsz, seq, d)[:, N_META:]
```

```python
import functools
import math

import jax
import jax.numpy as jnp
import numpy as np
from jax import lax
from jax.experimental import pallas as pl
from jax.experimental.pallas import tpu as pltpu

F32 = jnp.float32
MXU_DTYPE = jnp.bfloat16

N_META = 16
CHUNK = 64
N_FGROUPS = 4
FGROUP_DIM = 128
F_WIDTH = N_FGROUPS * FGROUP_DIM
N_HEADS = 8
HEAD_DIM = 128
QK_WIDTH = N_HEADS * HEAD_DIM
CONV_K = 5
N_GROUPS = 8
EXPERTS_PER_GROUP = 8
N_EXPERTS = N_GROUPS * EXPERTS_PER_GROUP
TOP_K = 2
EXPERT_BLOCK = 256
NORM_EPS = 1e-6
LANES = 128
SUBLANES = 8
VMEM_LIMIT = 52 * 1024 * 1024
GDN_GROUP = 11
DMA_UNROLL = 8
EXPERT_X_SLOTS = 4
EXPERT_Y_SLOTS = 3

COL_QKV = 0
COL_Z = 3 * QK_WIDTH
COL_GATE_F = COL_Z + QK_WIDTH
COL_GATE_D = COL_GATE_F + 1024
COL_FIN = COL_GATE_D + 1024
COL_AB = COL_FIN + F_WIDTH
PROJ_TN = 768
PROJ_WIDTH = -(-(COL_AB + LANES) // PROJ_TN) * PROJ_TN


def _dot(a, b):
    return jnp.dot(a, b, preferred_element_type=F32)


def _dot_nt(a, b):
    return lax.dot_general(a, b, (((1,), (1,)), ((), ())), preferred_element_type=F32)


def _dot_tn(a, b):
    return lax.dot_general(a, b, (((0,), (0,)), ((), ())), preferred_element_type=F32)


def _split3(x):
    hi = x.astype(MXU_DTYPE)
    r1 = x - hi.astype(F32)
    mid = r1.astype(MXU_DTYPE)
    lo = (r1 - mid.astype(F32)).astype(MXU_DTYPE)
    return hi, mid, lo


def _exact_left(sel, x):
    hi, mid, lo = _split3(x)
    return _dot(sel, hi) + _dot(sel, mid) + _dot(sel, lo)


def _exact_right(x, sel):
    hi, mid, lo = _split3(x)
    return _dot(hi, sel) + _dot(mid, sel) + _dot(lo, sel)


def _silu(x):
    return x * jax.nn.sigmoid(x)


def _params(sem):
    return pltpu.CompilerParams(dimension_semantics=sem, vmem_limit_bytes=VMEM_LIMIT)


def _inproj_kernel(h_ref, g_ref, w_ref, o_ref, hn_ref):
    @pl.when(pl.program_id(1) == 0)
    def _():
        x = h_ref[...]
        ms = jnp.mean(x * x, axis=-1, keepdims=True)
        hn_ref[...] = (x * lax.rsqrt(ms + NORM_EPS) * g_ref[...]).astype(MXU_DTYPE)

    o_ref[...] = _dot(hn_ref[...], w_ref[...])


def _in_proj(h2d, gain, w_cat, tm):
    t, d = h2d.shape
    n = w_cat.shape[1]
    return pl.pallas_call(
        _inproj_kernel,
        grid=(t // tm, n // PROJ_TN),
        in_specs=[
            pl.BlockSpec((tm, d), lambda i, j: (i, 0)),
            pl.BlockSpec((1, d), lambda i, j: (0, 0)),
            pl.BlockSpec((d, PROJ_TN), lambda i, j: (0, j)),
        ],
        out_specs=pl.BlockSpec((tm, PROJ_TN), lambda i, j: (i, j)),
        out_shape=jax.ShapeDtypeStruct((t, n), F32),
        scratch_shapes=[pltpu.VMEM((tm, d), MXU_DTYPE)],
        compiler_params=_params(("arbitrary", "arbitrary")),
        name="in_proj",
    )(h2d, gain, w_cat)


def _fnet_kernel(x_ref, cs_ref, tw_ref, o_ref, r_ref, *, seq, kp, scale):
    @pl.when(pl.program_id(1) == 0)
    def _():
        tail = kp - seq
        if tail:
            r_ref[seq:kp, :] = jnp.zeros((tail, F_WIDTH), MXU_DTYPE)
            r_ref[kp + seq:, :] = jnp.zeros((tail, F_WIDTH), MXU_DTYPE)
        cs = cs_ref[...]
        for g in range(N_FGROUPS):
            cols = slice(g * FGROUP_DIM, (g + 1) * FGROUP_DIM)
            p = _dot(x_ref[:, cols].astype(MXU_DTYPE), cs)
            r_ref[0:seq, cols] = p[:, :FGROUP_DIM].astype(MXU_DTYPE)
            r_ref[kp:kp + seq, cols] = p[:, FGROUP_DIM:].astype(MXU_DTYPE)

    o_ref[...] = (_dot(tw_ref[...], r_ref[...]) * scale).astype(o_ref.dtype)


def _fnet(proj3, cs128, twid, tmf):
    b, seq, _ = proj3.shape
    kp = twid.shape[1] // 2
    scale = 1.0 / math.sqrt(seq * FGROUP_DIM)
    return pl.pallas_call(
        functools.partial(_fnet_kernel, seq=seq, kp=kp, scale=scale),
        grid=(b, seq // tmf),
        in_specs=[
            pl.BlockSpec((None, seq, F_WIDTH), lambda bi, i: (bi, 0, COL_FIN // F_WIDTH)),
            pl.BlockSpec((FGROUP_DIM, 2 * FGROUP_DIM), lambda bi, i: (0, 0)),
            pl.BlockSpec((tmf, 2 * kp), lambda bi, i: (i, 0)),
        ],
        out_specs=pl.BlockSpec((None, tmf, F_WIDTH), lambda bi, i: (bi, i, 0)),
        out_shape=jax.ShapeDtypeStruct((b, seq, F_WIDTH), MXU_DTYPE),
        scratch_shapes=[pltpu.VMEM((2 * kp, F_WIDTH), MXU_DTYPE)],
        compiler_params=_params(("arbitrary", "arbitrary")),
        name="fnet",
    )(proj3, cs128, twid)


def _aligned(x, m):
    return x if isinstance(x, int) else pl.multiple_of(x, m)


def _gdn_kernel(q_ref, k_ref, v_ref, z_ref, ab_ref, cwq_ref, cwk_ref, cwv_ref, prm_ref, ong_ref,
                o_ref,
                rawq_ref, rawk_ref, rawv_ref, qs_ref, ks_ref, vs_ref, gsel_ref, grow_ref, gmix_ref,
                qe_ref, mp_ref, nn_ref, dd_ref, osum_ref,
                *, seq, lp, group, n_heads_total):
    step = pl.program_id(0)
    head = jnp.minimum(step, n_heads_total - 1) % N_HEADS
    nc = lp // CHUNK
    padr = lp - seq
    halo = SUBLANES
    row64 = lax.broadcasted_iota(jnp.int32, (CHUNK, LANES), 0)
    lane64 = lax.broadcasted_iota(jnp.int32, (CHUNK, LANES), 1)

    def for_groups(fn):
        n_full, rem = divmod(nc, group)
        if n_full:
            def body(i, carry):
                fn([i * group + j for j in range(group)])
                return carry
            lax.fori_loop(0, n_full, body, 0)
        if rem:
            fn([n_full * group + j for j in range(rem)])

    streams = ((rawq_ref, q_ref, cwq_ref, qs_ref, HEAD_DIM ** -0.5),
               (rawk_ref, k_ref, cwk_ref, ks_ref, 1.0),
               (rawv_ref, v_ref, cwv_ref, vs_ref, None))
    def stage_conv_inputs():
        for raw_ref, src_ref, _, _, _ in streams:
            raw_ref[0:padr + halo, :] = jnp.zeros((padr + halo, LANES), F32)
            raw_ref[padr + halo:padr + halo + seq, :] = src_ref[...]
            raw_ref[lp + halo:lp + 2 * halo, :] = jnp.zeros((halo, LANES), F32)

    def conv_chunk(c):
        base = pl.multiple_of(c * CHUNK, CHUNK)
        for raw_ref, _, cw_ref, dst_ref, l2_scale in streams:
            cw = cw_ref[...]
            acc = jnp.zeros((CHUNK, LANES), F32)
            for j in range(CONV_K):
                off = halo - (CONV_K - 1) // 2 + j
                acc = acc + cw[j:j + 1, :] * raw_ref[pl.ds(base + off, CHUNK), :]
            y = _silu(acc)
            y = jnp.where(row64 + base >= padr, y, 0.0)
            if l2_scale is not None:
                y = y * lax.rsqrt(jnp.sum(y * y, axis=-1, keepdims=True) + NORM_EPS) * l2_scale
            dst_ref[pl.ds(base, CHUNK), :] = y

    raw_ref = rawq_ref

    def stage_gate_inputs():
        raw_ref[0:padr, :] = jnp.zeros((padr, LANES), F32)
        raw_ref[padr:padr + seq, :] = ab_ref[...]

    neg_a = -jnp.exp(prm_ref[0:1, :])
    dt_bias = prm_ref[1:2, :]
    is_g = (lane64 < 8) | ((lane64 >= 16) & (lane64 < 24))
    is_beta = ((lane64 >= 8) & (lane64 < 16)) | ((lane64 >= 24) & (lane64 < 32))
    r128 = lax.broadcasted_iota(jnp.int32, (2 * CHUNK, CHUNK), 0)
    c128 = lax.broadcasted_iota(jnp.int32, (2 * CHUNK, CHUNK), 1)
    tri2 = jnp.where(((r128 < CHUNK) & (r128 >= c128)) | ((r128 >= CHUNK) & (r128 - CHUNK <= c128)),
                     1.0, 0.0).astype(MXU_DTYPE)
    sel_r = lax.broadcasted_iota(jnp.int32, (LANES, 4 * LANES), 0)
    sel_c = lax.broadcasted_iota(jnp.int32, (LANES, 4 * LANES), 1)
    n_gate_cols = 4 * N_HEADS
    sel_packed = jnp.where((sel_r < 3 * n_gate_cols)
                           & (sel_r % n_gate_cols == head + N_HEADS * (sel_c // LANES)), 1.0, 0.0).astype(MXU_DTYPE)
    tri2x3 = jnp.concatenate([tri2, tri2, tri2], axis=1)

    def gate_batch_group(cs):
        bases = [_aligned(c * CHUNK, CHUNK) for c in cs]
        gates, parts = [], []
        for base in bases:
            ab = raw_ref[pl.ds(base, CHUNK), :]
            g = neg_a * jax.nn.softplus(ab + dt_bias)
            beta = jax.nn.sigmoid(ab)
            gt = jnp.where(is_g, g, jnp.where(is_beta, beta, 0.0))
            gt = jnp.where(row64 + base >= padr, gt, 0.0)
            gates.append(gt)
            parts.append(jnp.concatenate(_split3(gt), axis=0))
        cums = [_dot(tri2x3, p) for p in parts]
        for c, base, cm, gt in zip(cs, bases, cums, gates):
            m = jnp.where(lane64 < 8, cm[:CHUNK], jnp.where((lane64 >= 16) & (lane64 < 24), cm[CHUNK:], gt))
            gmix_ref[pl.ds(base, CHUNK), :] = m
            grow_ref[pl.ds(_aligned(c * LANES, LANES), LANES), :] = jnp.concatenate([m, m], axis=0).T
        return None

    def gate_head_group(cs):
        rows = len(cs) * CHUNK
        base = _aligned(cs[0] * CHUNK, CHUNK)
        hi, mid, lo = _split3(gmix_ref[pl.ds(base, rows), :])
        packed = (hi.astype(F32) + pltpu.roll(mid.astype(F32), n_gate_cols, 1)
                  + pltpu.roll(lo.astype(F32), 2 * n_gate_cols, 1)).astype(MXU_DTYPE)
        gsel_ref[pl.ds(base, rows), :] = _dot(packed, sel_packed)
        return None

    fwd_half = lane64 < CHUNK
    col64 = lane64 & (CHUNK - 1)
    incl2 = (fwd_half & (row64 >= col64)) | (~fwd_half & (row64 <= col64))
    strict2 = (fwd_half & (row64 > col64)) | (~fwd_half & (row64 < col64))
    eye2 = jnp.where(row64 == col64, 1.0, 0.0).astype(F32)

    def blockdiag(p):
        return jnp.concatenate([jnp.where(fwd_half, p, 0.0), jnp.where(fwd_half, 0.0, p)],
                               axis=0).astype(MXU_DTYPE)

    def chunk_group(cs):
        n = len(cs)
        bases = [_aligned(c * CHUNK, CHUNK) for c in cs]
        cb128 = [_aligned(c * LANES, LANES) for c in cs]
        cb8 = [_aligned(c * SUBLANES, SUBLANES) for c in cs]

        def gate_cols(j):
            gs = gsel_ref[pl.ds(bases[j], CHUNK), :]
            return gs[:, 0:LANES], gs[:, LANES:2 * LANES], gs[:, 2 * LANES:3 * LANES], gs[:, 3 * LANES:]

        kq = []
        for j in range(n):
            k16 = ks_ref[pl.ds(bases[j], CHUNK), :].astype(MXU_DTYPE)
            q16 = qs_ref[pl.ds(bases[j], CHUNK), :].astype(MXU_DTYPE)
            kq.append(_dot_nt(jnp.concatenate([k16, q16], axis=0), jnp.concatenate([k16, k16], axis=0)))

        ts, ps, qkds = [], [], []
        for j in range(n):
            gf, bf, gb, bb = gate_cols(j)
            row_f = grow_ref[pl.ds(cb128[j] + head, 1), :]
            row_b = grow_ref[pl.ds(cb128[j] + 2 * N_HEADS + head, 1), :]
            g_col = jnp.where(fwd_half, gf, gb)
            g_row = jnp.where(fwd_half, row_f, row_b)
            decay = jnp.where(incl2, jnp.exp(jnp.where(incl2, g_col - g_row, 0.0)), 0.0)
            a2 = jnp.where(strict2, kq[j][:CHUNK] * jnp.where(fwd_half, bf, bb) * decay, 0.0)
            qkds.append(jnp.where(incl2, kq[j][CHUNK:] * decay, 0.0))
            ts.append(eye2 - a2)
            ps.append(a2)

        ps = [_dot(p.astype(MXU_DTYPE), blockdiag(p)) for p in ps]
        for _ in range(int(math.log2(CHUNK)) - 2):
            tps = [_dot(jnp.concatenate([t, p], axis=0).astype(MXU_DTYPE), blockdiag(p)) for t, p in zip(ts, ps)]
            ts = [t + tp[:CHUNK] for t, tp in zip(ts, tps)]
            ps = [tp[CHUNK:] for tp in tps]
        ts = [t + _dot(t.astype(MXU_DTYPE), blockdiag(p)) for t, p in zip(ts, ps)]

        uws = []
        for j in range(n):
            gf, bf, gb, bb = gate_cols(j)
            k = ks_ref[pl.ds(bases[j], CHUNK), :]
            v = vs_ref[pl.ds(bases[j], CHUNK), :]
            x_f = jnp.concatenate([v * bf, k * bf * jnp.exp(gf)], axis=1)
            x_b = jnp.concatenate([v * bb, k * bb * jnp.exp(gb)], axis=1)
            uw = _dot(blockdiag(ts[j]), jnp.concatenate([x_f, x_b], axis=0).astype(MXU_DTYPE))
            uws.append(uw.astype(MXU_DTYPE))

        xxs = [_dot(blockdiag(qkds[j]), uws[j]) for j in range(n)]
        for j in range(n):
            gf, _, gb, _ = gate_cols(j)
            k = ks_ref[pl.ds(bases[j], CHUNK), :]
            q = qs_ref[pl.ds(bases[j], CHUNK), :]
            xx = xxs[j]
            osum_ref[pl.ds(bases[j], CHUNK), :] = xx[:CHUNK, :LANES] + xx[CHUNK:, :LANES]
            for d, gc in enumerate((gf, gb)):
                rows = slice(d * CHUNK, (d + 1) * CHUNK)
                g_last = gc[CHUNK - 1:CHUNK, :] if d == 0 else gc[0:1, :]
                k_dec = k * jnp.exp(g_last - gc)
                mn = _dot_tn(k_dec.astype(MXU_DTYPE), uws[j][rows])
                qe_ref[d, pl.ds(bases[j], CHUNK), :] = (q * jnp.exp(gc) - xx[rows, LANES:]).astype(MXU_DTYPE)
                nn_ref[d, pl.ds(cb128[j], HEAD_DIM), :] = mn[:, :LANES]
                mp_ref[d, pl.ds(cb128[j], HEAD_DIM), :] = mn[:, LANES:].astype(MXU_DTYPE)
                dd_ref[d, pl.ds(cb8[j], SUBLANES), :] = jnp.broadcast_to(jnp.exp(g_last), (SUBLANES, LANES))
        return None

    def scan_step(i, carry):
        new = []
        for d in range(2):
            s = carry[d]
            c = i if d == 0 else nc - 1 - i
            base = pl.multiple_of(c * CHUNK, CHUNK)
            cbase = pl.multiple_of(c * HEAD_DIM, HEAD_DIM)
            dbase = pl.multiple_of(c * SUBLANES, SUBLANES)
            s16 = s.astype(MXU_DTYPE)
            osum_ref[pl.ds(base, CHUNK), :] += _dot(qe_ref[d, pl.ds(base, CHUNK), :], s16)
            dec = dd_ref[d, pl.ds(dbase, 1), :]
            new.append(dec * s - _dot(mp_ref[d, pl.ds(cbase, HEAD_DIM), :], s16)
                       + nn_ref[d, pl.ds(cbase, HEAD_DIM), :])
        return tuple(new)

    zero_state = jnp.zeros((HEAD_DIM, HEAD_DIM), F32)

    def write_output():
        gain = ong_ref[...]

        def finish(o, z):
            ms = jnp.mean(o * o, axis=-1, keepdims=True)
            return (o * lax.rsqrt(ms + NORM_EPS) * gain * _silu(z)).astype(o_ref.dtype)

        first = CHUNK - padr if padr else 0
        if first:
            o_ref[0:first, :] = finish(osum_ref[padr:CHUNK, :], z_ref[0:first, :])

        def out_body(c, carry):
            base = pl.multiple_of(c * CHUNK, CHUNK)
            dst = pl.multiple_of(c * CHUNK - padr, SUBLANES)
            o_ref[pl.ds(dst, CHUNK), :] = finish(osum_ref[pl.ds(base, CHUNK), :], z_ref[pl.ds(dst, CHUNK), :])
            return carry

        lax.fori_loop(1 if first else 0, nc, out_body, 0, unroll=4)

    def conv_only(c, carry):
        conv_chunk(c)
        return carry

    def conv_and_scan(c, carry):
        conv_chunk(c)
        return scan_step(c, carry)

    @pl.when(step == 0)
    def _():
        stage_conv_inputs()
        lax.fori_loop(0, nc, conv_only, 0, unroll=3)

    @pl.when((step > 0) & (step < n_heads_total))
    def _():
        stage_conv_inputs()
        lax.fori_loop(0, nc, conv_and_scan, (zero_state, zero_state), unroll=3)
        write_output()

    @pl.when(step == n_heads_total)
    def _():
        lax.fori_loop(0, nc, scan_step, (zero_state, zero_state))
        write_output()

    @pl.when((step < n_heads_total) & (head == 0))
    def _():
        stage_gate_inputs()
        for_groups(gate_batch_group)

    @pl.when(step < n_heads_total)
    def _():
        for_groups(gate_head_group)
        for_groups(chunk_group)


def _gdn(proj3, conv_w, gate_prm, out_norm_g):
    b, seq, _ = proj3.shape
    lp = -(-seq // CHUNK) * CHUNK
    nc = lp // CHUNK
    n_total = b * N_HEADS
    cur = lambda i: jnp.minimum(i, n_total - 1)
    prev = lambda i: jnp.maximum(i - 1, 0)
    col = lambda off: pl.BlockSpec((None, seq, LANES), lambda i: (cur(i) // N_HEADS, 0, off + cur(i) % N_HEADS))
    cw = lambda off: pl.BlockSpec((CONV_K, LANES), lambda i: (0, off + cur(i) % N_HEADS))
    return pl.pallas_call(
        functools.partial(_gdn_kernel, seq=seq, lp=lp, group=GDN_GROUP, n_heads_total=n_total),
        grid=(n_total + 1,),
        in_specs=[
            col(COL_QKV // LANES), col(COL_QKV // LANES + N_HEADS), col(COL_QKV // LANES + 2 * N_HEADS),
            pl.BlockSpec((None, seq, LANES),
                         lambda i: (prev(i) // N_HEADS, 0, COL_Z // LANES + prev(i) % N_HEADS)),
            pl.BlockSpec((None, seq, LANES), lambda i: (cur(i) // N_HEADS, 0, COL_AB // LANES)),
            cw(0), cw(N_HEADS), cw(2 * N_HEADS),
            pl.BlockSpec((SUBLANES, LANES), lambda i: (0, 0)),
            pl.BlockSpec((1, LANES), lambda i: (0, 0)),
        ],
        out_specs=pl.BlockSpec((None, seq, LANES), lambda i: (prev(i) // N_HEADS, 0, prev(i) % N_HEADS)),
        out_shape=jax.ShapeDtypeStruct((b, seq, N_HEADS * HEAD_DIM), MXU_DTYPE),
        scratch_shapes=[
            pltpu.VMEM((lp + 2 * SUBLANES, LANES), F32),
            pltpu.VMEM((lp + 2 * SUBLANES, LANES), F32),
            pltpu.VMEM((lp + 2 * SUBLANES, LANES), F32),
            pltpu.VMEM((lp, LANES), F32),
            pltpu.VMEM((lp, LANES), F32),
            pltpu.VMEM((lp, LANES), F32),
            pltpu.VMEM((lp, 4 * LANES), F32),
            pltpu.VMEM((nc * LANES, LANES), F32),
            pltpu.VMEM((lp, LANES), F32),
            pltpu.VMEM((2, lp, LANES), MXU_DTYPE),
            pltpu.VMEM((2, nc * HEAD_DIM, LANES), MXU_DTYPE),
            pltpu.VMEM((2, nc * HEAD_DIM, LANES), F32),
            pltpu.VMEM((2, nc * SUBLANES, LANES), F32),
            pltpu.VMEM((lp, LANES), F32),
        ],
        compiler_params=_params(("arbitrary",)),
        name="gdn",
    )(proj3, proj3, proj3, proj3, proj3, conv_w, conv_w, conv_w, gate_prm, out_norm_g)


def _merge_kernel(h_ref, og_ref, ym_ref, gf_ref, gd_ref, wf_ref, wd_ref, wo_ref, n2_ref, wr_ref, br_ref,
                  h1_ref, hn_ref, ridx_ref, rw_ref, cnt_ref):
    @pl.when(pl.program_id(0) == 0)
    def _():
        cnt_ref[...] = jnp.zeros_like(cnt_ref)

    y_f = _dot(ym_ref[...], wf_ref[...])
    y_d = _dot(og_ref[...], wd_ref[...])
    merged = jax.nn.sigmoid(gf_ref[...]) * y_f + jax.nn.sigmoid(gd_ref[...]) * y_d
    h1 = h_ref[...] + _dot(merged.astype(MXU_DTYPE), wo_ref[...])
    h1_ref[...] = h1
    ms = jnp.mean(h1 * h1, axis=-1, keepdims=True)
    hn = h1 * lax.rsqrt(ms + NORM_EPS) * n2_ref[...]
    hn_ref[...] = hn

    x_hi = hn.astype(MXU_DTYPE)
    x_lo = (hn - x_hi.astype(F32)).astype(MXU_DTYPE)
    w = wr_ref[...]
    w_hi = w.astype(MXU_DTYPE)
    w_lo = (w - w_hi.astype(F32)).astype(MXU_DTYPE)
    logits = _dot(x_hi, w_hi) + _dot(x_hi, w_lo) + _dot(x_lo, w_hi) + br_ref[...]

    tm = logits.shape[0]
    lane = lax.broadcasted_iota(jnp.int32, (tm, LANES), 1)
    neg_inf = -jnp.inf
    is_group = lane < N_GROUPS
    gl = jnp.where(is_group, logits, neg_inf)
    ge = jnp.exp(gl - jnp.max(gl, axis=-1, keepdims=True))
    gp = ge / jnp.sum(ge, axis=-1, keepdims=True)
    p_group = jnp.max(gp, axis=-1, keepdims=True)
    g_idx = jnp.min(jnp.where(is_group & (gp == p_group), lane, LANES), axis=-1, keepdims=True)

    in_group = (lane >= N_GROUPS) & (lane < N_GROUPS + N_EXPERTS) & (((lane - N_GROUPS) >> 3) == g_idx)
    el = jnp.where(in_group, logits, neg_inf)
    ee = jnp.exp(el - jnp.max(el, axis=-1, keepdims=True))
    ep = ee / jnp.sum(ee, axis=-1, keepdims=True)
    v1 = jnp.max(jnp.where(in_group, ep, -1.0), axis=-1, keepdims=True)
    i1 = jnp.min(jnp.where(in_group & (ep == v1), lane, LANES), axis=-1, keepdims=True)
    rest = in_group & (lane != i1)
    v2 = jnp.max(jnp.where(rest, ep, -1.0), axis=-1, keepdims=True)
    i2 = jnp.min(jnp.where(rest & (ep == v2), lane, LANES), axis=-1, keepdims=True)
    denom = v1 + v2
    w1 = p_group * (v1 / denom)
    w2 = p_group * (v2 / denom)
    e1 = i1 - N_GROUPS
    e2 = i2 - N_GROUPS
    ridx_ref[...] = jnp.where(lane == 0, e1, jnp.where(lane == 1, e2, 0))
    rw_ref[...] = jnp.where(lane == 0, w1, jnp.where(lane == 1, w2, 0.0))
    onehots = jnp.where((lane == e1) | (lane == e2), 1.0, 0.0)
    cnt_ref[...] += jnp.broadcast_to(jnp.sum(onehots, axis=0, keepdims=True), cnt_ref.shape)


def _merge(h2d, og2d, ymix2d, proj2d, w_fourier, w_delta, w_out, norm2_g, w_router, b_router, tm):
    t, d = h2d.shape
    row = lambda w: pl.BlockSpec((tm, w), lambda i: (i, 0))
    const = lambda shape: pl.BlockSpec(shape, lambda i: (0, 0))
    return pl.pallas_call(
        _merge_kernel,
        grid=(t // tm,),
        in_specs=[
            row(d), row(d), row(F_WIDTH),
            pl.BlockSpec((tm, d), lambda i: (i, COL_GATE_F // d)),
            pl.BlockSpec((tm, d), lambda i: (i, COL_GATE_D // d)),
            const((F_WIDTH, d)), const((d, d)), const((d, d)), const((1, d)),
            const((d, LANES)), const((1, LANES)),
        ],
        out_specs=[row(d), row(d), row(LANES), row(LANES), const((SUBLANES, LANES))],
        out_shape=[
            jax.ShapeDtypeStruct((t, d), F32),
            jax.ShapeDtypeStruct((t, d), F32),
            jax.ShapeDtypeStruct((t, LANES), jnp.int32),
            jax.ShapeDtypeStruct((t, LANES), F32),
            jax.ShapeDtypeStruct((SUBLANES, LANES), F32),
        ],
        compiler_params=_params(("arbitrary",)),
        name="merge",
    )(h2d, og2d, ymix2d, proj2d, proj2d, w_fourier, w_delta, w_out, norm2_g, w_router, b_router)


def _plan_kernel(ridx_ref, cnt_ref, dest_ref, seg_ref, base_ref, run_ref):
    tt = ridx_ref.shape[0]

    @pl.when(pl.program_id(0) == 0)
    def _():
        cnt = cnt_ref[...].astype(jnp.int32)
        n_blk = (cnt + (EXPERT_BLOCK - 1)) // EXPERT_BLOCK
        padded = (n_blk * EXPERT_BLOCK).astype(F32)
        r = lax.broadcasted_iota(jnp.int32, (LANES, LANES), 0)
        c = lax.broadcasted_iota(jnp.int32, (LANES, LANES), 1)
        before = jnp.where(r < c, 1.0, 0.0).astype(MXU_DTYPE)
        start = _exact_right(padded, before)
        base_ref[...] = start
        run_ref[...] = jnp.zeros_like(run_ref)
        seg_ref[0:SUBLANES, :] = start.astype(jnp.int32)
        seg_ref[SUBLANES:2 * SUBLANES, :] = n_blk
        seg_ref[2 * SUBLANES:, :] = cnt

    lane = lax.broadcasted_iota(jnp.int32, (tt, LANES), 1)
    ridx = ridx_ref[...]
    oh0 = jnp.where(lane == ridx[:, 0:1], 1.0, 0.0)
    oh1 = jnp.where(lane == ridx[:, 1:2], 1.0, 0.0)
    r = lax.broadcasted_iota(jnp.int32, (tt, tt), 0)
    c = lax.broadcasted_iota(jnp.int32, (tt, tt), 1)
    earlier = jnp.where(c < r, 1.0, 0.0).astype(MXU_DTYPE)
    prefix = _dot(earlier, jnp.concatenate([oh0, oh1], axis=1).astype(MXU_DTYPE))
    c0 = jnp.sum(oh0, axis=0, keepdims=True)
    c1 = jnp.sum(oh1, axis=0, keepdims=True)
    first = base_ref[0:1, :] + run_ref[0:1, :]
    d0 = jnp.sum(oh0 * (prefix[:, :LANES] + first), axis=-1, keepdims=True)
    d1 = jnp.sum(oh1 * (prefix[:, LANES:] + first + c0), axis=-1, keepdims=True)
    run_ref[...] += jnp.broadcast_to(c0 + c1, run_ref.shape)
    dest_ref[...] = jnp.where(lane == 0, d0, jnp.where(lane == 1, d1, 0.0)).astype(jnp.int32)


def _plan(ridx, cnt, tt):
    t = ridx.shape[0]
    return pl.pallas_call(
        _plan_kernel,
        grid=(t // tt,),
        in_specs=[pl.BlockSpec((tt, LANES), lambda i: (i, 0)),
                  pl.BlockSpec((SUBLANES, LANES), lambda i: (0, 0))],
        out_specs=[pl.BlockSpec((tt, LANES), lambda i: (i, 0)),
                   pl.BlockSpec((3 * SUBLANES, LANES), lambda i: (0, 0))],
        out_shape=[jax.ShapeDtypeStruct((t, LANES), jnp.int32),
                   jax.ShapeDtypeStruct((3 * SUBLANES, LANES), jnp.int32)],
        scratch_shapes=[pltpu.VMEM((SUBLANES, LANES), F32), pltpu.VMEM((SUBLANES, LANES), F32)],
        compiler_params=_params(("arbitrary",)),
        name="plan",
    )(ridx, cnt)


def _scatter_kernel(dest_ref, hn_ref, xb_in_ref, xb_ref, sem, *, ts):
    del xb_in_ref

    def issue(t, carry):
        src = hn_ref.at[pl.ds(t, 1)]
        for k in range(TOP_K):
            pltpu.make_async_copy(src, xb_ref.at[pl.ds(dest_ref[0, TOP_K * t + k], 1)], sem).start(priority=k % 2)
        return carry

    lax.fori_loop(0, ts, issue, 0, unroll=DMA_UNROLL)
    for _ in range(TOP_K):
        pltpu.make_async_copy(hn_ref, xb_ref.at[pl.ds(0, ts)], sem).wait()


def _scatter(dest_flat, hn2d, xb_zero, ts):
    t, d = hn2d.shape
    return pl.pallas_call(
        functools.partial(_scatter_kernel, ts=ts),
        grid=(t // ts,),
        in_specs=[pl.BlockSpec((None, 1, TOP_K * ts), lambda i: (i, 0, 0), memory_space=pltpu.SMEM),
                  pl.BlockSpec((ts, d), lambda i: (i, 0)),
                  pl.BlockSpec(memory_space=pl.ANY)],
        out_specs=pl.BlockSpec(memory_space=pl.ANY),
        out_shape=jax.ShapeDtypeStruct(xb_zero.shape, F32),
        scratch_shapes=[pltpu.SemaphoreType.DMA(())],
        input_output_aliases={2: 0},
        compiler_params=_params(("arbitrary",)),
        name="scatter",
    )(dest_flat, hn2d, xb_zero)


def _expert_kernel(start_ref, nblk_ref, xb_ref, wg_ref, wu_ref, wd_ref, yb_ref,
                   wg16_ref, wu16_ref, wd16_ref, xbuf_ref, ybuf_ref, sem_in, sem_out):
    expert = pl.program_id(0)
    last = pl.num_programs(0) - 1
    n = nblk_ref[expert]
    first = start_ref[expert] // EXPERT_BLOCK
    total = start_ref[last] // EXPERT_BLOCK + nblk_ref[last]

    def rows(g):
        return pl.ds(pl.multiple_of(g * EXPERT_BLOCK, EXPERT_BLOCK), EXPERT_BLOCK)

    def x_copy(g):
        slot = g % EXPERT_X_SLOTS
        return pltpu.make_async_copy(xb_ref.at[rows(g)], xbuf_ref.at[slot], sem_in.at[slot])

    def y_copy(g):
        slot = g % EXPERT_Y_SLOTS
        return pltpu.make_async_copy(ybuf_ref.at[slot], yb_ref.at[rows(g)], sem_out.at[slot])

    @pl.when(n > 0)
    def _():
        @pl.when(first == 0)
        def _():
            for j in range(EXPERT_X_SLOTS - 1):
                @pl.when(j < total)
                def _():
                    x_copy(j).start()

        wg16_ref[...] = wg_ref[...].astype(MXU_DTYPE)
        wu16_ref[...] = wu_ref[...].astype(MXU_DTYPE)
        wd16_ref[...] = wd_ref[...].astype(MXU_DTYPE)

        def body(i, carry):
            g = first + i

            @pl.when(g + (EXPERT_X_SLOTS - 1) < total)
            def _():
                x_copy(g + (EXPERT_X_SLOTS - 1)).start()

            x_copy(g).wait()

            @pl.when(g >= EXPERT_Y_SLOTS)
            def _():
                y_copy(g - EXPERT_Y_SLOTS).wait()

            x = xbuf_ref[g % EXPERT_X_SLOTS].astype(MXU_DTYPE)
            gate = _dot(x, wg16_ref[...])
            up = _dot(x, wu16_ref[...])
            ybuf_ref[g % EXPERT_Y_SLOTS] = _dot((_silu(gate) * up).astype(MXU_DTYPE), wd16_ref[...])
            y_copy(g).start()
            return carry

        lax.fori_loop(0, n, body, 0)

    @pl.when(expert == last)
    def _():
        for j in range(EXPERT_Y_SLOTS):
            @pl.when(total > j)
            def _():
                y_copy(total - 1 - j).wait()


def _experts(seg_start, seg_blocks, xb, w_gate, w_up, w_down):
    n_slots, d = xb.shape
    n_exp, _, de = w_gate.shape
    grid_spec = pltpu.PrefetchScalarGridSpec(
        num_scalar_prefetch=2,
        grid=(n_exp,),
        in_specs=[
            pl.BlockSpec(memory_space=pl.ANY),
            pl.BlockSpec((None, d, de), lambda e, st, nb: (e, 0, 0)),
            pl.BlockSpec((None, d, de), lambda e, st, nb: (e, 0, 0)),
            pl.BlockSpec((None, de, d), lambda e, st, nb: (e, 0, 0)),
        ],
        out_specs=pl.BlockSpec(memory_space=pl.ANY),
        scratch_shapes=[
            pltpu.VMEM((d, de), MXU_DTYPE), pltpu.VMEM((d, de), MXU_DTYPE), pltpu.VMEM((de, d), MXU_DTYPE),
            pltpu.VMEM((EXPERT_X_SLOTS, EXPERT_BLOCK, d), F32), pltpu.VMEM((EXPERT_Y_SLOTS, EXPERT_BLOCK, d), F32),
            pltpu.SemaphoreType.DMA((EXPERT_X_SLOTS,)), pltpu.SemaphoreType.DMA((EXPERT_Y_SLOTS,)),
        ],
    )
    return pl.pallas_call(
        _expert_kernel,
        grid_spec=grid_spec,
        out_shape=jax.ShapeDtypeStruct((n_slots, d), F32),
        input_output_aliases={2: 0},
        compiler_params=_params(("arbitrary",)),
        name="experts",
    )(seg_start, seg_blocks, xb, w_gate, w_up, w_down)


def _combine_kernel(dest_ref, yb_ref, h1_ref, rw_ref, fg_ref, o_ref, buf_ref, sem, *, tc):
    def issue(t, carry):
        for k in range(TOP_K):
            pltpu.make_async_copy(yb_ref.at[pl.ds(dest_ref[0, TOP_K * t + k], 1)],
                                  buf_ref.at[k, pl.ds(t, 1)], sem).start(priority=k % 2)
        return carry

    lax.fori_loop(0, tc, issue, 0, unroll=DMA_UNROLL)
    for k in range(TOP_K):
        pltpu.make_async_copy(yb_ref.at[pl.ds(0, tc)], buf_ref.at[k], sem).wait()

    rw = rw_ref[...]
    h2 = h1_ref[...] + (buf_ref[0] * rw[:, 0:1] + buf_ref[1] * rw[:, 1:2])
    ms = jnp.mean(h2 * h2, axis=-1, keepdims=True)
    o_ref[...] = h2 * lax.rsqrt(ms + NORM_EPS) * fg_ref[...]


def _combine(dest_flat, yb, h1, rw, final_g, tc):
    t, d = h1.shape
    return pl.pallas_call(
        functools.partial(_combine_kernel, tc=tc),
        grid=(t // tc,),
        in_specs=[pl.BlockSpec((None, 1, TOP_K * tc), lambda i: (i, 0, 0), memory_space=pltpu.SMEM),
                  pl.BlockSpec(memory_space=pl.ANY),
                  pl.BlockSpec((tc, d), lambda i: (i, 0)),
                  pl.BlockSpec((tc, LANES), lambda i: (i, 0)),
                  pl.BlockSpec((1, d), lambda i: (0, 0))],
        out_specs=pl.BlockSpec((tc, d), lambda i: (i, 0)),
        out_shape=jax.ShapeDtypeStruct((t, d), F32),
        scratch_shapes=[pltpu.VMEM((TOP_K, tc, d), F32), pltpu.SemaphoreType.DMA(())],
        compiler_params=_params(("arbitrary",)),
        name="combine",
    )(dest_flat, yb, h1, rw, final_g)


def _tile(total, candidates):
    for c in candidates:
        if total % c == 0:
            return c
    raise ValueError(f"no tile for {total}")


def _dft_tables(seq):
    kp = -(-seq // LANES) * LANES
    c = np.arange(FGROUP_DIM, dtype=np.int64)
    ang_c = (2.0 * np.pi / FGROUP_DIM) * ((c[:, None] * c[None, :]) % FGROUP_DIM)
    cs128 = np.concatenate([np.cos(ang_c), np.sin(ang_c)], axis=1)
    n = np.arange(seq, dtype=np.int64)
    ang_l = (2.0 * np.pi / seq) * ((n[:, None] * n[None, :]) % seq)
    twid = np.zeros((seq, 2 * kp), np.float32)
    twid[:, :seq] = np.cos(ang_l)
    twid[:, kp:kp + seq] = -np.sin(ang_l)
    return jnp.asarray(cs128.astype(MXU_DTYPE)), jnp.asarray(twid.astype(MXU_DTYPE))


def kernel(x, meta_tokens, norm1_g, w_in, conv_w, a_log_fwd, dt_bias_fwd, a_log_bwd, dt_bias_bwd, out_norm_g, w_fourier, w_delta, w_out, norm2_g, w_router_group, b_router_group, w_router_expert, b_router_expert, w_gate_e, w_up_e, w_down_e, final_norm_g):
    bsz, n_real, d = x.shape
    seq = N_META + n_real
    n_tok = bsz * seq
    assert (CHUNK - N_META) % CHUNK + seq == -(-seq // CHUNK) * CHUNK, "meta tokens must fill the tail of chunk 0"
    assert w_in.shape[0] == 1, "single trunk layer"
    layer = 0

    meta = jnp.broadcast_to(meta_tokens[None].astype(x.dtype), (bsz, N_META, d))
    h = jnp.concatenate([meta, x], axis=1).reshape(n_tok, d)

    w = w_in[layer]
    o_f, o_qkv, o_z, o_ab, o_gf, o_gd = 0, F_WIDTH, F_WIDTH + 3 * QK_WIDTH, F_WIDTH + 4 * QK_WIDTH, \
        F_WIDTH + 4 * QK_WIDTH + 4 * N_HEADS, F_WIDTH + 4 * QK_WIDTH + 4 * N_HEADS + d
    w_cat = jnp.concatenate(
        [w[:, o_qkv:o_z], w[:, o_z:o_ab], w[:, o_gf:o_gd], w[:, o_gd:o_gd + d], w[:, o_f:o_qkv], w[:, o_ab:o_gf],
         jnp.zeros((d, PROJ_WIDTH - w.shape[1]), w.dtype)], axis=1).astype(MXU_DTYPE)

    proj = _in_proj(h, norm1_g[layer][None, :], w_cat, _tile(n_tok, (2064, 104, 8)))
    proj3 = proj.reshape(bsz, seq, PROJ_WIDTH)

    cs128, twid = _dft_tables(seq)
    ymix = _fnet(proj3, cs128, twid, _tile(seq, (344, 104, 8)))

    gate_prm = jnp.zeros((SUBLANES, LANES), F32)
    gate_prm = gate_prm.at[0, 0:N_HEADS].set(a_log_fwd[layer]).at[0, 2 * N_HEADS:3 * N_HEADS].set(a_log_bwd[layer])
    gate_prm = gate_prm.at[1, 0:N_HEADS].set(dt_bias_fwd[layer]).at[1, 2 * N_HEADS:3 * N_HEADS].set(dt_bias_bwd[layer])
    og = _gdn(proj3, conv_w[layer], gate_prm, out_norm_g[layer][None, :])

    w_router = jnp.zeros((d, LANES), F32)
    w_router = w_router.at[:, 0:N_GROUPS].set(w_router_group[layer])
    w_router = w_router.at[:, N_GROUPS:N_GROUPS + N_EXPERTS].set(w_router_expert[layer])
    b_router = jnp.zeros((1, LANES), F32)
    b_router = b_router.at[0, 0:N_GROUPS].set(b_router_group[layer])
    b_router = b_router.at[0, N_GROUPS:N_GROUPS + N_EXPERTS].set(b_router_expert[layer])
    tt = _tile(n_tok, (384, 128))
    h1, hn, ridx, rw, cnt = _merge(
        h, og.reshape(n_tok, d), ymix.reshape(n_tok, F_WIDTH), proj,
        w_fourier[layer].astype(MXU_DTYPE), w_delta[layer].astype(MXU_DTYPE), w_out[layer].astype(MXU_DTYPE),
        norm2_g[layer][None, :], w_router, b_router, tt)

    n_assign = n_tok * TOP_K
    n_blocks = -(-n_assign // EXPERT_BLOCK) + N_EXPERTS
    dest, seg = _plan(ridx, cnt, tt)
    dest_flat = dest[:, :TOP_K].reshape(n_tok // tt, 1, TOP_K * tt)
    seg_start = seg[0, :N_EXPERTS]
    seg_blocks = seg[SUBLANES, :N_EXPERTS]

    xb = _scatter(dest_flat, hn, jnp.zeros((n_blocks * EXPERT_BLOCK, d), F32), tt)
    yb = _experts(seg_start, seg_blocks, xb, w_gate_e[layer], w_up_e[layer], w_down_e[layer])
    out = _combine(dest_flat, yb, h1, rw, final_norm_g[None, :], tt)
    return out.reshape(bsz, seq, d)[:, N_META:]
```

```python
import functools
import math

import jax
import jax.numpy as jnp
import numpy as np
from jax import lax
from jax.experimental import pallas as pl
from jax.experimental.pallas import tpu as pltpu

F32 = jnp.float32
MXU_DTYPE = jnp.bfloat16

N_META = 16
CHUNK = 64
N_FGROUPS = 4
FGROUP_DIM = 128
F_WIDTH = N_FGROUPS * FGROUP_DIM
N_HEADS = 8
HEAD_DIM = 128
QK_WIDTH = N_HEADS * HEAD_DIM
CONV_K = 5
N_GROUPS = 8
EXPERTS_PER_GROUP = 8
N_EXPERTS = N_GROUPS * EXPERTS_PER_GROUP
TOP_K = 2
EXPERT_BLOCK = 256
NORM_EPS = 1e-6
LANES = 128
SUBLANES = 8
VMEM_LIMIT = 52 * 1024 * 1024
GDN_GROUP = 11
DMA_UNROLL = 8
EXPERT_X_SLOTS = 4
EXPERT_Y_SLOTS = 3

COL_QKV = 0
COL_Z = 3 * QK_WIDTH
COL_GATE_F = COL_Z + QK_WIDTH
COL_GATE_D = COL_GATE_F + 1024
COL_FIN = COL_GATE_D + 1024
PROJ_WIDTH = COL_FIN + F_WIDTH
PROJ_TN = 512
PROJ_DTYPE = jnp.bfloat16


def _dot(a, b):
    return jnp.dot(a, b, preferred_element_type=F32)


def _dot_nt(a, b):
    return lax.dot_general(a, b, (((1,), (1,)), ((), ())), preferred_element_type=F32)


def _dot_tn(a, b):
    return lax.dot_general(a, b, (((0,), (0,)), ((), ())), preferred_element_type=F32)


def _split3(x):
    hi = x.astype(MXU_DTYPE)
    r1 = x - hi.astype(F32)
    mid = r1.astype(MXU_DTYPE)
    lo = (r1 - mid.astype(F32)).astype(MXU_DTYPE)
    return hi, mid, lo


def _exact_left(sel, x):
    hi, mid, lo = _split3(x)
    return _dot(sel, hi) + _dot(sel, mid) + _dot(sel, lo)


def _exact_right(x, sel):
    hi, mid, lo = _split3(x)
    return _dot(hi, sel) + _dot(mid, sel) + _dot(lo, sel)


def _silu(x):
    return x * jax.nn.sigmoid(x)


def _params(sem):
    return pltpu.CompilerParams(dimension_semantics=sem, vmem_limit_bytes=VMEM_LIMIT)


def _inproj_kernel(h_ref, g_ref, w_ref, wab_ref, o_ref, ab_ref, hn_ref):
    @pl.when(pl.program_id(1) == 0)
    def _():
        x = h_ref[...]
        ms = jnp.mean(x * x, axis=-1, keepdims=True)
        hn_ref[...] = (x * lax.rsqrt(ms + NORM_EPS) * g_ref[...]).astype(MXU_DTYPE)
        ab_ref[...] = _dot(hn_ref[...], wab_ref[...])

    o_ref[...] = _dot(hn_ref[...], w_ref[...]).astype(o_ref.dtype)


def _in_proj(h2d, gain, w_cat, w_ab, tm):
    t, d = h2d.shape
    n = w_cat.shape[1]
    return pl.pallas_call(
        _inproj_kernel,
        grid=(t // tm, n // PROJ_TN),
        in_specs=[
            pl.BlockSpec((tm, d), lambda i, j: (i, 0)),
            pl.BlockSpec((1, d), lambda i, j: (0, 0)),
            pl.BlockSpec((d, PROJ_TN), lambda i, j: (0, j)),
            pl.BlockSpec((d, LANES), lambda i, j: (0, 0)),
        ],
        out_specs=[pl.BlockSpec((tm, PROJ_TN), lambda i, j: (i, j)),
                   pl.BlockSpec((tm, LANES), lambda i, j: (i, 0))],
        out_shape=[jax.ShapeDtypeStruct((t, n), PROJ_DTYPE),
                   jax.ShapeDtypeStruct((t, LANES), F32)],
        scratch_shapes=[pltpu.VMEM((tm, d), MXU_DTYPE)],
        compiler_params=_params(("arbitrary", "arbitrary")),
        name="in_proj",
    )(h2d, gain, w_cat, w_ab)


def _fnet_kernel(x_ref, cs_ref, tw_ref, o_ref, r_ref, *, seq, kp, scale):
    @pl.when(pl.program_id(1) == 0)
    def _():
        tail = kp - seq
        if tail:
            r_ref[seq:kp, :] = jnp.zeros((tail, F_WIDTH), MXU_DTYPE)
            r_ref[kp + seq:, :] = jnp.zeros((tail, F_WIDTH), MXU_DTYPE)
        cs = cs_ref[...]
        for g in range(N_FGROUPS):
            cols = slice(g * FGROUP_DIM, (g + 1) * FGROUP_DIM)
            p = _dot(x_ref[:, cols].astype(MXU_DTYPE), cs)
            r_ref[0:seq, cols] = p[:, :FGROUP_DIM].astype(MXU_DTYPE)
            r_ref[kp:kp + seq, cols] = p[:, FGROUP_DIM:].astype(MXU_DTYPE)

    o_ref[...] = (_dot(tw_ref[...], r_ref[...]) * scale).astype(o_ref.dtype)


def _fnet(proj3, cs128, twid, tmf):
    b, seq, _ = proj3.shape
    kp = twid.shape[1] // 2
    scale = 1.0 / math.sqrt(seq * FGROUP_DIM)
    return pl.pallas_call(
        functools.partial(_fnet_kernel, seq=seq, kp=kp, scale=scale),
        grid=(b, seq // tmf),
        in_specs=[
            pl.BlockSpec((None, seq, F_WIDTH), lambda bi, i: (bi, 0, COL_FIN // F_WIDTH)),
            pl.BlockSpec((FGROUP_DIM, 2 * FGROUP_DIM), lambda bi, i: (0, 0)),
            pl.BlockSpec((tmf, 2 * kp), lambda bi, i: (i, 0)),
        ],
        out_specs=pl.BlockSpec((None, tmf, F_WIDTH), lambda bi, i: (bi, i, 0)),
        out_shape=jax.ShapeDtypeStruct((b, seq, F_WIDTH), MXU_DTYPE),
        scratch_shapes=[pltpu.VMEM((2 * kp, F_WIDTH), MXU_DTYPE)],
        compiler_params=_params(("arbitrary", "arbitrary")),
        name="fnet",
    )(proj3, cs128, twid)


def _aligned(x, m):
    return x if isinstance(x, int) else pl.multiple_of(x, m)


def _gdn_kernel(q_ref, k_ref, v_ref, z_ref, ab_ref, cwq_ref, cwk_ref, cwv_ref, prm_ref, ong_ref,
                o_ref,
                rawq_ref, rawk_ref, rawv_ref, qs_ref, ks_ref, vs_ref, gsel_ref, grow_ref, gmix_ref,
                qe_ref, mp_ref, nn_ref, dd_ref, osum_ref,
                *, seq, lp, group, n_heads_total):
    step = pl.program_id(0)
    head = jnp.minimum(step, n_heads_total - 1) % N_HEADS
    nc = lp // CHUNK
    padr = lp - seq
    halo = SUBLANES
    row64 = lax.broadcasted_iota(jnp.int32, (CHUNK, LANES), 0)
    lane64 = lax.broadcasted_iota(jnp.int32, (CHUNK, LANES), 1)

    def for_groups(fn, carry=0):
        n_full, rem = divmod(nc, group)
        if n_full:
            carry = lax.fori_loop(0, n_full, lambda i, c: fn([i * group + j for j in range(group)], c), carry)
        if rem:
            carry = fn([n_full * group + j for j in range(rem)], carry)
        return carry

    cur_half = step % 2
    qe_w, mp_w, nn_w, dd_w, osum_w = (r.at[cur_half] for r in (qe_ref, mp_ref, nn_ref, dd_ref, osum_ref))
    qe_r, mp_r, nn_r, dd_r, osum_r = (r.at[1 - cur_half] for r in (qe_ref, mp_ref, nn_ref, dd_ref, osum_ref))

    streams = ((rawq_ref, q_ref, cwq_ref, qs_ref, HEAD_DIM ** -0.5),
               (rawk_ref, k_ref, cwk_ref, ks_ref, 1.0),
               (rawv_ref, v_ref, cwv_ref, vs_ref, None))
    def stage_conv_inputs():
        for raw_ref, src_ref, _, _, _ in streams:
            raw_ref[0:padr + halo, :] = jnp.zeros((padr + halo, LANES), F32)
            raw_ref[padr + halo:padr + halo + seq, :] = src_ref[...].astype(F32)
            raw_ref[lp + halo:lp + 2 * halo, :] = jnp.zeros((halo, LANES), F32)

    def conv_chunk(c):
        base = pl.multiple_of(c * CHUNK, CHUNK)
        for raw_ref, _, cw_ref, dst_ref, l2_scale in streams:
            cw = cw_ref[...]
            acc = jnp.zeros((CHUNK, LANES), F32)
            for j in range(CONV_K):
                off = halo - (CONV_K - 1) // 2 + j
                acc = acc + cw[j:j + 1, :] * raw_ref[pl.ds(base + off, CHUNK), :]
            y = _silu(acc)
            y = jnp.where(row64 + base >= padr, y, 0.0)
            if l2_scale is not None:
                y = y * lax.rsqrt(jnp.sum(y * y, axis=-1, keepdims=True) + NORM_EPS) * l2_scale
            dst_ref[pl.ds(base, CHUNK), :] = y

    raw_ref = rawq_ref

    def stage_gate_inputs():
        raw_ref[0:padr, :] = jnp.zeros((padr, LANES), F32)
        raw_ref[padr:padr + seq, :] = ab_ref[...]

    neg_a = -jnp.exp(prm_ref[0:1, :])
    dt_bias = prm_ref[1:2, :]
    is_g = (lane64 < 8) | ((lane64 >= 16) & (lane64 < 24))
    is_beta = ((lane64 >= 8) & (lane64 < 16)) | ((lane64 >= 24) & (lane64 < 32))
    r128 = lax.broadcasted_iota(jnp.int32, (2 * CHUNK, CHUNK), 0)
    c128 = lax.broadcasted_iota(jnp.int32, (2 * CHUNK, CHUNK), 1)
    tri2 = jnp.where(((r128 < CHUNK) & (r128 >= c128)) | ((r128 >= CHUNK) & (r128 - CHUNK <= c128)),
                     1.0, 0.0).astype(MXU_DTYPE)
    sel_r = lax.broadcasted_iota(jnp.int32, (LANES, 4 * LANES), 0)
    sel_c = lax.broadcasted_iota(jnp.int32, (LANES, 4 * LANES), 1)
    n_gate_cols = 4 * N_HEADS
    sel_packed = jnp.where((sel_r < 3 * n_gate_cols)
                           & (sel_r % n_gate_cols == head + N_HEADS * (sel_c // LANES)), 1.0, 0.0).astype(MXU_DTYPE)
    tri2x3 = jnp.concatenate([tri2, tri2, tri2], axis=1)

    def gate_batch_group(cs, carry):
        bases = [_aligned(c * CHUNK, CHUNK) for c in cs]
        gates, parts = [], []
        for base in bases:
            ab = raw_ref[pl.ds(base, CHUNK), :]
            g = neg_a * jax.nn.softplus(ab + dt_bias)
            beta = jax.nn.sigmoid(ab)
            gt = jnp.where(is_g, g, jnp.where(is_beta, beta, 0.0))
            gt = jnp.where(row64 + base >= padr, gt, 0.0)
            gates.append(gt)
            parts.append(jnp.concatenate(_split3(gt), axis=0))
        cums = [_dot(tri2x3, p) for p in parts]
        for c, base, cm, gt in zip(cs, bases, cums, gates):
            m = jnp.where(lane64 < 8, cm[:CHUNK], jnp.where((lane64 >= 16) & (lane64 < 24), cm[CHUNK:], gt))
            gmix_ref[pl.ds(base, CHUNK), :] = m
            grow_ref[pl.ds(_aligned(c * LANES, LANES), LANES), :] = jnp.concatenate([m, m], axis=0).T
        return carry

    def gate_head_group(cs, carry):
        rows = len(cs) * CHUNK
        base = _aligned(cs[0] * CHUNK, CHUNK)
        hi, mid, lo = _split3(gmix_ref[pl.ds(base, rows), :])
        packed = (hi.astype(F32) + pltpu.roll(mid.astype(F32), n_gate_cols, 1)
                  + pltpu.roll(lo.astype(F32), 2 * n_gate_cols, 1)).astype(MXU_DTYPE)
        gsel_ref[pl.ds(base, rows), :] = _dot(packed, sel_packed)
        return carry

    fwd_half = lane64 < CHUNK
    col64 = lane64 & (CHUNK - 1)
    incl2 = (fwd_half & (row64 >= col64)) | (~fwd_half & (row64 <= col64))
    strict2 = (fwd_half & (row64 > col64)) | (~fwd_half & (row64 < col64))
    eye2 = jnp.where(row64 == col64, 1.0, 0.0).astype(F32)

    def blockdiag(p):
        return jnp.concatenate([jnp.where(fwd_half, p, 0.0), jnp.where(fwd_half, 0.0, p)],
                               axis=0).astype(MXU_DTYPE)

    n_neumann = int(math.log2(CHUNK)) - 2
    n_stages = n_neumann + 6

    def chunk_group(cs, carry, with_scan):
        n = len(cs)
        bases = [_aligned(c * CHUNK, CHUNK) for c in cs]
        cb128 = [_aligned(c * LANES, LANES) for c in cs]
        cb8 = [_aligned(c * SUBLANES, SUBLANES) for c in cs]
        stage = [0]

        def end_of_stage(carry):
            s = stage[0]
            stage[0] += 1
            if with_scan:
                for j in range(n):
                    if j * n_stages // n == s:
                        carry = scan_step(cs[j], carry)
            return carry

        def gate_cols(j):
            gs = gsel_ref[pl.ds(bases[j], CHUNK), :]
            return gs[:, 0:LANES], gs[:, LANES:2 * LANES], gs[:, 2 * LANES:3 * LANES], gs[:, 3 * LANES:]

        kq = []
        for j in range(n):
            k16 = ks_ref[pl.ds(bases[j], CHUNK), :].astype(MXU_DTYPE)
            q16 = qs_ref[pl.ds(bases[j], CHUNK), :].astype(MXU_DTYPE)
            kq.append(_dot_nt(jnp.concatenate([k16, q16], axis=0), jnp.concatenate([k16, k16], axis=0)))
        carry = end_of_stage(carry)

        ts, ps, qkds = [], [], []
        for j in range(n):
            gf, bf, gb, bb = gate_cols(j)
            row_f = grow_ref[pl.ds(cb128[j] + head, 1), :]
            row_b = grow_ref[pl.ds(cb128[j] + 2 * N_HEADS + head, 1), :]
            g_col = jnp.where(fwd_half, gf, gb)
            g_row = jnp.where(fwd_half, row_f, row_b)
            decay = jnp.where(incl2, jnp.exp(jnp.where(incl2, g_col - g_row, 0.0)), 0.0)
            a2 = jnp.where(strict2, kq[j][:CHUNK] * jnp.where(fwd_half, bf, bb) * decay, 0.0)
            qkds.append(jnp.where(incl2, kq[j][CHUNK:] * decay, 0.0))
            ts.append(eye2 - a2)
            ps.append(a2)

        ps = [_dot(p.astype(MXU_DTYPE), blockdiag(p)) for p in ps]
        carry = end_of_stage(carry)
        for _ in range(n_neumann):
            tps = [_dot(jnp.concatenate([t, p], axis=0).astype(MXU_DTYPE), blockdiag(p)) for t, p in zip(ts, ps)]
            carry = end_of_stage(carry)
            ts = [t + tp[:CHUNK] for t, tp in zip(ts, tps)]
            ps = [tp[CHUNK:] for tp in tps]
        ts = [t + _dot(t.astype(MXU_DTYPE), blockdiag(p)) for t, p in zip(ts, ps)]
        carry = end_of_stage(carry)

        uws = []
        for j in range(n):
            gf, bf, gb, bb = gate_cols(j)
            k = ks_ref[pl.ds(bases[j], CHUNK), :]
            v = vs_ref[pl.ds(bases[j], CHUNK), :]
            x_f = jnp.concatenate([v * bf, k * bf * jnp.exp(gf)], axis=1)
            x_b = jnp.concatenate([v * bb, k * bb * jnp.exp(gb)], axis=1)
            uw = _dot(blockdiag(ts[j]), jnp.concatenate([x_f, x_b], axis=0).astype(MXU_DTYPE))
            uws.append(uw.astype(MXU_DTYPE))
        carry = end_of_stage(carry)

        xxs = [_dot(blockdiag(qkds[j]), uws[j]) for j in range(n)]
        carry = end_of_stage(carry)
        for j in range(n):
            gf, _, gb, _ = gate_cols(j)
            k = ks_ref[pl.ds(bases[j], CHUNK), :]
            q = qs_ref[pl.ds(bases[j], CHUNK), :]
            xx = xxs[j]
            osum_w[pl.ds(bases[j], CHUNK), :] = xx[:CHUNK, :LANES] + xx[CHUNK:, :LANES]
            for d, gc in enumerate((gf, gb)):
                rows = slice(d * CHUNK, (d + 1) * CHUNK)
                g_last = gc[CHUNK - 1:CHUNK, :] if d == 0 else gc[0:1, :]
                k_dec = k * jnp.exp(g_last - gc)
                mn = _dot_tn(k_dec.astype(MXU_DTYPE), uws[j][rows])
                qe_w[d, pl.ds(bases[j], CHUNK), :] = (q * jnp.exp(gc) - xx[rows, LANES:]).astype(MXU_DTYPE)
                nn_w[d, pl.ds(cb128[j], HEAD_DIM), :] = mn[:, :LANES]
                mp_w[d, pl.ds(cb128[j], HEAD_DIM), :] = mn[:, LANES:].astype(MXU_DTYPE)
                dd_w[d, pl.ds(cb8[j], SUBLANES), :] = jnp.broadcast_to(jnp.exp(g_last), (SUBLANES, LANES))
        carry = end_of_stage(carry)
        assert stage[0] == n_stages
        return carry

    def scan_step(i, carry):
        new = []
        for d in range(2):
            s = carry[d]
            c = i if d == 0 else nc - 1 - i
            base = _aligned(c * CHUNK, CHUNK)
            cbase = _aligned(c * HEAD_DIM, HEAD_DIM)
            dbase = _aligned(c * SUBLANES, SUBLANES)
            s16 = s.astype(MXU_DTYPE)
            osum_r[pl.ds(base, CHUNK), :] += _dot(qe_r[d, pl.ds(base, CHUNK), :], s16)
            dec = dd_r[d, pl.ds(dbase, 1), :]
            new.append(dec * s - _dot(mp_r[d, pl.ds(cbase, HEAD_DIM), :], s16)
                       + nn_r[d, pl.ds(cbase, HEAD_DIM), :])
        return tuple(new)

    zero_state = jnp.zeros((HEAD_DIM, HEAD_DIM), F32)

    def write_output():
        gain = ong_ref[...]

        def finish(o, z):
            ms = jnp.mean(o * o, axis=-1, keepdims=True)
            return (o * lax.rsqrt(ms + NORM_EPS) * gain * _silu(z.astype(F32))).astype(o_ref.dtype)

        first = CHUNK - padr if padr else 0
        if first:
            o_ref[0:first, :] = finish(osum_r[padr:CHUNK, :], z_ref[0:first, :])

        def out_body(c, carry):
            base = pl.multiple_of(c * CHUNK, CHUNK)
            dst = pl.multiple_of(c * CHUNK - padr, SUBLANES)
            o_ref[pl.ds(dst, CHUNK), :] = finish(osum_r[pl.ds(base, CHUNK), :], z_ref[pl.ds(dst, CHUNK), :])
            return carry

        lax.fori_loop(1 if first else 0, nc, out_body, 0, unroll=4)

    def conv_only(c, carry):
        conv_chunk(c)
        return carry

    @pl.when(step < n_heads_total)
    def _():
        stage_conv_inputs()
        lax.fori_loop(0, nc, conv_only, 0, unroll=3)

    @pl.when((step < n_heads_total) & (head == 0))
    def _():
        stage_gate_inputs()
        for_groups(gate_batch_group)

    @pl.when(step < n_heads_total)
    def _():
        for_groups(gate_head_group)

    @pl.when(step == 0)
    def _():
        for_groups(functools.partial(chunk_group, with_scan=False))

    @pl.when((step > 0) & (step < n_heads_total))
    def _():
        for_groups(functools.partial(chunk_group, with_scan=True), (zero_state, zero_state))
        write_output()

    @pl.when(step == n_heads_total)
    def _():
        lax.fori_loop(0, nc, scan_step, (zero_state, zero_state))
        write_output()


def _gdn(proj3, ab3, conv_w, gate_prm, out_norm_g):
    b, seq, _ = proj3.shape
    lp = -(-seq // CHUNK) * CHUNK
    nc = lp // CHUNK
    n_total = b * N_HEADS
    cur = lambda i: jnp.minimum(i, n_total - 1)
    prev = lambda i: jnp.maximum(i - 1, 0)
    col = lambda off: pl.BlockSpec((None, seq, LANES), lambda i: (cur(i) // N_HEADS, 0, off + cur(i) % N_HEADS))
    cw = lambda off: pl.BlockSpec((CONV_K, LANES), lambda i: (0, off + cur(i) % N_HEADS))
    return pl.pallas_call(
        functools.partial(_gdn_kernel, seq=seq, lp=lp, group=GDN_GROUP, n_heads_total=n_total),
        grid=(n_total + 1,),
        in_specs=[
            col(COL_QKV // LANES), col(COL_QKV // LANES + N_HEADS), col(COL_QKV // LANES + 2 * N_HEADS),
            pl.BlockSpec((None, seq, LANES),
                         lambda i: (prev(i) // N_HEADS, 0, COL_Z // LANES + prev(i) % N_HEADS)),
            pl.BlockSpec((None, seq, LANES), lambda i: (cur(i) // N_HEADS, 0, 0)),
            cw(0), cw(N_HEADS), cw(2 * N_HEADS),
            pl.BlockSpec((SUBLANES, LANES), lambda i: (0, 0)),
            pl.BlockSpec((1, LANES), lambda i: (0, 0)),
        ],
        out_specs=pl.BlockSpec((None, seq, LANES), lambda i: (prev(i) // N_HEADS, 0, prev(i) % N_HEADS)),
        out_shape=jax.ShapeDtypeStruct((b, seq, N_HEADS * HEAD_DIM), MXU_DTYPE),
        scratch_shapes=[
            pltpu.VMEM((lp + 2 * SUBLANES, LANES), F32),
            pltpu.VMEM((lp + 2 * SUBLANES, LANES), F32),
            pltpu.VMEM((lp + 2 * SUBLANES, LANES), F32),
            pltpu.VMEM((lp, LANES), F32),
            pltpu.VMEM((lp, LANES), F32),
            pltpu.VMEM((lp, LANES), F32),
            pltpu.VMEM((lp, 4 * LANES), F32),
            pltpu.VMEM((nc * LANES, LANES), F32),
            pltpu.VMEM((lp, LANES), F32),
            pltpu.VMEM((2, 2, lp, LANES), MXU_DTYPE),
            pltpu.VMEM((2, 2, nc * HEAD_DIM, LANES), MXU_DTYPE),
            pltpu.VMEM((2, 2, nc * HEAD_DIM, LANES), F32),
            pltpu.VMEM((2, 2, nc * SUBLANES, LANES), F32),
            pltpu.VMEM((2, lp, LANES), F32),
        ],
        compiler_params=_params(("arbitrary",)),
        name="gdn",
    )(proj3, proj3, proj3, proj3, ab3, conv_w, conv_w, conv_w, gate_prm, out_norm_g)


def _merge_kernel(h_ref, og_ref, ym_ref, gf_ref, gd_ref, wf_ref, wd_ref, wo_ref, n2_ref, wr_ref, br_ref,
                  h1_ref, hn_ref, ridx_ref, rw_ref, cnt_ref):
    @pl.when(pl.program_id(0) == 0)
    def _():
        cnt_ref[...] = jnp.zeros_like(cnt_ref)

    y_f = _dot(ym_ref[...], wf_ref[...])
    y_d = _dot(og_ref[...], wd_ref[...])
    merged = (jax.nn.sigmoid(gf_ref[...].astype(F32)) * y_f
              + jax.nn.sigmoid(gd_ref[...].astype(F32)) * y_d)
    h1 = h_ref[...] + _dot(merged.astype(MXU_DTYPE), wo_ref[...])
    h1_ref[...] = h1
    ms = jnp.mean(h1 * h1, axis=-1, keepdims=True)
    hn = h1 * lax.rsqrt(ms + NORM_EPS) * n2_ref[...]
    hn_ref[...] = hn

    x_hi = hn.astype(MXU_DTYPE)
    x_lo = (hn - x_hi.astype(F32)).astype(MXU_DTYPE)
    w = wr_ref[...]
    w_hi = w.astype(MXU_DTYPE)
    w_lo = (w - w_hi.astype(F32)).astype(MXU_DTYPE)
    logits = _dot(x_hi, w_hi) + _dot(x_hi, w_lo) + _dot(x_lo, w_hi) + br_ref[...]

    tm = logits.shape[0]
    lane = lax.broadcasted_iota(jnp.int32, (tm, LANES), 1)
    neg_inf = -jnp.inf
    is_group = lane < N_GROUPS
    gl = jnp.where(is_group, logits, neg_inf)
    ge = jnp.exp(gl - jnp.max(gl, axis=-1, keepdims=True))
    gp = ge / jnp.sum(ge, axis=-1, keepdims=True)
    p_group = jnp.max(gp, axis=-1, keepdims=True)
    g_idx = jnp.min(jnp.where(is_group & (gp == p_group), lane, LANES), axis=-1, keepdims=True)

    in_group = (lane >= N_GROUPS) & (lane < N_GROUPS + N_EXPERTS) & (((lane - N_GROUPS) >> 3) == g_idx)
    el = jnp.where(in_group, logits, neg_inf)
    ee = jnp.exp(el - jnp.max(el, axis=-1, keepdims=True))
    ep = ee / jnp.sum(ee, axis=-1, keepdims=True)
    v1 = jnp.max(jnp.where(in_group, ep, -1.0), axis=-1, keepdims=True)
    i1 = jnp.min(jnp.where(in_group & (ep == v1), lane, LANES), axis=-1, keepdims=True)
    rest = in_group & (lane != i1)
    v2 = jnp.max(jnp.where(rest, ep, -1.0), axis=-1, keepdims=True)
    i2 = jnp.min(jnp.where(rest & (ep == v2), lane, LANES), axis=-1, keepdims=True)
    denom = v1 + v2
    w1 = p_group * (v1 / denom)
    w2 = p_group * (v2 / denom)
    e1 = i1 - N_GROUPS
    e2 = i2 - N_GROUPS
    ridx_ref[...] = jnp.where(lane == 0, e1, jnp.where(lane == 1, e2, 0))
    rw_ref[...] = jnp.where(lane == 0, w1, jnp.where(lane == 1, w2, 0.0))
    onehots = jnp.where((lane == e1) | (lane == e2), 1.0, 0.0)
    cnt_ref[...] += jnp.broadcast_to(jnp.sum(onehots, axis=0, keepdims=True), cnt_ref.shape)


def _merge(h2d, og2d, ymix2d, proj2d, w_fourier, w_delta, w_out, norm2_g, w_router, b_router, tm):
    t, d = h2d.shape
    row = lambda w: pl.BlockSpec((tm, w), lambda i: (i, 0))
    const = lambda shape: pl.BlockSpec(shape, lambda i: (0, 0))
    return pl.pallas_call(
        _merge_kernel,
        grid=(t // tm,),
        in_specs=[
            row(d), row(d), row(F_WIDTH),
            pl.BlockSpec((tm, d), lambda i: (i, COL_GATE_F // d)),
            pl.BlockSpec((tm, d), lambda i: (i, COL_GATE_D // d)),
            const((F_WIDTH, d)), const((d, d)), const((d, d)), const((1, d)),
            const((d, LANES)), const((1, LANES)),
        ],
        out_specs=[row(d), row(d), row(LANES), row(LANES), const((SUBLANES, LANES))],
        out_shape=[
            jax.ShapeDtypeStruct((t, d), F32),
            jax.ShapeDtypeStruct((t, d), F32),
            jax.ShapeDtypeStruct((t, LANES), jnp.int32),
            jax.ShapeDtypeStruct((t, LANES), F32),
            jax.ShapeDtypeStruct((SUBLANES, LANES), F32),
        ],
        compiler_params=_params(("arbitrary",)),
        name="merge",
    )(h2d, og2d, ymix2d, proj2d, proj2d, w_fourier, w_delta, w_out, norm2_g, w_router, b_router)


def _plan_kernel(ridx_ref, cnt_ref, dest_ref, seg_ref, base_ref, run_ref):
    tt = ridx_ref.shape[0]

    @pl.when(pl.program_id(0) == 0)
    def _():
        cnt = cnt_ref[...].astype(jnp.int32)
        n_blk = (cnt + (EXPERT_BLOCK - 1)) // EXPERT_BLOCK
        padded = (n_blk * EXPERT_BLOCK).astype(F32)
        r = lax.broadcasted_iota(jnp.int32, (LANES, LANES), 0)
        c = lax.broadcasted_iota(jnp.int32, (LANES, LANES), 1)
        before = jnp.where(r < c, 1.0, 0.0).astype(MXU_DTYPE)
        start = _exact_right(padded, before)
        base_ref[...] = start
        run_ref[...] = jnp.zeros_like(run_ref)
        seg_ref[0:SUBLANES, :] = start.astype(jnp.int32)
        seg_ref[SUBLANES:2 * SUBLANES, :] = n_blk
        seg_ref[2 * SUBLANES:, :] = cnt

    lane = lax.broadcasted_iota(jnp.int32, (tt, LANES), 1)
    ridx = ridx_ref[...]
    oh0 = jnp.where(lane == ridx[:, 0:1], 1.0, 0.0)
    oh1 = jnp.where(lane == ridx[:, 1:2], 1.0, 0.0)
    r = lax.broadcasted_iota(jnp.int32, (tt, tt), 0)
    c = lax.broadcasted_iota(jnp.int32, (tt, tt), 1)
    earlier = jnp.where(c < r, 1.0, 0.0).astype(MXU_DTYPE)
    prefix = _dot(earlier, jnp.concatenate([oh0, oh1], axis=1).astype(MXU_DTYPE))
    c0 = jnp.sum(oh0, axis=0, keepdims=True)
    c1 = jnp.sum(oh1, axis=0, keepdims=True)
    first = base_ref[0:1, :] + run_ref[0:1, :]
    d0 = jnp.sum(oh0 * (prefix[:, :LANES] + first), axis=-1, keepdims=True)
    d1 = jnp.sum(oh1 * (prefix[:, LANES:] + first + c0), axis=-1, keepdims=True)
    run_ref[...] += jnp.broadcast_to(c0 + c1, run_ref.shape)
    dest_ref[...] = jnp.where(lane == 0, d0, jnp.where(lane == 1, d1, 0.0)).astype(jnp.int32)


def _plan(ridx, cnt, tt):
    t = ridx.shape[0]
    return pl.pallas_call(
        _plan_kernel,
        grid=(t // tt,),
        in_specs=[pl.BlockSpec((tt, LANES), lambda i: (i, 0)),
                  pl.BlockSpec((SUBLANES, LANES), lambda i: (0, 0))],
        out_specs=[pl.BlockSpec((tt, LANES), lambda i: (i, 0)),
                   pl.BlockSpec((3 * SUBLANES, LANES), lambda i: (0, 0))],
        out_shape=[jax.ShapeDtypeStruct((t, LANES), jnp.int32),
                   jax.ShapeDtypeStruct((3 * SUBLANES, LANES), jnp.int32)],
        scratch_shapes=[pltpu.VMEM((SUBLANES, LANES), F32), pltpu.VMEM((SUBLANES, LANES), F32)],
        compiler_params=_params(("arbitrary",)),
        name="plan",
    )(ridx, cnt)


def _scatter_kernel(dest_ref, hn_ref, xb_in_ref, xb_ref, sem, *, ts):
    del xb_in_ref

    def issue(t, carry):
        src = hn_ref.at[pl.ds(t, 1)]
        for k in range(TOP_K):
            pltpu.make_async_copy(src, xb_ref.at[pl.ds(dest_ref[0, TOP_K * t + k], 1)], sem).start(priority=k % 2)
        return carry

    lax.fori_loop(0, ts, issue, 0, unroll=DMA_UNROLL)
    for _ in range(TOP_K):
        pltpu.make_async_copy(hn_ref, xb_ref.at[pl.ds(0, ts)], sem).wait()


def _scatter(dest_flat, hn2d, xb_zero, ts):
    t, d = hn2d.shape
    return pl.pallas_call(
        functools.partial(_scatter_kernel, ts=ts),
        grid=(t // ts,),
        in_specs=[pl.BlockSpec((None, 1, TOP_K * ts), lambda i: (i, 0, 0), memory_space=pltpu.SMEM),
                  pl.BlockSpec((ts, d), lambda i: (i, 0)),
                  pl.BlockSpec(memory_space=pl.ANY)],
        out_specs=pl.BlockSpec(memory_space=pl.ANY),
        out_shape=jax.ShapeDtypeStruct(xb_zero.shape, F32),
        scratch_shapes=[pltpu.SemaphoreType.DMA(())],
        input_output_aliases={2: 0},
        compiler_params=_params(("arbitrary",)),
        name="scatter",
    )(dest_flat, hn2d, xb_zero)


def _expert_kernel(start_ref, nblk_ref, xb_ref, wg_ref, wu_ref, wd_ref, yb_ref,
                   wg16_ref, wu16_ref, wd16_ref, xbuf_ref, ybuf_ref, sem_in, sem_out):
    expert = pl.program_id(0)
    last = pl.num_programs(0) - 1
    n = nblk_ref[expert]
    first = start_ref[expert] // EXPERT_BLOCK
    total = start_ref[last] // EXPERT_BLOCK + nblk_ref[last]

    def rows(g):
        return pl.ds(pl.multiple_of(g * EXPERT_BLOCK, EXPERT_BLOCK), EXPERT_BLOCK)

    def x_copy(g):
        slot = g % EXPERT_X_SLOTS
        return pltpu.make_async_copy(xb_ref.at[rows(g)], xbuf_ref.at[slot], sem_in.at[slot])

    def y_copy(g):
        slot = g % EXPERT_Y_SLOTS
        return pltpu.make_async_copy(ybuf_ref.at[slot], yb_ref.at[rows(g)], sem_out.at[slot])

    @pl.when(n > 0)
    def _():
        @pl.when(first == 0)
        def _():
            for j in range(EXPERT_X_SLOTS - 1):
                @pl.when(j < total)
                def _():
                    x_copy(j).start()

        wg16_ref[...] = wg_ref[...].astype(MXU_DTYPE)
        wu16_ref[...] = wu_ref[...].astype(MXU_DTYPE)
        wd16_ref[...] = wd_ref[...].astype(MXU_DTYPE)

        def body(i, carry):
            g = first + i

            @pl.when(g + (EXPERT_X_SLOTS - 1) < total)
            def _():
                x_copy(g + (EXPERT_X_SLOTS - 1)).start()

            x_copy(g).wait()

            @pl.when(g >= EXPERT_Y_SLOTS)
            def _():
                y_copy(g - EXPERT_Y_SLOTS).wait()

            x = xbuf_ref[g % EXPERT_X_SLOTS].astype(MXU_DTYPE)
            gate = _dot(x, wg16_ref[...])
            up = _dot(x, wu16_ref[...])
            ybuf_ref[g % EXPERT_Y_SLOTS] = _dot((_silu(gate) * up).astype(MXU_DTYPE), wd16_ref[...])
            y_copy(g).start()
            return carry

        lax.fori_loop(0, n, body, 0)

    @pl.when(expert == last)
    def _():
        for j in range(EXPERT_Y_SLOTS):
            @pl.when(total > j)
            def _():
                y_copy(total - 1 - j).wait()


def _experts(seg_start, seg_blocks, xb, w_gate, w_up, w_down):
    n_slots, d = xb.shape
    n_exp, _, de = w_gate.shape
    grid_spec = pltpu.PrefetchScalarGridSpec(
        num_scalar_prefetch=2,
        grid=(n_exp,),
        in_specs=[
            pl.BlockSpec(memory_space=pl.ANY),
            pl.BlockSpec((None, d, de), lambda e, st, nb: (e, 0, 0)),
            pl.BlockSpec((None, d, de), lambda e, st, nb: (e, 0, 0)),
            pl.BlockSpec((None, de, d), lambda e, st, nb: (e, 0, 0)),
        ],
        out_specs=pl.BlockSpec(memory_space=pl.ANY),
        scratch_shapes=[
            pltpu.VMEM((d, de), MXU_DTYPE), pltpu.VMEM((d, de), MXU_DTYPE), pltpu.VMEM((de, d), MXU_DTYPE),
            pltpu.VMEM((EXPERT_X_SLOTS, EXPERT_BLOCK, d), F32), pltpu.VMEM((EXPERT_Y_SLOTS, EXPERT_BLOCK, d), F32),
            pltpu.SemaphoreType.DMA((EXPERT_X_SLOTS,)), pltpu.SemaphoreType.DMA((EXPERT_Y_SLOTS,)),
        ],
    )
    return pl.pallas_call(
        _expert_kernel,
        grid_spec=grid_spec,
        out_shape=jax.ShapeDtypeStruct((n_slots, d), F32),
        input_output_aliases={2: 0},
        compiler_params=_params(("arbitrary",)),
        name="experts",
    )(seg_start, seg_blocks, xb, w_gate, w_up, w_down)


def _combine_kernel(dest_ref, yb_ref, h1_ref, rw_ref, fg_ref, o_ref, buf_ref, sem, *, tc):
    def issue(t, carry):
        for k in range(TOP_K):
            pltpu.make_async_copy(yb_ref.at[pl.ds(dest_ref[0, TOP_K * t + k], 1)],
                                  buf_ref.at[k, pl.ds(t, 1)], sem).start(priority=k % 2)
        return carry

    lax.fori_loop(0, tc, issue, 0, unroll=DMA_UNROLL)
    for k in range(TOP_K):
        pltpu.make_async_copy(yb_ref.at[pl.ds(0, tc)], buf_ref.at[k], sem).wait()

    rw = rw_ref[...]
    h2 = h1_ref[...] + (buf_ref[0] * rw[:, 0:1] + buf_ref[1] * rw[:, 1:2])
    ms = jnp.mean(h2 * h2, axis=-1, keepdims=True)
    o_ref[...] = h2 * lax.rsqrt(ms + NORM_EPS) * fg_ref[...]


def _combine(dest_flat, yb, h1, rw, final_g, tc):
    t, d = h1.shape
    return pl.pallas_call(
        functools.partial(_combine_kernel, tc=tc),
        grid=(t // tc,),
        in_specs=[pl.BlockSpec((None, 1, TOP_K * tc), lambda i: (i, 0, 0), memory_space=pltpu.SMEM),
                  pl.BlockSpec(memory_space=pl.ANY),
                  pl.BlockSpec((tc, d), lambda i: (i, 0)),
                  pl.BlockSpec((tc, LANES), lambda i: (i, 0)),
                  pl.BlockSpec((1, d), lambda i: (0, 0))],
        out_specs=pl.BlockSpec((tc, d), lambda i: (i, 0)),
        out_shape=jax.ShapeDtypeStruct((t, d), F32),
        scratch_shapes=[pltpu.VMEM((TOP_K, tc, d), F32), pltpu.SemaphoreType.DMA(())],
        compiler_params=_params(("arbitrary",)),
        name="combine",
    )(dest_flat, yb, h1, rw, final_g)


def _tile(total, candidates):
    for c in candidates:
        if total % c == 0:
            return c
    raise ValueError(f"no tile for {total}")


def _dft_tables(seq):
    kp = -(-seq // LANES) * LANES
    c = np.arange(FGROUP_DIM, dtype=np.int64)
    ang_c = (2.0 * np.pi / FGROUP_DIM) * ((c[:, None] * c[None, :]) % FGROUP_DIM)
    cs128 = np.concatenate([np.cos(ang_c), np.sin(ang_c)], axis=1)
    n = np.arange(seq, dtype=np.int64)
    ang_l = (2.0 * np.pi / seq) * ((n[:, None] * n[None, :]) % seq)
    twid = np.zeros((seq, 2 * kp), np.float32)
    twid[:, :seq] = np.cos(ang_l)
    twid[:, kp:kp + seq] = -np.sin(ang_l)
    return jnp.asarray(cs128.astype(MXU_DTYPE)), jnp.asarray(twid.astype(MXU_DTYPE))


def kernel(x, meta_tokens, norm1_g, w_in, conv_w, a_log_fwd, dt_bias_fwd, a_log_bwd, dt_bias_bwd, out_norm_g, w_fourier, w_delta, w_out, norm2_g, w_router_group, b_router_group, w_router_expert, b_router_expert, w_gate_e, w_up_e, w_down_e, final_norm_g):
    bsz, n_real, d = x.shape
    seq = N_META + n_real
    n_tok = bsz * seq
    assert (CHUNK - N_META) % CHUNK + seq == -(-seq // CHUNK) * CHUNK, "meta tokens must fill the tail of chunk 0"
    assert w_in.shape[0] == 1, "single trunk layer"
    layer = 0

    meta = jnp.broadcast_to(meta_tokens[None].astype(x.dtype), (bsz, N_META, d))
    h = jnp.concatenate([meta, x], axis=1).reshape(n_tok, d)

    w = w_in[layer]
    o_f, o_qkv, o_z, o_ab, o_gf, o_gd = 0, F_WIDTH, F_WIDTH + 3 * QK_WIDTH, F_WIDTH + 4 * QK_WIDTH, \
        F_WIDTH + 4 * QK_WIDTH + 4 * N_HEADS, F_WIDTH + 4 * QK_WIDTH + 4 * N_HEADS + d
    w_cat = jnp.concatenate(
        [w[:, o_qkv:o_z], w[:, o_z:o_ab], w[:, o_gf:o_gd], w[:, o_gd:o_gd + d], w[:, o_f:o_qkv]],
        axis=1).astype(MXU_DTYPE)
    w_ab = jnp.pad(w[:, o_ab:o_gf], ((0, 0), (0, LANES - 4 * N_HEADS))).astype(MXU_DTYPE)

    proj, ab = _in_proj(h, norm1_g[layer][None, :], w_cat, w_ab, _tile(n_tok, (2064, 104, 8)))
    proj3 = proj.reshape(bsz, seq, PROJ_WIDTH)

    cs128, twid = _dft_tables(seq)
    ymix = _fnet(proj3, cs128, twid, _tile(seq, (344, 104, 8)))

    gate_prm = jnp.zeros((SUBLANES, LANES), F32)
    gate_prm = gate_prm.at[0, 0:N_HEADS].set(a_log_fwd[layer]).at[0, 2 * N_HEADS:3 * N_HEADS].set(a_log_bwd[layer])
    gate_prm = gate_prm.at[1, 0:N_HEADS].set(dt_bias_fwd[layer]).at[1, 2 * N_HEADS:3 * N_HEADS].set(dt_bias_bwd[layer])
    og = _gdn(proj3, ab.reshape(bsz, seq, LANES), conv_w[layer], gate_prm, out_norm_g[layer][None, :])

    w_router = jnp.zeros((d, LANES), F32)
    w_router = w_router.at[:, 0:N_GROUPS].set(w_router_group[layer])
    w_router = w_router.at[:, N_GROUPS:N_GROUPS + N_EXPERTS].set(w_router_expert[layer])
    b_router = jnp.zeros((1, LANES), F32)
    b_router = b_router.at[0, 0:N_GROUPS].set(b_router_group[layer])
    b_router = b_router.at[0, N_GROUPS:N_GROUPS + N_EXPERTS].set(b_router_expert[layer])
    tt = _tile(n_tok, (384, 128))
    h1, hn, ridx, rw, cnt = _merge(
        h, og.reshape(n_tok, d), ymix.reshape(n_tok, F_WIDTH), proj,
        w_fourier[layer].astype(MXU_DTYPE), w_delta[layer].astype(MXU_DTYPE), w_out[layer].astype(MXU_DTYPE),
        norm2_g[layer][None, :], w_router, b_router, _tile(n_tok, (688, 384, 128)))

    n_assign = n_tok * TOP_K
    n_blocks = -(-n_assign // EXPERT_BLOCK) + N_EXPERTS
    dest, seg = _plan(ridx, cnt, tt)
    dest_flat = dest[:, :TOP_K].reshape(n_tok // tt, 1, TOP_K * tt)
    seg_start = seg[0, :N_EXPERTS]
    seg_blocks = seg[SUBLANES, :N_EXPERTS]

    xb = _scatter(dest_flat, hn, jnp.zeros((n_blocks * EXPERT_BLOCK, d), F32), tt)
    yb = _experts(seg_start, seg_blocks, xb, w_gate_e[layer], w_up_e[layer], w_down_e[layer])
    out = _combine(dest_flat, yb, h1, rw, final_norm_g[None, :], tt)
    return out.reshape(bsz, seq, d)[:, N_META:]
```

```python
import functools
import math

import jax
import jax.numpy as jnp
import numpy as np
from jax import lax
from jax.experimental import pallas as pl
from jax.experimental.pallas import tpu as pltpu

F32 = jnp.float32
MXU_DTYPE = jnp.bfloat16

N_META = 16
CHUNK = 64
N_FGROUPS = 4
FGROUP_DIM = 128
F_WIDTH = N_FGROUPS * FGROUP_DIM
N_HEADS = 8
HEAD_DIM = 128
QK_WIDTH = N_HEADS * HEAD_DIM
CONV_K = 5
N_GROUPS = 8
EXPERTS_PER_GROUP = 8
N_EXPERTS = N_GROUPS * EXPERTS_PER_GROUP
TOP_K = 2
EXPERT_BLOCK = 256
NORM_EPS = 1e-6
LANES = 128
SUBLANES = 8
VMEM_LIMIT = 52 * 1024 * 1024
GDN_GROUP = 11
DMA_UNROLL = 8
EXPERT_X_SLOTS = 4
EXPERT_Y_SLOTS = 3

COL_QKV = 0
COL_Z = 3 * QK_WIDTH
COL_GATE_F = COL_Z + QK_WIDTH
COL_GATE_D = COL_GATE_F + 1024
COL_FIN = COL_GATE_D + 1024
PROJ_WIDTH = COL_FIN + F_WIDTH
PROJ_TN = 3328
PROJ_DTYPE = jnp.bfloat16


def _dot(a, b):
    return jnp.dot(a, b, preferred_element_type=F32)


def _dot_nt(a, b):
    return lax.dot_general(a, b, (((1,), (1,)), ((), ())), preferred_element_type=F32)


def _dot_tn(a, b):
    return lax.dot_general(a, b, (((0,), (0,)), ((), ())), preferred_element_type=F32)


def _split3(x):
    hi = x.astype(MXU_DTYPE)
    r1 = x - hi.astype(F32)
    mid = r1.astype(MXU_DTYPE)
    lo = (r1 - mid.astype(F32)).astype(MXU_DTYPE)
    return hi, mid, lo


def _exact_left(sel, x):
    hi, mid, lo = _split3(x)
    return _dot(sel, hi) + _dot(sel, mid) + _dot(sel, lo)


def _exact_right(x, sel):
    hi, mid, lo = _split3(x)
    return _dot(hi, sel) + _dot(mid, sel) + _dot(lo, sel)


def _silu(x):
    return x * jax.nn.sigmoid(x)


def _params(sem):
    return pltpu.CompilerParams(dimension_semantics=sem, vmem_limit_bytes=VMEM_LIMIT)


def _inproj_kernel(h_ref, g_ref, w_ref, wab_ref, o_ref, ab_ref, hn_ref):
    @pl.when(pl.program_id(1) == 0)
    def _():
        x = h_ref[...]
        ms = jnp.mean(x * x, axis=-1, keepdims=True)
        hn_ref[...] = (x * lax.rsqrt(ms + NORM_EPS) * g_ref[...]).astype(MXU_DTYPE)
        ab_ref[...] = _dot(hn_ref[...], wab_ref[...])

    o_ref[...] = _dot(hn_ref[...], w_ref[...]).astype(o_ref.dtype)


def _in_proj(h2d, gain, w_cat, w_ab, tm):
    t, d = h2d.shape
    n = w_cat.shape[1]
    return pl.pallas_call(
        _inproj_kernel,
        grid=(t // tm, n // PROJ_TN),
        in_specs=[
            pl.BlockSpec((tm, d), lambda i, j: (i, 0)),
            pl.BlockSpec((1, d), lambda i, j: (0, 0)),
            pl.BlockSpec((d, PROJ_TN), lambda i, j: (0, j)),
            pl.BlockSpec((d, LANES), lambda i, j: (0, 0)),
        ],
        out_specs=[pl.BlockSpec((tm, PROJ_TN), lambda i, j: (i, j)),
                   pl.BlockSpec((tm, LANES), lambda i, j: (i, 0))],
        out_shape=[jax.ShapeDtypeStruct((t, n), PROJ_DTYPE),
                   jax.ShapeDtypeStruct((t, LANES), F32)],
        scratch_shapes=[pltpu.VMEM((tm, d), MXU_DTYPE)],
        compiler_params=_params(("arbitrary", "arbitrary")),
        name="in_proj",
    )(h2d, gain, w_cat, w_ab)


def _fnet_kernel(x_ref, cs_ref, tw_ref, flip_ref, o_ref, r_ref, *, seq, kp, nat, scale):
    tail = kp - seq
    if tail:
        r_ref[seq:kp, :] = jnp.zeros((tail, F_WIDTH), MXU_DTYPE)
        r_ref[kp + seq:, :] = jnp.zeros((tail, F_WIDTH), MXU_DTYPE)
    cs = cs_ref[...]
    for g in range(N_FGROUPS):
        cols = slice(g * FGROUP_DIM, (g + 1) * FGROUP_DIM)
        p = _dot(x_ref[:, cols].astype(MXU_DTYPE), cs)
        r_ref[0:seq, cols] = p[:, :FGROUP_DIM].astype(MXU_DTYPE)
        r_ref[kp:kp + seq, cols] = p[:, FGROUP_DIM:].astype(MXU_DTYPE)

    a = _dot(tw_ref[:, :kp], r_ref[0:kp, :])
    b = _dot(tw_ref[:, kp:], r_ref[kp:, :])
    o_ref[0:nat, :] = ((a - b) * scale).astype(o_ref.dtype)
    mirrored = ((a + b) * scale).astype(MXU_DTYPE)
    o_ref[nat:, :] = _dot(flip_ref[...], mirrored).astype(o_ref.dtype)


def _fnet(proj3, cs128, twid, flip):
    b, seq, _ = proj3.shape
    nat, kp = twid.shape[0], twid.shape[1] // 2
    scale = 1.0 / math.sqrt(seq * FGROUP_DIM)
    const = lambda shape: pl.BlockSpec(shape, lambda bi: (0, 0))
    return pl.pallas_call(
        functools.partial(_fnet_kernel, seq=seq, kp=kp, nat=nat, scale=scale),
        grid=(b,),
        in_specs=[
            pl.BlockSpec((None, seq, F_WIDTH), lambda bi: (bi, 0, COL_FIN // F_WIDTH)),
            const((FGROUP_DIM, 2 * FGROUP_DIM)), const((nat, 2 * kp)), const((seq - nat, nat)),
        ],
        out_specs=pl.BlockSpec((None, seq, F_WIDTH), lambda bi: (bi, 0, 0)),
        out_shape=jax.ShapeDtypeStruct((b, seq, F_WIDTH), MXU_DTYPE),
        scratch_shapes=[pltpu.VMEM((2 * kp, F_WIDTH), MXU_DTYPE)],
        compiler_params=_params(("arbitrary",)),
        name="fnet",
    )(proj3, cs128, twid, flip)


def _aligned(x, m):
    return x if isinstance(x, int) else pl.multiple_of(x, m)


def _gdn_kernel(q_ref, k_ref, v_ref, z_ref, ab_ref, cwq_ref, cwk_ref, cwv_ref, prm_ref, ong_ref,
                o_ref,
                rawq_ref, rawk_ref, rawv_ref, qs_ref, ks_ref, vs_ref, gsel_ref, grow_ref, gmix_ref,
                qe_ref, mp_ref, nn_ref, dd_ref, osum_ref,
                *, seq, lp, group, n_heads_total):
    step = pl.program_id(0)
    head = jnp.minimum(step, n_heads_total - 1) % N_HEADS
    nc = lp // CHUNK
    padr = lp - seq
    halo = SUBLANES
    row64 = lax.broadcasted_iota(jnp.int32, (CHUNK, LANES), 0)
    lane64 = lax.broadcasted_iota(jnp.int32, (CHUNK, LANES), 1)

    def for_groups(fn, carry=0):
        n_full, rem = divmod(nc, group)
        if n_full:
            carry = lax.fori_loop(0, n_full, lambda i, c: fn([i * group + j for j in range(group)], c), carry)
        if rem:
            carry = fn([n_full * group + j for j in range(rem)], carry)
        return carry

    cur_half = step % 2
    qe_w, mp_w, nn_w, dd_w, osum_w = (r.at[cur_half] for r in (qe_ref, mp_ref, nn_ref, dd_ref, osum_ref))
    qe_r, mp_r, nn_r, dd_r, osum_r = (r.at[1 - cur_half] for r in (qe_ref, mp_ref, nn_ref, dd_ref, osum_ref))

    streams = ((rawq_ref, q_ref, cwq_ref, qs_ref, HEAD_DIM ** -0.5),
               (rawk_ref, k_ref, cwk_ref, ks_ref, 1.0),
               (rawv_ref, v_ref, cwv_ref, vs_ref, None))
    def stage_conv_inputs():
        for raw_ref, src_ref, _, _, _ in streams:
            raw_ref[0:padr + halo, :] = jnp.zeros((padr + halo, LANES), F32)
            raw_ref[padr + halo:padr + halo + seq, :] = src_ref[...].astype(F32)
            raw_ref[lp + halo:lp + 2 * halo, :] = jnp.zeros((halo, LANES), F32)

    def conv_chunk(c):
        base = pl.multiple_of(c * CHUNK, CHUNK)
        for raw_ref, _, cw_ref, dst_ref, l2_scale in streams:
            cw = cw_ref[...]
            acc = jnp.zeros((CHUNK, LANES), F32)
            for j in range(CONV_K):
                off = halo - (CONV_K - 1) // 2 + j
                acc = acc + cw[j:j + 1, :] * raw_ref[pl.ds(base + off, CHUNK), :]
            y = _silu(acc)
            y = jnp.where(row64 + base >= padr, y, 0.0)
            if l2_scale is not None:
                y = y * lax.rsqrt(jnp.sum(y * y, axis=-1, keepdims=True) + NORM_EPS) * l2_scale
            dst_ref[pl.ds(base, CHUNK), :] = y

    raw_ref = rawq_ref

    def stage_gate_inputs():
        raw_ref[0:padr, :] = jnp.zeros((padr, LANES), F32)
        raw_ref[padr:padr + seq, :] = ab_ref[...]

    neg_a = -jnp.exp(prm_ref[0:1, :])
    dt_bias = prm_ref[1:2, :]
    is_g = (lane64 < 8) | ((lane64 >= 16) & (lane64 < 24))
    is_beta = ((lane64 >= 8) & (lane64 < 16)) | ((lane64 >= 24) & (lane64 < 32))
    r128 = lax.broadcasted_iota(jnp.int32, (2 * CHUNK, CHUNK), 0)
    c128 = lax.broadcasted_iota(jnp.int32, (2 * CHUNK, CHUNK), 1)
    tri2 = jnp.where(((r128 < CHUNK) & (r128 >= c128)) | ((r128 >= CHUNK) & (r128 - CHUNK <= c128)),
                     1.0, 0.0).astype(MXU_DTYPE)
    sel_r = lax.broadcasted_iota(jnp.int32, (LANES, 4 * LANES), 0)
    sel_c = lax.broadcasted_iota(jnp.int32, (LANES, 4 * LANES), 1)
    n_gate_cols = 4 * N_HEADS
    sel_packed = jnp.where((sel_r < 3 * n_gate_cols)
                           & (sel_r % n_gate_cols == head + N_HEADS * (sel_c // LANES)), 1.0, 0.0).astype(MXU_DTYPE)
    tri2x3 = jnp.concatenate([tri2, tri2, tri2], axis=1)

    def gate_batch_group(cs, carry):
        bases = [_aligned(c * CHUNK, CHUNK) for c in cs]
        gates, parts = [], []
        for base in bases:
            ab = raw_ref[pl.ds(base, CHUNK), :]
            g = neg_a * jax.nn.softplus(ab + dt_bias)
            beta = jax.nn.sigmoid(ab)
            gt = jnp.where(is_g, g, jnp.where(is_beta, beta, 0.0))
            gt = jnp.where(row64 + base >= padr, gt, 0.0)
            gates.append(gt)
            parts.append(jnp.concatenate(_split3(gt), axis=0))
        cums = [_dot(tri2x3, p) for p in parts]
        for c, base, cm, gt in zip(cs, bases, cums, gates):
            m = jnp.where(lane64 < 8, cm[:CHUNK], jnp.where((lane64 >= 16) & (lane64 < 24), cm[CHUNK:], gt))
            gmix_ref[pl.ds(base, CHUNK), :] = m
            grow_ref[pl.ds(_aligned(c * LANES, LANES), LANES), :] = jnp.concatenate([m, m], axis=0).T
        return carry

    def gate_head_group(cs, carry):
        rows = len(cs) * CHUNK
        base = _aligned(cs[0] * CHUNK, CHUNK)
        hi, mid, lo = _split3(gmix_ref[pl.ds(base, rows), :])
        packed = (hi.astype(F32) + pltpu.roll(mid.astype(F32), n_gate_cols, 1)
                  + pltpu.roll(lo.astype(F32), 2 * n_gate_cols, 1)).astype(MXU_DTYPE)
        gsel_ref[pl.ds(base, rows), :] = _dot(packed, sel_packed)
        return carry

    fwd_half = lane64 < CHUNK
    col64 = lane64 & (CHUNK - 1)
    incl2 = (fwd_half & (row64 >= col64)) | (~fwd_half & (row64 <= col64))
    strict2 = (fwd_half & (row64 > col64)) | (~fwd_half & (row64 < col64))
    eye2 = jnp.where(row64 == col64, 1.0, 0.0).astype(F32)

    def blockdiag(p):
        return jnp.concatenate([jnp.where(fwd_half, p, 0.0), jnp.where(fwd_half, 0.0, p)],
                               axis=0).astype(MXU_DTYPE)

    n_neumann = int(math.log2(CHUNK)) - 2
    n_stages = n_neumann + 6

    def chunk_group(cs, carry, with_scan):
        n = len(cs)
        bases = [_aligned(c * CHUNK, CHUNK) for c in cs]
        cb128 = [_aligned(c * LANES, LANES) for c in cs]
        cb8 = [_aligned(c * SUBLANES, SUBLANES) for c in cs]
        stage = [0]

        def end_of_stage(carry):
            s = stage[0]
            stage[0] += 1
            if with_scan:
                for j in range(n):
                    if j * n_stages // n == s:
                        carry = scan_step(cs[j], carry)
            return carry

        def gate_cols(j):
            gs = gsel_ref[pl.ds(bases[j], CHUNK), :]
            return gs[:, 0:LANES], gs[:, LANES:2 * LANES], gs[:, 2 * LANES:3 * LANES], gs[:, 3 * LANES:]

        kq = []
        for j in range(n):
            k16 = ks_ref[pl.ds(bases[j], CHUNK), :].astype(MXU_DTYPE)
            q16 = qs_ref[pl.ds(bases[j], CHUNK), :].astype(MXU_DTYPE)
            kq.append(_dot_nt(jnp.concatenate([k16, q16], axis=0), jnp.concatenate([k16, k16], axis=0)))
        carry = end_of_stage(carry)

        ts, ps, qkds = [], [], []
        for j in range(n):
            gf, bf, gb, bb = gate_cols(j)
            row_f = grow_ref[pl.ds(cb128[j] + head, 1), :]
            row_b = grow_ref[pl.ds(cb128[j] + 2 * N_HEADS + head, 1), :]
            g_col = jnp.where(fwd_half, gf, gb)
            g_row = jnp.where(fwd_half, row_f, row_b)
            decay = jnp.where(incl2, jnp.exp(jnp.where(incl2, g_col - g_row, 0.0)), 0.0)
            a2 = jnp.where(strict2, kq[j][:CHUNK] * jnp.where(fwd_half, bf, bb) * decay, 0.0)
            qkds.append(jnp.where(incl2, kq[j][CHUNK:] * decay, 0.0))
            ts.append(eye2 - a2)
            ps.append(a2)

        ps = [_dot(p.astype(MXU_DTYPE), blockdiag(p)) for p in ps]
        carry = end_of_stage(carry)
        for _ in range(n_neumann):
            tps = [_dot(jnp.concatenate([t, p], axis=0).astype(MXU_DTYPE), blockdiag(p)) for t, p in zip(ts, ps)]
            carry = end_of_stage(carry)
            ts = [t + tp[:CHUNK] for t, tp in zip(ts, tps)]
            ps = [tp[CHUNK:] for tp in tps]
        ts = [t + _dot(t.astype(MXU_DTYPE), blockdiag(p)) for t, p in zip(ts, ps)]
        carry = end_of_stage(carry)

        uws = []
        for j in range(n):
            gf, bf, gb, bb = gate_cols(j)
            k = ks_ref[pl.ds(bases[j], CHUNK), :]
            v = vs_ref[pl.ds(bases[j], CHUNK), :]
            x_f = jnp.concatenate([v * bf, k * bf * jnp.exp(gf)], axis=1)
            x_b = jnp.concatenate([v * bb, k * bb * jnp.exp(gb)], axis=1)
            uw = _dot(blockdiag(ts[j]), jnp.concatenate([x_f, x_b], axis=0).astype(MXU_DTYPE))
            uws.append(uw.astype(MXU_DTYPE))
        carry = end_of_stage(carry)

        xxs = [_dot(blockdiag(qkds[j]), uws[j]) for j in range(n)]
        carry = end_of_stage(carry)
        for j in range(n):
            gf, _, gb, _ = gate_cols(j)
            k = ks_ref[pl.ds(bases[j], CHUNK), :]
            q = qs_ref[pl.ds(bases[j], CHUNK), :]
            xx = xxs[j]
            osum_w[pl.ds(bases[j], CHUNK), :] = xx[:CHUNK, :LANES] + xx[CHUNK:, :LANES]
            for d, gc in enumerate((gf, gb)):
                rows = slice(d * CHUNK, (d + 1) * CHUNK)
                g_last = gc[CHUNK - 1:CHUNK, :] if d == 0 else gc[0:1, :]
                k_dec = k * jnp.exp(g_last - gc)
                mn = _dot_tn(k_dec.astype(MXU_DTYPE), uws[j][rows])
                qe_w[d, pl.ds(bases[j], CHUNK), :] = (q * jnp.exp(gc) - xx[rows, LANES:]).astype(MXU_DTYPE)
                nn_w[d, pl.ds(cb128[j], HEAD_DIM), :] = mn[:, :LANES]
                mp_w[d, pl.ds(cb128[j], HEAD_DIM), :] = mn[:, LANES:].astype(MXU_DTYPE)
                dd_w[d, pl.ds(cb8[j], SUBLANES), :] = jnp.broadcast_to(jnp.exp(g_last), (SUBLANES, LANES))
        carry = end_of_stage(carry)
        assert stage[0] == n_stages
        return carry

    def scan_step(i, carry):
        new = []
        for d in range(2):
            s = carry[d]
            c = i if d == 0 else nc - 1 - i
            base = _aligned(c * CHUNK, CHUNK)
            cbase = _aligned(c * HEAD_DIM, HEAD_DIM)
            dbase = _aligned(c * SUBLANES, SUBLANES)
            s16 = s.astype(MXU_DTYPE)
            osum_r[pl.ds(base, CHUNK), :] += _dot(qe_r[d, pl.ds(base, CHUNK), :], s16)
            dec = dd_r[d, pl.ds(dbase, 1), :]
            new.append(dec * s - _dot(mp_r[d, pl.ds(cbase, HEAD_DIM), :], s16)
                       + nn_r[d, pl.ds(cbase, HEAD_DIM), :])
        return tuple(new)

    zero_state = jnp.zeros((HEAD_DIM, HEAD_DIM), F32)

    def write_output():
        gain = ong_ref[...]

        def finish(o, z):
            ms = jnp.mean(o * o, axis=-1, keepdims=True)
            return (o * lax.rsqrt(ms + NORM_EPS) * gain * _silu(z.astype(F32))).astype(o_ref.dtype)

        first = CHUNK - padr if padr else 0
        if first:
            o_ref[0:first, :] = finish(osum_r[padr:CHUNK, :], z_ref[0:first, :])

        def out_body(c, carry):
            base = pl.multiple_of(c * CHUNK, CHUNK)
            dst = pl.multiple_of(c * CHUNK - padr, SUBLANES)
            o_ref[pl.ds(dst, CHUNK), :] = finish(osum_r[pl.ds(base, CHUNK), :], z_ref[pl.ds(dst, CHUNK), :])
            return carry

        lax.fori_loop(1 if first else 0, nc, out_body, 0, unroll=4)

    def conv_only(c, carry):
        conv_chunk(c)
        return carry

    @pl.when(step < n_heads_total)
    def _():
        stage_conv_inputs()
        lax.fori_loop(0, nc, conv_only, 0, unroll=3)

    @pl.when((step < n_heads_total) & (head == 0))
    def _():
        stage_gate_inputs()
        for_groups(gate_batch_group)

    @pl.when(step < n_heads_total)
    def _():
        for_groups(gate_head_group)

    @pl.when(step == 0)
    def _():
        for_groups(functools.partial(chunk_group, with_scan=False))

    @pl.when((step > 0) & (step < n_heads_total))
    def _():
        for_groups(functools.partial(chunk_group, with_scan=True), (zero_state, zero_state))
        write_output()

    @pl.when(step == n_heads_total)
    def _():
        lax.fori_loop(0, nc, scan_step, (zero_state, zero_state))
        write_output()


def _gdn(proj3, ab3, conv_w, gate_prm, out_norm_g):
    b, seq, _ = proj3.shape
    lp = -(-seq // CHUNK) * CHUNK
    nc = lp // CHUNK
    n_total = b * N_HEADS
    cur = lambda i: jnp.minimum(i, n_total - 1)
    prev = lambda i: jnp.maximum(i - 1, 0)
    col = lambda off: pl.BlockSpec((None, seq, LANES), lambda i: (cur(i) // N_HEADS, 0, off + cur(i) % N_HEADS))
    cw = lambda off: pl.BlockSpec((CONV_K, LANES), lambda i: (0, off + cur(i) % N_HEADS))
    return pl.pallas_call(
        functools.partial(_gdn_kernel, seq=seq, lp=lp, group=GDN_GROUP, n_heads_total=n_total),
        grid=(n_total + 1,),
        in_specs=[
            col(COL_QKV // LANES), col(COL_QKV // LANES + N_HEADS), col(COL_QKV // LANES + 2 * N_HEADS),
            pl.BlockSpec((None, seq, LANES),
                         lambda i: (prev(i) // N_HEADS, 0, COL_Z // LANES + prev(i) % N_HEADS)),
            pl.BlockSpec((None, seq, LANES), lambda i: (cur(i) // N_HEADS, 0, 0)),
            cw(0), cw(N_HEADS), cw(2 * N_HEADS),
            pl.BlockSpec((SUBLANES, LANES), lambda i: (0, 0)),
            pl.BlockSpec((1, LANES), lambda i: (0, 0)),
        ],
        out_specs=pl.BlockSpec((None, seq, LANES), lambda i: (prev(i) // N_HEADS, 0, prev(i) % N_HEADS)),
        out_shape=jax.ShapeDtypeStruct((b, seq, N_HEADS * HEAD_DIM), MXU_DTYPE),
        scratch_shapes=[
            pltpu.VMEM((lp + 2 * SUBLANES, LANES), F32),
            pltpu.VMEM((lp + 2 * SUBLANES, LANES), F32),
            pltpu.VMEM((lp + 2 * SUBLANES, LANES), F32),
            pltpu.VMEM((lp, LANES), F32),
            pltpu.VMEM((lp, LANES), F32),
            pltpu.VMEM((lp, LANES), F32),
            pltpu.VMEM((lp, 4 * LANES), F32),
            pltpu.VMEM((nc * LANES, LANES), F32),
            pltpu.VMEM((lp, LANES), F32),
            pltpu.VMEM((2, 2, lp, LANES), MXU_DTYPE),
            pltpu.VMEM((2, 2, nc * HEAD_DIM, LANES), MXU_DTYPE),
            pltpu.VMEM((2, 2, nc * HEAD_DIM, LANES), F32),
            pltpu.VMEM((2, 2, nc * SUBLANES, LANES), F32),
            pltpu.VMEM((2, lp, LANES), F32),
        ],
        compiler_params=_params(("arbitrary",)),
        name="gdn",
    )(proj3, proj3, proj3, proj3, ab3, conv_w, conv_w, conv_w, gate_prm, out_norm_g)


def _merge_kernel(h_ref, og_ref, ym_ref, gf_ref, gd_ref, wf_ref, wd_ref, wo_ref, n2_ref, wr_ref, br_ref,
                  h1_ref, hn_ref, ridx_ref, rw_ref, cnt_ref):
    @pl.when(pl.program_id(0) == 0)
    def _():
        cnt_ref[...] = jnp.zeros_like(cnt_ref)

    y_f = _dot(ym_ref[...], wf_ref[...])
    y_d = _dot(og_ref[...], wd_ref[...])
    merged = (jax.nn.sigmoid(gf_ref[...].astype(F32)) * y_f
              + jax.nn.sigmoid(gd_ref[...].astype(F32)) * y_d)
    h1 = h_ref[...] + _dot(merged.astype(MXU_DTYPE), wo_ref[...])
    h1_ref[...] = h1
    ms = jnp.mean(h1 * h1, axis=-1, keepdims=True)
    hn = h1 * lax.rsqrt(ms + NORM_EPS) * n2_ref[...]
    hn_ref[...] = hn

    x_hi = hn.astype(MXU_DTYPE)
    x_lo = (hn - x_hi.astype(F32)).astype(MXU_DTYPE)
    w = wr_ref[...]
    w_hi = w.astype(MXU_DTYPE)
    w_lo = (w - w_hi.astype(F32)).astype(MXU_DTYPE)
    logits = _dot(x_hi, w_hi) + _dot(x_hi, w_lo) + _dot(x_lo, w_hi) + br_ref[...]

    tm = logits.shape[0]
    lane = lax.broadcasted_iota(jnp.int32, (tm, LANES), 1)
    neg_inf = -jnp.inf
    is_group = lane < N_GROUPS
    gl = jnp.where(is_group, logits, neg_inf)
    ge = jnp.exp(gl - jnp.max(gl, axis=-1, keepdims=True))
    gp = ge / jnp.sum(ge, axis=-1, keepdims=True)
    p_group = jnp.max(gp, axis=-1, keepdims=True)
    g_idx = jnp.min(jnp.where(is_group & (gp == p_group), lane, LANES), axis=-1, keepdims=True)

    in_group = (lane >= N_GROUPS) & (lane < N_GROUPS + N_EXPERTS) & (((lane - N_GROUPS) >> 3) == g_idx)
    el = jnp.where(in_group, logits, neg_inf)
    ee = jnp.exp(el - jnp.max(el, axis=-1, keepdims=True))
    ep = ee / jnp.sum(ee, axis=-1, keepdims=True)
    v1 = jnp.max(jnp.where(in_group, ep, -1.0), axis=-1, keepdims=True)
    i1 = jnp.min(jnp.where(in_group & (ep == v1), lane, LANES), axis=-1, keepdims=True)
    rest = in_group & (lane != i1)
    v2 = jnp.max(jnp.where(rest, ep, -1.0), axis=-1, keepdims=True)
    i2 = jnp.min(jnp.where(rest & (ep == v2), lane, LANES), axis=-1, keepdims=True)
    denom = v1 + v2
    w1 = p_group * (v1 / denom)
    w2 = p_group * (v2 / denom)
    e1 = i1 - N_GROUPS
    e2 = i2 - N_GROUPS
    ridx_ref[...] = jnp.where(lane == 0, e1, jnp.where(lane == 1, e2, 0))
    rw_ref[...] = jnp.where(lane == 0, w1, jnp.where(lane == 1, w2, 0.0))
    onehots = jnp.where((lane == e1) | (lane == e2), 1.0, 0.0)
    cnt_ref[...] += jnp.broadcast_to(jnp.sum(onehots, axis=0, keepdims=True), cnt_ref.shape)


def _merge(h2d, og2d, ymix2d, proj2d, w_fourier, w_delta, w_out, norm2_g, w_router, b_router, tm):
    t, d = h2d.shape
    row = lambda w: pl.BlockSpec((tm, w), lambda i: (i, 0))
    const = lambda shape: pl.BlockSpec(shape, lambda i: (0, 0))
    return pl.pallas_call(
        _merge_kernel,
        grid=(t // tm,),
        in_specs=[
            row(d), row(d), row(F_WIDTH),
            pl.BlockSpec((tm, d), lambda i: (i, COL_GATE_F // d)),
            pl.BlockSpec((tm, d), lambda i: (i, COL_GATE_D // d)),
            const((F_WIDTH, d)), const((d, d)), const((d, d)), const((1, d)),
            const((d, LANES)), const((1, LANES)),
        ],
        out_specs=[row(d), row(d), row(LANES), row(LANES), const((SUBLANES, LANES))],
        out_shape=[
            jax.ShapeDtypeStruct((t, d), F32),
            jax.ShapeDtypeStruct((t, d), F32),
            jax.ShapeDtypeStruct((t, LANES), jnp.int32),
            jax.ShapeDtypeStruct((t, LANES), F32),
            jax.ShapeDtypeStruct((SUBLANES, LANES), F32),
        ],
        compiler_params=_params(("arbitrary",)),
        name="merge",
    )(h2d, og2d, ymix2d, proj2d, proj2d, w_fourier, w_delta, w_out, norm2_g, w_router, b_router)


def _plan_kernel(ridx_ref, cnt_ref, dest_ref, seg_ref, base_ref, run_ref):
    tt = ridx_ref.shape[0]

    @pl.when(pl.program_id(0) == 0)
    def _():
        cnt = cnt_ref[...].astype(jnp.int32)
        n_blk = (cnt + (EXPERT_BLOCK - 1)) // EXPERT_BLOCK
        padded = (n_blk * EXPERT_BLOCK).astype(F32)
        r = lax.broadcasted_iota(jnp.int32, (LANES, LANES), 0)
        c = lax.broadcasted_iota(jnp.int32, (LANES, LANES), 1)
        before = jnp.where(r < c, 1.0, 0.0).astype(MXU_DTYPE)
        start = _exact_right(padded, before)
        base_ref[...] = start
        run_ref[...] = jnp.zeros_like(run_ref)
        seg_ref[0:SUBLANES, :] = start.astype(jnp.int32)
        seg_ref[SUBLANES:2 * SUBLANES, :] = n_blk
        seg_ref[2 * SUBLANES:, :] = cnt

    lane = lax.broadcasted_iota(jnp.int32, (tt, LANES), 1)
    ridx = ridx_ref[...]
    oh0 = jnp.where(lane == ridx[:, 0:1], 1.0, 0.0)
    oh1 = jnp.where(lane == ridx[:, 1:2], 1.0, 0.0)
    r = lax.broadcasted_iota(jnp.int32, (tt, tt), 0)
    c = lax.broadcasted_iota(jnp.int32, (tt, tt), 1)
    earlier = jnp.where(c < r, 1.0, 0.0).astype(MXU_DTYPE)
    prefix = _dot(earlier, jnp.concatenate([oh0, oh1], axis=1).astype(MXU_DTYPE))
    c0 = jnp.sum(oh0, axis=0, keepdims=True)
    c1 = jnp.sum(oh1, axis=0, keepdims=True)
    first = base_ref[0:1, :] + run_ref[0:1, :]
    d0 = jnp.sum(oh0 * (prefix[:, :LANES] + first), axis=-1, keepdims=True)
    d1 = jnp.sum(oh1 * (prefix[:, LANES:] + first + c0), axis=-1, keepdims=True)
    run_ref[...] += jnp.broadcast_to(c0 + c1, run_ref.shape)
    dest_ref[...] = jnp.where(lane == 0, d0, jnp.where(lane == 1, d1, 0.0)).astype(jnp.int32)


def _plan(ridx, cnt, tt):
    t = ridx.shape[0]
    return pl.pallas_call(
        _plan_kernel,
        grid=(t // tt,),
        in_specs=[pl.BlockSpec((tt, LANES), lambda i: (i, 0)),
                  pl.BlockSpec((SUBLANES, LANES), lambda i: (0, 0))],
        out_specs=[pl.BlockSpec((tt, LANES), lambda i: (i, 0)),
                   pl.BlockSpec((3 * SUBLANES, LANES), lambda i: (0, 0))],
        out_shape=[jax.ShapeDtypeStruct((t, LANES), jnp.int32),
                   jax.ShapeDtypeStruct((3 * SUBLANES, LANES), jnp.int32)],
        scratch_shapes=[pltpu.VMEM((SUBLANES, LANES), F32), pltpu.VMEM((SUBLANES, LANES), F32)],
        compiler_params=_params(("arbitrary",)),
        name="plan",
    )(ridx, cnt)


def _scatter_kernel(dest_ref, hn_ref, xb_in_ref, xb_ref, sem, *, ts):
    del xb_in_ref

    def issue(t, carry):
        src = hn_ref.at[pl.ds(t, 1)]
        for k in range(TOP_K):
            pltpu.make_async_copy(src, xb_ref.at[pl.ds(dest_ref[0, TOP_K * t + k], 1)], sem).start(priority=k % 2)
        return carry

    lax.fori_loop(0, ts, issue, 0, unroll=DMA_UNROLL)
    for _ in range(TOP_K):
        pltpu.make_async_copy(hn_ref, xb_ref.at[pl.ds(0, ts)], sem).wait()


def _scatter(dest_flat, hn2d, xb_zero, ts):
    t, d = hn2d.shape
    return pl.pallas_call(
        functools.partial(_scatter_kernel, ts=ts),
        grid=(t // ts,),
        in_specs=[pl.BlockSpec((None, 1, TOP_K * ts), lambda i: (i, 0, 0), memory_space=pltpu.SMEM),
                  pl.BlockSpec((ts, d), lambda i: (i, 0)),
                  pl.BlockSpec(memory_space=pl.ANY)],
        out_specs=pl.BlockSpec(memory_space=pl.ANY),
        out_shape=jax.ShapeDtypeStruct(xb_zero.shape, F32),
        scratch_shapes=[pltpu.SemaphoreType.DMA(())],
        input_output_aliases={2: 0},
        compiler_params=_params(("arbitrary",)),
        name="scatter",
    )(dest_flat, hn2d, xb_zero)


def _expert_kernel(start_ref, nblk_ref, xb_ref, wg_ref, wu_ref, wd_ref, yb_ref,
                   wg16_ref, wu16_ref, wd16_ref, xbuf_ref, ybuf_ref, sem_in, sem_out):
    expert = pl.program_id(0)
    last = pl.num_programs(0) - 1
    n = nblk_ref[expert]
    first = start_ref[expert] // EXPERT_BLOCK
    total = start_ref[last] // EXPERT_BLOCK + nblk_ref[last]

    def rows(g):
        return pl.ds(pl.multiple_of(g * EXPERT_BLOCK, EXPERT_BLOCK), EXPERT_BLOCK)

    def x_copy(g):
        slot = g % EXPERT_X_SLOTS
        return pltpu.make_async_copy(xb_ref.at[rows(g)], xbuf_ref.at[slot], sem_in.at[slot])

    def y_copy(g):
        slot = g % EXPERT_Y_SLOTS
        return pltpu.make_async_copy(ybuf_ref.at[slot], yb_ref.at[rows(g)], sem_out.at[slot])

    @pl.when(n > 0)
    def _():
        @pl.when(first == 0)
        def _():
            for j in range(EXPERT_X_SLOTS - 1):
                @pl.when(j < total)
                def _():
                    x_copy(j).start()

        wg16_ref[...] = wg_ref[...].astype(MXU_DTYPE)
        wu16_ref[...] = wu_ref[...].astype(MXU_DTYPE)
        wd16_ref[...] = wd_ref[...].astype(MXU_DTYPE)

        def body(i, carry):
            g = first + i

            @pl.when(g + (EXPERT_X_SLOTS - 1) < total)
            def _():
                x_copy(g + (EXPERT_X_SLOTS - 1)).start()

            x_copy(g).wait()

            @pl.when(g >= EXPERT_Y_SLOTS)
            def _():
                y_copy(g - EXPERT_Y_SLOTS).wait()

            x = xbuf_ref[g % EXPERT_X_SLOTS].astype(MXU_DTYPE)
            gate = _dot(x, wg16_ref[...])
            up = _dot(x, wu16_ref[...])
            ybuf_ref[g % EXPERT_Y_SLOTS] = _dot((_silu(gate) * up).astype(MXU_DTYPE), wd16_ref[...])
            y_copy(g).start()
            return carry

        lax.fori_loop(0, n, body, 0)

    @pl.when(expert == last)
    def _():
        for j in range(EXPERT_Y_SLOTS):
            @pl.when(total > j)
            def _():
                y_copy(total - 1 - j).wait()


def _experts(seg_start, seg_blocks, xb, w_gate, w_up, w_down):
    n_slots, d = xb.shape
    n_exp, _, de = w_gate.shape
    grid_spec = pltpu.PrefetchScalarGridSpec(
        num_scalar_prefetch=2,
        grid=(n_exp,),
        in_specs=[
            pl.BlockSpec(memory_space=pl.ANY),
            pl.BlockSpec((None, d, de), lambda e, st, nb: (e, 0, 0)),
            pl.BlockSpec((None, d, de), lambda e, st, nb: (e, 0, 0)),
            pl.BlockSpec((None, de, d), lambda e, st, nb: (e, 0, 0)),
        ],
        out_specs=pl.BlockSpec(memory_space=pl.ANY),
        scratch_shapes=[
            pltpu.VMEM((d, de), MXU_DTYPE), pltpu.VMEM((d, de), MXU_DTYPE), pltpu.VMEM((de, d), MXU_DTYPE),
            pltpu.VMEM((EXPERT_X_SLOTS, EXPERT_BLOCK, d), F32), pltpu.VMEM((EXPERT_Y_SLOTS, EXPERT_BLOCK, d), F32),
            pltpu.SemaphoreType.DMA((EXPERT_X_SLOTS,)), pltpu.SemaphoreType.DMA((EXPERT_Y_SLOTS,)),
        ],
    )
    return pl.pallas_call(
        _expert_kernel,
        grid_spec=grid_spec,
        out_shape=jax.ShapeDtypeStruct((n_slots, d), F32),
        input_output_aliases={2: 0},
        compiler_params=_params(("arbitrary",)),
        name="experts",
    )(seg_start, seg_blocks, xb, w_gate, w_up, w_down)


def _combine_kernel(dest_ref, yb_ref, h1_ref, rw_ref, fg_ref, o_ref, buf_ref, res_ref, sem, out_sem,
                    *, tc, groups_per_batch):
    step = pl.program_id(0)
    last = pl.num_programs(0) - 1
    n_grp = tc // N_META

    def output_copies(at_step, action):
        slot = at_step % 2
        first = at_step * n_grp
        first_in_batch = first % groups_per_batch
        has_meta = (first_in_batch == 0) | (first_in_batch + n_grp > groups_per_batch)

        def copy_groups(q, n):
            grp = first + q
            dst = pl.multiple_of((grp - grp // groups_per_batch - 1) * N_META, N_META)
            copy = pltpu.make_async_copy(res_ref.at[slot, pl.ds(q * N_META, n * N_META)],
                                         o_ref.at[pl.ds(dst, n * N_META)], out_sem.at[slot])
            copy.start() if action == "start" else copy.wait()

        @pl.when(jnp.logical_not(has_meta))
        def _():
            copy_groups(0, n_grp)

        @pl.when(has_meta)
        def _():
            for q in range(n_grp):
                @pl.when((first + q) % groups_per_batch != 0)
                def _():
                    copy_groups(q, 1)

    def issue(t, carry):
        for k in range(TOP_K):
            pltpu.make_async_copy(yb_ref.at[pl.ds(dest_ref[0, TOP_K * t + k], 1)],
                                  buf_ref.at[k, pl.ds(t, 1)], sem).start(priority=k % 2)
        return carry

    lax.fori_loop(0, tc, issue, 0, unroll=DMA_UNROLL)
    for k in range(TOP_K):
        pltpu.make_async_copy(yb_ref.at[pl.ds(0, tc)], buf_ref.at[k], sem).wait()

    rw = rw_ref[...]
    h2 = h1_ref[...] + (buf_ref[0] * rw[:, 0:1] + buf_ref[1] * rw[:, 1:2])
    ms = jnp.mean(h2 * h2, axis=-1, keepdims=True)

    @pl.when(step >= 2)
    def _():
        output_copies(step - 2, "wait")

    res_ref[step % 2] = h2 * lax.rsqrt(ms + NORM_EPS) * fg_ref[...]
    output_copies(step, "start")

    @pl.when(step == last)
    def _():
        @pl.when(last >= 1)
        def _():
            output_copies(step - 1, "wait")

        output_copies(step, "wait")


def _combine(dest_flat, yb, h1, rw, final_g, tc, seq):
    t, d = h1.shape
    assert tc % N_META == 0 and seq % N_META == 0
    n_out = t - (t // seq) * N_META
    return pl.pallas_call(
        functools.partial(_combine_kernel, tc=tc, groups_per_batch=seq // N_META),
        grid=(t // tc,),
        in_specs=[pl.BlockSpec((None, 1, TOP_K * tc), lambda i: (i, 0, 0), memory_space=pltpu.SMEM),
                  pl.BlockSpec(memory_space=pl.ANY),
                  pl.BlockSpec((tc, d), lambda i: (i, 0)),
                  pl.BlockSpec((tc, LANES), lambda i: (i, 0)),
                  pl.BlockSpec((1, d), lambda i: (0, 0))],
        out_specs=pl.BlockSpec(memory_space=pl.ANY),
        out_shape=jax.ShapeDtypeStruct((n_out, d), F32),
        scratch_shapes=[pltpu.VMEM((TOP_K, tc, d), F32), pltpu.VMEM((2, tc, d), F32),
                        pltpu.SemaphoreType.DMA(()), pltpu.SemaphoreType.DMA((2,))],
        compiler_params=_params(("arbitrary",)),
        name="combine",
    )(dest_flat, yb, h1, rw, final_g)


def _tile(total, candidates):
    for c in candidates:
        if total % c == 0:
            return c
    raise ValueError(f"no tile for {total}")


def _dft_tables(seq):
    kp = -(-seq // LANES) * LANES
    nat = -(-(seq // 2 + 1) // 16) * 16
    mir = seq - nat
    c = np.arange(FGROUP_DIM, dtype=np.int64)
    ang_c = (2.0 * np.pi / FGROUP_DIM) * ((c[:, None] * c[None, :]) % FGROUP_DIM)
    cs128 = np.concatenate([np.cos(ang_c), np.sin(ang_c)], axis=1)
    k = np.arange(nat, dtype=np.int64)
    n = np.arange(seq, dtype=np.int64)
    ang_l = (2.0 * np.pi / seq) * ((k[:, None] * n[None, :]) % seq)
    twid = np.zeros((nat, 2 * kp), np.float32)
    twid[:, :seq] = np.cos(ang_l)
    twid[:, kp:kp + seq] = np.sin(ang_l)
    flip = np.zeros((mir, nat), np.float32)
    flip[np.arange(mir), mir - np.arange(mir)] = 1.0
    as_const = lambda t: jnp.asarray(t.astype(MXU_DTYPE))
    return as_const(cs128), as_const(twid), as_const(flip)


def kernel(x, meta_tokens, norm1_g, w_in, conv_w, a_log_fwd, dt_bias_fwd, a_log_bwd, dt_bias_bwd, out_norm_g, w_fourier, w_delta, w_out, norm2_g, w_router_group, b_router_group, w_router_expert, b_router_expert, w_gate_e, w_up_e, w_down_e, final_norm_g):
    bsz, n_real, d = x.shape
    seq = N_META + n_real
    n_tok = bsz * seq
    assert (CHUNK - N_META) % CHUNK + seq == -(-seq // CHUNK) * CHUNK, "meta tokens must fill the tail of chunk 0"
    assert w_in.shape[0] == 1, "single trunk layer"
    layer = 0

    meta = jnp.broadcast_to(meta_tokens[None].astype(x.dtype), (bsz, N_META, d))
    h = jnp.concatenate([meta, x], axis=1).reshape(n_tok, d)

    w = w_in[layer]
    o_f, o_qkv, o_z, o_ab, o_gf, o_gd = 0, F_WIDTH, F_WIDTH + 3 * QK_WIDTH, F_WIDTH + 4 * QK_WIDTH, \
        F_WIDTH + 4 * QK_WIDTH + 4 * N_HEADS, F_WIDTH + 4 * QK_WIDTH + 4 * N_HEADS + d
    w_cat = jnp.concatenate(
        [w[:, o_qkv:o_z], w[:, o_z:o_ab], w[:, o_gf:o_gd], w[:, o_gd:o_gd + d], w[:, o_f:o_qkv]],
        axis=1).astype(MXU_DTYPE)
    w_ab = jnp.pad(w[:, o_ab:o_gf], ((0, 0), (0, LANES - 4 * N_HEADS))).astype(MXU_DTYPE)

    proj, ab = _in_proj(h, norm1_g[layer][None, :], w_cat, w_ab, _tile(n_tok, (1032, 104, 8)))
    proj3 = proj.reshape(bsz, seq, PROJ_WIDTH)

    ymix = _fnet(proj3, *_dft_tables(seq))

    gate_prm = jnp.zeros((SUBLANES, LANES), F32)
    gate_prm = gate_prm.at[0, 0:N_HEADS].set(a_log_fwd[layer]).at[0, 2 * N_HEADS:3 * N_HEADS].set(a_log_bwd[layer])
    gate_prm = gate_prm.at[1, 0:N_HEADS].set(dt_bias_fwd[layer]).at[1, 2 * N_HEADS:3 * N_HEADS].set(dt_bias_bwd[layer])
    og = _gdn(proj3, ab.reshape(bsz, seq, LANES), conv_w[layer], gate_prm, out_norm_g[layer][None, :])

    w_router = jnp.zeros((d, LANES), F32)
    w_router = w_router.at[:, 0:N_GROUPS].set(w_router_group[layer])
    w_router = w_router.at[:, N_GROUPS:N_GROUPS + N_EXPERTS].set(w_router_expert[layer])
    b_router = jnp.zeros((1, LANES), F32)
    b_router = b_router.at[0, 0:N_GROUPS].set(b_router_group[layer])
    b_router = b_router.at[0, N_GROUPS:N_GROUPS + N_EXPERTS].set(b_router_expert[layer])
    tt = _tile(n_tok, (384, 128))
    h1, hn, ridx, rw, cnt = _merge(
        h, og.reshape(n_tok, d), ymix.reshape(n_tok, F_WIDTH), proj,
        w_fourier[layer].astype(MXU_DTYPE), w_delta[layer].astype(MXU_DTYPE), w_out[layer].astype(MXU_DTYPE),
        norm2_g[layer][None, :], w_router, b_router, _tile(n_tok, (688, 384, 128)))

    n_assign = n_tok * TOP_K
    n_blocks = -(-n_assign // EXPERT_BLOCK) + N_EXPERTS
    dest, seg = _plan(ridx, cnt, tt)
    dest_flat = dest[:, :TOP_K].reshape(n_tok // tt, 1, TOP_K * tt)
    seg_start = seg[0, :N_EXPERTS]
    seg_blocks = seg[SUBLANES, :N_EXPERTS]

    xb = _scatter(dest_flat, hn, jnp.zeros((n_blocks * EXPERT_BLOCK, d), F32), tt)
    yb = _experts(seg_start, seg_blocks, xb, w_gate_e[layer], w_up_e[layer], w_down_e[layer])
    out = _combine(dest_flat, yb, h1, rw, final_norm_g[None, :], tt, seq)
    return out.reshape(bsz, n_real, d)
```

```python
import functools
import math

import jax
import jax.numpy as jnp
import numpy as np
from jax import lax
from jax.experimental import pallas as pl
from jax.experimental.pallas import tpu as pltpu

F32 = jnp.float32
MXU_DTYPE = jnp.bfloat16

N_META = 16
CHUNK = 64
N_FGROUPS = 4
FGROUP_DIM = 128
F_WIDTH = N_FGROUPS * FGROUP_DIM
N_HEADS = 8
HEAD_DIM = 128
QK_WIDTH = N_HEADS * HEAD_DIM
CONV_K = 5
N_GROUPS = 8
EXPERTS_PER_GROUP = 8
N_EXPERTS = N_GROUPS * EXPERTS_PER_GROUP
TOP_K = 2
EXPERT_BLOCK = 256
NORM_EPS = 1e-6
LANES = 128
SUBLANES = 8
VMEM_LIMIT = 52 * 1024 * 1024
GDN_GROUP = 11
DMA_UNROLL = 8
EXPERT_X_SLOTS = 4
EXPERT_Y_SLOTS = 3

COL_QKV = 0
COL_Z = 3 * QK_WIDTH
COL_GATE_F = COL_Z + QK_WIDTH
COL_GATE_D = COL_GATE_F + 1024
COL_FIN = COL_GATE_D + 1024
PROJ_WIDTH = COL_FIN + F_WIDTH
PROJ_TN = 3328
PROJ_DTYPE = jnp.bfloat16


def _dot(a, b):
    return jnp.dot(a, b, preferred_element_type=F32)


def _dot_nt(a, b):
    return lax.dot_general(a, b, (((1,), (1,)), ((), ())), preferred_element_type=F32)


def _dot_tn(a, b):
    return lax.dot_general(a, b, (((0,), (0,)), ((), ())), preferred_element_type=F32)


def _split3(x):
    hi = x.astype(MXU_DTYPE)
    r1 = x - hi.astype(F32)
    mid = r1.astype(MXU_DTYPE)
    lo = (r1 - mid.astype(F32)).astype(MXU_DTYPE)
    return hi, mid, lo


def _exact_left(sel, x):
    hi, mid, lo = _split3(x)
    return _dot(sel, hi) + _dot(sel, mid) + _dot(sel, lo)


def _exact_right(x, sel):
    hi, mid, lo = _split3(x)
    return _dot(hi, sel) + _dot(mid, sel) + _dot(lo, sel)


def _pack_rows(x):
    half = x.shape[1] // 2
    lo = lax.bitcast_convert_type(x[:, :half].astype(jnp.bfloat16).astype(F32), jnp.uint32)
    hi = lax.bitcast_convert_type(x[:, half:].astype(jnp.bfloat16).astype(F32), jnp.uint32)
    return (lo >> 16) | hi


def _unpack_rows(u):
    lo = lax.bitcast_convert_type(u << 16, F32)
    hi = lax.bitcast_convert_type(u & jnp.uint32(0xFFFF0000), F32)
    return jnp.concatenate([lo, hi], axis=1)


def _silu(x):
    return x * jax.nn.sigmoid(x)


def _params(sem):
    return pltpu.CompilerParams(dimension_semantics=sem, vmem_limit_bytes=VMEM_LIMIT)


def _inproj_kernel(h_ref, g_ref, w_ref, wab_ref, o_ref, ab_ref, hn_ref):
    @pl.when(pl.program_id(1) == 0)
    def _():
        x = h_ref[...]
        ms = jnp.mean(x * x, axis=-1, keepdims=True)
        hn_ref[...] = (x * lax.rsqrt(ms + NORM_EPS) * g_ref[...]).astype(MXU_DTYPE)
        ab_ref[...] = _dot(hn_ref[...], wab_ref[...])

    o_ref[...] = _dot(hn_ref[...], w_ref[...]).astype(o_ref.dtype)


def _in_proj(h2d, gain, w_cat, w_ab, tm):
    t, d = h2d.shape
    n = w_cat.shape[1]
    return pl.pallas_call(
        _inproj_kernel,
        grid=(t // tm, n // PROJ_TN),
        in_specs=[
            pl.BlockSpec((tm, d), lambda i, j: (i, 0)),
            pl.BlockSpec((1, d), lambda i, j: (0, 0)),
            pl.BlockSpec((d, PROJ_TN), lambda i, j: (0, j)),
            pl.BlockSpec((d, LANES), lambda i, j: (0, 0)),
        ],
        out_specs=[pl.BlockSpec((tm, PROJ_TN), lambda i, j: (i, j)),
                   pl.BlockSpec((tm, LANES), lambda i, j: (i, 0))],
        out_shape=[jax.ShapeDtypeStruct((t, n), PROJ_DTYPE),
                   jax.ShapeDtypeStruct((t, LANES), F32)],
        scratch_shapes=[pltpu.VMEM((tm, d), MXU_DTYPE)],
        compiler_params=_params(("arbitrary", "arbitrary")),
        name="in_proj",
    )(h2d, gain, w_cat, w_ab)


def _fnet_kernel(x_ref, cs_ref, tw_ref, flip_ref, o_ref, r_ref, *, seq, kp, nat, scale):
    tail = kp - seq
    if tail:
        r_ref[seq:kp, :] = jnp.zeros((tail, F_WIDTH), MXU_DTYPE)
        r_ref[kp + seq:, :] = jnp.zeros((tail, F_WIDTH), MXU_DTYPE)
    cs = cs_ref[...]
    for g in range(N_FGROUPS):
        cols = slice(g * FGROUP_DIM, (g + 1) * FGROUP_DIM)
        p = _dot(x_ref[:, cols].astype(MXU_DTYPE), cs)
        r_ref[0:seq, cols] = p[:, :FGROUP_DIM].astype(MXU_DTYPE)
        r_ref[kp:kp + seq, cols] = p[:, FGROUP_DIM:].astype(MXU_DTYPE)

    a = _dot(tw_ref[:, :kp], r_ref[0:kp, :])
    b = _dot(tw_ref[:, kp:], r_ref[kp:, :])
    o_ref[0:nat, :] = ((a - b) * scale).astype(o_ref.dtype)
    mirrored = ((a + b) * scale).astype(MXU_DTYPE)
    o_ref[nat:, :] = _dot(flip_ref[...], mirrored).astype(o_ref.dtype)


def _fnet(proj3, cs128, twid, flip):
    b, seq, _ = proj3.shape
    nat, kp = twid.shape[0], twid.shape[1] // 2
    scale = 1.0 / math.sqrt(seq * FGROUP_DIM)
    const = lambda shape: pl.BlockSpec(shape, lambda bi: (0, 0))
    return pl.pallas_call(
        functools.partial(_fnet_kernel, seq=seq, kp=kp, nat=nat, scale=scale),
        grid=(b,),
        in_specs=[
            pl.BlockSpec((None, seq, F_WIDTH), lambda bi: (bi, 0, COL_FIN // F_WIDTH)),
            const((FGROUP_DIM, 2 * FGROUP_DIM)), const((nat, 2 * kp)), const((seq - nat, nat)),
        ],
        out_specs=pl.BlockSpec((None, seq, F_WIDTH), lambda bi: (bi, 0, 0)),
        out_shape=jax.ShapeDtypeStruct((b, seq, F_WIDTH), MXU_DTYPE),
        scratch_shapes=[pltpu.VMEM((2 * kp, F_WIDTH), MXU_DTYPE)],
        compiler_params=_params(("arbitrary",)),
        name="fnet",
    )(proj3, cs128, twid, flip)


def _aligned(x, m):
    return x if isinstance(x, int) else pl.multiple_of(x, m)


def _gdn_kernel(q_ref, k_ref, v_ref, z_ref, ab_ref, cwq_ref, cwk_ref, cwv_ref, prm_ref, ong_ref,
                o_ref,
                rawq_ref, rawk_ref, rawv_ref, qs_ref, ks_ref, vs_ref, gsel_ref, grow_ref, gmix_ref,
                qe_ref, mp_ref, nn_ref, dd_ref, osum_ref,
                *, seq, lp, group, n_heads_total):
    step = pl.program_id(0)
    head = jnp.minimum(step, n_heads_total - 1) % N_HEADS
    nc = lp // CHUNK
    padr = lp - seq
    halo = SUBLANES
    row64 = lax.broadcasted_iota(jnp.int32, (CHUNK, LANES), 0)
    lane64 = lax.broadcasted_iota(jnp.int32, (CHUNK, LANES), 1)

    def for_groups(fn, carry=0):
        n_full, rem = divmod(nc, group)
        if n_full:
            carry = lax.fori_loop(0, n_full, lambda i, c: fn([i * group + j for j in range(group)], c), carry)
        if rem:
            carry = fn([n_full * group + j for j in range(rem)], carry)
        return carry

    cur_half = step % 2
    qe_w, mp_w, nn_w, dd_w, osum_w = (r.at[cur_half] for r in (qe_ref, mp_ref, nn_ref, dd_ref, osum_ref))
    qe_r, mp_r, nn_r, dd_r, osum_r = (r.at[1 - cur_half] for r in (qe_ref, mp_ref, nn_ref, dd_ref, osum_ref))

    streams = ((rawq_ref, q_ref, cwq_ref, qs_ref, HEAD_DIM ** -0.5),
               (rawk_ref, k_ref, cwk_ref, ks_ref, 1.0),
               (rawv_ref, v_ref, cwv_ref, vs_ref, None))
    def stage_conv_inputs():
        for raw_ref, src_ref, _, _, _ in streams:
            raw_ref[0:padr + halo, :] = jnp.zeros((padr + halo, LANES), F32)
            raw_ref[padr + halo:padr + halo + seq, :] = src_ref[...].astype(F32)
            raw_ref[lp + halo:lp + 2 * halo, :] = jnp.zeros((halo, LANES), F32)

    def conv_chunk(c):
        base = pl.multiple_of(c * CHUNK, CHUNK)
        for raw_ref, _, cw_ref, dst_ref, l2_scale in streams:
            cw = cw_ref[...]
            acc = jnp.zeros((CHUNK, LANES), F32)
            for j in range(CONV_K):
                off = halo - (CONV_K - 1) // 2 + j
                acc = acc + cw[j:j + 1, :] * raw_ref[pl.ds(base + off, CHUNK), :]
            y = _silu(acc)
            y = jnp.where(row64 + base >= padr, y, 0.0)
            if l2_scale is not None:
                y = y * lax.rsqrt(jnp.sum(y * y, axis=-1, keepdims=True) + NORM_EPS) * l2_scale
            dst_ref[pl.ds(base, CHUNK), :] = y

    raw_ref = rawq_ref

    def stage_gate_inputs():
        raw_ref[0:padr, :] = jnp.zeros((padr, LANES), F32)
        raw_ref[padr:padr + seq, :] = ab_ref[...]

    neg_a = -jnp.exp(prm_ref[0:1, :])
    dt_bias = prm_ref[1:2, :]
    is_g = (lane64 < 8) | ((lane64 >= 16) & (lane64 < 24))
    is_beta = ((lane64 >= 8) & (lane64 < 16)) | ((lane64 >= 24) & (lane64 < 32))
    r128 = lax.broadcasted_iota(jnp.int32, (2 * CHUNK, CHUNK), 0)
    c128 = lax.broadcasted_iota(jnp.int32, (2 * CHUNK, CHUNK), 1)
    tri2 = jnp.where(((r128 < CHUNK) & (r128 >= c128)) | ((r128 >= CHUNK) & (r128 - CHUNK <= c128)),
                     1.0, 0.0).astype(MXU_DTYPE)
    sel_r = lax.broadcasted_iota(jnp.int32, (LANES, 4 * LANES), 0)
    sel_c = lax.broadcasted_iota(jnp.int32, (LANES, 4 * LANES), 1)
    n_gate_cols = 4 * N_HEADS
    sel_packed = jnp.where((sel_r < 3 * n_gate_cols)
                           & (sel_r % n_gate_cols == head + N_HEADS * (sel_c // LANES)), 1.0, 0.0).astype(MXU_DTYPE)
    tri2x3 = jnp.concatenate([tri2, tri2, tri2], axis=1)

    def gate_batch_group(cs, carry):
        bases = [_aligned(c * CHUNK, CHUNK) for c in cs]
        gates, parts = [], []
        for base in bases:
            ab = raw_ref[pl.ds(base, CHUNK), :]
            g = neg_a * jax.nn.softplus(ab + dt_bias)
            beta = jax.nn.sigmoid(ab)
            gt = jnp.where(is_g, g, jnp.where(is_beta, beta, 0.0))
            gt = jnp.where(row64 + base >= padr, gt, 0.0)
            gates.append(gt)
            parts.append(jnp.concatenate(_split3(gt), axis=0))
        cums = [_dot(tri2x3, p) for p in parts]
        for c, base, cm, gt in zip(cs, bases, cums, gates):
            m = jnp.where(lane64 < 8, cm[:CHUNK], jnp.where((lane64 >= 16) & (lane64 < 24), cm[CHUNK:], gt))
            gmix_ref[pl.ds(base, CHUNK), :] = m
            grow_ref[pl.ds(_aligned(c * LANES, LANES), LANES), :] = jnp.concatenate([m, m], axis=0).T
        return carry

    def gate_head_group(cs, carry):
        rows = len(cs) * CHUNK
        base = _aligned(cs[0] * CHUNK, CHUNK)
        hi, mid, lo = _split3(gmix_ref[pl.ds(base, rows), :])
        packed = (hi.astype(F32) + pltpu.roll(mid.astype(F32), n_gate_cols, 1)
                  + pltpu.roll(lo.astype(F32), 2 * n_gate_cols, 1)).astype(MXU_DTYPE)
        gsel_ref[pl.ds(base, rows), :] = _dot(packed, sel_packed)
        return carry

    fwd_half = lane64 < CHUNK
    col64 = lane64 & (CHUNK - 1)
    incl2 = (fwd_half & (row64 >= col64)) | (~fwd_half & (row64 <= col64))
    strict2 = (fwd_half & (row64 > col64)) | (~fwd_half & (row64 < col64))
    eye2 = jnp.where(row64 == col64, 1.0, 0.0).astype(F32)

    def blockdiag(p):
        return jnp.concatenate([jnp.where(fwd_half, p, 0.0), jnp.where(fwd_half, 0.0, p)],
                               axis=0).astype(MXU_DTYPE)

    n_neumann = int(math.log2(CHUNK)) - 2
    n_stages = n_neumann + 6

    def chunk_group(cs, carry, with_scan):
        n = len(cs)
        bases = [_aligned(c * CHUNK, CHUNK) for c in cs]
        cb128 = [_aligned(c * LANES, LANES) for c in cs]
        cb8 = [_aligned(c * SUBLANES, SUBLANES) for c in cs]
        stage = [0]

        def end_of_stage(carry):
            s = stage[0]
            stage[0] += 1
            if with_scan:
                for j in range(n):
                    if j * n_stages // n == s:
                        carry = scan_step(cs[j], carry)
            return carry

        def gate_cols(j):
            gs = gsel_ref[pl.ds(bases[j], CHUNK), :]
            return gs[:, 0:LANES], gs[:, LANES:2 * LANES], gs[:, 2 * LANES:3 * LANES], gs[:, 3 * LANES:]

        kq = []
        for j in range(n):
            k16 = ks_ref[pl.ds(bases[j], CHUNK), :].astype(MXU_DTYPE)
            q16 = qs_ref[pl.ds(bases[j], CHUNK), :].astype(MXU_DTYPE)
            kq.append(_dot_nt(jnp.concatenate([k16, q16], axis=0), jnp.concatenate([k16, k16], axis=0)))
        carry = end_of_stage(carry)

        ts, ps, qkds = [], [], []
        for j in range(n):
            gf, bf, gb, bb = gate_cols(j)
            row_f = grow_ref[pl.ds(cb128[j] + head, 1), :]
            row_b = grow_ref[pl.ds(cb128[j] + 2 * N_HEADS + head, 1), :]
            g_col = jnp.where(fwd_half, gf, gb)
            g_row = jnp.where(fwd_half, row_f, row_b)
            decay = jnp.where(incl2, jnp.exp(jnp.where(incl2, g_col - g_row, 0.0)), 0.0)
            a2 = jnp.where(strict2, kq[j][:CHUNK] * jnp.where(fwd_half, bf, bb) * decay, 0.0)
            qkds.append(jnp.where(incl2, kq[j][CHUNK:] * decay, 0.0))
            ts.append(eye2 - a2)
            ps.append(a2)

        ps = [_dot(p.astype(MXU_DTYPE), blockdiag(p)) for p in ps]
        carry = end_of_stage(carry)
        for _ in range(n_neumann):
            tps = [_dot(jnp.concatenate([t, p], axis=0).astype(MXU_DTYPE), blockdiag(p)) for t, p in zip(ts, ps)]
            carry = end_of_stage(carry)
            ts = [t + tp[:CHUNK] for t, tp in zip(ts, tps)]
            ps = [tp[CHUNK:] for tp in tps]
        ts = [t + _dot(t.astype(MXU_DTYPE), blockdiag(p)) for t, p in zip(ts, ps)]
        carry = end_of_stage(carry)

        uws = []
        for j in range(n):
            gf, bf, gb, bb = gate_cols(j)
            k = ks_ref[pl.ds(bases[j], CHUNK), :]
            v = vs_ref[pl.ds(bases[j], CHUNK), :]
            x_f = jnp.concatenate([v * bf, k * bf * jnp.exp(gf)], axis=1)
            x_b = jnp.concatenate([v * bb, k * bb * jnp.exp(gb)], axis=1)
            uw = _dot(blockdiag(ts[j]), jnp.concatenate([x_f, x_b], axis=0).astype(MXU_DTYPE))
            uws.append(uw.astype(MXU_DTYPE))
        carry = end_of_stage(carry)

        xxs = [_dot(blockdiag(qkds[j]), uws[j]) for j in range(n)]
        carry = end_of_stage(carry)
        for j in range(n):
            gf, _, gb, _ = gate_cols(j)
            k = ks_ref[pl.ds(bases[j], CHUNK), :]
            q = qs_ref[pl.ds(bases[j], CHUNK), :]
            xx = xxs[j]
            osum_w[pl.ds(bases[j], CHUNK), :] = xx[:CHUNK, :LANES] + xx[CHUNK:, :LANES]
            for d, gc in enumerate((gf, gb)):
                rows = slice(d * CHUNK, (d + 1) * CHUNK)
                g_last = gc[CHUNK - 1:CHUNK, :] if d == 0 else gc[0:1, :]
                k_dec = k * jnp.exp(g_last - gc)
                mn = _dot_tn(k_dec.astype(MXU_DTYPE), uws[j][rows])
                qe_w[d, pl.ds(bases[j], CHUNK), :] = (q * jnp.exp(gc) - xx[rows, LANES:]).astype(MXU_DTYPE)
                nn_w[d, pl.ds(cb128[j], HEAD_DIM), :] = mn[:, :LANES]
                mp_w[d, pl.ds(cb128[j], HEAD_DIM), :] = mn[:, LANES:].astype(MXU_DTYPE)
                dd_w[d, pl.ds(cb8[j], SUBLANES), :] = jnp.broadcast_to(jnp.exp(g_last), (SUBLANES, LANES))
        carry = end_of_stage(carry)
        assert stage[0] == n_stages
        return carry

    def scan_step(i, carry):
        new = []
        for d in range(2):
            s = carry[d]
            c = i if d == 0 else nc - 1 - i
            base = _aligned(c * CHUNK, CHUNK)
            cbase = _aligned(c * HEAD_DIM, HEAD_DIM)
            dbase = _aligned(c * SUBLANES, SUBLANES)
            s16 = s.astype(MXU_DTYPE)
            osum_r[pl.ds(base, CHUNK), :] += _dot(qe_r[d, pl.ds(base, CHUNK), :], s16)
            dec = dd_r[d, pl.ds(dbase, 1), :]
            new.append(dec * s - _dot(mp_r[d, pl.ds(cbase, HEAD_DIM), :], s16)
                       + nn_r[d, pl.ds(cbase, HEAD_DIM), :])
        return tuple(new)

    zero_state = jnp.zeros((HEAD_DIM, HEAD_DIM), F32)

    def write_output():
        gain = ong_ref[...]

        def finish(o, z):
            ms = jnp.mean(o * o, axis=-1, keepdims=True)
            return (o * lax.rsqrt(ms + NORM_EPS) * gain * _silu(z.astype(F32))).astype(o_ref.dtype)

        first = CHUNK - padr if padr else 0
        if first:
            o_ref[0:first, :] = finish(osum_r[padr:CHUNK, :], z_ref[0:first, :])

        def out_body(c, carry):
            base = pl.multiple_of(c * CHUNK, CHUNK)
            dst = pl.multiple_of(c * CHUNK - padr, SUBLANES)
            o_ref[pl.ds(dst, CHUNK), :] = finish(osum_r[pl.ds(base, CHUNK), :], z_ref[pl.ds(dst, CHUNK), :])
            return carry

        lax.fori_loop(1 if first else 0, nc, out_body, 0, unroll=4)

    def conv_only(c, carry):
        conv_chunk(c)
        return carry

    @pl.when(step < n_heads_total)
    def _():
        stage_conv_inputs()
        lax.fori_loop(0, nc, conv_only, 0, unroll=3)

    @pl.when((step < n_heads_total) & (head == 0))
    def _():
        stage_gate_inputs()
        for_groups(gate_batch_group)

    @pl.when(step < n_heads_total)
    def _():
        for_groups(gate_head_group)

    @pl.when(step == 0)
    def _():
        for_groups(functools.partial(chunk_group, with_scan=False))

    @pl.when((step > 0) & (step < n_heads_total))
    def _():
        for_groups(functools.partial(chunk_group, with_scan=True), (zero_state, zero_state))
        write_output()

    @pl.when(step == n_heads_total)
    def _():
        lax.fori_loop(0, nc, scan_step, (zero_state, zero_state))
        write_output()


def _gdn(proj3, ab3, conv_w, gate_prm, out_norm_g):
    b, seq, _ = proj3.shape
    lp = -(-seq // CHUNK) * CHUNK
    nc = lp // CHUNK
    n_total = b * N_HEADS
    cur = lambda i: jnp.minimum(i, n_total - 1)
    prev = lambda i: jnp.maximum(i - 1, 0)
    col = lambda off: pl.BlockSpec((None, seq, LANES), lambda i: (cur(i) // N_HEADS, 0, off + cur(i) % N_HEADS))
    cw = lambda off: pl.BlockSpec((CONV_K, LANES), lambda i: (0, off + cur(i) % N_HEADS))
    return pl.pallas_call(
        functools.partial(_gdn_kernel, seq=seq, lp=lp, group=GDN_GROUP, n_heads_total=n_total),
        grid=(n_total + 1,),
        in_specs=[
            col(COL_QKV // LANES), col(COL_QKV // LANES + N_HEADS), col(COL_QKV // LANES + 2 * N_HEADS),
            pl.BlockSpec((None, seq, LANES),
                         lambda i: (prev(i) // N_HEADS, 0, COL_Z // LANES + prev(i) % N_HEADS)),
            pl.BlockSpec((None, seq, LANES), lambda i: (cur(i) // N_HEADS, 0, 0)),
            cw(0), cw(N_HEADS), cw(2 * N_HEADS),
            pl.BlockSpec((SUBLANES, LANES), lambda i: (0, 0)),
            pl.BlockSpec((1, LANES), lambda i: (0, 0)),
        ],
        out_specs=pl.BlockSpec((None, seq, LANES), lambda i: (prev(i) // N_HEADS, 0, prev(i) % N_HEADS)),
        out_shape=jax.ShapeDtypeStruct((b, seq, N_HEADS * HEAD_DIM), MXU_DTYPE),
        scratch_shapes=[
            pltpu.VMEM((lp + 2 * SUBLANES, LANES), F32),
            pltpu.VMEM((lp + 2 * SUBLANES, LANES), F32),
            pltpu.VMEM((lp + 2 * SUBLANES, LANES), F32),
            pltpu.VMEM((lp, LANES), F32),
            pltpu.VMEM((lp, LANES), F32),
            pltpu.VMEM((lp, LANES), F32),
            pltpu.VMEM((lp, 4 * LANES), F32),
            pltpu.VMEM((nc * LANES, LANES), F32),
            pltpu.VMEM((lp, LANES), F32),
            pltpu.VMEM((2, 2, lp, LANES), MXU_DTYPE),
            pltpu.VMEM((2, 2, nc * HEAD_DIM, LANES), MXU_DTYPE),
            pltpu.VMEM((2, 2, nc * HEAD_DIM, LANES), F32),
            pltpu.VMEM((2, 2, nc * SUBLANES, LANES), F32),
            pltpu.VMEM((2, lp, LANES), F32),
        ],
        compiler_params=_params(("arbitrary",)),
        name="gdn",
    )(proj3, proj3, proj3, proj3, ab3, conv_w, conv_w, conv_w, gate_prm, out_norm_g)


def _merge_kernel(h_ref, og_ref, ym_ref, gf_ref, gd_ref, wf_ref, wd_ref, wo_ref, n2_ref, wr_ref, br_ref,
                  h1_ref, hn_ref, ridx_ref, rw_ref, cnt_ref):
    @pl.when(pl.program_id(0) == 0)
    def _():
        cnt_ref[...] = jnp.zeros_like(cnt_ref)

    y_f = _dot(ym_ref[...], wf_ref[...])
    y_d = _dot(og_ref[...], wd_ref[...])
    merged = (jax.nn.sigmoid(gf_ref[...].astype(F32)) * y_f
              + jax.nn.sigmoid(gd_ref[...].astype(F32)) * y_d)
    h1 = h_ref[...] + _dot(merged.astype(MXU_DTYPE), wo_ref[...])
    h1_ref[...] = h1
    ms = jnp.mean(h1 * h1, axis=-1, keepdims=True)
    hn = h1 * lax.rsqrt(ms + NORM_EPS) * n2_ref[...]
    hn_ref[...] = _pack_rows(hn)

    x_hi = hn.astype(MXU_DTYPE)
    x_lo = (hn - x_hi.astype(F32)).astype(MXU_DTYPE)
    w = wr_ref[...]
    w_hi = w.astype(MXU_DTYPE)
    w_lo = (w - w_hi.astype(F32)).astype(MXU_DTYPE)
    logits = _dot(x_hi, w_hi) + _dot(x_hi, w_lo) + _dot(x_lo, w_hi) + br_ref[...]

    tm = logits.shape[0]
    lane = lax.broadcasted_iota(jnp.int32, (tm, LANES), 1)
    neg_inf = -jnp.inf
    is_group = lane < N_GROUPS
    gl = jnp.where(is_group, logits, neg_inf)
    ge = jnp.exp(gl - jnp.max(gl, axis=-1, keepdims=True))
    gp = ge / jnp.sum(ge, axis=-1, keepdims=True)
    p_group = jnp.max(gp, axis=-1, keepdims=True)
    g_idx = jnp.min(jnp.where(is_group & (gp == p_group), lane, LANES), axis=-1, keepdims=True)

    in_group = (lane >= N_GROUPS) & (lane < N_GROUPS + N_EXPERTS) & (((lane - N_GROUPS) >> 3) == g_idx)
    el = jnp.where(in_group, logits, neg_inf)
    ee = jnp.exp(el - jnp.max(el, axis=-1, keepdims=True))
    ep = ee / jnp.sum(ee, axis=-1, keepdims=True)
    v1 = jnp.max(jnp.where(in_group, ep, -1.0), axis=-1, keepdims=True)
    i1 = jnp.min(jnp.where(in_group & (ep == v1), lane, LANES), axis=-1, keepdims=True)
    rest = in_group & (lane != i1)
    v2 = jnp.max(jnp.where(rest, ep, -1.0), axis=-1, keepdims=True)
    i2 = jnp.min(jnp.where(rest & (ep == v2), lane, LANES), axis=-1, keepdims=True)
    denom = v1 + v2
    w1 = p_group * (v1 / denom)
    w2 = p_group * (v2 / denom)
    e1 = i1 - N_GROUPS
    e2 = i2 - N_GROUPS
    ridx_ref[...] = jnp.where(lane == 0, e1, jnp.where(lane == 1, e2, 0))
    rw_ref[...] = jnp.where(lane == 0, w1, jnp.where(lane == 1, w2, 0.0))
    onehots = jnp.where((lane == e1) | (lane == e2), 1.0, 0.0)
    cnt_ref[...] += jnp.broadcast_to(jnp.sum(onehots, axis=0, keepdims=True), cnt_ref.shape)


def _merge(h2d, og2d, ymix2d, proj2d, w_fourier, w_delta, w_out, norm2_g, w_router, b_router, tm):
    t, d = h2d.shape
    row = lambda w: pl.BlockSpec((tm, w), lambda i: (i, 0))
    const = lambda shape: pl.BlockSpec(shape, lambda i: (0, 0))
    return pl.pallas_call(
        _merge_kernel,
        grid=(t // tm,),
        in_specs=[
            row(d), row(d), row(F_WIDTH),
            pl.BlockSpec((tm, d), lambda i: (i, COL_GATE_F // d)),
            pl.BlockSpec((tm, d), lambda i: (i, COL_GATE_D // d)),
            const((F_WIDTH, d)), const((d, d)), const((d, d)), const((1, d)),
            const((d, LANES)), const((1, LANES)),
        ],
        out_specs=[row(d), row(d // 2), row(LANES), row(LANES), const((SUBLANES, LANES))],
        out_shape=[
            jax.ShapeDtypeStruct((t, d), F32),
            jax.ShapeDtypeStruct((t, d // 2), jnp.uint32),
            jax.ShapeDtypeStruct((t, LANES), jnp.int32),
            jax.ShapeDtypeStruct((t, LANES), F32),
            jax.ShapeDtypeStruct((SUBLANES, LANES), F32),
        ],
        compiler_params=_params(("arbitrary",)),
        name="merge",
    )(h2d, og2d, ymix2d, proj2d, proj2d, w_fourier, w_delta, w_out, norm2_g, w_router, b_router)


def _plan_kernel(ridx_ref, cnt_ref, dest_ref, seg_ref, base_ref, run_ref):
    tt = ridx_ref.shape[0]

    @pl.when(pl.program_id(0) == 0)
    def _():
        cnt = cnt_ref[...].astype(jnp.int32)
        n_blk = (cnt + (EXPERT_BLOCK - 1)) // EXPERT_BLOCK
        padded = (n_blk * EXPERT_BLOCK).astype(F32)
        r = lax.broadcasted_iota(jnp.int32, (LANES, LANES), 0)
        c = lax.broadcasted_iota(jnp.int32, (LANES, LANES), 1)
        before = jnp.where(r < c, 1.0, 0.0).astype(MXU_DTYPE)
        start = _exact_right(padded, before)
        base_ref[...] = start
        run_ref[...] = jnp.zeros_like(run_ref)
        seg_ref[0:SUBLANES, :] = start.astype(jnp.int32)
        seg_ref[SUBLANES:2 * SUBLANES, :] = n_blk
        seg_ref[2 * SUBLANES:, :] = cnt

    lane = lax.broadcasted_iota(jnp.int32, (tt, LANES), 1)
    ridx = ridx_ref[...]
    oh0 = jnp.where(lane == ridx[:, 0:1], 1.0, 0.0)
    oh1 = jnp.where(lane == ridx[:, 1:2], 1.0, 0.0)
    r = lax.broadcasted_iota(jnp.int32, (tt, tt), 0)
    c = lax.broadcasted_iota(jnp.int32, (tt, tt), 1)
    earlier = jnp.where(c < r, 1.0, 0.0).astype(MXU_DTYPE)
    prefix = _dot(earlier, jnp.concatenate([oh0, oh1], axis=1).astype(MXU_DTYPE))
    c0 = jnp.sum(oh0, axis=0, keepdims=True)
    c1 = jnp.sum(oh1, axis=0, keepdims=True)
    first = base_ref[0:1, :] + run_ref[0:1, :]
    d0 = jnp.sum(oh0 * (prefix[:, :LANES] + first), axis=-1, keepdims=True)
    d1 = jnp.sum(oh1 * (prefix[:, LANES:] + first + c0), axis=-1, keepdims=True)
    run_ref[...] += jnp.broadcast_to(c0 + c1, run_ref.shape)
    dest_ref[...] = jnp.where(lane == 0, d0, jnp.where(lane == 1, d1, 0.0)).astype(jnp.int32)


def _plan(ridx, cnt, tt):
    t = ridx.shape[0]
    return pl.pallas_call(
        _plan_kernel,
        grid=(t // tt,),
        in_specs=[pl.BlockSpec((tt, LANES), lambda i: (i, 0)),
                  pl.BlockSpec((SUBLANES, LANES), lambda i: (0, 0))],
        out_specs=[pl.BlockSpec((tt, LANES), lambda i: (i, 0)),
                   pl.BlockSpec((3 * SUBLANES, LANES), lambda i: (0, 0))],
        out_shape=[jax.ShapeDtypeStruct((t, LANES), jnp.int32),
                   jax.ShapeDtypeStruct((3 * SUBLANES, LANES), jnp.int32)],
        scratch_shapes=[pltpu.VMEM((SUBLANES, LANES), F32), pltpu.VMEM((SUBLANES, LANES), F32)],
        compiler_params=_params(("arbitrary",)),
        name="plan",
    )(ridx, cnt)


def _scatter_kernel(dest_ref, hn_ref, xb_in_ref, xb_ref, sem, *, ts):
    del xb_in_ref

    def issue(t, carry):
        src = hn_ref.at[pl.ds(t, 1)]
        for k in range(TOP_K):
            pltpu.make_async_copy(src, xb_ref.at[pl.ds(dest_ref[0, TOP_K * t + k], 1)], sem).start(priority=k % 2)
        return carry

    lax.fori_loop(0, ts, issue, 0, unroll=DMA_UNROLL)
    for _ in range(TOP_K):
        pltpu.make_async_copy(hn_ref, xb_ref.at[pl.ds(0, ts)], sem).wait()


def _scatter(dest_flat, hn2d, xb_zero, ts):
    t, d = hn2d.shape
    return pl.pallas_call(
        functools.partial(_scatter_kernel, ts=ts),
        grid=(t // ts,),
        in_specs=[pl.BlockSpec((None, 1, TOP_K * ts), lambda i: (i, 0, 0), memory_space=pltpu.SMEM),
                  pl.BlockSpec((ts, d), lambda i: (i, 0)),
                  pl.BlockSpec(memory_space=pl.ANY)],
        out_specs=pl.BlockSpec(memory_space=pl.ANY),
        out_shape=jax.ShapeDtypeStruct(xb_zero.shape, xb_zero.dtype),
        scratch_shapes=[pltpu.SemaphoreType.DMA(())],
        input_output_aliases={2: 0},
        compiler_params=_params(("arbitrary",)),
        name="scatter",
    )(dest_flat, hn2d, xb_zero)


def _expert_kernel(start_ref, nblk_ref, xb_ref, wg_ref, wu_ref, wd_ref, yb_ref,
                   wg16_ref, wu16_ref, wd16_ref, xbuf_ref, ybuf_ref, sem_in, sem_out):
    expert = pl.program_id(0)
    last = pl.num_programs(0) - 1
    n = nblk_ref[expert]
    first = start_ref[expert] // EXPERT_BLOCK
    total = start_ref[last] // EXPERT_BLOCK + nblk_ref[last]

    def rows(g):
        return pl.ds(pl.multiple_of(g * EXPERT_BLOCK, EXPERT_BLOCK), EXPERT_BLOCK)

    def x_copy(g):
        slot = g % EXPERT_X_SLOTS
        return pltpu.make_async_copy(xb_ref.at[rows(g)], xbuf_ref.at[slot], sem_in.at[slot])

    def y_copy(g):
        slot = g % EXPERT_Y_SLOTS
        return pltpu.make_async_copy(ybuf_ref.at[slot], yb_ref.at[rows(g)], sem_out.at[slot])

    @pl.when(n > 0)
    def _():
        @pl.when(first == 0)
        def _():
            for j in range(EXPERT_X_SLOTS - 1):
                @pl.when(j < total)
                def _():
                    x_copy(j).start()

        wg16_ref[...] = wg_ref[...].astype(MXU_DTYPE)
        wu16_ref[...] = wu_ref[...].astype(MXU_DTYPE)
        wd16_ref[...] = wd_ref[...].astype(MXU_DTYPE)

        def body(i, carry):
            g = first + i

            @pl.when(g + (EXPERT_X_SLOTS - 1) < total)
            def _():
                x_copy(g + (EXPERT_X_SLOTS - 1)).start()

            x_copy(g).wait()

            @pl.when(g >= EXPERT_Y_SLOTS)
            def _():
                y_copy(g - EXPERT_Y_SLOTS).wait()

            x = _unpack_rows(xbuf_ref[g % EXPERT_X_SLOTS]).astype(MXU_DTYPE)
            gate = _dot(x, wg16_ref[...])
            up = _dot(x, wu16_ref[...])
            ybuf_ref[g % EXPERT_Y_SLOTS] = _pack_rows(_dot((_silu(gate) * up).astype(MXU_DTYPE), wd16_ref[...]))
            y_copy(g).start()
            return carry

        lax.fori_loop(0, n, body, 0)

    @pl.when(expert == last)
    def _():
        for j in range(EXPERT_Y_SLOTS):
            @pl.when(total > j)
            def _():
                y_copy(total - 1 - j).wait()


def _experts(seg_start, seg_blocks, xb, w_gate, w_up, w_down):
    n_slots, row_words = xb.shape
    n_exp, d, de = w_gate.shape
    grid_spec = pltpu.PrefetchScalarGridSpec(
        num_scalar_prefetch=2,
        grid=(n_exp,),
        in_specs=[
            pl.BlockSpec(memory_space=pl.ANY),
            pl.BlockSpec((None, d, de), lambda e, st, nb: (e, 0, 0)),
            pl.BlockSpec((None, d, de), lambda e, st, nb: (e, 0, 0)),
            pl.BlockSpec((None, de, d), lambda e, st, nb: (e, 0, 0)),
        ],
        out_specs=pl.BlockSpec(memory_space=pl.ANY),
        scratch_shapes=[
            pltpu.VMEM((d, de), MXU_DTYPE), pltpu.VMEM((d, de), MXU_DTYPE), pltpu.VMEM((de, d), MXU_DTYPE),
            pltpu.VMEM((EXPERT_X_SLOTS, EXPERT_BLOCK, row_words), jnp.uint32),
            pltpu.VMEM((EXPERT_Y_SLOTS, EXPERT_BLOCK, row_words), jnp.uint32),
            pltpu.SemaphoreType.DMA((EXPERT_X_SLOTS,)), pltpu.SemaphoreType.DMA((EXPERT_Y_SLOTS,)),
        ],
    )
    return pl.pallas_call(
        _expert_kernel,
        grid_spec=grid_spec,
        out_shape=jax.ShapeDtypeStruct((n_slots, row_words), jnp.uint32),
        input_output_aliases={2: 0},
        compiler_params=_params(("arbitrary",)),
        name="experts",
    )(seg_start, seg_blocks, xb, w_gate, w_up, w_down)


def _combine_kernel(dest_ref, yb_ref, h1_ref, rw_ref, fg_ref, o_ref, buf_ref, res_ref, sem, out_sem,
                    *, tc, groups_per_batch):
    step = pl.program_id(0)
    last = pl.num_programs(0) - 1
    n_grp = tc // N_META

    def output_copies(at_step, action):
        slot = at_step % 2
        first = at_step * n_grp
        first_in_batch = first % groups_per_batch
        has_meta = (first_in_batch == 0) | (first_in_batch + n_grp > groups_per_batch)

        def copy_groups(q, n):
            grp = first + q
            dst = pl.multiple_of((grp - grp // groups_per_batch - 1) * N_META, N_META)
            copy = pltpu.make_async_copy(res_ref.at[slot, pl.ds(q * N_META, n * N_META)],
                                         o_ref.at[pl.ds(dst, n * N_META)], out_sem.at[slot])
            copy.start() if action == "start" else copy.wait()

        @pl.when(jnp.logical_not(has_meta))
        def _():
            copy_groups(0, n_grp)

        @pl.when(has_meta)
        def _():
            for q in range(n_grp):
                @pl.when((first + q) % groups_per_batch != 0)
                def _():
                    copy_groups(q, 1)

    def issue(t, carry):
        for k in range(TOP_K):
            pltpu.make_async_copy(yb_ref.at[pl.ds(dest_ref[0, TOP_K * t + k], 1)],
                                  buf_ref.at[k, pl.ds(t, 1)], sem).start(priority=k % 2)
        return carry

    lax.fori_loop(0, tc, issue, 0, unroll=DMA_UNROLL)
    for k in range(TOP_K):
        pltpu.make_async_copy(yb_ref.at[pl.ds(0, tc)], buf_ref.at[k], sem).wait()

    rw = rw_ref[...]
    h2 = h1_ref[...] + (_unpack_rows(buf_ref[0]) * rw[:, 0:1] + _unpack_rows(buf_ref[1]) * rw[:, 1:2])
    ms = jnp.mean(h2 * h2, axis=-1, keepdims=True)

    @pl.when(step >= 2)
    def _():
        output_copies(step - 2, "wait")

    res_ref[step % 2] = h2 * lax.rsqrt(ms + NORM_EPS) * fg_ref[...]
    output_copies(step, "start")

    @pl.when(step == last)
    def _():
        @pl.when(last >= 1)
        def _():
            output_copies(step - 1, "wait")

        output_copies(step, "wait")


def _combine(dest_flat, yb, h1, rw, final_g, tc, seq):
    t, d = h1.shape
    assert tc % N_META == 0 and seq % N_META == 0
    n_out = t - (t // seq) * N_META
    return pl.pallas_call(
        functools.partial(_combine_kernel, tc=tc, groups_per_batch=seq // N_META),
        grid=(t // tc,),
        in_specs=[pl.BlockSpec((None, 1, TOP_K * tc), lambda i: (i, 0, 0), memory_space=pltpu.SMEM),
                  pl.BlockSpec(memory_space=pl.ANY),
                  pl.BlockSpec((tc, d), lambda i: (i, 0)),
                  pl.BlockSpec((tc, LANES), lambda i: (i, 0)),
                  pl.BlockSpec((1, d), lambda i: (0, 0))],
        out_specs=pl.BlockSpec(memory_space=pl.ANY),
        out_shape=jax.ShapeDtypeStruct((n_out, d), F32),
        scratch_shapes=[pltpu.VMEM((TOP_K, tc, d // 2), jnp.uint32), pltpu.VMEM((2, tc, d), F32),
                        pltpu.SemaphoreType.DMA(()), pltpu.SemaphoreType.DMA((2,))],
        compiler_params=_params(("arbitrary",)),
        name="combine",
    )(dest_flat, yb, h1, rw, final_g)


def _tile(total, candidates):
    for c in candidates:
        if total % c == 0:
            return c
    raise ValueError(f"no tile for {total}")


def _dft_tables(seq):
    kp = -(-seq // LANES) * LANES
    nat = -(-(seq // 2 + 1) // 16) * 16
    mir = seq - nat
    c = np.arange(FGROUP_DIM, dtype=np.int64)
    ang_c = (2.0 * np.pi / FGROUP_DIM) * ((c[:, None] * c[None, :]) % FGROUP_DIM)
    cs128 = np.concatenate([np.cos(ang_c), np.sin(ang_c)], axis=1)
    k = np.arange(nat, dtype=np.int64)
    n = np.arange(seq, dtype=np.int64)
    ang_l = (2.0 * np.pi / seq) * ((k[:, None] * n[None, :]) % seq)
    twid = np.zeros((nat, 2 * kp), np.float32)
    twid[:, :seq] = np.cos(ang_l)
    twid[:, kp:kp + seq] = np.sin(ang_l)
    flip = np.zeros((mir, nat), np.float32)
    flip[np.arange(mir), mir - np.arange(mir)] = 1.0
    as_const = lambda t: jnp.asarray(t.astype(MXU_DTYPE))
    return as_const(cs128), as_const(twid), as_const(flip)


def kernel(x, meta_tokens, norm1_g, w_in, conv_w, a_log_fwd, dt_bias_fwd, a_log_bwd, dt_bias_bwd, out_norm_g, w_fourier, w_delta, w_out, norm2_g, w_router_group, b_router_group, w_router_expert, b_router_expert, w_gate_e, w_up_e, w_down_e, final_norm_g):
    bsz, n_real, d = x.shape
    seq = N_META + n_real
    n_tok = bsz * seq
    assert (CHUNK - N_META) % CHUNK + seq == -(-seq // CHUNK) * CHUNK, "meta tokens must fill the tail of chunk 0"
    assert w_in.shape[0] == 1, "single trunk layer"
    layer = 0

    meta = jnp.broadcast_to(meta_tokens[None].astype(x.dtype), (bsz, N_META, d))
    h = jnp.concatenate([meta, x], axis=1).reshape(n_tok, d)

    w = w_in[layer]
    o_f, o_qkv, o_z, o_ab, o_gf, o_gd = 0, F_WIDTH, F_WIDTH + 3 * QK_WIDTH, F_WIDTH + 4 * QK_WIDTH, \
        F_WIDTH + 4 * QK_WIDTH + 4 * N_HEADS, F_WIDTH + 4 * QK_WIDTH + 4 * N_HEADS + d
    w_cat = jnp.concatenate(
        [w[:, o_qkv:o_z], w[:, o_z:o_ab], w[:, o_gf:o_gd], w[:, o_gd:o_gd + d], w[:, o_f:o_qkv]],
        axis=1).astype(MXU_DTYPE)
    w_ab = jnp.pad(w[:, o_ab:o_gf], ((0, 0), (0, LANES - 4 * N_HEADS))).astype(MXU_DTYPE)

    proj, ab = _in_proj(h, norm1_g[layer][None, :], w_cat, w_ab, _tile(n_tok, (1032, 104, 8)))
    proj3 = proj.reshape(bsz, seq, PROJ_WIDTH)

    ymix = _fnet(proj3, *_dft_tables(seq))

    gate_prm = jnp.zeros((SUBLANES, LANES), F32)
    gate_prm = gate_prm.at[0, 0:N_HEADS].set(a_log_fwd[layer]).at[0, 2 * N_HEADS:3 * N_HEADS].set(a_log_bwd[layer])
    gate_prm = gate_prm.at[1, 0:N_HEADS].set(dt_bias_fwd[layer]).at[1, 2 * N_HEADS:3 * N_HEADS].set(dt_bias_bwd[layer])
    og = _gdn(proj3, ab.reshape(bsz, seq, LANES), conv_w[layer], gate_prm, out_norm_g[layer][None, :])

    w_router = jnp.zeros((d, LANES), F32)
    w_router = w_router.at[:, 0:N_GROUPS].set(w_router_group[layer])
    w_router = w_router.at[:, N_GROUPS:N_GROUPS + N_EXPERTS].set(w_router_expert[layer])
    b_router = jnp.zeros((1, LANES), F32)
    b_router = b_router.at[0, 0:N_GROUPS].set(b_router_group[layer])
    b_router = b_router.at[0, N_GROUPS:N_GROUPS + N_EXPERTS].set(b_router_expert[layer])
    tt = _tile(n_tok, (384, 128))
    h1, hn, ridx, rw, cnt = _merge(
        h, og.reshape(n_tok, d), ymix.reshape(n_tok, F_WIDTH), proj,
        w_fourier[layer].astype(MXU_DTYPE), w_delta[layer].astype(MXU_DTYPE), w_out[layer].astype(MXU_DTYPE),
        norm2_g[layer][None, :], w_router, b_router, _tile(n_tok, (688, 384, 128)))

    n_assign = n_tok * TOP_K
    n_blocks = -(-n_assign // EXPERT_BLOCK) + N_EXPERTS
    dest, seg = _plan(ridx, cnt, tt)
    dest_flat = dest[:, :TOP_K].reshape(n_tok // tt, 1, TOP_K * tt)
    seg_start = seg[0, :N_EXPERTS]
    seg_blocks = seg[SUBLANES, :N_EXPERTS]

    xb = _scatter(dest_flat, hn, jnp.zeros((n_blocks * EXPERT_BLOCK, d // 2), jnp.uint32), tt)
    yb = _experts(seg_start, seg_blocks, xb, w_gate_e[layer], w_up_e[layer], w_down_e[layer])
    out = _combine(dest_flat, yb, h1, rw, final_norm_g[None, :], tt, seq)
    return out.reshape(bsz, n_real, d)
```

```python
import functools
import math

import jax
import jax.numpy as jnp
import numpy as np
from jax import lax
from jax.experimental import pallas as pl
from jax.experimental.pallas import tpu as pltpu

F32 = jnp.float32
MXU_DTYPE = jnp.bfloat16

N_META = 16
CHUNK = 64
N_FGROUPS = 4
FGROUP_DIM = 128
F_WIDTH = N_FGROUPS * FGROUP_DIM
N_HEADS = 8
HEAD_DIM = 128
QK_WIDTH = N_HEADS * HEAD_DIM
CONV_K = 5
N_GROUPS = 8
EXPERTS_PER_GROUP = 8
N_EXPERTS = N_GROUPS * EXPERTS_PER_GROUP
TOP_K = 2
EXPERT_BLOCK = 256
NORM_EPS = 1e-6
LANES = 128
SUBLANES = 8
VMEM_LIMIT = 52 * 1024 * 1024
GDN_GROUP = 11
DMA_UNROLL = 8
EXPERT_X_SLOTS = 4
EXPERT_Y_SLOTS = 3

COL_QKV = 0
COL_Z = 3 * QK_WIDTH
COL_GATE_F = COL_Z + QK_WIDTH
COL_GATE_D = COL_GATE_F + 1024
COL_FIN = COL_GATE_D + 1024
PROJ_WIDTH = COL_FIN + F_WIDTH
PROJ_TN = 3328
PROJ_DTYPE = jnp.bfloat16


def _dot(a, b):
    return jnp.dot(a, b, preferred_element_type=F32)


def _dot_nt(a, b):
    return lax.dot_general(a, b, (((1,), (1,)), ((), ())), preferred_element_type=F32)


def _dot_tn(a, b):
    return lax.dot_general(a, b, (((0,), (0,)), ((), ())), preferred_element_type=F32)


def _split3(x):
    hi = x.astype(MXU_DTYPE)
    r1 = x - hi.astype(F32)
    mid = r1.astype(MXU_DTYPE)
    lo = (r1 - mid.astype(F32)).astype(MXU_DTYPE)
    return hi, mid, lo


def _exact_left(sel, x):
    hi, mid, lo = _split3(x)
    return _dot(sel, hi) + _dot(sel, mid) + _dot(sel, lo)


def _exact_right(x, sel):
    hi, mid, lo = _split3(x)
    return _dot(hi, sel) + _dot(mid, sel) + _dot(lo, sel)


def _pack_rows(x):
    half = x.shape[1] // 2
    lo = lax.bitcast_convert_type(x[:, :half].astype(jnp.bfloat16).astype(F32), jnp.uint32)
    hi = lax.bitcast_convert_type(x[:, half:].astype(jnp.bfloat16).astype(F32), jnp.uint32)
    return (lo >> 16) | hi


def _unpack_rows(u):
    lo = lax.bitcast_convert_type(u << 16, F32)
    hi = lax.bitcast_convert_type(u & jnp.uint32(0xFFFF0000), F32)
    return jnp.concatenate([lo, hi], axis=1)


def _silu(x):
    return x * jax.nn.sigmoid(x)


def _params(sem):
    return pltpu.CompilerParams(dimension_semantics=sem, vmem_limit_bytes=VMEM_LIMIT)


def _inproj_kernel(h_ref, g_ref, w_ref, wab_ref, o_ref, ab_ref, hn_ref):
    @pl.when(pl.program_id(1) == 0)
    def _():
        x = h_ref[...]
        ms = jnp.mean(x * x, axis=-1, keepdims=True)
        hn_ref[...] = (x * lax.rsqrt(ms + NORM_EPS) * g_ref[...]).astype(MXU_DTYPE)
        ab_ref[...] = _dot(hn_ref[...], wab_ref[...])

    o_ref[...] = _dot(hn_ref[...], w_ref[...]).astype(o_ref.dtype)


def _in_proj(h2d, gain, w_cat, w_ab, tm):
    t, d = h2d.shape
    n = w_cat.shape[1]
    return pl.pallas_call(
        _inproj_kernel,
        grid=(t // tm, n // PROJ_TN),
        in_specs=[
            pl.BlockSpec((tm, d), lambda i, j: (i, 0)),
            pl.BlockSpec((1, d), lambda i, j: (0, 0)),
            pl.BlockSpec((d, PROJ_TN), lambda i, j: (0, j)),
            pl.BlockSpec((d, LANES), lambda i, j: (0, 0)),
        ],
        out_specs=[pl.BlockSpec((tm, PROJ_TN), lambda i, j: (i, j)),
                   pl.BlockSpec((tm, LANES), lambda i, j: (i, 0))],
        out_shape=[jax.ShapeDtypeStruct((t, n), PROJ_DTYPE),
                   jax.ShapeDtypeStruct((t, LANES), F32)],
        scratch_shapes=[pltpu.VMEM((tm, d), MXU_DTYPE)],
        compiler_params=_params(("arbitrary", "arbitrary")),
        name="in_proj",
    )(h2d, gain, w_cat, w_ab)


def _fnet_kernel(x_ref, cs_ref, tw_ref, flip_ref, o_ref, r_ref, *, seq, kp, nat, scale):
    tail = kp - seq
    if tail:
        r_ref[seq:kp, :] = jnp.zeros((tail, F_WIDTH), MXU_DTYPE)
        r_ref[kp + seq:, :] = jnp.zeros((tail, F_WIDTH), MXU_DTYPE)
    cs = cs_ref[...]
    for g in range(N_FGROUPS):
        cols = slice(g * FGROUP_DIM, (g + 1) * FGROUP_DIM)
        p = _dot(x_ref[:, cols].astype(MXU_DTYPE), cs)
        r_ref[0:seq, cols] = p[:, :FGROUP_DIM].astype(MXU_DTYPE)
        r_ref[kp:kp + seq, cols] = p[:, FGROUP_DIM:].astype(MXU_DTYPE)

    a = _dot(tw_ref[:, :kp], r_ref[0:kp, :])
    b = _dot(tw_ref[:, kp:], r_ref[kp:, :])
    o_ref[0:nat, :] = ((a - b) * scale).astype(o_ref.dtype)
    mirrored = ((a + b) * scale).astype(MXU_DTYPE)
    o_ref[nat:, :] = _dot(flip_ref[...], mirrored).astype(o_ref.dtype)


def _fnet(proj3, cs128, twid, flip):
    b, seq, _ = proj3.shape
    nat, kp = twid.shape[0], twid.shape[1] // 2
    scale = 1.0 / math.sqrt(seq * FGROUP_DIM)
    const = lambda shape: pl.BlockSpec(shape, lambda bi: (0, 0))
    return pl.pallas_call(
        functools.partial(_fnet_kernel, seq=seq, kp=kp, nat=nat, scale=scale),
        grid=(b,),
        in_specs=[
            pl.BlockSpec((None, seq, F_WIDTH), lambda bi: (bi, 0, COL_FIN // F_WIDTH)),
            const((FGROUP_DIM, 2 * FGROUP_DIM)), const((nat, 2 * kp)), const((seq - nat, nat)),
        ],
        out_specs=pl.BlockSpec((None, seq, F_WIDTH), lambda bi: (bi, 0, 0)),
        out_shape=jax.ShapeDtypeStruct((b, seq, F_WIDTH), MXU_DTYPE),
        scratch_shapes=[pltpu.VMEM((2 * kp, F_WIDTH), MXU_DTYPE)],
        compiler_params=_params(("arbitrary",)),
        name="fnet",
    )(proj3, cs128, twid, flip)


def _aligned(x, m):
    return x if isinstance(x, int) else pl.multiple_of(x, m)


def _gdn_kernel(q_ref, k_ref, v_ref, z_ref, ab_ref, cwq_ref, cwk_ref, cwv_ref, prm_ref, ong_ref,
                o_ref,
                rawq_ref, rawk_ref, rawv_ref, qs_ref, ks_ref, vs_ref, gsel_ref, grow_ref, gmix_ref,
                qe_ref, mp_ref, nn_ref, dd_ref, osum_ref,
                *, seq, lp, group, n_heads_total):
    step = pl.program_id(0)
    head = jnp.minimum(step, n_heads_total - 1) % N_HEADS
    nc = lp // CHUNK
    padr = lp - seq
    halo = SUBLANES
    row64 = lax.broadcasted_iota(jnp.int32, (CHUNK, LANES), 0)
    lane64 = lax.broadcasted_iota(jnp.int32, (CHUNK, LANES), 1)

    def for_groups(fn, carry=0):
        n_full, rem = divmod(nc, group)
        if n_full:
            carry = lax.fori_loop(0, n_full, lambda i, c: fn([i * group + j for j in range(group)], c), carry)
        if rem:
            carry = fn([n_full * group + j for j in range(rem)], carry)
        return carry

    cur_half = step % 2
    qe_w, mp_w, nn_w, dd_w, osum_w = (r.at[cur_half] for r in (qe_ref, mp_ref, nn_ref, dd_ref, osum_ref))
    qe_r, mp_r, nn_r, dd_r, osum_r = (r.at[1 - cur_half] for r in (qe_ref, mp_ref, nn_ref, dd_ref, osum_ref))

    streams = ((rawq_ref, q_ref, cwq_ref, qs_ref, HEAD_DIM ** -0.5),
               (rawk_ref, k_ref, cwk_ref, ks_ref, 1.0),
               (rawv_ref, v_ref, cwv_ref, vs_ref, None))
    def stage_conv_inputs():
        for raw_ref, src_ref, _, _, _ in streams:
            raw_ref[0:padr + halo, :] = jnp.zeros((padr + halo, LANES), F32)
            raw_ref[padr + halo:padr + halo + seq, :] = src_ref[...].astype(F32)
            raw_ref[lp + halo:lp + 2 * halo, :] = jnp.zeros((halo, LANES), F32)

    def conv_chunk(c):
        base = pl.multiple_of(c * CHUNK, CHUNK)
        for raw_ref, _, cw_ref, dst_ref, l2_scale in streams:
            cw = cw_ref[...]
            acc = jnp.zeros((CHUNK, LANES), F32)
            for j in range(CONV_K):
                off = halo - (CONV_K - 1) // 2 + j
                acc = acc + cw[j:j + 1, :] * raw_ref[pl.ds(base + off, CHUNK), :]
            y = _silu(acc)
            y = jnp.where(row64 + base >= padr, y, 0.0)
            if l2_scale is not None:
                y = y * lax.rsqrt(jnp.sum(y * y, axis=-1, keepdims=True) + NORM_EPS) * l2_scale
            dst_ref[pl.ds(base, CHUNK), :] = y

    raw_ref = rawq_ref

    def stage_gate_inputs():
        raw_ref[0:padr, :] = jnp.zeros((padr, LANES), F32)
        raw_ref[padr:padr + seq, :] = ab_ref[...]

    neg_a = -jnp.exp(prm_ref[0:1, :])
    dt_bias = prm_ref[1:2, :]
    is_g = (lane64 < 8) | ((lane64 >= 16) & (lane64 < 24))
    is_beta = ((lane64 >= 8) & (lane64 < 16)) | ((lane64 >= 24) & (lane64 < 32))
    r128 = lax.broadcasted_iota(jnp.int32, (2 * CHUNK, CHUNK), 0)
    c128 = lax.broadcasted_iota(jnp.int32, (2 * CHUNK, CHUNK), 1)
    tri2 = jnp.where(((r128 < CHUNK) & (r128 >= c128)) | ((r128 >= CHUNK) & (r128 - CHUNK <= c128)),
                     1.0, 0.0).astype(MXU_DTYPE)
    sel_r = lax.broadcasted_iota(jnp.int32, (LANES, 4 * LANES), 0)
    sel_c = lax.broadcasted_iota(jnp.int32, (LANES, 4 * LANES), 1)
    n_gate_cols = 4 * N_HEADS
    sel_packed = jnp.where((sel_r < 3 * n_gate_cols)
                           & (sel_r % n_gate_cols == head + N_HEADS * (sel_c // LANES)), 1.0, 0.0).astype(MXU_DTYPE)
    tri2x3 = jnp.concatenate([tri2, tri2, tri2], axis=1)

    def gate_batch_group(cs, carry):
        bases = [_aligned(c * CHUNK, CHUNK) for c in cs]
        gates, parts = [], []
        for base in bases:
            ab = raw_ref[pl.ds(base, CHUNK), :]
            g = neg_a * jax.nn.softplus(ab + dt_bias)
            beta = jax.nn.sigmoid(ab)
            gt = jnp.where(is_g, g, jnp.where(is_beta, beta, 0.0))
            gt = jnp.where(row64 + base >= padr, gt, 0.0)
            gates.append(gt)
            parts.append(jnp.concatenate(_split3(gt), axis=0))
        cums = [_dot(tri2x3, p) for p in parts]
        for c, base, cm, gt in zip(cs, bases, cums, gates):
            m = jnp.where(lane64 < 8, cm[:CHUNK], jnp.where((lane64 >= 16) & (lane64 < 24), cm[CHUNK:], gt))
            gmix_ref[pl.ds(base, CHUNK), :] = m
            grow_ref[pl.ds(_aligned(c * LANES, LANES), LANES), :] = jnp.concatenate([m, m], axis=0).T
        return carry

    def gate_head_group(cs, carry):
        rows = len(cs) * CHUNK
        base = _aligned(cs[0] * CHUNK, CHUNK)
        hi, mid, lo = _split3(gmix_ref[pl.ds(base, rows), :])
        packed = (hi.astype(F32) + pltpu.roll(mid.astype(F32), n_gate_cols, 1)
                  + pltpu.roll(lo.astype(F32), 2 * n_gate_cols, 1)).astype(MXU_DTYPE)
        gsel_ref[pl.ds(base, rows), :] = _dot(packed, sel_packed)
        return carry

    fwd_half = lane64 < CHUNK
    col64 = lane64 & (CHUNK - 1)
    incl2 = (fwd_half & (row64 >= col64)) | (~fwd_half & (row64 <= col64))
    strict2 = (fwd_half & (row64 > col64)) | (~fwd_half & (row64 < col64))
    eye2 = jnp.where(row64 == col64, 1.0, 0.0).astype(F32)

    def blockdiag(p):
        return jnp.concatenate([jnp.where(fwd_half, p, 0.0), jnp.where(fwd_half, 0.0, p)],
                               axis=0).astype(MXU_DTYPE)

    n_neumann = int(math.log2(CHUNK)) - 2
    n_stages = n_neumann + 6

    def chunk_group(cs, carry, with_scan):
        n = len(cs)
        bases = [_aligned(c * CHUNK, CHUNK) for c in cs]
        cb128 = [_aligned(c * LANES, LANES) for c in cs]
        cb8 = [_aligned(c * SUBLANES, SUBLANES) for c in cs]
        stage = [0]

        def end_of_stage(carry):
            s = stage[0]
            stage[0] += 1
            if with_scan:
                for j in range(n):
                    if j * n_stages // n == s:
                        carry = scan_step(cs[j], carry)
            return carry

        def gate_cols(j):
            gs = gsel_ref[pl.ds(bases[j], CHUNK), :]
            return gs[:, 0:LANES], gs[:, LANES:2 * LANES], gs[:, 2 * LANES:3 * LANES], gs[:, 3 * LANES:]

        kq = []
        for j in range(n):
            k16 = ks_ref[pl.ds(bases[j], CHUNK), :].astype(MXU_DTYPE)
            q16 = qs_ref[pl.ds(bases[j], CHUNK), :].astype(MXU_DTYPE)
            kq.append(_dot_nt(jnp.concatenate([k16, q16], axis=0), jnp.concatenate([k16, k16], axis=0)))
        carry = end_of_stage(carry)

        ts, ps, qkds = [], [], []
        for j in range(n):
            gf, bf, gb, bb = gate_cols(j)
            row_f = grow_ref[pl.ds(cb128[j] + head, 1), :]
            row_b = grow_ref[pl.ds(cb128[j] + 2 * N_HEADS + head, 1), :]
            g_col = jnp.where(fwd_half, gf, gb)
            g_row = jnp.where(fwd_half, row_f, row_b)
            decay = jnp.where(incl2, jnp.exp(jnp.where(incl2, g_col - g_row, 0.0)), 0.0)
            a2 = jnp.where(strict2, kq[j][:CHUNK] * jnp.where(fwd_half, bf, bb) * decay, 0.0)
            qkds.append(jnp.where(incl2, kq[j][CHUNK:] * decay, 0.0))
            ts.append(eye2 - a2)
            ps.append(a2)

        ps = [_dot(p.astype(MXU_DTYPE), blockdiag(p)) for p in ps]
        carry = end_of_stage(carry)
        for _ in range(n_neumann):
            tps = [_dot(jnp.concatenate([t, p], axis=0).astype(MXU_DTYPE), blockdiag(p)) for t, p in zip(ts, ps)]
            carry = end_of_stage(carry)
            ts = [t + tp[:CHUNK] for t, tp in zip(ts, tps)]
            ps = [tp[CHUNK:] for tp in tps]
        ts = [t + _dot(t.astype(MXU_DTYPE), blockdiag(p)) for t, p in zip(ts, ps)]
        carry = end_of_stage(carry)

        uws = []
        for j in range(n):
            gf, bf, gb, bb = gate_cols(j)
            k = ks_ref[pl.ds(bases[j], CHUNK), :]
            v = vs_ref[pl.ds(bases[j], CHUNK), :]
            x_f = jnp.concatenate([v * bf, k * bf * jnp.exp(gf)], axis=1)
            x_b = jnp.concatenate([v * bb, k * bb * jnp.exp(gb)], axis=1)
            uw = _dot(blockdiag(ts[j]), jnp.concatenate([x_f, x_b], axis=0).astype(MXU_DTYPE))
            uws.append(uw.astype(MXU_DTYPE))
        carry = end_of_stage(carry)

        xxs = [_dot(blockdiag(qkds[j]), uws[j]) for j in range(n)]
        carry = end_of_stage(carry)
        for j in range(n):
            gf, _, gb, _ = gate_cols(j)
            k = ks_ref[pl.ds(bases[j], CHUNK), :]
            q = qs_ref[pl.ds(bases[j], CHUNK), :]
            xx = xxs[j]
            osum_w[pl.ds(bases[j], CHUNK), :] = xx[:CHUNK, :LANES] + xx[CHUNK:, :LANES]
            for d, gc in enumerate((gf, gb)):
                rows = slice(d * CHUNK, (d + 1) * CHUNK)
                g_last = gc[CHUNK - 1:CHUNK, :] if d == 0 else gc[0:1, :]
                k_dec = k * jnp.exp(g_last - gc)
                mn = _dot_tn(k_dec.astype(MXU_DTYPE), uws[j][rows])
                qe_w[d, pl.ds(bases[j], CHUNK), :] = (q * jnp.exp(gc) - xx[rows, LANES:]).astype(MXU_DTYPE)
                nn_w[d, pl.ds(cb128[j], HEAD_DIM), :] = mn[:, :LANES]
                mp_w[d, pl.ds(cb128[j], HEAD_DIM), :] = mn[:, LANES:].astype(MXU_DTYPE)
                dd_w[d, pl.ds(cb8[j], SUBLANES), :] = jnp.broadcast_to(jnp.exp(g_last), (SUBLANES, LANES))
        carry = end_of_stage(carry)
        assert stage[0] == n_stages
        return carry

    def scan_step(i, carry):
        new = []
        for d in range(2):
            s = carry[d]
            c = i if d == 0 else nc - 1 - i
            base = _aligned(c * CHUNK, CHUNK)
            cbase = _aligned(c * HEAD_DIM, HEAD_DIM)
            dbase = _aligned(c * SUBLANES, SUBLANES)
            s16 = s.astype(MXU_DTYPE)
            osum_r[pl.ds(base, CHUNK), :] += _dot(qe_r[d, pl.ds(base, CHUNK), :], s16)
            dec = dd_r[d, pl.ds(dbase, 1), :]
            new.append(dec * s - _dot(mp_r[d, pl.ds(cbase, HEAD_DIM), :], s16)
                       + nn_r[d, pl.ds(cbase, HEAD_DIM), :])
        return tuple(new)

    zero_state = jnp.zeros((HEAD_DIM, HEAD_DIM), F32)

    def write_output():
        gain = ong_ref[...]

        def finish(o, z):
            ms = jnp.mean(o * o, axis=-1, keepdims=True)
            return (o * lax.rsqrt(ms + NORM_EPS) * gain * _silu(z.astype(F32))).astype(o_ref.dtype)

        first = CHUNK - padr if padr else 0
        if first:
            o_ref[0:first, :] = finish(osum_r[padr:CHUNK, :], z_ref[0:first, :])

        def out_body(c, carry):
            base = pl.multiple_of(c * CHUNK, CHUNK)
            dst = pl.multiple_of(c * CHUNK - padr, SUBLANES)
            o_ref[pl.ds(dst, CHUNK), :] = finish(osum_r[pl.ds(base, CHUNK), :], z_ref[pl.ds(dst, CHUNK), :])
            return carry

        lax.fori_loop(1 if first else 0, nc, out_body, 0, unroll=4)

    def conv_only(c, carry):
        conv_chunk(c)
        return carry

    @pl.when(step < n_heads_total)
    def _():
        stage_conv_inputs()
        lax.fori_loop(0, nc, conv_only, 0, unroll=3)

    @pl.when((step < n_heads_total) & (head == 0))
    def _():
        stage_gate_inputs()
        for_groups(gate_batch_group)

    @pl.when(step < n_heads_total)
    def _():
        for_groups(gate_head_group)

    @pl.when(step == 0)
    def _():
        for_groups(functools.partial(chunk_group, with_scan=False))

    @pl.when((step > 0) & (step < n_heads_total))
    def _():
        for_groups(functools.partial(chunk_group, with_scan=True), (zero_state, zero_state))
        write_output()

    @pl.when(step == n_heads_total)
    def _():
        lax.fori_loop(0, nc, scan_step, (zero_state, zero_state))
        write_output()


def _gdn(proj3, ab3, conv_w, gate_prm, out_norm_g):
    b, seq, _ = proj3.shape
    lp = -(-seq // CHUNK) * CHUNK
    nc = lp // CHUNK
    n_total = b * N_HEADS
    cur = lambda i: jnp.minimum(i, n_total - 1)
    prev = lambda i: jnp.maximum(i - 1, 0)
    col = lambda off: pl.BlockSpec((None, seq, LANES), lambda i: (cur(i) // N_HEADS, 0, off + cur(i) % N_HEADS))
    cw = lambda off: pl.BlockSpec((CONV_K, LANES), lambda i: (0, off + cur(i) % N_HEADS))
    return pl.pallas_call(
        functools.partial(_gdn_kernel, seq=seq, lp=lp, group=GDN_GROUP, n_heads_total=n_total),
        grid=(n_total + 1,),
        in_specs=[
            col(COL_QKV // LANES), col(COL_QKV // LANES + N_HEADS), col(COL_QKV // LANES + 2 * N_HEADS),
            pl.BlockSpec((None, seq, LANES),
                         lambda i: (prev(i) // N_HEADS, 0, COL_Z // LANES + prev(i) % N_HEADS)),
            pl.BlockSpec((None, seq, LANES), lambda i: (cur(i) // N_HEADS, 0, 0)),
            cw(0), cw(N_HEADS), cw(2 * N_HEADS),
            pl.BlockSpec((SUBLANES, LANES), lambda i: (0, 0)),
            pl.BlockSpec((1, LANES), lambda i: (0, 0)),
        ],
        out_specs=pl.BlockSpec((None, seq, LANES), lambda i: (prev(i) // N_HEADS, 0, prev(i) % N_HEADS)),
        out_shape=jax.ShapeDtypeStruct((b, seq, N_HEADS * HEAD_DIM), MXU_DTYPE),
        scratch_shapes=[
            pltpu.VMEM((lp + 2 * SUBLANES, LANES), F32),
            pltpu.VMEM((lp + 2 * SUBLANES, LANES), F32),
            pltpu.VMEM((lp + 2 * SUBLANES, LANES), F32),
            pltpu.VMEM((lp, LANES), F32),
            pltpu.VMEM((lp, LANES), F32),
            pltpu.VMEM((lp, LANES), F32),
            pltpu.VMEM((lp, 4 * LANES), F32),
            pltpu.VMEM((nc * LANES, LANES), F32),
            pltpu.VMEM((lp, LANES), F32),
            pltpu.VMEM((2, 2, lp, LANES), MXU_DTYPE),
            pltpu.VMEM((2, 2, nc * HEAD_DIM, LANES), MXU_DTYPE),
            pltpu.VMEM((2, 2, nc * HEAD_DIM, LANES), F32),
            pltpu.VMEM((2, 2, nc * SUBLANES, LANES), F32),
            pltpu.VMEM((2, lp, LANES), F32),
        ],
        compiler_params=_params(("arbitrary",)),
        name="gdn",
    )(proj3, proj3, proj3, proj3, ab3, conv_w, conv_w, conv_w, gate_prm, out_norm_g)


def _merge_kernel(h_ref, og_ref, ym_ref, gf_ref, gd_ref, wf_ref, wd_ref, wo_ref, n2_ref, wr_ref, br_ref,
                  h1_ref, hn_ref, ridx_ref, rw_ref, cnt_ref):
    @pl.when(pl.program_id(0) == 0)
    def _():
        cnt_ref[...] = jnp.zeros_like(cnt_ref)

    y_f = _dot(ym_ref[...], wf_ref[...])
    y_d = _dot(og_ref[...], wd_ref[...])
    merged = (jax.nn.sigmoid(gf_ref[...].astype(F32)) * y_f
              + jax.nn.sigmoid(gd_ref[...].astype(F32)) * y_d)
    h1 = h_ref[...] + _dot(merged.astype(MXU_DTYPE), wo_ref[...])
    h1_ref[...] = h1
    ms = jnp.mean(h1 * h1, axis=-1, keepdims=True)
    hn = h1 * lax.rsqrt(ms + NORM_EPS) * n2_ref[...]
    hn_ref[...] = _pack_rows(hn)

    x_hi = hn.astype(MXU_DTYPE)
    x_lo = (hn - x_hi.astype(F32)).astype(MXU_DTYPE)
    w = wr_ref[...]
    w_hi = w.astype(MXU_DTYPE)
    w_lo = (w - w_hi.astype(F32)).astype(MXU_DTYPE)
    logits = _dot(x_hi, w_hi) + _dot(x_hi, w_lo) + _dot(x_lo, w_hi) + br_ref[...]

    tm = logits.shape[0]
    lane = lax.broadcasted_iota(jnp.int32, (tm, LANES), 1)
    neg_inf = -jnp.inf
    is_group = lane < N_GROUPS
    gl = jnp.where(is_group, logits, neg_inf)
    ge = jnp.exp(gl - jnp.max(gl, axis=-1, keepdims=True))
    gp = ge / jnp.sum(ge, axis=-1, keepdims=True)
    p_group = jnp.max(gp, axis=-1, keepdims=True)
    g_idx = jnp.min(jnp.where(is_group & (gp == p_group), lane, LANES), axis=-1, keepdims=True)

    in_group = (lane >= N_GROUPS) & (lane < N_GROUPS + N_EXPERTS) & (((lane - N_GROUPS) >> 3) == g_idx)
    el = jnp.where(in_group, logits, neg_inf)
    ee = jnp.exp(el - jnp.max(el, axis=-1, keepdims=True))
    ep = ee / jnp.sum(ee, axis=-1, keepdims=True)
    v1 = jnp.max(jnp.where(in_group, ep, -1.0), axis=-1, keepdims=True)
    i1 = jnp.min(jnp.where(in_group & (ep == v1), lane, LANES), axis=-1, keepdims=True)
    rest = in_group & (lane != i1)
    v2 = jnp.max(jnp.where(rest, ep, -1.0), axis=-1, keepdims=True)
    i2 = jnp.min(jnp.where(rest & (ep == v2), lane, LANES), axis=-1, keepdims=True)
    denom = v1 + v2
    w1 = p_group * (v1 / denom)
    w2 = p_group * (v2 / denom)
    e1 = i1 - N_GROUPS
    e2 = i2 - N_GROUPS
    ridx_ref[...] = jnp.where(lane == 0, e1, jnp.where(lane == 1, e2, 0))
    rw_ref[...] = jnp.where(lane == 0, w1, jnp.where(lane == 1, w2, 0.0))
    onehots = jnp.where((lane == e1) | (lane == e2), 1.0, 0.0)
    cnt_ref[...] += jnp.broadcast_to(jnp.sum(onehots, axis=0, keepdims=True), cnt_ref.shape)


def _merge(h2d, og2d, ymix2d, proj2d, w_fourier, w_delta, w_out, norm2_g, w_router, b_router, tm):
    t, d = h2d.shape
    row = lambda w: pl.BlockSpec((tm, w), lambda i: (i, 0))
    const = lambda shape: pl.BlockSpec(shape, lambda i: (0, 0))
    return pl.pallas_call(
        _merge_kernel,
        grid=(t // tm,),
        in_specs=[
            row(d), row(d), row(F_WIDTH),
            pl.BlockSpec((tm, d), lambda i: (i, COL_GATE_F // d)),
            pl.BlockSpec((tm, d), lambda i: (i, COL_GATE_D // d)),
            const((F_WIDTH, d)), const((d, d)), const((d, d)), const((1, d)),
            const((d, LANES)), const((1, LANES)),
        ],
        out_specs=[row(d), row(d // 2), row(LANES), row(LANES), const((SUBLANES, LANES))],
        out_shape=[
            jax.ShapeDtypeStruct((t, d), F32),
            jax.ShapeDtypeStruct((t, d // 2), jnp.uint32),
            jax.ShapeDtypeStruct((t, LANES), jnp.int32),
            jax.ShapeDtypeStruct((t, LANES), F32),
            jax.ShapeDtypeStruct((SUBLANES, LANES), F32),
        ],
        compiler_params=_params(("arbitrary",)),
        name="merge",
    )(h2d, og2d, ymix2d, proj2d, proj2d, w_fourier, w_delta, w_out, norm2_g, w_router, b_router)


def _plan_kernel(ridx_ref, cnt_ref, dest_ref, seg_ref, base_ref, run_ref):
    tt = ridx_ref.shape[0]

    @pl.when(pl.program_id(0) == 0)
    def _():
        cnt = cnt_ref[...].astype(jnp.int32)
        n_blk = (cnt + (EXPERT_BLOCK - 1)) // EXPERT_BLOCK
        padded = (n_blk * EXPERT_BLOCK).astype(F32)
        r = lax.broadcasted_iota(jnp.int32, (LANES, LANES), 0)
        c = lax.broadcasted_iota(jnp.int32, (LANES, LANES), 1)
        before = jnp.where(r < c, 1.0, 0.0).astype(MXU_DTYPE)
        start = _exact_right(padded, before)
        base_ref[...] = start
        run_ref[...] = jnp.zeros_like(run_ref)
        seg_ref[0:SUBLANES, :] = start.astype(jnp.int32)
        seg_ref[SUBLANES:2 * SUBLANES, :] = n_blk
        seg_ref[2 * SUBLANES:, :] = cnt

    lane = lax.broadcasted_iota(jnp.int32, (tt, LANES), 1)
    ridx = ridx_ref[...]
    oh0 = jnp.where(lane == ridx[:, 0:1], 1.0, 0.0)
    oh1 = jnp.where(lane == ridx[:, 1:2], 1.0, 0.0)
    r = lax.broadcasted_iota(jnp.int32, (tt, tt), 0)
    c = lax.broadcasted_iota(jnp.int32, (tt, tt), 1)
    earlier = jnp.where(c < r, 1.0, 0.0).astype(MXU_DTYPE)
    prefix = _dot(earlier, jnp.concatenate([oh0, oh1], axis=1).astype(MXU_DTYPE))
    c0 = jnp.sum(oh0, axis=0, keepdims=True)
    c1 = jnp.sum(oh1, axis=0, keepdims=True)
    first = base_ref[0:1, :] + run_ref[0:1, :]
    d0 = jnp.sum(oh0 * (prefix[:, :LANES] + first), axis=-1, keepdims=True)
    d1 = jnp.sum(oh1 * (prefix[:, LANES:] + first + c0), axis=-1, keepdims=True)
    run_ref[...] += jnp.broadcast_to(c0 + c1, run_ref.shape)
    dest_ref[...] = jnp.where(lane == 0, d0, jnp.where(lane == 1, d1, 0.0)).astype(jnp.int32)


def _plan(ridx, cnt, tt):
    t = ridx.shape[0]
    return pl.pallas_call(
        _plan_kernel,
        grid=(t // tt,),
        in_specs=[pl.BlockSpec((tt, LANES), lambda i: (i, 0)),
                  pl.BlockSpec((SUBLANES, LANES), lambda i: (0, 0))],
        out_specs=[pl.BlockSpec((tt, LANES), lambda i: (i, 0)),
                   pl.BlockSpec((3 * SUBLANES, LANES), lambda i: (0, 0))],
        out_shape=[jax.ShapeDtypeStruct((t, LANES), jnp.int32),
                   jax.ShapeDtypeStruct((3 * SUBLANES, LANES), jnp.int32)],
        scratch_shapes=[pltpu.VMEM((SUBLANES, LANES), F32), pltpu.VMEM((SUBLANES, LANES), F32)],
        compiler_params=_params(("arbitrary",)),
        name="plan",
    )(ridx, cnt)


def _scatter_kernel(dest_ref, hn_ref, xb_in_ref, xb_ref, sem, *, ts):
    del xb_in_ref

    def issue(t, carry):
        src = hn_ref.at[pl.ds(t, 1)]
        for k in range(TOP_K):
            pltpu.make_async_copy(src, xb_ref.at[pl.ds(dest_ref[0, TOP_K * t + k], 1)], sem).start(priority=k % 2)
        return carry

    lax.fori_loop(0, ts, issue, 0, unroll=DMA_UNROLL)
    for _ in range(TOP_K):
        pltpu.make_async_copy(hn_ref, xb_ref.at[pl.ds(0, ts)], sem).wait()


def _scatter(dest_flat, hn2d, xb_zero, ts):
    t, d = hn2d.shape
    return pl.pallas_call(
        functools.partial(_scatter_kernel, ts=ts),
        grid=(t // ts,),
        in_specs=[pl.BlockSpec((None, 1, TOP_K * ts), lambda i: (i, 0, 0), memory_space=pltpu.SMEM),
                  pl.BlockSpec((ts, d), lambda i: (i, 0)),
                  pl.BlockSpec(memory_space=pl.ANY)],
        out_specs=pl.BlockSpec(memory_space=pl.ANY),
        out_shape=jax.ShapeDtypeStruct(xb_zero.shape, xb_zero.dtype),
        scratch_shapes=[pltpu.SemaphoreType.DMA(())],
        input_output_aliases={2: 0},
        compiler_params=_params(("arbitrary",)),
        name="scatter",
    )(dest_flat, hn2d, xb_zero)


def _expert_kernel(start_ref, nblk_ref, xb_ref, wg_ref, wu_ref, wd_ref, yb_ref,
                   wg16_ref, wu16_ref, wd16_ref, xbuf_ref, ybuf_ref, sem_in, sem_out):
    expert = pl.program_id(0)
    last = pl.num_programs(0) - 1
    n = nblk_ref[expert]
    first = start_ref[expert] // EXPERT_BLOCK
    total = start_ref[last] // EXPERT_BLOCK + nblk_ref[last]

    def rows(g):
        return pl.ds(pl.multiple_of(g * EXPERT_BLOCK, EXPERT_BLOCK), EXPERT_BLOCK)

    def x_copy(g):
        slot = g % EXPERT_X_SLOTS
        return pltpu.make_async_copy(xb_ref.at[rows(g)], xbuf_ref.at[slot], sem_in.at[slot])

    def y_copy(g):
        slot = g % EXPERT_Y_SLOTS
        return pltpu.make_async_copy(ybuf_ref.at[slot], yb_ref.at[rows(g)], sem_out.at[slot])

    @pl.when(n > 0)
    def _():
        @pl.when(first == 0)
        def _():
            for j in range(EXPERT_X_SLOTS - 1):
                @pl.when(j < total)
                def _():
                    x_copy(j).start()

        wg16_ref[...] = wg_ref[...].astype(MXU_DTYPE)
        wu16_ref[...] = wu_ref[...].astype(MXU_DTYPE)
        wd16_ref[...] = wd_ref[...].astype(MXU_DTYPE)

        def body(i, carry):
            g = first + i

            @pl.when(g + (EXPERT_X_SLOTS - 1) < total)
            def _():
                x_copy(g + (EXPERT_X_SLOTS - 1)).start()

            x_copy(g).wait()

            @pl.when(g >= EXPERT_Y_SLOTS)
            def _():
                y_copy(g - EXPERT_Y_SLOTS).wait()

            x = _unpack_rows(xbuf_ref[g % EXPERT_X_SLOTS]).astype(MXU_DTYPE)
            gate = _dot(x, wg16_ref[...])
            up = _dot(x, wu16_ref[...])
            ybuf_ref[g % EXPERT_Y_SLOTS] = _pack_rows(_dot((_silu(gate) * up).astype(MXU_DTYPE), wd16_ref[...]))
            y_copy(g).start()
            return carry

        lax.fori_loop(0, n, body, 0)

    @pl.when(expert == last)
    def _():
        for j in range(EXPERT_Y_SLOTS):
            @pl.when(total > j)
            def _():
                y_copy(total - 1 - j).wait()


def _experts(seg_start, seg_blocks, xb, w_gate, w_up, w_down):
    n_slots, row_words = xb.shape
    n_exp, d, de = w_gate.shape
    grid_spec = pltpu.PrefetchScalarGridSpec(
        num_scalar_prefetch=2,
        grid=(n_exp,),
        in_specs=[
            pl.BlockSpec(memory_space=pl.ANY),
            pl.BlockSpec((None, d, de), lambda e, st, nb: (e, 0, 0)),
            pl.BlockSpec((None, d, de), lambda e, st, nb: (e, 0, 0)),
            pl.BlockSpec((None, de, d), lambda e, st, nb: (e, 0, 0)),
        ],
        out_specs=pl.BlockSpec(memory_space=pl.ANY),
        scratch_shapes=[
            pltpu.VMEM((d, de), MXU_DTYPE), pltpu.VMEM((d, de), MXU_DTYPE), pltpu.VMEM((de, d), MXU_DTYPE),
            pltpu.VMEM((EXPERT_X_SLOTS, EXPERT_BLOCK, row_words), jnp.uint32),
            pltpu.VMEM((EXPERT_Y_SLOTS, EXPERT_BLOCK, row_words), jnp.uint32),
            pltpu.SemaphoreType.DMA((EXPERT_X_SLOTS,)), pltpu.SemaphoreType.DMA((EXPERT_Y_SLOTS,)),
        ],
    )
    return pl.pallas_call(
        _expert_kernel,
        grid_spec=grid_spec,
        out_shape=jax.ShapeDtypeStruct((n_slots, row_words), jnp.uint32),
        input_output_aliases={2: 0},
        compiler_params=_params(("arbitrary",)),
        name="experts",
    )(seg_start, seg_blocks, xb, w_gate, w_up, w_down)


def _combine_kernel(dest_ref, yb_ref, h1_ref, rw_ref, fg_ref, o_ref, buf_ref, res_ref, sem, out_sem,
                    *, tc, groups_per_batch):
    step = pl.program_id(0)
    n_tiles = pl.num_programs(0) - 1
    n_grp = tc // N_META
    cur = step % 2
    prev = 1 - cur

    def output_copies(tile, action):
        slot = tile % 2
        first = tile * n_grp
        first_in_batch = first % groups_per_batch
        has_meta = (first_in_batch == 0) | (first_in_batch + n_grp > groups_per_batch)

        def copy_groups(q, n):
            grp = first + q
            dst = pl.multiple_of((grp - grp // groups_per_batch - 1) * N_META, N_META)
            copy = pltpu.make_async_copy(res_ref.at[slot, pl.ds(q * N_META, n * N_META)],
                                         o_ref.at[pl.ds(dst, n * N_META)], out_sem.at[slot])
            copy.start() if action == "start" else copy.wait()

        @pl.when(jnp.logical_not(has_meta))
        def _():
            copy_groups(0, n_grp)

        @pl.when(has_meta)
        def _():
            for q in range(n_grp):
                @pl.when((first + q) % groups_per_batch != 0)
                def _():
                    copy_groups(q, 1)

    def request_rows(i):
        for u in range(SUBLANES):
            t = i * SUBLANES + u
            for k in range(TOP_K):
                pltpu.make_async_copy(yb_ref.at[pl.ds(dest_ref[0, TOP_K * t + k], 1)],
                                      buf_ref.at[cur, k, pl.ds(t, 1)], sem.at[cur]).start(priority=k % 2)

    def finish_rows(i):
        rows = pl.ds(pl.multiple_of(i * SUBLANES, SUBLANES), SUBLANES)
        rw = rw_ref[rows, :]
        h2 = h1_ref[rows, :] + (_unpack_rows(buf_ref[prev, 0, rows, :]) * rw[:, 0:1]
                                + _unpack_rows(buf_ref[prev, 1, rows, :]) * rw[:, 1:2])
        ms = jnp.mean(h2 * h2, axis=-1, keepdims=True)
        res_ref[prev, rows, :] = h2 * lax.rsqrt(ms + NORM_EPS) * fg_ref[...]

    def begin_finish():
        for k in range(TOP_K):
            pltpu.make_async_copy(yb_ref.at[pl.ds(0, tc)], buf_ref.at[prev, k], sem.at[prev]).wait()

        @pl.when(step >= 3)
        def _():
            output_copies(step - 3, "wait")

    def both(i, carry):
        request_rows(i)
        finish_rows(i)
        return carry

    def request_only(i, carry):
        request_rows(i)
        return carry

    def finish_only(i, carry):
        finish_rows(i)
        return carry

    @pl.when(step == 0)
    def _():
        lax.fori_loop(0, tc // SUBLANES, request_only, 0)

    @pl.when((step > 0) & (step < n_tiles))
    def _():
        begin_finish()
        lax.fori_loop(0, tc // SUBLANES, both, 0, unroll=8)
        output_copies(step - 1, "start")

    @pl.when(step == n_tiles)
    def _():
        begin_finish()
        lax.fori_loop(0, tc // SUBLANES, finish_only, 0, unroll=4)
        output_copies(step - 1, "start")

        @pl.when(step >= 2)
        def _():
            output_copies(step - 2, "wait")

        output_copies(step - 1, "wait")


def _combine(dest_flat, yb, h1, rw, final_g, tc, seq):
    t, d = h1.shape
    assert tc % N_META == 0 and seq % N_META == 0
    n_out = t - (t // seq) * N_META
    n_tiles = t // tc
    nxt = lambda i: jnp.minimum(i, n_tiles - 1)
    fin = lambda i: jnp.maximum(i - 1, 0)
    return pl.pallas_call(
        functools.partial(_combine_kernel, tc=tc, groups_per_batch=seq // N_META),
        grid=(n_tiles + 1,),
        in_specs=[pl.BlockSpec((None, 1, TOP_K * tc), lambda i: (nxt(i), 0, 0), memory_space=pltpu.SMEM),
                  pl.BlockSpec(memory_space=pl.ANY),
                  pl.BlockSpec((tc, d), lambda i: (fin(i), 0)),
                  pl.BlockSpec((tc, LANES), lambda i: (fin(i), 0)),
                  pl.BlockSpec((1, d), lambda i: (0, 0))],
        out_specs=pl.BlockSpec(memory_space=pl.ANY),
        out_shape=jax.ShapeDtypeStruct((n_out, d), F32),
        scratch_shapes=[pltpu.VMEM((2, TOP_K, tc, d // 2), jnp.uint32), pltpu.VMEM((2, tc, d), F32),
                        pltpu.SemaphoreType.DMA((2,)), pltpu.SemaphoreType.DMA((2,))],
        compiler_params=_params(("arbitrary",)),
        name="combine",
    )(dest_flat, yb, h1, rw, final_g)


def _tile(total, candidates):
    for c in candidates:
        if total % c == 0:
            return c
    raise ValueError(f"no tile for {total}")


def _dft_tables(seq):
    kp = -(-seq // LANES) * LANES
    nat = -(-(seq // 2 + 1) // 16) * 16
    mir = seq - nat
    c = np.arange(FGROUP_DIM, dtype=np.int64)
    ang_c = (2.0 * np.pi / FGROUP_DIM) * ((c[:, None] * c[None, :]) % FGROUP_DIM)
    cs128 = np.concatenate([np.cos(ang_c), np.sin(ang_c)], axis=1)
    k = np.arange(nat, dtype=np.int64)
    n = np.arange(seq, dtype=np.int64)
    ang_l = (2.0 * np.pi / seq) * ((k[:, None] * n[None, :]) % seq)
    twid = np.zeros((nat, 2 * kp), np.float32)
    twid[:, :seq] = np.cos(ang_l)
    twid[:, kp:kp + seq] = np.sin(ang_l)
    flip = np.zeros((mir, nat), np.float32)
    flip[np.arange(mir), mir - np.arange(mir)] = 1.0
    as_const = lambda t: jnp.asarray(t.astype(MXU_DTYPE))
    return as_const(cs128), as_const(twid), as_const(flip)


def kernel(x, meta_tokens, norm1_g, w_in, conv_w, a_log_fwd, dt_bias_fwd, a_log_bwd, dt_bias_bwd, out_norm_g, w_fourier, w_delta, w_out, norm2_g, w_router_group, b_router_group, w_router_expert, b_router_expert, w_gate_e, w_up_e, w_down_e, final_norm_g):
    bsz, n_real, d = x.shape
    seq = N_META + n_real
    n_tok = bsz * seq
    assert (CHUNK - N_META) % CHUNK + seq == -(-seq // CHUNK) * CHUNK, "meta tokens must fill the tail of chunk 0"
    assert w_in.shape[0] == 1, "single trunk layer"
    layer = 0

    meta = jnp.broadcast_to(meta_tokens[None].astype(x.dtype), (bsz, N_META, d))
    h = jnp.concatenate([meta, x], axis=1).reshape(n_tok, d)

    w = w_in[layer]
    o_f, o_qkv, o_z, o_ab, o_gf, o_gd = 0, F_WIDTH, F_WIDTH + 3 * QK_WIDTH, F_WIDTH + 4 * QK_WIDTH, \
        F_WIDTH + 4 * QK_WIDTH + 4 * N_HEADS, F_WIDTH + 4 * QK_WIDTH + 4 * N_HEADS + d
    w_cat = jnp.concatenate(
        [w[:, o_qkv:o_z], w[:, o_z:o_ab], w[:, o_gf:o_gd], w[:, o_gd:o_gd + d], w[:, o_f:o_qkv]],
        axis=1).astype(MXU_DTYPE)
    w_ab = jnp.pad(w[:, o_ab:o_gf], ((0, 0), (0, LANES - 4 * N_HEADS))).astype(MXU_DTYPE)

    proj, ab = _in_proj(h, norm1_g[layer][None, :], w_cat, w_ab, _tile(n_tok, (1032, 104, 8)))
    proj3 = proj.reshape(bsz, seq, PROJ_WIDTH)

    ymix = _fnet(proj3, *_dft_tables(seq))

    gate_prm = jnp.zeros((SUBLANES, LANES), F32)
    gate_prm = gate_prm.at[0, 0:N_HEADS].set(a_log_fwd[layer]).at[0, 2 * N_HEADS:3 * N_HEADS].set(a_log_bwd[layer])
    gate_prm = gate_prm.at[1, 0:N_HEADS].set(dt_bias_fwd[layer]).at[1, 2 * N_HEADS:3 * N_HEADS].set(dt_bias_bwd[layer])
    og = _gdn(proj3, ab.reshape(bsz, seq, LANES), conv_w[layer], gate_prm, out_norm_g[layer][None, :])

    w_router = jnp.zeros((d, LANES), F32)
    w_router = w_router.at[:, 0:N_GROUPS].set(w_router_group[layer])
    w_router = w_router.at[:, N_GROUPS:N_GROUPS + N_EXPERTS].set(w_router_expert[layer])
    b_router = jnp.zeros((1, LANES), F32)
    b_router = b_router.at[0, 0:N_GROUPS].set(b_router_group[layer])
    b_router = b_router.at[0, N_GROUPS:N_GROUPS + N_EXPERTS].set(b_router_expert[layer])
    tt = _tile(n_tok, (384, 128))
    h1, hn, ridx, rw, cnt = _merge(
        h, og.reshape(n_tok, d), ymix.reshape(n_tok, F_WIDTH), proj,
        w_fourier[layer].astype(MXU_DTYPE), w_delta[layer].astype(MXU_DTYPE), w_out[layer].astype(MXU_DTYPE),
        norm2_g[layer][None, :], w_router, b_router, _tile(n_tok, (688, 384, 128)))

    n_assign = n_tok * TOP_K
    n_blocks = -(-n_assign // EXPERT_BLOCK) + N_EXPERTS
    dest, seg = _plan(ridx, cnt, tt)
    dest_flat = dest[:, :TOP_K].reshape(n_tok // tt, 1, TOP_K * tt)
    seg_start = seg[0, :N_EXPERTS]
    seg_blocks = seg[SUBLANES, :N_EXPERTS]

    xb = _scatter(dest_flat, hn, jnp.zeros((n_blocks * EXPERT_BLOCK, d // 2), jnp.uint32), tt)
    yb = _experts(seg_start, seg_blocks, xb, w_gate_e[layer], w_up_e[layer], w_down_e[layer])
    out = _combine(dest_flat, yb, h1, rw, final_norm_g[None, :], tt, seq)
    return out.reshape(bsz, n_real, d)
```

```python
import functools
import math

import jax
import jax.numpy as jnp
import numpy as np
from jax import lax
from jax.experimental import pallas as pl
from jax.experimental.pallas import tpu as pltpu

F32 = jnp.float32
MXU_DTYPE = jnp.bfloat16

N_META = 16
CHUNK = 64
N_FGROUPS = 4
FGROUP_DIM = 128
F_WIDTH = N_FGROUPS * FGROUP_DIM
N_HEADS = 8
HEAD_DIM = 128
QK_WIDTH = N_HEADS * HEAD_DIM
CONV_K = 5
N_GROUPS = 8
EXPERTS_PER_GROUP = 8
N_EXPERTS = N_GROUPS * EXPERTS_PER_GROUP
TOP_K = 2
EXPERT_BLOCK = 256
NORM_EPS = 1e-6
LANES = 128
SUBLANES = 8
VMEM_LIMIT = 52 * 1024 * 1024
GDN_GROUP = 11
DMA_UNROLL = 8
EXPERT_X_SLOTS = 4
EXPERT_Y_SLOTS = 3

COL_QKV = 0
COL_Z = 3 * QK_WIDTH
COL_GATE_F = COL_Z + QK_WIDTH
COL_GATE_D = COL_GATE_F + 1024
COL_FIN = COL_GATE_D + 1024
PROJ_WIDTH = COL_FIN + F_WIDTH
PROJ_TN = 3328
PROJ_DTYPE = jnp.bfloat16


def _dot(a, b):
    return jnp.dot(a, b, preferred_element_type=F32)


def _dot_nt(a, b):
    return lax.dot_general(a, b, (((1,), (1,)), ((), ())), preferred_element_type=F32)


def _dot_tn(a, b):
    return lax.dot_general(a, b, (((0,), (0,)), ((), ())), preferred_element_type=F32)


def _split3(x):
    hi = x.astype(MXU_DTYPE)
    r1 = x - hi.astype(F32)
    mid = r1.astype(MXU_DTYPE)
    lo = (r1 - mid.astype(F32)).astype(MXU_DTYPE)
    return hi, mid, lo


def _exact_left(sel, x):
    hi, mid, lo = _split3(x)
    return _dot(sel, hi) + _dot(sel, mid) + _dot(sel, lo)


def _exact_right(x, sel):
    hi, mid, lo = _split3(x)
    return _dot(hi, sel) + _dot(mid, sel) + _dot(lo, sel)


def _pack_rows(x):
    half = x.shape[1] // 2
    lo = lax.bitcast_convert_type(x[:, :half].astype(jnp.bfloat16).astype(F32), jnp.uint32)
    hi = lax.bitcast_convert_type(x[:, half:].astype(jnp.bfloat16).astype(F32), jnp.uint32)
    return (lo >> 16) | hi


def _unpack_rows(u):
    lo = lax.bitcast_convert_type(u << 16, F32)
    hi = lax.bitcast_convert_type(u & jnp.uint32(0xFFFF0000), F32)
    return jnp.concatenate([lo, hi], axis=1)


def _silu(x):
    return x * jax.nn.sigmoid(x)


def _params(sem):
    return pltpu.CompilerParams(dimension_semantics=sem, vmem_limit_bytes=VMEM_LIMIT)


def _inproj_kernel(h_ref, g_ref, w_ref, wab_ref, o_ref, ab_ref, hn_ref):
    @pl.when(pl.program_id(1) == 0)
    def _():
        x = h_ref[...]
        ms = jnp.mean(x * x, axis=-1, keepdims=True)
        hn_ref[...] = (x * lax.rsqrt(ms + NORM_EPS) * g_ref[...]).astype(MXU_DTYPE)
        ab_ref[...] = _dot(hn_ref[...], wab_ref[...])

    o_ref[...] = _dot(hn_ref[...], w_ref[...]).astype(o_ref.dtype)


def _in_proj(h2d, gain, w_cat, w_ab, tm):
    t, d = h2d.shape
    n = w_cat.shape[1]
    return pl.pallas_call(
        _inproj_kernel,
        grid=(t // tm, n // PROJ_TN),
        in_specs=[
            pl.BlockSpec((tm, d), lambda i, j: (i, 0)),
            pl.BlockSpec((1, d), lambda i, j: (0, 0)),
            pl.BlockSpec((d, PROJ_TN), lambda i, j: (0, j)),
            pl.BlockSpec((d, LANES), lambda i, j: (0, 0)),
        ],
        out_specs=[pl.BlockSpec((tm, PROJ_TN), lambda i, j: (i, j)),
                   pl.BlockSpec((tm, LANES), lambda i, j: (i, 0))],
        out_shape=[jax.ShapeDtypeStruct((t, n), PROJ_DTYPE),
                   jax.ShapeDtypeStruct((t, LANES), F32)],
        scratch_shapes=[pltpu.VMEM((tm, d), MXU_DTYPE)],
        compiler_params=_params(("arbitrary", "arbitrary")),
        name="in_proj",
    )(h2d, gain, w_cat, w_ab)


def _fnet_kernel(x_ref, cs_ref, tw_ref, flip_ref, o_ref, r_ref, *, seq, kp, nat, scale):
    tail = kp - seq
    if tail:
        r_ref[seq:kp, :] = jnp.zeros((tail, F_WIDTH), MXU_DTYPE)
        r_ref[kp + seq:, :] = jnp.zeros((tail, F_WIDTH), MXU_DTYPE)
    cs = cs_ref[...]
    for g in range(N_FGROUPS):
        cols = slice(g * FGROUP_DIM, (g + 1) * FGROUP_DIM)
        p = _dot(x_ref[:, cols].astype(MXU_DTYPE), cs)
        r_ref[0:seq, cols] = p[:, :FGROUP_DIM].astype(MXU_DTYPE)
        r_ref[kp:kp + seq, cols] = p[:, FGROUP_DIM:].astype(MXU_DTYPE)

    a = _dot(tw_ref[:, :kp], r_ref[0:kp, :])
    b = _dot(tw_ref[:, kp:], r_ref[kp:, :])
    o_ref[0:nat, :] = ((a - b) * scale).astype(o_ref.dtype)
    mirrored = ((a + b) * scale).astype(MXU_DTYPE)
    o_ref[nat:, :] = _dot(flip_ref[...], mirrored).astype(o_ref.dtype)


def _fnet(proj3, cs128, twid, flip):
    b, seq, _ = proj3.shape
    nat, kp = twid.shape[0], twid.shape[1] // 2
    scale = 1.0 / math.sqrt(seq * FGROUP_DIM)
    const = lambda shape: pl.BlockSpec(shape, lambda bi: (0, 0))
    return pl.pallas_call(
        functools.partial(_fnet_kernel, seq=seq, kp=kp, nat=nat, scale=scale),
        grid=(b,),
        in_specs=[
            pl.BlockSpec((None, seq, F_WIDTH), lambda bi: (bi, 0, COL_FIN // F_WIDTH)),
            const((FGROUP_DIM, 2 * FGROUP_DIM)), const((nat, 2 * kp)), const((seq - nat, nat)),
        ],
        out_specs=pl.BlockSpec((None, seq, F_WIDTH), lambda bi: (bi, 0, 0)),
        out_shape=jax.ShapeDtypeStruct((b, seq, F_WIDTH), MXU_DTYPE),
        scratch_shapes=[pltpu.VMEM((2 * kp, F_WIDTH), MXU_DTYPE)],
        compiler_params=_params(("arbitrary",)),
        name="fnet",
    )(proj3, cs128, twid, flip)


def _aligned(x, m):
    return x if isinstance(x, int) else pl.multiple_of(x, m)


def _gdn_kernel(q_ref, k_ref, v_ref, z_ref, ab_ref, cwq_ref, cwk_ref, cwv_ref, prm_ref, ong_ref,
                o_ref,
                rawq_ref, rawk_ref, rawv_ref, qs_ref, ks_ref, vs_ref, gsel_ref, grow_ref, gmix_ref,
                qe_ref, mp_ref, nn_ref, dd_ref, osum_ref,
                *, seq, lp, group, n_heads_total):
    step = pl.program_id(0)
    head = jnp.minimum(step, n_heads_total - 1) % N_HEADS
    nc = lp // CHUNK
    padr = lp - seq
    halo = SUBLANES
    row64 = lax.broadcasted_iota(jnp.int32, (CHUNK, LANES), 0)
    lane64 = lax.broadcasted_iota(jnp.int32, (CHUNK, LANES), 1)

    def for_groups(fn, carry=0):
        n_full, rem = divmod(nc, group)
        if n_full:
            carry = lax.fori_loop(0, n_full, lambda i, c: fn([i * group + j for j in range(group)], c), carry)
        if rem:
            carry = fn([n_full * group + j for j in range(rem)], carry)
        return carry

    cur_half = step % 2
    qe_w, mp_w, nn_w, dd_w, osum_w = (r.at[cur_half] for r in (qe_ref, mp_ref, nn_ref, dd_ref, osum_ref))
    qe_r, mp_r, nn_r, dd_r, osum_r = (r.at[1 - cur_half] for r in (qe_ref, mp_ref, nn_ref, dd_ref, osum_ref))

    streams = ((rawq_ref, q_ref, cwq_ref, qs_ref, HEAD_DIM ** -0.5),
               (rawk_ref, k_ref, cwk_ref, ks_ref, 1.0),
               (rawv_ref, v_ref, cwv_ref, vs_ref, None))
    def stage_conv_inputs():
        for raw_ref, src_ref, _, _, _ in streams:
            raw_ref[0:padr + halo, :] = jnp.zeros((padr + halo, LANES), F32)
            raw_ref[padr + halo:padr + halo + seq, :] = src_ref[...].astype(F32)
            raw_ref[lp + halo:lp + 2 * halo, :] = jnp.zeros((halo, LANES), F32)

    def conv_chunk(c):
        base = _aligned(c * CHUNK, CHUNK)
        for raw_ref, _, cw_ref, dst_ref, l2_scale in streams:
            cw = cw_ref[...]
            acc = None
            for j in range(CONV_K):
                off = halo - (CONV_K - 1) // 2 + j
                tap = cw[j:j + 1, :] * raw_ref[pl.ds(base + off, CHUNK), :]
                acc = tap if acc is None else acc + tap
            y = _silu(acc)
            if isinstance(c, int) and c * CHUNK < padr:
                y = jnp.where(row64 + base >= padr, y, 0.0)
            if l2_scale is not None:
                inv = lax.rsqrt(jnp.sum(y * y, axis=-1, keepdims=True) + NORM_EPS)
                y = y * (inv if l2_scale == 1.0 else inv * l2_scale)
            dst_ref[pl.ds(base, CHUNK), :] = y

    raw_ref = rawq_ref

    def stage_gate_inputs():
        raw_ref[0:padr, :] = jnp.zeros((padr, LANES), F32)
        raw_ref[padr:padr + seq, :] = ab_ref[...]

    neg_a = -jnp.exp(prm_ref[0:1, :])
    dt_bias = prm_ref[1:2, :]
    is_g = (lane64 < 8) | ((lane64 >= 16) & (lane64 < 24))
    is_beta = ((lane64 >= 8) & (lane64 < 16)) | ((lane64 >= 24) & (lane64 < 32))
    r128 = lax.broadcasted_iota(jnp.int32, (2 * CHUNK, CHUNK), 0)
    c128 = lax.broadcasted_iota(jnp.int32, (2 * CHUNK, CHUNK), 1)
    tri2 = jnp.where(((r128 < CHUNK) & (r128 >= c128)) | ((r128 >= CHUNK) & (r128 - CHUNK <= c128)),
                     1.0, 0.0).astype(MXU_DTYPE)
    sel_r = lax.broadcasted_iota(jnp.int32, (LANES, 4 * LANES), 0)
    sel_c = lax.broadcasted_iota(jnp.int32, (LANES, 4 * LANES), 1)
    n_gate_cols = 4 * N_HEADS
    sel_packed = jnp.where((sel_r < 3 * n_gate_cols)
                           & (sel_r % n_gate_cols == head + N_HEADS * (sel_c // LANES)), 1.0, 0.0).astype(MXU_DTYPE)
    tri2x3 = jnp.concatenate([tri2, tri2, tri2], axis=1)

    def gate_batch_group(cs, carry):
        bases = [_aligned(c * CHUNK, CHUNK) for c in cs]
        gates, parts = [], []
        for base in bases:
            ab = raw_ref[pl.ds(base, CHUNK), :]
            g = neg_a * jax.nn.softplus(ab + dt_bias)
            beta = jax.nn.sigmoid(ab)
            gt = jnp.where(is_g, g, jnp.where(is_beta, beta, 0.0))
            gt = jnp.where(row64 + base >= padr, gt, 0.0)
            gates.append(gt)
            parts.append(jnp.concatenate(_split3(gt), axis=0))
        cums = [_dot(tri2x3, p) for p in parts]
        for c, base, cm, gt in zip(cs, bases, cums, gates):
            m = jnp.where(lane64 < 8, cm[:CHUNK], jnp.where((lane64 >= 16) & (lane64 < 24), cm[CHUNK:], gt))
            gmix_ref[pl.ds(base, CHUNK), :] = m
            grow_ref[pl.ds(_aligned(c * LANES, LANES), LANES), :] = jnp.concatenate([m, m], axis=0).T
        return carry

    def gate_head_group(cs, carry):
        rows = len(cs) * CHUNK
        base = _aligned(cs[0] * CHUNK, CHUNK)
        hi, mid, lo = _split3(gmix_ref[pl.ds(base, rows), :])
        packed = (hi.astype(F32) + pltpu.roll(mid.astype(F32), n_gate_cols, 1)
                  + pltpu.roll(lo.astype(F32), 2 * n_gate_cols, 1)).astype(MXU_DTYPE)
        gsel_ref[pl.ds(base, rows), :] = _dot(packed, sel_packed)
        return carry

    fwd_half = lane64 < CHUNK
    col64 = lane64 & (CHUNK - 1)
    incl2 = (fwd_half & (row64 >= col64)) | (~fwd_half & (row64 <= col64))
    strict2 = (fwd_half & (row64 > col64)) | (~fwd_half & (row64 < col64))
    eye2 = jnp.where(row64 == col64, 1.0, 0.0).astype(F32)

    def blockdiag(p):
        return jnp.concatenate([jnp.where(fwd_half, p, 0.0), jnp.where(fwd_half, 0.0, p)],
                               axis=0).astype(MXU_DTYPE)

    n_neumann = int(math.log2(CHUNK)) - 2
    n_stages = n_neumann + 6

    def chunk_group(cs, carry, with_scan):
        n = len(cs)
        bases = [_aligned(c * CHUNK, CHUNK) for c in cs]
        cb128 = [_aligned(c * LANES, LANES) for c in cs]
        cb8 = [_aligned(c * SUBLANES, SUBLANES) for c in cs]
        stage = [0]

        def end_of_stage(carry):
            s = stage[0]
            stage[0] += 1
            if with_scan:
                for j in range(n):
                    if j * n_stages // n == s:
                        carry = scan_step(cs[j], carry)
            return carry

        def gate_cols(j):
            gs = gsel_ref[pl.ds(bases[j], CHUNK), :]
            return gs[:, 0:LANES], gs[:, LANES:2 * LANES], gs[:, 2 * LANES:3 * LANES], gs[:, 3 * LANES:]

        kq = []
        for j in range(n):
            k16 = ks_ref[pl.ds(bases[j], CHUNK), :].astype(MXU_DTYPE)
            q16 = qs_ref[pl.ds(bases[j], CHUNK), :].astype(MXU_DTYPE)
            kq.append(_dot_nt(jnp.concatenate([k16, q16], axis=0), jnp.concatenate([k16, k16], axis=0)))
        carry = end_of_stage(carry)

        ts, ps, qkds = [], [], []
        for j in range(n):
            gf, bf, gb, bb = gate_cols(j)
            row_f = grow_ref[pl.ds(cb128[j] + head, 1), :]
            row_b = grow_ref[pl.ds(cb128[j] + 2 * N_HEADS + head, 1), :]
            g_col = jnp.where(fwd_half, gf, gb)
            g_row = jnp.where(fwd_half, row_f, row_b)
            decay = jnp.where(incl2, jnp.exp(jnp.where(incl2, g_col - g_row, 0.0)), 0.0)
            a2 = jnp.where(strict2, kq[j][:CHUNK] * jnp.where(fwd_half, bf, bb) * decay, 0.0)
            qkds.append(jnp.where(incl2, kq[j][CHUNK:] * decay, 0.0))
            ts.append(eye2 - a2)
            ps.append(a2)

        ps = [_dot(p.astype(MXU_DTYPE), blockdiag(p)) for p in ps]
        carry = end_of_stage(carry)
        for _ in range(n_neumann):
            tps = [_dot(jnp.concatenate([t, p], axis=0).astype(MXU_DTYPE), blockdiag(p)) for t, p in zip(ts, ps)]
            carry = end_of_stage(carry)
            ts = [t + tp[:CHUNK] for t, tp in zip(ts, tps)]
            ps = [tp[CHUNK:] for tp in tps]
        ts = [t + _dot(t.astype(MXU_DTYPE), blockdiag(p)) for t, p in zip(ts, ps)]
        carry = end_of_stage(carry)

        uws = []
        for j in range(n):
            gf, bf, gb, bb = gate_cols(j)
            k = ks_ref[pl.ds(bases[j], CHUNK), :]
            v = vs_ref[pl.ds(bases[j], CHUNK), :]
            x_f = jnp.concatenate([v * bf, k * bf * jnp.exp(gf)], axis=1)
            x_b = jnp.concatenate([v * bb, k * bb * jnp.exp(gb)], axis=1)
            uw = _dot(blockdiag(ts[j]), jnp.concatenate([x_f, x_b], axis=0).astype(MXU_DTYPE))
            uws.append(uw.astype(MXU_DTYPE))
        carry = end_of_stage(carry)

        xxs = [_dot(blockdiag(qkds[j]), uws[j]) for j in range(n)]
        carry = end_of_stage(carry)
        for j in range(n):
            gf, _, gb, _ = gate_cols(j)
            k = ks_ref[pl.ds(bases[j], CHUNK), :]
            q = qs_ref[pl.ds(bases[j], CHUNK), :]
            xx = xxs[j]
            osum_w[pl.ds(bases[j], CHUNK), :] = xx[:CHUNK, :LANES] + xx[CHUNK:, :LANES]
            for d, gc in enumerate((gf, gb)):
                rows = slice(d * CHUNK, (d + 1) * CHUNK)
                g_last = gc[CHUNK - 1:CHUNK, :] if d == 0 else gc[0:1, :]
                k_dec = k * jnp.exp(g_last - gc)
                mn = _dot_tn(k_dec.astype(MXU_DTYPE), uws[j][rows])
                qe_w[d, pl.ds(bases[j], CHUNK), :] = (q * jnp.exp(gc) - xx[rows, LANES:]).astype(MXU_DTYPE)
                nn_w[d, pl.ds(cb128[j], HEAD_DIM), :] = mn[:, :LANES]
                mp_w[d, pl.ds(cb128[j], HEAD_DIM), :] = mn[:, LANES:].astype(MXU_DTYPE)
                dd_w[d, pl.ds(cb8[j], SUBLANES), :] = jnp.broadcast_to(jnp.exp(g_last), (SUBLANES, LANES))
        carry = end_of_stage(carry)
        assert stage[0] == n_stages
        return carry

    def scan_step(i, carry):
        new = []
        for d in range(2):
            s = carry[d]
            c = i if d == 0 else nc - 1 - i
            base = _aligned(c * CHUNK, CHUNK)
            cbase = _aligned(c * HEAD_DIM, HEAD_DIM)
            dbase = _aligned(c * SUBLANES, SUBLANES)
            s16 = s.astype(MXU_DTYPE)
            osum_r[pl.ds(base, CHUNK), :] += _dot(qe_r[d, pl.ds(base, CHUNK), :], s16)
            dec = dd_r[d, pl.ds(dbase, 1), :]
            new.append(dec * s - _dot(mp_r[d, pl.ds(cbase, HEAD_DIM), :], s16)
                       + nn_r[d, pl.ds(cbase, HEAD_DIM), :])
        return tuple(new)

    zero_state = jnp.zeros((HEAD_DIM, HEAD_DIM), F32)

    def write_output():
        gain = ong_ref[...]

        def finish(o, z):
            ms = jnp.mean(o * o, axis=-1, keepdims=True)
            return (o * lax.rsqrt(ms + NORM_EPS) * gain * _silu(z.astype(F32))).astype(o_ref.dtype)

        first = CHUNK - padr if padr else 0
        if first:
            o_ref[0:first, :] = finish(osum_r[padr:CHUNK, :], z_ref[0:first, :])

        def out_body(c, carry):
            base = pl.multiple_of(c * CHUNK, CHUNK)
            dst = pl.multiple_of(c * CHUNK - padr, SUBLANES)
            o_ref[pl.ds(dst, CHUNK), :] = finish(osum_r[pl.ds(base, CHUNK), :], z_ref[pl.ds(dst, CHUNK), :])
            return carry

        lax.fori_loop(1 if first else 0, nc, out_body, 0, unroll=8)

    def conv_only(c, carry):
        conv_chunk(c)
        return carry

    @pl.when(step < n_heads_total)
    def _():
        stage_conv_inputs()
        conv_chunk(0)
        lax.fori_loop(1, nc, conv_only, 0, unroll=4)

    @pl.when((step < n_heads_total) & (head == 0))
    def _():
        stage_gate_inputs()
        for_groups(gate_batch_group)

    @pl.when(step < n_heads_total)
    def _():
        for_groups(gate_head_group)

    @pl.when(step == 0)
    def _():
        for_groups(functools.partial(chunk_group, with_scan=False))

    @pl.when((step > 0) & (step < n_heads_total))
    def _():
        for_groups(functools.partial(chunk_group, with_scan=True), (zero_state, zero_state))
        write_output()

    @pl.when(step == n_heads_total)
    def _():
        lax.fori_loop(0, nc, scan_step, (zero_state, zero_state))
        write_output()


def _gdn(proj3, ab3, conv_w, gate_prm, out_norm_g):
    b, seq, _ = proj3.shape
    lp = -(-seq // CHUNK) * CHUNK
    nc = lp // CHUNK
    n_total = b * N_HEADS
    cur = lambda i: jnp.minimum(i, n_total - 1)
    prev = lambda i: jnp.maximum(i - 1, 0)
    col = lambda off: pl.BlockSpec((None, seq, LANES), lambda i: (cur(i) // N_HEADS, 0, off + cur(i) % N_HEADS))
    cw = lambda off: pl.BlockSpec((CONV_K, LANES), lambda i: (0, off + cur(i) % N_HEADS))
    return pl.pallas_call(
        functools.partial(_gdn_kernel, seq=seq, lp=lp, group=GDN_GROUP, n_heads_total=n_total),
        grid=(n_total + 1,),
        in_specs=[
            col(COL_QKV // LANES), col(COL_QKV // LANES + N_HEADS), col(COL_QKV // LANES + 2 * N_HEADS),
            pl.BlockSpec((None, seq, LANES),
                         lambda i: (prev(i) // N_HEADS, 0, COL_Z // LANES + prev(i) % N_HEADS)),
            pl.BlockSpec((None, seq, LANES), lambda i: (cur(i) // N_HEADS, 0, 0)),
            cw(0), cw(N_HEADS), cw(2 * N_HEADS),
            pl.BlockSpec((SUBLANES, LANES), lambda i: (0, 0)),
            pl.BlockSpec((1, LANES), lambda i: (0, 0)),
        ],
        out_specs=pl.BlockSpec((None, seq, LANES), lambda i: (prev(i) // N_HEADS, 0, prev(i) % N_HEADS)),
        out_shape=jax.ShapeDtypeStruct((b, seq, N_HEADS * HEAD_DIM), MXU_DTYPE),
        scratch_shapes=[
            pltpu.VMEM((lp + 2 * SUBLANES, LANES), F32),
            pltpu.VMEM((lp + 2 * SUBLANES, LANES), F32),
            pltpu.VMEM((lp + 2 * SUBLANES, LANES), F32),
            pltpu.VMEM((lp, LANES), F32),
            pltpu.VMEM((lp, LANES), F32),
            pltpu.VMEM((lp, LANES), F32),
            pltpu.VMEM((lp, 4 * LANES), F32),
            pltpu.VMEM((nc * LANES, LANES), F32),
            pltpu.VMEM((lp, LANES), F32),
            pltpu.VMEM((2, 2, lp, LANES), MXU_DTYPE),
            pltpu.VMEM((2, 2, nc * HEAD_DIM, LANES), MXU_DTYPE),
            pltpu.VMEM((2, 2, nc * HEAD_DIM, LANES), F32),
            pltpu.VMEM((2, 2, nc * SUBLANES, LANES), F32),
            pltpu.VMEM((2, lp, LANES), F32),
        ],
        compiler_params=_params(("arbitrary",)),
        name="gdn",
    )(proj3, proj3, proj3, proj3, ab3, conv_w, conv_w, conv_w, gate_prm, out_norm_g)


def _merge_kernel(h_ref, og_ref, ym_ref, gf_ref, gd_ref, wf_ref, wd_ref, wo_ref, n2_ref, wr_ref, br_ref,
                  h1_ref, hn_ref, ridx_ref, rw_ref, cnt_ref):
    @pl.when(pl.program_id(0) == 0)
    def _():
        cnt_ref[...] = jnp.zeros_like(cnt_ref)

    y_f = _dot(ym_ref[...], wf_ref[...])
    y_d = _dot(og_ref[...], wd_ref[...])
    merged = (jax.nn.sigmoid(gf_ref[...].astype(F32)) * y_f
              + jax.nn.sigmoid(gd_ref[...].astype(F32)) * y_d)
    h1 = h_ref[...] + _dot(merged.astype(MXU_DTYPE), wo_ref[...])
    h1_ref[...] = h1
    ms = jnp.mean(h1 * h1, axis=-1, keepdims=True)
    hn = h1 * lax.rsqrt(ms + NORM_EPS) * n2_ref[...]
    hn_ref[...] = _pack_rows(hn)

    x_hi = hn.astype(MXU_DTYPE)
    x_lo = (hn - x_hi.astype(F32)).astype(MXU_DTYPE)
    w = wr_ref[...]
    w_hi = w.astype(MXU_DTYPE)
    w_lo = (w - w_hi.astype(F32)).astype(MXU_DTYPE)
    logits = _dot(x_hi, w_hi) + _dot(x_hi, w_lo) + _dot(x_lo, w_hi) + br_ref[...]

    tm = logits.shape[0]
    lane = lax.broadcasted_iota(jnp.int32, (tm, LANES), 1)
    neg_inf = -jnp.inf
    is_group = lane < N_GROUPS
    gl = jnp.where(is_group, logits, neg_inf)
    ge = jnp.exp(gl - jnp.max(gl, axis=-1, keepdims=True))
    gp = ge / jnp.sum(ge, axis=-1, keepdims=True)
    p_group = jnp.max(gp, axis=-1, keepdims=True)
    g_idx = jnp.min(jnp.where(is_group & (gp == p_group), lane, LANES), axis=-1, keepdims=True)

    in_group = (lane >= N_GROUPS) & (lane < N_GROUPS + N_EXPERTS) & (((lane - N_GROUPS) >> 3) == g_idx)
    el = jnp.where(in_group, logits, neg_inf)
    ee = jnp.exp(el - jnp.max(el, axis=-1, keepdims=True))
    ep = ee / jnp.sum(ee, axis=-1, keepdims=True)
    v1 = jnp.max(jnp.where(in_group, ep, -1.0), axis=-1, keepdims=True)
    i1 = jnp.min(jnp.where(in_group & (ep == v1), lane, LANES), axis=-1, keepdims=True)
    rest = in_group & (lane != i1)
    v2 = jnp.max(jnp.where(rest, ep, -1.0), axis=-1, keepdims=True)
    i2 = jnp.min(jnp.where(rest & (ep == v2), lane, LANES), axis=-1, keepdims=True)
    denom = v1 + v2
    w1 = p_group * (v1 / denom)
    w2 = p_group * (v2 / denom)
    e1 = i1 - N_GROUPS
    e2 = i2 - N_GROUPS
    ridx_ref[...] = jnp.where(lane == 0, e1, jnp.where(lane == 1, e2, 0))
    rw_ref[...] = jnp.where(lane == 0, w1, jnp.where(lane == 1, w2, 0.0))
    onehots = jnp.where((lane == e1) | (lane == e2), 1.0, 0.0)
    cnt_ref[...] += jnp.broadcast_to(jnp.sum(onehots, axis=0, keepdims=True), cnt_ref.shape)


def _merge(h2d, og2d, ymix2d, proj2d, w_fourier, w_delta, w_out, norm2_g, w_router, b_router, tm):
    t, d = h2d.shape
    row = lambda w: pl.BlockSpec((tm, w), lambda i: (i, 0))
    const = lambda shape: pl.BlockSpec(shape, lambda i: (0, 0))
    return pl.pallas_call(
        _merge_kernel,
        grid=(t // tm,),
        in_specs=[
            row(d), row(d), row(F_WIDTH),
            pl.BlockSpec((tm, d), lambda i: (i, COL_GATE_F // d)),
            pl.BlockSpec((tm, d), lambda i: (i, COL_GATE_D // d)),
            const((F_WIDTH, d)), const((d, d)), const((d, d)), const((1, d)),
            const((d, LANES)), const((1, LANES)),
        ],
        out_specs=[row(d), row(d // 2), row(LANES), row(LANES), const((SUBLANES, LANES))],
        out_shape=[
            jax.ShapeDtypeStruct((t, d), F32),
            jax.ShapeDtypeStruct((t, d // 2), jnp.uint32),
            jax.ShapeDtypeStruct((t, LANES), jnp.int32),
            jax.ShapeDtypeStruct((t, LANES), F32),
            jax.ShapeDtypeStruct((SUBLANES, LANES), F32),
        ],
        compiler_params=_params(("arbitrary",)),
        name="merge",
    )(h2d, og2d, ymix2d, proj2d, proj2d, w_fourier, w_delta, w_out, norm2_g, w_router, b_router)


def _plan_kernel(ridx_ref, cnt_ref, dest_ref, seg_ref, base_ref, run_ref):
    tt = ridx_ref.shape[0]

    @pl.when(pl.program_id(0) == 0)
    def _():
        cnt = cnt_ref[...].astype(jnp.int32)
        n_blk = (cnt + (EXPERT_BLOCK - 1)) // EXPERT_BLOCK
        padded = (n_blk * EXPERT_BLOCK).astype(F32)
        r = lax.broadcasted_iota(jnp.int32, (LANES, LANES), 0)
        c = lax.broadcasted_iota(jnp.int32, (LANES, LANES), 1)
        before = jnp.where(r < c, 1.0, 0.0).astype(MXU_DTYPE)
        start = _exact_right(padded, before)
        base_ref[...] = start
        run_ref[...] = jnp.zeros_like(run_ref)
        seg_ref[0:SUBLANES, :] = start.astype(jnp.int32)
        seg_ref[SUBLANES:2 * SUBLANES, :] = n_blk
        seg_ref[2 * SUBLANES:, :] = cnt

    lane = lax.broadcasted_iota(jnp.int32, (tt, LANES), 1)
    ridx = ridx_ref[...]
    oh0 = jnp.where(lane == ridx[:, 0:1], 1.0, 0.0)
    oh1 = jnp.where(lane == ridx[:, 1:2], 1.0, 0.0)
    r = lax.broadcasted_iota(jnp.int32, (tt, tt), 0)
    c = lax.broadcasted_iota(jnp.int32, (tt, tt), 1)
    earlier = jnp.where(c < r, 1.0, 0.0).astype(MXU_DTYPE)
    prefix = _dot(earlier, jnp.concatenate([oh0, oh1], axis=1).astype(MXU_DTYPE))
    c0 = jnp.sum(oh0, axis=0, keepdims=True)
    c1 = jnp.sum(oh1, axis=0, keepdims=True)
    first = base_ref[0:1, :] + run_ref[0:1, :]
    d0 = jnp.sum(oh0 * (prefix[:, :LANES] + first), axis=-1, keepdims=True)
    d1 = jnp.sum(oh1 * (prefix[:, LANES:] + first + c0), axis=-1, keepdims=True)
    run_ref[...] += jnp.broadcast_to(c0 + c1, run_ref.shape)
    dest_ref[...] = jnp.where(lane == 0, d0, jnp.where(lane == 1, d1, 0.0)).astype(jnp.int32)


def _plan(ridx, cnt, tt):
    t = ridx.shape[0]
    return pl.pallas_call(
        _plan_kernel,
        grid=(t // tt,),
        in_specs=[pl.BlockSpec((tt, LANES), lambda i: (i, 0)),
                  pl.BlockSpec((SUBLANES, LANES), lambda i: (0, 0))],
        out_specs=[pl.BlockSpec((tt, LANES), lambda i: (i, 0)),
                   pl.BlockSpec((3 * SUBLANES, LANES), lambda i: (0, 0))],
        out_shape=[jax.ShapeDtypeStruct((t, LANES), jnp.int32),
                   jax.ShapeDtypeStruct((3 * SUBLANES, LANES), jnp.int32)],
        scratch_shapes=[pltpu.VMEM((SUBLANES, LANES), F32), pltpu.VMEM((SUBLANES, LANES), F32)],
        compiler_params=_params(("arbitrary",)),
        name="plan",
    )(ridx, cnt)


def _scatter_kernel(dest_ref, hn_ref, xb_in_ref, xb_ref, sem, *, ts):
    del xb_in_ref

    def issue(t, carry):
        src = hn_ref.at[pl.ds(t, 1)]
        for k in range(TOP_K):
            pltpu.make_async_copy(src, xb_ref.at[pl.ds(dest_ref[0, TOP_K * t + k], 1)], sem).start(priority=k % 2)
        return carry

    lax.fori_loop(0, ts, issue, 0, unroll=DMA_UNROLL)
    for _ in range(TOP_K):
        pltpu.make_async_copy(hn_ref, xb_ref.at[pl.ds(0, ts)], sem).wait()


def _scatter(dest_flat, hn2d, xb_zero, ts):
    t, d = hn2d.shape
    return pl.pallas_call(
        functools.partial(_scatter_kernel, ts=ts),
        grid=(t // ts,),
        in_specs=[pl.BlockSpec((None, 1, TOP_K * ts), lambda i: (i, 0, 0), memory_space=pltpu.SMEM),
                  pl.BlockSpec((ts, d), lambda i: (i, 0)),
                  pl.BlockSpec(memory_space=pl.ANY)],
        out_specs=pl.BlockSpec(memory_space=pl.ANY),
        out_shape=jax.ShapeDtypeStruct(xb_zero.shape, xb_zero.dtype),
        scratch_shapes=[pltpu.SemaphoreType.DMA(())],
        input_output_aliases={2: 0},
        compiler_params=_params(("arbitrary",)),
        name="scatter",
    )(dest_flat, hn2d, xb_zero)


def _expert_kernel(start_ref, nblk_ref, xb_ref, wg_ref, wu_ref, wd_ref, yb_ref,
                   wg16_ref, wu16_ref, wd16_ref, xbuf_ref, ybuf_ref, sem_in, sem_out):
    expert = pl.program_id(0)
    last = pl.num_programs(0) - 1
    n = nblk_ref[expert]
    first = start_ref[expert] // EXPERT_BLOCK
    total = start_ref[last] // EXPERT_BLOCK + nblk_ref[last]

    def rows(g):
        return pl.ds(pl.multiple_of(g * EXPERT_BLOCK, EXPERT_BLOCK), EXPERT_BLOCK)

    def x_copy(g):
        slot = g % EXPERT_X_SLOTS
        return pltpu.make_async_copy(xb_ref.at[rows(g)], xbuf_ref.at[slot], sem_in.at[slot])

    def y_copy(g):
        slot = g % EXPERT_Y_SLOTS
        return pltpu.make_async_copy(ybuf_ref.at[slot], yb_ref.at[rows(g)], sem_out.at[slot])

    @pl.when(n > 0)
    def _():
        @pl.when(first == 0)
        def _():
            for j in range(EXPERT_X_SLOTS - 1):
                @pl.when(j < total)
                def _():
                    x_copy(j).start()

        wg16_ref[...] = wg_ref[...].astype(MXU_DTYPE)
        wu16_ref[...] = wu_ref[...].astype(MXU_DTYPE)
        wd16_ref[...] = wd_ref[...].astype(MXU_DTYPE)

        def body(i, carry):
            g = first + i

            @pl.when(g + (EXPERT_X_SLOTS - 1) < total)
            def _():
                x_copy(g + (EXPERT_X_SLOTS - 1)).start()

            x_copy(g).wait()

            @pl.when(g >= EXPERT_Y_SLOTS)
            def _():
                y_copy(g - EXPERT_Y_SLOTS).wait()

            x = _unpack_rows(xbuf_ref[g % EXPERT_X_SLOTS]).astype(MXU_DTYPE)
            gate = _dot(x, wg16_ref[...])
            up = _dot(x, wu16_ref[...])
            ybuf_ref[g % EXPERT_Y_SLOTS] = _pack_rows(_dot((_silu(gate) * up).astype(MXU_DTYPE), wd16_ref[...]))
            y_copy(g).start()
            return carry

        lax.fori_loop(0, n, body, 0)

    @pl.when(expert == last)
    def _():
        for j in range(EXPERT_Y_SLOTS):
            @pl.when(total > j)
            def _():
                y_copy(total - 1 - j).wait()


def _experts(seg_start, seg_blocks, xb, w_gate, w_up, w_down):
    n_slots, row_words = xb.shape
    n_exp, d, de = w_gate.shape
    grid_spec = pltpu.PrefetchScalarGridSpec(
        num_scalar_prefetch=2,
        grid=(n_exp,),
        in_specs=[
            pl.BlockSpec(memory_space=pl.ANY),
            pl.BlockSpec((None, d, de), lambda e, st, nb: (e, 0, 0)),
            pl.BlockSpec((None, d, de), lambda e, st, nb: (e, 0, 0)),
            pl.BlockSpec((None, de, d), lambda e, st, nb: (e, 0, 0)),
        ],
        out_specs=pl.BlockSpec(memory_space=pl.ANY),
        scratch_shapes=[
            pltpu.VMEM((d, de), MXU_DTYPE), pltpu.VMEM((d, de), MXU_DTYPE), pltpu.VMEM((de, d), MXU_DTYPE),
            pltpu.VMEM((EXPERT_X_SLOTS, EXPERT_BLOCK, row_words), jnp.uint32),
            pltpu.VMEM((EXPERT_Y_SLOTS, EXPERT_BLOCK, row_words), jnp.uint32),
            pltpu.SemaphoreType.DMA((EXPERT_X_SLOTS,)), pltpu.SemaphoreType.DMA((EXPERT_Y_SLOTS,)),
        ],
    )
    return pl.pallas_call(
        _expert_kernel,
        grid_spec=grid_spec,
        out_shape=jax.ShapeDtypeStruct((n_slots, row_words), jnp.uint32),
        input_output_aliases={2: 0},
        compiler_params=_params(("arbitrary",)),
        name="experts",
    )(seg_start, seg_blocks, xb, w_gate, w_up, w_down)


def _combine_kernel(dest_ref, yb_ref, h1_ref, rw_ref, fg_ref, o_ref, buf_ref, res_ref, sem, out_sem,
                    *, tc, groups_per_batch):
    step = pl.program_id(0)
    n_tiles = pl.num_programs(0) - 1
    n_grp = tc // N_META
    cur = step % 2
    prev = 1 - cur

    def output_copies(tile, action):
        slot = tile % 2
        first = tile * n_grp
        first_in_batch = first % groups_per_batch
        has_meta = (first_in_batch == 0) | (first_in_batch + n_grp > groups_per_batch)

        def copy_groups(q, n):
            grp = first + q
            dst = pl.multiple_of((grp - grp // groups_per_batch - 1) * N_META, N_META)
            copy = pltpu.make_async_copy(res_ref.at[slot, pl.ds(q * N_META, n * N_META)],
                                         o_ref.at[pl.ds(dst, n * N_META)], out_sem.at[slot])
            copy.start() if action == "start" else copy.wait()

        @pl.when(jnp.logical_not(has_meta))
        def _():
            copy_groups(0, n_grp)

        @pl.when(has_meta)
        def _():
            for q in range(n_grp):
                @pl.when((first + q) % groups_per_batch != 0)
                def _():
                    copy_groups(q, 1)

    def request_rows(i):
        for u in range(SUBLANES):
            t = i * SUBLANES + u
            for k in range(TOP_K):
                pltpu.make_async_copy(yb_ref.at[pl.ds(dest_ref[0, TOP_K * t + k], 1)],
                                      buf_ref.at[cur, k, pl.ds(t, 1)], sem.at[cur]).start(priority=k % 2)

    def finish_rows(i):
        rows = pl.ds(pl.multiple_of(i * SUBLANES, SUBLANES), SUBLANES)
        rw = rw_ref[rows, :]
        h2 = h1_ref[rows, :] + (_unpack_rows(buf_ref[prev, 0, rows, :]) * rw[:, 0:1]
                                + _unpack_rows(buf_ref[prev, 1, rows, :]) * rw[:, 1:2])
        ms = jnp.mean(h2 * h2, axis=-1, keepdims=True)
        res_ref[prev, rows, :] = h2 * lax.rsqrt(ms + NORM_EPS) * fg_ref[...]

    def begin_finish():
        for k in range(TOP_K):
            pltpu.make_async_copy(yb_ref.at[pl.ds(0, tc)], buf_ref.at[prev, k], sem.at[prev]).wait()

        @pl.when(step >= 3)
        def _():
            output_copies(step - 3, "wait")

    def both(i, carry):
        request_rows(i)
        finish_rows(i)
        return carry

    def request_only(i, carry):
        request_rows(i)
        return carry

    def finish_only(i, carry):
        finish_rows(i)
        return carry

    @pl.when(step == 0)
    def _():
        lax.fori_loop(0, tc // SUBLANES, request_only, 0)

    @pl.when((step > 0) & (step < n_tiles))
    def _():
        begin_finish()
        lax.fori_loop(0, tc // SUBLANES, both, 0, unroll=8)
        output_copies(step - 1, "start")

    @pl.when(step == n_tiles)
    def _():
        begin_finish()
        lax.fori_loop(0, tc // SUBLANES, finish_only, 0, unroll=4)
        output_copies(step - 1, "start")

        @pl.when(step >= 2)
        def _():
            output_copies(step - 2, "wait")

        output_copies(step - 1, "wait")


def _combine(dest_flat, yb, h1, rw, final_g, tc, seq):
    t, d = h1.shape
    assert tc % N_META == 0 and seq % N_META == 0
    n_out = t - (t // seq) * N_META
    n_tiles = t // tc
    nxt = lambda i: jnp.minimum(i, n_tiles - 1)
    fin = lambda i: jnp.maximum(i - 1, 0)
    return pl.pallas_call(
        functools.partial(_combine_kernel, tc=tc, groups_per_batch=seq // N_META),
        grid=(n_tiles + 1,),
        in_specs=[pl.BlockSpec((None, 1, TOP_K * tc), lambda i: (nxt(i), 0, 0), memory_space=pltpu.SMEM),
                  pl.BlockSpec(memory_space=pl.ANY),
                  pl.BlockSpec((tc, d), lambda i: (fin(i), 0)),
                  pl.BlockSpec((tc, LANES), lambda i: (fin(i), 0)),
                  pl.BlockSpec((1, d), lambda i: (0, 0))],
        out_specs=pl.BlockSpec(memory_space=pl.ANY),
        out_shape=jax.ShapeDtypeStruct((n_out, d), F32),
        scratch_shapes=[pltpu.VMEM((2, TOP_K, tc, d // 2), jnp.uint32), pltpu.VMEM((2, tc, d), F32),
                        pltpu.SemaphoreType.DMA((2,)), pltpu.SemaphoreType.DMA((2,))],
        compiler_params=_params(("arbitrary",)),
        name="combine",
    )(dest_flat, yb, h1, rw, final_g)


def _tile(total, candidates):
    for c in candidates:
        if total % c == 0:
            return c
    raise ValueError(f"no tile for {total}")


def _dft_tables(seq):
    kp = -(-seq // LANES) * LANES
    nat = -(-(seq // 2 + 1) // 16) * 16
    mir = seq - nat
    c = np.arange(FGROUP_DIM, dtype=np.int64)
    ang_c = (2.0 * np.pi / FGROUP_DIM) * ((c[:, None] * c[None, :]) % FGROUP_DIM)
    cs128 = np.concatenate([np.cos(ang_c), np.sin(ang_c)], axis=1)
    k = np.arange(nat, dtype=np.int64)
    n = np.arange(seq, dtype=np.int64)
    ang_l = (2.0 * np.pi / seq) * ((k[:, None] * n[None, :]) % seq)
    twid = np.zeros((nat, 2 * kp), np.float32)
    twid[:, :seq] = np.cos(ang_l)
    twid[:, kp:kp + seq] = np.sin(ang_l)
    flip = np.zeros((mir, nat), np.float32)
    flip[np.arange(mir), mir - np.arange(mir)] = 1.0
    as_const = lambda t: jnp.asarray(t.astype(MXU_DTYPE))
    return as_const(cs128), as_const(twid), as_const(flip)


def kernel(x, meta_tokens, norm1_g, w_in, conv_w, a_log_fwd, dt_bias_fwd, a_log_bwd, dt_bias_bwd, out_norm_g, w_fourier, w_delta, w_out, norm2_g, w_router_group, b_router_group, w_router_expert, b_router_expert, w_gate_e, w_up_e, w_down_e, final_norm_g):
    bsz, n_real, d = x.shape
    seq = N_META + n_real
    n_tok = bsz * seq
    assert (CHUNK - N_META) % CHUNK + seq == -(-seq // CHUNK) * CHUNK, "meta tokens must fill the tail of chunk 0"
    assert w_in.shape[0] == 1, "single trunk layer"
    layer = 0

    meta = jnp.broadcast_to(meta_tokens[None].astype(x.dtype), (bsz, N_META, d))
    h = jnp.concatenate([meta, x], axis=1).reshape(n_tok, d)

    w = w_in[layer]
    o_f, o_qkv, o_z, o_ab, o_gf, o_gd = 0, F_WIDTH, F_WIDTH + 3 * QK_WIDTH, F_WIDTH + 4 * QK_WIDTH, \
        F_WIDTH + 4 * QK_WIDTH + 4 * N_HEADS, F_WIDTH + 4 * QK_WIDTH + 4 * N_HEADS + d
    w_cat = jnp.concatenate(
        [w[:, o_qkv:o_z], w[:, o_z:o_ab], w[:, o_gf:o_gd], w[:, o_gd:o_gd + d], w[:, o_f:o_qkv]],
        axis=1).astype(MXU_DTYPE)
    w_ab = jnp.pad(w[:, o_ab:o_gf], ((0, 0), (0, LANES - 4 * N_HEADS))).astype(MXU_DTYPE)

    proj, ab = _in_proj(h, norm1_g[layer][None, :], w_cat, w_ab, _tile(n_tok, (1032, 104, 8)))
    proj3 = proj.reshape(bsz, seq, PROJ_WIDTH)

    ymix = _fnet(proj3, *_dft_tables(seq))

    gate_prm = jnp.zeros((SUBLANES, LANES), F32)
    gate_prm = gate_prm.at[0, 0:N_HEADS].set(a_log_fwd[layer]).at[0, 2 * N_HEADS:3 * N_HEADS].set(a_log_bwd[layer])
    gate_prm = gate_prm.at[1, 0:N_HEADS].set(dt_bias_fwd[layer]).at[1, 2 * N_HEADS:3 * N_HEADS].set(dt_bias_bwd[layer])
    og = _gdn(proj3, ab.reshape(bsz, seq, LANES), conv_w[layer], gate_prm, out_norm_g[layer][None, :])

    w_router = jnp.zeros((d, LANES), F32)
    w_router = w_router.at[:, 0:N_GROUPS].set(w_router_group[layer])
    w_router = w_router.at[:, N_GROUPS:N_GROUPS + N_EXPERTS].set(w_router_expert[layer])
    b_router = jnp.zeros((1, LANES), F32)
    b_router = b_router.at[0, 0:N_GROUPS].set(b_router_group[layer])
    b_router = b_router.at[0, N_GROUPS:N_GROUPS + N_EXPERTS].set(b_router_expert[layer])
    tt = _tile(n_tok, (384, 128))
    h1, hn, ridx, rw, cnt = _merge(
        h, og.reshape(n_tok, d), ymix.reshape(n_tok, F_WIDTH), proj,
        w_fourier[layer].astype(MXU_DTYPE), w_delta[layer].astype(MXU_DTYPE), w_out[layer].astype(MXU_DTYPE),
        norm2_g[layer][None, :], w_router, b_router, _tile(n_tok, (688, 384, 128)))

    n_assign = n_tok * TOP_K
    n_blocks = -(-n_assign // EXPERT_BLOCK) + N_EXPERTS
    dest, seg = _plan(ridx, cnt, tt)
    dest_flat = dest[:, :TOP_K].reshape(n_tok // tt, 1, TOP_K * tt)
    seg_start = seg[0, :N_EXPERTS]
    seg_blocks = seg[SUBLANES, :N_EXPERTS]

    xb = _scatter(dest_flat, hn, jnp.zeros((n_blocks * EXPERT_BLOCK, d // 2), jnp.uint32), tt)
    yb = _experts(seg_start, seg_blocks, xb, w_gate_e[layer], w_up_e[layer], w_down_e[layer])
    out = _combine(dest_flat, yb, h1, rw, final_norm_g[None, :], tt, seq)
    return out.reshape(bsz, n_real, d)
```

```python
import functools
import math

import jax
import jax.numpy as jnp
import numpy as np
from jax import lax
from jax.experimental import pallas as pl
from jax.experimental.pallas import tpu as pltpu

F32 = jnp.float32
MXU_DTYPE = jnp.bfloat16

N_META = 16
CHUNK = 64
N_FGROUPS = 4
FGROUP_DIM = 128
F_WIDTH = N_FGROUPS * FGROUP_DIM
N_HEADS = 8
HEAD_DIM = 128
QK_WIDTH = N_HEADS * HEAD_DIM
CONV_K = 5
N_GROUPS = 8
EXPERTS_PER_GROUP = 8
N_EXPERTS = N_GROUPS * EXPERTS_PER_GROUP
TOP_K = 2
EXPERT_BLOCK = 256
NORM_EPS = 1e-6
LANES = 128
SUBLANES = 8
VMEM_LIMIT = 52 * 1024 * 1024
GDN_GROUP = 11
DMA_UNROLL = 8
EXPERT_X_SLOTS = 4
EXPERT_Y_SLOTS = 3

COL_QKV = 0
COL_Z = 3 * QK_WIDTH
COL_GATE_F = COL_Z + QK_WIDTH
COL_GATE_D = COL_GATE_F + 1024
COL_FIN = COL_GATE_D + 1024
PROJ_WIDTH = COL_FIN + F_WIDTH
PROJ_TN = 3328
PROJ_DTYPE = jnp.bfloat16


def _dot(a, b):
    return jnp.dot(a, b, preferred_element_type=F32)


def _dot_nt(a, b):
    return lax.dot_general(a, b, (((1,), (1,)), ((), ())), preferred_element_type=F32)


def _dot_tn(a, b):
    return lax.dot_general(a, b, (((0,), (0,)), ((), ())), preferred_element_type=F32)


def _split3(x):
    hi = x.astype(MXU_DTYPE)
    r1 = x - hi.astype(F32)
    mid = r1.astype(MXU_DTYPE)
    lo = (r1 - mid.astype(F32)).astype(MXU_DTYPE)
    return hi, mid, lo


def _exact_left(sel, x):
    hi, mid, lo = _split3(x)
    return _dot(sel, hi) + _dot(sel, mid) + _dot(sel, lo)


def _exact_right(x, sel):
    hi, mid, lo = _split3(x)
    return _dot(hi, sel) + _dot(mid, sel) + _dot(lo, sel)


def _pack_rows(x):
    half = x.shape[1] // 2
    lo = lax.bitcast_convert_type(x[:, :half].astype(jnp.bfloat16).astype(F32), jnp.uint32)
    hi = lax.bitcast_convert_type(x[:, half:].astype(jnp.bfloat16).astype(F32), jnp.uint32)
    return (lo >> 16) | hi


def _unpack_rows(u):
    lo = lax.bitcast_convert_type(u << 16, F32)
    hi = lax.bitcast_convert_type(u & jnp.uint32(0xFFFF0000), F32)
    return jnp.concatenate([lo, hi], axis=1)


def _silu(x):
    return x * jax.nn.sigmoid(x)


def _params(sem):
    return pltpu.CompilerParams(dimension_semantics=sem, vmem_limit_bytes=VMEM_LIMIT)


def _inproj_kernel(x_ref, meta_ref, g_ref, w_ref, wab_ref, o_ref, ab_ref, h_ref, hn_ref):
    @pl.when(pl.program_id(1) == 0)
    def _():
        tm = h_ref.shape[0]

        @pl.when(pl.program_id(0) % 2 == 0)
        def _():
            h_ref[0:N_META, :] = meta_ref[...]
            h_ref[N_META:tm, :] = x_ref[0:tm - N_META, :]

        @pl.when(pl.program_id(0) % 2 == 1)
        def _():
            h_ref[...] = x_ref[...]

        x = h_ref[...]
        ms = jnp.mean(x * x, axis=-1, keepdims=True)
        hn_ref[...] = (x * lax.rsqrt(ms + NORM_EPS) * g_ref[...]).astype(MXU_DTYPE)
        ab_ref[...] = _dot(hn_ref[...], wab_ref[...])

    o_ref[...] = _dot(hn_ref[...], w_ref[...]).astype(o_ref.dtype)


def _in_proj(x2d, meta, gain, w_cat, w_ab, seq):
    n_x, d = x2d.shape
    n_real = seq - N_META
    t = n_x // n_real * seq
    tm = seq // 2
    assert seq % (2 * SUBLANES) == 0 and tm >= N_META
    n = w_cat.shape[1]
    x_spec = pl.BlockSpec(
        (pl.Element(tm), pl.Element(d)),
        lambda i, j: (((i // 2) * (n_real // SUBLANES) + (i % 2) * ((tm - N_META) // SUBLANES)) * SUBLANES, 0))
    return pl.pallas_call(
        _inproj_kernel,
        grid=(t // tm, n // PROJ_TN),
        in_specs=[
            x_spec,
            pl.BlockSpec((N_META, d), lambda i, j: (0, 0)),
            pl.BlockSpec((1, d), lambda i, j: (0, 0)),
            pl.BlockSpec((d, PROJ_TN), lambda i, j: (0, j)),
            pl.BlockSpec((d, LANES), lambda i, j: (0, 0)),
        ],
        out_specs=[pl.BlockSpec((tm, PROJ_TN), lambda i, j: (i, j)),
                   pl.BlockSpec((tm, LANES), lambda i, j: (i, 0)),
                   pl.BlockSpec((tm, d), lambda i, j: (i, 0))],
        out_shape=[jax.ShapeDtypeStruct((t, n), PROJ_DTYPE),
                   jax.ShapeDtypeStruct((t, LANES), F32),
                   jax.ShapeDtypeStruct((t, d), F32)],
        scratch_shapes=[pltpu.VMEM((tm, d), MXU_DTYPE)],
        compiler_params=_params(("arbitrary", "arbitrary")),
        name="in_proj",
    )(x2d, meta, gain, w_cat, w_ab)


def _fnet_kernel(x_ref, cs_ref, tw_ref, flip_ref, o_ref, r_ref, *, seq, kp, nat, scale):
    tail = kp - seq
    if tail:
        r_ref[seq:kp, :] = jnp.zeros((tail, F_WIDTH), MXU_DTYPE)
        r_ref[kp + seq:, :] = jnp.zeros((tail, F_WIDTH), MXU_DTYPE)
    cs = cs_ref[...]
    for g in range(N_FGROUPS):
        cols = slice(g * FGROUP_DIM, (g + 1) * FGROUP_DIM)
        p = _dot(x_ref[:, cols].astype(MXU_DTYPE), cs)
        r_ref[0:seq, cols] = p[:, :FGROUP_DIM].astype(MXU_DTYPE)
        r_ref[kp:kp + seq, cols] = p[:, FGROUP_DIM:].astype(MXU_DTYPE)

    a = _dot(tw_ref[:, :kp], r_ref[0:kp, :])
    b = _dot(tw_ref[:, kp:], r_ref[kp:, :])
    o_ref[0:nat, :] = ((a - b) * scale).astype(o_ref.dtype)
    mirrored = ((a + b) * scale).astype(MXU_DTYPE)
    o_ref[nat:, :] = _dot(flip_ref[...], mirrored).astype(o_ref.dtype)


def _fnet(proj3, cs128, twid, flip):
    b, seq, _ = proj3.shape
    nat, kp = twid.shape[0], twid.shape[1] // 2
    scale = 1.0 / math.sqrt(seq * FGROUP_DIM)
    const = lambda shape: pl.BlockSpec(shape, lambda bi: (0, 0))
    return pl.pallas_call(
        functools.partial(_fnet_kernel, seq=seq, kp=kp, nat=nat, scale=scale),
        grid=(b,),
        in_specs=[
            pl.BlockSpec((None, seq, F_WIDTH), lambda bi: (bi, 0, COL_FIN // F_WIDTH)),
            const((FGROUP_DIM, 2 * FGROUP_DIM)), const((nat, 2 * kp)), const((seq - nat, nat)),
        ],
        out_specs=pl.BlockSpec((None, seq, F_WIDTH), lambda bi: (bi, 0, 0)),
        out_shape=jax.ShapeDtypeStruct((b, seq, F_WIDTH), MXU_DTYPE),
        scratch_shapes=[pltpu.VMEM((2 * kp, F_WIDTH), MXU_DTYPE)],
        compiler_params=_params(("arbitrary",)),
        name="fnet",
    )(proj3, cs128, twid, flip)


def _aligned(x, m):
    return x if isinstance(x, int) else pl.multiple_of(x, m)


def _gdn_kernel(q_ref, k_ref, v_ref, z_ref, ab_ref, cwq_ref, cwk_ref, cwv_ref, prm_ref, ong_ref,
                o_ref,
                rawq_ref, rawk_ref, rawv_ref, qs_ref, ks_ref, vs_ref, gsel_ref, grow_ref, gmix_ref,
                qe_ref, mp_ref, nn_ref, dd_ref, osum_ref,
                *, seq, lp, group, n_heads_total):
    step = pl.program_id(0)
    head = jnp.minimum(step, n_heads_total - 1) % N_HEADS
    nc = lp // CHUNK
    padr = lp - seq
    halo = SUBLANES
    row64 = lax.broadcasted_iota(jnp.int32, (CHUNK, LANES), 0)
    lane64 = lax.broadcasted_iota(jnp.int32, (CHUNK, LANES), 1)

    def for_groups(fn, carry=0):
        n_full, rem = divmod(nc, group)
        if n_full:
            carry = lax.fori_loop(0, n_full, lambda i, c: fn([i * group + j for j in range(group)], c), carry)
        if rem:
            carry = fn([n_full * group + j for j in range(rem)], carry)
        return carry

    cur_half = step % 2
    qe_w, mp_w, nn_w, dd_w, osum_w = (r.at[cur_half] for r in (qe_ref, mp_ref, nn_ref, dd_ref, osum_ref))
    qe_r, mp_r, nn_r, dd_r, osum_r = (r.at[1 - cur_half] for r in (qe_ref, mp_ref, nn_ref, dd_ref, osum_ref))

    streams = ((rawq_ref, q_ref, cwq_ref, qs_ref, HEAD_DIM ** -0.5),
               (rawk_ref, k_ref, cwk_ref, ks_ref, 1.0),
               (rawv_ref, v_ref, cwv_ref, vs_ref, None))
    def stage_conv_inputs():
        for raw_ref, src_ref, _, _, _ in streams:
            raw_ref[0:padr + halo, :] = jnp.zeros((padr + halo, LANES), F32)
            raw_ref[padr + halo:padr + halo + seq, :] = src_ref[...].astype(F32)
            raw_ref[lp + halo:lp + 2 * halo, :] = jnp.zeros((halo, LANES), F32)

    def conv_chunk(c):
        base = _aligned(c * CHUNK, CHUNK)
        for raw_ref, _, cw_ref, dst_ref, l2_scale in streams:
            cw = cw_ref[...]
            acc = None
            for j in range(CONV_K):
                off = halo - (CONV_K - 1) // 2 + j
                tap = cw[j:j + 1, :] * raw_ref[pl.ds(base + off, CHUNK), :]
                acc = tap if acc is None else acc + tap
            y = _silu(acc)
            if isinstance(c, int) and c * CHUNK < padr:
                y = jnp.where(row64 + base >= padr, y, 0.0)
            if l2_scale is not None:
                inv = lax.rsqrt(jnp.sum(y * y, axis=-1, keepdims=True) + NORM_EPS)
                y = y * (inv if l2_scale == 1.0 else inv * l2_scale)
            dst_ref[pl.ds(base, CHUNK), :] = y

    raw_ref = rawq_ref

    def stage_gate_inputs():
        raw_ref[0:padr, :] = jnp.zeros((padr, LANES), F32)
        raw_ref[padr:padr + seq, :] = ab_ref[...]

    neg_a = -jnp.exp(prm_ref[0:1, :])
    dt_bias = prm_ref[1:2, :]
    is_g = (lane64 < 8) | ((lane64 >= 16) & (lane64 < 24))
    is_beta = ((lane64 >= 8) & (lane64 < 16)) | ((lane64 >= 24) & (lane64 < 32))
    r128 = lax.broadcasted_iota(jnp.int32, (2 * CHUNK, CHUNK), 0)
    c128 = lax.broadcasted_iota(jnp.int32, (2 * CHUNK, CHUNK), 1)
    tri2 = jnp.where(((r128 < CHUNK) & (r128 >= c128)) | ((r128 >= CHUNK) & (r128 - CHUNK <= c128)),
                     1.0, 0.0).astype(MXU_DTYPE)
    sel_r = lax.broadcasted_iota(jnp.int32, (LANES, 4 * LANES), 0)
    sel_c = lax.broadcasted_iota(jnp.int32, (LANES, 4 * LANES), 1)
    n_gate_cols = 4 * N_HEADS
    sel_packed = jnp.where((sel_r < 3 * n_gate_cols)
                           & (sel_r % n_gate_cols == head + N_HEADS * (sel_c // LANES)), 1.0, 0.0).astype(MXU_DTYPE)
    tri2x3 = jnp.concatenate([tri2, tri2, tri2], axis=1)

    def gate_batch_group(cs, carry):
        bases = [_aligned(c * CHUNK, CHUNK) for c in cs]
        gates, parts = [], []
        for base in bases:
            ab = raw_ref[pl.ds(base, CHUNK), :]
            g = neg_a * jax.nn.softplus(ab + dt_bias)
            beta = jax.nn.sigmoid(ab)
            gt = jnp.where(is_g, g, jnp.where(is_beta, beta, 0.0))
            gt = jnp.where(row64 + base >= padr, gt, 0.0)
            gates.append(gt)
            parts.append(jnp.concatenate(_split3(gt), axis=0))
        cums = [_dot(tri2x3, p) for p in parts]
        for c, base, cm, gt in zip(cs, bases, cums, gates):
            m = jnp.where(lane64 < 8, cm[:CHUNK], jnp.where((lane64 >= 16) & (lane64 < 24), cm[CHUNK:], gt))
            gmix_ref[pl.ds(base, CHUNK), :] = m
            grow_ref[pl.ds(_aligned(c * LANES, LANES), LANES), :] = jnp.concatenate([m, m], axis=0).T
        return carry

    def gate_head_group(cs, carry):
        rows = len(cs) * CHUNK
        base = _aligned(cs[0] * CHUNK, CHUNK)
        hi, mid, lo = _split3(gmix_ref[pl.ds(base, rows), :])
        packed = (hi.astype(F32) + pltpu.roll(mid.astype(F32), n_gate_cols, 1)
                  + pltpu.roll(lo.astype(F32), 2 * n_gate_cols, 1)).astype(MXU_DTYPE)
        gsel_ref[pl.ds(base, rows), :] = _dot(packed, sel_packed)
        return carry

    fwd_half = lane64 < CHUNK
    col64 = lane64 & (CHUNK - 1)
    incl2 = (fwd_half & (row64 >= col64)) | (~fwd_half & (row64 <= col64))
    strict2 = (fwd_half & (row64 > col64)) | (~fwd_half & (row64 < col64))
    eye2 = jnp.where(row64 == col64, 1.0, 0.0).astype(F32)

    def blockdiag(p):
        return jnp.concatenate([jnp.where(fwd_half, p, 0.0), jnp.where(fwd_half, 0.0, p)],
                               axis=0).astype(MXU_DTYPE)

    n_neumann = int(math.log2(CHUNK)) - 2
    n_stages = n_neumann + 6

    def chunk_group(cs, carry, with_scan):
        n = len(cs)
        bases = [_aligned(c * CHUNK, CHUNK) for c in cs]
        cb128 = [_aligned(c * LANES, LANES) for c in cs]
        cb8 = [_aligned(c * SUBLANES, SUBLANES) for c in cs]
        stage = [0]

        def end_of_stage(carry):
            s = stage[0]
            stage[0] += 1
            if with_scan:
                for j in range(n):
                    if j * n_stages // n == s:
                        carry = scan_step(cs[j], carry)
            return carry

        def gate_cols(j):
            gs = gsel_ref[pl.ds(bases[j], CHUNK), :]
            return gs[:, 0:LANES], gs[:, LANES:2 * LANES], gs[:, 2 * LANES:3 * LANES], gs[:, 3 * LANES:]

        kq = []
        for j in range(n):
            k16 = ks_ref[pl.ds(bases[j], CHUNK), :].astype(MXU_DTYPE)
            q16 = qs_ref[pl.ds(bases[j], CHUNK), :].astype(MXU_DTYPE)
            kq.append(_dot_nt(jnp.concatenate([k16, q16], axis=0), jnp.concatenate([k16, k16], axis=0)))
        carry = end_of_stage(carry)

        ts, ps, qkds = [], [], []
        for j in range(n):
            gf, bf, gb, bb = gate_cols(j)
            row_f = grow_ref[pl.ds(cb128[j] + head, 1), :]
            row_b = grow_ref[pl.ds(cb128[j] + 2 * N_HEADS + head, 1), :]
            g_col = jnp.where(fwd_half, gf, gb)
            g_row = jnp.where(fwd_half, row_f, row_b)
            decay = jnp.where(incl2, jnp.exp(jnp.where(incl2, g_col - g_row, 0.0)), 0.0)
            a2 = jnp.where(strict2, kq[j][:CHUNK] * jnp.where(fwd_half, bf, bb) * decay, 0.0)
            qkds.append(jnp.where(incl2, kq[j][CHUNK:] * decay, 0.0))
            ts.append(eye2 - a2)
            ps.append(a2)

        ps = [_dot(p.astype(MXU_DTYPE), blockdiag(p)) for p in ps]
        carry = end_of_stage(carry)
        for _ in range(n_neumann):
            tps = [_dot(jnp.concatenate([t, p], axis=0).astype(MXU_DTYPE), blockdiag(p)) for t, p in zip(ts, ps)]
            carry = end_of_stage(carry)
            ts = [t + tp[:CHUNK] for t, tp in zip(ts, tps)]
            ps = [tp[CHUNK:] for tp in tps]
        ts = [t + _dot(t.astype(MXU_DTYPE), blockdiag(p)) for t, p in zip(ts, ps)]
        carry = end_of_stage(carry)

        uws = []
        for j in range(n):
            gf, bf, gb, bb = gate_cols(j)
            k = ks_ref[pl.ds(bases[j], CHUNK), :]
            v = vs_ref[pl.ds(bases[j], CHUNK), :]
            x_f = jnp.concatenate([v * bf, k * bf * jnp.exp(gf)], axis=1)
            x_b = jnp.concatenate([v * bb, k * bb * jnp.exp(gb)], axis=1)
            uw = _dot(blockdiag(ts[j]), jnp.concatenate([x_f, x_b], axis=0).astype(MXU_DTYPE))
            uws.append(uw.astype(MXU_DTYPE))
        carry = end_of_stage(carry)

        xxs = [_dot(blockdiag(qkds[j]), uws[j]) for j in range(n)]
        carry = end_of_stage(carry)
        for j in range(n):
            gf, _, gb, _ = gate_cols(j)
            k = ks_ref[pl.ds(bases[j], CHUNK), :]
            q = qs_ref[pl.ds(bases[j], CHUNK), :]
            xx = xxs[j]
            osum_w[pl.ds(bases[j], CHUNK), :] = xx[:CHUNK, :LANES] + xx[CHUNK:, :LANES]
            for d, gc in enumerate((gf, gb)):
                rows = slice(d * CHUNK, (d + 1) * CHUNK)
                g_last = gc[CHUNK - 1:CHUNK, :] if d == 0 else gc[0:1, :]
                k_dec = k * jnp.exp(g_last - gc)
                mn = _dot_tn(k_dec.astype(MXU_DTYPE), uws[j][rows])
                qe_w[d, pl.ds(bases[j], CHUNK), :] = (q * jnp.exp(gc) - xx[rows, LANES:]).astype(MXU_DTYPE)
                nn_w[d, pl.ds(cb128[j], HEAD_DIM), :] = mn[:, :LANES]
                mp_w[d, pl.ds(cb128[j], HEAD_DIM), :] = mn[:, LANES:].astype(MXU_DTYPE)
                dd_w[d, pl.ds(cb8[j], SUBLANES), :] = jnp.broadcast_to(jnp.exp(g_last), (SUBLANES, LANES))
        carry = end_of_stage(carry)
        assert stage[0] == n_stages
        return carry

    def scan_step(i, carry):
        new = []
        for d in range(2):
            s = carry[d]
            c = i if d == 0 else nc - 1 - i
            base = _aligned(c * CHUNK, CHUNK)
            cbase = _aligned(c * HEAD_DIM, HEAD_DIM)
            dbase = _aligned(c * SUBLANES, SUBLANES)
            s16 = s.astype(MXU_DTYPE)
            osum_r[pl.ds(base, CHUNK), :] += _dot(qe_r[d, pl.ds(base, CHUNK), :], s16)
            dec = dd_r[d, pl.ds(dbase, 1), :]
            new.append(dec * s - _dot(mp_r[d, pl.ds(cbase, HEAD_DIM), :], s16)
                       + nn_r[d, pl.ds(cbase, HEAD_DIM), :])
        return tuple(new)

    zero_state = jnp.zeros((HEAD_DIM, HEAD_DIM), F32)

    def write_output():
        gain = ong_ref[...]

        def finish(o, z):
            ms = jnp.mean(o * o, axis=-1, keepdims=True)
            return (o * lax.rsqrt(ms + NORM_EPS) * gain * _silu(z.astype(F32))).astype(o_ref.dtype)

        first = CHUNK - padr if padr else 0
        if first:
            o_ref[0:first, :] = finish(osum_r[padr:CHUNK, :], z_ref[0:first, :])

        def out_body(c, carry):
            base = pl.multiple_of(c * CHUNK, CHUNK)
            dst = pl.multiple_of(c * CHUNK - padr, SUBLANES)
            o_ref[pl.ds(dst, CHUNK), :] = finish(osum_r[pl.ds(base, CHUNK), :], z_ref[pl.ds(dst, CHUNK), :])
            return carry

        lax.fori_loop(1 if first else 0, nc, out_body, 0, unroll=8)

    def conv_only(c, carry):
        conv_chunk(c)
        return carry

    @pl.when(step < n_heads_total)
    def _():
        stage_conv_inputs()
        conv_chunk(0)
        lax.fori_loop(1, nc, conv_only, 0, unroll=4)

    @pl.when((step < n_heads_total) & (head == 0))
    def _():
        stage_gate_inputs()
        for_groups(gate_batch_group)

    @pl.when(step < n_heads_total)
    def _():
        for_groups(gate_head_group)

    @pl.when(step == 0)
    def _():
        for_groups(functools.partial(chunk_group, with_scan=False))

    @pl.when((step > 0) & (step < n_heads_total))
    def _():
        for_groups(functools.partial(chunk_group, with_scan=True), (zero_state, zero_state))
        write_output()

    @pl.when(step == n_heads_total)
    def _():
        lax.fori_loop(0, nc, scan_step, (zero_state, zero_state))
        write_output()


def _gdn(proj3, ab3, conv_w, gate_prm, out_norm_g):
    b, seq, _ = proj3.shape
    lp = -(-seq // CHUNK) * CHUNK
    nc = lp // CHUNK
    n_total = b * N_HEADS
    cur = lambda i: jnp.minimum(i, n_total - 1)
    prev = lambda i: jnp.maximum(i - 1, 0)
    col = lambda off: pl.BlockSpec((None, seq, LANES), lambda i: (cur(i) // N_HEADS, 0, off + cur(i) % N_HEADS))
    cw = lambda off: pl.BlockSpec((CONV_K, LANES), lambda i: (0, off + cur(i) % N_HEADS))
    return pl.pallas_call(
        functools.partial(_gdn_kernel, seq=seq, lp=lp, group=GDN_GROUP, n_heads_total=n_total),
        grid=(n_total + 1,),
        in_specs=[
            col(COL_QKV // LANES), col(COL_QKV // LANES + N_HEADS), col(COL_QKV // LANES + 2 * N_HEADS),
            pl.BlockSpec((None, seq, LANES),
                         lambda i: (prev(i) // N_HEADS, 0, COL_Z // LANES + prev(i) % N_HEADS)),
            pl.BlockSpec((None, seq, LANES), lambda i: (cur(i) // N_HEADS, 0, 0)),
            cw(0), cw(N_HEADS), cw(2 * N_HEADS),
            pl.BlockSpec((SUBLANES, LANES), lambda i: (0, 0)),
            pl.BlockSpec((1, LANES), lambda i: (0, 0)),
        ],
        out_specs=pl.BlockSpec((None, seq, LANES), lambda i: (prev(i) // N_HEADS, 0, prev(i) % N_HEADS)),
        out_shape=jax.ShapeDtypeStruct((b, seq, N_HEADS * HEAD_DIM), MXU_DTYPE),
        scratch_shapes=[
            pltpu.VMEM((lp + 2 * SUBLANES, LANES), F32),
            pltpu.VMEM((lp + 2 * SUBLANES, LANES), F32),
            pltpu.VMEM((lp + 2 * SUBLANES, LANES), F32),
            pltpu.VMEM((lp, LANES), F32),
            pltpu.VMEM((lp, LANES), F32),
            pltpu.VMEM((lp, LANES), F32),
            pltpu.VMEM((lp, 4 * LANES), F32),
            pltpu.VMEM((nc * LANES, LANES), F32),
            pltpu.VMEM((lp, LANES), F32),
            pltpu.VMEM((2, 2, lp, LANES), MXU_DTYPE),
            pltpu.VMEM((2, 2, nc * HEAD_DIM, LANES), MXU_DTYPE),
            pltpu.VMEM((2, 2, nc * HEAD_DIM, LANES), F32),
            pltpu.VMEM((2, 2, nc * SUBLANES, LANES), F32),
            pltpu.VMEM((2, lp, LANES), F32),
        ],
        compiler_params=_params(("arbitrary",)),
        name="gdn",
    )(proj3, proj3, proj3, proj3, ab3, conv_w, conv_w, conv_w, gate_prm, out_norm_g)


def _merge_kernel(h_ref, og_ref, ym_ref, gf_ref, gd_ref, wf_ref, wd_ref, wo_ref, n2_ref, wr_ref, br_ref,
                  h1_ref, hn_ref, ridx_ref, rw_ref, cnt_ref):
    @pl.when(pl.program_id(0) == 0)
    def _():
        cnt_ref[...] = jnp.zeros_like(cnt_ref)

    y_f = _dot(ym_ref[...], wf_ref[...])
    y_d = _dot(og_ref[...], wd_ref[...])
    merged = (jax.nn.sigmoid(gf_ref[...].astype(F32)) * y_f
              + jax.nn.sigmoid(gd_ref[...].astype(F32)) * y_d)
    h1 = h_ref[...] + _dot(merged.astype(MXU_DTYPE), wo_ref[...])
    h1_ref[...] = h1
    ms = jnp.mean(h1 * h1, axis=-1, keepdims=True)
    hn = h1 * lax.rsqrt(ms + NORM_EPS) * n2_ref[...]
    hn_ref[...] = _pack_rows(hn)

    x_hi = hn.astype(MXU_DTYPE)
    x_lo = (hn - x_hi.astype(F32)).astype(MXU_DTYPE)
    w = wr_ref[...]
    w_hi = w.astype(MXU_DTYPE)
    w_lo = (w - w_hi.astype(F32)).astype(MXU_DTYPE)
    logits = _dot(x_hi, w_hi) + _dot(x_hi, w_lo) + _dot(x_lo, w_hi) + br_ref[...]

    tm = logits.shape[0]
    lane = lax.broadcasted_iota(jnp.int32, (tm, LANES), 1)
    neg_inf = -jnp.inf
    is_group = lane < N_GROUPS
    gl = jnp.where(is_group, logits, neg_inf)
    ge = jnp.exp(gl - jnp.max(gl, axis=-1, keepdims=True))
    gp = ge / jnp.sum(ge, axis=-1, keepdims=True)
    p_group = jnp.max(gp, axis=-1, keepdims=True)
    g_idx = jnp.min(jnp.where(is_group & (gp == p_group), lane, LANES), axis=-1, keepdims=True)

    in_group = (lane >= N_GROUPS) & (lane < N_GROUPS + N_EXPERTS) & (((lane - N_GROUPS) >> 3) == g_idx)
    el = jnp.where(in_group, logits, neg_inf)
    ee = jnp.exp(el - jnp.max(el, axis=-1, keepdims=True))
    ep = ee / jnp.sum(ee, axis=-1, keepdims=True)
    v1 = jnp.max(jnp.where(in_group, ep, -1.0), axis=-1, keepdims=True)
    i1 = jnp.min(jnp.where(in_group & (ep == v1), lane, LANES), axis=-1, keepdims=True)
    rest = in_group & (lane != i1)
    v2 = jnp.max(jnp.where(rest, ep, -1.0), axis=-1, keepdims=True)
    i2 = jnp.min(jnp.where(rest & (ep == v2), lane, LANES), axis=-1, keepdims=True)
    denom = v1 + v2
    w1 = p_group * (v1 / denom)
    w2 = p_group * (v2 / denom)
    e1 = i1 - N_GROUPS
    e2 = i2 - N_GROUPS
    ridx_ref[...] = jnp.where(lane == 0, e1, jnp.where(lane == 1, e2, 0))
    rw_ref[...] = jnp.where(lane == 0, w1, jnp.where(lane == 1, w2, 0.0))
    onehots = jnp.where((lane == e1) | (lane == e2), 1.0, 0.0)
    cnt_ref[...] += jnp.broadcast_to(jnp.sum(onehots, axis=0, keepdims=True), cnt_ref.shape)


def _merge(h2d, og2d, ymix2d, proj2d, w_fourier, w_delta, w_out, norm2_g, w_router, b_router, tm):
    t, d = h2d.shape
    row = lambda w: pl.BlockSpec((tm, w), lambda i: (i, 0))
    const = lambda shape: pl.BlockSpec(shape, lambda i: (0, 0))
    return pl.pallas_call(
        _merge_kernel,
        grid=(t // tm,),
        in_specs=[
            row(d), row(d), row(F_WIDTH),
            pl.BlockSpec((tm, d), lambda i: (i, COL_GATE_F // d)),
            pl.BlockSpec((tm, d), lambda i: (i, COL_GATE_D // d)),
            const((F_WIDTH, d)), const((d, d)), const((d, d)), const((1, d)),
            const((d, LANES)), const((1, LANES)),
        ],
        out_specs=[row(d), row(d // 2), row(LANES), row(LANES), const((SUBLANES, LANES))],
        out_shape=[
            jax.ShapeDtypeStruct((t, d), F32),
            jax.ShapeDtypeStruct((t, d // 2), jnp.uint32),
            jax.ShapeDtypeStruct((t, LANES), jnp.int32),
            jax.ShapeDtypeStruct((t, LANES), F32),
            jax.ShapeDtypeStruct((SUBLANES, LANES), F32),
        ],
        compiler_params=_params(("arbitrary",)),
        name="merge",
    )(h2d, og2d, ymix2d, proj2d, proj2d, w_fourier, w_delta, w_out, norm2_g, w_router, b_router)


def _plan_kernel(ridx_ref, cnt_ref, dest_ref, seg_ref, base_ref, run_ref):
    tt = ridx_ref.shape[0]

    @pl.when(pl.program_id(0) == 0)
    def _():
        cnt = cnt_ref[...].astype(jnp.int32)
        n_blk = (cnt + (EXPERT_BLOCK - 1)) // EXPERT_BLOCK
        padded = (n_blk * EXPERT_BLOCK).astype(F32)
        r = lax.broadcasted_iota(jnp.int32, (LANES, LANES), 0)
        c = lax.broadcasted_iota(jnp.int32, (LANES, LANES), 1)
        before = jnp.where(r < c, 1.0, 0.0).astype(MXU_DTYPE)
        start = _exact_right(padded, before)
        base_ref[...] = start
        run_ref[...] = jnp.zeros_like(run_ref)
        seg_ref[0:SUBLANES, :] = start.astype(jnp.int32)
        seg_ref[SUBLANES:2 * SUBLANES, :] = n_blk
        seg_ref[2 * SUBLANES:, :] = cnt

    lane = lax.broadcasted_iota(jnp.int32, (tt, LANES), 1)
    ridx = ridx_ref[...]
    oh0 = jnp.where(lane == ridx[:, 0:1], 1.0, 0.0)
    oh1 = jnp.where(lane == ridx[:, 1:2], 1.0, 0.0)
    r = lax.broadcasted_iota(jnp.int32, (tt, tt), 0)
    c = lax.broadcasted_iota(jnp.int32, (tt, tt), 1)
    earlier = jnp.where(c < r, 1.0, 0.0).astype(MXU_DTYPE)
    prefix = _dot(earlier, jnp.concatenate([oh0, oh1], axis=1).astype(MXU_DTYPE))
    c0 = jnp.sum(oh0, axis=0, keepdims=True)
    c1 = jnp.sum(oh1, axis=0, keepdims=True)
    first = base_ref[0:1, :] + run_ref[0:1, :]
    d0 = jnp.sum(oh0 * (prefix[:, :LANES] + first), axis=-1, keepdims=True)
    d1 = jnp.sum(oh1 * (prefix[:, LANES:] + first + c0), axis=-1, keepdims=True)
    run_ref[...] += jnp.broadcast_to(c0 + c1, run_ref.shape)
    dest_ref[...] = jnp.where(lane == 0, d0, jnp.where(lane == 1, d1, 0.0)).astype(jnp.int32)


def _plan(ridx, cnt, tt):
    t = ridx.shape[0]
    return pl.pallas_call(
        _plan_kernel,
        grid=(t // tt,),
        in_specs=[pl.BlockSpec((tt, LANES), lambda i: (i, 0)),
                  pl.BlockSpec((SUBLANES, LANES), lambda i: (0, 0))],
        out_specs=[pl.BlockSpec((tt, LANES), lambda i: (i, 0)),
                   pl.BlockSpec((3 * SUBLANES, LANES), lambda i: (0, 0))],
        out_shape=[jax.ShapeDtypeStruct((t, LANES), jnp.int32),
                   jax.ShapeDtypeStruct((3 * SUBLANES, LANES), jnp.int32)],
        scratch_shapes=[pltpu.VMEM((SUBLANES, LANES), F32), pltpu.VMEM((SUBLANES, LANES), F32)],
        compiler_params=_params(("arbitrary",)),
        name="plan",
    )(ridx, cnt)


def _scatter_kernel(dest_ref, hn_ref, xb_in_ref, xb_ref, sem, *, ts):
    del xb_in_ref

    def issue(t, carry):
        src = hn_ref.at[pl.ds(t, 1)]
        for k in range(TOP_K):
            pltpu.make_async_copy(src, xb_ref.at[pl.ds(dest_ref[0, TOP_K * t + k], 1)], sem).start(priority=k % 2)
        return carry

    lax.fori_loop(0, ts, issue, 0, unroll=DMA_UNROLL)
    for _ in range(TOP_K):
        pltpu.make_async_copy(hn_ref, xb_ref.at[pl.ds(0, ts)], sem).wait()


def _scatter(dest_flat, hn2d, xb_zero, ts):
    t, d = hn2d.shape
    return pl.pallas_call(
        functools.partial(_scatter_kernel, ts=ts),
        grid=(t // ts,),
        in_specs=[pl.BlockSpec((None, 1, TOP_K * ts), lambda i: (i, 0, 0), memory_space=pltpu.SMEM),
                  pl.BlockSpec((ts, d), lambda i: (i, 0)),
                  pl.BlockSpec(memory_space=pl.ANY)],
        out_specs=pl.BlockSpec(memory_space=pl.ANY),
        out_shape=jax.ShapeDtypeStruct(xb_zero.shape, xb_zero.dtype),
        scratch_shapes=[pltpu.SemaphoreType.DMA(())],
        input_output_aliases={2: 0},
        compiler_params=_params(("arbitrary",)),
        name="scatter",
    )(dest_flat, hn2d, xb_zero)


def _expert_kernel(start_ref, nblk_ref, xb_ref, wg_ref, wu_ref, wd_ref, yb_ref,
                   wg16_ref, wu16_ref, wd16_ref, xbuf_ref, ybuf_ref, sem_in, sem_out):
    expert = pl.program_id(0)
    last = pl.num_programs(0) - 1
    n = nblk_ref[expert]
    first = start_ref[expert] // EXPERT_BLOCK
    total = start_ref[last] // EXPERT_BLOCK + nblk_ref[last]

    def rows(g):
        return pl.ds(pl.multiple_of(g * EXPERT_BLOCK, EXPERT_BLOCK), EXPERT_BLOCK)

    def x_copy(g):
        slot = g % EXPERT_X_SLOTS
        return pltpu.make_async_copy(xb_ref.at[rows(g)], xbuf_ref.at[slot], sem_in.at[slot])

    def y_copy(g):
        slot = g % EXPERT_Y_SLOTS
        return pltpu.make_async_copy(ybuf_ref.at[slot], yb_ref.at[rows(g)], sem_out.at[slot])

    @pl.when(n > 0)
    def _():
        @pl.when(first == 0)
        def _():
            for j in range(EXPERT_X_SLOTS - 1):
                @pl.when(j < total)
                def _():
                    x_copy(j).start()

        wg16_ref[...] = wg_ref[...].astype(MXU_DTYPE)
        wu16_ref[...] = wu_ref[...].astype(MXU_DTYPE)
        wd16_ref[...] = wd_ref[...].astype(MXU_DTYPE)

        def body(i, carry):
            g = first + i

            @pl.when(g + (EXPERT_X_SLOTS - 1) < total)
            def _():
                x_copy(g + (EXPERT_X_SLOTS - 1)).start()

            x_copy(g).wait()

            @pl.when(g >= EXPERT_Y_SLOTS)
            def _():
                y_copy(g - EXPERT_Y_SLOTS).wait()

            x = _unpack_rows(xbuf_ref[g % EXPERT_X_SLOTS]).astype(MXU_DTYPE)
            gate = _dot(x, wg16_ref[...])
            up = _dot(x, wu16_ref[...])
            ybuf_ref[g % EXPERT_Y_SLOTS] = _pack_rows(_dot((_silu(gate) * up).astype(MXU_DTYPE), wd16_ref[...]))
            y_copy(g).start()
            return carry

        lax.fori_loop(0, n, body, 0)

    @pl.when(expert == last)
    def _():
        for j in range(EXPERT_Y_SLOTS):
            @pl.when(total > j)
            def _():
                y_copy(total - 1 - j).wait()


def _experts(seg_start, seg_blocks, xb, w_gate, w_up, w_down):
    n_slots, row_words = xb.shape
    n_exp, d, de = w_gate.shape
    grid_spec = pltpu.PrefetchScalarGridSpec(
        num_scalar_prefetch=2,
        grid=(n_exp,),
        in_specs=[
            pl.BlockSpec(memory_space=pl.ANY),
            pl.BlockSpec((None, d, de), lambda e, st, nb: (e, 0, 0)),
            pl.BlockSpec((None, d, de), lambda e, st, nb: (e, 0, 0)),
            pl.BlockSpec((None, de, d), lambda e, st, nb: (e, 0, 0)),
        ],
        out_specs=pl.BlockSpec(memory_space=pl.ANY),
        scratch_shapes=[
            pltpu.VMEM((d, de), MXU_DTYPE), pltpu.VMEM((d, de), MXU_DTYPE), pltpu.VMEM((de, d), MXU_DTYPE),
            pltpu.VMEM((EXPERT_X_SLOTS, EXPERT_BLOCK, row_words), jnp.uint32),
            pltpu.VMEM((EXPERT_Y_SLOTS, EXPERT_BLOCK, row_words), jnp.uint32),
            pltpu.SemaphoreType.DMA((EXPERT_X_SLOTS,)), pltpu.SemaphoreType.DMA((EXPERT_Y_SLOTS,)),
        ],
    )
    return pl.pallas_call(
        _expert_kernel,
        grid_spec=grid_spec,
        out_shape=jax.ShapeDtypeStruct((n_slots, row_words), jnp.uint32),
        input_output_aliases={2: 0},
        compiler_params=_params(("arbitrary",)),
        name="experts",
    )(seg_start, seg_blocks, xb, w_gate, w_up, w_down)


def _combine_kernel(dest_ref, yb_ref, h1_ref, rw_ref, fg_ref, o_ref, buf_ref, res_ref, sem, out_sem,
                    *, tc, groups_per_batch):
    step = pl.program_id(0)
    n_tiles = pl.num_programs(0) - 1
    n_grp = tc // N_META
    cur = step % 2
    prev = 1 - cur

    def output_copies(tile, action):
        slot = tile % 2
        first = tile * n_grp
        first_in_batch = first % groups_per_batch
        has_meta = (first_in_batch == 0) | (first_in_batch + n_grp > groups_per_batch)

        def copy_groups(q, n):
            grp = first + q
            dst = pl.multiple_of((grp - grp // groups_per_batch - 1) * N_META, N_META)
            copy = pltpu.make_async_copy(res_ref.at[slot, pl.ds(q * N_META, n * N_META)],
                                         o_ref.at[pl.ds(dst, n * N_META)], out_sem.at[slot])
            copy.start() if action == "start" else copy.wait()

        @pl.when(jnp.logical_not(has_meta))
        def _():
            copy_groups(0, n_grp)

        @pl.when(has_meta)
        def _():
            for q in range(n_grp):
                @pl.when((first + q) % groups_per_batch != 0)
                def _():
                    copy_groups(q, 1)

    def request_rows(i):
        for u in range(SUBLANES):
            t = i * SUBLANES + u
            for k in range(TOP_K):
                pltpu.make_async_copy(yb_ref.at[pl.ds(dest_ref[0, TOP_K * t + k], 1)],
                                      buf_ref.at[cur, k, pl.ds(t, 1)], sem.at[cur]).start(priority=k % 2)

    def finish_rows(i):
        rows = pl.ds(pl.multiple_of(i * SUBLANES, SUBLANES), SUBLANES)
        rw = rw_ref[rows, :]
        h2 = h1_ref[rows, :] + (_unpack_rows(buf_ref[prev, 0, rows, :]) * rw[:, 0:1]
                                + _unpack_rows(buf_ref[prev, 1, rows, :]) * rw[:, 1:2])
        ms = jnp.mean(h2 * h2, axis=-1, keepdims=True)
        res_ref[prev, rows, :] = h2 * lax.rsqrt(ms + NORM_EPS) * fg_ref[...]

    def begin_finish():
        for k in range(TOP_K):
            pltpu.make_async_copy(yb_ref.at[pl.ds(0, tc)], buf_ref.at[prev, k], sem.at[prev]).wait()

        @pl.when(step >= 3)
        def _():
            output_copies(step - 3, "wait")

    def both(i, carry):
        request_rows(i)
        finish_rows(i)
        return carry

    def request_only(i, carry):
        request_rows(i)
        return carry

    def finish_only(i, carry):
        finish_rows(i)
        return carry

    @pl.when(step == 0)
    def _():
        lax.fori_loop(0, tc // SUBLANES, request_only, 0)

    @pl.when((step > 0) & (step < n_tiles))
    def _():
        begin_finish()
        lax.fori_loop(0, tc // SUBLANES, both, 0, unroll=8)
        output_copies(step - 1, "start")

    @pl.when(step == n_tiles)
    def _():
        begin_finish()
        lax.fori_loop(0, tc // SUBLANES, finish_only, 0, unroll=4)
        output_copies(step - 1, "start")

        @pl.when(step >= 2)
        def _():
            output_copies(step - 2, "wait")

        output_copies(step - 1, "wait")


def _combine(dest_flat, yb, h1, rw, final_g, tc, seq):
    t, d = h1.shape
    assert tc % N_META == 0 and seq % N_META == 0
    n_out = t - (t // seq) * N_META
    n_tiles = t // tc
    nxt = lambda i: jnp.minimum(i, n_tiles - 1)
    fin = lambda i: jnp.maximum(i - 1, 0)
    return pl.pallas_call(
        functools.partial(_combine_kernel, tc=tc, groups_per_batch=seq // N_META),
        grid=(n_tiles + 1,),
        in_specs=[pl.BlockSpec((None, 1, TOP_K * tc), lambda i: (nxt(i), 0, 0), memory_space=pltpu.SMEM),
                  pl.BlockSpec(memory_space=pl.ANY),
                  pl.BlockSpec((tc, d), lambda i: (fin(i), 0)),
                  pl.BlockSpec((tc, LANES), lambda i: (fin(i), 0)),
                  pl.BlockSpec((1, d), lambda i: (0, 0))],
        out_specs=pl.BlockSpec(memory_space=pl.ANY),
        out_shape=jax.ShapeDtypeStruct((n_out, d), F32),
        scratch_shapes=[pltpu.VMEM((2, TOP_K, tc, d // 2), jnp.uint32), pltpu.VMEM((2, tc, d), F32),
                        pltpu.SemaphoreType.DMA((2,)), pltpu.SemaphoreType.DMA((2,))],
        compiler_params=_params(("arbitrary",)),
        name="combine",
    )(dest_flat, yb, h1, rw, final_g)


def _tile(total, candidates):
    for c in candidates:
        if total % c == 0:
            return c
    raise ValueError(f"no tile for {total}")


def _dft_tables(seq):
    kp = -(-seq // LANES) * LANES
    nat = -(-(seq // 2 + 1) // 16) * 16
    mir = seq - nat
    c = np.arange(FGROUP_DIM, dtype=np.int64)
    ang_c = (2.0 * np.pi / FGROUP_DIM) * ((c[:, None] * c[None, :]) % FGROUP_DIM)
    cs128 = np.concatenate([np.cos(ang_c), np.sin(ang_c)], axis=1)
    k = np.arange(nat, dtype=np.int64)
    n = np.arange(seq, dtype=np.int64)
    ang_l = (2.0 * np.pi / seq) * ((k[:, None] * n[None, :]) % seq)
    twid = np.zeros((nat, 2 * kp), np.float32)
    twid[:, :seq] = np.cos(ang_l)
    twid[:, kp:kp + seq] = np.sin(ang_l)
    flip = np.zeros((mir, nat), np.float32)
    flip[np.arange(mir), mir - np.arange(mir)] = 1.0
    as_const = lambda t: jnp.asarray(t.astype(MXU_DTYPE))
    return as_const(cs128), as_const(twid), as_const(flip)


def kernel(x, meta_tokens, norm1_g, w_in, conv_w, a_log_fwd, dt_bias_fwd, a_log_bwd, dt_bias_bwd, out_norm_g, w_fourier, w_delta, w_out, norm2_g, w_router_group, b_router_group, w_router_expert, b_router_expert, w_gate_e, w_up_e, w_down_e, final_norm_g):
    bsz, n_real, d = x.shape
    seq = N_META + n_real
    n_tok = bsz * seq
    assert (CHUNK - N_META) % CHUNK + seq == -(-seq // CHUNK) * CHUNK, "meta tokens must fill the tail of chunk 0"
    assert w_in.shape[0] == 1, "single trunk layer"
    layer = 0


    w = w_in[layer]
    o_f, o_qkv, o_z, o_ab, o_gf, o_gd = 0, F_WIDTH, F_WIDTH + 3 * QK_WIDTH, F_WIDTH + 4 * QK_WIDTH, \
        F_WIDTH + 4 * QK_WIDTH + 4 * N_HEADS, F_WIDTH + 4 * QK_WIDTH + 4 * N_HEADS + d
    w_cat = jnp.concatenate(
        [w[:, o_qkv:o_z], w[:, o_z:o_ab], w[:, o_gf:o_gd], w[:, o_gd:o_gd + d], w[:, o_f:o_qkv]],
        axis=1).astype(MXU_DTYPE)
    w_ab = jnp.pad(w[:, o_ab:o_gf], ((0, 0), (0, LANES - 4 * N_HEADS))).astype(MXU_DTYPE)

    proj, ab, h = _in_proj(x.reshape(bsz * n_real, d), meta_tokens.astype(x.dtype), norm1_g[layer][None, :],
                           w_cat, w_ab, seq)
    proj3 = proj.reshape(bsz, seq, PROJ_WIDTH)

    ymix = _fnet(proj3, *_dft_tables(seq))

    gate_prm = jnp.zeros((SUBLANES, LANES), F32)
    gate_prm = gate_prm.at[0, 0:N_HEADS].set(a_log_fwd[layer]).at[0, 2 * N_HEADS:3 * N_HEADS].set(a_log_bwd[layer])
    gate_prm = gate_prm.at[1, 0:N_HEADS].set(dt_bias_fwd[layer]).at[1, 2 * N_HEADS:3 * N_HEADS].set(dt_bias_bwd[layer])
    og = _gdn(proj3, ab.reshape(bsz, seq, LANES), conv_w[layer], gate_prm, out_norm_g[layer][None, :])

    w_router = jnp.zeros((d, LANES), F32)
    w_router = w_router.at[:, 0:N_GROUPS].set(w_router_group[layer])
    w_router = w_router.at[:, N_GROUPS:N_GROUPS + N_EXPERTS].set(w_router_expert[layer])
    b_router = jnp.zeros((1, LANES), F32)
    b_router = b_router.at[0, 0:N_GROUPS].set(b_router_group[layer])
    b_router = b_router.at[0, N_GROUPS:N_GROUPS + N_EXPERTS].set(b_router_expert[layer])
    tt = _tile(n_tok, (384, 128))
    h1, hn, ridx, rw, cnt = _merge(
        h, og.reshape(n_tok, d), ymix.reshape(n_tok, F_WIDTH), proj,
        w_fourier[layer].astype(MXU_DTYPE), w_delta[layer].astype(MXU_DTYPE), w_out[layer].astype(MXU_DTYPE),
        norm2_g[layer][None, :], w_router, b_router, _tile(n_tok, (688, 384, 128)))

    n_assign = n_tok * TOP_K
    n_blocks = -(-n_assign // EXPERT_BLOCK) + N_EXPERTS
    dest, seg = _plan(ridx, cnt, tt)
    dest_flat = dest[:, :TOP_K].reshape(n_tok // tt, 1, TOP_K * tt)
    seg_start = seg[0, :N_EXPERTS]
    seg_blocks = seg[SUBLANES, :N_EXPERTS]

    xb = _scatter(dest_flat, hn, jnp.zeros((n_blocks * EXPERT_BLOCK, d // 2), jnp.uint32), tt)
    yb = _experts(seg_start, seg_blocks, xb, w_gate_e[layer], w_up_e[layer], w_down_e[layer])
    out = _combine(dest_flat, yb, h1, rw, final_norm_g[None, :], tt, seq)
    return out.reshape(bsz, n_real, d)
```

```python
import functools
import math

import jax
import jax.numpy as jnp
import numpy as np
from jax import lax
from jax.experimental import pallas as pl
from jax.experimental.pallas import tpu as pltpu

F32 = jnp.float32
MXU_DTYPE = jnp.bfloat16

N_META = 16
CHUNK = 64
N_FGROUPS = 4
FGROUP_DIM = 128
F_WIDTH = N_FGROUPS * FGROUP_DIM
N_HEADS = 8
HEAD_DIM = 128
QK_WIDTH = N_HEADS * HEAD_DIM
CONV_K = 5
N_GROUPS = 8
EXPERTS_PER_GROUP = 8
N_EXPERTS = N_GROUPS * EXPERTS_PER_GROUP
TOP_K = 2
EXPERT_BLOCK = 256
NORM_EPS = 1e-6
LANES = 128
SUBLANES = 8
VMEM_LIMIT = 52 * 1024 * 1024
GDN_GROUP = 11
DMA_UNROLL = 8
EXPERT_X_SLOTS = 4
EXPERT_Y_SLOTS = 3

COL_QKV = 0
COL_Z = 3 * QK_WIDTH
COL_GATE_F = COL_Z + QK_WIDTH
COL_GATE_D = COL_GATE_F + 1024
COL_FIN = COL_GATE_D + 1024
PROJ_WIDTH = COL_FIN + F_WIDTH
PROJ_TN = 3328
PROJ_DTYPE = jnp.bfloat16


def _dot(a, b):
    return jnp.dot(a, b, preferred_element_type=F32)


def _dot_nt(a, b):
    return lax.dot_general(a, b, (((1,), (1,)), ((), ())), preferred_element_type=F32)


def _dot_tn(a, b):
    return lax.dot_general(a, b, (((0,), (0,)), ((), ())), preferred_element_type=F32)


def _split3(x):
    hi = x.astype(MXU_DTYPE)
    r1 = x - hi.astype(F32)
    mid = r1.astype(MXU_DTYPE)
    lo = (r1 - mid.astype(F32)).astype(MXU_DTYPE)
    return hi, mid, lo


def _exact_left(sel, x):
    hi, mid, lo = _split3(x)
    return _dot(sel, hi) + _dot(sel, mid) + _dot(sel, lo)


def _exact_right(x, sel):
    hi, mid, lo = _split3(x)
    return _dot(hi, sel) + _dot(mid, sel) + _dot(lo, sel)


def _pack_rows(x):
    half = x.shape[1] // 2
    lo = lax.bitcast_convert_type(x[:, :half].astype(jnp.bfloat16).astype(F32), jnp.uint32)
    hi = lax.bitcast_convert_type(x[:, half:].astype(jnp.bfloat16).astype(F32), jnp.uint32)
    return (lo >> 16) | hi


def _unpack_rows(u):
    lo = lax.bitcast_convert_type(u << 16, F32)
    hi = lax.bitcast_convert_type(u & jnp.uint32(0xFFFF0000), F32)
    return jnp.concatenate([lo, hi], axis=1)


def _silu(x):
    return x * jax.nn.sigmoid(x)


def _params(sem):
    return pltpu.CompilerParams(dimension_semantics=sem, vmem_limit_bytes=VMEM_LIMIT)


def _inproj_kernel(x_ref, meta_ref, g_ref, w_ref, wab_ref, o_ref, ab_ref, h_ref, hn_ref):
    @pl.when(pl.program_id(1) == 0)
    def _():
        tm = h_ref.shape[0]

        @pl.when(pl.program_id(0) % 2 == 0)
        def _():
            h_ref[0:N_META, :] = meta_ref[...]
            h_ref[N_META:tm, :] = x_ref[0:tm - N_META, :]

        @pl.when(pl.program_id(0) % 2 == 1)
        def _():
            h_ref[...] = x_ref[...]

        x = h_ref[...]
        ms = jnp.mean(x * x, axis=-1, keepdims=True)
        hn_ref[...] = (x * lax.rsqrt(ms + NORM_EPS) * g_ref[...]).astype(MXU_DTYPE)
        ab_ref[...] = _dot(hn_ref[...], wab_ref[...])

    o_ref[...] = _dot(hn_ref[...], w_ref[...]).astype(o_ref.dtype)


def _in_proj(x2d, meta, gain, w_cat, w_ab, seq):
    n_x, d = x2d.shape
    n_real = seq - N_META
    t = n_x // n_real * seq
    tm = seq // 2
    assert seq % (2 * SUBLANES) == 0 and tm >= N_META
    n = w_cat.shape[1]
    x_spec = pl.BlockSpec(
        (pl.Element(tm), pl.Element(d)),
        lambda i, j: (((i // 2) * (n_real // SUBLANES) + (i % 2) * ((tm - N_META) // SUBLANES)) * SUBLANES, 0))
    return pl.pallas_call(
        _inproj_kernel,
        grid=(t // tm, n // PROJ_TN),
        in_specs=[
            x_spec,
            pl.BlockSpec((N_META, d), lambda i, j: (0, 0)),
            pl.BlockSpec((1, d), lambda i, j: (0, 0)),
            pl.BlockSpec((d, PROJ_TN), lambda i, j: (0, j)),
            pl.BlockSpec((d, LANES), lambda i, j: (0, 0)),
        ],
        out_specs=[pl.BlockSpec((tm, PROJ_TN), lambda i, j: (i, j)),
                   pl.BlockSpec((tm, LANES), lambda i, j: (i, 0)),
                   pl.BlockSpec((tm, d), lambda i, j: (i, 0))],
        out_shape=[jax.ShapeDtypeStruct((t, n), PROJ_DTYPE),
                   jax.ShapeDtypeStruct((t, LANES), F32),
                   jax.ShapeDtypeStruct((t, d), F32)],
        scratch_shapes=[pltpu.VMEM((tm, d), MXU_DTYPE)],
        compiler_params=_params(("arbitrary", "arbitrary")),
        name="in_proj",
    )(x2d, meta, gain, w_cat, w_ab)


def _fnet_kernel(x_ref, cs_ref, tw_ref, flip_ref, o_ref, r_ref, *, seq, kp, nat, scale):
    tail = kp - seq
    if tail:
        r_ref[seq:kp, :] = jnp.zeros((tail, F_WIDTH), MXU_DTYPE)
        r_ref[kp + seq:, :] = jnp.zeros((tail, F_WIDTH), MXU_DTYPE)
    cs = cs_ref[...]
    for g in range(N_FGROUPS):
        cols = slice(g * FGROUP_DIM, (g + 1) * FGROUP_DIM)
        p = _dot(x_ref[:, cols].astype(MXU_DTYPE), cs)
        r_ref[0:seq, cols] = p[:, :FGROUP_DIM].astype(MXU_DTYPE)
        r_ref[kp:kp + seq, cols] = p[:, FGROUP_DIM:].astype(MXU_DTYPE)

    a = _dot(tw_ref[:, :kp], r_ref[0:kp, :])
    b = _dot(tw_ref[:, kp:], r_ref[kp:, :])
    o_ref[0:nat, :] = ((a - b) * scale).astype(o_ref.dtype)
    mirrored = ((a + b) * scale).astype(MXU_DTYPE)
    o_ref[nat:, :] = _dot(flip_ref[...], mirrored).astype(o_ref.dtype)


def _fnet(proj3, cs128, twid, flip):
    b, seq, _ = proj3.shape
    nat, kp = twid.shape[0], twid.shape[1] // 2
    scale = 1.0 / math.sqrt(seq * FGROUP_DIM)
    const = lambda shape: pl.BlockSpec(shape, lambda bi: (0, 0))
    return pl.pallas_call(
        functools.partial(_fnet_kernel, seq=seq, kp=kp, nat=nat, scale=scale),
        grid=(b,),
        in_specs=[
            pl.BlockSpec((None, seq, F_WIDTH), lambda bi: (bi, 0, COL_FIN // F_WIDTH)),
            const((FGROUP_DIM, 2 * FGROUP_DIM)), const((nat, 2 * kp)), const((seq - nat, nat)),
        ],
        out_specs=pl.BlockSpec((None, seq, F_WIDTH), lambda bi: (bi, 0, 0)),
        out_shape=jax.ShapeDtypeStruct((b, seq, F_WIDTH), MXU_DTYPE),
        scratch_shapes=[pltpu.VMEM((2 * kp, F_WIDTH), MXU_DTYPE)],
        compiler_params=_params(("arbitrary",)),
        name="fnet",
    )(proj3, cs128, twid, flip)


def _aligned(x, m):
    return x if isinstance(x, int) else pl.multiple_of(x, m)


def _gdn_kernel(q_ref, k_ref, v_ref, z_ref, ab_ref, cwq_ref, cwk_ref, cwv_ref, prm_ref, ong_ref,
                o_ref,
                rawq_ref, rawk_ref, rawv_ref, qs_ref, ks_ref, vs_ref, gsel_ref, grow_ref, gmix_ref,
                qe_ref, mp_ref, nn_ref, dd_ref, osum_ref,
                *, seq, lp, group, n_heads_total):
    step = pl.program_id(0)
    head = jnp.minimum(step, n_heads_total - 1) % N_HEADS
    nc = lp // CHUNK
    padr = lp - seq
    halo = SUBLANES
    row64 = lax.broadcasted_iota(jnp.int32, (CHUNK, LANES), 0)
    lane64 = lax.broadcasted_iota(jnp.int32, (CHUNK, LANES), 1)

    def for_groups(fn, carry=0):
        n_full, rem = divmod(nc, group)
        if n_full:
            carry = lax.fori_loop(0, n_full, lambda i, c: fn([i * group + j for j in range(group)], c), carry)
        if rem:
            carry = fn([n_full * group + j for j in range(rem)], carry)
        return carry

    cur_half = step % 2
    qe_w, mp_w, nn_w, dd_w, osum_w = (r.at[cur_half] for r in (qe_ref, mp_ref, nn_ref, dd_ref, osum_ref))
    qe_r, mp_r, nn_r, dd_r, osum_r = (r.at[1 - cur_half] for r in (qe_ref, mp_ref, nn_ref, dd_ref, osum_ref))

    streams = ((rawq_ref, q_ref, cwq_ref, qs_ref, HEAD_DIM ** -0.5),
               (rawk_ref, k_ref, cwk_ref, ks_ref, 1.0),
               (rawv_ref, v_ref, cwv_ref, vs_ref, None))
    def stage_conv_inputs():
        for raw_ref, src_ref, _, _, _ in streams:
            raw_ref[0:padr + halo, :] = jnp.zeros((padr + halo, LANES), F32)
            raw_ref[padr + halo:padr + halo + seq, :] = src_ref[...].astype(F32)
            raw_ref[lp + halo:lp + 2 * halo, :] = jnp.zeros((halo, LANES), F32)

    def conv_chunk(c):
        base = _aligned(c * CHUNK, CHUNK)
        for raw_ref, _, cw_ref, dst_ref, l2_scale in streams:
            cw = cw_ref[...]
            acc = None
            for j in range(CONV_K):
                off = halo - (CONV_K - 1) // 2 + j
                tap = cw[j:j + 1, :] * raw_ref[pl.ds(base + off, CHUNK), :]
                acc = tap if acc is None else acc + tap
            y = _silu(acc)
            if isinstance(c, int) and c * CHUNK < padr:
                y = jnp.where(row64 + base >= padr, y, 0.0)
            if l2_scale is not None:
                inv = lax.rsqrt(jnp.sum(y * y, axis=-1, keepdims=True) + NORM_EPS)
                y = y * (inv if l2_scale == 1.0 else inv * l2_scale)
            dst_ref[pl.ds(base, CHUNK), :] = y

    raw_ref = rawq_ref

    def stage_gate_inputs():
        raw_ref[0:padr, :] = jnp.zeros((padr, LANES), F32)
        raw_ref[padr:padr + seq, :] = ab_ref[...]

    neg_a = -jnp.exp(prm_ref[0:1, :])
    dt_bias = prm_ref[1:2, :]
    is_g = (lane64 < 8) | ((lane64 >= 16) & (lane64 < 24))
    is_beta = ((lane64 >= 8) & (lane64 < 16)) | ((lane64 >= 24) & (lane64 < 32))
    r128 = lax.broadcasted_iota(jnp.int32, (2 * CHUNK, CHUNK), 0)
    c128 = lax.broadcasted_iota(jnp.int32, (2 * CHUNK, CHUNK), 1)
    tri2 = jnp.where(((r128 < CHUNK) & (r128 >= c128)) | ((r128 >= CHUNK) & (r128 - CHUNK <= c128)),
                     1.0, 0.0).astype(MXU_DTYPE)
    sel_r = lax.broadcasted_iota(jnp.int32, (LANES, 4 * LANES), 0)
    sel_c = lax.broadcasted_iota(jnp.int32, (LANES, 4 * LANES), 1)
    n_gate_cols = 4 * N_HEADS
    sel_packed = jnp.where((sel_r < 3 * n_gate_cols)
                           & (sel_r % n_gate_cols == head + N_HEADS * (sel_c // LANES)), 1.0, 0.0).astype(MXU_DTYPE)
    tri2x3 = jnp.concatenate([tri2, tri2, tri2], axis=1)

    def gate_batch_group(cs, carry):
        bases = [_aligned(c * CHUNK, CHUNK) for c in cs]
        gates, parts = [], []
        for base in bases:
            ab = raw_ref[pl.ds(base, CHUNK), :]
            g = neg_a * jax.nn.softplus(ab + dt_bias)
            beta = jax.nn.sigmoid(ab)
            gt = jnp.where(is_g, g, jnp.where(is_beta, beta, 0.0))
            gt = jnp.where(row64 + base >= padr, gt, 0.0)
            gates.append(gt)
            parts.append(jnp.concatenate(_split3(gt), axis=0))
        cums = [_dot(tri2x3, p) for p in parts]
        for c, base, cm, gt in zip(cs, bases, cums, gates):
            m = jnp.where(lane64 < 8, cm[:CHUNK], jnp.where((lane64 >= 16) & (lane64 < 24), cm[CHUNK:], gt))
            gmix_ref[pl.ds(base, CHUNK), :] = m
            grow_ref[pl.ds(_aligned(c * LANES, LANES), LANES), :] = jnp.concatenate([m, m], axis=0).T
        return carry

    def gate_head_group(cs, carry):
        rows = len(cs) * CHUNK
        base = _aligned(cs[0] * CHUNK, CHUNK)
        hi, mid, lo = _split3(gmix_ref[pl.ds(base, rows), :])
        packed = (hi.astype(F32) + pltpu.roll(mid.astype(F32), n_gate_cols, 1)
                  + pltpu.roll(lo.astype(F32), 2 * n_gate_cols, 1)).astype(MXU_DTYPE)
        gsel_ref[pl.ds(base, rows), :] = _dot(packed, sel_packed)
        return carry

    fwd_half = lane64 < CHUNK
    col64 = lane64 & (CHUNK - 1)
    incl2 = (fwd_half & (row64 >= col64)) | (~fwd_half & (row64 <= col64))
    strict2 = (fwd_half & (row64 > col64)) | (~fwd_half & (row64 < col64))
    eye2 = jnp.where(row64 == col64, 1.0, 0.0).astype(F32)

    def blockdiag(p):
        return jnp.concatenate([jnp.where(fwd_half, p, 0.0), jnp.where(fwd_half, 0.0, p)],
                               axis=0).astype(MXU_DTYPE)

    n_neumann = int(math.log2(CHUNK)) - 2
    n_stages = n_neumann + 6

    def chunk_group(cs, carry, with_scan):
        n = len(cs)
        bases = [_aligned(c * CHUNK, CHUNK) for c in cs]
        cb128 = [_aligned(c * LANES, LANES) for c in cs]
        cb8 = [_aligned(c * SUBLANES, SUBLANES) for c in cs]
        stage = [0]

        def end_of_stage(carry):
            s = stage[0]
            stage[0] += 1
            if with_scan:
                for j in range(n):
                    if j * n_stages // n == s:
                        carry = scan_step(cs[j], carry)
            return carry

        def gate_cols(j):
            gs = gsel_ref[pl.ds(bases[j], CHUNK), :]
            return gs[:, 0:LANES], gs[:, LANES:2 * LANES], gs[:, 2 * LANES:3 * LANES], gs[:, 3 * LANES:]

        kq = []
        for j in range(n):
            k16 = ks_ref[pl.ds(bases[j], CHUNK), :].astype(MXU_DTYPE)
            q16 = qs_ref[pl.ds(bases[j], CHUNK), :].astype(MXU_DTYPE)
            kq.append(_dot_nt(jnp.concatenate([k16, q16], axis=0), jnp.concatenate([k16, k16], axis=0)))
        carry = end_of_stage(carry)

        ts, ps, qkds = [], [], []
        for j in range(n):
            gf, bf, gb, bb = gate_cols(j)
            row_f = grow_ref[pl.ds(cb128[j] + head, 1), :]
            row_b = grow_ref[pl.ds(cb128[j] + 2 * N_HEADS + head, 1), :]
            g_col = jnp.where(fwd_half, gf, gb)
            g_row = jnp.where(fwd_half, row_f, row_b)
            decay = jnp.where(incl2, jnp.exp(jnp.where(incl2, g_col - g_row, 0.0)), 0.0)
            a2 = jnp.where(strict2, kq[j][:CHUNK] * jnp.where(fwd_half, bf, bb) * decay, 0.0)
            qkds.append(jnp.where(incl2, kq[j][CHUNK:] * decay, 0.0))
            ts.append(eye2 - a2)
            ps.append(a2)

        ps = [_dot(p.astype(MXU_DTYPE), blockdiag(p)) for p in ps]
        carry = end_of_stage(carry)
        for _ in range(n_neumann):
            tps = [_dot(jnp.concatenate([t, p], axis=0).astype(MXU_DTYPE), blockdiag(p)) for t, p in zip(ts, ps)]
            carry = end_of_stage(carry)
            ts = [t + tp[:CHUNK] for t, tp in zip(ts, tps)]
            ps = [tp[CHUNK:] for tp in tps]
        ts = [t + _dot(t.astype(MXU_DTYPE), blockdiag(p)) for t, p in zip(ts, ps)]
        carry = end_of_stage(carry)

        uws = []
        for j in range(n):
            gf, bf, gb, bb = gate_cols(j)
            k = ks_ref[pl.ds(bases[j], CHUNK), :]
            v = vs_ref[pl.ds(bases[j], CHUNK), :]
            x_f = jnp.concatenate([v * bf, k * bf * jnp.exp(gf)], axis=1)
            x_b = jnp.concatenate([v * bb, k * bb * jnp.exp(gb)], axis=1)
            uw = _dot(blockdiag(ts[j]), jnp.concatenate([x_f, x_b], axis=0).astype(MXU_DTYPE))
            uws.append(uw.astype(MXU_DTYPE))
        carry = end_of_stage(carry)

        xxs = [_dot(blockdiag(qkds[j]), uws[j]) for j in range(n)]
        carry = end_of_stage(carry)
        for j in range(n):
            gf, _, gb, _ = gate_cols(j)
            k = ks_ref[pl.ds(bases[j], CHUNK), :]
            q = qs_ref[pl.ds(bases[j], CHUNK), :]
            xx = xxs[j]
            osum_w[pl.ds(bases[j], CHUNK), :] = xx[:CHUNK, :LANES] + xx[CHUNK:, :LANES]
            for d, gc in enumerate((gf, gb)):
                rows = slice(d * CHUNK, (d + 1) * CHUNK)
                g_last = gc[CHUNK - 1:CHUNK, :] if d == 0 else gc[0:1, :]
                k_dec = k * jnp.exp(g_last - gc)
                mn = _dot_tn(k_dec.astype(MXU_DTYPE), uws[j][rows])
                qe_w[d, pl.ds(bases[j], CHUNK), :] = (q * jnp.exp(gc) - xx[rows, LANES:]).astype(MXU_DTYPE)
                nn_w[d, pl.ds(cb128[j], HEAD_DIM), :] = mn[:, :LANES]
                mp_w[d, pl.ds(cb128[j], HEAD_DIM), :] = mn[:, LANES:].astype(MXU_DTYPE)
                dd_w[d, pl.ds(cb8[j], SUBLANES), :] = jnp.broadcast_to(jnp.exp(g_last), (SUBLANES, LANES))
        carry = end_of_stage(carry)
        assert stage[0] == n_stages
        return carry

    def scan_step(i, carry):
        new = []
        for d in range(2):
            s = carry[d]
            c = i if d == 0 else nc - 1 - i
            base = _aligned(c * CHUNK, CHUNK)
            cbase = _aligned(c * HEAD_DIM, HEAD_DIM)
            dbase = _aligned(c * SUBLANES, SUBLANES)
            s16 = s.astype(MXU_DTYPE)
            osum_r[pl.ds(base, CHUNK), :] += _dot(qe_r[d, pl.ds(base, CHUNK), :], s16)
            dec = dd_r[d, pl.ds(dbase, 1), :]
            new.append(dec * s - _dot(mp_r[d, pl.ds(cbase, HEAD_DIM), :], s16)
                       + nn_r[d, pl.ds(cbase, HEAD_DIM), :])
        return tuple(new)

    zero_state = jnp.zeros((HEAD_DIM, HEAD_DIM), F32)

    def write_output():
        gain = ong_ref[...]

        def finish(o, z):
            ms = jnp.mean(o * o, axis=-1, keepdims=True)
            return (o * lax.rsqrt(ms + NORM_EPS) * gain * _silu(z.astype(F32))).astype(o_ref.dtype)

        first = CHUNK - padr if padr else 0
        if first:
            o_ref[0:first, :] = finish(osum_r[padr:CHUNK, :], z_ref[0:first, :])

        def out_body(c, carry):
            base = pl.multiple_of(c * CHUNK, CHUNK)
            dst = pl.multiple_of(c * CHUNK - padr, SUBLANES)
            o_ref[pl.ds(dst, CHUNK), :] = finish(osum_r[pl.ds(base, CHUNK), :], z_ref[pl.ds(dst, CHUNK), :])
            return carry

        lax.fori_loop(1 if first else 0, nc, out_body, 0, unroll=8)

    def conv_only(c, carry):
        conv_chunk(c)
        return carry

    @pl.when(step < n_heads_total)
    def _():
        stage_conv_inputs()
        conv_chunk(0)
        lax.fori_loop(1, nc, conv_only, 0, unroll=4)

    @pl.when((step < n_heads_total) & (head == 0))
    def _():
        stage_gate_inputs()
        for_groups(gate_batch_group)

    @pl.when(step < n_heads_total)
    def _():
        for_groups(gate_head_group)

    @pl.when(step == 0)
    def _():
        for_groups(functools.partial(chunk_group, with_scan=False))

    @pl.when((step > 0) & (step < n_heads_total))
    def _():
        for_groups(functools.partial(chunk_group, with_scan=True), (zero_state, zero_state))
        write_output()

    @pl.when(step == n_heads_total)
    def _():
        lax.fori_loop(0, nc, scan_step, (zero_state, zero_state))
        write_output()


def _gdn(proj3, ab3, conv_w, gate_prm, out_norm_g):
    b, seq, _ = proj3.shape
    lp = -(-seq // CHUNK) * CHUNK
    nc = lp // CHUNK
    n_total = b * N_HEADS
    cur = lambda i: jnp.minimum(i, n_total - 1)
    prev = lambda i: jnp.maximum(i - 1, 0)
    col = lambda off: pl.BlockSpec((None, seq, LANES), lambda i: (cur(i) // N_HEADS, 0, off + cur(i) % N_HEADS))
    cw = lambda off: pl.BlockSpec((CONV_K, LANES), lambda i: (0, off + cur(i) % N_HEADS))
    return pl.pallas_call(
        functools.partial(_gdn_kernel, seq=seq, lp=lp, group=GDN_GROUP, n_heads_total=n_total),
        grid=(n_total + 1,),
        in_specs=[
            col(COL_QKV // LANES), col(COL_QKV // LANES + N_HEADS), col(COL_QKV // LANES + 2 * N_HEADS),
            pl.BlockSpec((None, seq, LANES),
                         lambda i: (prev(i) // N_HEADS, 0, COL_Z // LANES + prev(i) % N_HEADS)),
            pl.BlockSpec((None, seq, LANES), lambda i: (cur(i) // N_HEADS, 0, 0)),
            cw(0), cw(N_HEADS), cw(2 * N_HEADS),
            pl.BlockSpec((SUBLANES, LANES), lambda i: (0, 0)),
            pl.BlockSpec((1, LANES), lambda i: (0, 0)),
        ],
        out_specs=pl.BlockSpec((None, seq, LANES), lambda i: (prev(i) // N_HEADS, 0, prev(i) % N_HEADS)),
        out_shape=jax.ShapeDtypeStruct((b, seq, N_HEADS * HEAD_DIM), MXU_DTYPE),
        scratch_shapes=[
            pltpu.VMEM((lp + 2 * SUBLANES, LANES), F32),
            pltpu.VMEM((lp + 2 * SUBLANES, LANES), F32),
            pltpu.VMEM((lp + 2 * SUBLANES, LANES), F32),
            pltpu.VMEM((lp, LANES), F32),
            pltpu.VMEM((lp, LANES), F32),
            pltpu.VMEM((lp, LANES), F32),
            pltpu.VMEM((lp, 4 * LANES), F32),
            pltpu.VMEM((nc * LANES, LANES), F32),
            pltpu.VMEM((lp, LANES), F32),
            pltpu.VMEM((2, 2, lp, LANES), MXU_DTYPE),
            pltpu.VMEM((2, 2, nc * HEAD_DIM, LANES), MXU_DTYPE),
            pltpu.VMEM((2, 2, nc * HEAD_DIM, LANES), F32),
            pltpu.VMEM((2, 2, nc * SUBLANES, LANES), F32),
            pltpu.VMEM((2, lp, LANES), F32),
        ],
        compiler_params=_params(("arbitrary",)),
        name="gdn",
    )(proj3, proj3, proj3, proj3, ab3, conv_w, conv_w, conv_w, gate_prm, out_norm_g)


def _merge_kernel(h_ref, og_ref, ym_ref, gf_ref, gd_ref, wf_ref, wd_ref, wo_ref, n2_ref, wr_ref, br_ref,
                  h1_ref, hn_ref, ridx_ref, rw_ref, cnt_ref):
    @pl.when(pl.program_id(0) == 0)
    def _():
        cnt_ref[...] = jnp.zeros_like(cnt_ref)

    y_f = _dot(ym_ref[...], wf_ref[...])
    y_d = _dot(og_ref[...], wd_ref[...])
    merged = (jax.nn.sigmoid(gf_ref[...].astype(F32)) * y_f
              + jax.nn.sigmoid(gd_ref[...].astype(F32)) * y_d)
    h1 = h_ref[...] + _dot(merged.astype(MXU_DTYPE), wo_ref[...])
    h1_ref[...] = h1
    ms = jnp.mean(h1 * h1, axis=-1, keepdims=True)
    hn = h1 * lax.rsqrt(ms + NORM_EPS) * n2_ref[...]
    hn_ref[...] = _pack_rows(hn)

    x_hi = hn.astype(MXU_DTYPE)
    x_lo = (hn - x_hi.astype(F32)).astype(MXU_DTYPE)
    w = wr_ref[...]
    w_hi = w.astype(MXU_DTYPE)
    w_lo = (w - w_hi.astype(F32)).astype(MXU_DTYPE)
    hi_terms = _dot(x_hi, jnp.concatenate([w_hi, w_lo], axis=1))
    logits = hi_terms[:, :LANES] + hi_terms[:, LANES:] + _dot(x_lo, w_hi) + br_ref[...]

    tm = logits.shape[0]
    lane = lax.broadcasted_iota(jnp.int32, (tm, LANES), 1)
    neg_inf = -jnp.inf
    is_group = lane < N_GROUPS
    gl = jnp.where(is_group, logits, neg_inf)
    ge = jnp.exp(gl - jnp.max(gl, axis=-1, keepdims=True))
    gp = ge / jnp.sum(ge, axis=-1, keepdims=True)
    p_group = jnp.max(gp, axis=-1, keepdims=True)
    g_idx = jnp.min(jnp.where(is_group & (gp == p_group), lane, LANES), axis=-1, keepdims=True)

    in_group = (lane >= N_GROUPS) & (lane < N_GROUPS + N_EXPERTS) & (((lane - N_GROUPS) >> 3) == g_idx)
    el = jnp.where(in_group, logits, neg_inf)
    ee = jnp.exp(el - jnp.max(el, axis=-1, keepdims=True))
    ep = ee / jnp.sum(ee, axis=-1, keepdims=True)
    v1 = jnp.max(jnp.where(in_group, ep, -1.0), axis=-1, keepdims=True)
    i1 = jnp.min(jnp.where(in_group & (ep == v1), lane, LANES), axis=-1, keepdims=True)
    rest = in_group & (lane != i1)
    v2 = jnp.max(jnp.where(rest, ep, -1.0), axis=-1, keepdims=True)
    i2 = jnp.min(jnp.where(rest & (ep == v2), lane, LANES), axis=-1, keepdims=True)
    denom = v1 + v2
    w1 = p_group * (v1 / denom)
    w2 = p_group * (v2 / denom)
    e1 = i1 - N_GROUPS
    e2 = i2 - N_GROUPS
    ridx_ref[...] = jnp.where(lane == 0, e1, jnp.where(lane == 1, e2, 0))
    rw_ref[...] = jnp.where(lane == 0, w1, jnp.where(lane == 1, w2, 0.0))
    onehots = jnp.where((lane == e1) | (lane == e2), 1.0, 0.0)
    cnt_ref[...] += jnp.broadcast_to(jnp.sum(onehots, axis=0, keepdims=True), cnt_ref.shape)


def _merge(h2d, og2d, ymix2d, proj2d, w_fourier, w_delta, w_out, norm2_g, w_router, b_router, tm):
    t, d = h2d.shape
    row = lambda w: pl.BlockSpec((tm, w), lambda i: (i, 0))
    const = lambda shape: pl.BlockSpec(shape, lambda i: (0, 0))
    return pl.pallas_call(
        _merge_kernel,
        grid=(t // tm,),
        in_specs=[
            row(d), row(d), row(F_WIDTH),
            pl.BlockSpec((tm, d), lambda i: (i, COL_GATE_F // d)),
            pl.BlockSpec((tm, d), lambda i: (i, COL_GATE_D // d)),
            const((F_WIDTH, d)), const((d, d)), const((d, d)), const((1, d)),
            const((d, LANES)), const((1, LANES)),
        ],
        out_specs=[row(d), row(d // 2), row(LANES), row(LANES), const((SUBLANES, LANES))],
        out_shape=[
            jax.ShapeDtypeStruct((t, d), F32),
            jax.ShapeDtypeStruct((t, d // 2), jnp.uint32),
            jax.ShapeDtypeStruct((t, LANES), jnp.int32),
            jax.ShapeDtypeStruct((t, LANES), F32),
            jax.ShapeDtypeStruct((SUBLANES, LANES), F32),
        ],
        compiler_params=_params(("arbitrary",)),
        name="merge",
    )(h2d, og2d, ymix2d, proj2d, proj2d, w_fourier, w_delta, w_out, norm2_g, w_router, b_router)


def _plan_kernel(ridx_ref, cnt_ref, dest_ref, seg_ref, base_ref, run_ref):
    tt = ridx_ref.shape[0]

    @pl.when(pl.program_id(0) == 0)
    def _():
        cnt = cnt_ref[...].astype(jnp.int32)
        n_blk = (cnt + (EXPERT_BLOCK - 1)) // EXPERT_BLOCK
        padded = (n_blk * EXPERT_BLOCK).astype(F32)
        r = lax.broadcasted_iota(jnp.int32, (LANES, LANES), 0)
        c = lax.broadcasted_iota(jnp.int32, (LANES, LANES), 1)
        before = jnp.where(r < c, 1.0, 0.0).astype(MXU_DTYPE)
        start = _exact_right(padded, before)
        base_ref[...] = start
        run_ref[...] = jnp.zeros_like(run_ref)
        seg_ref[0:SUBLANES, :] = start.astype(jnp.int32)
        seg_ref[SUBLANES:2 * SUBLANES, :] = n_blk
        seg_ref[2 * SUBLANES:, :] = cnt

    lane = lax.broadcasted_iota(jnp.int32, (tt, LANES), 1)
    ridx = ridx_ref[...]
    oh0 = jnp.where(lane == ridx[:, 0:1], 1.0, 0.0)
    oh1 = jnp.where(lane == ridx[:, 1:2], 1.0, 0.0)
    r = lax.broadcasted_iota(jnp.int32, (tt, tt), 0)
    c = lax.broadcasted_iota(jnp.int32, (tt, tt), 1)
    earlier = jnp.where(c < r, 1.0, 0.0).astype(MXU_DTYPE)
    prefix = _dot(earlier, jnp.concatenate([oh0, oh1], axis=1).astype(MXU_DTYPE))
    c0 = jnp.sum(oh0, axis=0, keepdims=True)
    c1 = jnp.sum(oh1, axis=0, keepdims=True)
    first = base_ref[0:1, :] + run_ref[0:1, :]
    d0 = jnp.sum(oh0 * (prefix[:, :LANES] + first), axis=-1, keepdims=True)
    d1 = jnp.sum(oh1 * (prefix[:, LANES:] + first + c0), axis=-1, keepdims=True)
    run_ref[...] += jnp.broadcast_to(c0 + c1, run_ref.shape)
    both = jnp.where(lane == 0, d0, jnp.where(lane == 1, d1, 0.0))
    both_t = both.T.astype(jnp.int32)
    for k in range(TOP_K):
        dest_ref[k] = both_t[k:k + 1, :]


def _plan(ridx, cnt, tt):
    t = ridx.shape[0]
    return pl.pallas_call(
        _plan_kernel,
        grid=(t // tt,),
        in_specs=[pl.BlockSpec((tt, LANES), lambda i: (i, 0)),
                  pl.BlockSpec((SUBLANES, LANES), lambda i: (0, 0))],
        out_specs=[pl.BlockSpec((TOP_K, None, 1, tt), lambda i: (0, i, 0, 0)),
                   pl.BlockSpec((3 * SUBLANES, LANES), lambda i: (0, 0))],
        out_shape=[jax.ShapeDtypeStruct((TOP_K, t // tt, 1, tt), jnp.int32),
                   jax.ShapeDtypeStruct((3 * SUBLANES, LANES), jnp.int32)],
        scratch_shapes=[pltpu.VMEM((SUBLANES, LANES), F32), pltpu.VMEM((SUBLANES, LANES), F32)],
        compiler_params=_params(("arbitrary",)),
        name="plan",
    )(ridx, cnt)


def _scatter_kernel(dest0_ref, dest1_ref, hn_ref, xb_in_ref, xb_ref, sem, *, ts):
    dest_refs = (dest0_ref, dest1_ref)
    del xb_in_ref

    def issue(t, carry):
        src = hn_ref.at[pl.ds(t, 1)]
        for k in range(TOP_K):
            pltpu.make_async_copy(src, xb_ref.at[pl.ds(dest_refs[k][0, t], 1)], sem).start(priority=k % 2)
        return carry

    lax.fori_loop(0, ts, issue, 0, unroll=DMA_UNROLL)
    for _ in range(TOP_K):
        pltpu.make_async_copy(hn_ref, xb_ref.at[pl.ds(0, ts)], sem).wait()


def _scatter(dest_flat, hn2d, xb_zero, ts):
    t, d = hn2d.shape
    return pl.pallas_call(
        functools.partial(_scatter_kernel, ts=ts),
        grid=(t // ts,),
        in_specs=[pl.BlockSpec((None, None, 1, ts), lambda i: (0, i, 0, 0), memory_space=pltpu.SMEM),
                  pl.BlockSpec((None, None, 1, ts), lambda i: (1, i, 0, 0), memory_space=pltpu.SMEM),
                  pl.BlockSpec((ts, d), lambda i: (i, 0)),
                  pl.BlockSpec(memory_space=pl.ANY)],
        out_specs=pl.BlockSpec(memory_space=pl.ANY),
        out_shape=jax.ShapeDtypeStruct(xb_zero.shape, xb_zero.dtype),
        scratch_shapes=[pltpu.SemaphoreType.DMA(())],
        input_output_aliases={3: 0},
        compiler_params=_params(("arbitrary",)),
        name="scatter",
    )(dest_flat, dest_flat, hn2d, xb_zero)


def _expert_kernel(start_ref, nblk_ref, xb_ref, wg_ref, wu_ref, wd_ref, yb_ref,
                   wg16_ref, wu16_ref, wd16_ref, xbuf_ref, ybuf_ref, sem_in, sem_out):
    expert = pl.program_id(0)
    last = pl.num_programs(0) - 1
    n = nblk_ref[expert]
    first = start_ref[expert] // EXPERT_BLOCK
    total = start_ref[last] // EXPERT_BLOCK + nblk_ref[last]

    def rows(g):
        return pl.ds(pl.multiple_of(g * EXPERT_BLOCK, EXPERT_BLOCK), EXPERT_BLOCK)

    def x_copy(g):
        slot = g % EXPERT_X_SLOTS
        return pltpu.make_async_copy(xb_ref.at[rows(g)], xbuf_ref.at[slot], sem_in.at[slot])

    def y_copy(g):
        slot = g % EXPERT_Y_SLOTS
        return pltpu.make_async_copy(ybuf_ref.at[slot], yb_ref.at[rows(g)], sem_out.at[slot])

    @pl.when(n > 0)
    def _():
        @pl.when(first == 0)
        def _():
            for j in range(EXPERT_X_SLOTS - 1):
                @pl.when(j < total)
                def _():
                    x_copy(j).start()

        wg16_ref[...] = wg_ref[...].astype(MXU_DTYPE)
        wu16_ref[...] = wu_ref[...].astype(MXU_DTYPE)
        wd16_ref[...] = wd_ref[...].astype(MXU_DTYPE)

        def body(i, carry):
            g = first + i

            @pl.when(g + (EXPERT_X_SLOTS - 1) < total)
            def _():
                x_copy(g + (EXPERT_X_SLOTS - 1)).start()

            x_copy(g).wait()

            @pl.when(g >= EXPERT_Y_SLOTS)
            def _():
                y_copy(g - EXPERT_Y_SLOTS).wait()

            x = _unpack_rows(xbuf_ref[g % EXPERT_X_SLOTS]).astype(MXU_DTYPE)
            gate = _dot(x, wg16_ref[...])
            up = _dot(x, wu16_ref[...])
            ybuf_ref[g % EXPERT_Y_SLOTS] = _pack_rows(_dot((_silu(gate) * up).astype(MXU_DTYPE), wd16_ref[...]))
            y_copy(g).start()
            return carry

        lax.fori_loop(0, n, body, 0)

    @pl.when(expert == last)
    def _():
        for j in range(EXPERT_Y_SLOTS):
            @pl.when(total > j)
            def _():
                y_copy(total - 1 - j).wait()


def _experts(seg_start, seg_blocks, xb, w_gate, w_up, w_down):
    n_slots, row_words = xb.shape
    n_exp, d, de = w_gate.shape
    grid_spec = pltpu.PrefetchScalarGridSpec(
        num_scalar_prefetch=2,
        grid=(n_exp,),
        in_specs=[
            pl.BlockSpec(memory_space=pl.ANY),
            pl.BlockSpec((None, d, de), lambda e, st, nb: (e, 0, 0)),
            pl.BlockSpec((None, d, de), lambda e, st, nb: (e, 0, 0)),
            pl.BlockSpec((None, de, d), lambda e, st, nb: (e, 0, 0)),
        ],
        out_specs=pl.BlockSpec(memory_space=pl.ANY),
        scratch_shapes=[
            pltpu.VMEM((d, de), MXU_DTYPE), pltpu.VMEM((d, de), MXU_DTYPE), pltpu.VMEM((de, d), MXU_DTYPE),
            pltpu.VMEM((EXPERT_X_SLOTS, EXPERT_BLOCK, row_words), jnp.uint32),
            pltpu.VMEM((EXPERT_Y_SLOTS, EXPERT_BLOCK, row_words), jnp.uint32),
            pltpu.SemaphoreType.DMA((EXPERT_X_SLOTS,)), pltpu.SemaphoreType.DMA((EXPERT_Y_SLOTS,)),
        ],
    )
    return pl.pallas_call(
        _expert_kernel,
        grid_spec=grid_spec,
        out_shape=jax.ShapeDtypeStruct((n_slots, row_words), jnp.uint32),
        input_output_aliases={2: 0},
        compiler_params=_params(("arbitrary",)),
        name="experts",
    )(seg_start, seg_blocks, xb, w_gate, w_up, w_down)


def _combine_kernel(dest0_ref, dest1_ref, yb_ref, h1_ref, rw_ref, fg_ref, o_ref, buf_ref, res_ref, sem, out_sem,
                    *, tc, groups_per_batch):
    step = pl.program_id(0)
    n_tiles = pl.num_programs(0) - 1
    n_grp = tc // N_META
    cur = step % 2
    prev = 1 - cur

    def output_copies(tile, action):
        slot = tile % 2
        first = tile * n_grp
        first_in_batch = first % groups_per_batch
        has_meta = (first_in_batch == 0) | (first_in_batch + n_grp > groups_per_batch)

        def copy_groups(q, n):
            grp = first + q
            dst = pl.multiple_of((grp - grp // groups_per_batch - 1) * N_META, N_META)
            copy = pltpu.make_async_copy(res_ref.at[slot, pl.ds(q * N_META, n * N_META)],
                                         o_ref.at[pl.ds(dst, n * N_META)], out_sem.at[slot])
            copy.start() if action == "start" else copy.wait()

        @pl.when(jnp.logical_not(has_meta))
        def _():
            copy_groups(0, n_grp)

        @pl.when(has_meta)
        def _():
            for q in range(n_grp):
                @pl.when((first + q) % groups_per_batch != 0)
                def _():
                    copy_groups(q, 1)

    def request_rows(i):
        for u in range(SUBLANES):
            t = i * SUBLANES + u
            for k in range(TOP_K):
                pltpu.make_async_copy(yb_ref.at[pl.ds((dest0_ref, dest1_ref)[k][0, t], 1)],
                                      buf_ref.at[cur, k, pl.ds(t, 1)], sem.at[cur]).start(priority=k % 2)

    def finish_rows(i):
        rows = pl.ds(pl.multiple_of(i * SUBLANES, SUBLANES), SUBLANES)
        rw = rw_ref[rows, :]
        h2 = h1_ref[rows, :] + (_unpack_rows(buf_ref[prev, 0, rows, :]) * rw[:, 0:1]
                                + _unpack_rows(buf_ref[prev, 1, rows, :]) * rw[:, 1:2])
        ms = jnp.mean(h2 * h2, axis=-1, keepdims=True)
        res_ref[prev, rows, :] = h2 * lax.rsqrt(ms + NORM_EPS) * fg_ref[...]

    def begin_finish():
        for k in range(TOP_K):
            pltpu.make_async_copy(yb_ref.at[pl.ds(0, tc)], buf_ref.at[prev, k], sem.at[prev]).wait()

        @pl.when(step >= 3)
        def _():
            output_copies(step - 3, "wait")

    def both(i, carry):
        request_rows(i)
        finish_rows(i)
        return carry

    def request_only(i, carry):
        request_rows(i)
        return carry

    def finish_only(i, carry):
        finish_rows(i)
        return carry

    @pl.when(step == 0)
    def _():
        lax.fori_loop(0, tc // SUBLANES, request_only, 0)

    @pl.when((step > 0) & (step < n_tiles))
    def _():
        begin_finish()
        lax.fori_loop(0, tc // SUBLANES, both, 0, unroll=8)
        output_copies(step - 1, "start")

    @pl.when(step == n_tiles)
    def _():
        begin_finish()
        lax.fori_loop(0, tc // SUBLANES, finish_only, 0, unroll=4)
        output_copies(step - 1, "start")

        @pl.when(step >= 2)
        def _():
            output_copies(step - 2, "wait")

        output_copies(step - 1, "wait")


def _combine(dest_flat, yb, h1, rw, final_g, tc, seq):
    t, d = h1.shape
    assert tc % N_META == 0 and seq % N_META == 0
    n_out = t - (t // seq) * N_META
    n_tiles = t // tc
    nxt = lambda i: jnp.minimum(i, n_tiles - 1)
    fin = lambda i: jnp.maximum(i - 1, 0)
    return pl.pallas_call(
        functools.partial(_combine_kernel, tc=tc, groups_per_batch=seq // N_META),
        grid=(n_tiles + 1,),
        in_specs=[pl.BlockSpec((None, None, 1, tc), lambda i: (0, nxt(i), 0, 0), memory_space=pltpu.SMEM),
                  pl.BlockSpec((None, None, 1, tc), lambda i: (1, nxt(i), 0, 0), memory_space=pltpu.SMEM),
                  pl.BlockSpec(memory_space=pl.ANY),
                  pl.BlockSpec((tc, d), lambda i: (fin(i), 0)),
                  pl.BlockSpec((tc, LANES), lambda i: (fin(i), 0)),
                  pl.BlockSpec((1, d), lambda i: (0, 0))],
        out_specs=pl.BlockSpec(memory_space=pl.ANY),
        out_shape=jax.ShapeDtypeStruct((n_out, d), F32),
        scratch_shapes=[pltpu.VMEM((2, TOP_K, tc, d // 2), jnp.uint32), pltpu.VMEM((2, tc, d), F32),
                        pltpu.SemaphoreType.DMA((2,)), pltpu.SemaphoreType.DMA((2,))],
        compiler_params=_params(("arbitrary",)),
        name="combine",
    )(dest_flat, dest_flat, yb, h1, rw, final_g)


def _tile(total, candidates):
    for c in candidates:
        if total % c == 0:
            return c
    raise ValueError(f"no tile for {total}")


def _dft_tables(seq):
    kp = -(-seq // LANES) * LANES
    nat = -(-(seq // 2 + 1) // 16) * 16
    mir = seq - nat
    c = np.arange(FGROUP_DIM, dtype=np.int64)
    ang_c = (2.0 * np.pi / FGROUP_DIM) * ((c[:, None] * c[None, :]) % FGROUP_DIM)
    cs128 = np.concatenate([np.cos(ang_c), np.sin(ang_c)], axis=1)
    k = np.arange(nat, dtype=np.int64)
    n = np.arange(seq, dtype=np.int64)
    ang_l = (2.0 * np.pi / seq) * ((k[:, None] * n[None, :]) % seq)
    twid = np.zeros((nat, 2 * kp), np.float32)
    twid[:, :seq] = np.cos(ang_l)
    twid[:, kp:kp + seq] = np.sin(ang_l)
    flip = np.zeros((mir, nat), np.float32)
    flip[np.arange(mir), mir - np.arange(mir)] = 1.0
    as_const = lambda t: jnp.asarray(t.astype(MXU_DTYPE))
    return as_const(cs128), as_const(twid), as_const(flip)


def kernel(x, meta_tokens, norm1_g, w_in, conv_w, a_log_fwd, dt_bias_fwd, a_log_bwd, dt_bias_bwd, out_norm_g, w_fourier, w_delta, w_out, norm2_g, w_router_group, b_router_group, w_router_expert, b_router_expert, w_gate_e, w_up_e, w_down_e, final_norm_g):
    bsz, n_real, d = x.shape
    seq = N_META + n_real
    n_tok = bsz * seq
    assert (CHUNK - N_META) % CHUNK + seq == -(-seq // CHUNK) * CHUNK, "meta tokens must fill the tail of chunk 0"
    assert w_in.shape[0] == 1, "single trunk layer"
    layer = 0


    w = w_in[layer]
    o_f, o_qkv, o_z, o_ab, o_gf, o_gd = 0, F_WIDTH, F_WIDTH + 3 * QK_WIDTH, F_WIDTH + 4 * QK_WIDTH, \
        F_WIDTH + 4 * QK_WIDTH + 4 * N_HEADS, F_WIDTH + 4 * QK_WIDTH + 4 * N_HEADS + d
    w_cat = jnp.concatenate(
        [w[:, o_qkv:o_z], w[:, o_z:o_ab], w[:, o_gf:o_gd], w[:, o_gd:o_gd + d], w[:, o_f:o_qkv]],
        axis=1).astype(MXU_DTYPE)
    w_ab = jnp.pad(w[:, o_ab:o_gf], ((0, 0), (0, LANES - 4 * N_HEADS))).astype(MXU_DTYPE)

    proj, ab, h = _in_proj(x.reshape(bsz * n_real, d), meta_tokens.astype(x.dtype), norm1_g[layer][None, :],
                           w_cat, w_ab, seq)
    proj3 = proj.reshape(bsz, seq, PROJ_WIDTH)

    ymix = _fnet(proj3, *_dft_tables(seq))

    gate_prm = jnp.zeros((SUBLANES, LANES), F32)
    gate_prm = gate_prm.at[0, 0:N_HEADS].set(a_log_fwd[layer]).at[0, 2 * N_HEADS:3 * N_HEADS].set(a_log_bwd[layer])
    gate_prm = gate_prm.at[1, 0:N_HEADS].set(dt_bias_fwd[layer]).at[1, 2 * N_HEADS:3 * N_HEADS].set(dt_bias_bwd[layer])
    og = _gdn(proj3, ab.reshape(bsz, seq, LANES), conv_w[layer], gate_prm, out_norm_g[layer][None, :])

    w_router = jnp.zeros((d, LANES), F32)
    w_router = w_router.at[:, 0:N_GROUPS].set(w_router_group[layer])
    w_router = w_router.at[:, N_GROUPS:N_GROUPS + N_EXPERTS].set(w_router_expert[layer])
    b_router = jnp.zeros((1, LANES), F32)
    b_router = b_router.at[0, 0:N_GROUPS].set(b_router_group[layer])
    b_router = b_router.at[0, N_GROUPS:N_GROUPS + N_EXPERTS].set(b_router_expert[layer])
    tt = _tile(n_tok, (384, 128))
    h1, hn, ridx, rw, cnt = _merge(
        h, og.reshape(n_tok, d), ymix.reshape(n_tok, F_WIDTH), proj,
        w_fourier[layer].astype(MXU_DTYPE), w_delta[layer].astype(MXU_DTYPE), w_out[layer].astype(MXU_DTYPE),
        norm2_g[layer][None, :], w_router, b_router, _tile(n_tok, (688, 384, 128)))

    n_assign = n_tok * TOP_K
    n_blocks = -(-n_assign // EXPERT_BLOCK) + N_EXPERTS
    dest, seg = _plan(ridx, cnt, tt)
    seg_start = seg[0, :N_EXPERTS]
    seg_blocks = seg[SUBLANES, :N_EXPERTS]

    xb = _scatter(dest, hn, jnp.zeros((n_blocks * EXPERT_BLOCK, d // 2), jnp.uint32), tt)
    yb = _experts(seg_start, seg_blocks, xb, w_gate_e[layer], w_up_e[layer], w_down_e[layer])
    out = _combine(dest, yb, h1, rw, final_norm_g[None, :], tt, seq)
    return out.reshape(bsz, n_real, d)
```

```python
import functools
import math

import jax
import jax.numpy as jnp
import numpy as np
from jax import lax
from jax.experimental import pallas as pl
from jax.experimental.pallas import tpu as pltpu

F32 = jnp.float32
MXU_DTYPE = jnp.bfloat16

N_META = 16
CHUNK = 64
N_FGROUPS = 4
FGROUP_DIM = 128
F_WIDTH = N_FGROUPS * FGROUP_DIM
N_HEADS = 8
HEAD_DIM = 128
QK_WIDTH = N_HEADS * HEAD_DIM
CONV_K = 5
N_GROUPS = 8
EXPERTS_PER_GROUP = 8
N_EXPERTS = N_GROUPS * EXPERTS_PER_GROUP
TOP_K = 2
EXPERT_BLOCK = 256
NORM_EPS = 1e-6
LANES = 128
SUBLANES = 8
VMEM_LIMIT = 52 * 1024 * 1024
GDN_GROUP = 11
DMA_UNROLL = 8
EXPERT_X_SLOTS = 4
EXPERT_Y_SLOTS = 3

COL_QKV = 0
COL_Z = 3 * QK_WIDTH
COL_GATE_F = COL_Z + QK_WIDTH
COL_GATE_D = COL_GATE_F + 1024
COL_FIN = COL_GATE_D + 1024
PROJ_WIDTH = COL_FIN + F_WIDTH
PROJ_TN = 3328
PROJ_DTYPE = jnp.bfloat16


def _dot(a, b):
    return jnp.dot(a, b, preferred_element_type=F32)


def _dot_nt(a, b):
    return lax.dot_general(a, b, (((1,), (1,)), ((), ())), preferred_element_type=F32)


def _dot_tn(a, b):
    return lax.dot_general(a, b, (((0,), (0,)), ((), ())), preferred_element_type=F32)


def _split3(x):
    hi = x.astype(MXU_DTYPE)
    r1 = x - hi.astype(F32)
    mid = r1.astype(MXU_DTYPE)
    lo = (r1 - mid.astype(F32)).astype(MXU_DTYPE)
    return hi, mid, lo


def _exact_left(sel, x):
    hi, mid, lo = _split3(x)
    return _dot(sel, hi) + _dot(sel, mid) + _dot(sel, lo)


def _exact_right(x, sel):
    hi, mid, lo = _split3(x)
    return _dot(hi, sel) + _dot(mid, sel) + _dot(lo, sel)


def _pack_rows(x):
    half = x.shape[1] // 2
    lo = lax.bitcast_convert_type(x[:, :half].astype(jnp.bfloat16).astype(F32), jnp.uint32)
    hi = lax.bitcast_convert_type(x[:, half:].astype(jnp.bfloat16).astype(F32), jnp.uint32)
    return (lo >> 16) | hi


def _unpack_rows(u):
    lo = lax.bitcast_convert_type(u << 16, F32)
    hi = lax.bitcast_convert_type(u & jnp.uint32(0xFFFF0000), F32)
    return jnp.concatenate([lo, hi], axis=1)


def _silu(x):
    return x * jax.nn.sigmoid(x)


def _params(sem):
    return pltpu.CompilerParams(dimension_semantics=sem, vmem_limit_bytes=VMEM_LIMIT)


def _inproj_kernel(x_ref, meta_ref, g_ref, w_ref, wab_ref, o_ref, ab_ref, h_ref, hn_even, hn_odd, *, n_tiles):
    step, col = pl.program_id(0), pl.program_id(1)
    tm = h_ref.shape[0]
    cut = -(-(tm // 2) // SUBLANES) * SUBLANES
    nxt = jnp.minimum(step, n_tiles - 1)
    first_half = nxt % 2 == 0
    shift = jnp.where(first_half, N_META, 0)

    def window_rows(lo, hi):
        return x_ref[pl.ds(pl.multiple_of(lo - shift, SUBLANES), hi - lo), :]

    def prepare(lo, hi, hn_ref):
        if lo == 0:
            head = jnp.where(first_half, meta_ref[...], x_ref[0:N_META, :])
            x = jnp.concatenate([head, window_rows(N_META, hi)], axis=0)
        else:
            x = window_rows(lo, hi)
        h_ref[lo:hi, :] = x
        ms = jnp.mean(x * x, axis=-1, keepdims=True)
        hn = (x * lax.rsqrt(ms + NORM_EPS) * g_ref[...]).astype(MXU_DTYPE)
        hn_ref[lo:hi, :] = hn
        ab_ref[lo:hi, :] = _dot(hn, wab_ref[...])

    for parity, (hn_cur, hn_nxt) in enumerate(((hn_odd, hn_even), (hn_even, hn_odd))):
        for c, (lo, hi) in enumerate(((0, cut), (cut, tm))):
            @pl.when((step > 0) & (step % 2 == parity) & (col == c))
            def _():
                o_ref[...] = _dot(hn_cur[...], w_ref[...]).astype(o_ref.dtype)
                prepare(lo, hi, hn_nxt)

    for c, (lo, hi) in enumerate(((0, cut), (cut, tm))):
        @pl.when((step == 0) & (col == c))
        def _():
            prepare(lo, hi, hn_even)


def _in_proj(x2d, meta, gain, w_cat, w_ab, seq):
    n_x, d = x2d.shape
    n_real = seq - N_META
    t = n_x // n_real * seq
    tm = seq // 2
    n_tiles = t // tm
    n = w_cat.shape[1]
    assert seq % (2 * SUBLANES) == 0 and tm >= 2 * N_META and n == 2 * PROJ_TN
    nxt = lambda i: jnp.minimum(i, n_tiles - 1)
    x_spec = pl.BlockSpec(
        (pl.Element(tm), pl.Element(d)),
        lambda i, j: (((nxt(i) // 2) * (n_real // SUBLANES)
                       + (nxt(i) % 2) * ((tm - N_META) // SUBLANES)) * SUBLANES, 0))
    return pl.pallas_call(
        functools.partial(_inproj_kernel, n_tiles=n_tiles),
        grid=(n_tiles + 1, n // PROJ_TN),
        in_specs=[
            x_spec,
            pl.BlockSpec((N_META, d), lambda i, j: (0, 0)),
            pl.BlockSpec((1, d), lambda i, j: (0, 0)),
            pl.BlockSpec((d, PROJ_TN), lambda i, j: (0, j)),
            pl.BlockSpec((d, LANES), lambda i, j: (0, 0)),
        ],
        out_specs=[pl.BlockSpec((tm, PROJ_TN), lambda i, j: (jnp.maximum(i - 1, 0), jnp.where(i > 0, j, 0))),
                   pl.BlockSpec((tm, LANES), lambda i, j: (nxt(i), 0)),
                   pl.BlockSpec((tm, d), lambda i, j: (nxt(i), 0))],
        out_shape=[jax.ShapeDtypeStruct((t, n), PROJ_DTYPE),
                   jax.ShapeDtypeStruct((t, LANES), F32),
                   jax.ShapeDtypeStruct((t, d), F32)],
        scratch_shapes=[pltpu.VMEM((tm, d), MXU_DTYPE), pltpu.VMEM((tm, d), MXU_DTYPE)],
        compiler_params=_params(("arbitrary", "arbitrary")),
        name="in_proj",
    )(x2d, meta, gain, w_cat, w_ab)


def _fnet_kernel(x_ref, cs_ref, tw_ref, flip_ref, o_ref, r_ref, *, seq, kp, nat, scale):
    tail = kp - seq
    if tail:
        r_ref[seq:kp, :] = jnp.zeros((tail, F_WIDTH), MXU_DTYPE)
        r_ref[kp + seq:, :] = jnp.zeros((tail, F_WIDTH), MXU_DTYPE)
    cs = cs_ref[...]
    for g in range(N_FGROUPS):
        cols = slice(g * FGROUP_DIM, (g + 1) * FGROUP_DIM)
        p = _dot(x_ref[:, cols].astype(MXU_DTYPE), cs)
        r_ref[0:seq, cols] = p[:, :FGROUP_DIM].astype(MXU_DTYPE)
        r_ref[kp:kp + seq, cols] = p[:, FGROUP_DIM:].astype(MXU_DTYPE)

    a = _dot(tw_ref[:, :kp], r_ref[0:kp, :])
    b = _dot(tw_ref[:, kp:], r_ref[kp:, :])
    o_ref[0:nat, :] = ((a - b) * scale).astype(o_ref.dtype)
    mirrored = ((a + b) * scale).astype(MXU_DTYPE)
    o_ref[nat:, :] = _dot(flip_ref[...], mirrored).astype(o_ref.dtype)


def _fnet(proj3, cs128, twid, flip):
    b, seq, _ = proj3.shape
    nat, kp = twid.shape[0], twid.shape[1] // 2
    scale = 1.0 / math.sqrt(seq * FGROUP_DIM)
    const = lambda shape: pl.BlockSpec(shape, lambda bi: (0, 0))
    return pl.pallas_call(
        functools.partial(_fnet_kernel, seq=seq, kp=kp, nat=nat, scale=scale),
        grid=(b,),
        in_specs=[
            pl.BlockSpec((None, seq, F_WIDTH), lambda bi: (bi, 0, COL_FIN // F_WIDTH)),
            const((FGROUP_DIM, 2 * FGROUP_DIM)), const((nat, 2 * kp)), const((seq - nat, nat)),
        ],
        out_specs=pl.BlockSpec((None, seq, F_WIDTH), lambda bi: (bi, 0, 0)),
        out_shape=jax.ShapeDtypeStruct((b, seq, F_WIDTH), MXU_DTYPE),
        scratch_shapes=[pltpu.VMEM((2 * kp, F_WIDTH), MXU_DTYPE)],
        compiler_params=_params(("arbitrary",)),
        name="fnet",
    )(proj3, cs128, twid, flip)


def _aligned(x, m):
    return x if isinstance(x, int) else pl.multiple_of(x, m)


def _gdn_kernel(q_ref, k_ref, v_ref, z_ref, ab_ref, cwq_ref, cwk_ref, cwv_ref, prm_ref, ong_ref,
                o_ref,
                rawq_ref, rawk_ref, rawv_ref, qs_ref, ks_ref, vs_ref, gsel_ref, grow_ref, gmix_ref,
                qe_ref, mp_ref, nn_ref, dd_ref, osum_ref,
                *, seq, lp, group, n_heads_total):
    step = pl.program_id(0)
    head = jnp.minimum(step, n_heads_total - 1) % N_HEADS
    nc = lp // CHUNK
    padr = lp - seq
    halo = SUBLANES
    row64 = lax.broadcasted_iota(jnp.int32, (CHUNK, LANES), 0)
    lane64 = lax.broadcasted_iota(jnp.int32, (CHUNK, LANES), 1)

    def for_groups(fn, carry=0):
        n_full, rem = divmod(nc, group)
        if n_full:
            carry = lax.fori_loop(0, n_full, lambda i, c: fn([i * group + j for j in range(group)], c), carry)
        if rem:
            carry = fn([n_full * group + j for j in range(rem)], carry)
        return carry

    cur_half = step % 2
    qe_w, mp_w, nn_w, dd_w, osum_w = (r.at[cur_half] for r in (qe_ref, mp_ref, nn_ref, dd_ref, osum_ref))
    qe_r, mp_r, nn_r, dd_r, osum_r = (r.at[1 - cur_half] for r in (qe_ref, mp_ref, nn_ref, dd_ref, osum_ref))

    streams = ((rawq_ref, q_ref, cwq_ref, qs_ref, HEAD_DIM ** -0.5),
               (rawk_ref, k_ref, cwk_ref, ks_ref, 1.0),
               (rawv_ref, v_ref, cwv_ref, vs_ref, None))
    def stage_conv_inputs():
        for raw_ref, src_ref, _, _, _ in streams:
            raw_ref[0:padr + halo, :] = jnp.zeros((padr + halo, LANES), F32)
            raw_ref[padr + halo:padr + halo + seq, :] = src_ref[...].astype(F32)
            raw_ref[lp + halo:lp + 2 * halo, :] = jnp.zeros((halo, LANES), F32)

    def conv_chunk(c):
        base = _aligned(c * CHUNK, CHUNK)
        for raw_ref, _, cw_ref, dst_ref, l2_scale in streams:
            cw = cw_ref[...]
            acc = None
            for j in range(CONV_K):
                off = halo - (CONV_K - 1) // 2 + j
                tap = cw[j:j + 1, :] * raw_ref[pl.ds(base + off, CHUNK), :]
                acc = tap if acc is None else acc + tap
            y = _silu(acc)
            if isinstance(c, int) and c * CHUNK < padr:
                y = jnp.where(row64 + base >= padr, y, 0.0)
            if l2_scale is not None:
                inv = lax.rsqrt(jnp.sum(y * y, axis=-1, keepdims=True) + NORM_EPS)
                y = y * (inv if l2_scale == 1.0 else inv * l2_scale)
            dst_ref[pl.ds(base, CHUNK), :] = y

    raw_ref = rawq_ref

    def stage_gate_inputs():
        raw_ref[0:padr, :] = jnp.zeros((padr, LANES), F32)
        raw_ref[padr:padr + seq, :] = ab_ref[...]

    neg_a = -jnp.exp(prm_ref[0:1, :])
    dt_bias = prm_ref[1:2, :]
    is_g = (lane64 < 8) | ((lane64 >= 16) & (lane64 < 24))
    is_beta = ((lane64 >= 8) & (lane64 < 16)) | ((lane64 >= 24) & (lane64 < 32))
    r128 = lax.broadcasted_iota(jnp.int32, (2 * CHUNK, CHUNK), 0)
    c128 = lax.broadcasted_iota(jnp.int32, (2 * CHUNK, CHUNK), 1)
    tri2 = jnp.where(((r128 < CHUNK) & (r128 >= c128)) | ((r128 >= CHUNK) & (r128 - CHUNK <= c128)),
                     1.0, 0.0).astype(MXU_DTYPE)
    sel_r = lax.broadcasted_iota(jnp.int32, (LANES, 4 * LANES), 0)
    sel_c = lax.broadcasted_iota(jnp.int32, (LANES, 4 * LANES), 1)
    n_gate_cols = 4 * N_HEADS
    sel_packed = jnp.where((sel_r < 3 * n_gate_cols)
                           & (sel_r % n_gate_cols == head + N_HEADS * (sel_c // LANES)), 1.0, 0.0).astype(MXU_DTYPE)
    tri2x3 = jnp.concatenate([tri2, tri2, tri2], axis=1)

    def gate_batch_group(cs, carry):
        bases = [_aligned(c * CHUNK, CHUNK) for c in cs]
        gates, parts = [], []
        for base in bases:
            ab = raw_ref[pl.ds(base, CHUNK), :]
            g = neg_a * jax.nn.softplus(ab + dt_bias)
            beta = jax.nn.sigmoid(ab)
            gt = jnp.where(is_g, g, jnp.where(is_beta, beta, 0.0))
            gt = jnp.where(row64 + base >= padr, gt, 0.0)
            gates.append(gt)
            parts.append(jnp.concatenate(_split3(gt), axis=0))
        cums = [_dot(tri2x3, p) for p in parts]
        for c, base, cm, gt in zip(cs, bases, cums, gates):
            m = jnp.where(lane64 < 8, cm[:CHUNK], jnp.where((lane64 >= 16) & (lane64 < 24), cm[CHUNK:], gt))
            gmix_ref[pl.ds(base, CHUNK), :] = m
            grow_ref[pl.ds(_aligned(c * LANES, LANES), LANES), :] = jnp.concatenate([m, m], axis=0).T
        return carry

    def gate_head_group(cs, carry):
        rows = len(cs) * CHUNK
        base = _aligned(cs[0] * CHUNK, CHUNK)
        hi, mid, lo = _split3(gmix_ref[pl.ds(base, rows), :])
        packed = (hi.astype(F32) + pltpu.roll(mid.astype(F32), n_gate_cols, 1)
                  + pltpu.roll(lo.astype(F32), 2 * n_gate_cols, 1)).astype(MXU_DTYPE)
        gsel_ref[pl.ds(base, rows), :] = _dot(packed, sel_packed)
        return carry

    fwd_half = lane64 < CHUNK
    col64 = lane64 & (CHUNK - 1)
    incl2 = (fwd_half & (row64 >= col64)) | (~fwd_half & (row64 <= col64))
    strict2 = (fwd_half & (row64 > col64)) | (~fwd_half & (row64 < col64))
    eye2 = jnp.where(row64 == col64, 1.0, 0.0).astype(F32)

    def blockdiag(p):
        return jnp.concatenate([jnp.where(fwd_half, p, 0.0), jnp.where(fwd_half, 0.0, p)],
                               axis=0).astype(MXU_DTYPE)

    n_neumann = int(math.log2(CHUNK)) - 2
    n_stages = n_neumann + 6

    def chunk_group(cs, carry, with_scan):
        n = len(cs)
        bases = [_aligned(c * CHUNK, CHUNK) for c in cs]
        cb128 = [_aligned(c * LANES, LANES) for c in cs]
        cb8 = [_aligned(c * SUBLANES, SUBLANES) for c in cs]
        stage = [0]

        def end_of_stage(carry):
            s = stage[0]
            stage[0] += 1
            if with_scan:
                for j in range(n):
                    if j * n_stages // n == s:
                        carry = scan_step(cs[j], carry)
            return carry

        def gate_cols(j):
            gs = gsel_ref[pl.ds(bases[j], CHUNK), :]
            return gs[:, 0:LANES], gs[:, LANES:2 * LANES], gs[:, 2 * LANES:3 * LANES], gs[:, 3 * LANES:]

        kq = []
        for j in range(n):
            k16 = ks_ref[pl.ds(bases[j], CHUNK), :].astype(MXU_DTYPE)
            q16 = qs_ref[pl.ds(bases[j], CHUNK), :].astype(MXU_DTYPE)
            kq.append(_dot_nt(jnp.concatenate([k16, q16], axis=0), jnp.concatenate([k16, k16], axis=0)))
        carry = end_of_stage(carry)

        ts, ps, qkds = [], [], []
        for j in range(n):
            gf, bf, gb, bb = gate_cols(j)
            row_f = grow_ref[pl.ds(cb128[j] + head, 1), :]
            row_b = grow_ref[pl.ds(cb128[j] + 2 * N_HEADS + head, 1), :]
            g_col = jnp.where(fwd_half, gf, gb)
            g_row = jnp.where(fwd_half, row_f, row_b)
            decay = jnp.where(incl2, jnp.exp(jnp.where(incl2, g_col - g_row, 0.0)), 0.0)
            a2 = jnp.where(strict2, kq[j][:CHUNK] * jnp.where(fwd_half, bf, bb) * decay, 0.0)
            qkds.append(jnp.where(incl2, kq[j][CHUNK:] * decay, 0.0))
            ts.append(eye2 - a2)
            ps.append(a2)

        ps = [_dot(p.astype(MXU_DTYPE), blockdiag(p)) for p in ps]
        carry = end_of_stage(carry)
        for _ in range(n_neumann):
            tps = [_dot(jnp.concatenate([t, p], axis=0).astype(MXU_DTYPE), blockdiag(p)) for t, p in zip(ts, ps)]
            carry = end_of_stage(carry)
            ts = [t + tp[:CHUNK] for t, tp in zip(ts, tps)]
            ps = [tp[CHUNK:] for tp in tps]
        ts = [t + _dot(t.astype(MXU_DTYPE), blockdiag(p)) for t, p in zip(ts, ps)]
        carry = end_of_stage(carry)

        uws = []
        for j in range(n):
            gf, bf, gb, bb = gate_cols(j)
            k = ks_ref[pl.ds(bases[j], CHUNK), :]
            v = vs_ref[pl.ds(bases[j], CHUNK), :]
            x_f = jnp.concatenate([v * bf, k * bf * jnp.exp(gf)], axis=1)
            x_b = jnp.concatenate([v * bb, k * bb * jnp.exp(gb)], axis=1)
            uw = _dot(blockdiag(ts[j]), jnp.concatenate([x_f, x_b], axis=0).astype(MXU_DTYPE))
            uws.append(uw.astype(MXU_DTYPE))
        carry = end_of_stage(carry)

        xxs = [_dot(blockdiag(qkds[j]), uws[j]) for j in range(n)]
        carry = end_of_stage(carry)
        for j in range(n):
            gf, _, gb, _ = gate_cols(j)
            k = ks_ref[pl.ds(bases[j], CHUNK), :]
            q = qs_ref[pl.ds(bases[j], CHUNK), :]
            xx = xxs[j]
            osum_w[pl.ds(bases[j], CHUNK), :] = xx[:CHUNK, :LANES] + xx[CHUNK:, :LANES]
            for d, gc in enumerate((gf, gb)):
                rows = slice(d * CHUNK, (d + 1) * CHUNK)
                g_last = gc[CHUNK - 1:CHUNK, :] if d == 0 else gc[0:1, :]
                k_dec = k * jnp.exp(g_last - gc)
                mn = _dot_tn(k_dec.astype(MXU_DTYPE), uws[j][rows])
                qe_w[d, pl.ds(bases[j], CHUNK), :] = (q * jnp.exp(gc) - xx[rows, LANES:]).astype(MXU_DTYPE)
                nn_w[d, pl.ds(cb128[j], HEAD_DIM), :] = mn[:, :LANES]
                mp_w[d, pl.ds(cb128[j], HEAD_DIM), :] = mn[:, LANES:].astype(MXU_DTYPE)
                dd_w[d, pl.ds(cb8[j], SUBLANES), :] = jnp.broadcast_to(jnp.exp(g_last), (SUBLANES, LANES))
        carry = end_of_stage(carry)
        assert stage[0] == n_stages
        return carry

    def scan_step(i, carry):
        new = []
        for d in range(2):
            s = carry[d]
            c = i if d == 0 else nc - 1 - i
            base = _aligned(c * CHUNK, CHUNK)
            cbase = _aligned(c * HEAD_DIM, HEAD_DIM)
            dbase = _aligned(c * SUBLANES, SUBLANES)
            s16 = s.astype(MXU_DTYPE)
            osum_r[pl.ds(base, CHUNK), :] += _dot(qe_r[d, pl.ds(base, CHUNK), :], s16)
            dec = dd_r[d, pl.ds(dbase, 1), :]
            new.append(dec * s - _dot(mp_r[d, pl.ds(cbase, HEAD_DIM), :], s16)
                       + nn_r[d, pl.ds(cbase, HEAD_DIM), :])
        return tuple(new)

    zero_state = jnp.zeros((HEAD_DIM, HEAD_DIM), F32)

    def write_output():
        gain = ong_ref[...]

        def finish(o, z):
            ms = jnp.mean(o * o, axis=-1, keepdims=True)
            return (o * lax.rsqrt(ms + NORM_EPS) * gain * _silu(z.astype(F32))).astype(o_ref.dtype)

        first = CHUNK - padr if padr else 0
        if first:
            o_ref[0:first, :] = finish(osum_r[padr:CHUNK, :], z_ref[0:first, :])

        def out_body(c, carry):
            base = pl.multiple_of(c * CHUNK, CHUNK)
            dst = pl.multiple_of(c * CHUNK - padr, SUBLANES)
            o_ref[pl.ds(dst, CHUNK), :] = finish(osum_r[pl.ds(base, CHUNK), :], z_ref[pl.ds(dst, CHUNK), :])
            return carry

        lax.fori_loop(1 if first else 0, nc, out_body, 0, unroll=8)

    def conv_only(c, carry):
        conv_chunk(c)
        return carry

    @pl.when(step < n_heads_total)
    def _():
        stage_conv_inputs()
        conv_chunk(0)
        lax.fori_loop(1, nc, conv_only, 0, unroll=4)

    @pl.when((step < n_heads_total) & (head == 0))
    def _():
        stage_gate_inputs()
        for_groups(gate_batch_group)

    @pl.when(step < n_heads_total)
    def _():
        for_groups(gate_head_group)

    @pl.when(step == 0)
    def _():
        for_groups(functools.partial(chunk_group, with_scan=False))

    @pl.when((step > 0) & (step < n_heads_total))
    def _():
        for_groups(functools.partial(chunk_group, with_scan=True), (zero_state, zero_state))
        write_output()

    @pl.when(step == n_heads_total)
    def _():
        lax.fori_loop(0, nc, scan_step, (zero_state, zero_state))
        write_output()


def _gdn(proj3, ab3, conv_w, gate_prm, out_norm_g):
    b, seq, _ = proj3.shape
    lp = -(-seq // CHUNK) * CHUNK
    nc = lp // CHUNK
    n_total = b * N_HEADS
    cur = lambda i: jnp.minimum(i, n_total - 1)
    prev = lambda i: jnp.maximum(i - 1, 0)
    col = lambda off: pl.BlockSpec((None, seq, LANES), lambda i: (cur(i) // N_HEADS, 0, off + cur(i) % N_HEADS))
    cw = lambda off: pl.BlockSpec((CONV_K, LANES), lambda i: (0, off + cur(i) % N_HEADS))
    return pl.pallas_call(
        functools.partial(_gdn_kernel, seq=seq, lp=lp, group=GDN_GROUP, n_heads_total=n_total),
        grid=(n_total + 1,),
        in_specs=[
            col(COL_QKV // LANES), col(COL_QKV // LANES + N_HEADS), col(COL_QKV // LANES + 2 * N_HEADS),
            pl.BlockSpec((None, seq, LANES),
                         lambda i: (prev(i) // N_HEADS, 0, COL_Z // LANES + prev(i) % N_HEADS)),
            pl.BlockSpec((None, seq, LANES), lambda i: (cur(i) // N_HEADS, 0, 0)),
            cw(0), cw(N_HEADS), cw(2 * N_HEADS),
            pl.BlockSpec((SUBLANES, LANES), lambda i: (0, 0)),
            pl.BlockSpec((1, LANES), lambda i: (0, 0)),
        ],
        out_specs=pl.BlockSpec((None, seq, LANES), lambda i: (prev(i) // N_HEADS, 0, prev(i) % N_HEADS)),
        out_shape=jax.ShapeDtypeStruct((b, seq, N_HEADS * HEAD_DIM), MXU_DTYPE),
        scratch_shapes=[
            pltpu.VMEM((lp + 2 * SUBLANES, LANES), F32),
            pltpu.VMEM((lp + 2 * SUBLANES, LANES), F32),
            pltpu.VMEM((lp + 2 * SUBLANES, LANES), F32),
            pltpu.VMEM((lp, LANES), F32),
            pltpu.VMEM((lp, LANES), F32),
            pltpu.VMEM((lp, LANES), F32),
            pltpu.VMEM((lp, 4 * LANES), F32),
            pltpu.VMEM((nc * LANES, LANES), F32),
            pltpu.VMEM((lp, LANES), F32),
            pltpu.VMEM((2, 2, lp, LANES), MXU_DTYPE),
            pltpu.VMEM((2, 2, nc * HEAD_DIM, LANES), MXU_DTYPE),
            pltpu.VMEM((2, 2, nc * HEAD_DIM, LANES), F32),
            pltpu.VMEM((2, 2, nc * SUBLANES, LANES), F32),
            pltpu.VMEM((2, lp, LANES), F32),
        ],
        compiler_params=_params(("arbitrary",)),
        name="gdn",
    )(proj3, proj3, proj3, proj3, ab3, conv_w, conv_w, conv_w, gate_prm, out_norm_g)


def _merge_kernel(h_ref, og_ref, ym_ref, gf_ref, gd_ref, wf_ref, wd_ref, wo_ref, n2_ref, wr_ref, br_ref,
                  h1_ref, hn_ref, ridx_ref, rw_ref, cnt_ref):
    @pl.when(pl.program_id(0) == 0)
    def _():
        cnt_ref[...] = jnp.zeros_like(cnt_ref)

    y_f = _dot(ym_ref[...], wf_ref[...])
    y_d = _dot(og_ref[...], wd_ref[...])
    merged = (jax.nn.sigmoid(gf_ref[...].astype(F32)) * y_f
              + jax.nn.sigmoid(gd_ref[...].astype(F32)) * y_d)
    h1 = h_ref[...] + _dot(merged.astype(MXU_DTYPE), wo_ref[...])
    h1_ref[...] = h1
    ms = jnp.mean(h1 * h1, axis=-1, keepdims=True)
    hn = h1 * lax.rsqrt(ms + NORM_EPS) * n2_ref[...]
    hn_ref[...] = _pack_rows(hn)

    x_hi = hn.astype(MXU_DTYPE)
    x_lo = (hn - x_hi.astype(F32)).astype(MXU_DTYPE)
    w = wr_ref[...]
    w_hi = w.astype(MXU_DTYPE)
    w_lo = (w - w_hi.astype(F32)).astype(MXU_DTYPE)
    hi_terms = _dot(x_hi, jnp.concatenate([w_hi, w_lo], axis=1))
    logits = hi_terms[:, :LANES] + hi_terms[:, LANES:] + _dot(x_lo, w_hi) + br_ref[...]

    tm = logits.shape[0]
    lane = lax.broadcasted_iota(jnp.int32, (tm, LANES), 1)
    neg_inf = -jnp.inf
    is_group = lane < N_GROUPS
    gl = jnp.where(is_group, logits, neg_inf)
    ge = jnp.exp(gl - jnp.max(gl, axis=-1, keepdims=True))
    gp = ge / jnp.sum(ge, axis=-1, keepdims=True)
    p_group = jnp.max(gp, axis=-1, keepdims=True)
    g_idx = jnp.min(jnp.where(is_group & (gp == p_group), lane, LANES), axis=-1, keepdims=True)

    in_group = (lane >= N_GROUPS) & (lane < N_GROUPS + N_EXPERTS) & (((lane - N_GROUPS) >> 3) == g_idx)
    el = jnp.where(in_group, logits, neg_inf)
    ee = jnp.exp(el - jnp.max(el, axis=-1, keepdims=True))
    ep = ee / jnp.sum(ee, axis=-1, keepdims=True)
    v1 = jnp.max(jnp.where(in_group, ep, -1.0), axis=-1, keepdims=True)
    i1 = jnp.min(jnp.where(in_group & (ep == v1), lane, LANES), axis=-1, keepdims=True)
    rest = in_group & (lane != i1)
    v2 = jnp.max(jnp.where(rest, ep, -1.0), axis=-1, keepdims=True)
    i2 = jnp.min(jnp.where(rest & (ep == v2), lane, LANES), axis=-1, keepdims=True)
    denom = v1 + v2
    w1 = p_group * (v1 / denom)
    w2 = p_group * (v2 / denom)
    e1 = i1 - N_GROUPS
    e2 = i2 - N_GROUPS
    ridx_ref[...] = jnp.where(lane == 0, e1, jnp.where(lane == 1, e2, 0))
    rw_ref[...] = jnp.where(lane == 0, w1, jnp.where(lane == 1, w2, 0.0))
    onehots = jnp.where((lane == e1) | (lane == e2), 1.0, 0.0)
    cnt_ref[...] += jnp.broadcast_to(jnp.sum(onehots, axis=0, keepdims=True), cnt_ref.shape)


def _merge(h2d, og2d, ymix2d, proj2d, w_fourier, w_delta, w_out, norm2_g, w_router, b_router, tm):
    t, d = h2d.shape
    row = lambda w: pl.BlockSpec((tm, w), lambda i: (i, 0))
    const = lambda shape: pl.BlockSpec(shape, lambda i: (0, 0))
    return pl.pallas_call(
        _merge_kernel,
        grid=(t // tm,),
        in_specs=[
            row(d), row(d), row(F_WIDTH),
            pl.BlockSpec((tm, d), lambda i: (i, COL_GATE_F // d)),
            pl.BlockSpec((tm, d), lambda i: (i, COL_GATE_D // d)),
            const((F_WIDTH, d)), const((d, d)), const((d, d)), const((1, d)),
            const((d, LANES)), const((1, LANES)),
        ],
        out_specs=[row(d), row(d // 2), row(LANES), row(LANES), const((SUBLANES, LANES))],
        out_shape=[
            jax.ShapeDtypeStruct((t, d), F32),
            jax.ShapeDtypeStruct((t, d // 2), jnp.uint32),
            jax.ShapeDtypeStruct((t, LANES), jnp.int32),
            jax.ShapeDtypeStruct((t, LANES), F32),
            jax.ShapeDtypeStruct((SUBLANES, LANES), F32),
        ],
        compiler_params=_params(("arbitrary",)),
        name="merge",
    )(h2d, og2d, ymix2d, proj2d, proj2d, w_fourier, w_delta, w_out, norm2_g, w_router, b_router)


def _plan_kernel(ridx_ref, cnt_ref, dest_ref, seg_ref, base_ref, run_ref):
    tt = ridx_ref.shape[0]

    @pl.when(pl.program_id(0) == 0)
    def _():
        cnt = cnt_ref[...].astype(jnp.int32)
        n_blk = (cnt + (EXPERT_BLOCK - 1)) // EXPERT_BLOCK
        padded = (n_blk * EXPERT_BLOCK).astype(F32)
        r = lax.broadcasted_iota(jnp.int32, (LANES, LANES), 0)
        c = lax.broadcasted_iota(jnp.int32, (LANES, LANES), 1)
        before = jnp.where(r < c, 1.0, 0.0).astype(MXU_DTYPE)
        start = _exact_right(padded, before)
        base_ref[...] = start
        run_ref[...] = jnp.zeros_like(run_ref)
        seg_ref[0:SUBLANES, :] = start.astype(jnp.int32)
        seg_ref[SUBLANES:2 * SUBLANES, :] = n_blk
        seg_ref[2 * SUBLANES:, :] = cnt

    lane = lax.broadcasted_iota(jnp.int32, (tt, LANES), 1)
    ridx = ridx_ref[...]
    oh0 = jnp.where(lane == ridx[:, 0:1], 1.0, 0.0)
    oh1 = jnp.where(lane == ridx[:, 1:2], 1.0, 0.0)
    r = lax.broadcasted_iota(jnp.int32, (tt, tt), 0)
    c = lax.broadcasted_iota(jnp.int32, (tt, tt), 1)
    earlier = jnp.where(c < r, 1.0, 0.0).astype(MXU_DTYPE)
    prefix = _dot(earlier, jnp.concatenate([oh0, oh1], axis=1).astype(MXU_DTYPE))
    c0 = jnp.sum(oh0, axis=0, keepdims=True)
    c1 = jnp.sum(oh1, axis=0, keepdims=True)
    first = base_ref[0:1, :] + run_ref[0:1, :]
    d0 = jnp.sum(oh0 * (prefix[:, :LANES] + first), axis=-1, keepdims=True)
    d1 = jnp.sum(oh1 * (prefix[:, LANES:] + first + c0), axis=-1, keepdims=True)
    run_ref[...] += jnp.broadcast_to(c0 + c1, run_ref.shape)
    both = jnp.where(lane == 0, d0, jnp.where(lane == 1, d1, 0.0))
    both_t = both.T.astype(jnp.int32)
    for k in range(TOP_K):
        dest_ref[k] = both_t[k:k + 1, :]


def _plan(ridx, cnt, tt):
    t = ridx.shape[0]
    return pl.pallas_call(
        _plan_kernel,
        grid=(t // tt,),
        in_specs=[pl.BlockSpec((tt, LANES), lambda i: (i, 0)),
                  pl.BlockSpec((SUBLANES, LANES), lambda i: (0, 0))],
        out_specs=[pl.BlockSpec((TOP_K, None, 1, tt), lambda i: (0, i, 0, 0)),
                   pl.BlockSpec((3 * SUBLANES, LANES), lambda i: (0, 0))],
        out_shape=[jax.ShapeDtypeStruct((TOP_K, t // tt, 1, tt), jnp.int32),
                   jax.ShapeDtypeStruct((3 * SUBLANES, LANES), jnp.int32)],
        scratch_shapes=[pltpu.VMEM((SUBLANES, LANES), F32), pltpu.VMEM((SUBLANES, LANES), F32)],
        compiler_params=_params(("arbitrary",)),
        name="plan",
    )(ridx, cnt)


def _scatter_kernel(dest0_ref, dest1_ref, hn_ref, xb_in_ref, xb_ref, sem, *, ts):
    dest_refs = (dest0_ref, dest1_ref)
    del xb_in_ref

    def issue(t, carry):
        src = hn_ref.at[pl.ds(t, 1)]
        for k in range(TOP_K):
            pltpu.make_async_copy(src, xb_ref.at[pl.ds(dest_refs[k][0, t], 1)], sem).start(priority=k % 2)
        return carry

    lax.fori_loop(0, ts, issue, 0, unroll=DMA_UNROLL)
    for _ in range(TOP_K):
        pltpu.make_async_copy(hn_ref, xb_ref.at[pl.ds(0, ts)], sem).wait()


def _scatter(dest_flat, hn2d, xb_zero, ts):
    t, d = hn2d.shape
    return pl.pallas_call(
        functools.partial(_scatter_kernel, ts=ts),
        grid=(t // ts,),
        in_specs=[pl.BlockSpec((None, None, 1, ts), lambda i: (0, i, 0, 0), memory_space=pltpu.SMEM),
                  pl.BlockSpec((None, None, 1, ts), lambda i: (1, i, 0, 0), memory_space=pltpu.SMEM),
                  pl.BlockSpec((ts, d), lambda i: (i, 0)),
                  pl.BlockSpec(memory_space=pl.ANY)],
        out_specs=pl.BlockSpec(memory_space=pl.ANY),
        out_shape=jax.ShapeDtypeStruct(xb_zero.shape, xb_zero.dtype),
        scratch_shapes=[pltpu.SemaphoreType.DMA(())],
        input_output_aliases={3: 0},
        compiler_params=_params(("arbitrary",)),
        name="scatter",
    )(dest_flat, dest_flat, hn2d, xb_zero)


def _expert_kernel(start_ref, nblk_ref, xb_ref, wg_ref, wu_ref, wd_ref, yb_ref,
                   wg16_ref, wu16_ref, wd16_ref, xbuf_ref, ybuf_ref, sem_in, sem_out):
    expert = pl.program_id(0)
    last = pl.num_programs(0) - 1
    n = nblk_ref[expert]
    first = start_ref[expert] // EXPERT_BLOCK
    total = start_ref[last] // EXPERT_BLOCK + nblk_ref[last]

    def rows(g):
        return pl.ds(pl.multiple_of(g * EXPERT_BLOCK, EXPERT_BLOCK), EXPERT_BLOCK)

    def x_copy(g):
        slot = g % EXPERT_X_SLOTS
        return pltpu.make_async_copy(xb_ref.at[rows(g)], xbuf_ref.at[slot], sem_in.at[slot])

    def y_copy(g):
        slot = g % EXPERT_Y_SLOTS
        return pltpu.make_async_copy(ybuf_ref.at[slot], yb_ref.at[rows(g)], sem_out.at[slot])

    @pl.when(n > 0)
    def _():
        @pl.when(first == 0)
        def _():
            for j in range(EXPERT_X_SLOTS - 1):
                @pl.when(j < total)
                def _():
                    x_copy(j).start()

        wg16_ref[...] = wg_ref[...].astype(MXU_DTYPE)
        wu16_ref[...] = wu_ref[...].astype(MXU_DTYPE)
        wd16_ref[...] = wd_ref[...].astype(MXU_DTYPE)

        def body(i, carry):
            g = first + i

            @pl.when(g + (EXPERT_X_SLOTS - 1) < total)
            def _():
                x_copy(g + (EXPERT_X_SLOTS - 1)).start()

            x_copy(g).wait()

            @pl.when(g >= EXPERT_Y_SLOTS)
            def _():
                y_copy(g - EXPERT_Y_SLOTS).wait()

            x = _unpack_rows(xbuf_ref[g % EXPERT_X_SLOTS]).astype(MXU_DTYPE)
            gate = _dot(x, wg16_ref[...])
            up = _dot(x, wu16_ref[...])
            ybuf_ref[g % EXPERT_Y_SLOTS] = _pack_rows(_dot((_silu(gate) * up).astype(MXU_DTYPE), wd16_ref[...]))
            y_copy(g).start()
            return carry

        lax.fori_loop(0, n, body, 0)

    @pl.when(expert == last)
    def _():
        for j in range(EXPERT_Y_SLOTS):
            @pl.when(total > j)
            def _():
                y_copy(total - 1 - j).wait()


def _experts(seg_start, seg_blocks, xb, w_gate, w_up, w_down):
    n_slots, row_words = xb.shape
    n_exp, d, de = w_gate.shape
    grid_spec = pltpu.PrefetchScalarGridSpec(
        num_scalar_prefetch=2,
        grid=(n_exp,),
        in_specs=[
            pl.BlockSpec(memory_space=pl.ANY),
            pl.BlockSpec((None, d, de), lambda e, st, nb: (e, 0, 0)),
            pl.BlockSpec((None, d, de), lambda e, st, nb: (e, 0, 0)),
            pl.BlockSpec((None, de, d), lambda e, st, nb: (e, 0, 0)),
        ],
        out_specs=pl.BlockSpec(memory_space=pl.ANY),
        scratch_shapes=[
            pltpu.VMEM((d, de), MXU_DTYPE), pltpu.VMEM((d, de), MXU_DTYPE), pltpu.VMEM((de, d), MXU_DTYPE),
            pltpu.VMEM((EXPERT_X_SLOTS, EXPERT_BLOCK, row_words), jnp.uint32),
            pltpu.VMEM((EXPERT_Y_SLOTS, EXPERT_BLOCK, row_words), jnp.uint32),
            pltpu.SemaphoreType.DMA((EXPERT_X_SLOTS,)), pltpu.SemaphoreType.DMA((EXPERT_Y_SLOTS,)),
        ],
    )
    return pl.pallas_call(
        _expert_kernel,
        grid_spec=grid_spec,
        out_shape=jax.ShapeDtypeStruct((n_slots, row_words), jnp.uint32),
        input_output_aliases={2: 0},
        compiler_params=_params(("arbitrary",)),
        name="experts",
    )(seg_start, seg_blocks, xb, w_gate, w_up, w_down)


def _combine_kernel(dest0_ref, dest1_ref, yb_ref, h1_ref, rw_ref, fg_ref, o_ref, buf_ref, res_ref, sem, out_sem,
                    *, tc, groups_per_batch):
    step = pl.program_id(0)
    n_tiles = pl.num_programs(0) - 1
    n_grp = tc // N_META
    cur = step % 2
    prev = 1 - cur

    def output_copies(tile, action):
        slot = tile % 2
        first = tile * n_grp
        first_in_batch = first % groups_per_batch
        has_meta = (first_in_batch == 0) | (first_in_batch + n_grp > groups_per_batch)

        def copy_groups(q, n):
            grp = first + q
            dst = pl.multiple_of((grp - grp // groups_per_batch - 1) * N_META, N_META)
            copy = pltpu.make_async_copy(res_ref.at[slot, pl.ds(q * N_META, n * N_META)],
                                         o_ref.at[pl.ds(dst, n * N_META)], out_sem.at[slot])
            copy.start() if action == "start" else copy.wait()

        @pl.when(jnp.logical_not(has_meta))
        def _():
            copy_groups(0, n_grp)

        @pl.when(has_meta)
        def _():
            for q in range(n_grp):
                @pl.when((first + q) % groups_per_batch != 0)
                def _():
                    copy_groups(q, 1)

    def request_rows(i):
        for u in range(SUBLANES):
            t = i * SUBLANES + u
            for k in range(TOP_K):
                pltpu.make_async_copy(yb_ref.at[pl.ds((dest0_ref, dest1_ref)[k][0, t], 1)],
                                      buf_ref.at[cur, k, pl.ds(t, 1)], sem.at[cur]).start(priority=k % 2)

    def finish_rows(i):
        rows = pl.ds(pl.multiple_of(i * SUBLANES, SUBLANES), SUBLANES)
        rw = rw_ref[rows, :]
        h2 = h1_ref[rows, :] + (_unpack_rows(buf_ref[prev, 0, rows, :]) * rw[:, 0:1]
                                + _unpack_rows(buf_ref[prev, 1, rows, :]) * rw[:, 1:2])
        ms = jnp.mean(h2 * h2, axis=-1, keepdims=True)
        res_ref[prev, rows, :] = h2 * lax.rsqrt(ms + NORM_EPS) * fg_ref[...]

    def begin_finish():
        for k in range(TOP_K):
            pltpu.make_async_copy(yb_ref.at[pl.ds(0, tc)], buf_ref.at[prev, k], sem.at[prev]).wait()

        @pl.when(step >= 3)
        def _():
            output_copies(step - 3, "wait")

    def both(i, carry):
        request_rows(i)
        finish_rows(i)
        return carry

    def request_only(i, carry):
        request_rows(i)
        return carry

    def finish_only(i, carry):
        finish_rows(i)
        return carry

    @pl.when(step == 0)
    def _():
        lax.fori_loop(0, tc // SUBLANES, request_only, 0)

    @pl.when((step > 0) & (step < n_tiles))
    def _():
        begin_finish()
        lax.fori_loop(0, tc // SUBLANES, both, 0, unroll=8)
        output_copies(step - 1, "start")

    @pl.when(step == n_tiles)
    def _():
        begin_finish()
        lax.fori_loop(0, tc // SUBLANES, finish_only, 0, unroll=4)
        output_copies(step - 1, "start")

        @pl.when(step >= 2)
        def _():
            output_copies(step - 2, "wait")

        output_copies(step - 1, "wait")


def _combine(dest_flat, yb, h1, rw, final_g, tc, seq):
    t, d = h1.shape
    assert tc % N_META == 0 and seq % N_META == 0
    n_out = t - (t // seq) * N_META
    n_tiles = t // tc
    nxt = lambda i: jnp.minimum(i, n_tiles - 1)
    fin = lambda i: jnp.maximum(i - 1, 0)
    return pl.pallas_call(
        functools.partial(_combine_kernel, tc=tc, groups_per_batch=seq // N_META),
        grid=(n_tiles + 1,),
        in_specs=[pl.BlockSpec((None, None, 1, tc), lambda i: (0, nxt(i), 0, 0), memory_space=pltpu.SMEM),
                  pl.BlockSpec((None, None, 1, tc), lambda i: (1, nxt(i), 0, 0), memory_space=pltpu.SMEM),
                  pl.BlockSpec(memory_space=pl.ANY),
                  pl.BlockSpec((tc, d), lambda i: (fin(i), 0)),
                  pl.BlockSpec((tc, LANES), lambda i: (fin(i), 0)),
                  pl.BlockSpec((1, d), lambda i: (0, 0))],
        out_specs=pl.BlockSpec(memory_space=pl.ANY),
        out_shape=jax.ShapeDtypeStruct((n_out, d), F32),
        scratch_shapes=[pltpu.VMEM((2, TOP_K, tc, d // 2), jnp.uint32), pltpu.VMEM((2, tc, d), F32),
                        pltpu.SemaphoreType.DMA((2,)), pltpu.SemaphoreType.DMA((2,))],
        compiler_params=_params(("arbitrary",)),
        name="combine",
    )(dest_flat, dest_flat, yb, h1, rw, final_g)


def _tile(total, candidates):
    for c in candidates:
        if total % c == 0:
            return c
    raise ValueError(f"no tile for {total}")


def _dft_tables(seq):
    kp = -(-seq // LANES) * LANES
    nat = -(-(seq // 2 + 1) // 16) * 16
    mir = seq - nat
    c = np.arange(FGROUP_DIM, dtype=np.int64)
    ang_c = (2.0 * np.pi / FGROUP_DIM) * ((c[:, None] * c[None, :]) % FGROUP_DIM)
    cs128 = np.concatenate([np.cos(ang_c), np.sin(ang_c)], axis=1)
    k = np.arange(nat, dtype=np.int64)
    n = np.arange(seq, dtype=np.int64)
    ang_l = (2.0 * np.pi / seq) * ((k[:, None] * n[None, :]) % seq)
    twid = np.zeros((nat, 2 * kp), np.float32)
    twid[:, :seq] = np.cos(ang_l)
    twid[:, kp:kp + seq] = np.sin(ang_l)
    flip = np.zeros((mir, nat), np.float32)
    flip[np.arange(mir), mir - np.arange(mir)] = 1.0
    as_const = lambda t: jnp.asarray(t.astype(MXU_DTYPE))
    return as_const(cs128), as_const(twid), as_const(flip)


def kernel(x, meta_tokens, norm1_g, w_in, conv_w, a_log_fwd, dt_bias_fwd, a_log_bwd, dt_bias_bwd, out_norm_g, w_fourier, w_delta, w_out, norm2_g, w_router_group, b_router_group, w_router_expert, b_router_expert, w_gate_e, w_up_e, w_down_e, final_norm_g):
    bsz, n_real, d = x.shape
    seq = N_META + n_real
    n_tok = bsz * seq
    assert (CHUNK - N_META) % CHUNK + seq == -(-seq // CHUNK) * CHUNK, "meta tokens must fill the tail of chunk 0"
    assert w_in.shape[0] == 1, "single trunk layer"
    layer = 0


    w = w_in[layer]
    o_f, o_qkv, o_z, o_ab, o_gf, o_gd = 0, F_WIDTH, F_WIDTH + 3 * QK_WIDTH, F_WIDTH + 4 * QK_WIDTH, \
        F_WIDTH + 4 * QK_WIDTH + 4 * N_HEADS, F_WIDTH + 4 * QK_WIDTH + 4 * N_HEADS + d
    w_cat = jnp.concatenate(
        [w[:, o_qkv:o_z], w[:, o_z:o_ab], w[:, o_gf:o_gd], w[:, o_gd:o_gd + d], w[:, o_f:o_qkv]],
        axis=1).astype(MXU_DTYPE)
    w_ab = jnp.pad(w[:, o_ab:o_gf], ((0, 0), (0, LANES - 4 * N_HEADS))).astype(MXU_DTYPE)

    proj, ab, h = _in_proj(x.reshape(bsz * n_real, d), meta_tokens.astype(x.dtype), norm1_g[layer][None, :],
                           w_cat, w_ab, seq)
    proj3 = proj.reshape(bsz, seq, PROJ_WIDTH)

    ymix = _fnet(proj3, *_dft_tables(seq))

    gate_prm = jnp.zeros((SUBLANES, LANES), F32)
    gate_prm = gate_prm.at[0, 0:N_HEADS].set(a_log_fwd[layer]).at[0, 2 * N_HEADS:3 * N_HEADS].set(a_log_bwd[layer])
    gate_prm = gate_prm.at[1, 0:N_HEADS].set(dt_bias_fwd[layer]).at[1, 2 * N_HEADS:3 * N_HEADS].set(dt_bias_bwd[layer])
    og = _gdn(proj3, ab.reshape(bsz, seq, LANES), conv_w[layer], gate_prm, out_norm_g[layer][None, :])

    w_router = jnp.zeros((d, LANES), F32)
    w_router = w_router.at[:, 0:N_GROUPS].set(w_router_group[layer])
    w_router = w_router.at[:, N_GROUPS:N_GROUPS + N_EXPERTS].set(w_router_expert[layer])
    b_router = jnp.zeros((1, LANES), F32)
    b_router = b_router.at[0, 0:N_GROUPS].set(b_router_group[layer])
    b_router = b_router.at[0, N_GROUPS:N_GROUPS + N_EXPERTS].set(b_router_expert[layer])
    tt = _tile(n_tok, (384, 128))
    h1, hn, ridx, rw, cnt = _merge(
        h, og.reshape(n_tok, d), ymix.reshape(n_tok, F_WIDTH), proj,
        w_fourier[layer].astype(MXU_DTYPE), w_delta[layer].astype(MXU_DTYPE), w_out[layer].astype(MXU_DTYPE),
        norm2_g[layer][None, :], w_router, b_router, _tile(n_tok, (688, 384, 128)))

    n_assign = n_tok * TOP_K
    n_blocks = -(-n_assign // EXPERT_BLOCK) + N_EXPERTS
    dest, seg = _plan(ridx, cnt, tt)
    seg_start = seg[0, :N_EXPERTS]
    seg_blocks = seg[SUBLANES, :N_EXPERTS]

    xb = _scatter(dest, hn, jnp.zeros((n_blocks * EXPERT_BLOCK, d // 2), jnp.uint32), tt)
    yb = _experts(seg_start, seg_blocks, xb, w_gate_e[layer], w_up_e[layer], w_down_e[layer])
    out = _combine(dest, yb, h1, rw, final_norm_g[None, :], tt, seq)
    return out.reshape(bsz, n_real, d)
```

```python
import functools
import math

import jax
import jax.numpy as jnp
import numpy as np
from jax import lax
from jax.experimental import pallas as pl
from jax.experimental.pallas import tpu as pltpu

F32 = jnp.float32
MXU_DTYPE = jnp.bfloat16

N_META = 16
CHUNK = 64
N_FGROUPS = 4
FGROUP_DIM = 128
F_WIDTH = N_FGROUPS * FGROUP_DIM
N_HEADS = 8
HEAD_DIM = 128
QK_WIDTH = N_HEADS * HEAD_DIM
CONV_K = 5
N_GROUPS = 8
EXPERTS_PER_GROUP = 8
N_EXPERTS = N_GROUPS * EXPERTS_PER_GROUP
TOP_K = 2
EXPERT_BLOCK = 256
NORM_EPS = 1e-6
LANES = 128
SUBLANES = 8
VMEM_LIMIT = 52 * 1024 * 1024
GDN_GROUP = 11
DMA_UNROLL = 16
EXPERT_X_SLOTS = 4
EXPERT_Y_SLOTS = 3

COL_QKV = 0
COL_Z = 3 * QK_WIDTH
COL_GATE_F = COL_Z + QK_WIDTH
COL_GATE_D = COL_GATE_F + 1024
COL_FIN = COL_GATE_D + 1024
PROJ_WIDTH = COL_FIN + F_WIDTH
PROJ_TN = 3328
PROJ_DTYPE = jnp.bfloat16


def _dot(a, b):
    return jnp.dot(a, b, preferred_element_type=F32)


def _dot_nt(a, b):
    return lax.dot_general(a, b, (((1,), (1,)), ((), ())), preferred_element_type=F32)


def _dot_tn(a, b):
    return lax.dot_general(a, b, (((0,), (0,)), ((), ())), preferred_element_type=F32)


def _split3(x):
    hi = x.astype(MXU_DTYPE)
    r1 = x - hi.astype(F32)
    mid = r1.astype(MXU_DTYPE)
    lo = (r1 - mid.astype(F32)).astype(MXU_DTYPE)
    return hi, mid, lo


def _exact_left(sel, x):
    hi, mid, lo = _split3(x)
    return _dot(sel, hi) + _dot(sel, mid) + _dot(sel, lo)


def _exact_right(x, sel):
    hi, mid, lo = _split3(x)
    return _dot(hi, sel) + _dot(mid, sel) + _dot(lo, sel)


def _pack_rows(x):
    half = x.shape[1] // 2
    lo = lax.bitcast_convert_type(x[:, :half].astype(jnp.bfloat16).astype(F32), jnp.uint32)
    hi = lax.bitcast_convert_type(x[:, half:].astype(jnp.bfloat16).astype(F32), jnp.uint32)
    return (lo >> 16) | hi


def _unpack_rows(u):
    lo = lax.bitcast_convert_type(u << 16, F32)
    hi = lax.bitcast_convert_type(u & jnp.uint32(0xFFFF0000), F32)
    return jnp.concatenate([lo, hi], axis=1)


def _silu(x):
    return x * jax.nn.sigmoid(x)


def _params(sem):
    return pltpu.CompilerParams(dimension_semantics=sem, vmem_limit_bytes=VMEM_LIMIT)


def _inproj_kernel(x_ref, meta_ref, g_ref, w_ref, wab_ref, o_ref, ab_ref, h_ref, hn_ref):
    @pl.when(pl.program_id(1) == 0)
    def _():
        tm = h_ref.shape[0]

        @pl.when(pl.program_id(0) % 2 == 0)
        def _():
            h_ref[0:N_META, :] = meta_ref[...]
            h_ref[N_META:tm, :] = x_ref[0:tm - N_META, :]

        @pl.when(pl.program_id(0) % 2 == 1)
        def _():
            h_ref[...] = x_ref[...]

        x = h_ref[...]
        ms = jnp.mean(x * x, axis=-1, keepdims=True)
        hn_ref[...] = (x * lax.rsqrt(ms + NORM_EPS) * g_ref[...]).astype(MXU_DTYPE)
        ab_ref[...] = _dot(hn_ref[...], wab_ref[...])

    o_ref[...] = _dot(hn_ref[...], w_ref[...]).astype(o_ref.dtype)


def _in_proj(x2d, meta, gain, w_cat, w_ab, seq):
    n_x, d = x2d.shape
    n_real = seq - N_META
    t = n_x // n_real * seq
    tm = seq // 2
    assert seq % (2 * SUBLANES) == 0 and tm >= N_META
    n = w_cat.shape[1]
    x_spec = pl.BlockSpec(
        (pl.Element(tm), pl.Element(d)),
        lambda i, j: (((i // 2) * (n_real // SUBLANES) + (i % 2) * ((tm - N_META) // SUBLANES)) * SUBLANES, 0))
    return pl.pallas_call(
        _inproj_kernel,
        grid=(t // tm, n // PROJ_TN),
        in_specs=[
            x_spec,
            pl.BlockSpec((N_META, d), lambda i, j: (0, 0)),
            pl.BlockSpec((1, d), lambda i, j: (0, 0)),
            pl.BlockSpec((d, PROJ_TN), lambda i, j: (0, j)),
            pl.BlockSpec((d, LANES), lambda i, j: (0, 0)),
        ],
        out_specs=[pl.BlockSpec((tm, PROJ_TN), lambda i, j: (i, j)),
                   pl.BlockSpec((tm, LANES), lambda i, j: (i, 0)),
                   pl.BlockSpec((tm, d), lambda i, j: (i, 0))],
        out_shape=[jax.ShapeDtypeStruct((t, n), PROJ_DTYPE),
                   jax.ShapeDtypeStruct((t, LANES), F32),
                   jax.ShapeDtypeStruct((t, d), F32)],
        scratch_shapes=[pltpu.VMEM((tm, d), MXU_DTYPE)],
        compiler_params=_params(("arbitrary", "arbitrary")),
        name="in_proj",
    )(x2d, meta, gain, w_cat, w_ab)


def _fnet_kernel(x_ref, cs_ref, tw_ref, flip_ref, o_ref, r_ref, *, seq, kp, nat, scale):
    tail = kp - seq
    if tail:
        r_ref[seq:kp, :] = jnp.zeros((tail, F_WIDTH), MXU_DTYPE)
        r_ref[kp + seq:, :] = jnp.zeros((tail, F_WIDTH), MXU_DTYPE)
    cs = cs_ref[...]
    for g in range(N_FGROUPS):
        cols = slice(g * FGROUP_DIM, (g + 1) * FGROUP_DIM)
        p = _dot(x_ref[:, cols].astype(MXU_DTYPE), cs)
        r_ref[0:seq, cols] = p[:, :FGROUP_DIM].astype(MXU_DTYPE)
        r_ref[kp:kp + seq, cols] = p[:, FGROUP_DIM:].astype(MXU_DTYPE)

    a = _dot(tw_ref[:, :kp], r_ref[0:kp, :])
    b = _dot(tw_ref[:, kp:], r_ref[kp:, :])
    o_ref[0:nat, :] = ((a - b) * scale).astype(o_ref.dtype)
    mirrored = ((a + b) * scale).astype(MXU_DTYPE)
    o_ref[nat:, :] = _dot(flip_ref[...], mirrored).astype(o_ref.dtype)


def _fnet(proj3, cs128, twid, flip):
    b, seq, _ = proj3.shape
    nat, kp = twid.shape[0], twid.shape[1] // 2
    scale = 1.0 / math.sqrt(seq * FGROUP_DIM)
    const = lambda shape: pl.BlockSpec(shape, lambda bi: (0, 0))
    return pl.pallas_call(
        functools.partial(_fnet_kernel, seq=seq, kp=kp, nat=nat, scale=scale),
        grid=(b,),
        in_specs=[
            pl.BlockSpec((None, seq, F_WIDTH), lambda bi: (bi, 0, COL_FIN // F_WIDTH)),
            const((FGROUP_DIM, 2 * FGROUP_DIM)), const((nat, 2 * kp)), const((seq - nat, nat)),
        ],
        out_specs=pl.BlockSpec((None, seq, F_WIDTH), lambda bi: (bi, 0, 0)),
        out_shape=jax.ShapeDtypeStruct((b, seq, F_WIDTH), MXU_DTYPE),
        scratch_shapes=[pltpu.VMEM((2 * kp, F_WIDTH), MXU_DTYPE)],
        compiler_params=_params(("arbitrary",)),
        name="fnet",
    )(proj3, cs128, twid, flip)


def _aligned(x, m):
    return x if isinstance(x, int) else pl.multiple_of(x, m)


def _gdn_kernel(q_ref, k_ref, v_ref, z_ref, ab_ref, cwq_ref, cwk_ref, cwv_ref, prm_ref, ong_ref,
                o_ref,
                rawq_ref, rawk_ref, rawv_ref, qs_ref, ks_ref, vs_ref, gsel_ref, grow_ref, gmix_ref,
                qe_ref, mp_ref, nn_ref, dd_ref, osum_ref,
                *, seq, lp, group, n_heads_total):
    step = pl.program_id(0)
    head = jnp.minimum(step, n_heads_total - 1) % N_HEADS
    nc = lp // CHUNK
    padr = lp - seq
    halo = SUBLANES
    row64 = lax.broadcasted_iota(jnp.int32, (CHUNK, LANES), 0)
    lane64 = lax.broadcasted_iota(jnp.int32, (CHUNK, LANES), 1)

    def for_groups(fn, carry=0):
        n_full, rem = divmod(nc, group)
        if n_full:
            carry = lax.fori_loop(0, n_full, lambda i, c: fn([i * group + j for j in range(group)], c), carry)
        if rem:
            carry = fn([n_full * group + j for j in range(rem)], carry)
        return carry

    cur_half = step % 2
    qe_w, mp_w, nn_w, dd_w, osum_w = (r.at[cur_half] for r in (qe_ref, mp_ref, nn_ref, dd_ref, osum_ref))
    qe_r, mp_r, nn_r, dd_r, osum_r = (r.at[1 - cur_half] for r in (qe_ref, mp_ref, nn_ref, dd_ref, osum_ref))

    streams = ((rawq_ref, q_ref, cwq_ref, qs_ref, HEAD_DIM ** -0.5),
               (rawk_ref, k_ref, cwk_ref, ks_ref, 1.0),
               (rawv_ref, v_ref, cwv_ref, vs_ref, None))
    def stage_conv_inputs():
        for raw_ref, src_ref, _, _, _ in streams:
            raw_ref[0:padr + halo, :] = jnp.zeros((padr + halo, LANES), F32)
            raw_ref[padr + halo:padr + halo + seq, :] = src_ref[...].astype(F32)
            raw_ref[lp + halo:lp + 2 * halo, :] = jnp.zeros((halo, LANES), F32)

    def conv_chunk(c):
        base = _aligned(c * CHUNK, CHUNK)
        for raw_ref, _, cw_ref, dst_ref, l2_scale in streams:
            cw = cw_ref[...]
            acc = None
            for j in range(CONV_K):
                off = halo - (CONV_K - 1) // 2 + j
                tap = cw[j:j + 1, :] * raw_ref[pl.ds(base + off, CHUNK), :]
                acc = tap if acc is None else acc + tap
            y = _silu(acc)
            if isinstance(c, int) and c * CHUNK < padr:
                y = jnp.where(row64 + base >= padr, y, 0.0)
            if l2_scale is not None:
                inv = lax.rsqrt(jnp.sum(y * y, axis=-1, keepdims=True) + NORM_EPS)
                y = y * (inv if l2_scale == 1.0 else inv * l2_scale)
            dst_ref[pl.ds(base, CHUNK), :] = y

    raw_ref = rawq_ref

    def stage_gate_inputs():
        raw_ref[0:padr, :] = jnp.zeros((padr, LANES), F32)
        raw_ref[padr:padr + seq, :] = ab_ref[...]

    neg_a = -jnp.exp(prm_ref[0:1, :])
    dt_bias = prm_ref[1:2, :]
    is_g = (lane64 < 8) | ((lane64 >= 16) & (lane64 < 24))
    is_beta = ((lane64 >= 8) & (lane64 < 16)) | ((lane64 >= 24) & (lane64 < 32))
    r128 = lax.broadcasted_iota(jnp.int32, (2 * CHUNK, CHUNK), 0)
    c128 = lax.broadcasted_iota(jnp.int32, (2 * CHUNK, CHUNK), 1)
    tri2 = jnp.where(((r128 < CHUNK) & (r128 >= c128)) | ((r128 >= CHUNK) & (r128 - CHUNK <= c128)),
                     1.0, 0.0).astype(MXU_DTYPE)
    sel_r = lax.broadcasted_iota(jnp.int32, (LANES, 4 * LANES), 0)
    sel_c = lax.broadcasted_iota(jnp.int32, (LANES, 4 * LANES), 1)
    n_gate_cols = 4 * N_HEADS
    sel_packed = jnp.where((sel_r < 3 * n_gate_cols)
                           & (sel_r % n_gate_cols == head + N_HEADS * (sel_c // LANES)), 1.0, 0.0).astype(MXU_DTYPE)
    tri2x3 = jnp.concatenate([tri2, tri2, tri2], axis=1)

    def gate_batch_group(cs, carry):
        bases = [_aligned(c * CHUNK, CHUNK) for c in cs]
        gates, parts = [], []
        for base in bases:
            ab = raw_ref[pl.ds(base, CHUNK), :]
            g = neg_a * jax.nn.softplus(ab + dt_bias)
            beta = jax.nn.sigmoid(ab)
            gt = jnp.where(is_g, g, jnp.where(is_beta, beta, 0.0))
            gt = jnp.where(row64 + base >= padr, gt, 0.0)
            gates.append(gt)
            parts.append(jnp.concatenate(_split3(gt), axis=0))
        cums = [_dot(tri2x3, p) for p in parts]
        for c, base, cm, gt in zip(cs, bases, cums, gates):
            m = jnp.where(lane64 < 8, cm[:CHUNK], jnp.where((lane64 >= 16) & (lane64 < 24), cm[CHUNK:], gt))
            gmix_ref[pl.ds(base, CHUNK), :] = m
            grow_ref[pl.ds(_aligned(c * LANES, LANES), LANES), :] = jnp.concatenate([m, m], axis=0).T
        return carry

    def gate_head_group(cs, carry):
        rows = len(cs) * CHUNK
        base = _aligned(cs[0] * CHUNK, CHUNK)
        hi, mid, lo = _split3(gmix_ref[pl.ds(base, rows), :])
        packed = (hi.astype(F32) + pltpu.roll(mid.astype(F32), n_gate_cols, 1)
                  + pltpu.roll(lo.astype(F32), 2 * n_gate_cols, 1)).astype(MXU_DTYPE)
        gsel_ref[pl.ds(base, rows), :] = _dot(packed, sel_packed)
        return carry

    fwd_half = lane64 < CHUNK
    col64 = lane64 & (CHUNK - 1)
    incl2 = (fwd_half & (row64 >= col64)) | (~fwd_half & (row64 <= col64))
    strict2 = (fwd_half & (row64 > col64)) | (~fwd_half & (row64 < col64))
    eye2 = jnp.where(row64 == col64, 1.0, 0.0).astype(F32)

    def blockdiag(p):
        return jnp.concatenate([jnp.where(fwd_half, p, 0.0), jnp.where(fwd_half, 0.0, p)],
                               axis=0).astype(MXU_DTYPE)

    n_neumann = int(math.log2(CHUNK)) - 2
    n_stages = n_neumann + 6

    def chunk_group(cs, carry, with_scan):
        n = len(cs)
        bases = [_aligned(c * CHUNK, CHUNK) for c in cs]
        cb128 = [_aligned(c * LANES, LANES) for c in cs]
        cb8 = [_aligned(c * SUBLANES, SUBLANES) for c in cs]
        stage = [0]

        def end_of_stage(carry):
            s = stage[0]
            stage[0] += 1
            if with_scan:
                for j in range(n):
                    if j * n_stages // n == s:
                        carry = scan_step(cs[j], carry)
            return carry

        def gate_cols(j):
            gs = gsel_ref[pl.ds(bases[j], CHUNK), :]
            return gs[:, 0:LANES], gs[:, LANES:2 * LANES], gs[:, 2 * LANES:3 * LANES], gs[:, 3 * LANES:]

        kq = []
        for j in range(n):
            k16 = ks_ref[pl.ds(bases[j], CHUNK), :].astype(MXU_DTYPE)
            q16 = qs_ref[pl.ds(bases[j], CHUNK), :].astype(MXU_DTYPE)
            kq.append(_dot_nt(jnp.concatenate([k16, q16], axis=0), jnp.concatenate([k16, k16], axis=0)))
        carry = end_of_stage(carry)

        ts, ps, qkds = [], [], []
        for j in range(n):
            gf, bf, gb, bb = gate_cols(j)
            row_f = grow_ref[pl.ds(cb128[j] + head, 1), :]
            row_b = grow_ref[pl.ds(cb128[j] + 2 * N_HEADS + head, 1), :]
            g_col = jnp.where(fwd_half, gf, gb)
            g_row = jnp.where(fwd_half, row_f, row_b)
            decay = jnp.where(incl2, jnp.exp(jnp.where(incl2, g_col - g_row, 0.0)), 0.0)
            a2 = jnp.where(strict2, kq[j][:CHUNK] * jnp.where(fwd_half, bf, bb) * decay, 0.0)
            qkds.append(jnp.where(incl2, kq[j][CHUNK:] * decay, 0.0))
            ts.append(eye2 - a2)
            ps.append(a2)

        ps = [_dot(p.astype(MXU_DTYPE), blockdiag(p)) for p in ps]
        carry = end_of_stage(carry)
        for _ in range(n_neumann):
            tps = [_dot(jnp.concatenate([t, p], axis=0).astype(MXU_DTYPE), blockdiag(p)) for t, p in zip(ts, ps)]
            carry = end_of_stage(carry)
            ts = [t + tp[:CHUNK] for t, tp in zip(ts, tps)]
            ps = [tp[CHUNK:] for tp in tps]
        ts = [t + _dot(t.astype(MXU_DTYPE), blockdiag(p)) for t, p in zip(ts, ps)]
        carry = end_of_stage(carry)

        uws = []
        for j in range(n):
            gf, bf, gb, bb = gate_cols(j)
            k = ks_ref[pl.ds(bases[j], CHUNK), :]
            v = vs_ref[pl.ds(bases[j], CHUNK), :]
            x_f = jnp.concatenate([v * bf, k * bf * jnp.exp(gf)], axis=1)
            x_b = jnp.concatenate([v * bb, k * bb * jnp.exp(gb)], axis=1)
            uw = _dot(blockdiag(ts[j]), jnp.concatenate([x_f, x_b], axis=0).astype(MXU_DTYPE))
            uws.append(uw.astype(MXU_DTYPE))
        carry = end_of_stage(carry)

        xxs = [_dot(blockdiag(qkds[j]), uws[j]) for j in range(n)]
        carry = end_of_stage(carry)
        for j in range(n):
            gf, _, gb, _ = gate_cols(j)
            k = ks_ref[pl.ds(bases[j], CHUNK), :]
            q = qs_ref[pl.ds(bases[j], CHUNK), :]
            xx = xxs[j]
            osum_w[pl.ds(bases[j], CHUNK), :] = xx[:CHUNK, :LANES] + xx[CHUNK:, :LANES]
            for d, gc in enumerate((gf, gb)):
                rows = slice(d * CHUNK, (d + 1) * CHUNK)
                g_last = gc[CHUNK - 1:CHUNK, :] if d == 0 else gc[0:1, :]
                k_dec = k * jnp.exp(g_last - gc)
                mn = _dot_tn(k_dec.astype(MXU_DTYPE), uws[j][rows])
                qe_w[d, pl.ds(bases[j], CHUNK), :] = (q * jnp.exp(gc) - xx[rows, LANES:]).astype(MXU_DTYPE)
                nn_w[d, pl.ds(cb128[j], HEAD_DIM), :] = mn[:, :LANES]
                mp_w[d, pl.ds(cb128[j], HEAD_DIM), :] = mn[:, LANES:].astype(MXU_DTYPE)
                dd_w[d, pl.ds(cb8[j], SUBLANES), :] = jnp.broadcast_to(jnp.exp(g_last), (SUBLANES, LANES))
        carry = end_of_stage(carry)
        assert stage[0] == n_stages
        return carry

    def scan_step(i, carry):
        new = []
        for d in range(2):
            s = carry[d]
            c = i if d == 0 else nc - 1 - i
            base = _aligned(c * CHUNK, CHUNK)
            cbase = _aligned(c * HEAD_DIM, HEAD_DIM)
            dbase = _aligned(c * SUBLANES, SUBLANES)
            s16 = s.astype(MXU_DTYPE)
            osum_r[pl.ds(base, CHUNK), :] += _dot(qe_r[d, pl.ds(base, CHUNK), :], s16)
            dec = dd_r[d, pl.ds(dbase, 1), :]
            new.append(dec * s - _dot(mp_r[d, pl.ds(cbase, HEAD_DIM), :], s16)
                       + nn_r[d, pl.ds(cbase, HEAD_DIM), :])
        return tuple(new)

    zero_state = jnp.zeros((HEAD_DIM, HEAD_DIM), F32)

    def write_output():
        gain = ong_ref[...]

        def finish(o, z):
            ms = jnp.mean(o * o, axis=-1, keepdims=True)
            return (o * lax.rsqrt(ms + NORM_EPS) * gain * _silu(z.astype(F32))).astype(o_ref.dtype)

        first = CHUNK - padr if padr else 0
        if first:
            o_ref[0:first, :] = finish(osum_r[padr:CHUNK, :], z_ref[0:first, :])

        def out_body(c, carry):
            base = pl.multiple_of(c * CHUNK, CHUNK)
            dst = pl.multiple_of(c * CHUNK - padr, SUBLANES)
            o_ref[pl.ds(dst, CHUNK), :] = finish(osum_r[pl.ds(base, CHUNK), :], z_ref[pl.ds(dst, CHUNK), :])
            return carry

        lax.fori_loop(1 if first else 0, nc, out_body, 0, unroll=8)

    def conv_only(c, carry):
        conv_chunk(c)
        return carry

    @pl.when(step < n_heads_total)
    def _():
        stage_conv_inputs()
        conv_chunk(0)
        lax.fori_loop(1, nc, conv_only, 0, unroll=4)

    @pl.when((step < n_heads_total) & (head == 0))
    def _():
        stage_gate_inputs()
        for_groups(gate_batch_group)

    @pl.when(step < n_heads_total)
    def _():
        for_groups(gate_head_group)

    @pl.when(step == 0)
    def _():
        for_groups(functools.partial(chunk_group, with_scan=False))

    @pl.when((step > 0) & (step < n_heads_total))
    def _():
        for_groups(functools.partial(chunk_group, with_scan=True), (zero_state, zero_state))
        write_output()

    @pl.when(step == n_heads_total)
    def _():
        lax.fori_loop(0, nc, scan_step, (zero_state, zero_state))
        write_output()


def _gdn(proj3, ab3, conv_w, gate_prm, out_norm_g):
    b, seq, _ = proj3.shape
    lp = -(-seq // CHUNK) * CHUNK
    nc = lp // CHUNK
    n_total = b * N_HEADS
    cur = lambda i: jnp.minimum(i, n_total - 1)
    prev = lambda i: jnp.maximum(i - 1, 0)
    col = lambda off: pl.BlockSpec((None, seq, LANES), lambda i: (cur(i) // N_HEADS, 0, off + cur(i) % N_HEADS))
    cw = lambda off: pl.BlockSpec((CONV_K, LANES), lambda i: (0, off + cur(i) % N_HEADS))
    return pl.pallas_call(
        functools.partial(_gdn_kernel, seq=seq, lp=lp, group=GDN_GROUP, n_heads_total=n_total),
        grid=(n_total + 1,),
        in_specs=[
            col(COL_QKV // LANES), col(COL_QKV // LANES + N_HEADS), col(COL_QKV // LANES + 2 * N_HEADS),
            pl.BlockSpec((None, seq, LANES),
                         lambda i: (prev(i) // N_HEADS, 0, COL_Z // LANES + prev(i) % N_HEADS)),
            pl.BlockSpec((None, seq, LANES), lambda i: (cur(i) // N_HEADS, 0, 0)),
            cw(0), cw(N_HEADS), cw(2 * N_HEADS),
            pl.BlockSpec((SUBLANES, LANES), lambda i: (0, 0)),
            pl.BlockSpec((1, LANES), lambda i: (0, 0)),
        ],
        out_specs=pl.BlockSpec((None, seq, LANES), lambda i: (prev(i) // N_HEADS, 0, prev(i) % N_HEADS)),
        out_shape=jax.ShapeDtypeStruct((b, seq, N_HEADS * HEAD_DIM), MXU_DTYPE),
        scratch_shapes=[
            pltpu.VMEM((lp + 2 * SUBLANES, LANES), F32),
            pltpu.VMEM((lp + 2 * SUBLANES, LANES), F32),
            pltpu.VMEM((lp + 2 * SUBLANES, LANES), F32),
            pltpu.VMEM((lp, LANES), F32),
            pltpu.VMEM((lp, LANES), F32),
            pltpu.VMEM((lp, LANES), F32),
            pltpu.VMEM((lp, 4 * LANES), F32),
            pltpu.VMEM((nc * LANES, LANES), F32),
            pltpu.VMEM((lp, LANES), F32),
            pltpu.VMEM((2, 2, lp, LANES), MXU_DTYPE),
            pltpu.VMEM((2, 2, nc * HEAD_DIM, LANES), MXU_DTYPE),
            pltpu.VMEM((2, 2, nc * HEAD_DIM, LANES), F32),
            pltpu.VMEM((2, 2, nc * SUBLANES, LANES), F32),
            pltpu.VMEM((2, lp, LANES), F32),
        ],
        compiler_params=_params(("arbitrary",)),
        name="gdn",
    )(proj3, proj3, proj3, proj3, ab3, conv_w, conv_w, conv_w, gate_prm, out_norm_g)


def _merge_kernel(h_ref, og_ref, ym_ref, gf_ref, gd_ref, wf_ref, wd_ref, wo_ref, n2_ref, wr_ref, br_ref,
                  h1_ref, hn_ref, ridx_ref, rw_ref, cnt_ref):
    @pl.when(pl.program_id(0) == 0)
    def _():
        cnt_ref[...] = jnp.zeros_like(cnt_ref)

    y_f = _dot(ym_ref[...], wf_ref[...])
    y_d = _dot(og_ref[...], wd_ref[...])
    merged = (jax.nn.sigmoid(gf_ref[...].astype(F32)) * y_f
              + jax.nn.sigmoid(gd_ref[...].astype(F32)) * y_d)
    h1 = h_ref[...] + _dot(merged.astype(MXU_DTYPE), wo_ref[...])
    h1_ref[...] = h1
    ms = jnp.mean(h1 * h1, axis=-1, keepdims=True)
    hn = h1 * lax.rsqrt(ms + NORM_EPS) * n2_ref[...]
    hn_ref[...] = _pack_rows(hn)

    x_hi = hn.astype(MXU_DTYPE)
    x_lo = (hn - x_hi.astype(F32)).astype(MXU_DTYPE)
    w = wr_ref[...]
    w_hi = w.astype(MXU_DTYPE)
    w_lo = (w - w_hi.astype(F32)).astype(MXU_DTYPE)
    hi_terms = _dot(x_hi, jnp.concatenate([w_hi, w_lo], axis=1))
    logits = hi_terms[:, :LANES] + hi_terms[:, LANES:] + _dot(x_lo, w_hi) + br_ref[...]

    tm = logits.shape[0]
    lane = lax.broadcasted_iota(jnp.int32, (tm, LANES), 1)
    neg_inf = -jnp.inf
    is_group = lane < N_GROUPS
    gl = jnp.where(is_group, logits, neg_inf)
    ge = jnp.exp(gl - jnp.max(gl, axis=-1, keepdims=True))
    gp = ge / jnp.sum(ge, axis=-1, keepdims=True)
    p_group = jnp.max(gp, axis=-1, keepdims=True)
    g_idx = jnp.min(jnp.where(is_group & (gp == p_group), lane, LANES), axis=-1, keepdims=True)

    in_group = (lane >= N_GROUPS) & (lane < N_GROUPS + N_EXPERTS) & (((lane - N_GROUPS) >> 3) == g_idx)
    el = jnp.where(in_group, logits, neg_inf)
    ee = jnp.exp(el - jnp.max(el, axis=-1, keepdims=True))
    ep = ee / jnp.sum(ee, axis=-1, keepdims=True)
    v1 = jnp.max(jnp.where(in_group, ep, -1.0), axis=-1, keepdims=True)
    i1 = jnp.min(jnp.where(in_group & (ep == v1), lane, LANES), axis=-1, keepdims=True)
    rest = in_group & (lane != i1)
    v2 = jnp.max(jnp.where(rest, ep, -1.0), axis=-1, keepdims=True)
    i2 = jnp.min(jnp.where(rest & (ep == v2), lane, LANES), axis=-1, keepdims=True)
    denom = v1 + v2
    w1 = p_group * (v1 / denom)
    w2 = p_group * (v2 / denom)
    e1 = i1 - N_GROUPS
    e2 = i2 - N_GROUPS
    ridx_ref[...] = jnp.where(lane == 0, e1, jnp.where(lane == 1, e2, 0))
    rw_ref[...] = jnp.where(lane == 0, w1, jnp.where(lane == 1, w2, 0.0))
    onehots = jnp.where((lane == e1) | (lane == e2), 1.0, 0.0)
    cnt_ref[...] += jnp.broadcast_to(jnp.sum(onehots, axis=0, keepdims=True), cnt_ref.shape)


def _merge(h2d, og2d, ymix2d, proj2d, w_fourier, w_delta, w_out, norm2_g, w_router, b_router, tm):
    t, d = h2d.shape
    row = lambda w: pl.BlockSpec((tm, w), lambda i: (i, 0))
    const = lambda shape: pl.BlockSpec(shape, lambda i: (0, 0))
    return pl.pallas_call(
        _merge_kernel,
        grid=(t // tm,),
        in_specs=[
            row(d), row(d), row(F_WIDTH),
            pl.BlockSpec((tm, d), lambda i: (i, COL_GATE_F // d)),
            pl.BlockSpec((tm, d), lambda i: (i, COL_GATE_D // d)),
            const((F_WIDTH, d)), const((d, d)), const((d, d)), const((1, d)),
            const((d, LANES)), const((1, LANES)),
        ],
        out_specs=[row(d), row(d // 2), row(LANES), row(LANES), const((SUBLANES, LANES))],
        out_shape=[
            jax.ShapeDtypeStruct((t, d), F32),
            jax.ShapeDtypeStruct((t, d // 2), jnp.uint32),
            jax.ShapeDtypeStruct((t, LANES), jnp.int32),
            jax.ShapeDtypeStruct((t, LANES), F32),
            jax.ShapeDtypeStruct((SUBLANES, LANES), F32),
        ],
        compiler_params=_params(("arbitrary",)),
        name="merge",
    )(h2d, og2d, ymix2d, proj2d, proj2d, w_fourier, w_delta, w_out, norm2_g, w_router, b_router)


def _plan_kernel(ridx_ref, cnt_ref, dest_ref, seg_ref, base_ref, run_ref):
    tt = ridx_ref.shape[0]

    @pl.when(pl.program_id(0) == 0)
    def _():
        cnt = cnt_ref[...].astype(jnp.int32)
        n_blk = (cnt + (EXPERT_BLOCK - 1)) // EXPERT_BLOCK
        padded = (n_blk * EXPERT_BLOCK).astype(F32)
        r = lax.broadcasted_iota(jnp.int32, (LANES, LANES), 0)
        c = lax.broadcasted_iota(jnp.int32, (LANES, LANES), 1)
        before = jnp.where(r < c, 1.0, 0.0).astype(MXU_DTYPE)
        start = _exact_right(padded, before)
        base_ref[...] = start
        run_ref[...] = jnp.zeros_like(run_ref)
        seg_ref[0:SUBLANES, :] = start.astype(jnp.int32)
        seg_ref[SUBLANES:2 * SUBLANES, :] = n_blk
        seg_ref[2 * SUBLANES:, :] = cnt

    lane = lax.broadcasted_iota(jnp.int32, (tt, LANES), 1)
    ridx = ridx_ref[...]
    oh0 = jnp.where(lane == ridx[:, 0:1], 1.0, 0.0)
    oh1 = jnp.where(lane == ridx[:, 1:2], 1.0, 0.0)
    r = lax.broadcasted_iota(jnp.int32, (tt, tt), 0)
    c = lax.broadcasted_iota(jnp.int32, (tt, tt), 1)
    earlier = jnp.where(c < r, 1.0, 0.0).astype(MXU_DTYPE)
    prefix = _dot(earlier, jnp.concatenate([oh0, oh1], axis=1).astype(MXU_DTYPE))
    c0 = jnp.sum(oh0, axis=0, keepdims=True)
    c1 = jnp.sum(oh1, axis=0, keepdims=True)
    first = base_ref[0:1, :] + run_ref[0:1, :]
    d0 = jnp.sum(oh0 * (prefix[:, :LANES] + first), axis=-1, keepdims=True)
    d1 = jnp.sum(oh1 * (prefix[:, LANES:] + first + c0), axis=-1, keepdims=True)
    run_ref[...] += jnp.broadcast_to(c0 + c1, run_ref.shape)
    both = jnp.where(lane == 0, d0, jnp.where(lane == 1, d1, 0.0))
    both_t = both.T.astype(jnp.int32)
    for k in range(TOP_K):
        dest_ref[k] = both_t[k:k + 1, :]


def _plan(ridx, cnt, tt):
    t = ridx.shape[0]
    return pl.pallas_call(
        _plan_kernel,
        grid=(t // tt,),
        in_specs=[pl.BlockSpec((tt, LANES), lambda i: (i, 0)),
                  pl.BlockSpec((SUBLANES, LANES), lambda i: (0, 0))],
        out_specs=[pl.BlockSpec((TOP_K, None, 1, tt), lambda i: (0, i, 0, 0)),
                   pl.BlockSpec((3 * SUBLANES, LANES), lambda i: (0, 0))],
        out_shape=[jax.ShapeDtypeStruct((TOP_K, t // tt, 1, tt), jnp.int32),
                   jax.ShapeDtypeStruct((3 * SUBLANES, LANES), jnp.int32)],
        scratch_shapes=[pltpu.VMEM((SUBLANES, LANES), F32), pltpu.VMEM((SUBLANES, LANES), F32)],
        compiler_params=_params(("arbitrary",)),
        name="plan",
    )(ridx, cnt)


def _scatter_kernel(dest0_ref, dest1_ref, hn_ref, xb_in_ref, xb_ref, sem, *, ts):
    dest_refs = (dest0_ref, dest1_ref)
    del xb_in_ref

    def issue(t, carry):
        src = hn_ref.at[pl.ds(t, 1)]
        for k in range(TOP_K):
            pltpu.make_async_copy(src, xb_ref.at[pl.ds(dest_refs[k][0, t], 1)], sem).start(priority=k % 2)
        return carry

    lax.fori_loop(0, ts, issue, 0, unroll=DMA_UNROLL)
    for _ in range(TOP_K):
        pltpu.make_async_copy(hn_ref, xb_ref.at[pl.ds(0, ts)], sem).wait()


def _scatter(dest_flat, hn2d, xb_zero, ts):
    t, d = hn2d.shape
    return pl.pallas_call(
        functools.partial(_scatter_kernel, ts=ts),
        grid=(t // ts,),
        in_specs=[pl.BlockSpec((None, None, 1, ts), lambda i: (0, i, 0, 0), memory_space=pltpu.SMEM),
                  pl.BlockSpec((None, None, 1, ts), lambda i: (1, i, 0, 0), memory_space=pltpu.SMEM),
                  pl.BlockSpec((ts, d), lambda i: (i, 0)),
                  pl.BlockSpec(memory_space=pl.ANY)],
        out_specs=pl.BlockSpec(memory_space=pl.ANY),
        out_shape=jax.ShapeDtypeStruct(xb_zero.shape, xb_zero.dtype),
        scratch_shapes=[pltpu.SemaphoreType.DMA(())],
        input_output_aliases={3: 0},
        compiler_params=_params(("arbitrary",)),
        name="scatter",
    )(dest_flat, dest_flat, hn2d, xb_zero)


def _expert_kernel(start_ref, nblk_ref, xb_ref, wg_ref, wu_ref, wd_ref, yb_ref,
                   wg16_ref, wu16_ref, wd16_ref, xbuf_ref, ybuf_ref, sem_in, sem_out):
    expert = pl.program_id(0)
    last = pl.num_programs(0) - 1
    n = nblk_ref[expert]
    first = start_ref[expert] // EXPERT_BLOCK
    total = start_ref[last] // EXPERT_BLOCK + nblk_ref[last]

    def rows(g):
        return pl.ds(pl.multiple_of(g * EXPERT_BLOCK, EXPERT_BLOCK), EXPERT_BLOCK)

    def x_copy(g):
        slot = g % EXPERT_X_SLOTS
        return pltpu.make_async_copy(xb_ref.at[rows(g)], xbuf_ref.at[slot], sem_in.at[slot])

    def y_copy(g):
        slot = g % EXPERT_Y_SLOTS
        return pltpu.make_async_copy(ybuf_ref.at[slot], yb_ref.at[rows(g)], sem_out.at[slot])

    @pl.when(n > 0)
    def _():
        @pl.when(first == 0)
        def _():
            for j in range(EXPERT_X_SLOTS - 1):
                @pl.when(j < total)
                def _():
                    x_copy(j).start()

        wg16_ref[...] = wg_ref[...].astype(MXU_DTYPE)
        wu16_ref[...] = wu_ref[...].astype(MXU_DTYPE)
        wd16_ref[...] = wd_ref[...].astype(MXU_DTYPE)

        def body(i, carry):
            g = first + i

            @pl.when(g + (EXPERT_X_SLOTS - 1) < total)
            def _():
                x_copy(g + (EXPERT_X_SLOTS - 1)).start()

            x_copy(g).wait()

            @pl.when(g >= EXPERT_Y_SLOTS)
            def _():
                y_copy(g - EXPERT_Y_SLOTS).wait()

            x = _unpack_rows(xbuf_ref[g % EXPERT_X_SLOTS]).astype(MXU_DTYPE)
            gate = _dot(x, wg16_ref[...])
            up = _dot(x, wu16_ref[...])
            ybuf_ref[g % EXPERT_Y_SLOTS] = _pack_rows(_dot((_silu(gate) * up).astype(MXU_DTYPE), wd16_ref[...]))
            y_copy(g).start()
            return carry

        lax.fori_loop(0, n, body, 0)

    @pl.when(expert == last)
    def _():
        for j in range(EXPERT_Y_SLOTS):
            @pl.when(total > j)
            def _():
                y_copy(total - 1 - j).wait()


def _experts(seg_start, seg_blocks, xb, w_gate, w_up, w_down):
    n_slots, row_words = xb.shape
    n_exp, d, de = w_gate.shape
    grid_spec = pltpu.PrefetchScalarGridSpec(
        num_scalar_prefetch=2,
        grid=(n_exp,),
        in_specs=[
            pl.BlockSpec(memory_space=pl.ANY),
            pl.BlockSpec((None, d, de), lambda e, st, nb: (e, 0, 0)),
            pl.BlockSpec((None, d, de), lambda e, st, nb: (e, 0, 0)),
            pl.BlockSpec((None, de, d), lambda e, st, nb: (e, 0, 0)),
        ],
        out_specs=pl.BlockSpec(memory_space=pl.ANY),
        scratch_shapes=[
            pltpu.VMEM((d, de), MXU_DTYPE), pltpu.VMEM((d, de), MXU_DTYPE), pltpu.VMEM((de, d), MXU_DTYPE),
            pltpu.VMEM((EXPERT_X_SLOTS, EXPERT_BLOCK, row_words), jnp.uint32),
            pltpu.VMEM((EXPERT_Y_SLOTS, EXPERT_BLOCK, row_words), jnp.uint32),
            pltpu.SemaphoreType.DMA((EXPERT_X_SLOTS,)), pltpu.SemaphoreType.DMA((EXPERT_Y_SLOTS,)),
        ],
    )
    return pl.pallas_call(
        _expert_kernel,
        grid_spec=grid_spec,
        out_shape=jax.ShapeDtypeStruct((n_slots, row_words), jnp.uint32),
        input_output_aliases={2: 0},
        compiler_params=_params(("arbitrary",)),
        name="experts",
    )(seg_start, seg_blocks, xb, w_gate, w_up, w_down)


def _combine_kernel(dest0_ref, dest1_ref, yb_ref, h1_ref, rw_ref, fg_ref, o_ref, buf_ref, res_ref, sem, out_sem,
                    *, tc, groups_per_batch):
    step = pl.program_id(0)
    n_tiles = pl.num_programs(0) - 1
    n_grp = tc // N_META
    cur = step % 2
    prev = 1 - cur

    def output_copies(tile, action):
        slot = tile % 2
        first = tile * n_grp
        first_in_batch = first % groups_per_batch
        has_meta = (first_in_batch == 0) | (first_in_batch + n_grp > groups_per_batch)

        def copy_groups(q, n):
            grp = first + q
            dst = pl.multiple_of((grp - grp // groups_per_batch - 1) * N_META, N_META)
            copy = pltpu.make_async_copy(res_ref.at[slot, pl.ds(q * N_META, n * N_META)],
                                         o_ref.at[pl.ds(dst, n * N_META)], out_sem.at[slot])
            copy.start() if action == "start" else copy.wait()

        @pl.when(jnp.logical_not(has_meta))
        def _():
            copy_groups(0, n_grp)

        @pl.when(has_meta)
        def _():
            for q in range(n_grp):
                @pl.when((first + q) % groups_per_batch != 0)
                def _():
                    copy_groups(q, 1)

    def request_rows(i):
        for u in range(SUBLANES):
            t = i * SUBLANES + u
            for k in range(TOP_K):
                pltpu.make_async_copy(yb_ref.at[pl.ds((dest0_ref, dest1_ref)[k][0, t], 1)],
                                      buf_ref.at[cur, k, pl.ds(t, 1)], sem.at[cur]).start(priority=k % 2)

    def finish_rows(i):
        rows = pl.ds(pl.multiple_of(i * SUBLANES, SUBLANES), SUBLANES)
        rw = rw_ref[rows, :]
        h2 = h1_ref[rows, :] + (_unpack_rows(buf_ref[prev, 0, rows, :]) * rw[:, 0:1]
                                + _unpack_rows(buf_ref[prev, 1, rows, :]) * rw[:, 1:2])
        ms = jnp.mean(h2 * h2, axis=-1, keepdims=True)
        res_ref[prev, rows, :] = h2 * lax.rsqrt(ms + NORM_EPS) * fg_ref[...]

    def begin_finish():
        for k in range(TOP_K):
            pltpu.make_async_copy(yb_ref.at[pl.ds(0, tc)], buf_ref.at[prev, k], sem.at[prev]).wait()

        @pl.when(step >= 3)
        def _():
            output_copies(step - 3, "wait")

    def both(i, carry):
        request_rows(i)
        finish_rows(i)
        return carry

    def request_only(i, carry):
        request_rows(i)
        return carry

    def finish_only(i, carry):
        finish_rows(i)
        return carry

    @pl.when(step == 0)
    def _():
        lax.fori_loop(0, tc // SUBLANES, request_only, 0)

    @pl.when((step > 0) & (step < n_tiles))
    def _():
        begin_finish()
        lax.fori_loop(0, tc // SUBLANES, both, 0, unroll=16)
        output_copies(step - 1, "start")

    @pl.when(step == n_tiles)
    def _():
        begin_finish()
        lax.fori_loop(0, tc // SUBLANES, finish_only, 0, unroll=4)
        output_copies(step - 1, "start")

        @pl.when(step >= 2)
        def _():
            output_copies(step - 2, "wait")

        output_copies(step - 1, "wait")


def _combine(dest_flat, yb, h1, rw, final_g, tc, seq):
    t, d = h1.shape
    assert tc % N_META == 0 and seq % N_META == 0
    n_out = t - (t // seq) * N_META
    n_tiles = t // tc
    nxt = lambda i: jnp.minimum(i, n_tiles - 1)
    fin = lambda i: jnp.maximum(i - 1, 0)
    return pl.pallas_call(
        functools.partial(_combine_kernel, tc=tc, groups_per_batch=seq // N_META),
        grid=(n_tiles + 1,),
        in_specs=[pl.BlockSpec((None, None, 1, tc), lambda i: (0, nxt(i), 0, 0), memory_space=pltpu.SMEM),
                  pl.BlockSpec((None, None, 1, tc), lambda i: (1, nxt(i), 0, 0), memory_space=pltpu.SMEM),
                  pl.BlockSpec(memory_space=pl.ANY),
                  pl.BlockSpec((tc, d), lambda i: (fin(i), 0)),
                  pl.BlockSpec((tc, LANES), lambda i: (fin(i), 0)),
                  pl.BlockSpec((1, d), lambda i: (0, 0))],
        out_specs=pl.BlockSpec(memory_space=pl.ANY),
        out_shape=jax.ShapeDtypeStruct((n_out, d), F32),
        scratch_shapes=[pltpu.VMEM((2, TOP_K, tc, d // 2), jnp.uint32), pltpu.VMEM((2, tc, d), F32),
                        pltpu.SemaphoreType.DMA((2,)), pltpu.SemaphoreType.DMA((2,))],
        compiler_params=_params(("arbitrary",)),
        name="combine",
    )(dest_flat, dest_flat, yb, h1, rw, final_g)


def _tile(total, candidates):
    for c in candidates:
        if total % c == 0:
            return c
    raise ValueError(f"no tile for {total}")


def _dft_tables(seq):
    kp = -(-seq // LANES) * LANES
    nat = -(-(seq // 2 + 1) // 16) * 16
    mir = seq - nat
    c = np.arange(FGROUP_DIM, dtype=np.int64)
    ang_c = (2.0 * np.pi / FGROUP_DIM) * ((c[:, None] * c[None, :]) % FGROUP_DIM)
    cs128 = np.concatenate([np.cos(ang_c), np.sin(ang_c)], axis=1)
    k = np.arange(nat, dtype=np.int64)
    n = np.arange(seq, dtype=np.int64)
    ang_l = (2.0 * np.pi / seq) * ((k[:, None] * n[None, :]) % seq)
    twid = np.zeros((nat, 2 * kp), np.float32)
    twid[:, :seq] = np.cos(ang_l)
    twid[:, kp:kp + seq] = np.sin(ang_l)
    flip = np.zeros((mir, nat), np.float32)
    flip[np.arange(mir), mir - np.arange(mir)] = 1.0
    as_const = lambda t: jnp.asarray(t.astype(MXU_DTYPE))
    return as_const(cs128), as_const(twid), as_const(flip)


def kernel(x, meta_tokens, norm1_g, w_in, conv_w, a_log_fwd, dt_bias_fwd, a_log_bwd, dt_bias_bwd, out_norm_g, w_fourier, w_delta, w_out, norm2_g, w_router_group, b_router_group, w_router_expert, b_router_expert, w_gate_e, w_up_e, w_down_e, final_norm_g):
    bsz, n_real, d = x.shape
    seq = N_META + n_real
    n_tok = bsz * seq
    assert (CHUNK - N_META) % CHUNK + seq == -(-seq // CHUNK) * CHUNK, "meta tokens must fill the tail of chunk 0"
    assert w_in.shape[0] == 1, "single trunk layer"
    layer = 0


    w = w_in[layer]
    o_f, o_qkv, o_z, o_ab, o_gf, o_gd = 0, F_WIDTH, F_WIDTH + 3 * QK_WIDTH, F_WIDTH + 4 * QK_WIDTH, \
        F_WIDTH + 4 * QK_WIDTH + 4 * N_HEADS, F_WIDTH + 4 * QK_WIDTH + 4 * N_HEADS + d
    w_cat = jnp.concatenate(
        [w[:, o_qkv:o_z], w[:, o_z:o_ab], w[:, o_gf:o_gd], w[:, o_gd:o_gd + d], w[:, o_f:o_qkv]],
        axis=1).astype(MXU_DTYPE)
    w_ab = jnp.pad(w[:, o_ab:o_gf], ((0, 0), (0, LANES - 4 * N_HEADS))).astype(MXU_DTYPE)

    proj, ab, h = _in_proj(x.reshape(bsz * n_real, d), meta_tokens.astype(x.dtype), norm1_g[layer][None, :],
                           w_cat, w_ab, seq)
    proj3 = proj.reshape(bsz, seq, PROJ_WIDTH)

    ymix = _fnet(proj3, *_dft_tables(seq))

    gate_prm = jnp.zeros((SUBLANES, LANES), F32)
    gate_prm = gate_prm.at[0, 0:N_HEADS].set(a_log_fwd[layer]).at[0, 2 * N_HEADS:3 * N_HEADS].set(a_log_bwd[layer])
    gate_prm = gate_prm.at[1, 0:N_HEADS].set(dt_bias_fwd[layer]).at[1, 2 * N_HEADS:3 * N_HEADS].set(dt_bias_bwd[layer])
    og = _gdn(proj3, ab.reshape(bsz, seq, LANES), conv_w[layer], gate_prm, out_norm_g[layer][None, :])

    w_router = jnp.zeros((d, LANES), F32)
    w_router = w_router.at[:, 0:N_GROUPS].set(w_router_group[layer])
    w_router = w_router.at[:, N_GROUPS:N_GROUPS + N_EXPERTS].set(w_router_expert[layer])
    b_router = jnp.zeros((1, LANES), F32)
    b_router = b_router.at[0, 0:N_GROUPS].set(b_router_group[layer])
    b_router = b_router.at[0, N_GROUPS:N_GROUPS + N_EXPERTS].set(b_router_expert[layer])
    tt = _tile(n_tok, (384, 128))
    h1, hn, ridx, rw, cnt = _merge(
        h, og.reshape(n_tok, d), ymix.reshape(n_tok, F_WIDTH), proj,
        w_fourier[layer].astype(MXU_DTYPE), w_delta[layer].astype(MXU_DTYPE), w_out[layer].astype(MXU_DTYPE),
        norm2_g[layer][None, :], w_router, b_router, _tile(n_tok, (688, 384, 128)))

    n_assign = n_tok * TOP_K
    n_blocks = -(-n_assign // EXPERT_BLOCK) + N_EXPERTS
    dest, seg = _plan(ridx, cnt, tt)
    seg_start = seg[0, :N_EXPERTS]
    seg_blocks = seg[SUBLANES, :N_EXPERTS]

    xb = _scatter(dest, hn, jnp.zeros((n_blocks * EXPERT_BLOCK, d // 2), jnp.uint32), tt)
    yb = _experts(seg_start, seg_blocks, xb, w_gate_e[layer], w_up_e[layer], w_down_e[layer])
    out = _combine(dest, yb, h1, rw, final_norm_g[None, :], tt, seq)
    return out.reshape(bsz, n_real, d)
```

```python
import functools
import math

import jax
import jax.numpy as jnp
import numpy as np
from jax import lax
from jax.experimental import pallas as pl
from jax.experimental.pallas import tpu as pltpu

F32 = jnp.float32
MXU_DTYPE = jnp.bfloat16

N_META = 16
CHUNK = 64
N_FGROUPS = 4
FGROUP_DIM = 128
F_WIDTH = N_FGROUPS * FGROUP_DIM
N_HEADS = 8
HEAD_DIM = 128
QK_WIDTH = N_HEADS * HEAD_DIM
CONV_K = 5
N_GROUPS = 8
EXPERTS_PER_GROUP = 8
N_EXPERTS = N_GROUPS * EXPERTS_PER_GROUP
TOP_K = 2
EXPERT_BLOCK = 256
NORM_EPS = 1e-6
LANES = 128
SUBLANES = 8
VMEM_LIMIT = 52 * 1024 * 1024
GDN_GROUP = 17
DMA_UNROLL = 16
EXPERT_X_SLOTS = 4
EXPERT_Y_SLOTS = 3

COL_QKV = 0
COL_Z = 3 * QK_WIDTH
COL_GATE_F = COL_Z + QK_WIDTH
COL_GATE_D = COL_GATE_F + 1024
COL_FIN = COL_GATE_D + 1024
PROJ_WIDTH = COL_FIN + F_WIDTH
PROJ_TN = 3328
PROJ_DTYPE = jnp.bfloat16


def _dot(a, b):
    return jnp.dot(a, b, preferred_element_type=F32)


def _dot_nt(a, b):
    return lax.dot_general(a, b, (((1,), (1,)), ((), ())), preferred_element_type=F32)


def _dot_tn(a, b):
    return lax.dot_general(a, b, (((0,), (0,)), ((), ())), preferred_element_type=F32)


def _split3(x):
    hi = x.astype(MXU_DTYPE)
    r1 = x - hi.astype(F32)
    mid = r1.astype(MXU_DTYPE)
    lo = (r1 - mid.astype(F32)).astype(MXU_DTYPE)
    return hi, mid, lo


def _exact_left(sel, x):
    hi, mid, lo = _split3(x)
    return _dot(sel, hi) + _dot(sel, mid) + _dot(sel, lo)


def _exact_right(x, sel):
    hi, mid, lo = _split3(x)
    return _dot(hi, sel) + _dot(mid, sel) + _dot(lo, sel)


def _pack_rows(x):
    half = x.shape[1] // 2
    lo = lax.bitcast_convert_type(x[:, :half].astype(jnp.bfloat16).astype(F32), jnp.uint32)
    hi = lax.bitcast_convert_type(x[:, half:].astype(jnp.bfloat16).astype(F32), jnp.uint32)
    return (lo >> 16) | hi


def _unpack_rows(u):
    lo = lax.bitcast_convert_type(u << 16, F32)
    hi = lax.bitcast_convert_type(u & jnp.uint32(0xFFFF0000), F32)
    return jnp.concatenate([lo, hi], axis=1)


def _silu(x):
    return x * jax.nn.sigmoid(x)


def _params(sem):
    return pltpu.CompilerParams(dimension_semantics=sem, vmem_limit_bytes=VMEM_LIMIT)


def _inproj_kernel(x_ref, meta_ref, g_ref, w_ref, wab_ref, o_ref, ab_ref, h_ref, hn_ref):
    @pl.when(pl.program_id(1) == 0)
    def _():
        tm = h_ref.shape[0]

        @pl.when(pl.program_id(0) % 2 == 0)
        def _():
            h_ref[0:N_META, :] = meta_ref[...]
            h_ref[N_META:tm, :] = x_ref[0:tm - N_META, :]

        @pl.when(pl.program_id(0) % 2 == 1)
        def _():
            h_ref[...] = x_ref[...]

        x = h_ref[...]
        ms = jnp.mean(x * x, axis=-1, keepdims=True)
        hn_ref[...] = (x * lax.rsqrt(ms + NORM_EPS) * g_ref[...]).astype(MXU_DTYPE)
        ab_ref[...] = _dot(hn_ref[...], wab_ref[...])

    o_ref[...] = _dot(hn_ref[...], w_ref[...]).astype(o_ref.dtype)


def _in_proj(x2d, meta, gain, w_cat, w_ab, seq):
    n_x, d = x2d.shape
    n_real = seq - N_META
    t = n_x // n_real * seq
    tm = seq // 2
    assert seq % (2 * SUBLANES) == 0 and tm >= N_META
    n = w_cat.shape[1]
    x_spec = pl.BlockSpec(
        (pl.Element(tm), pl.Element(d)),
        lambda i, j: (((i // 2) * (n_real // SUBLANES) + (i % 2) * ((tm - N_META) // SUBLANES)) * SUBLANES, 0))
    return pl.pallas_call(
        _inproj_kernel,
        grid=(t // tm, n // PROJ_TN),
        in_specs=[
            x_spec,
            pl.BlockSpec((N_META, d), lambda i, j: (0, 0)),
            pl.BlockSpec((1, d), lambda i, j: (0, 0)),
            pl.BlockSpec((d, PROJ_TN), lambda i, j: (0, j)),
            pl.BlockSpec((d, LANES), lambda i, j: (0, 0)),
        ],
        out_specs=[pl.BlockSpec((tm, PROJ_TN), lambda i, j: (i, j)),
                   pl.BlockSpec((tm, LANES), lambda i, j: (i, 0)),
                   pl.BlockSpec((tm, d), lambda i, j: (i, 0))],
        out_shape=[jax.ShapeDtypeStruct((t, n), PROJ_DTYPE),
                   jax.ShapeDtypeStruct((t, LANES), F32),
                   jax.ShapeDtypeStruct((t, d), F32)],
        scratch_shapes=[pltpu.VMEM((tm, d), MXU_DTYPE)],
        compiler_params=_params(("arbitrary", "arbitrary")),
        name="in_proj",
    )(x2d, meta, gain, w_cat, w_ab)


def _fnet_kernel(x_ref, cs_ref, tw_ref, flip_ref, o_ref, r_ref, *, seq, kp, nat, scale):
    tail = kp - seq
    if tail:
        r_ref[seq:kp, :] = jnp.zeros((tail, F_WIDTH), MXU_DTYPE)
        r_ref[kp + seq:, :] = jnp.zeros((tail, F_WIDTH), MXU_DTYPE)
    cs = cs_ref[...]
    for g in range(N_FGROUPS):
        cols = slice(g * FGROUP_DIM, (g + 1) * FGROUP_DIM)
        p = _dot(x_ref[:, cols].astype(MXU_DTYPE), cs)
        r_ref[0:seq, cols] = p[:, :FGROUP_DIM].astype(MXU_DTYPE)
        r_ref[kp:kp + seq, cols] = p[:, FGROUP_DIM:].astype(MXU_DTYPE)

    a = _dot(tw_ref[:, :kp], r_ref[0:kp, :])
    b = _dot(tw_ref[:, kp:], r_ref[kp:, :])
    o_ref[0:nat, :] = ((a - b) * scale).astype(o_ref.dtype)
    mirrored = ((a + b) * scale).astype(MXU_DTYPE)
    o_ref[nat:, :] = _dot(flip_ref[...], mirrored).astype(o_ref.dtype)


def _fnet(proj3, cs128, twid, flip):
    b, seq, _ = proj3.shape
    nat, kp = twid.shape[0], twid.shape[1] // 2
    scale = 1.0 / math.sqrt(seq * FGROUP_DIM)
    const = lambda shape: pl.BlockSpec(shape, lambda bi: (0, 0))
    return pl.pallas_call(
        functools.partial(_fnet_kernel, seq=seq, kp=kp, nat=nat, scale=scale),
        grid=(b,),
        in_specs=[
            pl.BlockSpec((None, seq, F_WIDTH), lambda bi: (bi, 0, COL_FIN // F_WIDTH)),
            const((FGROUP_DIM, 2 * FGROUP_DIM)), const((nat, 2 * kp)), const((seq - nat, nat)),
        ],
        out_specs=pl.BlockSpec((None, seq, F_WIDTH), lambda bi: (bi, 0, 0)),
        out_shape=jax.ShapeDtypeStruct((b, seq, F_WIDTH), MXU_DTYPE),
        scratch_shapes=[pltpu.VMEM((2 * kp, F_WIDTH), MXU_DTYPE)],
        compiler_params=_params(("arbitrary",)),
        name="fnet",
    )(proj3, cs128, twid, flip)


def _aligned(x, m):
    return x if isinstance(x, int) else pl.multiple_of(x, m)


def _gdn_kernel(q_ref, k_ref, v_ref, z_ref, ab_ref, cwq_ref, cwk_ref, cwv_ref, prm_ref, ong_ref,
                o_ref,
                rawq_ref, rawk_ref, rawv_ref, qs_ref, ks_ref, vs_ref, gsel_ref, grow_ref, gmix_ref,
                qe_ref, mp_ref, nn_ref, dd_ref, osum_ref,
                *, seq, lp, group, n_heads_total):
    step = pl.program_id(0)
    head = jnp.minimum(step, n_heads_total - 1) % N_HEADS
    nc = lp // CHUNK
    padr = lp - seq
    halo = SUBLANES
    row64 = lax.broadcasted_iota(jnp.int32, (CHUNK, LANES), 0)
    lane64 = lax.broadcasted_iota(jnp.int32, (CHUNK, LANES), 1)

    def for_groups(fn, carry=0, scan_first=0, scan_total=0):
        n_full, rem = divmod(nc, group)
        m_full = scan_total * group // nc if rem else scan_total // max(n_full, 1)
        if n_full == 1:
            carry = fn(list(range(group)), carry, scan_first, m_full)
        elif n_full:
            carry = lax.fori_loop(
                0, n_full,
                lambda i, c: fn([i * group + j for j in range(group)], c, scan_first + i * m_full, m_full), carry)
        if rem:
            carry = fn([n_full * group + j for j in range(rem)], carry,
                       scan_first + n_full * m_full, scan_total - n_full * m_full)
        return carry

    cur_half = step % 2
    qe_w, mp_w, nn_w, dd_w, osum_w = (r.at[cur_half] for r in (qe_ref, mp_ref, nn_ref, dd_ref, osum_ref))
    qe_r, mp_r, nn_r, dd_r, osum_r = (r.at[1 - cur_half] for r in (qe_ref, mp_ref, nn_ref, dd_ref, osum_ref))

    streams = ((rawq_ref, q_ref, cwq_ref, qs_ref, HEAD_DIM ** -0.5),
               (rawk_ref, k_ref, cwk_ref, ks_ref, 1.0),
               (rawv_ref, v_ref, cwv_ref, vs_ref, None))
    def stage_conv_inputs():
        for raw_ref, src_ref, _, _, _ in streams:
            raw_ref[0:padr + halo, :] = jnp.zeros((padr + halo, LANES), F32)
            raw_ref[padr + halo:padr + halo + seq, :] = src_ref[...].astype(F32)
            raw_ref[lp + halo:lp + 2 * halo, :] = jnp.zeros((halo, LANES), F32)

    def conv_chunk(c):
        base = _aligned(c * CHUNK, CHUNK)
        for raw_ref, _, cw_ref, dst_ref, l2_scale in streams:
            cw = cw_ref[...]
            acc = None
            for j in range(CONV_K):
                off = halo - (CONV_K - 1) // 2 + j
                tap = cw[j:j + 1, :] * raw_ref[pl.ds(base + off, CHUNK), :]
                acc = tap if acc is None else acc + tap
            y = _silu(acc)
            if isinstance(c, int) and c * CHUNK < padr:
                y = jnp.where(row64 + base >= padr, y, 0.0)
            if l2_scale is not None:
                inv = lax.rsqrt(jnp.sum(y * y, axis=-1, keepdims=True) + NORM_EPS)
                y = y * (inv if l2_scale == 1.0 else inv * l2_scale)
            dst_ref[pl.ds(base, CHUNK), :] = y

    raw_ref = rawq_ref

    def stage_gate_inputs():
        raw_ref[0:padr, :] = jnp.zeros((padr, LANES), F32)
        raw_ref[padr:padr + seq, :] = ab_ref[...]

    neg_a = -jnp.exp(prm_ref[0:1, :])
    dt_bias = prm_ref[1:2, :]
    is_g = (lane64 < 8) | ((lane64 >= 16) & (lane64 < 24))
    is_beta = ((lane64 >= 8) & (lane64 < 16)) | ((lane64 >= 24) & (lane64 < 32))
    r128 = lax.broadcasted_iota(jnp.int32, (2 * CHUNK, CHUNK), 0)
    c128 = lax.broadcasted_iota(jnp.int32, (2 * CHUNK, CHUNK), 1)
    tri2 = jnp.where(((r128 < CHUNK) & (r128 >= c128)) | ((r128 >= CHUNK) & (r128 - CHUNK <= c128)),
                     1.0, 0.0).astype(MXU_DTYPE)
    sel_r = lax.broadcasted_iota(jnp.int32, (LANES, 4 * LANES), 0)
    sel_c = lax.broadcasted_iota(jnp.int32, (LANES, 4 * LANES), 1)
    n_gate_cols = 4 * N_HEADS
    sel_packed = jnp.where((sel_r < 3 * n_gate_cols)
                           & (sel_r % n_gate_cols == head + N_HEADS * (sel_c // LANES)), 1.0, 0.0).astype(MXU_DTYPE)
    tri2x3 = jnp.concatenate([tri2, tri2, tri2], axis=1)

    def gate_batch_group(cs, carry, *_):
        bases = [_aligned(c * CHUNK, CHUNK) for c in cs]
        gates, parts = [], []
        for base in bases:
            ab = raw_ref[pl.ds(base, CHUNK), :]
            g = neg_a * jax.nn.softplus(ab + dt_bias)
            beta = jax.nn.sigmoid(ab)
            gt = jnp.where(is_g, g, jnp.where(is_beta, beta, 0.0))
            gt = jnp.where(row64 + base >= padr, gt, 0.0)
            gates.append(gt)
            parts.append(jnp.concatenate(_split3(gt), axis=0))
        cums = [_dot(tri2x3, p) for p in parts]
        for c, base, cm, gt in zip(cs, bases, cums, gates):
            m = jnp.where(lane64 < 8, cm[:CHUNK], jnp.where((lane64 >= 16) & (lane64 < 24), cm[CHUNK:], gt))
            gmix_ref[pl.ds(base, CHUNK), :] = m
            grow_ref[pl.ds(_aligned(c * LANES, LANES), LANES), :] = jnp.concatenate([m, m], axis=0).T
        return carry

    def gate_head_group(cs, carry, *_):
        rows = len(cs) * CHUNK
        base = _aligned(cs[0] * CHUNK, CHUNK)
        hi, mid, lo = _split3(gmix_ref[pl.ds(base, rows), :])
        packed = (hi.astype(F32) + pltpu.roll(mid.astype(F32), n_gate_cols, 1)
                  + pltpu.roll(lo.astype(F32), 2 * n_gate_cols, 1)).astype(MXU_DTYPE)
        gsel_ref[pl.ds(base, rows), :] = _dot(packed, sel_packed)
        return carry

    fwd_half = lane64 < CHUNK
    col64 = lane64 & (CHUNK - 1)
    incl2 = (fwd_half & (row64 >= col64)) | (~fwd_half & (row64 <= col64))
    strict2 = (fwd_half & (row64 > col64)) | (~fwd_half & (row64 < col64))
    eye2 = jnp.where(row64 == col64, 1.0, 0.0).astype(F32)

    def blockdiag(p):
        return jnp.concatenate([jnp.where(fwd_half, p, 0.0), jnp.where(fwd_half, 0.0, p)],
                               axis=0).astype(MXU_DTYPE)

    n_neumann = int(math.log2(CHUNK)) - 2
    n_stages = n_neumann + 6

    def chunk_group(cs, carry, scan_first, scan_steps):
        n = len(cs)
        bases = [_aligned(c * CHUNK, CHUNK) for c in cs]
        cb128 = [_aligned(c * LANES, LANES) for c in cs]
        cb8 = [_aligned(c * SUBLANES, SUBLANES) for c in cs]
        stage = [0]

        def end_of_stage(carry):
            s = stage[0]
            stage[0] += 1
            for j in range(scan_steps):
                if j * n_stages // scan_steps == s:
                    carry = scan_step(scan_first + j, carry)
            return carry

        def gate_cols(j):
            gs = gsel_ref[pl.ds(bases[j], CHUNK), :]
            return gs[:, 0:LANES], gs[:, LANES:2 * LANES], gs[:, 2 * LANES:3 * LANES], gs[:, 3 * LANES:]

        kq = []
        for j in range(n):
            k16 = ks_ref[pl.ds(bases[j], CHUNK), :].astype(MXU_DTYPE)
            q16 = qs_ref[pl.ds(bases[j], CHUNK), :].astype(MXU_DTYPE)
            kq.append(_dot_nt(jnp.concatenate([k16, q16], axis=0), jnp.concatenate([k16, k16], axis=0)))
        carry = end_of_stage(carry)

        ts, ps, qkds = [], [], []
        for j in range(n):
            gf, bf, gb, bb = gate_cols(j)
            row_f = grow_ref[pl.ds(cb128[j] + head, 1), :]
            row_b = grow_ref[pl.ds(cb128[j] + 2 * N_HEADS + head, 1), :]
            g_col = jnp.where(fwd_half, gf, gb)
            g_row = jnp.where(fwd_half, row_f, row_b)
            decay = jnp.where(incl2, jnp.exp(jnp.where(incl2, g_col - g_row, 0.0)), 0.0)
            a2 = jnp.where(strict2, kq[j][:CHUNK] * jnp.where(fwd_half, bf, bb) * decay, 0.0)
            qkds.append(jnp.where(incl2, kq[j][CHUNK:] * decay, 0.0))
            ts.append(eye2 - a2)
            ps.append(a2)

        ps = [_dot(p.astype(MXU_DTYPE), blockdiag(p)) for p in ps]
        carry = end_of_stage(carry)
        for _ in range(n_neumann):
            tps = [_dot(jnp.concatenate([t, p], axis=0).astype(MXU_DTYPE), blockdiag(p)) for t, p in zip(ts, ps)]
            carry = end_of_stage(carry)
            ts = [t + tp[:CHUNK] for t, tp in zip(ts, tps)]
            ps = [tp[CHUNK:] for tp in tps]
        ts = [t + _dot(t.astype(MXU_DTYPE), blockdiag(p)) for t, p in zip(ts, ps)]
        carry = end_of_stage(carry)

        uws = []
        for j in range(n):
            gf, bf, gb, bb = gate_cols(j)
            k = ks_ref[pl.ds(bases[j], CHUNK), :]
            v = vs_ref[pl.ds(bases[j], CHUNK), :]
            x_f = jnp.concatenate([v * bf, k * bf * jnp.exp(gf)], axis=1)
            x_b = jnp.concatenate([v * bb, k * bb * jnp.exp(gb)], axis=1)
            uw = _dot(blockdiag(ts[j]), jnp.concatenate([x_f, x_b], axis=0).astype(MXU_DTYPE))
            uws.append(uw.astype(MXU_DTYPE))
        carry = end_of_stage(carry)

        xxs = [_dot(blockdiag(qkds[j]), uws[j]) for j in range(n)]
        carry = end_of_stage(carry)
        for j in range(n):
            gf, _, gb, _ = gate_cols(j)
            k = ks_ref[pl.ds(bases[j], CHUNK), :]
            q = qs_ref[pl.ds(bases[j], CHUNK), :]
            xx = xxs[j]
            osum_w[pl.ds(bases[j], CHUNK), :] = xx[:CHUNK, :LANES] + xx[CHUNK:, :LANES]
            for d, gc in enumerate((gf, gb)):
                rows = slice(d * CHUNK, (d + 1) * CHUNK)
                g_last = gc[CHUNK - 1:CHUNK, :] if d == 0 else gc[0:1, :]
                k_dec = k * jnp.exp(g_last - gc)
                mn = _dot_tn(k_dec.astype(MXU_DTYPE), uws[j][rows])
                qe_w[d, pl.ds(bases[j], CHUNK), :] = (q * jnp.exp(gc) - xx[rows, LANES:]).astype(MXU_DTYPE)
                nn_w[d, pl.ds(cb128[j], HEAD_DIM), :] = mn[:, :LANES]
                mp_w[d, pl.ds(cb128[j], HEAD_DIM), :] = mn[:, LANES:].astype(MXU_DTYPE)
                dd_w[d, pl.ds(cb8[j], SUBLANES), :] = jnp.broadcast_to(jnp.exp(g_last), (SUBLANES, LANES))
        carry = end_of_stage(carry)
        assert stage[0] == n_stages
        return carry

    def scan_step(i, carry):
        new = []
        for d in range(2):
            s = carry[d]
            c = i if d == 0 else nc - 1 - i
            base = _aligned(c * CHUNK, CHUNK)
            cbase = _aligned(c * HEAD_DIM, HEAD_DIM)
            dbase = _aligned(c * SUBLANES, SUBLANES)
            s16 = s.astype(MXU_DTYPE)
            osum_r[pl.ds(base, CHUNK), :] += _dot(qe_r[d, pl.ds(base, CHUNK), :], s16)
            dec = dd_r[d, pl.ds(dbase, 1), :]
            new.append(dec * s - _dot(mp_r[d, pl.ds(cbase, HEAD_DIM), :], s16)
                       + nn_r[d, pl.ds(cbase, HEAD_DIM), :])
        return tuple(new)

    zero_state = jnp.zeros((HEAD_DIM, HEAD_DIM), F32)

    def write_output():
        gain = ong_ref[...]

        def finish(o, z):
            ms = jnp.mean(o * o, axis=-1, keepdims=True)
            return (o * lax.rsqrt(ms + NORM_EPS) * gain * _silu(z.astype(F32))).astype(o_ref.dtype)

        first = CHUNK - padr if padr else 0
        if first:
            o_ref[0:first, :] = finish(osum_r[padr:CHUNK, :], z_ref[0:first, :])

        def out_body(c, carry):
            base = pl.multiple_of(c * CHUNK, CHUNK)
            dst = pl.multiple_of(c * CHUNK - padr, SUBLANES)
            o_ref[pl.ds(dst, CHUNK), :] = finish(osum_r[pl.ds(base, CHUNK), :], z_ref[pl.ds(dst, CHUNK), :])
            return carry

        lax.fori_loop(1 if first else 0, nc, out_body, 0, unroll=8)

    def conv_only(c, carry):
        conv_chunk(c)
        return carry

    conv_per_scan = 3
    n_conv_scan = (nc - 1) // conv_per_scan

    def conv_and_scan(i, carry):
        for u in range(conv_per_scan):
            conv_chunk(1 + conv_per_scan * i + u)
        return scan_step(i, carry)

    def gates():
        @pl.when(head == 0)
        def _():
            stage_gate_inputs()
            for_groups(gate_batch_group)

        for_groups(gate_head_group)

    @pl.when(step == 0)
    def _():
        stage_conv_inputs()
        conv_chunk(0)
        lax.fori_loop(1, nc, conv_only, 0, unroll=4)
        gates()
        for_groups(chunk_group)

    @pl.when((step > 0) & (step < n_heads_total))
    def _():
        stage_conv_inputs()
        conv_chunk(0)
        carry = lax.fori_loop(0, n_conv_scan, conv_and_scan, (zero_state, zero_state))
        lax.fori_loop(1 + conv_per_scan * n_conv_scan, nc, conv_only, 0)
        gates()
        for_groups(chunk_group, carry, n_conv_scan, nc - n_conv_scan)
        write_output()

    @pl.when(step == n_heads_total)
    def _():
        lax.fori_loop(0, nc, scan_step, (zero_state, zero_state))
        write_output()


def _gdn(proj3, ab3, conv_w, gate_prm, out_norm_g):
    b, seq, _ = proj3.shape
    lp = -(-seq // CHUNK) * CHUNK
    nc = lp // CHUNK
    n_total = b * N_HEADS
    cur = lambda i: jnp.minimum(i, n_total - 1)
    prev = lambda i: jnp.maximum(i - 1, 0)
    col = lambda off: pl.BlockSpec((None, seq, LANES), lambda i: (cur(i) // N_HEADS, 0, off + cur(i) % N_HEADS))
    cw = lambda off: pl.BlockSpec((CONV_K, LANES), lambda i: (0, off + cur(i) % N_HEADS))
    return pl.pallas_call(
        functools.partial(_gdn_kernel, seq=seq, lp=lp, group=GDN_GROUP, n_heads_total=n_total),
        grid=(n_total + 1,),
        in_specs=[
            col(COL_QKV // LANES), col(COL_QKV // LANES + N_HEADS), col(COL_QKV // LANES + 2 * N_HEADS),
            pl.BlockSpec((None, seq, LANES),
                         lambda i: (prev(i) // N_HEADS, 0, COL_Z // LANES + prev(i) % N_HEADS)),
            pl.BlockSpec((None, seq, LANES), lambda i: (cur(i) // N_HEADS, 0, 0)),
            cw(0), cw(N_HEADS), cw(2 * N_HEADS),
            pl.BlockSpec((SUBLANES, LANES), lambda i: (0, 0)),
            pl.BlockSpec((1, LANES), lambda i: (0, 0)),
        ],
        out_specs=pl.BlockSpec((None, seq, LANES), lambda i: (prev(i) // N_HEADS, 0, prev(i) % N_HEADS)),
        out_shape=jax.ShapeDtypeStruct((b, seq, N_HEADS * HEAD_DIM), MXU_DTYPE),
        scratch_shapes=[
            pltpu.VMEM((lp + 2 * SUBLANES, LANES), F32),
            pltpu.VMEM((lp + 2 * SUBLANES, LANES), F32),
            pltpu.VMEM((lp + 2 * SUBLANES, LANES), F32),
            pltpu.VMEM((lp, LANES), F32),
            pltpu.VMEM((lp, LANES), F32),
            pltpu.VMEM((lp, LANES), F32),
            pltpu.VMEM((lp, 4 * LANES), F32),
            pltpu.VMEM((nc * LANES, LANES), F32),
            pltpu.VMEM((lp, LANES), F32),
            pltpu.VMEM((2, 2, lp, LANES), MXU_DTYPE),
            pltpu.VMEM((2, 2, nc * HEAD_DIM, LANES), MXU_DTYPE),
            pltpu.VMEM((2, 2, nc * HEAD_DIM, LANES), F32),
            pltpu.VMEM((2, 2, nc * SUBLANES, LANES), F32),
            pltpu.VMEM((2, lp, LANES), F32),
        ],
        compiler_params=_params(("arbitrary",)),
        name="gdn",
    )(proj3, proj3, proj3, proj3, ab3, conv_w, conv_w, conv_w, gate_prm, out_norm_g)


def _merge_kernel(h_ref, og_ref, ym_ref, gf_ref, gd_ref, wf_ref, wd_ref, wo_ref, n2_ref, wr_ref, br_ref,
                  h1_ref, hn_ref, ridx_ref, rw_ref, cnt_ref):
    @pl.when(pl.program_id(0) == 0)
    def _():
        cnt_ref[...] = jnp.zeros_like(cnt_ref)

    y_f = _dot(ym_ref[...], wf_ref[...])
    y_d = _dot(og_ref[...], wd_ref[...])
    merged = (jax.nn.sigmoid(gf_ref[...].astype(F32)) * y_f
              + jax.nn.sigmoid(gd_ref[...].astype(F32)) * y_d)
    h1 = h_ref[...] + _dot(merged.astype(MXU_DTYPE), wo_ref[...])
    h1_ref[...] = h1
    ms = jnp.mean(h1 * h1, axis=-1, keepdims=True)
    hn = h1 * lax.rsqrt(ms + NORM_EPS) * n2_ref[...]
    hn_ref[...] = _pack_rows(hn)

    x_hi = hn.astype(MXU_DTYPE)
    x_lo = (hn - x_hi.astype(F32)).astype(MXU_DTYPE)
    w = wr_ref[...]
    w_hi = w.astype(MXU_DTYPE)
    w_lo = (w - w_hi.astype(F32)).astype(MXU_DTYPE)
    hi_terms = _dot(x_hi, jnp.concatenate([w_hi, w_lo], axis=1))
    logits = hi_terms[:, :LANES] + hi_terms[:, LANES:] + _dot(x_lo, w_hi) + br_ref[...]

    tm = logits.shape[0]
    lane = lax.broadcasted_iota(jnp.int32, (tm, LANES), 1)
    neg_inf = -jnp.inf
    is_group = lane < N_GROUPS
    gl = jnp.where(is_group, logits, neg_inf)
    ge = jnp.exp(gl - jnp.max(gl, axis=-1, keepdims=True))
    gp = ge / jnp.sum(ge, axis=-1, keepdims=True)
    p_group = jnp.max(gp, axis=-1, keepdims=True)
    g_idx = jnp.min(jnp.where(is_group & (gp == p_group), lane, LANES), axis=-1, keepdims=True)

    in_group = (lane >= N_GROUPS) & (lane < N_GROUPS + N_EXPERTS) & (((lane - N_GROUPS) >> 3) == g_idx)
    el = jnp.where(in_group, logits, neg_inf)
    ee = jnp.exp(el - jnp.max(el, axis=-1, keepdims=True))
    ep = ee / jnp.sum(ee, axis=-1, keepdims=True)
    v1 = jnp.max(jnp.where(in_group, ep, -1.0), axis=-1, keepdims=True)
    i1 = jnp.min(jnp.where(in_group & (ep == v1), lane, LANES), axis=-1, keepdims=True)
    rest = in_group & (lane != i1)
    v2 = jnp.max(jnp.where(rest, ep, -1.0), axis=-1, keepdims=True)
    i2 = jnp.min(jnp.where(rest & (ep == v2), lane, LANES), axis=-1, keepdims=True)
    denom = v1 + v2
    w1 = p_group * (v1 / denom)
    w2 = p_group * (v2 / denom)
    e1 = i1 - N_GROUPS
    e2 = i2 - N_GROUPS
    ridx_ref[...] = jnp.where(lane == 0, e1, jnp.where(lane == 1, e2, 0))
    rw_ref[...] = jnp.where(lane == 0, w1, jnp.where(lane == 1, w2, 0.0))
    onehots = jnp.where((lane == e1) | (lane == e2), 1.0, 0.0)
    cnt_ref[...] += jnp.broadcast_to(jnp.sum(onehots, axis=0, keepdims=True), cnt_ref.shape)


def _merge(h2d, og2d, ymix2d, proj2d, w_fourier, w_delta, w_out, norm2_g, w_router, b_router, tm):
    t, d = h2d.shape
    row = lambda w: pl.BlockSpec((tm, w), lambda i: (i, 0))
    const = lambda shape: pl.BlockSpec(shape, lambda i: (0, 0))
    return pl.pallas_call(
        _merge_kernel,
        grid=(t // tm,),
        in_specs=[
            row(d), row(d), row(F_WIDTH),
            pl.BlockSpec((tm, d), lambda i: (i, COL_GATE_F // d)),
            pl.BlockSpec((tm, d), lambda i: (i, COL_GATE_D // d)),
            const((F_WIDTH, d)), const((d, d)), const((d, d)), const((1, d)),
            const((d, LANES)), const((1, LANES)),
        ],
        out_specs=[row(d), row(d // 2), row(LANES), row(LANES), const((SUBLANES, LANES))],
        out_shape=[
            jax.ShapeDtypeStruct((t, d), F32),
            jax.ShapeDtypeStruct((t, d // 2), jnp.uint32),
            jax.ShapeDtypeStruct((t, LANES), jnp.int32),
            jax.ShapeDtypeStruct((t, LANES), F32),
            jax.ShapeDtypeStruct((SUBLANES, LANES), F32),
        ],
        compiler_params=_params(("arbitrary",)),
        name="merge",
    )(h2d, og2d, ymix2d, proj2d, proj2d, w_fourier, w_delta, w_out, norm2_g, w_router, b_router)


def _plan_kernel(ridx_ref, cnt_ref, dest_ref, seg_ref, base_ref, run_ref):
    tt = ridx_ref.shape[0]

    @pl.when(pl.program_id(0) == 0)
    def _():
        cnt = cnt_ref[...].astype(jnp.int32)
        n_blk = (cnt + (EXPERT_BLOCK - 1)) // EXPERT_BLOCK
        padded = (n_blk * EXPERT_BLOCK).astype(F32)
        r = lax.broadcasted_iota(jnp.int32, (LANES, LANES), 0)
        c = lax.broadcasted_iota(jnp.int32, (LANES, LANES), 1)
        before = jnp.where(r < c, 1.0, 0.0).astype(MXU_DTYPE)
        start = _exact_right(padded, before)
        base_ref[...] = start
        run_ref[...] = jnp.zeros_like(run_ref)
        seg_ref[0:SUBLANES, :] = start.astype(jnp.int32)
        seg_ref[SUBLANES:2 * SUBLANES, :] = n_blk
        seg_ref[2 * SUBLANES:, :] = cnt

    lane = lax.broadcasted_iota(jnp.int32, (tt, LANES), 1)
    ridx = ridx_ref[...]
    oh0 = jnp.where(lane == ridx[:, 0:1], 1.0, 0.0)
    oh1 = jnp.where(lane == ridx[:, 1:2], 1.0, 0.0)
    r = lax.broadcasted_iota(jnp.int32, (tt, tt), 0)
    c = lax.broadcasted_iota(jnp.int32, (tt, tt), 1)
    earlier = jnp.where(c < r, 1.0, 0.0).astype(MXU_DTYPE)
    prefix = _dot(earlier, jnp.concatenate([oh0, oh1], axis=1).astype(MXU_DTYPE))
    c0 = jnp.sum(oh0, axis=0, keepdims=True)
    c1 = jnp.sum(oh1, axis=0, keepdims=True)
    first = base_ref[0:1, :] + run_ref[0:1, :]
    d0 = jnp.sum(oh0 * (prefix[:, :LANES] + first), axis=-1, keepdims=True)
    d1 = jnp.sum(oh1 * (prefix[:, LANES:] + first + c0), axis=-1, keepdims=True)
    run_ref[...] += jnp.broadcast_to(c0 + c1, run_ref.shape)
    both = jnp.where(lane == 0, d0, jnp.where(lane == 1, d1, 0.0))
    both_t = both.T.astype(jnp.int32)
    for k in range(TOP_K):
        dest_ref[k] = both_t[k:k + 1, :]


def _plan(ridx, cnt, tt):
    t = ridx.shape[0]
    return pl.pallas_call(
        _plan_kernel,
        grid=(t // tt,),
        in_specs=[pl.BlockSpec((tt, LANES), lambda i: (i, 0)),
                  pl.BlockSpec((SUBLANES, LANES), lambda i: (0, 0))],
        out_specs=[pl.BlockSpec((TOP_K, None, 1, tt), lambda i: (0, i, 0, 0)),
                   pl.BlockSpec((3 * SUBLANES, LANES), lambda i: (0, 0))],
        out_shape=[jax.ShapeDtypeStruct((TOP_K, t // tt, 1, tt), jnp.int32),
                   jax.ShapeDtypeStruct((3 * SUBLANES, LANES), jnp.int32)],
        scratch_shapes=[pltpu.VMEM((SUBLANES, LANES), F32), pltpu.VMEM((SUBLANES, LANES), F32)],
        compiler_params=_params(("arbitrary",)),
        name="plan",
    )(ridx, cnt)


def _scatter_kernel(dest0_ref, dest1_ref, hn_ref, xb_in_ref, xb_ref, sem, *, ts):
    dest_refs = (dest0_ref, dest1_ref)
    del xb_in_ref

    def issue(t, carry):
        src = hn_ref.at[pl.ds(t, 1)]
        for k in range(TOP_K):
            pltpu.make_async_copy(src, xb_ref.at[pl.ds(dest_refs[k][0, t], 1)], sem).start(priority=k % 2)
        return carry

    lax.fori_loop(0, ts, issue, 0, unroll=DMA_UNROLL)
    for _ in range(TOP_K):
        pltpu.make_async_copy(hn_ref, xb_ref.at[pl.ds(0, ts)], sem).wait()


def _scatter(dest_flat, hn2d, xb_zero, ts):
    t, d = hn2d.shape
    return pl.pallas_call(
        functools.partial(_scatter_kernel, ts=ts),
        grid=(t // ts,),
        in_specs=[pl.BlockSpec((None, None, 1, ts), lambda i: (0, i, 0, 0), memory_space=pltpu.SMEM),
                  pl.BlockSpec((None, None, 1, ts), lambda i: (1, i, 0, 0), memory_space=pltpu.SMEM),
                  pl.BlockSpec((ts, d), lambda i: (i, 0)),
                  pl.BlockSpec(memory_space=pl.ANY)],
        out_specs=pl.BlockSpec(memory_space=pl.ANY),
        out_shape=jax.ShapeDtypeStruct(xb_zero.shape, xb_zero.dtype),
        scratch_shapes=[pltpu.SemaphoreType.DMA(())],
        input_output_aliases={3: 0},
        compiler_params=_params(("arbitrary",)),
        name="scatter",
    )(dest_flat, dest_flat, hn2d, xb_zero)


def _expert_kernel(start_ref, nblk_ref, xb_ref, wg_ref, wu_ref, wd_ref, yb_ref,
                   wg16_ref, wu16_ref, wd16_ref, xbuf_ref, ybuf_ref, sem_in, sem_out):
    expert = pl.program_id(0)
    last = pl.num_programs(0) - 1
    n = nblk_ref[expert]
    first = start_ref[expert] // EXPERT_BLOCK
    total = start_ref[last] // EXPERT_BLOCK + nblk_ref[last]

    def rows(g):
        return pl.ds(pl.multiple_of(g * EXPERT_BLOCK, EXPERT_BLOCK), EXPERT_BLOCK)

    def x_copy(g):
        slot = g % EXPERT_X_SLOTS
        return pltpu.make_async_copy(xb_ref.at[rows(g)], xbuf_ref.at[slot], sem_in.at[slot])

    def y_copy(g):
        slot = g % EXPERT_Y_SLOTS
        return pltpu.make_async_copy(ybuf_ref.at[slot], yb_ref.at[rows(g)], sem_out.at[slot])

    @pl.when(n > 0)
    def _():
        @pl.when(first == 0)
        def _():
            for j in range(EXPERT_X_SLOTS - 1):
                @pl.when(j < total)
                def _():
                    x_copy(j).start()

        wg16_ref[...] = wg_ref[...].astype(MXU_DTYPE)
        wu16_ref[...] = wu_ref[...].astype(MXU_DTYPE)
        wd16_ref[...] = wd_ref[...].astype(MXU_DTYPE)

        def body(i, carry):
            g = first + i

            @pl.when(g + (EXPERT_X_SLOTS - 1) < total)
            def _():
                x_copy(g + (EXPERT_X_SLOTS - 1)).start()

            x_copy(g).wait()

            @pl.when(g >= EXPERT_Y_SLOTS)
            def _():
                y_copy(g - EXPERT_Y_SLOTS).wait()

            x = _unpack_rows(xbuf_ref[g % EXPERT_X_SLOTS]).astype(MXU_DTYPE)
            gate = _dot(x, wg16_ref[...])
            up = _dot(x, wu16_ref[...])
            ybuf_ref[g % EXPERT_Y_SLOTS] = _pack_rows(_dot((_silu(gate) * up).astype(MXU_DTYPE), wd16_ref[...]))
            y_copy(g).start()
            return carry

        lax.fori_loop(0, n, body, 0)

    @pl.when(expert == last)
    def _():
        for j in range(EXPERT_Y_SLOTS):
            @pl.when(total > j)
            def _():
                y_copy(total - 1 - j).wait()


def _experts(seg_start, seg_blocks, xb, w_gate, w_up, w_down):
    n_slots, row_words = xb.shape
    n_exp, d, de = w_gate.shape
    grid_spec = pltpu.PrefetchScalarGridSpec(
        num_scalar_prefetch=2,
        grid=(n_exp,),
        in_specs=[
            pl.BlockSpec(memory_space=pl.ANY),
            pl.BlockSpec((None, d, de), lambda e, st, nb: (e, 0, 0)),
            pl.BlockSpec((None, d, de), lambda e, st, nb: (e, 0, 0)),
            pl.BlockSpec((None, de, d), lambda e, st, nb: (e, 0, 0)),
        ],
        out_specs=pl.BlockSpec(memory_space=pl.ANY),
        scratch_shapes=[
            pltpu.VMEM((d, de), MXU_DTYPE), pltpu.VMEM((d, de), MXU_DTYPE), pltpu.VMEM((de, d), MXU_DTYPE),
            pltpu.VMEM((EXPERT_X_SLOTS, EXPERT_BLOCK, row_words), jnp.uint32),
            pltpu.VMEM((EXPERT_Y_SLOTS, EXPERT_BLOCK, row_words), jnp.uint32),
            pltpu.SemaphoreType.DMA((EXPERT_X_SLOTS,)), pltpu.SemaphoreType.DMA((EXPERT_Y_SLOTS,)),
        ],
    )
    return pl.pallas_call(
        _expert_kernel,
        grid_spec=grid_spec,
        out_shape=jax.ShapeDtypeStruct((n_slots, row_words), jnp.uint32),
        input_output_aliases={2: 0},
        compiler_params=_params(("arbitrary",)),
        name="experts",
    )(seg_start, seg_blocks, xb, w_gate, w_up, w_down)


def _combine_kernel(dest0_ref, dest1_ref, yb_ref, h1_ref, rw_ref, fg_ref, o_ref, buf_ref, res_ref, sem, out_sem,
                    *, tc, groups_per_batch):
    step = pl.program_id(0)
    n_tiles = pl.num_programs(0) - 1
    n_grp = tc // N_META
    cur = step % 2
    prev = 1 - cur

    def output_copies(tile, action):
        slot = tile % 2
        first = tile * n_grp
        first_in_batch = first % groups_per_batch
        has_meta = (first_in_batch == 0) | (first_in_batch + n_grp > groups_per_batch)

        def copy_groups(q, n):
            grp = first + q
            dst = pl.multiple_of((grp - grp // groups_per_batch - 1) * N_META, N_META)
            copy = pltpu.make_async_copy(res_ref.at[slot, pl.ds(q * N_META, n * N_META)],
                                         o_ref.at[pl.ds(dst, n * N_META)], out_sem.at[slot])
            copy.start() if action == "start" else copy.wait()

        @pl.when(jnp.logical_not(has_meta))
        def _():
            copy_groups(0, n_grp)

        @pl.when(has_meta)
        def _():
            for q in range(n_grp):
                @pl.when((first + q) % groups_per_batch != 0)
                def _():
                    copy_groups(q, 1)

    def request_rows(i):
        for u in range(SUBLANES):
            t = i * SUBLANES + u
            for k in range(TOP_K):
                pltpu.make_async_copy(yb_ref.at[pl.ds((dest0_ref, dest1_ref)[k][0, t], 1)],
                                      buf_ref.at[cur, k, pl.ds(t, 1)], sem.at[cur]).start(priority=k % 2)

    def finish_rows(i):
        rows = pl.ds(pl.multiple_of(i * SUBLANES, SUBLANES), SUBLANES)
        rw = rw_ref[rows, :]
        h2 = h1_ref[rows, :] + (_unpack_rows(buf_ref[prev, 0, rows, :]) * rw[:, 0:1]
                                + _unpack_rows(buf_ref[prev, 1, rows, :]) * rw[:, 1:2])
        ms = jnp.mean(h2 * h2, axis=-1, keepdims=True)
        res_ref[prev, rows, :] = h2 * lax.rsqrt(ms + NORM_EPS) * fg_ref[...]

    def begin_finish():
        for k in range(TOP_K):
            pltpu.make_async_copy(yb_ref.at[pl.ds(0, tc)], buf_ref.at[prev, k], sem.at[prev]).wait()

        @pl.when(step >= 3)
        def _():
            output_copies(step - 3, "wait")

    def both(i, carry):
        request_rows(i)
        finish_rows(i)
        return carry

    def request_only(i, carry):
        request_rows(i)
        return carry

    def finish_only(i, carry):
        finish_rows(i)
        return carry

    @pl.when(step == 0)
    def _():
        lax.fori_loop(0, tc // SUBLANES, request_only, 0)

    @pl.when((step > 0) & (step < n_tiles))
    def _():
        begin_finish()
        lax.fori_loop(0, tc // SUBLANES, both, 0, unroll=16)
        output_copies(step - 1, "start")

    @pl.when(step == n_tiles)
    def _():
        begin_finish()
        lax.fori_loop(0, tc // SUBLANES, finish_only, 0, unroll=4)
        output_copies(step - 1, "start")

        @pl.when(step >= 2)
        def _():
            output_copies(step - 2, "wait")

        output_copies(step - 1, "wait")


def _combine(dest_flat, yb, h1, rw, final_g, tc, seq):
    t, d = h1.shape
    assert tc % N_META == 0 and seq % N_META == 0
    n_out = t - (t // seq) * N_META
    n_tiles = t // tc
    nxt = lambda i: jnp.minimum(i, n_tiles - 1)
    fin = lambda i: jnp.maximum(i - 1, 0)
    return pl.pallas_call(
        functools.partial(_combine_kernel, tc=tc, groups_per_batch=seq // N_META),
        grid=(n_tiles + 1,),
        in_specs=[pl.BlockSpec((None, None, 1, tc), lambda i: (0, nxt(i), 0, 0), memory_space=pltpu.SMEM),
                  pl.BlockSpec((None, None, 1, tc), lambda i: (1, nxt(i), 0, 0), memory_space=pltpu.SMEM),
                  pl.BlockSpec(memory_space=pl.ANY),
                  pl.BlockSpec((tc, d), lambda i: (fin(i), 0)),
                  pl.BlockSpec((tc, LANES), lambda i: (fin(i), 0)),
                  pl.BlockSpec((1, d), lambda i: (0, 0))],
        out_specs=pl.BlockSpec(memory_space=pl.ANY),
        out_shape=jax.ShapeDtypeStruct((n_out, d), F32),
        scratch_shapes=[pltpu.VMEM((2, TOP_K, tc, d // 2), jnp.uint32), pltpu.VMEM((2, tc, d), F32),
                        pltpu.SemaphoreType.DMA((2,)), pltpu.SemaphoreType.DMA((2,))],
        compiler_params=_params(("arbitrary",)),
        name="combine",
    )(dest_flat, dest_flat, yb, h1, rw, final_g)


def _tile(total, candidates):
    for c in candidates:
        if total % c == 0:
            return c
    raise ValueError(f"no tile for {total}")


def _dft_tables(seq):
    kp = -(-seq // LANES) * LANES
    nat = -(-(seq // 2 + 1) // 16) * 16
    mir = seq - nat
    c = np.arange(FGROUP_DIM, dtype=np.int64)
    ang_c = (2.0 * np.pi / FGROUP_DIM) * ((c[:, None] * c[None, :]) % FGROUP_DIM)
    cs128 = np.concatenate([np.cos(ang_c), np.sin(ang_c)], axis=1)
    k = np.arange(nat, dtype=np.int64)
    n = np.arange(seq, dtype=np.int64)
    ang_l = (2.0 * np.pi / seq) * ((k[:, None] * n[None, :]) % seq)
    twid = np.zeros((nat, 2 * kp), np.float32)
    twid[:, :seq] = np.cos(ang_l)
    twid[:, kp:kp + seq] = np.sin(ang_l)
    flip = np.zeros((mir, nat), np.float32)
    flip[np.arange(mir), mir - np.arange(mir)] = 1.0
    as_const = lambda t: jnp.asarray(t.astype(MXU_DTYPE))
    return as_const(cs128), as_const(twid), as_const(flip)


def kernel(x, meta_tokens, norm1_g, w_in, conv_w, a_log_fwd, dt_bias_fwd, a_log_bwd, dt_bias_bwd, out_norm_g, w_fourier, w_delta, w_out, norm2_g, w_router_group, b_router_group, w_router_expert, b_router_expert, w_gate_e, w_up_e, w_down_e, final_norm_g):
    bsz, n_real, d = x.shape
    seq = N_META + n_real
    n_tok = bsz * seq
    assert (CHUNK - N_META) % CHUNK + seq == -(-seq // CHUNK) * CHUNK, "meta tokens must fill the tail of chunk 0"
    assert w_in.shape[0] == 1, "single trunk layer"
    layer = 0


    w = w_in[layer]
    o_f, o_qkv, o_z, o_ab, o_gf, o_gd = 0, F_WIDTH, F_WIDTH + 3 * QK_WIDTH, F_WIDTH + 4 * QK_WIDTH, \
        F_WIDTH + 4 * QK_WIDTH + 4 * N_HEADS, F_WIDTH + 4 * QK_WIDTH + 4 * N_HEADS + d
    w_cat = jnp.concatenate(
        [w[:, o_qkv:o_z], w[:, o_z:o_ab], w[:, o_gf:o_gd], w[:, o_gd:o_gd + d], w[:, o_f:o_qkv]],
        axis=1).astype(MXU_DTYPE)
    w_ab = jnp.pad(w[:, o_ab:o_gf], ((0, 0), (0, LANES - 4 * N_HEADS))).astype(MXU_DTYPE)

    proj, ab, h = _in_proj(x.reshape(bsz * n_real, d), meta_tokens.astype(x.dtype), norm1_g[layer][None, :],
                           w_cat, w_ab, seq)
    proj3 = proj.reshape(bsz, seq, PROJ_WIDTH)

    ymix = _fnet(proj3, *_dft_tables(seq))

    gate_prm = jnp.zeros((SUBLANES, LANES), F32)
    gate_prm = gate_prm.at[0, 0:N_HEADS].set(a_log_fwd[layer]).at[0, 2 * N_HEADS:3 * N_HEADS].set(a_log_bwd[layer])
    gate_prm = gate_prm.at[1, 0:N_HEADS].set(dt_bias_fwd[layer]).at[1, 2 * N_HEADS:3 * N_HEADS].set(dt_bias_bwd[layer])
    og = _gdn(proj3, ab.reshape(bsz, seq, LANES), conv_w[layer], gate_prm, out_norm_g[layer][None, :])

    w_router = jnp.zeros((d, LANES), F32)
    w_router = w_router.at[:, 0:N_GROUPS].set(w_router_group[layer])
    w_router = w_router.at[:, N_GROUPS:N_GROUPS + N_EXPERTS].set(w_router_expert[layer])
    b_router = jnp.zeros((1, LANES), F32)
    b_router = b_router.at[0, 0:N_GROUPS].set(b_router_group[layer])
    b_router = b_router.at[0, N_GROUPS:N_GROUPS + N_EXPERTS].set(b_router_expert[layer])
    tt = _tile(n_tok, (384, 128))
    h1, hn, ridx, rw, cnt = _merge(
        h, og.reshape(n_tok, d), ymix.reshape(n_tok, F_WIDTH), proj,
        w_fourier[layer].astype(MXU_DTYPE), w_delta[layer].astype(MXU_DTYPE), w_out[layer].astype(MXU_DTYPE),
        norm2_g[layer][None, :], w_router, b_router, _tile(n_tok, (688, 384, 128)))

    n_assign = n_tok * TOP_K
    n_blocks = -(-n_assign // EXPERT_BLOCK) + N_EXPERTS
    dest, seg = _plan(ridx, cnt, tt)
    seg_start = seg[0, :N_EXPERTS]
    seg_blocks = seg[SUBLANES, :N_EXPERTS]

    xb = _scatter(dest, hn, jnp.zeros((n_blocks * EXPERT_BLOCK, d // 2), jnp.uint32), tt)
    yb = _experts(seg_start, seg_blocks, xb, w_gate_e[layer], w_up_e[layer], w_down_e[layer])
    out = _combine(dest, yb, h1, rw, final_norm_g[None, :], tt, seq)
    return out.reshape(bsz, n_real, d)
```

```python
import functools
import math

import jax
import jax.numpy as jnp
import numpy as np
from jax import lax
from jax.experimental import pallas as pl
from jax.experimental.pallas import tpu as pltpu

F32 = jnp.float32
MXU_DTYPE = jnp.bfloat16

N_META = 16
CHUNK = 64
N_FGROUPS = 4
FGROUP_DIM = 128
F_WIDTH = N_FGROUPS * FGROUP_DIM
N_HEADS = 8
HEAD_DIM = 128
QK_WIDTH = N_HEADS * HEAD_DIM
CONV_K = 5
N_GROUPS = 8
EXPERTS_PER_GROUP = 8
N_EXPERTS = N_GROUPS * EXPERTS_PER_GROUP
TOP_K = 2
EXPERT_BLOCK = 256
NORM_EPS = 1e-6
LANES = 128
SUBLANES = 8
VMEM_LIMIT = 52 * 1024 * 1024
GDN_GROUP = 17
DMA_UNROLL = 16
EXPERT_X_SLOTS = 6
EXPERT_Y_SLOTS = 4

COL_QKV = 0
COL_Z = 3 * QK_WIDTH
COL_GATE_F = COL_Z + QK_WIDTH
COL_GATE_D = COL_GATE_F + 1024
COL_FIN = COL_GATE_D + 1024
PROJ_WIDTH = COL_FIN + F_WIDTH
PROJ_TN = 3328
PROJ_DTYPE = jnp.bfloat16


def _dot(a, b):
    return jnp.dot(a, b, preferred_element_type=F32)


def _dot_nt(a, b):
    return lax.dot_general(a, b, (((1,), (1,)), ((), ())), preferred_element_type=F32)


def _dot_tn(a, b):
    return lax.dot_general(a, b, (((0,), (0,)), ((), ())), preferred_element_type=F32)


def _split3(x):
    hi = x.astype(MXU_DTYPE)
    r1 = x - hi.astype(F32)
    mid = r1.astype(MXU_DTYPE)
    lo = (r1 - mid.astype(F32)).astype(MXU_DTYPE)
    return hi, mid, lo


def _exact_left(sel, x):
    hi, mid, lo = _split3(x)
    return _dot(sel, hi) + _dot(sel, mid) + _dot(sel, lo)


def _exact_right(x, sel):
    hi, mid, lo = _split3(x)
    return _dot(hi, sel) + _dot(mid, sel) + _dot(lo, sel)


def _pack_rows(x):
    half = x.shape[1] // 2
    lo = lax.bitcast_convert_type(x[:, :half].astype(jnp.bfloat16).astype(F32), jnp.uint32)
    hi = lax.bitcast_convert_type(x[:, half:].astype(jnp.bfloat16).astype(F32), jnp.uint32)
    return (lo >> 16) | hi


def _unpack_rows(u):
    lo = lax.bitcast_convert_type(u << 16, F32)
    hi = lax.bitcast_convert_type(u & jnp.uint32(0xFFFF0000), F32)
    return jnp.concatenate([lo, hi], axis=1)


def _silu(x):
    return x * jax.nn.sigmoid(x)


def _params(sem):
    return pltpu.CompilerParams(dimension_semantics=sem, vmem_limit_bytes=VMEM_LIMIT)


def _inproj_kernel(x_ref, meta_ref, g_ref, w_ref, wab_ref, o_ref, ab_ref, h_ref, hn_ref):
    @pl.when(pl.program_id(1) == 0)
    def _():
        tm = h_ref.shape[0]

        @pl.when(pl.program_id(0) % 2 == 0)
        def _():
            h_ref[0:N_META, :] = meta_ref[...]
            h_ref[N_META:tm, :] = x_ref[0:tm - N_META, :]

        @pl.when(pl.program_id(0) % 2 == 1)
        def _():
            h_ref[...] = x_ref[...]

        x = h_ref[...]
        ms = jnp.mean(x * x, axis=-1, keepdims=True)
        hn_ref[...] = (x * lax.rsqrt(ms + NORM_EPS) * g_ref[...]).astype(MXU_DTYPE)
        ab_ref[...] = _dot(hn_ref[...], wab_ref[...])

    o_ref[...] = _dot(hn_ref[...], w_ref[...]).astype(o_ref.dtype)


def _in_proj(x2d, meta, gain, w_cat, w_ab, seq):
    n_x, d = x2d.shape
    n_real = seq - N_META
    t = n_x // n_real * seq
    tm = seq // 2
    assert seq % (2 * SUBLANES) == 0 and tm >= N_META
    n = w_cat.shape[1]
    x_spec = pl.BlockSpec(
        (pl.Element(tm), pl.Element(d)),
        lambda i, j: (((i // 2) * (n_real // SUBLANES) + (i % 2) * ((tm - N_META) // SUBLANES)) * SUBLANES, 0))
    return pl.pallas_call(
        _inproj_kernel,
        grid=(t // tm, n // PROJ_TN),
        in_specs=[
            x_spec,
            pl.BlockSpec((N_META, d), lambda i, j: (0, 0)),
            pl.BlockSpec((1, d), lambda i, j: (0, 0)),
            pl.BlockSpec((d, PROJ_TN), lambda i, j: (0, j)),
            pl.BlockSpec((d, LANES), lambda i, j: (0, 0)),
        ],
        out_specs=[pl.BlockSpec((tm, PROJ_TN), lambda i, j: (i, j)),
                   pl.BlockSpec((tm, LANES), lambda i, j: (i, 0)),
                   pl.BlockSpec((tm, d), lambda i, j: (i, 0))],
        out_shape=[jax.ShapeDtypeStruct((t, n), PROJ_DTYPE),
                   jax.ShapeDtypeStruct((t, LANES), F32),
                   jax.ShapeDtypeStruct((t, d), F32)],
        scratch_shapes=[pltpu.VMEM((tm, d), MXU_DTYPE)],
        compiler_params=_params(("arbitrary", "arbitrary")),
        name="in_proj",
    )(x2d, meta, gain, w_cat, w_ab)


def _fnet_kernel(x_ref, cs_ref, tw_ref, flip_ref, o_ref, r_ref, *, seq, kp, nat, scale):
    tail = kp - seq
    if tail:
        r_ref[seq:kp, :] = jnp.zeros((tail, F_WIDTH), MXU_DTYPE)
        r_ref[kp + seq:, :] = jnp.zeros((tail, F_WIDTH), MXU_DTYPE)
    cs = cs_ref[...]
    for g in range(N_FGROUPS):
        cols = slice(g * FGROUP_DIM, (g + 1) * FGROUP_DIM)
        p = _dot(x_ref[:, cols].astype(MXU_DTYPE), cs)
        r_ref[0:seq, cols] = p[:, :FGROUP_DIM].astype(MXU_DTYPE)
        r_ref[kp:kp + seq, cols] = p[:, FGROUP_DIM:].astype(MXU_DTYPE)

    a = _dot(tw_ref[:, :kp], r_ref[0:kp, :])
    b = _dot(tw_ref[:, kp:], r_ref[kp:, :])
    o_ref[0:nat, :] = ((a - b) * scale).astype(o_ref.dtype)
    mirrored = ((a + b) * scale).astype(MXU_DTYPE)
    o_ref[nat:, :] = _dot(flip_ref[...], mirrored).astype(o_ref.dtype)


def _fnet(proj3, cs128, twid, flip):
    b, seq, _ = proj3.shape
    nat, kp = twid.shape[0], twid.shape[1] // 2
    scale = 1.0 / math.sqrt(seq * FGROUP_DIM)
    const = lambda shape: pl.BlockSpec(shape, lambda bi: (0, 0))
    return pl.pallas_call(
        functools.partial(_fnet_kernel, seq=seq, kp=kp, nat=nat, scale=scale),
        grid=(b,),
        in_specs=[
            pl.BlockSpec((None, seq, F_WIDTH), lambda bi: (bi, 0, COL_FIN // F_WIDTH)),
            const((FGROUP_DIM, 2 * FGROUP_DIM)), const((nat, 2 * kp)), const((seq - nat, nat)),
        ],
        out_specs=pl.BlockSpec((None, seq, F_WIDTH), lambda bi: (bi, 0, 0)),
        out_shape=jax.ShapeDtypeStruct((b, seq, F_WIDTH), MXU_DTYPE),
        scratch_shapes=[pltpu.VMEM((2 * kp, F_WIDTH), MXU_DTYPE)],
        compiler_params=_params(("arbitrary",)),
        name="fnet",
    )(proj3, cs128, twid, flip)


def _aligned(x, m):
    return x if isinstance(x, int) else pl.multiple_of(x, m)


def _gdn_kernel(q_ref, k_ref, v_ref, z_ref, ab_ref, cwq_ref, cwk_ref, cwv_ref, prm_ref, ong_ref,
                o_ref,
                rawq_ref, rawk_ref, rawv_ref, qs_ref, ks_ref, vs_ref, gsel_ref, grow_ref, gmix_ref,
                qe_ref, mp_ref, nn_ref, dd_ref, osum_ref,
                *, seq, lp, group, n_heads_total):
    step = pl.program_id(0)
    head = jnp.minimum(step, n_heads_total - 1) % N_HEADS
    nc = lp // CHUNK
    padr = lp - seq
    halo = SUBLANES
    row64 = lax.broadcasted_iota(jnp.int32, (CHUNK, LANES), 0)
    lane64 = lax.broadcasted_iota(jnp.int32, (CHUNK, LANES), 1)

    def for_groups(fn, carry=0, scan_first=0, scan_total=0):
        n_full, rem = divmod(nc, group)
        m_full = scan_total * group // nc if rem else scan_total // max(n_full, 1)
        if n_full == 1:
            carry = fn(list(range(group)), carry, scan_first, m_full)
        elif n_full:
            carry = lax.fori_loop(
                0, n_full,
                lambda i, c: fn([i * group + j for j in range(group)], c, scan_first + i * m_full, m_full), carry)
        if rem:
            carry = fn([n_full * group + j for j in range(rem)], carry,
                       scan_first + n_full * m_full, scan_total - n_full * m_full)
        return carry

    cur_half = step % 2
    qe_w, mp_w, nn_w, dd_w, osum_w = (r.at[cur_half] for r in (qe_ref, mp_ref, nn_ref, dd_ref, osum_ref))
    qe_r, mp_r, nn_r, dd_r, osum_r = (r.at[1 - cur_half] for r in (qe_ref, mp_ref, nn_ref, dd_ref, osum_ref))

    streams = ((rawq_ref, q_ref, cwq_ref, qs_ref, HEAD_DIM ** -0.5),
               (rawk_ref, k_ref, cwk_ref, ks_ref, 1.0),
               (rawv_ref, v_ref, cwv_ref, vs_ref, None))
    def stage_conv_inputs():
        for raw_ref, src_ref, _, _, _ in streams:
            raw_ref[0:padr + halo, :] = jnp.zeros((padr + halo, LANES), F32)
            raw_ref[padr + halo:padr + halo + seq, :] = src_ref[...].astype(F32)
            raw_ref[lp + halo:lp + 2 * halo, :] = jnp.zeros((halo, LANES), F32)

    def conv_chunk(c):
        base = _aligned(c * CHUNK, CHUNK)
        for raw_ref, _, cw_ref, dst_ref, l2_scale in streams:
            cw = cw_ref[...]
            acc = None
            for j in range(CONV_K):
                off = halo - (CONV_K - 1) // 2 + j
                tap = cw[j:j + 1, :] * raw_ref[pl.ds(base + off, CHUNK), :]
                acc = tap if acc is None else acc + tap
            y = _silu(acc)
            if isinstance(c, int) and c * CHUNK < padr:
                y = jnp.where(row64 + base >= padr, y, 0.0)
            if l2_scale is not None:
                inv = lax.rsqrt(jnp.sum(y * y, axis=-1, keepdims=True) + NORM_EPS)
                y = y * (inv if l2_scale == 1.0 else inv * l2_scale)
            dst_ref[pl.ds(base, CHUNK), :] = y

    raw_ref = rawq_ref

    def stage_gate_inputs():
        raw_ref[0:padr, :] = jnp.zeros((padr, LANES), F32)
        raw_ref[padr:padr + seq, :] = ab_ref[...]

    neg_a = -jnp.exp(prm_ref[0:1, :])
    dt_bias = prm_ref[1:2, :]
    is_g = (lane64 < 8) | ((lane64 >= 16) & (lane64 < 24))
    is_beta = ((lane64 >= 8) & (lane64 < 16)) | ((lane64 >= 24) & (lane64 < 32))
    r128 = lax.broadcasted_iota(jnp.int32, (2 * CHUNK, CHUNK), 0)
    c128 = lax.broadcasted_iota(jnp.int32, (2 * CHUNK, CHUNK), 1)
    tri2 = jnp.where(((r128 < CHUNK) & (r128 >= c128)) | ((r128 >= CHUNK) & (r128 - CHUNK <= c128)),
                     1.0, 0.0).astype(MXU_DTYPE)
    sel_r = lax.broadcasted_iota(jnp.int32, (LANES, 4 * LANES), 0)
    sel_c = lax.broadcasted_iota(jnp.int32, (LANES, 4 * LANES), 1)
    n_gate_cols = 4 * N_HEADS
    sel_packed = jnp.where((sel_r < 3 * n_gate_cols)
                           & (sel_r % n_gate_cols == head + N_HEADS * (sel_c // LANES)), 1.0, 0.0).astype(MXU_DTYPE)
    tri2x3 = jnp.concatenate([tri2, tri2, tri2], axis=1)

    def gate_batch_group(cs, carry, *_):
        bases = [_aligned(c * CHUNK, CHUNK) for c in cs]
        gates, parts = [], []
        for base in bases:
            ab = raw_ref[pl.ds(base, CHUNK), :]
            g = neg_a * jax.nn.softplus(ab + dt_bias)
            beta = jax.nn.sigmoid(ab)
            gt = jnp.where(is_g, g, jnp.where(is_beta, beta, 0.0))
            gt = jnp.where(row64 + base >= padr, gt, 0.0)
            gates.append(gt)
            parts.append(jnp.concatenate(_split3(gt), axis=0))
        cums = [_dot(tri2x3, p) for p in parts]
        for c, base, cm, gt in zip(cs, bases, cums, gates):
            m = jnp.where(lane64 < 8, cm[:CHUNK], jnp.where((lane64 >= 16) & (lane64 < 24), cm[CHUNK:], gt))
            gmix_ref[pl.ds(base, CHUNK), :] = m
            grow_ref[pl.ds(_aligned(c * LANES, LANES), LANES), :] = jnp.concatenate([m, m], axis=0).T
        return carry

    def gate_head_group(cs, carry, *_):
        rows = len(cs) * CHUNK
        base = _aligned(cs[0] * CHUNK, CHUNK)
        hi, mid, lo = _split3(gmix_ref[pl.ds(base, rows), :])
        packed = (hi.astype(F32) + pltpu.roll(mid.astype(F32), n_gate_cols, 1)
                  + pltpu.roll(lo.astype(F32), 2 * n_gate_cols, 1)).astype(MXU_DTYPE)
        gsel_ref[pl.ds(base, rows), :] = _dot(packed, sel_packed)
        return carry

    fwd_half = lane64 < CHUNK
    col64 = lane64 & (CHUNK - 1)
    incl2 = (fwd_half & (row64 >= col64)) | (~fwd_half & (row64 <= col64))
    strict2 = (fwd_half & (row64 > col64)) | (~fwd_half & (row64 < col64))
    eye2 = jnp.where(row64 == col64, 1.0, 0.0).astype(F32)

    def blockdiag(p):
        return jnp.concatenate([jnp.where(fwd_half, p, 0.0), jnp.where(fwd_half, 0.0, p)],
                               axis=0).astype(MXU_DTYPE)

    n_neumann = int(math.log2(CHUNK)) - 2
    n_stages = n_neumann + 6

    def chunk_group(cs, carry, scan_first, scan_steps):
        n = len(cs)
        bases = [_aligned(c * CHUNK, CHUNK) for c in cs]
        cb128 = [_aligned(c * LANES, LANES) for c in cs]
        cb8 = [_aligned(c * SUBLANES, SUBLANES) for c in cs]
        stage = [0]

        def end_of_stage(carry):
            s = stage[0]
            stage[0] += 1
            for j in range(scan_steps):
                if j * n_stages // scan_steps == s:
                    carry = scan_step(scan_first + j, carry)
            return carry

        def gate_cols(j):
            gs = gsel_ref[pl.ds(bases[j], CHUNK), :]
            return gs[:, 0:LANES], gs[:, LANES:2 * LANES], gs[:, 2 * LANES:3 * LANES], gs[:, 3 * LANES:]

        kq = []
        for j in range(n):
            k16 = ks_ref[pl.ds(bases[j], CHUNK), :].astype(MXU_DTYPE)
            q16 = qs_ref[pl.ds(bases[j], CHUNK), :].astype(MXU_DTYPE)
            kq.append(_dot_nt(jnp.concatenate([k16, q16], axis=0), jnp.concatenate([k16, k16], axis=0)))
        carry = end_of_stage(carry)

        ts, ps, qkds = [], [], []
        for j in range(n):
            gf, bf, gb, bb = gate_cols(j)
            row_f = grow_ref[pl.ds(cb128[j] + head, 1), :]
            row_b = grow_ref[pl.ds(cb128[j] + 2 * N_HEADS + head, 1), :]
            g_col = jnp.where(fwd_half, gf, gb)
            g_row = jnp.where(fwd_half, row_f, row_b)
            decay = jnp.where(incl2, jnp.exp(jnp.where(incl2, g_col - g_row, 0.0)), 0.0)
            a2 = jnp.where(strict2, kq[j][:CHUNK] * jnp.where(fwd_half, bf, bb) * decay, 0.0)
            qkds.append(jnp.where(incl2, kq[j][CHUNK:] * decay, 0.0))
            ts.append(eye2 - a2)
            ps.append(a2)

        ps = [_dot(p.astype(MXU_DTYPE), blockdiag(p)) for p in ps]
        carry = end_of_stage(carry)
        for _ in range(n_neumann):
            tps = [_dot(jnp.concatenate([t, p], axis=0).astype(MXU_DTYPE), blockdiag(p)) for t, p in zip(ts, ps)]
            carry = end_of_stage(carry)
            ts = [t + tp[:CHUNK] for t, tp in zip(ts, tps)]
            ps = [tp[CHUNK:] for tp in tps]
        ts = [t + _dot(t.astype(MXU_DTYPE), blockdiag(p)) for t, p in zip(ts, ps)]
        carry = end_of_stage(carry)

        uws = []
        for j in range(n):
            gf, bf, gb, bb = gate_cols(j)
            k = ks_ref[pl.ds(bases[j], CHUNK), :]
            v = vs_ref[pl.ds(bases[j], CHUNK), :]
            x_f = jnp.concatenate([v * bf, k * bf * jnp.exp(gf)], axis=1)
            x_b = jnp.concatenate([v * bb, k * bb * jnp.exp(gb)], axis=1)
            uw = _dot(blockdiag(ts[j]), jnp.concatenate([x_f, x_b], axis=0).astype(MXU_DTYPE))
            uws.append(uw.astype(MXU_DTYPE))
        carry = end_of_stage(carry)

        xxs = [_dot(blockdiag(qkds[j]), uws[j]) for j in range(n)]
        carry = end_of_stage(carry)
        for j in range(n):
            gf, _, gb, _ = gate_cols(j)
            k = ks_ref[pl.ds(bases[j], CHUNK), :]
            q = qs_ref[pl.ds(bases[j], CHUNK), :]
            xx = xxs[j]
            osum_w[pl.ds(bases[j], CHUNK), :] = xx[:CHUNK, :LANES] + xx[CHUNK:, :LANES]
            for d, gc in enumerate((gf, gb)):
                rows = slice(d * CHUNK, (d + 1) * CHUNK)
                g_last = gc[CHUNK - 1:CHUNK, :] if d == 0 else gc[0:1, :]
                k_dec = k * jnp.exp(g_last - gc)
                mn = _dot_tn(k_dec.astype(MXU_DTYPE), uws[j][rows])
                qe_w[d, pl.ds(bases[j], CHUNK), :] = (q * jnp.exp(gc) - xx[rows, LANES:]).astype(MXU_DTYPE)
                nn_w[d, pl.ds(cb128[j], HEAD_DIM), :] = mn[:, :LANES]
                mp_w[d, pl.ds(cb128[j], HEAD_DIM), :] = mn[:, LANES:].astype(MXU_DTYPE)
                dd_w[d, pl.ds(cb8[j], SUBLANES), :] = jnp.broadcast_to(jnp.exp(g_last), (SUBLANES, LANES))
        carry = end_of_stage(carry)
        assert stage[0] == n_stages
        return carry

    def scan_step(i, carry):
        new = []
        for d in range(2):
            s = carry[d]
            c = i if d == 0 else nc - 1 - i
            base = _aligned(c * CHUNK, CHUNK)
            cbase = _aligned(c * HEAD_DIM, HEAD_DIM)
            dbase = _aligned(c * SUBLANES, SUBLANES)
            s16 = s.astype(MXU_DTYPE)
            osum_r[pl.ds(base, CHUNK), :] += _dot(qe_r[d, pl.ds(base, CHUNK), :], s16)
            dec = dd_r[d, pl.ds(dbase, 1), :]
            new.append(dec * s - _dot(mp_r[d, pl.ds(cbase, HEAD_DIM), :], s16)
                       + nn_r[d, pl.ds(cbase, HEAD_DIM), :])
        return tuple(new)

    zero_state = jnp.zeros((HEAD_DIM, HEAD_DIM), F32)

    def write_output():
        gain = ong_ref[...]

        def finish(o, z):
            ms = jnp.mean(o * o, axis=-1, keepdims=True)
            return (o * lax.rsqrt(ms + NORM_EPS) * gain * _silu(z.astype(F32))).astype(o_ref.dtype)

        first = CHUNK - padr if padr else 0
        if first:
            o_ref[0:first, :] = finish(osum_r[padr:CHUNK, :], z_ref[0:first, :])

        def out_body(c, carry):
            base = pl.multiple_of(c * CHUNK, CHUNK)
            dst = pl.multiple_of(c * CHUNK - padr, SUBLANES)
            o_ref[pl.ds(dst, CHUNK), :] = finish(osum_r[pl.ds(base, CHUNK), :], z_ref[pl.ds(dst, CHUNK), :])
            return carry

        lax.fori_loop(1 if first else 0, nc, out_body, 0, unroll=8)

    def conv_only(c, carry):
        conv_chunk(c)
        return carry

    conv_per_scan = 3
    n_conv_scan = (nc - 1) // conv_per_scan

    def conv_and_scan(i, carry):
        for u in range(conv_per_scan):
            conv_chunk(1 + conv_per_scan * i + u)
        return scan_step(i, carry)

    def gates():
        @pl.when(head == 0)
        def _():
            stage_gate_inputs()
            for_groups(gate_batch_group)

        for_groups(gate_head_group)

    @pl.when(step == 0)
    def _():
        stage_conv_inputs()
        conv_chunk(0)
        lax.fori_loop(1, nc, conv_only, 0, unroll=4)
        gates()
        for_groups(chunk_group)

    @pl.when((step > 0) & (step < n_heads_total))
    def _():
        stage_conv_inputs()
        conv_chunk(0)
        carry = lax.fori_loop(0, n_conv_scan, conv_and_scan, (zero_state, zero_state))
        lax.fori_loop(1 + conv_per_scan * n_conv_scan, nc, conv_only, 0)
        gates()
        for_groups(chunk_group, carry, n_conv_scan, nc - n_conv_scan)
        write_output()

    @pl.when(step == n_heads_total)
    def _():
        lax.fori_loop(0, nc, scan_step, (zero_state, zero_state))
        write_output()


def _gdn(proj3, ab3, conv_w, gate_prm, out_norm_g):
    b, seq, _ = proj3.shape
    lp = -(-seq // CHUNK) * CHUNK
    nc = lp // CHUNK
    n_total = b * N_HEADS
    cur = lambda i: jnp.minimum(i, n_total - 1)
    prev = lambda i: jnp.maximum(i - 1, 0)
    col = lambda off: pl.BlockSpec((None, seq, LANES), lambda i: (cur(i) // N_HEADS, 0, off + cur(i) % N_HEADS))
    cw = lambda off: pl.BlockSpec((CONV_K, LANES), lambda i: (0, off + cur(i) % N_HEADS))
    return pl.pallas_call(
        functools.partial(_gdn_kernel, seq=seq, lp=lp, group=GDN_GROUP, n_heads_total=n_total),
        grid=(n_total + 1,),
        in_specs=[
            col(COL_QKV // LANES), col(COL_QKV // LANES + N_HEADS), col(COL_QKV // LANES + 2 * N_HEADS),
            pl.BlockSpec((None, seq, LANES),
                         lambda i: (prev(i) // N_HEADS, 0, COL_Z // LANES + prev(i) % N_HEADS)),
            pl.BlockSpec((None, seq, LANES), lambda i: (cur(i) // N_HEADS, 0, 0)),
            cw(0), cw(N_HEADS), cw(2 * N_HEADS),
            pl.BlockSpec((SUBLANES, LANES), lambda i: (0, 0)),
            pl.BlockSpec((1, LANES), lambda i: (0, 0)),
        ],
        out_specs=pl.BlockSpec((None, seq, LANES), lambda i: (prev(i) // N_HEADS, 0, prev(i) % N_HEADS)),
        out_shape=jax.ShapeDtypeStruct((b, seq, N_HEADS * HEAD_DIM), MXU_DTYPE),
        scratch_shapes=[
            pltpu.VMEM((lp + 2 * SUBLANES, LANES), F32),
            pltpu.VMEM((lp + 2 * SUBLANES, LANES), F32),
            pltpu.VMEM((lp + 2 * SUBLANES, LANES), F32),
            pltpu.VMEM((lp, LANES), F32),
            pltpu.VMEM((lp, LANES), F32),
            pltpu.VMEM((lp, LANES), F32),
            pltpu.VMEM((lp, 4 * LANES), F32),
            pltpu.VMEM((nc * LANES, LANES), F32),
            pltpu.VMEM((lp, LANES), F32),
            pltpu.VMEM((2, 2, lp, LANES), MXU_DTYPE),
            pltpu.VMEM((2, 2, nc * HEAD_DIM, LANES), MXU_DTYPE),
            pltpu.VMEM((2, 2, nc * HEAD_DIM, LANES), F32),
            pltpu.VMEM((2, 2, nc * SUBLANES, LANES), F32),
            pltpu.VMEM((2, lp, LANES), F32),
        ],
        compiler_params=_params(("arbitrary",)),
        name="gdn",
    )(proj3, proj3, proj3, proj3, ab3, conv_w, conv_w, conv_w, gate_prm, out_norm_g)


def _merge_kernel(h_ref, og_ref, ym_ref, gf_ref, gd_ref, wf_ref, wd_ref, wo_ref, n2_ref, wr_ref, br_ref,
                  h1_ref, hn_ref, ridx_ref, rw_ref, cnt_ref):
    @pl.when(pl.program_id(0) == 0)
    def _():
        cnt_ref[...] = jnp.zeros_like(cnt_ref)

    y_f = _dot(ym_ref[...], wf_ref[...])
    y_d = _dot(og_ref[...], wd_ref[...])
    merged = (jax.nn.sigmoid(gf_ref[...].astype(F32)) * y_f
              + jax.nn.sigmoid(gd_ref[...].astype(F32)) * y_d)
    h1 = h_ref[...] + _dot(merged.astype(MXU_DTYPE), wo_ref[...])
    h1_ref[...] = h1
    ms = jnp.mean(h1 * h1, axis=-1, keepdims=True)
    hn = h1 * lax.rsqrt(ms + NORM_EPS) * n2_ref[...]
    hn_ref[...] = _pack_rows(hn)

    x_hi = hn.astype(MXU_DTYPE)
    x_lo = (hn - x_hi.astype(F32)).astype(MXU_DTYPE)
    w = wr_ref[...]
    w_hi = w.astype(MXU_DTYPE)
    w_lo = (w - w_hi.astype(F32)).astype(MXU_DTYPE)
    hi_terms = _dot(x_hi, jnp.concatenate([w_hi, w_lo], axis=1))
    logits = hi_terms[:, :LANES] + hi_terms[:, LANES:] + _dot(x_lo, w_hi) + br_ref[...]

    tm = logits.shape[0]
    lane = lax.broadcasted_iota(jnp.int32, (tm, LANES), 1)
    neg_inf = -jnp.inf
    is_group = lane < N_GROUPS
    gl = jnp.where(is_group, logits, neg_inf)
    ge = jnp.exp(gl - jnp.max(gl, axis=-1, keepdims=True))
    gp = ge / jnp.sum(ge, axis=-1, keepdims=True)
    p_group = jnp.max(gp, axis=-1, keepdims=True)
    g_idx = jnp.min(jnp.where(is_group & (gp == p_group), lane, LANES), axis=-1, keepdims=True)

    in_group = (lane >= N_GROUPS) & (lane < N_GROUPS + N_EXPERTS) & (((lane - N_GROUPS) >> 3) == g_idx)
    el = jnp.where(in_group, logits, neg_inf)
    ee = jnp.exp(el - jnp.max(el, axis=-1, keepdims=True))
    ep = ee / jnp.sum(ee, axis=-1, keepdims=True)
    v1 = jnp.max(jnp.where(in_group, ep, -1.0), axis=-1, keepdims=True)
    i1 = jnp.min(jnp.where(in_group & (ep == v1), lane, LANES), axis=-1, keepdims=True)
    rest = in_group & (lane != i1)
    v2 = jnp.max(jnp.where(rest, ep, -1.0), axis=-1, keepdims=True)
    i2 = jnp.min(jnp.where(rest & (ep == v2), lane, LANES), axis=-1, keepdims=True)
    denom = v1 + v2
    w1 = p_group * (v1 / denom)
    w2 = p_group * (v2 / denom)
    e1 = i1 - N_GROUPS
    e2 = i2 - N_GROUPS
    ridx_ref[...] = jnp.where(lane == 0, e1, jnp.where(lane == 1, e2, 0))
    rw_ref[...] = jnp.where(lane == 0, w1, jnp.where(lane == 1, w2, 0.0))
    onehots = jnp.where((lane == e1) | (lane == e2), 1.0, 0.0)
    cnt_ref[...] += jnp.broadcast_to(jnp.sum(onehots, axis=0, keepdims=True), cnt_ref.shape)


def _merge(h2d, og2d, ymix2d, proj2d, w_fourier, w_delta, w_out, norm2_g, w_router, b_router, tm):
    t, d = h2d.shape
    row = lambda w: pl.BlockSpec((tm, w), lambda i: (i, 0))
    const = lambda shape: pl.BlockSpec(shape, lambda i: (0, 0))
    return pl.pallas_call(
        _merge_kernel,
        grid=(t // tm,),
        in_specs=[
            row(d), row(d), row(F_WIDTH),
            pl.BlockSpec((tm, d), lambda i: (i, COL_GATE_F // d)),
            pl.BlockSpec((tm, d), lambda i: (i, COL_GATE_D // d)),
            const((F_WIDTH, d)), const((d, d)), const((d, d)), const((1, d)),
            const((d, LANES)), const((1, LANES)),
        ],
        out_specs=[row(d), row(d // 2), row(LANES), row(LANES), const((SUBLANES, LANES))],
        out_shape=[
            jax.ShapeDtypeStruct((t, d), F32),
            jax.ShapeDtypeStruct((t, d // 2), jnp.uint32),
            jax.ShapeDtypeStruct((t, LANES), jnp.int32),
            jax.ShapeDtypeStruct((t, LANES), F32),
            jax.ShapeDtypeStruct((SUBLANES, LANES), F32),
        ],
        compiler_params=_params(("arbitrary",)),
        name="merge",
    )(h2d, og2d, ymix2d, proj2d, proj2d, w_fourier, w_delta, w_out, norm2_g, w_router, b_router)


def _plan_kernel(ridx_ref, cnt_ref, dest_ref, seg_ref, base_ref, run_ref):
    tt = ridx_ref.shape[0]

    @pl.when(pl.program_id(0) == 0)
    def _():
        cnt = cnt_ref[...].astype(jnp.int32)
        n_blk = (cnt + (EXPERT_BLOCK - 1)) // EXPERT_BLOCK
        padded = (n_blk * EXPERT_BLOCK).astype(F32)
        r = lax.broadcasted_iota(jnp.int32, (LANES, LANES), 0)
        c = lax.broadcasted_iota(jnp.int32, (LANES, LANES), 1)
        before = jnp.where(r < c, 1.0, 0.0).astype(MXU_DTYPE)
        start = _exact_right(padded, before)
        base_ref[...] = start
        run_ref[...] = jnp.zeros_like(run_ref)
        seg_ref[0:SUBLANES, :] = start.astype(jnp.int32)
        seg_ref[SUBLANES:2 * SUBLANES, :] = n_blk
        seg_ref[2 * SUBLANES:, :] = cnt

    lane = lax.broadcasted_iota(jnp.int32, (tt, LANES), 1)
    ridx = ridx_ref[...]
    oh0 = jnp.where(lane == ridx[:, 0:1], 1.0, 0.0)
    oh1 = jnp.where(lane == ridx[:, 1:2], 1.0, 0.0)
    r = lax.broadcasted_iota(jnp.int32, (tt, tt), 0)
    c = lax.broadcasted_iota(jnp.int32, (tt, tt), 1)
    earlier = jnp.where(c < r, 1.0, 0.0).astype(MXU_DTYPE)
    prefix = _dot(earlier, jnp.concatenate([oh0, oh1], axis=1).astype(MXU_DTYPE))
    c0 = jnp.sum(oh0, axis=0, keepdims=True)
    c1 = jnp.sum(oh1, axis=0, keepdims=True)
    first = base_ref[0:1, :] + run_ref[0:1, :]
    d0 = jnp.sum(oh0 * (prefix[:, :LANES] + first), axis=-1, keepdims=True)
    d1 = jnp.sum(oh1 * (prefix[:, LANES:] + first + c0), axis=-1, keepdims=True)
    run_ref[...] += jnp.broadcast_to(c0 + c1, run_ref.shape)
    both = jnp.where(lane == 0, d0, jnp.where(lane == 1, d1, 0.0))
    both_t = both.T.astype(jnp.int32)
    for k in range(TOP_K):
        dest_ref[k] = both_t[k:k + 1, :]


def _plan(ridx, cnt, tt):
    t = ridx.shape[0]
    return pl.pallas_call(
        _plan_kernel,
        grid=(t // tt,),
        in_specs=[pl.BlockSpec((tt, LANES), lambda i: (i, 0)),
                  pl.BlockSpec((SUBLANES, LANES), lambda i: (0, 0))],
        out_specs=[pl.BlockSpec((TOP_K, None, 1, tt), lambda i: (0, i, 0, 0)),
                   pl.BlockSpec((3 * SUBLANES, LANES), lambda i: (0, 0))],
        out_shape=[jax.ShapeDtypeStruct((TOP_K, t // tt, 1, tt), jnp.int32),
                   jax.ShapeDtypeStruct((3 * SUBLANES, LANES), jnp.int32)],
        scratch_shapes=[pltpu.VMEM((SUBLANES, LANES), F32), pltpu.VMEM((SUBLANES, LANES), F32)],
        compiler_params=_params(("arbitrary",)),
        name="plan",
    )(ridx, cnt)


def _scatter_kernel(dest0_ref, dest1_ref, hn_ref, xb_in_ref, xb_ref, sem, *, ts):
    dest_refs = (dest0_ref, dest1_ref)
    del xb_in_ref

    def issue(t, carry):
        src = hn_ref.at[pl.ds(t, 1)]
        for k in range(TOP_K):
            pltpu.make_async_copy(src, xb_ref.at[pl.ds(dest_refs[k][0, t], 1)], sem).start(priority=k % 2)
        return carry

    lax.fori_loop(0, ts, issue, 0, unroll=DMA_UNROLL)
    for _ in range(TOP_K):
        pltpu.make_async_copy(hn_ref, xb_ref.at[pl.ds(0, ts)], sem).wait()


def _scatter(dest_flat, hn2d, xb_zero, ts):
    t, d = hn2d.shape
    return pl.pallas_call(
        functools.partial(_scatter_kernel, ts=ts),
        grid=(t // ts,),
        in_specs=[pl.BlockSpec((None, None, 1, ts), lambda i: (0, i, 0, 0), memory_space=pltpu.SMEM),
                  pl.BlockSpec((None, None, 1, ts), lambda i: (1, i, 0, 0), memory_space=pltpu.SMEM),
                  pl.BlockSpec((ts, d), lambda i: (i, 0)),
                  pl.BlockSpec(memory_space=pl.ANY)],
        out_specs=pl.BlockSpec(memory_space=pl.ANY),
        out_shape=jax.ShapeDtypeStruct(xb_zero.shape, xb_zero.dtype),
        scratch_shapes=[pltpu.SemaphoreType.DMA(())],
        input_output_aliases={3: 0},
        compiler_params=_params(("arbitrary",)),
        name="scatter",
    )(dest_flat, dest_flat, hn2d, xb_zero)


def _expert_kernel(start_ref, nblk_ref, xb_ref, wg_ref, wu_ref, wd_ref, yb_ref,
                   wg16_ref, wu16_ref, wd16_ref, xbuf_ref, ybuf_ref, sem_in, sem_out):
    expert = pl.program_id(0)
    last = pl.num_programs(0) - 1
    n = nblk_ref[expert]
    first = start_ref[expert] // EXPERT_BLOCK
    total = start_ref[last] // EXPERT_BLOCK + nblk_ref[last]

    def rows(g):
        return pl.ds(pl.multiple_of(g * EXPERT_BLOCK, EXPERT_BLOCK), EXPERT_BLOCK)

    def x_copy(g):
        slot = g % EXPERT_X_SLOTS
        return pltpu.make_async_copy(xb_ref.at[rows(g)], xbuf_ref.at[slot], sem_in.at[slot])

    def y_copy(g):
        slot = g % EXPERT_Y_SLOTS
        return pltpu.make_async_copy(ybuf_ref.at[slot], yb_ref.at[rows(g)], sem_out.at[slot])

    ahead = EXPERT_X_SLOTS - 2

    def process(g, k):
        for j in range(k):
            @pl.when(g + j + ahead < total)
            def _():
                x_copy(g + j + ahead).start()

        for j in range(k):
            x_copy(g + j).wait()

        for j in range(k):
            @pl.when(g + j >= EXPERT_Y_SLOTS)
            def _():
                y_copy(g + j - EXPERT_Y_SLOTS).wait()

        x = jnp.concatenate([_unpack_rows(xbuf_ref[(g + j) % EXPERT_X_SLOTS]) for j in range(k)],
                            axis=0).astype(MXU_DTYPE)
        gate = _dot(x, wg16_ref[...])
        up = _dot(x, wu16_ref[...])
        y = _pack_rows(_dot((_silu(gate) * up).astype(MXU_DTYPE), wd16_ref[...]))
        for j in range(k):
            ybuf_ref[(g + j) % EXPERT_Y_SLOTS] = y[j * EXPERT_BLOCK:(j + 1) * EXPERT_BLOCK]
            y_copy(g + j).start()

    @pl.when(n > 0)
    def _():
        @pl.when(first == 0)
        def _():
            for j in range(ahead):
                @pl.when(j < total)
                def _():
                    x_copy(j).start()

        wg16_ref[...] = wg_ref[...].astype(MXU_DTYPE)
        wu16_ref[...] = wu_ref[...].astype(MXU_DTYPE)
        wd16_ref[...] = wd_ref[...].astype(MXU_DTYPE)

        def pair(i, carry):
            process(first + 2 * i, 2)
            return carry

        lax.fori_loop(0, n // 2, pair, 0)

        @pl.when(n % 2 == 1)
        def _():
            process(first + n - 1, 1)

    @pl.when(expert == last)
    def _():
        for j in range(EXPERT_Y_SLOTS):
            @pl.when(total > j)
            def _():
                y_copy(total - 1 - j).wait()


def _experts(seg_start, seg_blocks, xb, w_gate, w_up, w_down):
    n_slots, row_words = xb.shape
    n_exp, d, de = w_gate.shape
    grid_spec = pltpu.PrefetchScalarGridSpec(
        num_scalar_prefetch=2,
        grid=(n_exp,),
        in_specs=[
            pl.BlockSpec(memory_space=pl.ANY),
            pl.BlockSpec((None, d, de), lambda e, st, nb: (e, 0, 0)),
            pl.BlockSpec((None, d, de), lambda e, st, nb: (e, 0, 0)),
            pl.BlockSpec((None, de, d), lambda e, st, nb: (e, 0, 0)),
        ],
        out_specs=pl.BlockSpec(memory_space=pl.ANY),
        scratch_shapes=[
            pltpu.VMEM((d, de), MXU_DTYPE), pltpu.VMEM((d, de), MXU_DTYPE), pltpu.VMEM((de, d), MXU_DTYPE),
            pltpu.VMEM((EXPERT_X_SLOTS, EXPERT_BLOCK, row_words), jnp.uint32),
            pltpu.VMEM((EXPERT_Y_SLOTS, EXPERT_BLOCK, row_words), jnp.uint32),
            pltpu.SemaphoreType.DMA((EXPERT_X_SLOTS,)), pltpu.SemaphoreType.DMA((EXPERT_Y_SLOTS,)),
        ],
    )
    return pl.pallas_call(
        _expert_kernel,
        grid_spec=grid_spec,
        out_shape=jax.ShapeDtypeStruct((n_slots, row_words), jnp.uint32),
        input_output_aliases={2: 0},
        compiler_params=_params(("arbitrary",)),
        name="experts",
    )(seg_start, seg_blocks, xb, w_gate, w_up, w_down)


def _combine_kernel(dest0_ref, dest1_ref, yb_ref, h1_ref, rw_ref, fg_ref, o_ref, buf_ref, res_ref, sem, out_sem,
                    *, tc, groups_per_batch):
    step = pl.program_id(0)
    n_tiles = pl.num_programs(0) - 1
    n_grp = tc // N_META
    cur = step % 2
    prev = 1 - cur

    def output_copies(tile, action):
        slot = tile % 2
        first = tile * n_grp
        first_in_batch = first % groups_per_batch
        has_meta = (first_in_batch == 0) | (first_in_batch + n_grp > groups_per_batch)

        def copy_groups(q, n):
            grp = first + q
            dst = pl.multiple_of((grp - grp // groups_per_batch - 1) * N_META, N_META)
            copy = pltpu.make_async_copy(res_ref.at[slot, pl.ds(q * N_META, n * N_META)],
                                         o_ref.at[pl.ds(dst, n * N_META)], out_sem.at[slot])
            copy.start() if action == "start" else copy.wait()

        @pl.when(jnp.logical_not(has_meta))
        def _():
            copy_groups(0, n_grp)

        @pl.when(has_meta)
        def _():
            for q in range(n_grp):
                @pl.when((first + q) % groups_per_batch != 0)
                def _():
                    copy_groups(q, 1)

    def request_rows(i):
        for u in range(SUBLANES):
            t = i * SUBLANES + u
            for k in range(TOP_K):
                pltpu.make_async_copy(yb_ref.at[pl.ds((dest0_ref, dest1_ref)[k][0, t], 1)],
                                      buf_ref.at[cur, k, pl.ds(t, 1)], sem.at[cur]).start(priority=k % 2)

    def finish_rows(i):
        rows = pl.ds(pl.multiple_of(i * SUBLANES, SUBLANES), SUBLANES)
        rw = rw_ref[rows, :]
        h2 = h1_ref[rows, :] + (_unpack_rows(buf_ref[prev, 0, rows, :]) * rw[:, 0:1]
                                + _unpack_rows(buf_ref[prev, 1, rows, :]) * rw[:, 1:2])
        ms = jnp.mean(h2 * h2, axis=-1, keepdims=True)
        res_ref[prev, rows, :] = h2 * lax.rsqrt(ms + NORM_EPS) * fg_ref[...]

    def begin_finish():
        for k in range(TOP_K):
            pltpu.make_async_copy(yb_ref.at[pl.ds(0, tc)], buf_ref.at[prev, k], sem.at[prev]).wait()

        @pl.when(step >= 3)
        def _():
            output_copies(step - 3, "wait")

    def both(i, carry):
        request_rows(i)
        finish_rows(i)
        return carry

    def request_only(i, carry):
        request_rows(i)
        return carry

    def finish_only(i, carry):
        finish_rows(i)
        return carry

    @pl.when(step == 0)
    def _():
        lax.fori_loop(0, tc // SUBLANES, request_only, 0)

    @pl.when((step > 0) & (step < n_tiles))
    def _():
        begin_finish()
        lax.fori_loop(0, tc // SUBLANES, both, 0, unroll=16)
        output_copies(step - 1, "start")

    @pl.when(step == n_tiles)
    def _():
        begin_finish()
        lax.fori_loop(0, tc // SUBLANES, finish_only, 0, unroll=4)
        output_copies(step - 1, "start")

        @pl.when(step >= 2)
        def _():
            output_copies(step - 2, "wait")

        output_copies(step - 1, "wait")


def _combine(dest_flat, yb, h1, rw, final_g, tc, seq):
    t, d = h1.shape
    assert tc % N_META == 0 and seq % N_META == 0
    n_out = t - (t // seq) * N_META
    n_tiles = t // tc
    nxt = lambda i: jnp.minimum(i, n_tiles - 1)
    fin = lambda i: jnp.maximum(i - 1, 0)
    return pl.pallas_call(
        functools.partial(_combine_kernel, tc=tc, groups_per_batch=seq // N_META),
        grid=(n_tiles + 1,),
        in_specs=[pl.BlockSpec((None, None, 1, tc), lambda i: (0, nxt(i), 0, 0), memory_space=pltpu.SMEM),
                  pl.BlockSpec((None, None, 1, tc), lambda i: (1, nxt(i), 0, 0), memory_space=pltpu.SMEM),
                  pl.BlockSpec(memory_space=pl.ANY),
                  pl.BlockSpec((tc, d), lambda i: (fin(i), 0)),
                  pl.BlockSpec((tc, LANES), lambda i: (fin(i), 0)),
                  pl.BlockSpec((1, d), lambda i: (0, 0))],
        out_specs=pl.BlockSpec(memory_space=pl.ANY),
        out_shape=jax.ShapeDtypeStruct((n_out, d), F32),
        scratch_shapes=[pltpu.VMEM((2, TOP_K, tc, d // 2), jnp.uint32), pltpu.VMEM((2, tc, d), F32),
                        pltpu.SemaphoreType.DMA((2,)), pltpu.SemaphoreType.DMA((2,))],
        compiler_params=_params(("arbitrary",)),
        name="combine",
    )(dest_flat, dest_flat, yb, h1, rw, final_g)


def _tile(total, candidates):
    for c in candidates:
        if total % c == 0:
            return c
    raise ValueError(f"no tile for {total}")


def _dft_tables(seq):
    kp = -(-seq // LANES) * LANES
    nat = -(-(seq // 2 + 1) // 16) * 16
    mir = seq - nat
    c = np.arange(FGROUP_DIM, dtype=np.int64)
    ang_c = (2.0 * np.pi / FGROUP_DIM) * ((c[:, None] * c[None, :]) % FGROUP_DIM)
    cs128 = np.concatenate([np.cos(ang_c), np.sin(ang_c)], axis=1)
    k = np.arange(nat, dtype=np.int64)
    n = np.arange(seq, dtype=np.int64)
    ang_l = (2.0 * np.pi / seq) * ((k[:, None] * n[None, :]) % seq)
    twid = np.zeros((nat, 2 * kp), np.float32)
    twid[:, :seq] = np.cos(ang_l)
    twid[:, kp:kp + seq] = np.sin(ang_l)
    flip = np.zeros((mir, nat), np.float32)
    flip[np.arange(mir), mir - np.arange(mir)] = 1.0
    as_const = lambda t: jnp.asarray(t.astype(MXU_DTYPE))
    return as_const(cs128), as_const(twid), as_const(flip)


def kernel(x, meta_tokens, norm1_g, w_in, conv_w, a_log_fwd, dt_bias_fwd, a_log_bwd, dt_bias_bwd, out_norm_g, w_fourier, w_delta, w_out, norm2_g, w_router_group, b_router_group, w_router_expert, b_router_expert, w_gate_e, w_up_e, w_down_e, final_norm_g):
    bsz, n_real, d = x.shape
    seq = N_META + n_real
    n_tok = bsz * seq
    assert (CHUNK - N_META) % CHUNK + seq == -(-seq // CHUNK) * CHUNK, "meta tokens must fill the tail of chunk 0"
    assert w_in.shape[0] == 1, "single trunk layer"
    layer = 0


    w = w_in[layer]
    o_f, o_qkv, o_z, o_ab, o_gf, o_gd = 0, F_WIDTH, F_WIDTH + 3 * QK_WIDTH, F_WIDTH + 4 * QK_WIDTH, \
        F_WIDTH + 4 * QK_WIDTH + 4 * N_HEADS, F_WIDTH + 4 * QK_WIDTH + 4 * N_HEADS + d
    w_cat = jnp.concatenate(
        [w[:, o_qkv:o_z], w[:, o_z:o_ab], w[:, o_gf:o_gd], w[:, o_gd:o_gd + d], w[:, o_f:o_qkv]],
        axis=1).astype(MXU_DTYPE)
    w_ab = jnp.pad(w[:, o_ab:o_gf], ((0, 0), (0, LANES - 4 * N_HEADS))).astype(MXU_DTYPE)

    proj, ab, h = _in_proj(x.reshape(bsz * n_real, d), meta_tokens.astype(x.dtype), norm1_g[layer][None, :],
                           w_cat, w_ab, seq)
    proj3 = proj.reshape(bsz, seq, PROJ_WIDTH)

    ymix = _fnet(proj3, *_dft_tables(seq))

    gate_prm = jnp.zeros((SUBLANES, LANES), F32)
    gate_prm = gate_prm.at[0, 0:N_HEADS].set(a_log_fwd[layer]).at[0, 2 * N_HEADS:3 * N_HEADS].set(a_log_bwd[layer])
    gate_prm = gate_prm.at[1, 0:N_HEADS].set(dt_bias_fwd[layer]).at[1, 2 * N_HEADS:3 * N_HEADS].set(dt_bias_bwd[layer])
    og = _gdn(proj3, ab.reshape(bsz, seq, LANES), conv_w[layer], gate_prm, out_norm_g[layer][None, :])

    w_router = jnp.zeros((d, LANES), F32)
    w_router = w_router.at[:, 0:N_GROUPS].set(w_router_group[layer])
    w_router = w_router.at[:, N_GROUPS:N_GROUPS + N_EXPERTS].set(w_router_expert[layer])
    b_router = jnp.zeros((1, LANES), F32)
    b_router = b_router.at[0, 0:N_GROUPS].set(b_router_group[layer])
    b_router = b_router.at[0, N_GROUPS:N_GROUPS + N_EXPERTS].set(b_router_expert[layer])
    tt = _tile(n_tok, (384, 128))
    h1, hn, ridx, rw, cnt = _merge(
        h, og.reshape(n_tok, d), ymix.reshape(n_tok, F_WIDTH), proj,
        w_fourier[layer].astype(MXU_DTYPE), w_delta[layer].astype(MXU_DTYPE), w_out[layer].astype(MXU_DTYPE),
        norm2_g[layer][None, :], w_router, b_router, _tile(n_tok, (688, 384, 128)))

    n_assign = n_tok * TOP_K
    n_blocks = -(-n_assign // EXPERT_BLOCK) + N_EXPERTS
    dest, seg = _plan(ridx, cnt, tt)
    seg_start = seg[0, :N_EXPERTS]
    seg_blocks = seg[SUBLANES, :N_EXPERTS]

    xb = _scatter(dest, hn, jnp.zeros((n_blocks * EXPERT_BLOCK, d // 2), jnp.uint32), tt)
    yb = _experts(seg_start, seg_blocks, xb, w_gate_e[layer], w_up_e[layer], w_down_e[layer])
    out = _combine(dest, yb, h1, rw, final_norm_g[None, :], tt, seq)
    return out.reshape(bsz, n_real, d)
```

```python
import functools
import math

import jax
import jax.numpy as jnp
import numpy as np
from jax import lax
from jax.experimental import pallas as pl
from jax.experimental.pallas import tpu as pltpu

F32 = jnp.float32
MXU_DTYPE = jnp.bfloat16

N_META = 16
CHUNK = 64
N_FGROUPS = 4
FGROUP_DIM = 128
F_WIDTH = N_FGROUPS * FGROUP_DIM
N_HEADS = 8
HEAD_DIM = 128
QK_WIDTH = N_HEADS * HEAD_DIM
CONV_K = 5
N_GROUPS = 8
EXPERTS_PER_GROUP = 8
N_EXPERTS = N_GROUPS * EXPERTS_PER_GROUP
TOP_K = 2
EXPERT_BLOCK = 256
NORM_EPS = 1e-6
LANES = 128
SUBLANES = 8
VMEM_LIMIT = 52 * 1024 * 1024
GDN_GROUP = 17
DMA_UNROLL = 16
EXPERT_X_SLOTS = 6
EXPERT_Y_SLOTS = 4

COL_QKV = 0
COL_Z = 3 * QK_WIDTH
COL_GATE_F = COL_Z + QK_WIDTH
COL_GATE_D = COL_GATE_F + 1024
COL_FIN = COL_GATE_D + 1024
PROJ_WIDTH = COL_FIN + F_WIDTH
PROJ_TN = 3328
PROJ_DTYPE = jnp.bfloat16


def _dot(a, b):
    return jnp.dot(a, b, preferred_element_type=F32)


def _dot_nt(a, b):
    return lax.dot_general(a, b, (((1,), (1,)), ((), ())), preferred_element_type=F32)


def _dot_tn(a, b):
    return lax.dot_general(a, b, (((0,), (0,)), ((), ())), preferred_element_type=F32)


def _split3(x):
    hi = x.astype(MXU_DTYPE)
    r1 = x - hi.astype(F32)
    mid = r1.astype(MXU_DTYPE)
    lo = (r1 - mid.astype(F32)).astype(MXU_DTYPE)
    return hi, mid, lo


def _exact_left(sel, x):
    hi, mid, lo = _split3(x)
    return _dot(sel, hi) + _dot(sel, mid) + _dot(sel, lo)


def _exact_right(x, sel):
    hi, mid, lo = _split3(x)
    return _dot(hi, sel) + _dot(mid, sel) + _dot(lo, sel)


def _pack_rows(x):
    half = x.shape[1] // 2
    lo = lax.bitcast_convert_type(x[:, :half].astype(jnp.bfloat16).astype(F32), jnp.uint32)
    hi = lax.bitcast_convert_type(x[:, half:].astype(jnp.bfloat16).astype(F32), jnp.uint32)
    return (lo >> 16) | hi


def _unpack_rows(u):
    lo = lax.bitcast_convert_type(u << 16, F32)
    hi = lax.bitcast_convert_type(u & jnp.uint32(0xFFFF0000), F32)
    return jnp.concatenate([lo, hi], axis=1)


def _silu(x):
    return x * jax.nn.sigmoid(x)


def _params(sem):
    return pltpu.CompilerParams(dimension_semantics=sem, vmem_limit_bytes=VMEM_LIMIT)


def _inproj_kernel(x_ref, meta_ref, g_ref, w_ref, wab_ref, o_ref, ab_ref, h_ref, hn_ref):
    @pl.when(pl.program_id(1) == 0)
    def _():
        tm = h_ref.shape[0]

        @pl.when(pl.program_id(0) % 2 == 0)
        def _():
            h_ref[0:N_META, :] = meta_ref[...]
            h_ref[N_META:tm, :] = x_ref[0:tm - N_META, :]

        @pl.when(pl.program_id(0) % 2 == 1)
        def _():
            h_ref[...] = x_ref[...]

        x = h_ref[...]
        ms = jnp.mean(x * x, axis=-1, keepdims=True)
        hn_ref[...] = (x * lax.rsqrt(ms + NORM_EPS) * g_ref[...]).astype(MXU_DTYPE)
        ab_ref[...] = _dot(hn_ref[...], wab_ref[...])

    o_ref[...] = _dot(hn_ref[...], w_ref[...]).astype(o_ref.dtype)


def _in_proj(x2d, meta, gain, w_cat, w_ab, seq):
    n_x, d = x2d.shape
    n_real = seq - N_META
    t = n_x // n_real * seq
    tm = seq // 2
    assert seq % (2 * SUBLANES) == 0 and tm >= N_META
    n = w_cat.shape[1]
    x_spec = pl.BlockSpec(
        (pl.Element(tm), pl.Element(d)),
        lambda i, j: (((i // 2) * (n_real // SUBLANES) + (i % 2) * ((tm - N_META) // SUBLANES)) * SUBLANES, 0))
    return pl.pallas_call(
        _inproj_kernel,
        grid=(t // tm, n // PROJ_TN),
        in_specs=[
            x_spec,
            pl.BlockSpec((N_META, d), lambda i, j: (0, 0)),
            pl.BlockSpec((1, d), lambda i, j: (0, 0)),
            pl.BlockSpec((d, PROJ_TN), lambda i, j: (0, j)),
            pl.BlockSpec((d, LANES), lambda i, j: (0, 0)),
        ],
        out_specs=[pl.BlockSpec((tm, PROJ_TN), lambda i, j: (i, j)),
                   pl.BlockSpec((tm, LANES), lambda i, j: (i, 0)),
                   pl.BlockSpec((tm, d), lambda i, j: (i, 0))],
        out_shape=[jax.ShapeDtypeStruct((t, n), PROJ_DTYPE),
                   jax.ShapeDtypeStruct((t, LANES), F32),
                   jax.ShapeDtypeStruct((t, d), F32)],
        scratch_shapes=[pltpu.VMEM((tm, d), MXU_DTYPE)],
        compiler_params=_params(("arbitrary", "arbitrary")),
        name="in_proj",
    )(x2d, meta, gain, w_cat, w_ab)


def _fnet_kernel(x_ref, cs_ref, tw_ref, flip_ref, o_ref, r_ref, *, seq, kp, nat, scale):
    tail = kp - seq
    if tail:
        r_ref[seq:kp, :] = jnp.zeros((tail, F_WIDTH), MXU_DTYPE)
        r_ref[kp + seq:, :] = jnp.zeros((tail, F_WIDTH), MXU_DTYPE)
    cs = cs_ref[...]
    for g in range(N_FGROUPS):
        cols = slice(g * FGROUP_DIM, (g + 1) * FGROUP_DIM)
        p = _dot(x_ref[:, cols].astype(MXU_DTYPE), cs)
        r_ref[0:seq, cols] = p[:, :FGROUP_DIM].astype(MXU_DTYPE)
        r_ref[kp:kp + seq, cols] = p[:, FGROUP_DIM:].astype(MXU_DTYPE)

    a = _dot(tw_ref[:, :kp], r_ref[0:kp, :])
    b = _dot(tw_ref[:, kp:], r_ref[kp:, :])
    o_ref[0:nat, :] = ((a - b) * scale).astype(o_ref.dtype)
    mirrored = ((a + b) * scale).astype(MXU_DTYPE)
    o_ref[nat:, :] = _dot(flip_ref[...], mirrored).astype(o_ref.dtype)


def _fnet(proj3, cs128, twid, flip):
    b, seq, _ = proj3.shape
    nat, kp = twid.shape[0], twid.shape[1] // 2
    scale = 1.0 / math.sqrt(seq * FGROUP_DIM)
    const = lambda shape: pl.BlockSpec(shape, lambda bi: (0, 0))
    return pl.pallas_call(
        functools.partial(_fnet_kernel, seq=seq, kp=kp, nat=nat, scale=scale),
        grid=(b,),
        in_specs=[
            pl.BlockSpec((None, seq, F_WIDTH), lambda bi: (bi, 0, COL_FIN // F_WIDTH)),
            const((FGROUP_DIM, 2 * FGROUP_DIM)), const((nat, 2 * kp)), const((seq - nat, nat)),
        ],
        out_specs=pl.BlockSpec((None, seq, F_WIDTH), lambda bi: (bi, 0, 0)),
        out_shape=jax.ShapeDtypeStruct((b, seq, F_WIDTH), MXU_DTYPE),
        scratch_shapes=[pltpu.VMEM((2 * kp, F_WIDTH), MXU_DTYPE)],
        compiler_params=_params(("arbitrary",)),
        name="fnet",
    )(proj3, cs128, twid, flip)


def _aligned(x, m):
    return x if isinstance(x, int) else pl.multiple_of(x, m)


def _gdn_kernel(q_ref, k_ref, v_ref, z_ref, ab_ref, cwq_ref, cwk_ref, cwv_ref, prm_ref, ong_ref,
                o_ref,
                rawq_ref, rawk_ref, rawv_ref, qs_ref, ks_ref, vs_ref, gsel_ref, grow_ref, gmix_ref,
                qe_ref, mp_ref, nn_ref, dd_ref, osum_ref,
                *, seq, lp, group, n_heads_total):
    step = pl.program_id(0)
    head = jnp.minimum(step, n_heads_total - 1) % N_HEADS
    nc = lp // CHUNK
    padr = lp - seq
    halo = SUBLANES
    row64 = lax.broadcasted_iota(jnp.int32, (CHUNK, LANES), 0)
    lane64 = lax.broadcasted_iota(jnp.int32, (CHUNK, LANES), 1)

    def for_groups(fn, carry=0, scan_first=0, scan_total=0):
        n_full, rem = divmod(nc, group)
        m_full = scan_total * group // nc if rem else scan_total // max(n_full, 1)
        if n_full == 1:
            carry = fn(list(range(group)), carry, scan_first, m_full)
        elif n_full:
            carry = lax.fori_loop(
                0, n_full,
                lambda i, c: fn([i * group + j for j in range(group)], c, scan_first + i * m_full, m_full), carry)
        if rem:
            carry = fn([n_full * group + j for j in range(rem)], carry,
                       scan_first + n_full * m_full, scan_total - n_full * m_full)
        return carry

    cur_half = step % 2
    qe_w, mp_w, nn_w, dd_w, osum_w = (r.at[cur_half] for r in (qe_ref, mp_ref, nn_ref, dd_ref, osum_ref))
    qe_r, mp_r, nn_r, dd_r, osum_r = (r.at[1 - cur_half] for r in (qe_ref, mp_ref, nn_ref, dd_ref, osum_ref))

    streams = ((rawq_ref, q_ref, cwq_ref, qs_ref, HEAD_DIM ** -0.5),
               (rawk_ref, k_ref, cwk_ref, ks_ref, 1.0),
               (rawv_ref, v_ref, cwv_ref, vs_ref, None))
    def stage_conv_inputs():
        for raw_ref, src_ref, _, _, _ in streams:
            raw_ref[0:padr + halo, :] = jnp.zeros((padr + halo, LANES), F32)
            raw_ref[padr + halo:padr + halo + seq, :] = src_ref[...].astype(F32)
            raw_ref[lp + halo:lp + 2 * halo, :] = jnp.zeros((halo, LANES), F32)

    def conv_chunk(c):
        base = _aligned(c * CHUNK, CHUNK)
        for raw_ref, _, cw_ref, dst_ref, l2_scale in streams:
            cw = cw_ref[...]
            acc = None
            for j in range(CONV_K):
                off = halo - (CONV_K - 1) // 2 + j
                tap = cw[j:j + 1, :] * raw_ref[pl.ds(base + off, CHUNK), :]
                acc = tap if acc is None else acc + tap
            y = _silu(acc)
            if isinstance(c, int) and c * CHUNK < padr:
                y = jnp.where(row64 + base >= padr, y, 0.0)
            if l2_scale is not None:
                inv = lax.rsqrt(jnp.sum(y * y, axis=-1, keepdims=True) + NORM_EPS)
                y = y * (inv if l2_scale == 1.0 else inv * l2_scale)
            dst_ref[pl.ds(base, CHUNK), :] = y

    raw_ref = rawq_ref

    def stage_gate_inputs():
        raw_ref[0:padr, :] = jnp.zeros((padr, LANES), F32)
        raw_ref[padr:padr + seq, :] = ab_ref[...]

    neg_a = -jnp.exp(prm_ref[0:1, :])
    dt_bias = prm_ref[1:2, :]
    is_g = (lane64 < 8) | ((lane64 >= 16) & (lane64 < 24))
    is_beta = ((lane64 >= 8) & (lane64 < 16)) | ((lane64 >= 24) & (lane64 < 32))
    r128 = lax.broadcasted_iota(jnp.int32, (2 * CHUNK, CHUNK), 0)
    c128 = lax.broadcasted_iota(jnp.int32, (2 * CHUNK, CHUNK), 1)
    tri2 = jnp.where(((r128 < CHUNK) & (r128 >= c128)) | ((r128 >= CHUNK) & (r128 - CHUNK <= c128)),
                     1.0, 0.0).astype(MXU_DTYPE)
    sel_r = lax.broadcasted_iota(jnp.int32, (LANES, 4 * LANES), 0)
    sel_c = lax.broadcasted_iota(jnp.int32, (LANES, 4 * LANES), 1)
    n_gate_cols = 4 * N_HEADS
    sel_packed = jnp.where((sel_r < 3 * n_gate_cols)
                           & (sel_r % n_gate_cols == head + N_HEADS * (sel_c // LANES)), 1.0, 0.0).astype(MXU_DTYPE)
    tri2x3 = jnp.concatenate([tri2, tri2, tri2], axis=1)

    def gate_batch_group(cs, carry, *_):
        bases = [_aligned(c * CHUNK, CHUNK) for c in cs]
        gates, parts = [], []
        for base in bases:
            ab = raw_ref[pl.ds(base, CHUNK), :]
            g = neg_a * jax.nn.softplus(ab + dt_bias)
            beta = jax.nn.sigmoid(ab)
            gt = jnp.where(is_g, g, jnp.where(is_beta, beta, 0.0))
            gt = jnp.where(row64 + base >= padr, gt, 0.0)
            gates.append(gt)
            parts.append(jnp.concatenate(_split3(gt), axis=0))
        cums = [_dot(tri2x3, p) for p in parts]
        for c, base, cm, gt in zip(cs, bases, cums, gates):
            m = jnp.where(lane64 < 8, cm[:CHUNK], jnp.where((lane64 >= 16) & (lane64 < 24), cm[CHUNK:], gt))
            gmix_ref[pl.ds(base, CHUNK), :] = m
            grow_ref[pl.ds(_aligned(c * LANES, LANES), LANES), :] = jnp.concatenate([m, m], axis=0).T
        return carry

    def gate_head_group(cs, carry, *_):
        rows = len(cs) * CHUNK
        base = _aligned(cs[0] * CHUNK, CHUNK)
        hi, mid, lo = _split3(gmix_ref[pl.ds(base, rows), :])
        packed = (hi.astype(F32) + pltpu.roll(mid.astype(F32), n_gate_cols, 1)
                  + pltpu.roll(lo.astype(F32), 2 * n_gate_cols, 1)).astype(MXU_DTYPE)
        gsel_ref[pl.ds(base, rows), :] = _dot(packed, sel_packed)
        return carry

    fwd_half = lane64 < CHUNK
    col64 = lane64 & (CHUNK - 1)
    incl2 = (fwd_half & (row64 >= col64)) | (~fwd_half & (row64 <= col64))
    strict2 = (fwd_half & (row64 > col64)) | (~fwd_half & (row64 < col64))
    eye2 = jnp.where(row64 == col64, 1.0, 0.0).astype(F32)

    def blockdiag(p):
        return jnp.concatenate([jnp.where(fwd_half, p, 0.0), jnp.where(fwd_half, 0.0, p)],
                               axis=0).astype(MXU_DTYPE)

    n_neumann = int(math.log2(CHUNK)) - 2
    n_stages = n_neumann + 6

    def chunk_group(cs, carry, scan_first, scan_steps):
        n = len(cs)
        bases = [_aligned(c * CHUNK, CHUNK) for c in cs]
        cb128 = [_aligned(c * LANES, LANES) for c in cs]
        cb8 = [_aligned(c * SUBLANES, SUBLANES) for c in cs]
        stage = [0]

        def end_of_stage(carry):
            s = stage[0]
            stage[0] += 1
            for j in range(scan_steps):
                if j * n_stages // scan_steps == s:
                    carry = scan_step(scan_first + j, carry)
            return carry

        def gate_cols(j):
            gs = gsel_ref[pl.ds(bases[j], CHUNK), :]
            return gs[:, 0:LANES], gs[:, LANES:2 * LANES], gs[:, 2 * LANES:3 * LANES], gs[:, 3 * LANES:]

        kq = []
        for j in range(n):
            k16 = ks_ref[pl.ds(bases[j], CHUNK), :].astype(MXU_DTYPE)
            q16 = qs_ref[pl.ds(bases[j], CHUNK), :].astype(MXU_DTYPE)
            kq.append(_dot_nt(jnp.concatenate([k16, q16], axis=0), jnp.concatenate([k16, k16], axis=0)))
        carry = end_of_stage(carry)

        ts, ps, qkds = [], [], []
        for j in range(n):
            gf, bf, gb, bb = gate_cols(j)
            row_f = grow_ref[pl.ds(cb128[j] + head, 1), :]
            row_b = grow_ref[pl.ds(cb128[j] + 2 * N_HEADS + head, 1), :]
            g_col = jnp.where(fwd_half, gf, gb)
            g_row = jnp.where(fwd_half, row_f, row_b)
            decay = jnp.where(incl2, jnp.exp(jnp.where(incl2, g_col - g_row, 0.0)), 0.0)
            a2 = jnp.where(strict2, kq[j][:CHUNK] * jnp.where(fwd_half, bf, bb) * decay, 0.0)
            qkds.append(jnp.where(incl2, kq[j][CHUNK:] * decay, 0.0))
            ts.append(eye2 - a2)
            ps.append(a2)

        ps = [_dot(p.astype(MXU_DTYPE), blockdiag(p)) for p in ps]
        carry = end_of_stage(carry)
        for _ in range(n_neumann):
            tps = [_dot(jnp.concatenate([t, p], axis=0).astype(MXU_DTYPE), blockdiag(p)) for t, p in zip(ts, ps)]
            carry = end_of_stage(carry)
            ts = [t + tp[:CHUNK] for t, tp in zip(ts, tps)]
            ps = [tp[CHUNK:] for tp in tps]
        ts = [t + _dot(t.astype(MXU_DTYPE), blockdiag(p)) for t, p in zip(ts, ps)]
        carry = end_of_stage(carry)

        uws = []
        for j in range(n):
            gf, bf, gb, bb = gate_cols(j)
            k = ks_ref[pl.ds(bases[j], CHUNK), :]
            v = vs_ref[pl.ds(bases[j], CHUNK), :]
            x_f = jnp.concatenate([v * bf, k * bf * jnp.exp(gf)], axis=1)
            x_b = jnp.concatenate([v * bb, k * bb * jnp.exp(gb)], axis=1)
            uw = _dot(blockdiag(ts[j]), jnp.concatenate([x_f, x_b], axis=0).astype(MXU_DTYPE))
            uws.append(uw.astype(MXU_DTYPE))
        carry = end_of_stage(carry)

        xxs = [_dot(blockdiag(qkds[j]), uws[j]) for j in range(n)]
        carry = end_of_stage(carry)
        for j in range(n):
            gf, _, gb, _ = gate_cols(j)
            k = ks_ref[pl.ds(bases[j], CHUNK), :]
            q = qs_ref[pl.ds(bases[j], CHUNK), :]
            xx = xxs[j]
            osum_w[pl.ds(bases[j], CHUNK), :] = xx[:CHUNK, :LANES] + xx[CHUNK:, :LANES]
            for d, gc in enumerate((gf, gb)):
                rows = slice(d * CHUNK, (d + 1) * CHUNK)
                g_last = gc[CHUNK - 1:CHUNK, :] if d == 0 else gc[0:1, :]
                k_dec = k * jnp.exp(g_last - gc)
                mn = _dot_tn(k_dec.astype(MXU_DTYPE), uws[j][rows])
                qe_w[d, pl.ds(bases[j], CHUNK), :] = (q * jnp.exp(gc) - xx[rows, LANES:]).astype(MXU_DTYPE)
                nn_w[d, pl.ds(cb128[j], HEAD_DIM), :] = mn[:, :LANES]
                mp_w[d, pl.ds(cb128[j], HEAD_DIM), :] = mn[:, LANES:].astype(MXU_DTYPE)
                dd_w[d, pl.ds(cb8[j], SUBLANES), :] = jnp.broadcast_to(jnp.exp(g_last), (SUBLANES, LANES))
        carry = end_of_stage(carry)
        assert stage[0] == n_stages
        return carry

    def scan_step(i, carry):
        new = []
        for d in range(2):
            s = carry[d]
            c = i if d == 0 else nc - 1 - i
            base = _aligned(c * CHUNK, CHUNK)
            cbase = _aligned(c * HEAD_DIM, HEAD_DIM)
            dbase = _aligned(c * SUBLANES, SUBLANES)
            s16 = s.astype(MXU_DTYPE)
            osum_r[pl.ds(base, CHUNK), :] += _dot(qe_r[d, pl.ds(base, CHUNK), :], s16)
            dec = dd_r[d, pl.ds(dbase, 1), :]
            new.append(dec * s - _dot(mp_r[d, pl.ds(cbase, HEAD_DIM), :], s16)
                       + nn_r[d, pl.ds(cbase, HEAD_DIM), :])
        return tuple(new)

    zero_state = jnp.zeros((HEAD_DIM, HEAD_DIM), F32)

    def write_output():
        gain = ong_ref[...]

        def finish(o, z):
            ms = jnp.mean(o * o, axis=-1, keepdims=True)
            return (o * lax.rsqrt(ms + NORM_EPS) * gain * _silu(z.astype(F32))).astype(o_ref.dtype)

        first = CHUNK - padr if padr else 0
        if first:
            o_ref[0:first, :] = finish(osum_r[padr:CHUNK, :], z_ref[0:first, :])

        def out_body(c, carry):
            base = pl.multiple_of(c * CHUNK, CHUNK)
            dst = pl.multiple_of(c * CHUNK - padr, SUBLANES)
            o_ref[pl.ds(dst, CHUNK), :] = finish(osum_r[pl.ds(base, CHUNK), :], z_ref[pl.ds(dst, CHUNK), :])
            return carry

        lax.fori_loop(1 if first else 0, nc, out_body, 0, unroll=8)

    def conv_only(c, carry):
        conv_chunk(c)
        return carry

    conv_per_scan = 3
    n_conv_scan = (nc - 1) // conv_per_scan

    def conv_and_scan(i, carry):
        for u in range(conv_per_scan):
            conv_chunk(1 + conv_per_scan * i + u)
        return scan_step(i, carry)

    def gates():
        @pl.when(head == 0)
        def _():
            stage_gate_inputs()
            for_groups(gate_batch_group)

        for_groups(gate_head_group)

    @pl.when(step == 0)
    def _():
        stage_conv_inputs()
        conv_chunk(0)
        lax.fori_loop(1, nc, conv_only, 0, unroll=4)
        gates()
        for_groups(chunk_group)

    @pl.when((step > 0) & (step < n_heads_total))
    def _():
        stage_conv_inputs()
        conv_chunk(0)
        carry = lax.fori_loop(0, n_conv_scan, conv_and_scan, (zero_state, zero_state))
        lax.fori_loop(1 + conv_per_scan * n_conv_scan, nc, conv_only, 0)
        gates()
        for_groups(chunk_group, carry, n_conv_scan, nc - n_conv_scan)
        write_output()

    @pl.when(step == n_heads_total)
    def _():
        lax.fori_loop(0, nc, scan_step, (zero_state, zero_state))
        write_output()


def _gdn(proj3, ab3, conv_w, gate_prm, out_norm_g):
    b, seq, _ = proj3.shape
    lp = -(-seq // CHUNK) * CHUNK
    nc = lp // CHUNK
    n_total = b * N_HEADS
    cur = lambda i: jnp.minimum(i, n_total - 1)
    prev = lambda i: jnp.maximum(i - 1, 0)
    col = lambda off: pl.BlockSpec((None, seq, LANES), lambda i: (cur(i) // N_HEADS, 0, off + cur(i) % N_HEADS))
    cw = lambda off: pl.BlockSpec((CONV_K, LANES), lambda i: (0, off + cur(i) % N_HEADS))
    return pl.pallas_call(
        functools.partial(_gdn_kernel, seq=seq, lp=lp, group=GDN_GROUP, n_heads_total=n_total),
        grid=(n_total + 1,),
        in_specs=[
            col(COL_QKV // LANES), col(COL_QKV // LANES + N_HEADS), col(COL_QKV // LANES + 2 * N_HEADS),
            pl.BlockSpec((None, seq, LANES),
                         lambda i: (prev(i) // N_HEADS, 0, COL_Z // LANES + prev(i) % N_HEADS)),
            pl.BlockSpec((None, seq, LANES), lambda i: (cur(i) // N_HEADS, 0, 0)),
            cw(0), cw(N_HEADS), cw(2 * N_HEADS),
            pl.BlockSpec((SUBLANES, LANES), lambda i: (0, 0)),
            pl.BlockSpec((1, LANES), lambda i: (0, 0)),
        ],
        out_specs=pl.BlockSpec((None, seq, LANES), lambda i: (prev(i) // N_HEADS, 0, prev(i) % N_HEADS)),
        out_shape=jax.ShapeDtypeStruct((b, seq, N_HEADS * HEAD_DIM), MXU_DTYPE),
        scratch_shapes=[
            pltpu.VMEM((lp + 2 * SUBLANES, LANES), F32),
            pltpu.VMEM((lp + 2 * SUBLANES, LANES), F32),
            pltpu.VMEM((lp + 2 * SUBLANES, LANES), F32),
            pltpu.VMEM((lp, LANES), F32),
            pltpu.VMEM((lp, LANES), F32),
            pltpu.VMEM((lp, LANES), F32),
            pltpu.VMEM((lp, 4 * LANES), F32),
            pltpu.VMEM((nc * LANES, LANES), F32),
            pltpu.VMEM((lp, LANES), F32),
            pltpu.VMEM((2, 2, lp, LANES), MXU_DTYPE),
            pltpu.VMEM((2, 2, nc * HEAD_DIM, LANES), MXU_DTYPE),
            pltpu.VMEM((2, 2, nc * HEAD_DIM, LANES), F32),
            pltpu.VMEM((2, 2, nc * SUBLANES, LANES), F32),
            pltpu.VMEM((2, lp, LANES), F32),
        ],
        compiler_params=_params(("arbitrary",)),
        name="gdn",
    )(proj3, proj3, proj3, proj3, ab3, conv_w, conv_w, conv_w, gate_prm, out_norm_g)


def _merge_kernel(h_ref, og_ref, ym_ref, gf_ref, gd_ref, wf_ref, wd_ref, wo_ref, n2_ref, wr_ref, br_ref,
                  h1_ref, hn_ref, ridx_ref, rw_ref, cnt_ref):
    @pl.when(pl.program_id(0) == 0)
    def _():
        cnt_ref[...] = jnp.zeros_like(cnt_ref)

    y_f = _dot(ym_ref[...], wf_ref[...])
    y_d = _dot(og_ref[...], wd_ref[...])
    merged = (jax.nn.sigmoid(gf_ref[...].astype(F32)) * y_f
              + jax.nn.sigmoid(gd_ref[...].astype(F32)) * y_d)
    h1 = h_ref[...] + _dot(merged.astype(MXU_DTYPE), wo_ref[...])
    h1_ref[...] = h1
    ms = jnp.mean(h1 * h1, axis=-1, keepdims=True)
    hn = h1 * lax.rsqrt(ms + NORM_EPS) * n2_ref[...]
    hn_ref[...] = _pack_rows(hn)

    x_hi = hn.astype(MXU_DTYPE)
    x_lo = (hn - x_hi.astype(F32)).astype(MXU_DTYPE)
    w = wr_ref[...]
    w_hi = w.astype(MXU_DTYPE)
    w_lo = (w - w_hi.astype(F32)).astype(MXU_DTYPE)
    hi_terms = _dot(x_hi, jnp.concatenate([w_hi, w_lo], axis=1))
    logits = hi_terms[:, :LANES] + hi_terms[:, LANES:] + _dot(x_lo, w_hi) + br_ref[...]

    tm = logits.shape[0]
    lane = lax.broadcasted_iota(jnp.int32, (tm, LANES), 1)
    neg_inf = -jnp.inf
    is_group = lane < N_GROUPS
    gl = jnp.where(is_group, logits, neg_inf)
    ge = jnp.exp(gl - jnp.max(gl, axis=-1, keepdims=True))
    gp = ge / jnp.sum(ge, axis=-1, keepdims=True)
    p_group = jnp.max(gp, axis=-1, keepdims=True)
    g_idx = jnp.min(jnp.where(is_group & (gp == p_group), lane, LANES), axis=-1, keepdims=True)

    in_group = (lane >= N_GROUPS) & (lane < N_GROUPS + N_EXPERTS) & (((lane - N_GROUPS) >> 3) == g_idx)
    el = jnp.where(in_group, logits, neg_inf)
    ee = jnp.exp(el - jnp.max(el, axis=-1, keepdims=True))
    ep = ee / jnp.sum(ee, axis=-1, keepdims=True)
    v1 = jnp.max(jnp.where(in_group, ep, -1.0), axis=-1, keepdims=True)
    i1 = jnp.min(jnp.where(in_group & (ep == v1), lane, LANES), axis=-1, keepdims=True)
    rest = in_group & (lane != i1)
    v2 = jnp.max(jnp.where(rest, ep, -1.0), axis=-1, keepdims=True)
    i2 = jnp.min(jnp.where(rest & (ep == v2), lane, LANES), axis=-1, keepdims=True)
    denom = v1 + v2
    w1 = p_group * (v1 / denom)
    w2 = p_group * (v2 / denom)
    e1 = i1 - N_GROUPS
    e2 = i2 - N_GROUPS
    ridx_ref[...] = jnp.where(lane == 0, e1, jnp.where(lane == 1, e2, 0))
    rw_ref[...] = jnp.where(lane == 0, w1, jnp.where(lane == 1, w2, 0.0))
    onehots = jnp.where((lane == e1) | (lane == e2), 1.0, 0.0)
    cnt_ref[...] += jnp.broadcast_to(jnp.sum(onehots, axis=0, keepdims=True), cnt_ref.shape)


def _merge(h2d, og2d, ymix2d, proj2d, w_fourier, w_delta, w_out, norm2_g, w_router, b_router, tm):
    t, d = h2d.shape
    row = lambda w: pl.BlockSpec((tm, w), lambda i: (i, 0))
    const = lambda shape: pl.BlockSpec(shape, lambda i: (0, 0))
    return pl.pallas_call(
        _merge_kernel,
        grid=(t // tm,),
        in_specs=[
            row(d), row(d), row(F_WIDTH),
            pl.BlockSpec((tm, d), lambda i: (i, COL_GATE_F // d)),
            pl.BlockSpec((tm, d), lambda i: (i, COL_GATE_D // d)),
            const((F_WIDTH, d)), const((d, d)), const((d, d)), const((1, d)),
            const((d, LANES)), const((1, LANES)),
        ],
        out_specs=[row(d), row(d // 2), row(LANES), row(LANES), const((SUBLANES, LANES))],
        out_shape=[
            jax.ShapeDtypeStruct((t, d), F32),
            jax.ShapeDtypeStruct((t, d // 2), jnp.uint32),
            jax.ShapeDtypeStruct((t, LANES), jnp.int32),
            jax.ShapeDtypeStruct((t, LANES), F32),
            jax.ShapeDtypeStruct((SUBLANES, LANES), F32),
        ],
        compiler_params=_params(("arbitrary",)),
        name="merge",
    )(h2d, og2d, ymix2d, proj2d, proj2d, w_fourier, w_delta, w_out, norm2_g, w_router, b_router)


def _plan_kernel(ridx_ref, cnt_ref, dest_ref, seg_ref, xb_ref, base_ref, run_ref):
    tt = ridx_ref.shape[0]
    xb_ref[...] = jnp.zeros_like(xb_ref)

    @pl.when(pl.program_id(0) == 0)
    def _():
        cnt = cnt_ref[...].astype(jnp.int32)
        n_blk = (cnt + (EXPERT_BLOCK - 1)) // EXPERT_BLOCK
        padded = (n_blk * EXPERT_BLOCK).astype(F32)
        r = lax.broadcasted_iota(jnp.int32, (LANES, LANES), 0)
        c = lax.broadcasted_iota(jnp.int32, (LANES, LANES), 1)
        before = jnp.where(r < c, 1.0, 0.0).astype(MXU_DTYPE)
        start = _exact_right(padded, before)
        base_ref[...] = start
        run_ref[...] = jnp.zeros_like(run_ref)
        seg_ref[0:SUBLANES, :] = start.astype(jnp.int32)
        seg_ref[SUBLANES:2 * SUBLANES, :] = n_blk
        seg_ref[2 * SUBLANES:, :] = cnt

    lane = lax.broadcasted_iota(jnp.int32, (tt, LANES), 1)
    ridx = ridx_ref[...]
    oh0 = jnp.where(lane == ridx[:, 0:1], 1.0, 0.0)
    oh1 = jnp.where(lane == ridx[:, 1:2], 1.0, 0.0)
    r = lax.broadcasted_iota(jnp.int32, (tt, tt), 0)
    c = lax.broadcasted_iota(jnp.int32, (tt, tt), 1)
    earlier = jnp.where(c < r, 1.0, 0.0).astype(MXU_DTYPE)
    prefix = _dot(earlier, jnp.concatenate([oh0, oh1], axis=1).astype(MXU_DTYPE))
    c0 = jnp.sum(oh0, axis=0, keepdims=True)
    c1 = jnp.sum(oh1, axis=0, keepdims=True)
    first = base_ref[0:1, :] + run_ref[0:1, :]
    d0 = jnp.sum(oh0 * (prefix[:, :LANES] + first), axis=-1, keepdims=True)
    d1 = jnp.sum(oh1 * (prefix[:, LANES:] + first + c0), axis=-1, keepdims=True)
    run_ref[...] += jnp.broadcast_to(c0 + c1, run_ref.shape)
    both = jnp.where(lane == 0, d0, jnp.where(lane == 1, d1, 0.0))
    both_t = both.T.astype(jnp.int32)
    for k in range(TOP_K):
        dest_ref[k] = both_t[k:k + 1, :]


def _plan(ridx, cnt, tt, min_slots, row_words):
    t = ridx.shape[0]
    n_tiles = t // tt
    slot_rows = -(-min_slots // (n_tiles * EXPERT_BLOCK)) * EXPERT_BLOCK
    return pl.pallas_call(
        _plan_kernel,
        grid=(t // tt,),
        in_specs=[pl.BlockSpec((tt, LANES), lambda i: (i, 0)),
                  pl.BlockSpec((SUBLANES, LANES), lambda i: (0, 0))],
        out_specs=[pl.BlockSpec((TOP_K, None, 1, tt), lambda i: (0, i, 0, 0)),
                   pl.BlockSpec((3 * SUBLANES, LANES), lambda i: (0, 0)),
                   pl.BlockSpec((slot_rows, row_words), lambda i: (i, 0))],
        out_shape=[jax.ShapeDtypeStruct((TOP_K, t // tt, 1, tt), jnp.int32),
                   jax.ShapeDtypeStruct((3 * SUBLANES, LANES), jnp.int32),
                   jax.ShapeDtypeStruct((n_tiles * slot_rows, row_words), jnp.uint32)],
        scratch_shapes=[pltpu.VMEM((SUBLANES, LANES), F32), pltpu.VMEM((SUBLANES, LANES), F32)],
        compiler_params=_params(("arbitrary",)),
        name="plan",
    )(ridx, cnt)


def _scatter_kernel(dest0_ref, dest1_ref, hn_ref, xb_in_ref, xb_ref, sem, *, ts):
    dest_refs = (dest0_ref, dest1_ref)
    del xb_in_ref

    def issue(t, carry):
        src = hn_ref.at[pl.ds(t, 1)]
        for k in range(TOP_K):
            pltpu.make_async_copy(src, xb_ref.at[pl.ds(dest_refs[k][0, t], 1)], sem).start(priority=k % 2)
        return carry

    lax.fori_loop(0, ts, issue, 0, unroll=DMA_UNROLL)
    for _ in range(TOP_K):
        pltpu.make_async_copy(hn_ref, xb_ref.at[pl.ds(0, ts)], sem).wait()


def _scatter(dest_flat, hn2d, xb_zero, ts):
    t, d = hn2d.shape
    return pl.pallas_call(
        functools.partial(_scatter_kernel, ts=ts),
        grid=(t // ts,),
        in_specs=[pl.BlockSpec((None, None, 1, ts), lambda i: (0, i, 0, 0), memory_space=pltpu.SMEM),
                  pl.BlockSpec((None, None, 1, ts), lambda i: (1, i, 0, 0), memory_space=pltpu.SMEM),
                  pl.BlockSpec((ts, d), lambda i: (i, 0)),
                  pl.BlockSpec(memory_space=pl.ANY)],
        out_specs=pl.BlockSpec(memory_space=pl.ANY),
        out_shape=jax.ShapeDtypeStruct(xb_zero.shape, xb_zero.dtype),
        scratch_shapes=[pltpu.SemaphoreType.DMA(())],
        input_output_aliases={3: 0},
        compiler_params=_params(("arbitrary",)),
        name="scatter",
    )(dest_flat, dest_flat, hn2d, xb_zero)


def _expert_kernel(start_ref, nblk_ref, xb_ref, wg_ref, wu_ref, wd_ref, yb_ref,
                   wg16_ref, wu16_ref, wd16_ref, xbuf_ref, ybuf_ref, sem_in, sem_out):
    expert = pl.program_id(0)
    last = pl.num_programs(0) - 1
    n = nblk_ref[expert]
    first = start_ref[expert] // EXPERT_BLOCK
    total = start_ref[last] // EXPERT_BLOCK + nblk_ref[last]

    def rows(g):
        return pl.ds(pl.multiple_of(g * EXPERT_BLOCK, EXPERT_BLOCK), EXPERT_BLOCK)

    def x_copy(g):
        slot = g % EXPERT_X_SLOTS
        return pltpu.make_async_copy(xb_ref.at[rows(g)], xbuf_ref.at[slot], sem_in.at[slot])

    def y_copy(g):
        slot = g % EXPERT_Y_SLOTS
        return pltpu.make_async_copy(ybuf_ref.at[slot], yb_ref.at[rows(g)], sem_out.at[slot])

    ahead = EXPERT_X_SLOTS - 2

    def process(g, k):
        for j in range(k):
            @pl.when(g + j + ahead < total)
            def _():
                x_copy(g + j + ahead).start()

        for j in range(k):
            x_copy(g + j).wait()

        for j in range(k):
            @pl.when(g + j >= EXPERT_Y_SLOTS)
            def _():
                y_copy(g + j - EXPERT_Y_SLOTS).wait()

        x = jnp.concatenate([_unpack_rows(xbuf_ref[(g + j) % EXPERT_X_SLOTS]) for j in range(k)],
                            axis=0).astype(MXU_DTYPE)
        gate = _dot(x, wg16_ref[...])
        up = _dot(x, wu16_ref[...])
        y = _pack_rows(_dot((_silu(gate) * up).astype(MXU_DTYPE), wd16_ref[...]))
        for j in range(k):
            ybuf_ref[(g + j) % EXPERT_Y_SLOTS] = y[j * EXPERT_BLOCK:(j + 1) * EXPERT_BLOCK]
            y_copy(g + j).start()

    @pl.when(n > 0)
    def _():
        @pl.when(first == 0)
        def _():
            for j in range(ahead):
                @pl.when(j < total)
                def _():
                    x_copy(j).start()

        wg16_ref[...] = wg_ref[...].astype(MXU_DTYPE)
        wu16_ref[...] = wu_ref[...].astype(MXU_DTYPE)
        wd16_ref[...] = wd_ref[...].astype(MXU_DTYPE)

        def pair(i, carry):
            process(first + 2 * i, 2)
            return carry

        lax.fori_loop(0, n // 2, pair, 0)

        @pl.when(n % 2 == 1)
        def _():
            process(first + n - 1, 1)

    @pl.when(expert == last)
    def _():
        for j in range(EXPERT_Y_SLOTS):
            @pl.when(total > j)
            def _():
                y_copy(total - 1 - j).wait()


def _experts(seg_start, seg_blocks, xb, w_gate, w_up, w_down):
    n_slots, row_words = xb.shape
    n_exp, d, de = w_gate.shape
    grid_spec = pltpu.PrefetchScalarGridSpec(
        num_scalar_prefetch=2,
        grid=(n_exp,),
        in_specs=[
            pl.BlockSpec(memory_space=pl.ANY),
            pl.BlockSpec((None, d, de), lambda e, st, nb: (e, 0, 0)),
            pl.BlockSpec((None, d, de), lambda e, st, nb: (e, 0, 0)),
            pl.BlockSpec((None, de, d), lambda e, st, nb: (e, 0, 0)),
        ],
        out_specs=pl.BlockSpec(memory_space=pl.ANY),
        scratch_shapes=[
            pltpu.VMEM((d, de), MXU_DTYPE), pltpu.VMEM((d, de), MXU_DTYPE), pltpu.VMEM((de, d), MXU_DTYPE),
            pltpu.VMEM((EXPERT_X_SLOTS, EXPERT_BLOCK, row_words), jnp.uint32),
            pltpu.VMEM((EXPERT_Y_SLOTS, EXPERT_BLOCK, row_words), jnp.uint32),
            pltpu.SemaphoreType.DMA((EXPERT_X_SLOTS,)), pltpu.SemaphoreType.DMA((EXPERT_Y_SLOTS,)),
        ],
    )
    return pl.pallas_call(
        _expert_kernel,
        grid_spec=grid_spec,
        out_shape=jax.ShapeDtypeStruct((n_slots, row_words), jnp.uint32),
        input_output_aliases={2: 0},
        compiler_params=_params(("arbitrary",)),
        name="experts",
    )(seg_start, seg_blocks, xb, w_gate, w_up, w_down)


def _combine_kernel(dest0_ref, dest1_ref, yb_ref, h1_ref, rw_ref, fg_ref, o_ref, buf_ref, res_ref, sem, out_sem,
                    *, tc, groups_per_batch):
    step = pl.program_id(0)
    n_tiles = pl.num_programs(0) - 1
    n_grp = tc // N_META
    cur = step % 2
    prev = 1 - cur

    def output_copies(tile, action):
        slot = tile % 2
        first = tile * n_grp
        first_in_batch = first % groups_per_batch
        has_meta = (first_in_batch == 0) | (first_in_batch + n_grp > groups_per_batch)

        def copy_groups(q, n):
            grp = first + q
            dst = pl.multiple_of((grp - grp // groups_per_batch - 1) * N_META, N_META)
            copy = pltpu.make_async_copy(res_ref.at[slot, pl.ds(q * N_META, n * N_META)],
                                         o_ref.at[pl.ds(dst, n * N_META)], out_sem.at[slot])
            copy.start() if action == "start" else copy.wait()

        @pl.when(jnp.logical_not(has_meta))
        def _():
            copy_groups(0, n_grp)

        @pl.when(has_meta)
        def _():
            for q in range(n_grp):
                @pl.when((first + q) % groups_per_batch != 0)
                def _():
                    copy_groups(q, 1)

    def request_rows(i):
        for u in range(SUBLANES):
            t = i * SUBLANES + u
            for k in range(TOP_K):
                pltpu.make_async_copy(yb_ref.at[pl.ds((dest0_ref, dest1_ref)[k][0, t], 1)],
                                      buf_ref.at[cur, k, pl.ds(t, 1)], sem.at[cur]).start(priority=k % 2)

    def finish_rows(i):
        rows = pl.ds(pl.multiple_of(i * SUBLANES, SUBLANES), SUBLANES)
        rw = rw_ref[rows, :]
        h2 = h1_ref[rows, :] + (_unpack_rows(buf_ref[prev, 0, rows, :]) * rw[:, 0:1]
                                + _unpack_rows(buf_ref[prev, 1, rows, :]) * rw[:, 1:2])
        ms = jnp.mean(h2 * h2, axis=-1, keepdims=True)
        res_ref[prev, rows, :] = h2 * lax.rsqrt(ms + NORM_EPS) * fg_ref[...]

    def begin_finish():
        for k in range(TOP_K):
            pltpu.make_async_copy(yb_ref.at[pl.ds(0, tc)], buf_ref.at[prev, k], sem.at[prev]).wait()

        @pl.when(step >= 3)
        def _():
            output_copies(step - 3, "wait")

    def both(i, carry):
        request_rows(i)
        finish_rows(i)
        return carry

    def request_only(i, carry):
        request_rows(i)
        return carry

    def finish_only(i, carry):
        finish_rows(i)
        return carry

    @pl.when(step == 0)
    def _():
        lax.fori_loop(0, tc // SUBLANES, request_only, 0)

    @pl.when((step > 0) & (step < n_tiles))
    def _():
        begin_finish()
        lax.fori_loop(0, tc // SUBLANES, both, 0, unroll=16)
        output_copies(step - 1, "start")

    @pl.when(step == n_tiles)
    def _():
        begin_finish()
        lax.fori_loop(0, tc // SUBLANES, finish_only, 0, unroll=4)
        output_copies(step - 1, "start")

        @pl.when(step >= 2)
        def _():
            output_copies(step - 2, "wait")

        output_copies(step - 1, "wait")


def _combine(dest_flat, yb, h1, rw, final_g, tc, seq):
    t, d = h1.shape
    assert tc % N_META == 0 and seq % N_META == 0
    n_out = t - (t // seq) * N_META
    n_tiles = t // tc
    nxt = lambda i: jnp.minimum(i, n_tiles - 1)
    fin = lambda i: jnp.maximum(i - 1, 0)
    return pl.pallas_call(
        functools.partial(_combine_kernel, tc=tc, groups_per_batch=seq // N_META),
        grid=(n_tiles + 1,),
        in_specs=[pl.BlockSpec((None, None, 1, tc), lambda i: (0, nxt(i), 0, 0), memory_space=pltpu.SMEM),
                  pl.BlockSpec((None, None, 1, tc), lambda i: (1, nxt(i), 0, 0), memory_space=pltpu.SMEM),
                  pl.BlockSpec(memory_space=pl.ANY),
                  pl.BlockSpec((tc, d), lambda i: (fin(i), 0)),
                  pl.BlockSpec((tc, LANES), lambda i: (fin(i), 0)),
                  pl.BlockSpec((1, d), lambda i: (0, 0))],
        out_specs=pl.BlockSpec(memory_space=pl.ANY),
        out_shape=jax.ShapeDtypeStruct((n_out, d), F32),
        scratch_shapes=[pltpu.VMEM((2, TOP_K, tc, d // 2), jnp.uint32), pltpu.VMEM((2, tc, d), F32),
                        pltpu.SemaphoreType.DMA((2,)), pltpu.SemaphoreType.DMA((2,))],
        compiler_params=_params(("arbitrary",)),
        name="combine",
    )(dest_flat, dest_flat, yb, h1, rw, final_g)


def _tile(total, candidates):
    for c in candidates:
        if total % c == 0:
            return c
    raise ValueError(f"no tile for {total}")


def _dft_tables(seq):
    kp = -(-seq // LANES) * LANES
    nat = -(-(seq // 2 + 1) // 16) * 16
    mir = seq - nat
    c = np.arange(FGROUP_DIM, dtype=np.int64)
    ang_c = (2.0 * np.pi / FGROUP_DIM) * ((c[:, None] * c[None, :]) % FGROUP_DIM)
    cs128 = np.concatenate([np.cos(ang_c), np.sin(ang_c)], axis=1)
    k = np.arange(nat, dtype=np.int64)
    n = np.arange(seq, dtype=np.int64)
    ang_l = (2.0 * np.pi / seq) * ((k[:, None] * n[None, :]) % seq)
    twid = np.zeros((nat, 2 * kp), np.float32)
    twid[:, :seq] = np.cos(ang_l)
    twid[:, kp:kp + seq] = np.sin(ang_l)
    flip = np.zeros((mir, nat), np.float32)
    flip[np.arange(mir), mir - np.arange(mir)] = 1.0
    as_const = lambda t: jnp.asarray(t.astype(MXU_DTYPE))
    return as_const(cs128), as_const(twid), as_const(flip)


def kernel(x, meta_tokens, norm1_g, w_in, conv_w, a_log_fwd, dt_bias_fwd, a_log_bwd, dt_bias_bwd, out_norm_g, w_fourier, w_delta, w_out, norm2_g, w_router_group, b_router_group, w_router_expert, b_router_expert, w_gate_e, w_up_e, w_down_e, final_norm_g):
    bsz, n_real, d = x.shape
    seq = N_META + n_real
    n_tok = bsz * seq
    assert (CHUNK - N_META) % CHUNK + seq == -(-seq // CHUNK) * CHUNK, "meta tokens must fill the tail of chunk 0"
    assert w_in.shape[0] == 1, "single trunk layer"
    layer = 0


    w = w_in[layer]
    o_f, o_qkv, o_z, o_ab, o_gf, o_gd = 0, F_WIDTH, F_WIDTH + 3 * QK_WIDTH, F_WIDTH + 4 * QK_WIDTH, \
        F_WIDTH + 4 * QK_WIDTH + 4 * N_HEADS, F_WIDTH + 4 * QK_WIDTH + 4 * N_HEADS + d
    w_cat = jnp.concatenate(
        [w[:, o_qkv:o_z], w[:, o_z:o_ab], w[:, o_gf:o_gd], w[:, o_gd:o_gd + d], w[:, o_f:o_qkv]],
        axis=1).astype(MXU_DTYPE)
    w_ab = jnp.pad(w[:, o_ab:o_gf], ((0, 0), (0, LANES - 4 * N_HEADS))).astype(MXU_DTYPE)

    proj, ab, h = _in_proj(x.reshape(bsz * n_real, d), meta_tokens.astype(x.dtype), norm1_g[layer][None, :],
                           w_cat, w_ab, seq)
    proj3 = proj.reshape(bsz, seq, PROJ_WIDTH)

    ymix = _fnet(proj3, *_dft_tables(seq))

    gate_prm = jnp.zeros((SUBLANES, LANES), F32)
    gate_prm = gate_prm.at[0, 0:N_HEADS].set(a_log_fwd[layer]).at[0, 2 * N_HEADS:3 * N_HEADS].set(a_log_bwd[layer])
    gate_prm = gate_prm.at[1, 0:N_HEADS].set(dt_bias_fwd[layer]).at[1, 2 * N_HEADS:3 * N_HEADS].set(dt_bias_bwd[layer])
    og = _gdn(proj3, ab.reshape(bsz, seq, LANES), conv_w[layer], gate_prm, out_norm_g[layer][None, :])

    w_router = jnp.zeros((d, LANES), F32)
    w_router = w_router.at[:, 0:N_GROUPS].set(w_router_group[layer])
    w_router = w_router.at[:, N_GROUPS:N_GROUPS + N_EXPERTS].set(w_router_expert[layer])
    b_router = jnp.zeros((1, LANES), F32)
    b_router = b_router.at[0, 0:N_GROUPS].set(b_router_group[layer])
    b_router = b_router.at[0, N_GROUPS:N_GROUPS + N_EXPERTS].set(b_router_expert[layer])
    tt = _tile(n_tok, (384, 128))
    h1, hn, ridx, rw, cnt = _merge(
        h, og.reshape(n_tok, d), ymix.reshape(n_tok, F_WIDTH), proj,
        w_fourier[layer].astype(MXU_DTYPE), w_delta[layer].astype(MXU_DTYPE), w_out[layer].astype(MXU_DTYPE),
        norm2_g[layer][None, :], w_router, b_router, _tile(n_tok, (688, 384, 128)))

    n_assign = n_tok * TOP_K
    n_blocks = -(-n_assign // EXPERT_BLOCK) + N_EXPERTS
    dest, seg, xb_zero = _plan(ridx, cnt, tt, n_blocks * EXPERT_BLOCK, d // 2)
    seg_start = seg[0, :N_EXPERTS]
    seg_blocks = seg[SUBLANES, :N_EXPERTS]

    xb = _scatter(dest, hn, xb_zero, tt)
    yb = _experts(seg_start, seg_blocks, xb, w_gate_e[layer], w_up_e[layer], w_down_e[layer])
    out = _combine(dest, yb, h1, rw, final_norm_g[None, :], tt, seq)
    return out.reshape(bsz, n_real, d)
```

```python
import functools
import math

import jax
import jax.numpy as jnp
import numpy as np
from jax import lax
from jax.experimental import pallas as pl
from jax.experimental.pallas import tpu as pltpu

F32 = jnp.float32
MXU_DTYPE = jnp.bfloat16

N_META = 16
CHUNK = 64
N_FGROUPS = 4
FGROUP_DIM = 128
F_WIDTH = N_FGROUPS * FGROUP_DIM
N_HEADS = 8
HEAD_DIM = 128
QK_WIDTH = N_HEADS * HEAD_DIM
CONV_K = 5
N_GROUPS = 8
EXPERTS_PER_GROUP = 8
N_EXPERTS = N_GROUPS * EXPERTS_PER_GROUP
TOP_K = 2
EXPERT_BLOCK = 256
NORM_EPS = 1e-6
LANES = 128
SUBLANES = 8
VMEM_LIMIT = 52 * 1024 * 1024
GDN_GROUP = 17
DMA_UNROLL = 16
EXPERT_PASS = 3
EXPERT_X_SLOTS = 8
EXPERT_Y_SLOTS = 6

COL_QKV = 0
COL_Z = 3 * QK_WIDTH
COL_GATE_F = COL_Z + QK_WIDTH
COL_GATE_D = COL_GATE_F + 1024
COL_FIN = COL_GATE_D + 1024
PROJ_WIDTH = COL_FIN + F_WIDTH
PROJ_TN = 3328
PROJ_DTYPE = jnp.bfloat16


def _dot(a, b):
    return jnp.dot(a, b, preferred_element_type=F32)


def _dot_nt(a, b):
    return lax.dot_general(a, b, (((1,), (1,)), ((), ())), preferred_element_type=F32)


def _dot_tn(a, b):
    return lax.dot_general(a, b, (((0,), (0,)), ((), ())), preferred_element_type=F32)


def _split3(x):
    hi = x.astype(MXU_DTYPE)
    r1 = x - hi.astype(F32)
    mid = r1.astype(MXU_DTYPE)
    lo = (r1 - mid.astype(F32)).astype(MXU_DTYPE)
    return hi, mid, lo


def _exact_left(sel, x):
    hi, mid, lo = _split3(x)
    return _dot(sel, hi) + _dot(sel, mid) + _dot(sel, lo)


def _exact_right(x, sel):
    hi, mid, lo = _split3(x)
    return _dot(hi, sel) + _dot(mid, sel) + _dot(lo, sel)


def _pack_rows(x):
    half = x.shape[1] // 2
    lo = lax.bitcast_convert_type(x[:, :half].astype(jnp.bfloat16).astype(F32), jnp.uint32)
    hi = lax.bitcast_convert_type(x[:, half:].astype(jnp.bfloat16).astype(F32), jnp.uint32)
    return (lo >> 16) | hi


def _unpack_rows(u):
    lo = lax.bitcast_convert_type(u << 16, F32)
    hi = lax.bitcast_convert_type(u & jnp.uint32(0xFFFF0000), F32)
    return jnp.concatenate([lo, hi], axis=1)


def _silu(x):
    return x * jax.nn.sigmoid(x)


def _params(sem):
    return pltpu.CompilerParams(dimension_semantics=sem, vmem_limit_bytes=VMEM_LIMIT)


def _inproj_kernel(x_ref, meta_ref, g_ref, w_ref, wab_ref, o_ref, ab_ref, h_ref, hn_ref):
    @pl.when(pl.program_id(1) == 0)
    def _():
        tm = h_ref.shape[0]

        @pl.when(pl.program_id(0) % 2 == 0)
        def _():
            h_ref[0:N_META, :] = meta_ref[...]
            h_ref[N_META:tm, :] = x_ref[0:tm - N_META, :]

        @pl.when(pl.program_id(0) % 2 == 1)
        def _():
            h_ref[...] = x_ref[...]

        x = h_ref[...]
        ms = jnp.mean(x * x, axis=-1, keepdims=True)
        hn_ref[...] = (x * lax.rsqrt(ms + NORM_EPS) * g_ref[...]).astype(MXU_DTYPE)
        ab_ref[...] = _dot(hn_ref[...], wab_ref[...])

    o_ref[...] = _dot(hn_ref[...], w_ref[...]).astype(o_ref.dtype)


def _in_proj(x2d, meta, gain, w_cat, w_ab, seq):
    n_x, d = x2d.shape
    n_real = seq - N_META
    t = n_x // n_real * seq
    tm = seq // 2
    assert seq % (2 * SUBLANES) == 0 and tm >= N_META
    n = w_cat.shape[1]
    x_spec = pl.BlockSpec(
        (pl.Element(tm), pl.Element(d)),
        lambda i, j: (((i // 2) * (n_real // SUBLANES) + (i % 2) * ((tm - N_META) // SUBLANES)) * SUBLANES, 0))
    return pl.pallas_call(
        _inproj_kernel,
        grid=(t // tm, n // PROJ_TN),
        in_specs=[
            x_spec,
            pl.BlockSpec((N_META, d), lambda i, j: (0, 0)),
            pl.BlockSpec((1, d), lambda i, j: (0, 0)),
            pl.BlockSpec((d, PROJ_TN), lambda i, j: (0, j)),
            pl.BlockSpec((d, LANES), lambda i, j: (0, 0)),
        ],
        out_specs=[pl.BlockSpec((tm, PROJ_TN), lambda i, j: (i, j)),
                   pl.BlockSpec((tm, LANES), lambda i, j: (i, 0)),
                   pl.BlockSpec((tm, d), lambda i, j: (i, 0))],
        out_shape=[jax.ShapeDtypeStruct((t, n), PROJ_DTYPE),
                   jax.ShapeDtypeStruct((t, LANES), F32),
                   jax.ShapeDtypeStruct((t, d), F32)],
        scratch_shapes=[pltpu.VMEM((tm, d), MXU_DTYPE)],
        compiler_params=_params(("arbitrary", "arbitrary")),
        name="in_proj",
    )(x2d, meta, gain, w_cat, w_ab)


def _fnet_kernel(x_ref, cs_ref, tw_ref, flip_ref, o_ref, r_ref, *, seq, kp, nat, scale):
    tail = kp - seq
    if tail:
        r_ref[seq:kp, :] = jnp.zeros((tail, F_WIDTH), MXU_DTYPE)
        r_ref[kp + seq:, :] = jnp.zeros((tail, F_WIDTH), MXU_DTYPE)
    cs = cs_ref[...]
    for g in range(N_FGROUPS):
        cols = slice(g * FGROUP_DIM, (g + 1) * FGROUP_DIM)
        p = _dot(x_ref[:, cols].astype(MXU_DTYPE), cs)
        r_ref[0:seq, cols] = p[:, :FGROUP_DIM].astype(MXU_DTYPE)
        r_ref[kp:kp + seq, cols] = p[:, FGROUP_DIM:].astype(MXU_DTYPE)

    a = _dot(tw_ref[:, :kp], r_ref[0:kp, :])
    b = _dot(tw_ref[:, kp:], r_ref[kp:, :])
    o_ref[0:nat, :] = ((a - b) * scale).astype(o_ref.dtype)
    mirrored = ((a + b) * scale).astype(MXU_DTYPE)
    o_ref[nat:, :] = _dot(flip_ref[...], mirrored).astype(o_ref.dtype)


def _fnet(proj3, cs128, twid, flip):
    b, seq, _ = proj3.shape
    nat, kp = twid.shape[0], twid.shape[1] // 2
    scale = 1.0 / math.sqrt(seq * FGROUP_DIM)
    const = lambda shape: pl.BlockSpec(shape, lambda bi: (0, 0))
    return pl.pallas_call(
        functools.partial(_fnet_kernel, seq=seq, kp=kp, nat=nat, scale=scale),
        grid=(b,),
        in_specs=[
            pl.BlockSpec((None, seq, F_WIDTH), lambda bi: (bi, 0, COL_FIN // F_WIDTH)),
            const((FGROUP_DIM, 2 * FGROUP_DIM)), const((nat, 2 * kp)), const((seq - nat, nat)),
        ],
        out_specs=pl.BlockSpec((None, seq, F_WIDTH), lambda bi: (bi, 0, 0)),
        out_shape=jax.ShapeDtypeStruct((b, seq, F_WIDTH), MXU_DTYPE),
        scratch_shapes=[pltpu.VMEM((2 * kp, F_WIDTH), MXU_DTYPE)],
        compiler_params=_params(("arbitrary",)),
        name="fnet",
    )(proj3, cs128, twid, flip)


def _aligned(x, m):
    return x if isinstance(x, int) else pl.multiple_of(x, m)


def _gdn_kernel(q_ref, k_ref, v_ref, z_ref, ab_ref, cwq_ref, cwk_ref, cwv_ref, prm_ref, ong_ref,
                o_ref,
                rawq_ref, rawk_ref, rawv_ref, qs_ref, ks_ref, vs_ref, gsel_ref, grow_ref, gmix_ref,
                qe_ref, mp_ref, nn_ref, dd_ref, osum_ref,
                *, seq, lp, group, n_heads_total):
    step = pl.program_id(0)
    head = jnp.minimum(step, n_heads_total - 1) % N_HEADS
    nc = lp // CHUNK
    padr = lp - seq
    halo = SUBLANES
    row64 = lax.broadcasted_iota(jnp.int32, (CHUNK, LANES), 0)
    lane64 = lax.broadcasted_iota(jnp.int32, (CHUNK, LANES), 1)

    def for_groups(fn, carry=0, scan_first=0, scan_total=0):
        n_full, rem = divmod(nc, group)
        m_full = scan_total * group // nc if rem else scan_total // max(n_full, 1)
        if n_full == 1:
            carry = fn(list(range(group)), carry, scan_first, m_full)
        elif n_full:
            carry = lax.fori_loop(
                0, n_full,
                lambda i, c: fn([i * group + j for j in range(group)], c, scan_first + i * m_full, m_full), carry)
        if rem:
            carry = fn([n_full * group + j for j in range(rem)], carry,
                       scan_first + n_full * m_full, scan_total - n_full * m_full)
        return carry

    cur_half = step % 2
    qe_w, mp_w, nn_w, dd_w, osum_w = (r.at[cur_half] for r in (qe_ref, mp_ref, nn_ref, dd_ref, osum_ref))
    qe_r, mp_r, nn_r, dd_r, osum_r = (r.at[1 - cur_half] for r in (qe_ref, mp_ref, nn_ref, dd_ref, osum_ref))

    streams = ((rawq_ref, q_ref, cwq_ref, qs_ref, HEAD_DIM ** -0.5),
               (rawk_ref, k_ref, cwk_ref, ks_ref, 1.0),
               (rawv_ref, v_ref, cwv_ref, vs_ref, None))
    def stage_conv_inputs():
        for raw_ref, src_ref, _, _, _ in streams:
            raw_ref[0:padr + halo, :] = jnp.zeros((padr + halo, LANES), F32)
            raw_ref[padr + halo:padr + halo + seq, :] = src_ref[...].astype(F32)
            raw_ref[lp + halo:lp + 2 * halo, :] = jnp.zeros((halo, LANES), F32)

    def conv_chunk(c):
        base = _aligned(c * CHUNK, CHUNK)
        for raw_ref, _, cw_ref, dst_ref, l2_scale in streams:
            cw = cw_ref[...]
            acc = None
            for j in range(CONV_K):
                off = halo - (CONV_K - 1) // 2 + j
                tap = cw[j:j + 1, :] * raw_ref[pl.ds(base + off, CHUNK), :]
                acc = tap if acc is None else acc + tap
            y = _silu(acc)
            if isinstance(c, int) and c * CHUNK < padr:
                y = jnp.where(row64 + base >= padr, y, 0.0)
            if l2_scale is not None:
                inv = lax.rsqrt(jnp.sum(y * y, axis=-1, keepdims=True) + NORM_EPS)
                y = y * (inv if l2_scale == 1.0 else inv * l2_scale)
            dst_ref[pl.ds(base, CHUNK), :] = y

    raw_ref = rawq_ref

    def stage_gate_inputs():
        raw_ref[0:padr, :] = jnp.zeros((padr, LANES), F32)
        raw_ref[padr:padr + seq, :] = ab_ref[...]

    neg_a = -jnp.exp(prm_ref[0:1, :])
    dt_bias = prm_ref[1:2, :]
    is_g = (lane64 < 8) | ((lane64 >= 16) & (lane64 < 24))
    is_beta = ((lane64 >= 8) & (lane64 < 16)) | ((lane64 >= 24) & (lane64 < 32))
    r128 = lax.broadcasted_iota(jnp.int32, (2 * CHUNK, CHUNK), 0)
    c128 = lax.broadcasted_iota(jnp.int32, (2 * CHUNK, CHUNK), 1)
    tri2 = jnp.where(((r128 < CHUNK) & (r128 >= c128)) | ((r128 >= CHUNK) & (r128 - CHUNK <= c128)),
                     1.0, 0.0).astype(MXU_DTYPE)
    sel_r = lax.broadcasted_iota(jnp.int32, (LANES, 4 * LANES), 0)
    sel_c = lax.broadcasted_iota(jnp.int32, (LANES, 4 * LANES), 1)
    n_gate_cols = 4 * N_HEADS
    sel_packed = jnp.where((sel_r < 3 * n_gate_cols)
                           & (sel_r % n_gate_cols == head + N_HEADS * (sel_c // LANES)), 1.0, 0.0).astype(MXU_DTYPE)
    tri2x3 = jnp.concatenate([tri2, tri2, tri2], axis=1)

    def gate_batch_group(cs, carry, *_):
        bases = [_aligned(c * CHUNK, CHUNK) for c in cs]
        gates, parts = [], []
        for base in bases:
            ab = raw_ref[pl.ds(base, CHUNK), :]
            g = neg_a * jax.nn.softplus(ab + dt_bias)
            beta = jax.nn.sigmoid(ab)
            gt = jnp.where(is_g, g, jnp.where(is_beta, beta, 0.0))
            gt = jnp.where(row64 + base >= padr, gt, 0.0)
            gates.append(gt)
            parts.append(jnp.concatenate(_split3(gt), axis=0))
        cums = [_dot(tri2x3, p) for p in parts]
        for c, base, cm, gt in zip(cs, bases, cums, gates):
            m = jnp.where(lane64 < 8, cm[:CHUNK], jnp.where((lane64 >= 16) & (lane64 < 24), cm[CHUNK:], gt))
            gmix_ref[pl.ds(base, CHUNK), :] = m
            grow_ref[pl.ds(_aligned(c * LANES, LANES), LANES), :] = jnp.concatenate([m, m], axis=0).T
        return carry

    def gate_head_group(cs, carry, *_):
        rows = len(cs) * CHUNK
        base = _aligned(cs[0] * CHUNK, CHUNK)
        hi, mid, lo = _split3(gmix_ref[pl.ds(base, rows), :])
        packed = (hi.astype(F32) + pltpu.roll(mid.astype(F32), n_gate_cols, 1)
                  + pltpu.roll(lo.astype(F32), 2 * n_gate_cols, 1)).astype(MXU_DTYPE)
        gsel_ref[pl.ds(base, rows), :] = _dot(packed, sel_packed)
        return carry

    fwd_half = lane64 < CHUNK
    col64 = lane64 & (CHUNK - 1)
    incl2 = (fwd_half & (row64 >= col64)) | (~fwd_half & (row64 <= col64))
    strict2 = (fwd_half & (row64 > col64)) | (~fwd_half & (row64 < col64))
    eye2 = jnp.where(row64 == col64, 1.0, 0.0).astype(F32)

    def blockdiag(p):
        return jnp.concatenate([jnp.where(fwd_half, p, 0.0), jnp.where(fwd_half, 0.0, p)],
                               axis=0).astype(MXU_DTYPE)

    n_neumann = int(math.log2(CHUNK)) - 2
    n_stages = n_neumann + 6

    def chunk_group(cs, carry, scan_first, scan_steps):
        n = len(cs)
        bases = [_aligned(c * CHUNK, CHUNK) for c in cs]
        cb128 = [_aligned(c * LANES, LANES) for c in cs]
        cb8 = [_aligned(c * SUBLANES, SUBLANES) for c in cs]
        stage = [0]

        def end_of_stage(carry):
            s = stage[0]
            stage[0] += 1
            for j in range(scan_steps):
                if j * n_stages // scan_steps == s:
                    carry = scan_step(scan_first + j, carry)
            return carry

        def gate_cols(j):
            gs = gsel_ref[pl.ds(bases[j], CHUNK), :]
            return gs[:, 0:LANES], gs[:, LANES:2 * LANES], gs[:, 2 * LANES:3 * LANES], gs[:, 3 * LANES:]

        kq = []
        for j in range(n):
            k16 = ks_ref[pl.ds(bases[j], CHUNK), :].astype(MXU_DTYPE)
            q16 = qs_ref[pl.ds(bases[j], CHUNK), :].astype(MXU_DTYPE)
            kq.append(_dot_nt(jnp.concatenate([k16, q16], axis=0), jnp.concatenate([k16, k16], axis=0)))
        carry = end_of_stage(carry)

        ts, ps, qkds = [], [], []
        for j in range(n):
            gf, bf, gb, bb = gate_cols(j)
            row_f = grow_ref[pl.ds(cb128[j] + head, 1), :]
            row_b = grow_ref[pl.ds(cb128[j] + 2 * N_HEADS + head, 1), :]
            g_col = jnp.where(fwd_half, gf, gb)
            g_row = jnp.where(fwd_half, row_f, row_b)
            decay = jnp.where(incl2, jnp.exp(jnp.where(incl2, g_col - g_row, 0.0)), 0.0)
            a2 = jnp.where(strict2, kq[j][:CHUNK] * jnp.where(fwd_half, bf, bb) * decay, 0.0)
            qkds.append(jnp.where(incl2, kq[j][CHUNK:] * decay, 0.0))
            ts.append(eye2 - a2)
            ps.append(a2)

        ps = [_dot(p.astype(MXU_DTYPE), blockdiag(p)) for p in ps]
        carry = end_of_stage(carry)
        for _ in range(n_neumann):
            tps = [_dot(jnp.concatenate([t, p], axis=0).astype(MXU_DTYPE), blockdiag(p)) for t, p in zip(ts, ps)]
            carry = end_of_stage(carry)
            ts = [t + tp[:CHUNK] for t, tp in zip(ts, tps)]
            ps = [tp[CHUNK:] for tp in tps]
        ts = [t + _dot(t.astype(MXU_DTYPE), blockdiag(p)) for t, p in zip(ts, ps)]
        carry = end_of_stage(carry)

        uws = []
        for j in range(n):
            gf, bf, gb, bb = gate_cols(j)
            k = ks_ref[pl.ds(bases[j], CHUNK), :]
            v = vs_ref[pl.ds(bases[j], CHUNK), :]
            x_f = jnp.concatenate([v * bf, k * bf * jnp.exp(gf)], axis=1)
            x_b = jnp.concatenate([v * bb, k * bb * jnp.exp(gb)], axis=1)
            uw = _dot(blockdiag(ts[j]), jnp.concatenate([x_f, x_b], axis=0).astype(MXU_DTYPE))
            uws.append(uw.astype(MXU_DTYPE))
        carry = end_of_stage(carry)

        xxs = [_dot(blockdiag(qkds[j]), uws[j]) for j in range(n)]
        carry = end_of_stage(carry)
        for j in range(n):
            gf, _, gb, _ = gate_cols(j)
            k = ks_ref[pl.ds(bases[j], CHUNK), :]
            q = qs_ref[pl.ds(bases[j], CHUNK), :]
            xx = xxs[j]
            osum_w[pl.ds(bases[j], CHUNK), :] = xx[:CHUNK, :LANES] + xx[CHUNK:, :LANES]
            for d, gc in enumerate((gf, gb)):
                rows = slice(d * CHUNK, (d + 1) * CHUNK)
                g_last = gc[CHUNK - 1:CHUNK, :] if d == 0 else gc[0:1, :]
                k_dec = k * jnp.exp(g_last - gc)
                mn = _dot_tn(k_dec.astype(MXU_DTYPE), uws[j][rows])
                qe_w[d, pl.ds(bases[j], CHUNK), :] = (q * jnp.exp(gc) - xx[rows, LANES:]).astype(MXU_DTYPE)
                nn_w[d, pl.ds(cb128[j], HEAD_DIM), :] = mn[:, :LANES]
                mp_w[d, pl.ds(cb128[j], HEAD_DIM), :] = mn[:, LANES:].astype(MXU_DTYPE)
                dd_w[d, pl.ds(cb8[j], SUBLANES), :] = jnp.broadcast_to(jnp.exp(g_last), (SUBLANES, LANES))
        carry = end_of_stage(carry)
        assert stage[0] == n_stages
        return carry

    def scan_step(i, carry):
        new = []
        for d in range(2):
            s = carry[d]
            c = i if d == 0 else nc - 1 - i
            base = _aligned(c * CHUNK, CHUNK)
            cbase = _aligned(c * HEAD_DIM, HEAD_DIM)
            dbase = _aligned(c * SUBLANES, SUBLANES)
            s16 = s.astype(MXU_DTYPE)
            osum_r[pl.ds(base, CHUNK), :] += _dot(qe_r[d, pl.ds(base, CHUNK), :], s16)
            dec = dd_r[d, pl.ds(dbase, 1), :]
            new.append(dec * s - _dot(mp_r[d, pl.ds(cbase, HEAD_DIM), :], s16)
                       + nn_r[d, pl.ds(cbase, HEAD_DIM), :])
        return tuple(new)

    zero_state = jnp.zeros((HEAD_DIM, HEAD_DIM), F32)

    def write_output():
        gain = ong_ref[...]

        def finish(o, z):
            ms = jnp.mean(o * o, axis=-1, keepdims=True)
            return (o * lax.rsqrt(ms + NORM_EPS) * gain * _silu(z.astype(F32))).astype(o_ref.dtype)

        first = CHUNK - padr if padr else 0
        if first:
            o_ref[0:first, :] = finish(osum_r[padr:CHUNK, :], z_ref[0:first, :])

        def out_body(c, carry):
            base = pl.multiple_of(c * CHUNK, CHUNK)
            dst = pl.multiple_of(c * CHUNK - padr, SUBLANES)
            o_ref[pl.ds(dst, CHUNK), :] = finish(osum_r[pl.ds(base, CHUNK), :], z_ref[pl.ds(dst, CHUNK), :])
            return carry

        lax.fori_loop(1 if first else 0, nc, out_body, 0, unroll=8)

    def conv_only(c, carry):
        conv_chunk(c)
        return carry

    conv_per_scan = 3
    n_conv_scan = (nc - 1) // conv_per_scan

    def conv_and_scan(i, carry):
        for u in range(conv_per_scan):
            conv_chunk(1 + conv_per_scan * i + u)
        return scan_step(i, carry)

    def gates():
        @pl.when(head == 0)
        def _():
            stage_gate_inputs()
            for_groups(gate_batch_group)

        for_groups(gate_head_group)

    @pl.when(step == 0)
    def _():
        stage_conv_inputs()
        conv_chunk(0)
        lax.fori_loop(1, nc, conv_only, 0, unroll=4)
        gates()
        for_groups(chunk_group)

    @pl.when((step > 0) & (step < n_heads_total))
    def _():
        stage_conv_inputs()
        conv_chunk(0)
        carry = lax.fori_loop(0, n_conv_scan, conv_and_scan, (zero_state, zero_state))
        lax.fori_loop(1 + conv_per_scan * n_conv_scan, nc, conv_only, 0)
        gates()
        for_groups(chunk_group, carry, n_conv_scan, nc - n_conv_scan)
        write_output()

    @pl.when(step == n_heads_total)
    def _():
        lax.fori_loop(0, nc, scan_step, (zero_state, zero_state))
        write_output()


def _gdn(proj3, ab3, conv_w, gate_prm, out_norm_g):
    b, seq, _ = proj3.shape
    lp = -(-seq // CHUNK) * CHUNK
    nc = lp // CHUNK
    n_total = b * N_HEADS
    cur = lambda i: jnp.minimum(i, n_total - 1)
    prev = lambda i: jnp.maximum(i - 1, 0)
    col = lambda off: pl.BlockSpec((None, seq, LANES), lambda i: (cur(i) // N_HEADS, 0, off + cur(i) % N_HEADS))
    cw = lambda off: pl.BlockSpec((CONV_K, LANES), lambda i: (0, off + cur(i) % N_HEADS))
    return pl.pallas_call(
        functools.partial(_gdn_kernel, seq=seq, lp=lp, group=GDN_GROUP, n_heads_total=n_total),
        grid=(n_total + 1,),
        in_specs=[
            col(COL_QKV // LANES), col(COL_QKV // LANES + N_HEADS), col(COL_QKV // LANES + 2 * N_HEADS),
            pl.BlockSpec((None, seq, LANES),
                         lambda i: (prev(i) // N_HEADS, 0, COL_Z // LANES + prev(i) % N_HEADS)),
            pl.BlockSpec((None, seq, LANES), lambda i: (cur(i) // N_HEADS, 0, 0)),
            cw(0), cw(N_HEADS), cw(2 * N_HEADS),
            pl.BlockSpec((SUBLANES, LANES), lambda i: (0, 0)),
            pl.BlockSpec((1, LANES), lambda i: (0, 0)),
        ],
        out_specs=pl.BlockSpec((None, seq, LANES), lambda i: (prev(i) // N_HEADS, 0, prev(i) % N_HEADS)),
        out_shape=jax.ShapeDtypeStruct((b, seq, N_HEADS * HEAD_DIM), MXU_DTYPE),
        scratch_shapes=[
            pltpu.VMEM((lp + 2 * SUBLANES, LANES), F32),
            pltpu.VMEM((lp + 2 * SUBLANES, LANES), F32),
            pltpu.VMEM((lp + 2 * SUBLANES, LANES), F32),
            pltpu.VMEM((lp, LANES), F32),
            pltpu.VMEM((lp, LANES), F32),
            pltpu.VMEM((lp, LANES), F32),
            pltpu.VMEM((lp, 4 * LANES), F32),
            pltpu.VMEM((nc * LANES, LANES), F32),
            pltpu.VMEM((lp, LANES), F32),
            pltpu.VMEM((2, 2, lp, LANES), MXU_DTYPE),
            pltpu.VMEM((2, 2, nc * HEAD_DIM, LANES), MXU_DTYPE),
            pltpu.VMEM((2, 2, nc * HEAD_DIM, LANES), F32),
            pltpu.VMEM((2, 2, nc * SUBLANES, LANES), F32),
            pltpu.VMEM((2, lp, LANES), F32),
        ],
        compiler_params=_params(("arbitrary",)),
        name="gdn",
    )(proj3, proj3, proj3, proj3, ab3, conv_w, conv_w, conv_w, gate_prm, out_norm_g)


def _merge_kernel(h_ref, og_ref, ym_ref, gf_ref, gd_ref, wf_ref, wd_ref, wo_ref, n2_ref, wr_ref, br_ref,
                  h1_ref, hn_ref, ridx_ref, rw_ref, cnt_ref):
    @pl.when(pl.program_id(0) == 0)
    def _():
        cnt_ref[...] = jnp.zeros_like(cnt_ref)

    y_f = _dot(ym_ref[...], wf_ref[...])
    y_d = _dot(og_ref[...], wd_ref[...])
    merged = (jax.nn.sigmoid(gf_ref[...].astype(F32)) * y_f
              + jax.nn.sigmoid(gd_ref[...].astype(F32)) * y_d)
    h1 = h_ref[...] + _dot(merged.astype(MXU_DTYPE), wo_ref[...])
    h1_ref[...] = h1
    ms = jnp.mean(h1 * h1, axis=-1, keepdims=True)
    hn = h1 * lax.rsqrt(ms + NORM_EPS) * n2_ref[...]
    hn_ref[...] = _pack_rows(hn)

    x_hi = hn.astype(MXU_DTYPE)
    x_lo = (hn - x_hi.astype(F32)).astype(MXU_DTYPE)
    w = wr_ref[...]
    w_hi = w.astype(MXU_DTYPE)
    w_lo = (w - w_hi.astype(F32)).astype(MXU_DTYPE)
    hi_terms = _dot(x_hi, jnp.concatenate([w_hi, w_lo], axis=1))
    logits = hi_terms[:, :LANES] + hi_terms[:, LANES:] + _dot(x_lo, w_hi) + br_ref[...]

    tm = logits.shape[0]
    lane = lax.broadcasted_iota(jnp.int32, (tm, LANES), 1)
    neg_inf = -jnp.inf
    is_group = lane < N_GROUPS
    gl = jnp.where(is_group, logits, neg_inf)
    ge = jnp.exp(gl - jnp.max(gl, axis=-1, keepdims=True))
    gp = ge / jnp.sum(ge, axis=-1, keepdims=True)
    p_group = jnp.max(gp, axis=-1, keepdims=True)
    g_idx = jnp.min(jnp.where(is_group & (gp == p_group), lane, LANES), axis=-1, keepdims=True)

    in_group = (lane >= N_GROUPS) & (lane < N_GROUPS + N_EXPERTS) & (((lane - N_GROUPS) >> 3) == g_idx)
    el = jnp.where(in_group, logits, neg_inf)
    ee = jnp.exp(el - jnp.max(el, axis=-1, keepdims=True))
    ep = ee / jnp.sum(ee, axis=-1, keepdims=True)
    v1 = jnp.max(jnp.where(in_group, ep, -1.0), axis=-1, keepdims=True)
    i1 = jnp.min(jnp.where(in_group & (ep == v1), lane, LANES), axis=-1, keepdims=True)
    rest = in_group & (lane != i1)
    v2 = jnp.max(jnp.where(rest, ep, -1.0), axis=-1, keepdims=True)
    i2 = jnp.min(jnp.where(rest & (ep == v2), lane, LANES), axis=-1, keepdims=True)
    denom = v1 + v2
    w1 = p_group * (v1 / denom)
    w2 = p_group * (v2 / denom)
    e1 = i1 - N_GROUPS
    e2 = i2 - N_GROUPS
    ridx_ref[...] = jnp.where(lane == 0, e1, jnp.where(lane == 1, e2, 0))
    rw_ref[...] = jnp.where(lane == 0, w1, jnp.where(lane == 1, w2, 0.0))
    onehots = jnp.where((lane == e1) | (lane == e2), 1.0, 0.0)
    cnt_ref[...] += jnp.broadcast_to(jnp.sum(onehots, axis=0, keepdims=True), cnt_ref.shape)


def _merge(h2d, og2d, ymix2d, proj2d, w_fourier, w_delta, w_out, norm2_g, w_router, b_router, tm):
    t, d = h2d.shape
    row = lambda w: pl.BlockSpec((tm, w), lambda i: (i, 0))
    const = lambda shape: pl.BlockSpec(shape, lambda i: (0, 0))
    return pl.pallas_call(
        _merge_kernel,
        grid=(t // tm,),
        in_specs=[
            row(d), row(d), row(F_WIDTH),
            pl.BlockSpec((tm, d), lambda i: (i, COL_GATE_F // d)),
            pl.BlockSpec((tm, d), lambda i: (i, COL_GATE_D // d)),
            const((F_WIDTH, d)), const((d, d)), const((d, d)), const((1, d)),
            const((d, LANES)), const((1, LANES)),
        ],
        out_specs=[row(d), row(d // 2), row(LANES), row(LANES), const((SUBLANES, LANES))],
        out_shape=[
            jax.ShapeDtypeStruct((t, d), F32),
            jax.ShapeDtypeStruct((t, d // 2), jnp.uint32),
            jax.ShapeDtypeStruct((t, LANES), jnp.int32),
            jax.ShapeDtypeStruct((t, LANES), F32),
            jax.ShapeDtypeStruct((SUBLANES, LANES), F32),
        ],
        compiler_params=_params(("arbitrary",)),
        name="merge",
    )(h2d, og2d, ymix2d, proj2d, proj2d, w_fourier, w_delta, w_out, norm2_g, w_router, b_router)


def _plan_kernel(ridx_ref, cnt_ref, dest_ref, seg_ref, base_ref, run_ref):
    tt = ridx_ref.shape[0]

    @pl.when(pl.program_id(0) == 0)
    def _():
        cnt = cnt_ref[...].astype(jnp.int32)
        n_blk = (cnt + (EXPERT_BLOCK - 1)) // EXPERT_BLOCK
        padded = (n_blk * EXPERT_BLOCK).astype(F32)
        r = lax.broadcasted_iota(jnp.int32, (LANES, LANES), 0)
        c = lax.broadcasted_iota(jnp.int32, (LANES, LANES), 1)
        before = jnp.where(r < c, 1.0, 0.0).astype(MXU_DTYPE)
        start = _exact_right(padded, before)
        base_ref[...] = start
        run_ref[...] = jnp.zeros_like(run_ref)
        seg_ref[0:SUBLANES, :] = start.astype(jnp.int32)
        seg_ref[SUBLANES:2 * SUBLANES, :] = n_blk
        seg_ref[2 * SUBLANES:, :] = cnt

    lane = lax.broadcasted_iota(jnp.int32, (tt, LANES), 1)
    ridx = ridx_ref[...]
    oh0 = jnp.where(lane == ridx[:, 0:1], 1.0, 0.0)
    oh1 = jnp.where(lane == ridx[:, 1:2], 1.0, 0.0)
    r = lax.broadcasted_iota(jnp.int32, (tt, tt), 0)
    c = lax.broadcasted_iota(jnp.int32, (tt, tt), 1)
    earlier = jnp.where(c < r, 1.0, 0.0).astype(MXU_DTYPE)
    prefix = _dot(earlier, jnp.concatenate([oh0, oh1], axis=1).astype(MXU_DTYPE))
    c0 = jnp.sum(oh0, axis=0, keepdims=True)
    c1 = jnp.sum(oh1, axis=0, keepdims=True)
    first = base_ref[0:1, :] + run_ref[0:1, :]
    d0 = jnp.sum(oh0 * (prefix[:, :LANES] + first), axis=-1, keepdims=True)
    d1 = jnp.sum(oh1 * (prefix[:, LANES:] + first + c0), axis=-1, keepdims=True)
    run_ref[...] += jnp.broadcast_to(c0 + c1, run_ref.shape)
    both = jnp.where(lane == 0, d0, jnp.where(lane == 1, d1, 0.0))
    both_t = both.T.astype(jnp.int32)
    for k in range(TOP_K):
        dest_ref[k] = both_t[k:k + 1, :]


def _plan(ridx, cnt, tt):
    t = ridx.shape[0]
    return pl.pallas_call(
        _plan_kernel,
        grid=(t // tt,),
        in_specs=[pl.BlockSpec((tt, LANES), lambda i: (i, 0)),
                  pl.BlockSpec((SUBLANES, LANES), lambda i: (0, 0))],
        out_specs=[pl.BlockSpec((TOP_K, None, 1, tt), lambda i: (0, i, 0, 0)),
                   pl.BlockSpec((3 * SUBLANES, LANES), lambda i: (0, 0))],
        out_shape=[jax.ShapeDtypeStruct((TOP_K, t // tt, 1, tt), jnp.int32),
                   jax.ShapeDtypeStruct((3 * SUBLANES, LANES), jnp.int32)],
        scratch_shapes=[pltpu.VMEM((SUBLANES, LANES), F32), pltpu.VMEM((SUBLANES, LANES), F32)],
        compiler_params=_params(("arbitrary",)),
        name="plan",
    )(ridx, cnt)


def _scatter_kernel(dest0_ref, dest1_ref, hn_ref, xb_in_ref, xb_ref, sem, *, ts):
    dest_refs = (dest0_ref, dest1_ref)
    del xb_in_ref

    def issue(t, carry):
        src = hn_ref.at[pl.ds(t, 1)]
        for k in range(TOP_K):
            pltpu.make_async_copy(src, xb_ref.at[pl.ds(dest_refs[k][0, t], 1)], sem).start(priority=k % 2)
        return carry

    lax.fori_loop(0, ts, issue, 0, unroll=DMA_UNROLL)
    for _ in range(TOP_K):
        pltpu.make_async_copy(hn_ref, xb_ref.at[pl.ds(0, ts)], sem).wait()


def _scatter(dest_flat, hn2d, xb_zero, ts):
    t, d = hn2d.shape
    return pl.pallas_call(
        functools.partial(_scatter_kernel, ts=ts),
        grid=(t // ts,),
        in_specs=[pl.BlockSpec((None, None, 1, ts), lambda i: (0, i, 0, 0), memory_space=pltpu.SMEM),
                  pl.BlockSpec((None, None, 1, ts), lambda i: (1, i, 0, 0), memory_space=pltpu.SMEM),
                  pl.BlockSpec((ts, d), lambda i: (i, 0)),
                  pl.BlockSpec(memory_space=pl.ANY)],
        out_specs=pl.BlockSpec(memory_space=pl.ANY),
        out_shape=jax.ShapeDtypeStruct(xb_zero.shape, xb_zero.dtype),
        scratch_shapes=[pltpu.SemaphoreType.DMA(())],
        input_output_aliases={3: 0},
        compiler_params=_params(("arbitrary",)),
        name="scatter",
    )(dest_flat, dest_flat, hn2d, xb_zero)


def _expert_kernel(start_ref, nblk_ref, xb_ref, wg_ref, wu_ref, wd_ref, yb_ref,
                   wg16_ref, wu16_ref, wd16_ref, xbuf_ref, ybuf_ref, sem_in, sem_out):
    expert = pl.program_id(0)
    last = pl.num_programs(0) - 1
    n = nblk_ref[expert]
    first = start_ref[expert] // EXPERT_BLOCK
    total = start_ref[last] // EXPERT_BLOCK + nblk_ref[last]

    def rows(g):
        return pl.ds(pl.multiple_of(g * EXPERT_BLOCK, EXPERT_BLOCK), EXPERT_BLOCK)

    def x_copy(g):
        slot = g % EXPERT_X_SLOTS
        return pltpu.make_async_copy(xb_ref.at[rows(g)], xbuf_ref.at[slot], sem_in.at[slot])

    def y_copy(g):
        slot = g % EXPERT_Y_SLOTS
        return pltpu.make_async_copy(ybuf_ref.at[slot], yb_ref.at[rows(g)], sem_out.at[slot])

    ahead = EXPERT_X_SLOTS - EXPERT_PASS

    def process(g, k):
        for j in range(k):
            @pl.when(g + j + ahead < total)
            def _():
                x_copy(g + j + ahead).start()

        for j in range(k):
            x_copy(g + j).wait()

        for j in range(k):
            @pl.when(g + j >= EXPERT_Y_SLOTS)
            def _():
                y_copy(g + j - EXPERT_Y_SLOTS).wait()

        x = jnp.concatenate([_unpack_rows(xbuf_ref[(g + j) % EXPERT_X_SLOTS]) for j in range(k)],
                            axis=0).astype(MXU_DTYPE)
        gate = _dot(x, wg16_ref[...])
        up = _dot(x, wu16_ref[...])
        y = _pack_rows(_dot((_silu(gate) * up).astype(MXU_DTYPE), wd16_ref[...]))
        for j in range(k):
            ybuf_ref[(g + j) % EXPERT_Y_SLOTS] = y[j * EXPERT_BLOCK:(j + 1) * EXPERT_BLOCK]
            y_copy(g + j).start()

    @pl.when(n > 0)
    def _():
        @pl.when(first == 0)
        def _():
            for j in range(ahead):
                @pl.when(j < total)
                def _():
                    x_copy(j).start()

        wg16_ref[...] = wg_ref[...].astype(MXU_DTYPE)
        wu16_ref[...] = wu_ref[...].astype(MXU_DTYPE)
        wd16_ref[...] = wd_ref[...].astype(MXU_DTYPE)

        def full_pass(i, carry):
            process(first + EXPERT_PASS * i, EXPERT_PASS)
            return carry

        lax.fori_loop(0, n // EXPERT_PASS, full_pass, 0)
        rest = n % EXPERT_PASS
        for k in range(1, EXPERT_PASS):
            @pl.when(rest == k)
            def _():
                process(first + n - k, k)

    @pl.when(expert == last)
    def _():
        for j in range(EXPERT_Y_SLOTS):
            @pl.when(total > j)
            def _():
                y_copy(total - 1 - j).wait()


def _experts(seg_start, seg_blocks, xb, w_gate, w_up, w_down):
    n_slots, row_words = xb.shape
    n_exp, d, de = w_gate.shape
    grid_spec = pltpu.PrefetchScalarGridSpec(
        num_scalar_prefetch=2,
        grid=(n_exp,),
        in_specs=[
            pl.BlockSpec(memory_space=pl.ANY),
            pl.BlockSpec((None, d, de), lambda e, st, nb: (e, 0, 0)),
            pl.BlockSpec((None, d, de), lambda e, st, nb: (e, 0, 0)),
            pl.BlockSpec((None, de, d), lambda e, st, nb: (e, 0, 0)),
        ],
        out_specs=pl.BlockSpec(memory_space=pl.ANY),
        scratch_shapes=[
            pltpu.VMEM((d, de), MXU_DTYPE), pltpu.VMEM((d, de), MXU_DTYPE), pltpu.VMEM((de, d), MXU_DTYPE),
            pltpu.VMEM((EXPERT_X_SLOTS, EXPERT_BLOCK, row_words), jnp.uint32),
            pltpu.VMEM((EXPERT_Y_SLOTS, EXPERT_BLOCK, row_words), jnp.uint32),
            pltpu.SemaphoreType.DMA((EXPERT_X_SLOTS,)), pltpu.SemaphoreType.DMA((EXPERT_Y_SLOTS,)),
        ],
    )
    return pl.pallas_call(
        _expert_kernel,
        grid_spec=grid_spec,
        out_shape=jax.ShapeDtypeStruct((n_slots, row_words), jnp.uint32),
        input_output_aliases={2: 0},
        compiler_params=_params(("arbitrary",)),
        name="experts",
    )(seg_start, seg_blocks, xb, w_gate, w_up, w_down)


def _combine_kernel(dest0_ref, dest1_ref, yb_ref, h1_ref, rw_ref, fg_ref, o_ref, buf_ref, res_ref, sem, out_sem,
                    *, tc, groups_per_batch):
    step = pl.program_id(0)
    n_tiles = pl.num_programs(0) - 1
    n_grp = tc // N_META
    cur = step % 2
    prev = 1 - cur

    def output_copies(tile, action):
        slot = tile % 2
        first = tile * n_grp
        first_in_batch = first % groups_per_batch
        has_meta = (first_in_batch == 0) | (first_in_batch + n_grp > groups_per_batch)

        def copy_groups(q, n):
            grp = first + q
            dst = pl.multiple_of((grp - grp // groups_per_batch - 1) * N_META, N_META)
            copy = pltpu.make_async_copy(res_ref.at[slot, pl.ds(q * N_META, n * N_META)],
                                         o_ref.at[pl.ds(dst, n * N_META)], out_sem.at[slot])
            copy.start() if action == "start" else copy.wait()

        @pl.when(jnp.logical_not(has_meta))
        def _():
            copy_groups(0, n_grp)

        @pl.when(has_meta)
        def _():
            for q in range(n_grp):
                @pl.when((first + q) % groups_per_batch != 0)
                def _():
                    copy_groups(q, 1)

    def request_rows(i):
        for u in range(SUBLANES):
            t = i * SUBLANES + u
            for k in range(TOP_K):
                pltpu.make_async_copy(yb_ref.at[pl.ds((dest0_ref, dest1_ref)[k][0, t], 1)],
                                      buf_ref.at[cur, k, pl.ds(t, 1)], sem.at[cur]).start(priority=k % 2)

    def finish_rows(i):
        rows = pl.ds(pl.multiple_of(i * SUBLANES, SUBLANES), SUBLANES)
        rw = rw_ref[rows, :]
        h2 = h1_ref[rows, :] + (_unpack_rows(buf_ref[prev, 0, rows, :]) * rw[:, 0:1]
                                + _unpack_rows(buf_ref[prev, 1, rows, :]) * rw[:, 1:2])
        ms = jnp.mean(h2 * h2, axis=-1, keepdims=True)
        res_ref[prev, rows, :] = h2 * lax.rsqrt(ms + NORM_EPS) * fg_ref[...]

    def begin_finish():
        for k in range(TOP_K):
            pltpu.make_async_copy(yb_ref.at[pl.ds(0, tc)], buf_ref.at[prev, k], sem.at[prev]).wait()

        @pl.when(step >= 3)
        def _():
            output_copies(step - 3, "wait")

    def both(i, carry):
        request_rows(i)
        finish_rows(i)
        return carry

    def request_only(i, carry):
        request_rows(i)
        return carry

    def finish_only(i, carry):
        finish_rows(i)
        return carry

    @pl.when(step == 0)
    def _():
        lax.fori_loop(0, tc // SUBLANES, request_only, 0)

    @pl.when((step > 0) & (step < n_tiles))
    def _():
        begin_finish()
        lax.fori_loop(0, tc // SUBLANES, both, 0, unroll=16)
        output_copies(step - 1, "start")

    @pl.when(step == n_tiles)
    def _():
        begin_finish()
        lax.fori_loop(0, tc // SUBLANES, finish_only, 0, unroll=4)
        output_copies(step - 1, "start")

        @pl.when(step >= 2)
        def _():
            output_copies(step - 2, "wait")

        output_copies(step - 1, "wait")


def _combine(dest_flat, yb, h1, rw, final_g, tc, seq):
    t, d = h1.shape
    assert tc % N_META == 0 and seq % N_META == 0
    n_out = t - (t // seq) * N_META
    n_tiles = t // tc
    nxt = lambda i: jnp.minimum(i, n_tiles - 1)
    fin = lambda i: jnp.maximum(i - 1, 0)
    return pl.pallas_call(
        functools.partial(_combine_kernel, tc=tc, groups_per_batch=seq // N_META),
        grid=(n_tiles + 1,),
        in_specs=[pl.BlockSpec((None, None, 1, tc), lambda i: (0, nxt(i), 0, 0), memory_space=pltpu.SMEM),
                  pl.BlockSpec((None, None, 1, tc), lambda i: (1, nxt(i), 0, 0), memory_space=pltpu.SMEM),
                  pl.BlockSpec(memory_space=pl.ANY),
                  pl.BlockSpec((tc, d), lambda i: (fin(i), 0)),
                  pl.BlockSpec((tc, LANES), lambda i: (fin(i), 0)),
                  pl.BlockSpec((1, d), lambda i: (0, 0))],
        out_specs=pl.BlockSpec(memory_space=pl.ANY),
        out_shape=jax.ShapeDtypeStruct((n_out, d), F32),
        scratch_shapes=[pltpu.VMEM((2, TOP_K, tc, d // 2), jnp.uint32), pltpu.VMEM((2, tc, d), F32),
                        pltpu.SemaphoreType.DMA((2,)), pltpu.SemaphoreType.DMA((2,))],
        compiler_params=_params(("arbitrary",)),
        name="combine",
    )(dest_flat, dest_flat, yb, h1, rw, final_g)


def _tile(total, candidates):
    for c in candidates:
        if total % c == 0:
            return c
    raise ValueError(f"no tile for {total}")


def _dft_tables(seq):
    kp = -(-seq // LANES) * LANES
    nat = -(-(seq // 2 + 1) // 16) * 16
    mir = seq - nat
    c = np.arange(FGROUP_DIM, dtype=np.int64)
    ang_c = (2.0 * np.pi / FGROUP_DIM) * ((c[:, None] * c[None, :]) % FGROUP_DIM)
    cs128 = np.concatenate([np.cos(ang_c), np.sin(ang_c)], axis=1)
    k = np.arange(nat, dtype=np.int64)
    n = np.arange(seq, dtype=np.int64)
    ang_l = (2.0 * np.pi / seq) * ((k[:, None] * n[None, :]) % seq)
    twid = np.zeros((nat, 2 * kp), np.float32)
    twid[:, :seq] = np.cos(ang_l)
    twid[:, kp:kp + seq] = np.sin(ang_l)
    flip = np.zeros((mir, nat), np.float32)
    flip[np.arange(mir), mir - np.arange(mir)] = 1.0
    as_const = lambda t: jnp.asarray(t.astype(MXU_DTYPE))
    return as_const(cs128), as_const(twid), as_const(flip)


def kernel(x, meta_tokens, norm1_g, w_in, conv_w, a_log_fwd, dt_bias_fwd, a_log_bwd, dt_bias_bwd, out_norm_g, w_fourier, w_delta, w_out, norm2_g, w_router_group, b_router_group, w_router_expert, b_router_expert, w_gate_e, w_up_e, w_down_e, final_norm_g):
    bsz, n_real, d = x.shape
    seq = N_META + n_real
    n_tok = bsz * seq
    assert (CHUNK - N_META) % CHUNK + seq == -(-seq // CHUNK) * CHUNK, "meta tokens must fill the tail of chunk 0"
    assert w_in.shape[0] == 1, "single trunk layer"
    layer = 0


    w = w_in[layer]
    o_f, o_qkv, o_z, o_ab, o_gf, o_gd = 0, F_WIDTH, F_WIDTH + 3 * QK_WIDTH, F_WIDTH + 4 * QK_WIDTH, \
        F_WIDTH + 4 * QK_WIDTH + 4 * N_HEADS, F_WIDTH + 4 * QK_WIDTH + 4 * N_HEADS + d
    w_cat = jnp.concatenate(
        [w[:, o_qkv:o_z], w[:, o_z:o_ab], w[:, o_gf:o_gd], w[:, o_gd:o_gd + d], w[:, o_f:o_qkv]],
        axis=1).astype(MXU_DTYPE)
    w_ab = jnp.pad(w[:, o_ab:o_gf], ((0, 0), (0, LANES - 4 * N_HEADS))).astype(MXU_DTYPE)

    proj, ab, h = _in_proj(x.reshape(bsz * n_real, d), meta_tokens.astype(x.dtype), norm1_g[layer][None, :],
                           w_cat, w_ab, seq)
    proj3 = proj.reshape(bsz, seq, PROJ_WIDTH)

    ymix = _fnet(proj3, *_dft_tables(seq))

    gate_prm = jnp.zeros((SUBLANES, LANES), F32)
    gate_prm = gate_prm.at[0, 0:N_HEADS].set(a_log_fwd[layer]).at[0, 2 * N_HEADS:3 * N_HEADS].set(a_log_bwd[layer])
    gate_prm = gate_prm.at[1, 0:N_HEADS].set(dt_bias_fwd[layer]).at[1, 2 * N_HEADS:3 * N_HEADS].set(dt_bias_bwd[layer])
    og = _gdn(proj3, ab.reshape(bsz, seq, LANES), conv_w[layer], gate_prm, out_norm_g[layer][None, :])

    w_router = jnp.zeros((d, LANES), F32)
    w_router = w_router.at[:, 0:N_GROUPS].set(w_router_group[layer])
    w_router = w_router.at[:, N_GROUPS:N_GROUPS + N_EXPERTS].set(w_router_expert[layer])
    b_router = jnp.zeros((1, LANES), F32)
    b_router = b_router.at[0, 0:N_GROUPS].set(b_router_group[layer])
    b_router = b_router.at[0, N_GROUPS:N_GROUPS + N_EXPERTS].set(b_router_expert[layer])
    tt = _tile(n_tok, (384, 128))
    h1, hn, ridx, rw, cnt = _merge(
        h, og.reshape(n_tok, d), ymix.reshape(n_tok, F_WIDTH), proj,
        w_fourier[layer].astype(MXU_DTYPE), w_delta[layer].astype(MXU_DTYPE), w_out[layer].astype(MXU_DTYPE),
        norm2_g[layer][None, :], w_router, b_router, _tile(n_tok, (688, 384, 128)))

    n_assign = n_tok * TOP_K
    n_blocks = -(-n_assign // EXPERT_BLOCK) + N_EXPERTS
    dest, seg = _plan(ridx, cnt, tt)
    seg_start = seg[0, :N_EXPERTS]
    seg_blocks = seg[SUBLANES, :N_EXPERTS]

    xb = _scatter(dest, hn, jnp.zeros((n_blocks * EXPERT_BLOCK, d // 2), jnp.uint32), tt)
    yb = _experts(seg_start, seg_blocks, xb, w_gate_e[layer], w_up_e[layer], w_down_e[layer])
    out = _combine(dest, yb, h1, rw, final_norm_g[None, :], tt, seq)
    return out.reshape(bsz, n_real, d)
```
